```python
import jax, jax.numpy as jnp
from jax import lax
import numpy as np

D_MODEL = 2048
BATCH = 2
SEQ = 4096
DEPTH = 2

MIX_WIDTH = D_MODEL
RWKV_WIDTH = MIX_WIDTH // 2
RWKV_HEAD = 64
RWKV_HEADS = RWKV_WIDTH // RWKV_HEAD
DECAY_LORA = 64
ICLR_LORA = 64
VRES_LORA = 32
GATE_LORA = 128
ATTN_WIDTH = MIX_WIDTH - RWKV_WIDTH
ATTN_HEAD = 128
ATTN_HEADS = ATTN_WIDTH // ATTN_HEAD
MOBA_BLOCK = 256
MOBA_TOPK = 3
QUERY_CHUNK = 16
ROPE_THETA = 10000.0
D_FF = 4 * D_MODEL
NORM_EPS = 1e-6
LNX_EPS = 64e-5
N_SHIFT = 3 * RWKV_WIDTH + DECAY_LORA + ICLR_LORA + GATE_LORA
N_IN = N_SHIFT + 3 * ATTN_WIDTH

kernel_name = 'hybrid_rwkv7_moba_sandwich'


def rms_norm(x, gain):
    xf = x.astype(jnp.float32)
    y = xf * lax.rsqrt(jnp.mean(xf * xf, axis=-1, keepdims=True) + NORM_EPS)
    return (y * gain.astype(jnp.float32)).astype(x.dtype)


def token_shift(z, mu):
    prev = jnp.pad(z[:, :-1], ((0, 0), (1, 0), (0, 0)))
    return z + (prev - z) * mu


def wkv7_scan(r, w, k, v, a, b):
    bsz, _, nh, n = r.shape

    def step(state, inp):
        r_t, w_t, k_t, v_t, a_t, b_t = inp
        sa = jnp.einsum('bhij,bhj->bhi', state, a_t)
        state = (state * w_t[:, :, None, :] + sa[..., None] * b_t[:, :, None, :]
                 + v_t[..., None] * k_t[:, :, None, :])
        return state, jnp.einsum('bhij,bhj->bhi', state, r_t)

    xs = tuple(jnp.moveaxis(t, 1, 0) for t in (r, w, k, v, a, b))
    state0 = jnp.zeros((bsz, nh, n, n), jnp.float32)
    _, y = lax.scan(step, state0, xs)
    return jnp.moveaxis(y, 0, 1)


def rwkv7_mix(zs, v_first, vres, w0, w2, a0, a2, g2, k_k, k_a, r_k, lnx_g, lnx_b):
    bsz, s, _ = zs.shape
    c = RWKV_WIDTH
    f32 = jnp.float32
    r, k, v = zs[..., :c], zs[..., c:2 * c], zs[..., 2 * c:3 * c]
    o = 3 * c
    wd = zs[..., o:o + DECAY_LORA]
    o += DECAY_LORA
    ad = zs[..., o:o + ICLR_LORA]
    o += ICLR_LORA
    gd = zs[..., o:o + GATE_LORA]
    w_log = -jax.nn.softplus(-(w0 + jnp.tanh(wd) @ w2)) - 0.5
    decay = jnp.exp(-jnp.exp(w_log.astype(f32)))
    a = jax.nn.sigmoid(a0 + ad @ a2)
    g = jax.nn.sigmoid(gd) @ g2
    if vres is not None:
        vd, v0, v2 = vres
        v = v + (v_first - v) * jax.nn.sigmoid(v0 + vd @ v2)
    heads = lambda t: t.reshape(bsz, s, RWKV_HEADS, RWKV_HEAD).astype(f32)
    kk = heads(k * k_k)
    kk = kk / jnp.maximum(jnp.sqrt(jnp.sum(kk * kk, axis=-1, keepdims=True)), 1e-12)
    k = k * (1 + (a - 1) * k_a)
    rh, kh, vh, ah = heads(r), heads(k), heads(v), heads(a)
    y = wkv7_scan(rh, heads(decay), kh, vh, -kk, kk * ah)
    mu = jnp.mean(y, axis=-1, keepdims=True)
    var = jnp.mean(jnp.square(y - mu), axis=-1, keepdims=True)
    yn = ((y - mu) * lax.rsqrt(var + LNX_EPS)).reshape(bsz, s, c)
    yn = yn * lnx_g.astype(f32) + lnx_b.astype(f32)
    bonus = jnp.sum(rh * kh * r_k.astype(f32), axis=-1, keepdims=True) * vh
    out = (yn + bonus.reshape(bsz, s, c)) * g.astype(f32)
    return out.astype(zs.dtype), v


def rope(x, positions):
    half = x.shape[-1] // 2
    inv_freq = ROPE_THETA ** (-jnp.arange(half, dtype=jnp.float32) / half)
    ang = positions.astype(jnp.float32)[:, None] * inv_freq[None, :]
    cos, sin = jnp.cos(ang), jnp.sin(ang)
    xf = x.astype(jnp.float32)
    x1, x2 = xf[..., :half], xf[..., half:]
    return jnp.concatenate([x1 * cos - x2 * sin, x2 * cos + x1 * sin], axis=-1).astype(x.dtype)


def moba_attention(q, k, v):
    bsz, nh, s, dh = q.shape
    f32 = jnp.float32
    nb = -(-s // MOBA_BLOCK)
    s_pad = nb * MOBA_BLOCK
    pad = ((0, 0), (0, 0), (0, s_pad - s), (0, 0))
    q, k, v = (jnp.pad(t, pad) for t in (q, k, v))
    k_blk = k.reshape(bsz, nh, nb, MOBA_BLOCK, dh).astype(f32)
    v_blk = v.reshape(bsz, nh, nb, MOBA_BLOCK, dh).astype(f32)
    k_mean = jnp.mean(k_blk, axis=3)
    gate = jnp.einsum('bhsd,bhnd->bhsn', q.astype(f32), k_mean)
    q_blk = jnp.arange(s_pad) // MOBA_BLOCK
    past = jnp.arange(nb)[None, :] < q_blk[:, None]
    gate = jnp.where(past, gate, -jnp.inf)
    topk = min(MOBA_TOPK, nb)
    sel_score, sel_idx = lax.top_k(gate, topk)
    sel_valid = jnp.isfinite(sel_score)

    n_chunks = s_pad // QUERY_CHUNK
    scale = dh ** -0.5

    def to_chunks(t):
        t = t.reshape(bsz, nh, n_chunks, QUERY_CHUNK, *t.shape[3:])
        return jnp.moveaxis(t, 2, 0)

    b_idx = jnp.arange(bsz)[:, None, None, None]
    h_idx = jnp.arange(nh)[None, :, None, None]
    q_offs = jnp.arange(QUERY_CHUNK)
    k_offs = jnp.arange(MOBA_BLOCK)

    def attend(inp):
        c, q_c, idx_c, valid_c = inp
        start = c * QUERY_CHUNK
        own = start // MOBA_BLOCK
        k_sel = k_blk[b_idx, h_idx, idx_c]
        v_sel = v_blk[b_idx, h_idx, idx_c]
        k_own = lax.dynamic_index_in_dim(k_blk, own, axis=2, keepdims=False)
        v_own = lax.dynamic_index_in_dim(v_blk, own, axis=2, keepdims=False)
        qf = q_c.astype(f32) * scale
        s_sel = jnp.einsum('bhqd,bhqkjd->bhqkj', qf, k_sel)
        s_sel = jnp.where(valid_c[..., None], s_sel, -jnp.inf)
        s_sel = s_sel.reshape(bsz, nh, QUERY_CHUNK, topk * MOBA_BLOCK)
        s_own = jnp.einsum('bhqd,bhjd->bhqj', qf, k_own)
        q_pos = start + q_offs
        k_pos = own * MOBA_BLOCK + k_offs
        s_own = jnp.where(k_pos[None, :] <= q_pos[:, None], s_own, -jnp.inf)
        p = jax.nn.softmax(jnp.concatenate([s_sel, s_own], axis=-1), axis=-1)
        p_sel = p[..., :topk * MOBA_BLOCK].reshape(bsz, nh, QUERY_CHUNK, topk, MOBA_BLOCK)
        p_own = p[..., topk * MOBA_BLOCK:]
        o = (jnp.einsum('bhqkj,bhqkjd->bhqd', p_sel, v_sel)
             + jnp.einsum('bhqj,bhjd->bhqd', p_own, v_own))
        return o.astype(q_c.dtype)

    out = lax.map(attend, (jnp.arange(n_chunks), to_chunks(q), to_chunks(sel_idx), to_chunks(sel_valid)))
    out = jnp.moveaxis(out, 0, 2).reshape(bsz, nh, s_pad, dh)
    return out[:, :, :s]


def setup_inputs(seed: int = 0) -> dict:
    key = jax.random.key(seed)
    ks = iter(jax.random.split(key, 32))
    nrm = lambda shape, scale: jax.random.normal(next(ks), shape, jnp.float32) * scale
    gain = lambda shape: 1.0 + nrm(shape, 0.05)
    unif = lambda shape: jax.random.uniform(next(ks), shape, jnp.float32)
    L, Lv, C = DEPTH, DEPTH - 1, RWKV_WIDTH
    return {
        'x': nrm((BATCH, SEQ, D_MODEL), 1.0),
        'norm_mix_pre': gain((L, D_MODEL)),
        'norm_mix_post': gain((L, D_MODEL)),
        'norm_mlp_pre': gain((L, D_MODEL)),
        'norm_mlp_post': gain((L, D_MODEL)),
        'w_in': nrm((L, D_MODEL, N_IN), D_MODEL ** -0.5),
        'w_in_vres': nrm((Lv, D_MODEL, VRES_LORA), D_MODEL ** -0.5),
        'shift_mu': unif((L, N_SHIFT)),
        'shift_mu_vres': unif((Lv, VRES_LORA)),
        'decay_w0': nrm((L, C), 0.5),
        'decay_w2': nrm((L, DECAY_LORA, C), 0.5 * DECAY_LORA ** -0.5),
        'iclr_a0': nrm((L, C), 0.5),
        'iclr_a2': nrm((L, ICLR_LORA, C), 0.5 * ICLR_LORA ** -0.5),
        'vres_v0': nrm((Lv, C), 0.5),
        'vres_v2': nrm((Lv, VRES_LORA, C), 0.5 * VRES_LORA ** -0.5),
        'gate_g2': nrm((L, GATE_LORA, C), GATE_LORA ** -0.5),
        'k_k': 0.85 + nrm((L, C), 0.05),
        'k_a': gain((L, C)),
        'r_k': nrm((L, RWKV_HEADS, RWKV_HEAD), 0.1),
        'lnx_gain': gain((L, C)),
        'lnx_bias': nrm((L, C), 0.01),
        'w_out': nrm((L, MIX_WIDTH, D_MODEL), MIX_WIDTH ** -0.5),
        'w_up': nrm((L, D_MODEL, D_FF), D_MODEL ** -0.5),
        'w_down': nrm((L, D_FF, D_MODEL), D_FF ** -0.5),
    }


def reference(x, norm_mix_pre, norm_mix_post, norm_mlp_pre, norm_mlp_post, w_in, w_in_vres,
              shift_mu, shift_mu_vres, decay_w0, decay_w2, iclr_a0, iclr_a2, vres_v0, vres_v2,
              gate_g2, k_k, k_a, r_k, lnx_gain, lnx_bias, w_out, w_up, w_down):
    bsz, s, _ = x.shape
    positions = jnp.arange(s)
    v_first = None
    for i in range(DEPTH):
        h = rms_norm(x, norm_mix_pre[i])
        w_comb = w_in[i] if i == 0 else jnp.concatenate([w_in[i], w_in_vres[i - 1]], axis=1)
        z = h @ w_comb
        z_rwkv = token_shift(z[..., :N_SHIFT], shift_mu[i])
        z_att = z[..., N_SHIFT:N_IN]
        vres = None if i == 0 else (token_shift(z[..., N_IN:], shift_mu_vres[i - 1]),
                                    vres_v0[i - 1], vres_v2[i - 1])
        y_r, v_r = rwkv7_mix(z_rwkv, v_first, vres, decay_w0[i], decay_w2[i], iclr_a0[i], iclr_a2[i],
                             gate_g2[i], k_k[i], k_a[i], r_k[i], lnx_gain[i], lnx_bias[i])
        if i == 0:
            v_first = v_r
        qkv = z_att.reshape(bsz, s, 3, ATTN_HEADS, ATTN_HEAD)
        q = rope(jnp.transpose(qkv[:, :, 0], (0, 2, 1, 3)), positions)
        k = rope(jnp.transpose(qkv[:, :, 1], (0, 2, 1, 3)), positions)
        v = jnp.transpose(qkv[:, :, 2], (0, 2, 1, 3))
        y_a = jnp.transpose(moba_attention(q, k, v), (0, 2, 1, 3)).reshape(bsz, s, ATTN_WIDTH)
        y = jnp.concatenate([y_r, y_a], axis=-1) @ w_out[i]
        x = x + rms_norm(y, norm_mix_post[i])
        h = rms_norm(x, norm_mlp_pre[i])
        m = jnp.square(jax.nn.relu(h @ w_up[i])) @ w_down[i]
        x = x + rms_norm(m, norm_mlp_post[i])
    return x
```

```python
import functools

import jax
import jax.numpy as jnp
from jax import lax
from jax.experimental import pallas as pl
from jax.experimental.pallas import tpu as pltpu

F32 = jnp.float32
BF16 = jnp.bfloat16

RWKV_HEAD = 64
DECAY_LORA = 64
ICLR_LORA = 64
VRES_LORA = 32
GATE_LORA = 128
ATTN_HEAD = 128
MOBA_BLOCK = 256
MOBA_TOPK = 3
ROPE_THETA = 10000.0
NORM_EPS = 1e-6
LNX_EPS = 64e-5

LORA_PAD = 512
WKV_CHUNK = 64
VMEM_LIMIT = 56 * 1024 * 1024


def _cparams(*sem):
    return pltpu.CompilerParams(dimension_semantics=sem, vmem_limit_bytes=VMEM_LIMIT)


_NN = (((1,), (0,)), ((), ()))
_NT = (((1,), (1,)), ((), ()))
_TN = (((0,), (0,)), ((), ()))


def _split2(x):
    hi = x.astype(BF16)
    lo = (x - hi.astype(F32)).astype(BF16)
    return hi, lo


def _split3(x):
    hi = x.astype(BF16)
    r1 = x - hi.astype(F32)
    mid = r1.astype(BF16)
    lo = (r1 - mid.astype(F32)).astype(BF16)
    return hi, mid, lo


def _mm(a, b, dims=_NN, passes=1):
    d = lambda p, q: lax.dot_general(p, q, dims, preferred_element_type=F32)
    if passes == 1:
        return d(a.astype(BF16), b.astype(BF16))
    ah, al = _split2(a)
    bh, bl = _split2(b)
    return d(ah, bh) + (d(ah, bl) + d(al, bh))


def _mm_exact_rhs(a, b_bf16, dims=_NN):
    d = lambda p: lax.dot_general(p, b_bf16, dims, preferred_element_type=F32)
    hi, mid, lo = _split3(a)
    return d(hi) + (d(mid) + d(lo))


def _sigmoid(x):
    return 1.0 / (1.0 + jnp.exp(-x))


def _softplus(x):
    return jnp.maximum(x, 0.0) + jnp.log(1.0 + jnp.exp(-jnp.abs(x)))


def _norm_matmul_kernel(x_ref, g_ref, w_ref, o_ref, h_ref):
    @pl.when(pl.program_id(1) == 0)
    def _():
        x = x_ref[...]
        ms = jnp.mean(x * x, axis=-1, keepdims=True)
        h_ref[...] = (x * lax.rsqrt(ms + NORM_EPS) * g_ref[...]).astype(BF16)

    o_ref[...] = jnp.dot(h_ref[...], w_ref[...], preferred_element_type=F32)


def _norm_matmul(x2, gain, w, tm=512, tn=512):
    m, d = x2.shape
    n = w.shape[1]
    return pl.pallas_call(
        _norm_matmul_kernel,
        grid=(m // tm, n // tn),
        in_specs=[pl.BlockSpec((tm, d), lambda i, j: (i, 0)),
                  pl.BlockSpec((1, d), lambda i, j: (0, 0)),
                  pl.BlockSpec((d, tn), lambda i, j: (0, j))],
        out_specs=pl.BlockSpec((tm, tn), lambda i, j: (i, j)),
        out_shape=jax.ShapeDtypeStruct((m, n), F32),
        scratch_shapes=[pltpu.VMEM((tm, d), BF16)],
        compiler_params=_cparams("parallel", "arbitrary"),
        name="norm_matmul",
    )(x2, gain.reshape(1, d), w)


def _shifted(cur, halo, mu, first):
    rows = lax.broadcasted_iota(jnp.int32, cur.shape, 0)
    last = jnp.where(first, 0.0, halo[7:8, :])
    prev = jnp.where(rows == 0, last, pltpu.roll(cur, 1, axis=0))
    return cur + (prev - cur) * mu


def _rwkv_prep_kernel(has_vres, *refs):
    if has_vres:
        (zm_ref, zmh_ref, zl_ref, zlh_ref, mum_ref, mul_ref, w0_ref, w2_ref, a0_ref, a2_ref, g2_ref,
         kk_ref, ka_ref, hsum_ref, hbc_ref, vf_ref, v0_ref, v2_ref,
         r_ref, lw_ref, k_ref, v_ref, an_ref, b_ref, g_ref) = refs
    else:
        (zm_ref, zmh_ref, zl_ref, zlh_ref, mum_ref, mul_ref, w0_ref, w2_ref, a0_ref, a2_ref, g2_ref,
         kk_ref, ka_ref, hsum_ref, hbc_ref,
         r_ref, lw_ref, k_ref, v_ref, an_ref, b_ref, g_ref) = refs
    c = r_ref.shape[-1]
    first = pl.program_id(1) == 0
    zs = _shifted(zm_ref[0], zmh_ref[0], mum_ref[...], first)
    zl = _shifted(zl_ref[0], zlh_ref[0], mul_ref[...], first)
    r, k, v = zs[:, :c], zs[:, c:2 * c], zs[:, 2 * c:3 * c]
    o = 0
    wd = zl[:, o:o + DECAY_LORA]
    o += DECAY_LORA
    ad = zl[:, o:o + ICLR_LORA]
    o += ICLR_LORA
    gd = zl[:, o:o + GATE_LORA]
    o += GATE_LORA
    w_log = -_softplus(-(w0_ref[...] + _mm(jnp.tanh(wd), w2_ref[...], passes=3))) - 0.5
    lw_ref[0] = -jnp.exp(w_log)
    a = _sigmoid(a0_ref[...] + _mm(ad, a2_ref[...], passes=3))
    g_ref[0] = _mm(_sigmoid(gd), g2_ref[...], passes=3)
    if has_vres:
        vd = zl[:, o:o + VRES_LORA]
        v = v + (vf_ref[0] - v) * _sigmoid(v0_ref[...] + _mm(vd, v2_ref[...], passes=3))
    kk = k * kk_ref[...]
    ss = _mm_exact_rhs(kk * kk, hsum_ref[...])
    inv = 1.0 / jnp.maximum(jnp.sqrt(ss), 1e-12)
    kk = kk * _mm_exact_rhs(inv, hbc_ref[...])
    r_ref[0] = r
    k_ref[0] = k * (1.0 + (a - 1.0) * ka_ref[...])
    v_ref[0] = v
    an_ref[0] = -kk
    b_ref[0] = kk * a


def _rwkv_prep(z3, c, lora_col, mu_main, mu_lora, w0, w2, a0, a2, g2, k_k, k_a, hsum, hbc, vres, tm=256):
    bsz, s, _ = z3.shape
    has_vres = vres is not None
    row = lambda a: a.reshape(1, -1)
    hb = tm // 8
    lcb = lora_col // LORA_PAD
    halo = lambda b, i: (b, jnp.maximum(i * hb - 1, 0), 0)
    halo_l = lambda b, i: (b, jnp.maximum(i * hb - 1, 0), lcb)
    full = lambda a: pl.BlockSpec(a.shape, lambda b, i: (0,) * a.ndim)
    ins = [z3, z3, z3, z3, row(mu_main), row(mu_lora), row(w0), w2, row(a0), a2, g2, row(k_k), row(k_a),
           hsum, hbc]
    in_specs = [pl.BlockSpec((1, tm, 3 * c), lambda b, i: (b, i, 0)),
                pl.BlockSpec((1, 8, 3 * c), halo),
                pl.BlockSpec((1, tm, LORA_PAD), lambda b, i: (b, i, lcb)),
                pl.BlockSpec((1, 8, LORA_PAD), halo_l)] + [full(a) for a in ins[4:]]
    if has_vres:
        v_first, v0, v2 = vres
        extra = [v_first, row(v0), v2]
        ins += extra
        in_specs += [pl.BlockSpec((1, tm, c), lambda b, i: (b, i, 0)), full(extra[1]), full(extra[2])]
    out_spec = pl.BlockSpec((1, tm, c), lambda b, i: (b, i, 0))
    out_sd = jax.ShapeDtypeStruct((bsz, s, c), F32)
    return pl.pallas_call(
        functools.partial(_rwkv_prep_kernel, has_vres),
        grid=(bsz, s // tm),
        in_specs=in_specs,
        out_specs=[out_spec] * 7,
        out_shape=[out_sd] * 7,
        compiler_params=_cparams("parallel", "arbitrary"),
        name="rwkv_prep",
    )(*ins)


def _unit_lower_inverse(a_strict, rows, cols):
    n = a_strict.shape[0]
    lower = rows > cols
    t = jnp.where(rows == cols, 1.0, 0.0) + jnp.where(lower & ((rows >> 1) == (cols >> 1)), a_strict, 0.0)
    sh = 1
    while (2 << sh) <= n:
        off = jnp.where(lower & ((rows >> (sh + 1)) == (cols >> (sh + 1))) & ((rows >> sh) != (cols >> sh)),
                        a_strict, 0.0)
        t = t + _mm(_mm(t, off, passes=3), t, passes=3)
        sh += 1
    return t


def _wkv_kernel(r_ref, lw_ref, k_ref, v_ref, an_ref, b_ref, y_ref, state_ref):
    ln = r_ref.shape[1]
    hg = r_ref.shape[2] // RWKV_HEAD
    n = RWKV_HEAD

    @pl.when(pl.program_id(2) == 0)
    def _():
        state_ref[...] = jnp.zeros_like(state_ref)

    rows = lax.broadcasted_iota(jnp.int32, (ln, ln), 0)
    cols = lax.broadcasted_iota(jnp.int32, (ln, ln), 1)
    tril = jnp.where(rows >= cols, 1.0, 0.0).astype(BF16)
    lw = lw_ref[0]
    cw = _mm_exact_rhs_left(tril, lw)
    cw_end = cw[ln - 1:ln, :]
    e_in = jnp.exp(cw)
    e_ex = jnp.exp(cw - lw)
    e_neg = jnp.exp(-cw)
    e_end = jnp.exp(cw_end - cw)
    w_end = jnp.exp(cw_end)
    r_t = r_ref[0] * e_in
    a_t = an_ref[0] * e_ex
    b_t = b_ref[0] * e_neg
    k_t = k_ref[0] * e_neg
    b_h = b_ref[0] * e_end
    k_h = k_ref[0] * e_end
    v = v_ref[0]
    eye_n = lax.broadcasted_iota(jnp.int32, (n, n), 0) == lax.broadcasted_iota(jnp.int32, (n, n), 1)

    ys = []
    for h in range(hg):
        sl = slice(h * n, (h + 1) * n)
        ar = jnp.concatenate([a_t[:, sl], r_t[:, sl]], axis=0)
        bk = jnp.concatenate([b_t[:, sl], k_t[:, sl]], axis=0)
        amat = _mm(ar, bk, _NT, passes=3)
        a_ab = jnp.where(rows > cols, amat[:ln, :ln], 0.0)
        a_ak = jnp.where(rows > cols, amat[:ln, ln:], 0.0)
        a_rb = jnp.where(rows >= cols, amat[ln:, :ln], 0.0)
        a_rk = jnp.where(rows >= cols, amat[ln:, ln:], 0.0)
        t = _unit_lower_inverse(a_ab, rows, cols)
        vh = v[:, sl]
        rhs = jnp.concatenate([a_t[:, sl], _mm(a_ak, vh, passes=3)], axis=1)
        pq = _mm(t, rhs, passes=3)
        p, q = pq[:, :n], pq[:, n:]
        r_eff = r_t[:, sl] + _mm(a_rb, p, passes=3)
        y0 = _mm(a_rb, q, passes=3) + _mm(a_rk, vh, passes=3)
        m_mat = jnp.where(eye_n, jnp.broadcast_to(w_end[:, sl], (n, n)), 0.0) + _mm(b_h[:, sl], p, _TN, passes=3)
        n_mat = _mm(b_h[:, sl], q, _TN, passes=3) + _mm(k_h[:, sl], vh, _TN, passes=3)
        st = state_ref[h]
        ys.append(_mm(r_eff, st, passes=3) + y0)
        state_ref[h] = _mm(m_mat, st, passes=3) + n_mat
    y_ref[0] = jnp.concatenate(ys, axis=1)


def _mm_exact_rhs_left(l_bf16, a):
    d = lambda p: lax.dot_general(l_bf16, p, _NN, preferred_element_type=F32)
    hi, mid, lo = _split3(a)
    return d(hi) + (d(mid) + d(lo))


def _wkv(r, lw, k, v, an, b, heads_per_step=2):
    bsz, s, c = r.shape
    wb = heads_per_step * RWKV_HEAD
    spec = pl.BlockSpec((1, WKV_CHUNK, wb), lambda bi, hi, ci: (bi, ci, hi))
    return pl.pallas_call(
        _wkv_kernel,
        grid=(bsz, c // wb, s // WKV_CHUNK),
        in_specs=[spec] * 6,
        out_specs=spec,
        out_shape=jax.ShapeDtypeStruct((bsz, s, c), F32),
        scratch_shapes=[pltpu.VMEM((heads_per_step, RWKV_HEAD, RWKV_HEAD), F32)],
        compiler_params=_cparams("parallel", "parallel", "arbitrary"),
        name="wkv",
    )(r, lw, k, v, an, b)


def _rwkv_post_kernel(y_ref, r_ref, k_ref, v_ref, g_ref, rk_ref, lg_ref, lb_ref, hsum_ref, hbc_ref, o_ref):
    hsum = hsum_ref[...]
    hbc = hbc_ref[...]
    y = y_ref[0]
    inv_n = 1.0 / RWKV_HEAD
    mu = _mm_exact_rhs(_mm_exact_rhs(y, hsum) * inv_n, hbc)
    d = y - mu
    var = _mm_exact_rhs(d * d, hsum) * inv_n
    yn = d * _mm_exact_rhs(lax.rsqrt(var + LNX_EPS), hbc)
    yn = yn * lg_ref[...] + lb_ref[...]
    bonus = _mm_exact_rhs(_mm_exact_rhs(r_ref[0] * k_ref[0] * rk_ref[...], hsum), hbc) * v_ref[0]
    o_ref[0] = ((yn + bonus) * g_ref[0]).astype(o_ref.dtype)


def _rwkv_post(y, r, k, v, g, r_k, lnx_g, lnx_b, hsum, hbc, tm=256):
    bsz, s, c = y.shape
    row = lambda a: a.reshape(1, -1)
    full = lambda a: pl.BlockSpec(a.shape, lambda b, i: (0,) * a.ndim)
    spec = pl.BlockSpec((1, tm, c), lambda b, i: (b, i, 0))
    small = [row(r_k), row(lnx_g), row(lnx_b), hsum, hbc]
    return pl.pallas_call(
        _rwkv_post_kernel,
        grid=(bsz, s // tm),
        in_specs=[spec] * 5 + [full(a) for a in small],
        out_specs=spec,
        out_shape=jax.ShapeDtypeStruct((bsz, s, c), BF16),
        compiler_params=_cparams("parallel", "parallel"),
        name="rwkv_post",
    )(y, r, k, v, g, *small)


def _rope_kernel(q_ref, k_ref, v_ref, cos_ref, sin_ref, qo_ref, ko_ref, vo_ref, km_ref):
    cos = cos_ref[...]
    sin = sin_ref[...]
    nh = q_ref.shape[-1] // ATTN_HEAD
    inv_rows = 1.0 / q_ref.shape[1]
    for h in range(nh):
        sl = slice(h * ATTN_HEAD, (h + 1) * ATTN_HEAD)
        q = q_ref[0, :, sl]
        k = k_ref[0, :, sl]
        qo_ref[0, :, sl] = q * cos + pltpu.roll(q, ATTN_HEAD // 2, axis=1) * sin
        kr = k * cos + pltpu.roll(k, ATTN_HEAD // 2, axis=1) * sin
        ko_ref[0, :, sl] = kr.astype(BF16)
        km_ref[0, 0, :, sl] = jnp.sum(kr, axis=0, keepdims=True) * inv_rows
    vo_ref[0] = v_ref[0].astype(BF16)


def _rope(z3, col0, width, cos2, sin2):
    bsz, s, _ = z3.shape
    nb = s // MOBA_BLOCK
    cb = col0 // width
    blk = lambda j: pl.BlockSpec((1, MOBA_BLOCK, width), lambda b, i: (b, i, cb + j))
    tab = pl.BlockSpec((MOBA_BLOCK, ATTN_HEAD), lambda b, i: (i, 0))
    out = pl.BlockSpec((1, MOBA_BLOCK, width), lambda b, i: (b, i, 0))
    return pl.pallas_call(
        _rope_kernel,
        grid=(bsz, nb),
        in_specs=[blk(0), blk(1), blk(2), tab, tab],
        out_specs=[out, out, out, pl.BlockSpec((1, 1, 1, width), lambda b, i: (b, i, 0, 0))],
        out_shape=[jax.ShapeDtypeStruct((bsz, s, width), F32),
                   jax.ShapeDtypeStruct((bsz, s, width), BF16),
                   jax.ShapeDtypeStruct((bsz, s, width), BF16),
                   jax.ShapeDtypeStruct((bsz, nb, 1, width), F32)],
        compiler_params=_cparams("parallel", "parallel"),
        name="rope",
    )(z3, z3, z3, cos2, sin2)


def _moba_kernel(q_ref, k_ref, v_ref, km_ref, o_ref):
    blk = MOBA_BLOCK
    qb = pl.program_id(2)
    nb = km_ref.shape[1]
    scale = ATTN_HEAD ** -0.5
    q = q_ref[0]
    gate = _mm(q, km_ref[0], _NT, passes=3)
    blk_id = lax.broadcasted_iota(jnp.int32, (blk, nb), 1)
    neg = -jnp.inf
    gate = jnp.where(blk_id < qb, gate, neg)
    rank = jnp.zeros((blk, nb), jnp.int32)
    for m in range(nb):
        gm = gate[:, m:m + 1]
        ahead = (gm > gate) | ((gm == gate) & (m < blk_id))
        rank += ahead.astype(jnp.int32)
    sel = (blk_id < qb) & (rank < MOBA_TOPK)
    bias = jnp.where(sel, 0.0, neg)

    qs = (q * scale).astype(BF16)
    own = pl.multiple_of(qb * blk, blk)
    s_own = lax.dot_general(qs, k_ref[0, pl.ds(own, blk), :], _NT, preferred_element_type=F32)
    qi = lax.broadcasted_iota(jnp.int32, (blk, blk), 0)
    ki = lax.broadcasted_iota(jnp.int32, (blk, blk), 1)
    s_own = jnp.where(ki <= qi, s_own, neg)
    m0 = jnp.max(s_own, axis=-1, keepdims=True)
    p = jnp.exp(s_own - m0)
    l0 = jnp.sum(p, axis=-1, keepdims=True)
    acc0 = jnp.dot(p.astype(BF16), v_ref[0, pl.ds(own, blk), :], preferred_element_type=F32)

    def body(kb, carry):
        m_prev, l_prev, acc = carry
        start = pl.multiple_of(kb * blk, blk)
        s = lax.dot_general(qs, k_ref[0, pl.ds(start, blk), :], _NT, preferred_element_type=F32)
        s = s + jnp.min(jnp.where(blk_id == kb, bias, 0.0), axis=-1, keepdims=True)
        m_new = jnp.maximum(m_prev, jnp.max(s, axis=-1, keepdims=True))
        alpha = jnp.exp(m_prev - m_new)
        p = jnp.exp(s - m_new)
        l_new = alpha * l_prev + jnp.sum(p, axis=-1, keepdims=True)
        acc = alpha * acc + jnp.dot(p.astype(BF16), v_ref[0, pl.ds(start, blk), :], preferred_element_type=F32)
        return m_new, l_new, acc

    _, l_fin, acc = lax.fori_loop(0, qb, body, (m0, l0, acc0))
    o_ref[0] = (acc / l_fin).astype(o_ref.dtype)


def _moba(q, k, v, kmean):
    bsz, s, width = q.shape
    nh = width // ATTN_HEAD
    nb = s // MOBA_BLOCK
    kv = pl.BlockSpec((1, s, ATTN_HEAD), lambda b, h, i: (b, 0, h))
    qo = pl.BlockSpec((1, MOBA_BLOCK, ATTN_HEAD), lambda b, h, i: (b, i, h))
    return pl.pallas_call(
        _moba_kernel,
        grid=(bsz, nh, nb),
        in_specs=[qo, kv, kv, pl.BlockSpec((1, nb, ATTN_HEAD), lambda b, h, i: (b, 0, h))],
        out_specs=qo,
        out_shape=jax.ShapeDtypeStruct((bsz, s, width), BF16),
        compiler_params=_cparams("parallel", "parallel", "arbitrary"),
        name="moba",
    )(q, k, v, kmean)


def _out_proj_kernel(yr_ref, ya_ref, wr_ref, wa_ref, x_ref, g_ref, o_ref):
    y = jnp.dot(yr_ref[...], wr_ref[...], preferred_element_type=F32)
    y += jnp.dot(ya_ref[...], wa_ref[...], preferred_element_type=F32)
    ms = jnp.mean(y * y, axis=-1, keepdims=True)
    o_ref[...] = x_ref[...] + y * lax.rsqrt(ms + NORM_EPS) * g_ref[...]


def _out_proj(y_r, y_a, w_r, w_a, x2, gain, tm=256):
    m, d = x2.shape
    cr, ca = y_r.shape[1], y_a.shape[1]
    const = lambda shape: pl.BlockSpec(shape, lambda i: (0, 0))
    return pl.pallas_call(
        _out_proj_kernel,
        grid=(m // tm,),
        in_specs=[pl.BlockSpec((tm, cr), lambda i: (i, 0)), pl.BlockSpec((tm, ca), lambda i: (i, 0)),
                  const((cr, d)), const((ca, d)), pl.BlockSpec((tm, d), lambda i: (i, 0)), const((1, d))],
        out_specs=pl.BlockSpec((tm, d), lambda i: (i, 0)),
        out_shape=jax.ShapeDtypeStruct((m, d), F32),
        compiler_params=_cparams("parallel"),
        name="out_proj",
    )(y_r, y_a, w_r, w_a, x2, gain.reshape(1, d))


def _mlp_kernel(x_ref, gpre_ref, wu_ref, wd_ref, gpost_ref, o_ref, h_ref, acc_ref):
    f = pl.program_id(1)

    @pl.when(f == 0)
    def _():
        x = x_ref[...]
        ms = jnp.mean(x * x, axis=-1, keepdims=True)
        h_ref[...] = (x * lax.rsqrt(ms + NORM_EPS) * gpre_ref[...]).astype(BF16)
        acc_ref[...] = jnp.zeros_like(acc_ref)

    u = jnp.maximum(jnp.dot(h_ref[...], wu_ref[...], preferred_element_type=F32), 0.0)
    acc_ref[...] += jnp.dot((u * u).astype(BF16), wd_ref[...], preferred_element_type=F32)

    @pl.when(f == pl.num_programs(1) - 1)
    def _():
        mlp = acc_ref[...]
        ms = jnp.mean(mlp * mlp, axis=-1, keepdims=True)
        o_ref[...] = x_ref[...] + mlp * lax.rsqrt(ms + NORM_EPS) * gpost_ref[...]


def _mlp(x2, g_pre, w_up, w_down, g_post, tm=512, tf=512):
    m, d = x2.shape
    dff = w_up.shape[1]
    return pl.pallas_call(
        _mlp_kernel,
        grid=(m // tm, dff // tf),
        in_specs=[pl.BlockSpec((tm, d), lambda i, f: (i, 0)),
                  pl.BlockSpec((1, d), lambda i, f: (0, 0)),
                  pl.BlockSpec((d, tf), lambda i, f: (0, f)),
                  pl.BlockSpec((tf, d), lambda i, f: (f, 0)),
                  pl.BlockSpec((1, d), lambda i, f: (0, 0))],
        out_specs=pl.BlockSpec((tm, d), lambda i, f: (i, 0)),
        out_shape=jax.ShapeDtypeStruct((m, d), F32),
        scratch_shapes=[pltpu.VMEM((tm, d), BF16), pltpu.VMEM((tm, d), F32)],
        compiler_params=_cparams("parallel", "arbitrary"),
        name="mlp",
    )(x2, g_pre.reshape(1, d), w_up, w_down, g_post.reshape(1, d))


def _head_indicators(c):
    head = jnp.arange(c) // RWKV_HEAD
    hsum = (head[:, None] == jnp.arange(128)[None, :]).astype(BF16)
    return hsum, hsum.T


def _rope_tables(s):
    half = ATTN_HEAD // 2
    inv_freq = ROPE_THETA ** (-jnp.arange(half, dtype=F32) / half)
    ang = jnp.arange(s).astype(F32)[:, None] * inv_freq[None, :]
    cos, sin = jnp.cos(ang), jnp.sin(ang)
    return jnp.concatenate([cos, cos], axis=-1), jnp.concatenate([-sin, sin], axis=-1)


def kernel(x, norm_mix_pre, norm_mix_post, norm_mlp_pre, norm_mlp_post, w_in, w_in_vres, shift_mu, shift_mu_vres, decay_w0, decay_w2, iclr_a0, iclr_a2, vres_v0, vres_v2, gate_g2, k_k, k_a, r_k, lnx_gain, lnx_bias, w_out, w_up, w_down):
    bsz, s, d = x.shape
    depth = w_in.shape[0]
    c = decay_w0.shape[1]
    n_lora = DECAY_LORA + ICLR_LORA + GATE_LORA
    n_shift = 3 * c + n_lora
    ca = (w_in.shape[2] - n_shift) // 3
    hsum, hbc = _head_indicators(c)
    cos2, sin2 = _rope_tables(s)
    x2 = x.reshape(bsz * s, d)
    v_first = None
    for i in range(depth):
        lora_w = [w_in[i][:, 3 * c:n_shift]]
        lora_mu = [shift_mu[i][3 * c:]]
        n_used = n_lora
        if i > 0:
            lora_w.append(w_in_vres[i - 1])
            lora_mu.append(shift_mu_vres[i - 1])
            n_used += VRES_LORA
        lora_w.append(jnp.zeros((d, LORA_PAD - n_used), F32))
        lora_mu.append(jnp.zeros((LORA_PAD - n_used,), F32))
        w_comb = jnp.concatenate([w_in[i][:, :3 * c], w_in[i][:, n_shift:]] + lora_w, axis=1).astype(BF16)
        z = _norm_matmul(x2, norm_mix_pre[i], w_comb)
        z3 = z.reshape(bsz, s, -1)

        vres = None if i == 0 else (v_first, vres_v0[i - 1], vres_v2[i - 1])
        r, lw, k, v, an, b, g = _rwkv_prep(z3, c, 3 * c + 3 * ca, shift_mu[i][:3 * c],
                                           jnp.concatenate(lora_mu), decay_w0[i],
                                           decay_w2[i], iclr_a0[i], iclr_a2[i], gate_g2[i], k_k[i], k_a[i],
                                           hsum, hbc, vres)
        if i == 0:
            v_first = v
        y = _wkv(r, lw, k, v, an, b)
        y_r = _rwkv_post(y, r, k, v, g, r_k[i].reshape(-1), lnx_gain[i], lnx_bias[i], hsum, hbc)

        q_rot, k_rot, v_att, kmean = _rope(z3, 3 * c, ca, cos2, sin2)
        y_a = _moba(q_rot, k_rot, v_att, kmean.reshape(bsz, -1, ca))

        w_o = w_out[i].astype(BF16)
        x2 = _out_proj(y_r.reshape(bsz * s, c), y_a.reshape(bsz * s, ca), w_o[:c], w_o[c:], x2,
                       norm_mix_post[i])
        x2 = _mlp(x2, norm_mlp_pre[i], w_up[i].astype(BF16), w_down[i].astype(BF16), norm_mlp_post[i])
    return x2.reshape(bsz, s, d)
```

```python
import functools

import jax
import jax.numpy as jnp
from jax import lax
from jax.experimental import pallas as pl
from jax.experimental.pallas import tpu as pltpu

F32 = jnp.float32
BF16 = jnp.bfloat16

RWKV_HEAD = 64
DECAY_LORA = 64
ICLR_LORA = 64
VRES_LORA = 32
GATE_LORA = 128
ATTN_HEAD = 128
MOBA_BLOCK = 256
MOBA_TOPK = 3
ROPE_THETA = 10000.0
NORM_EPS = 1e-6
LNX_EPS = 64e-5

LORA_PAD = 512
WKV_CHUNK = 64
P_A, P_INV, P_PQ, P_OUT, P_STATE = 1, 1, 1, 1, 1
VMEM_LIMIT = 56 * 1024 * 1024


def _cparams(*sem):
    return pltpu.CompilerParams(dimension_semantics=sem, vmem_limit_bytes=VMEM_LIMIT)


_NN = (((1,), (0,)), ((), ()))
_NT = (((1,), (1,)), ((), ()))
_TN = (((0,), (0,)), ((), ()))


def _split2(x):
    hi = x.astype(BF16)
    lo = (x - hi.astype(F32)).astype(BF16)
    return hi, lo


def _split3(x):
    hi = x.astype(BF16)
    r1 = x - hi.astype(F32)
    mid = r1.astype(BF16)
    lo = (r1 - mid.astype(F32)).astype(BF16)
    return hi, mid, lo


def _mm(a, b, dims=_NN, passes=1):
    d = lambda p, q: lax.dot_general(p, q, dims, preferred_element_type=F32)
    if passes == 1:
        return d(a.astype(BF16), b.astype(BF16))
    ah, al = _split2(a)
    bh, bl = _split2(b)
    return d(ah, bh) + (d(ah, bl) + d(al, bh))


def _mm_exact_rhs(a, b_bf16, dims=_NN):
    d = lambda p: lax.dot_general(p, b_bf16, dims, preferred_element_type=F32)
    hi, mid, lo = _split3(a)
    return d(hi) + (d(mid) + d(lo))


def _sigmoid(x):
    return 1.0 / (1.0 + jnp.exp(-x))


def _softplus(x):
    return jnp.maximum(x, 0.0) + jnp.log(1.0 + jnp.exp(-jnp.abs(x)))


def _norm_matmul_kernel(x_ref, g_ref, w_ref, o_ref, h_ref):
    @pl.when(pl.program_id(1) == 0)
    def _():
        x = x_ref[...]
        ms = jnp.mean(x * x, axis=-1, keepdims=True)
        h_ref[...] = (x * lax.rsqrt(ms + NORM_EPS) * g_ref[...]).astype(BF16)

    o_ref[...] = jnp.dot(h_ref[...], w_ref[...], preferred_element_type=F32)


def _norm_matmul(x2, gain, w, tm=512, tn=512):
    m, d = x2.shape
    n = w.shape[1]
    return pl.pallas_call(
        _norm_matmul_kernel,
        grid=(m // tm, n // tn),
        in_specs=[pl.BlockSpec((tm, d), lambda i, j: (i, 0)),
                  pl.BlockSpec((1, d), lambda i, j: (0, 0)),
                  pl.BlockSpec((d, tn), lambda i, j: (0, j))],
        out_specs=pl.BlockSpec((tm, tn), lambda i, j: (i, j)),
        out_shape=jax.ShapeDtypeStruct((m, n), F32),
        scratch_shapes=[pltpu.VMEM((tm, d), BF16)],
        compiler_params=_cparams("parallel", "arbitrary"),
        name="norm_matmul",
    )(x2, gain.reshape(1, d), w)


def _shifted(cur, halo, mu, first):
    rows = lax.broadcasted_iota(jnp.int32, cur.shape, 0)
    last = jnp.where(first, 0.0, halo[7:8, :])
    prev = jnp.where(rows == 0, last, pltpu.roll(cur, 1, axis=0))
    return cur + (prev - cur) * mu


def _rwkv_prep_kernel(has_vres, *refs):
    if has_vres:
        (zm_ref, zmh_ref, zl_ref, zlh_ref, mum_ref, mul_ref, w0_ref, w2_ref, a0_ref, a2_ref, g2_ref,
         kk_ref, ka_ref, hsum_ref, hbc_ref, vf_ref, v0_ref, v2_ref,
         r_ref, lw_ref, k_ref, v_ref, an_ref, b_ref, g_ref) = refs
    else:
        (zm_ref, zmh_ref, zl_ref, zlh_ref, mum_ref, mul_ref, w0_ref, w2_ref, a0_ref, a2_ref, g2_ref,
         kk_ref, ka_ref, hsum_ref, hbc_ref,
         r_ref, lw_ref, k_ref, v_ref, an_ref, b_ref, g_ref) = refs
    c = r_ref.shape[-1]
    first = pl.program_id(1) == 0
    zs = _shifted(zm_ref[0], zmh_ref[0], mum_ref[...], first)
    zl = _shifted(zl_ref[0], zlh_ref[0], mul_ref[...], first)
    r, k, v = zs[:, :c], zs[:, c:2 * c], zs[:, 2 * c:3 * c]
    o = 0
    wd = zl[:, o:o + DECAY_LORA]
    o += DECAY_LORA
    ad = zl[:, o:o + ICLR_LORA]
    o += ICLR_LORA
    gd = zl[:, o:o + GATE_LORA]
    o += GATE_LORA
    w_log = -_softplus(-(w0_ref[...] + _mm(jnp.tanh(wd), w2_ref[...], passes=3))) - 0.5
    lw_ref[0] = -jnp.exp(w_log)
    a = _sigmoid(a0_ref[...] + _mm(ad, a2_ref[...], passes=3))
    g_ref[0] = _mm(_sigmoid(gd), g2_ref[...], passes=3)
    if has_vres:
        vd = zl[:, o:o + VRES_LORA]
        v = v + (vf_ref[0] - v) * _sigmoid(v0_ref[...] + _mm(vd, v2_ref[...], passes=3))
    kk = k * kk_ref[...]
    ss = _mm_exact_rhs(kk * kk, hsum_ref[...])
    inv = 1.0 / jnp.maximum(jnp.sqrt(ss), 1e-12)
    kk = kk * _mm_exact_rhs(inv, hbc_ref[...])
    r_ref[0] = r
    k_ref[0] = k * (1.0 + (a - 1.0) * ka_ref[...])
    v_ref[0] = v
    an_ref[0] = -kk
    b_ref[0] = kk * a


def _rwkv_prep(z3, c, lora_col, mu_main, mu_lora, w0, w2, a0, a2, g2, k_k, k_a, hsum, hbc, vres, tm=256):
    bsz, s, _ = z3.shape
    has_vres = vres is not None
    row = lambda a: a.reshape(1, -1)
    hb = tm // 8
    lcb = lora_col // LORA_PAD
    halo = lambda b, i: (b, jnp.maximum(i * hb - 1, 0), 0)
    halo_l = lambda b, i: (b, jnp.maximum(i * hb - 1, 0), lcb)
    full = lambda a: pl.BlockSpec(a.shape, lambda b, i: (0,) * a.ndim)
    ins = [z3, z3, z3, z3, row(mu_main), row(mu_lora), row(w0), w2, row(a0), a2, g2, row(k_k), row(k_a),
           hsum, hbc]
    in_specs = [pl.BlockSpec((1, tm, 3 * c), lambda b, i: (b, i, 0)),
                pl.BlockSpec((1, 8, 3 * c), halo),
                pl.BlockSpec((1, tm, LORA_PAD), lambda b, i: (b, i, lcb)),
                pl.BlockSpec((1, 8, LORA_PAD), halo_l)] + [full(a) for a in ins[4:]]
    if has_vres:
        v_first, v0, v2 = vres
        extra = [v_first, row(v0), v2]
        ins += extra
        in_specs += [pl.BlockSpec((1, tm, c), lambda b, i: (b, i, 0)), full(extra[1]), full(extra[2])]
    out_spec = pl.BlockSpec((1, tm, c), lambda b, i: (b, i, 0))
    out_sd = jax.ShapeDtypeStruct((bsz, s, c), F32)
    return pl.pallas_call(
        functools.partial(_rwkv_prep_kernel, has_vres),
        grid=(bsz, s // tm),
        in_specs=in_specs,
        out_specs=[out_spec] * 7,
        out_shape=[out_sd] * 7,
        compiler_params=_cparams("parallel", "arbitrary"),
        name="rwkv_prep",
    )(*ins)


def _mm_heads(a_list, b_list, dims=_NN, passes=3):
    d = lambda p, q: lax.dot_general(p, q, dims, preferred_element_type=F32)
    if passes == 1:
        return [d(a.astype(BF16), b.astype(BF16)) for a, b in zip(a_list, b_list)]
    sa = [_split2(a) for a in a_list]
    sb = [_split2(b) for b in b_list]
    out = [d(x[0], y[0]) for x, y in zip(sa, sb)]
    out = [o + d(x[0], y[1]) for o, x, y in zip(out, sa, sb)]
    return [o + d(x[1], y[0]) for o, x, y in zip(out, sa, sb)]


def _unit_lower_inverse(a_list, rows, cols):
    n = rows.shape[0]
    lower = rows > cols
    base = lower & ((rows >> 1) == (cols >> 1))
    t = [jnp.where(rows == cols, 1.0, jnp.where(base, a, 0.0)) for a in a_list]
    sh = 1
    while (2 << sh) <= n:
        sub = lower & ((rows >> (sh + 1)) == (cols >> (sh + 1))) & ((rows >> sh) != (cols >> sh))
        off = [jnp.where(sub, a, 0.0) for a in a_list]
        upd = _mm_heads(_mm_heads(t, off, passes=P_INV), t, passes=P_INV)
        t = [x + u for x, u in zip(t, upd)]
        sh += 1
    return t


def _wkv_kernel(r_ref, lw_ref, k_ref, v_ref, an_ref, b_ref, y_ref, state_ref):
    ln = r_ref.shape[1]
    hg = r_ref.shape[2] // RWKV_HEAD
    n = RWKV_HEAD

    @pl.when(pl.program_id(2) == 0)
    def _():
        state_ref[...] = jnp.zeros_like(state_ref)

    rows = lax.broadcasted_iota(jnp.int32, (ln, ln), 0)
    cols = lax.broadcasted_iota(jnp.int32, (ln, ln), 1)
    tril = jnp.where(rows >= cols, 1.0, 0.0).astype(BF16)
    lw = lw_ref[0]
    cw = _mm_exact_rhs_left(tril, lw)
    cw_end = cw[ln - 1:ln, :]
    e_neg = jnp.exp(-cw)
    w_end = jnp.exp(cw_end)
    r_t = r_ref[0] * jnp.exp(cw)
    a_t = an_ref[0] * jnp.exp(cw - lw)
    b_t = b_ref[0] * e_neg
    k_t = k_ref[0] * e_neg
    e_end = jnp.exp(cw_end - cw)
    b_h = b_ref[0] * e_end
    k_h = k_ref[0] * e_end
    v = v_ref[0]
    eye_n = lax.broadcasted_iota(jnp.int32, (n, n), 0) == lax.broadcasted_iota(jnp.int32, (n, n), 1)
    zeros = jnp.zeros((ln, n), F32)
    sls = [slice(h * n, (h + 1) * n) for h in range(hg)]

    ar = [jnp.concatenate([a_t[:, s], r_t[:, s]], axis=0) for s in sls]
    bk = [jnp.concatenate([b_t[:, s], k_t[:, s]], axis=0) for s in sls]
    amat = _mm_heads(ar, bk, _NT, passes=P_A)
    strict = rows > cols
    incl = rows >= cols
    a_ab = [jnp.where(strict, m[:ln, :ln], 0.0) for m in amat]
    a_ak = [jnp.where(strict, m[:ln, ln:], 0.0) for m in amat]
    a_r = [jnp.concatenate([jnp.where(incl, m[ln:, :ln], 0.0), jnp.where(incl, m[ln:, ln:], 0.0)], axis=1)
           for m in amat]
    vh = [v[:, s] for s in sls]
    akv = _mm_heads(a_ak, vh, passes=P_PQ)
    t = _unit_lower_inverse(a_ab, rows, cols)
    rhs = [jnp.concatenate([a_t[:, s], x], axis=1) for s, x in zip(sls, akv)]
    pq = _mm_heads(t, rhs, passes=P_PQ)
    pqv = [jnp.concatenate([x, jnp.concatenate([zeros, u], axis=1)], axis=0)
           for x, u in zip(pq, vh)]
    ry = _mm_heads(a_r, pqv, passes=P_OUT)
    bkh = [jnp.concatenate([b_h[:, s], k_h[:, s]], axis=0) for s in sls]
    mn = _mm_heads(bkh, pqv, _TN, passes=P_OUT)
    lhs = [jnp.concatenate([r_t[:, s] + x[:, :n],
                            jnp.where(eye_n, jnp.broadcast_to(w_end[:, s], (n, n)), 0.0) + m[:, :n]], axis=0)
           for s, x, m in zip(sls, ry, mn)]
    st = [state_ref[h] for h in range(hg)]
    upd = _mm_heads(lhs, st, passes=P_STATE)
    for h in range(hg):
        state_ref[h] = upd[h][ln:, :] + mn[h][:, n:]
    y_ref[0] = jnp.concatenate([u[:ln, :] + x[:, n:] for u, x in zip(upd, ry)], axis=1)


def _mm_exact_rhs_left(l_bf16, a):
    d = lambda p: lax.dot_general(l_bf16, p, _NN, preferred_element_type=F32)
    hi, mid, lo = _split3(a)
    return d(hi) + (d(mid) + d(lo))


def _wkv(r, lw, k, v, an, b, heads_per_step=16):
    bsz, s, c = r.shape
    wb = heads_per_step * RWKV_HEAD
    spec = pl.BlockSpec((1, WKV_CHUNK, wb), lambda bi, hi, ci: (bi, ci, hi))
    return pl.pallas_call(
        _wkv_kernel,
        grid=(bsz, c // wb, s // WKV_CHUNK),
        in_specs=[spec] * 6,
        out_specs=spec,
        out_shape=jax.ShapeDtypeStruct((bsz, s, c), F32),
        scratch_shapes=[pltpu.VMEM((heads_per_step, RWKV_HEAD, RWKV_HEAD), F32)],
        compiler_params=_cparams("parallel", "parallel", "arbitrary"),
        name="wkv",
    )(r, lw, k, v, an, b)


def _rwkv_post_kernel(y_ref, r_ref, k_ref, v_ref, g_ref, rk_ref, lg_ref, lb_ref, hsum_ref, hbc_ref, o_ref):
    hsum = hsum_ref[...]
    hbc = hbc_ref[...]
    y = y_ref[0]
    inv_n = 1.0 / RWKV_HEAD
    mu = _mm_exact_rhs(_mm_exact_rhs(y, hsum) * inv_n, hbc)
    d = y - mu
    var = _mm_exact_rhs(d * d, hsum) * inv_n
    yn = d * _mm_exact_rhs(lax.rsqrt(var + LNX_EPS), hbc)
    yn = yn * lg_ref[...] + lb_ref[...]
    bonus = _mm_exact_rhs(_mm_exact_rhs(r_ref[0] * k_ref[0] * rk_ref[...], hsum), hbc) * v_ref[0]
    o_ref[0] = ((yn + bonus) * g_ref[0]).astype(o_ref.dtype)


def _rwkv_post(y, r, k, v, g, r_k, lnx_g, lnx_b, hsum, hbc, tm=256):
    bsz, s, c = y.shape
    row = lambda a: a.reshape(1, -1)
    full = lambda a: pl.BlockSpec(a.shape, lambda b, i: (0,) * a.ndim)
    spec = pl.BlockSpec((1, tm, c), lambda b, i: (b, i, 0))
    small = [row(r_k), row(lnx_g), row(lnx_b), hsum, hbc]
    return pl.pallas_call(
        _rwkv_post_kernel,
        grid=(bsz, s // tm),
        in_specs=[spec] * 5 + [full(a) for a in small],
        out_specs=spec,
        out_shape=jax.ShapeDtypeStruct((bsz, s, c), BF16),
        compiler_params=_cparams("parallel", "parallel"),
        name="rwkv_post",
    )(y, r, k, v, g, *small)


def _rope_kernel(q_ref, k_ref, v_ref, cos_ref, sin_ref, qo_ref, ko_ref, vo_ref, km_ref):
    cos = cos_ref[...]
    sin = sin_ref[...]
    nh = q_ref.shape[-1] // ATTN_HEAD
    inv_rows = 1.0 / q_ref.shape[1]
    for h in range(nh):
        sl = slice(h * ATTN_HEAD, (h + 1) * ATTN_HEAD)
        q = q_ref[0, :, sl]
        k = k_ref[0, :, sl]
        qo_ref[0, :, sl] = q * cos + pltpu.roll(q, ATTN_HEAD // 2, axis=1) * sin
        kr = k * cos + pltpu.roll(k, ATTN_HEAD // 2, axis=1) * sin
        ko_ref[0, :, sl] = kr.astype(BF16)
        km_ref[0, 0, :, sl] = jnp.sum(kr, axis=0, keepdims=True) * inv_rows
    vo_ref[0] = v_ref[0].astype(BF16)


def _rope(z3, col0, width, cos2, sin2):
    bsz, s, _ = z3.shape
    nb = s // MOBA_BLOCK
    cb = col0 // width
    blk = lambda j: pl.BlockSpec((1, MOBA_BLOCK, width), lambda b, i: (b, i, cb + j))
    tab = pl.BlockSpec((MOBA_BLOCK, ATTN_HEAD), lambda b, i: (i, 0))
    out = pl.BlockSpec((1, MOBA_BLOCK, width), lambda b, i: (b, i, 0))
    return pl.pallas_call(
        _rope_kernel,
        grid=(bsz, nb),
        in_specs=[blk(0), blk(1), blk(2), tab, tab],
        out_specs=[out, out, out, pl.BlockSpec((1, 1, 1, width), lambda b, i: (b, i, 0, 0))],
        out_shape=[jax.ShapeDtypeStruct((bsz, s, width), F32),
                   jax.ShapeDtypeStruct((bsz, s, width), BF16),
                   jax.ShapeDtypeStruct((bsz, s, width), BF16),
                   jax.ShapeDtypeStruct((bsz, nb, 1, width), F32)],
        compiler_params=_cparams("parallel", "parallel"),
        name="rope",
    )(z3, z3, z3, cos2, sin2)


def _moba_kernel(q_ref, k_ref, v_ref, km_ref, o_ref):
    blk = MOBA_BLOCK
    qb = pl.program_id(2)
    nb = km_ref.shape[1]
    scale = ATTN_HEAD ** -0.5
    q = q_ref[0]
    gate = _mm(q, km_ref[0], _NT, passes=3)
    blk_id = lax.broadcasted_iota(jnp.int32, (blk, nb), 1)
    neg = -jnp.inf
    gate = jnp.where(blk_id < qb, gate, neg)
    rank = jnp.zeros((blk, nb), jnp.int32)
    for m in range(nb):
        gm = gate[:, m:m + 1]
        ahead = (gm > gate) | ((gm == gate) & (m < blk_id))
        rank += ahead.astype(jnp.int32)
    sel = (blk_id < qb) & (rank < MOBA_TOPK)
    bias = jnp.where(sel, 0.0, neg)

    qs = (q * scale).astype(BF16)
    own = pl.multiple_of(qb * blk, blk)
    s_own = lax.dot_general(qs, k_ref[0, pl.ds(own, blk), :], _NT, preferred_element_type=F32)
    qi = lax.broadcasted_iota(jnp.int32, (blk, blk), 0)
    ki = lax.broadcasted_iota(jnp.int32, (blk, blk), 1)
    s_own = jnp.where(ki <= qi, s_own, neg)
    m0 = jnp.max(s_own, axis=-1, keepdims=True)
    p = jnp.exp(s_own - m0)
    l0 = jnp.sum(p, axis=-1, keepdims=True)
    acc0 = jnp.dot(p.astype(BF16), v_ref[0, pl.ds(own, blk), :], preferred_element_type=F32)

    def body(kb, carry):
        m_prev, l_prev, acc = carry
        start = pl.multiple_of(kb * blk, blk)
        s = lax.dot_general(qs, k_ref[0, pl.ds(start, blk), :], _NT, preferred_element_type=F32)
        s = s + jnp.min(jnp.where(blk_id == kb, bias, 0.0), axis=-1, keepdims=True)
        m_new = jnp.maximum(m_prev, jnp.max(s, axis=-1, keepdims=True))
        alpha = jnp.exp(m_prev - m_new)
        p = jnp.exp(s - m_new)
        l_new = alpha * l_prev + jnp.sum(p, axis=-1, keepdims=True)
        acc = alpha * acc + jnp.dot(p.astype(BF16), v_ref[0, pl.ds(start, blk), :], preferred_element_type=F32)
        return m_new, l_new, acc

    _, l_fin, acc = lax.fori_loop(0, qb, body, (m0, l0, acc0))
    o_ref[0] = (acc / l_fin).astype(o_ref.dtype)


def _moba(q, k, v, kmean):
    bsz, s, width = q.shape
    nh = width // ATTN_HEAD
    nb = s // MOBA_BLOCK
    kv = pl.BlockSpec((1, s, ATTN_HEAD), lambda b, h, i: (b, 0, h))
    qo = pl.BlockSpec((1, MOBA_BLOCK, ATTN_HEAD), lambda b, h, i: (b, i, h))
    return pl.pallas_call(
        _moba_kernel,
        grid=(bsz, nh, nb),
        in_specs=[qo, kv, kv, pl.BlockSpec((1, nb, ATTN_HEAD), lambda b, h, i: (b, 0, h))],
        out_specs=qo,
        out_shape=jax.ShapeDtypeStruct((bsz, s, width), BF16),
        compiler_params=_cparams("parallel", "parallel", "arbitrary"),
        name="moba",
    )(q, k, v, kmean)


def _out_proj_kernel(yr_ref, ya_ref, wr_ref, wa_ref, x_ref, g_ref, o_ref):
    y = jnp.dot(yr_ref[...], wr_ref[...], preferred_element_type=F32)
    y += jnp.dot(ya_ref[...], wa_ref[...], preferred_element_type=F32)
    ms = jnp.mean(y * y, axis=-1, keepdims=True)
    o_ref[...] = x_ref[...] + y * lax.rsqrt(ms + NORM_EPS) * g_ref[...]


def _out_proj(y_r, y_a, w_r, w_a, x2, gain, tm=256):
    m, d = x2.shape
    cr, ca = y_r.shape[1], y_a.shape[1]
    const = lambda shape: pl.BlockSpec(shape, lambda i: (0, 0))
    return pl.pallas_call(
        _out_proj_kernel,
        grid=(m // tm,),
        in_specs=[pl.BlockSpec((tm, cr), lambda i: (i, 0)), pl.BlockSpec((tm, ca), lambda i: (i, 0)),
                  const((cr, d)), const((ca, d)), pl.BlockSpec((tm, d), lambda i: (i, 0)), const((1, d))],
        out_specs=pl.BlockSpec((tm, d), lambda i: (i, 0)),
        out_shape=jax.ShapeDtypeStruct((m, d), F32),
        compiler_params=_cparams("parallel"),
        name="out_proj",
    )(y_r, y_a, w_r, w_a, x2, gain.reshape(1, d))


def _mlp_kernel(x_ref, gpre_ref, wu_ref, wd_ref, gpost_ref, o_ref, h_ref, acc_ref):
    f = pl.program_id(1)

    @pl.when(f == 0)
    def _():
        x = x_ref[...]
        ms = jnp.mean(x * x, axis=-1, keepdims=True)
        h_ref[...] = (x * lax.rsqrt(ms + NORM_EPS) * gpre_ref[...]).astype(BF16)
        acc_ref[...] = jnp.zeros_like(acc_ref)

    u = jnp.maximum(jnp.dot(h_ref[...], wu_ref[...], preferred_element_type=F32), 0.0)
    acc_ref[...] += jnp.dot((u * u).astype(BF16), wd_ref[...], preferred_element_type=F32)

    @pl.when(f == pl.num_programs(1) - 1)
    def _():
        mlp = acc_ref[...]
        ms = jnp.mean(mlp * mlp, axis=-1, keepdims=True)
        o_ref[...] = x_ref[...] + mlp * lax.rsqrt(ms + NORM_EPS) * gpost_ref[...]


def _mlp(x2, g_pre, w_up, w_down, g_post, tm=512, tf=512):
    m, d = x2.shape
    dff = w_up.shape[1]
    return pl.pallas_call(
        _mlp_kernel,
        grid=(m // tm, dff // tf),
        in_specs=[pl.BlockSpec((tm, d), lambda i, f: (i, 0)),
                  pl.BlockSpec((1, d), lambda i, f: (0, 0)),
                  pl.BlockSpec((d, tf), lambda i, f: (0, f)),
                  pl.BlockSpec((tf, d), lambda i, f: (f, 0)),
                  pl.BlockSpec((1, d), lambda i, f: (0, 0))],
        out_specs=pl.BlockSpec((tm, d), lambda i, f: (i, 0)),
        out_shape=jax.ShapeDtypeStruct((m, d), F32),
        scratch_shapes=[pltpu.VMEM((tm, d), BF16), pltpu.VMEM((tm, d), F32)],
        compiler_params=_cparams("parallel", "arbitrary"),
        name="mlp",
    )(x2, g_pre.reshape(1, d), w_up, w_down, g_post.reshape(1, d))


def _head_indicators(c):
    head = jnp.arange(c) // RWKV_HEAD
    hsum = (head[:, None] == jnp.arange(128)[None, :]).astype(BF16)
    return hsum, hsum.T


def _rope_tables(s):
    half = ATTN_HEAD // 2
    inv_freq = ROPE_THETA ** (-jnp.arange(half, dtype=F32) / half)
    ang = jnp.arange(s).astype(F32)[:, None] * inv_freq[None, :]
    cos, sin = jnp.cos(ang), jnp.sin(ang)
    return jnp.concatenate([cos, cos], axis=-1), jnp.concatenate([-sin, sin], axis=-1)


def kernel(x, norm_mix_pre, norm_mix_post, norm_mlp_pre, norm_mlp_post, w_in, w_in_vres, shift_mu, shift_mu_vres, decay_w0, decay_w2, iclr_a0, iclr_a2, vres_v0, vres_v2, gate_g2, k_k, k_a, r_k, lnx_gain, lnx_bias, w_out, w_up, w_down):
    bsz, s, d = x.shape
    depth = w_in.shape[0]
    c = decay_w0.shape[1]
    n_lora = DECAY_LORA + ICLR_LORA + GATE_LORA
    n_shift = 3 * c + n_lora
    ca = (w_in.shape[2] - n_shift) // 3
    hsum, hbc = _head_indicators(c)
    cos2, sin2 = _rope_tables(s)
    x2 = x.reshape(bsz * s, d)
    v_first = None
    for i in range(depth):
        lora_w = [w_in[i][:, 3 * c:n_shift]]
        lora_mu = [shift_mu[i][3 * c:]]
        n_used = n_lora
        if i > 0:
            lora_w.append(w_in_vres[i - 1])
            lora_mu.append(shift_mu_vres[i - 1])
            n_used += VRES_LORA
        lora_w.append(jnp.zeros((d, LORA_PAD - n_used), F32))
        lora_mu.append(jnp.zeros((LORA_PAD - n_used,), F32))
        w_comb = jnp.concatenate([w_in[i][:, :3 * c], w_in[i][:, n_shift:]] + lora_w, axis=1).astype(BF16)
        z = _norm_matmul(x2, norm_mix_pre[i], w_comb)
        z3 = z.reshape(bsz, s, -1)

        vres = None if i == 0 else (v_first, vres_v0[i - 1], vres_v2[i - 1])
        r, lw, k, v, an, b, g = _rwkv_prep(z3, c, 3 * c + 3 * ca, shift_mu[i][:3 * c],
                                           jnp.concatenate(lora_mu), decay_w0[i],
                                           decay_w2[i], iclr_a0[i], iclr_a2[i], gate_g2[i], k_k[i], k_a[i],
                                           hsum, hbc, vres)
        if i == 0:
            v_first = v
        y = _wkv(r, lw, k, v, an, b)
        y_r = _rwkv_post(y, r, k, v, g, r_k[i].reshape(-1), lnx_gain[i], lnx_bias[i], hsum, hbc)

        q_rot, k_rot, v_att, kmean = _rope(z3, 3 * c, ca, cos2, sin2)
        y_a = _moba(q_rot, k_rot, v_att, kmean.reshape(bsz, -1, ca))

        w_o = w_out[i].astype(BF16)
        x2 = _out_proj(y_r.reshape(bsz * s, c), y_a.reshape(bsz * s, ca), w_o[:c], w_o[c:], x2,
                       norm_mix_post[i])
        x2 = _mlp(x2, norm_mlp_pre[i], w_up[i].astype(BF16), w_down[i].astype(BF16), norm_mlp_post[i])
    return x2.reshape(bsz, s, d)
```

```python
import functools

import jax
import jax.numpy as jnp
from jax import lax
from jax.experimental import pallas as pl
from jax.experimental.pallas import tpu as pltpu

F32 = jnp.float32
BF16 = jnp.bfloat16

RWKV_HEAD = 64
DECAY_LORA = 64
ICLR_LORA = 64
VRES_LORA = 32
GATE_LORA = 128
ATTN_HEAD = 128
MOBA_BLOCK = 256
MOBA_TOPK = 3
ROPE_THETA = 10000.0
NORM_EPS = 1e-6
LNX_EPS = 64e-5

LORA_PAD = 512
WKV_CHUNK = 64
P_A, P_INV, P_PQ, P_OUT, P_STATE = 1, 1, 1, 1, 1
VMEM_LIMIT = 56 * 1024 * 1024


def _cparams(*sem):
    return pltpu.CompilerParams(dimension_semantics=sem, vmem_limit_bytes=VMEM_LIMIT)


_NN = (((1,), (0,)), ((), ()))
_NT = (((1,), (1,)), ((), ()))
_TN = (((0,), (0,)), ((), ()))


def _split2(x):
    hi = x.astype(BF16)
    lo = (x - hi.astype(F32)).astype(BF16)
    return hi, lo


def _split3(x):
    hi = x.astype(BF16)
    r1 = x - hi.astype(F32)
    mid = r1.astype(BF16)
    lo = (r1 - mid.astype(F32)).astype(BF16)
    return hi, mid, lo


def _mm(a, b, dims=_NN, passes=1):
    d = lambda p, q: lax.dot_general(p, q, dims, preferred_element_type=F32)
    if passes == 1:
        return d(a.astype(BF16), b.astype(BF16))
    ah, al = _split2(a)
    bh, bl = _split2(b)
    return d(ah, bh) + (d(ah, bl) + d(al, bh))


def _mm_exact_rhs(a, b_bf16, dims=_NN):
    d = lambda p: lax.dot_general(p, b_bf16, dims, preferred_element_type=F32)
    hi, mid, lo = _split3(a)
    return d(hi) + (d(mid) + d(lo))


def _sigmoid(x):
    return 1.0 / (1.0 + jnp.exp(-x))


def _softplus(x):
    return jnp.maximum(x, 0.0) + jnp.log(1.0 + jnp.exp(-jnp.abs(x)))


def _norm_matmul_kernel(x_ref, g_ref, w_ref, o_ref, h_ref):
    @pl.when(pl.program_id(1) == 0)
    def _():
        x = x_ref[...]
        ms = jnp.mean(x * x, axis=-1, keepdims=True)
        h_ref[...] = (x * lax.rsqrt(ms + NORM_EPS) * g_ref[...]).astype(BF16)

    o_ref[...] = jnp.dot(h_ref[...], w_ref[...], preferred_element_type=F32)


def _norm_matmul(x2, gain, w, tm=512, tn=512):
    m, d = x2.shape
    n = w.shape[1]
    return pl.pallas_call(
        _norm_matmul_kernel,
        grid=(m // tm, n // tn),
        in_specs=[pl.BlockSpec((tm, d), lambda i, j: (i, 0)),
                  pl.BlockSpec((1, d), lambda i, j: (0, 0)),
                  pl.BlockSpec((d, tn), lambda i, j: (0, j))],
        out_specs=pl.BlockSpec((tm, tn), lambda i, j: (i, j)),
        out_shape=jax.ShapeDtypeStruct((m, n), F32),
        scratch_shapes=[pltpu.VMEM((tm, d), BF16)],
        compiler_params=_cparams("parallel", "arbitrary"),
        name="norm_matmul",
    )(x2, gain.reshape(1, d), w)


def _shifted(cur, halo, mu, first):
    rows = lax.broadcasted_iota(jnp.int32, cur.shape, 0)
    last = jnp.where(first, 0.0, halo[7:8, :])
    prev = jnp.where(rows == 0, last, pltpu.roll(cur, 1, axis=0))
    return cur + (prev - cur) * mu


def _rwkv_prep_kernel(has_vres, *refs):
    if has_vres:
        (zm_ref, zmh_ref, zl_ref, zlh_ref, mum_ref, mul_ref, w0_ref, w2_ref, a0_ref, a2_ref, g2_ref,
         kk_ref, ka_ref, hsum_ref, hbc_ref, vf_ref, v0_ref, v2_ref,
         r_ref, lw_ref, k_ref, v_ref, an_ref, b_ref, g_ref) = refs
    else:
        (zm_ref, zmh_ref, zl_ref, zlh_ref, mum_ref, mul_ref, w0_ref, w2_ref, a0_ref, a2_ref, g2_ref,
         kk_ref, ka_ref, hsum_ref, hbc_ref,
         r_ref, lw_ref, k_ref, v_ref, an_ref, b_ref, g_ref) = refs
    c = r_ref.shape[-1]
    first = pl.program_id(1) == 0
    zs = _shifted(zm_ref[0], zmh_ref[0], mum_ref[...], first)
    zl = _shifted(zl_ref[0], zlh_ref[0], mul_ref[...], first)
    r, k, v = zs[:, :c], zs[:, c:2 * c], zs[:, 2 * c:3 * c]
    o = 0
    wd = zl[:, o:o + DECAY_LORA]
    o += DECAY_LORA
    ad = zl[:, o:o + ICLR_LORA]
    o += ICLR_LORA
    gd = zl[:, o:o + GATE_LORA]
    o += GATE_LORA
    w_log = -_softplus(-(w0_ref[...] + _mm(jnp.tanh(wd), w2_ref[...], passes=3))) - 0.5
    lw_ref[0] = -jnp.exp(w_log)
    a = _sigmoid(a0_ref[...] + _mm(ad, a2_ref[...], passes=3))
    g_ref[0] = _mm(_sigmoid(gd), g2_ref[...], passes=3)
    if has_vres:
        vd = zl[:, o:o + VRES_LORA]
        v = v + (vf_ref[0] - v) * _sigmoid(v0_ref[...] + _mm(vd, v2_ref[...], passes=3))
    kk = k * kk_ref[...]
    ss = _mm_exact_rhs(kk * kk, hsum_ref[...])
    inv = 1.0 / jnp.maximum(jnp.sqrt(ss), 1e-12)
    kk = kk * _mm_exact_rhs(inv, hbc_ref[...])
    r_ref[0] = r
    k_ref[0] = k * (1.0 + (a - 1.0) * ka_ref[...])
    v_ref[0] = v
    an_ref[0] = -kk
    b_ref[0] = kk * a


def _rwkv_prep(z3, c, lora_col, mu_main, mu_lora, w0, w2, a0, a2, g2, k_k, k_a, hsum, hbc, vres, tm=256):
    bsz, s, _ = z3.shape
    has_vres = vres is not None
    row = lambda a: a.reshape(1, -1)
    hb = tm // 8
    lcb = lora_col // LORA_PAD
    halo = lambda b, i: (b, jnp.maximum(i * hb - 1, 0), 0)
    halo_l = lambda b, i: (b, jnp.maximum(i * hb - 1, 0), lcb)
    full = lambda a: pl.BlockSpec(a.shape, lambda b, i: (0,) * a.ndim)
    ins = [z3, z3, z3, z3, row(mu_main), row(mu_lora), row(w0), w2, row(a0), a2, g2, row(k_k), row(k_a),
           hsum, hbc]
    in_specs = [pl.BlockSpec((1, tm, 3 * c), lambda b, i: (b, i, 0)),
                pl.BlockSpec((1, 8, 3 * c), halo),
                pl.BlockSpec((1, tm, LORA_PAD), lambda b, i: (b, i, lcb)),
                pl.BlockSpec((1, 8, LORA_PAD), halo_l)] + [full(a) for a in ins[4:]]
    if has_vres:
        v_first, v0, v2 = vres
        extra = [v_first, row(v0), v2]
        ins += extra
        in_specs += [pl.BlockSpec((1, tm, c), lambda b, i: (b, i, 0)), full(extra[1]), full(extra[2])]
    out_spec = pl.BlockSpec((1, tm, c), lambda b, i: (b, i, 0))
    out_sd = jax.ShapeDtypeStruct((bsz, s, c), F32)
    return pl.pallas_call(
        functools.partial(_rwkv_prep_kernel, has_vres),
        grid=(bsz, s // tm),
        in_specs=in_specs,
        out_specs=[out_spec] * 7,
        out_shape=[out_sd] * 7,
        compiler_params=_cparams("parallel", "arbitrary"),
        name="rwkv_prep",
    )(*ins)


def _mm_heads(a_list, b_list, dims=_NN, passes=3):
    d = lambda p, q: lax.dot_general(p, q, dims, preferred_element_type=F32)
    if passes == 1:
        return [d(a.astype(BF16), b.astype(BF16)) for a, b in zip(a_list, b_list)]
    sa = [_split2(a) for a in a_list]
    sb = [_split2(b) for b in b_list]
    out = [d(x[0], y[0]) for x, y in zip(sa, sb)]
    out = [o + d(x[0], y[1]) for o, x, y in zip(out, sa, sb)]
    return [o + d(x[1], y[0]) for o, x, y in zip(out, sa, sb)]


def _unit_lower_inverse(a_list, rows, cols):
    n = rows.shape[0]
    lower = rows > cols
    base = lower & ((rows >> 1) == (cols >> 1))
    t = [jnp.where(rows == cols, 1.0, jnp.where(base, a, 0.0)) for a in a_list]
    sh = 1
    while (2 << sh) <= n:
        sub = lower & ((rows >> (sh + 1)) == (cols >> (sh + 1))) & ((rows >> sh) != (cols >> sh))
        off = [jnp.where(sub, a, 0.0) for a in a_list]
        upd = _mm_heads(_mm_heads(t, off, passes=P_INV), t, passes=P_INV)
        t = [x + u for x, u in zip(t, upd)]
        sh += 1
    return t


def _wkv_kernel(r_ref, lw_ref, k_ref, v_ref, an_ref, b_ref, y_ref, state_ref):
    ln = r_ref.shape[1]
    hg = r_ref.shape[2] // RWKV_HEAD
    n = RWKV_HEAD

    @pl.when(pl.program_id(2) == 0)
    def _():
        state_ref[...] = jnp.zeros_like(state_ref)

    rows = lax.broadcasted_iota(jnp.int32, (ln, ln), 0)
    cols = lax.broadcasted_iota(jnp.int32, (ln, ln), 1)
    tril = jnp.where(rows >= cols, 1.0, 0.0).astype(BF16)
    lw = lw_ref[0]
    cw = _mm_exact_rhs_left(tril, lw)
    cw_end = cw[ln - 1:ln, :]
    e_neg = jnp.exp(-cw)
    w_end = jnp.exp(cw_end)
    r_t = r_ref[0] * jnp.exp(cw)
    a_t = an_ref[0] * jnp.exp(cw - lw)
    b_t = b_ref[0] * e_neg
    k_t = k_ref[0] * e_neg
    e_end = jnp.exp(cw_end - cw)
    b_h = b_ref[0] * e_end
    k_h = k_ref[0] * e_end
    v = v_ref[0]
    eye_n = lax.broadcasted_iota(jnp.int32, (n, n), 0) == lax.broadcasted_iota(jnp.int32, (n, n), 1)
    zeros = jnp.zeros((ln, n), F32)
    sls = [slice(h * n, (h + 1) * n) for h in range(hg)]

    ar = [jnp.concatenate([a_t[:, s], r_t[:, s]], axis=0) for s in sls]
    bk = [jnp.concatenate([b_t[:, s], k_t[:, s]], axis=0) for s in sls]
    amat = _mm_heads(ar, bk, _NT, passes=P_A)
    strict = rows > cols
    incl = rows >= cols
    a_ab = [jnp.where(strict, m[:ln, :ln], 0.0) for m in amat]
    a_ak = [jnp.where(strict, m[:ln, ln:], 0.0) for m in amat]
    a_r = [jnp.concatenate([jnp.where(incl, m[ln:, :ln], 0.0), jnp.where(incl, m[ln:, ln:], 0.0)], axis=1)
           for m in amat]
    vh = [v[:, s] for s in sls]
    akv = _mm_heads(a_ak, vh, passes=P_PQ)
    t = _unit_lower_inverse(a_ab, rows, cols)
    rhs = [jnp.concatenate([a_t[:, s], x], axis=1) for s, x in zip(sls, akv)]
    pq = _mm_heads(t, rhs, passes=P_PQ)
    pqv = [jnp.concatenate([x, jnp.concatenate([zeros, u], axis=1)], axis=0)
           for x, u in zip(pq, vh)]
    ry = _mm_heads(a_r, pqv, passes=P_OUT)
    bkh = [jnp.concatenate([b_h[:, s], k_h[:, s]], axis=0) for s in sls]
    mn = _mm_heads(bkh, pqv, _TN, passes=P_OUT)
    lhs = [jnp.concatenate([r_t[:, s] + x[:, :n],
                            jnp.where(eye_n, jnp.broadcast_to(w_end[:, s], (n, n)), 0.0) + m[:, :n]], axis=0)
           for s, x, m in zip(sls, ry, mn)]
    st = [state_ref[h] for h in range(hg)]
    upd = _mm_heads(lhs, st, passes=P_STATE)
    for h in range(hg):
        state_ref[h] = upd[h][ln:, :] + mn[h][:, n:]
    y_ref[0] = jnp.concatenate([u[:ln, :] + x[:, n:] for u, x in zip(upd, ry)], axis=1)


def _mm_exact_rhs_left(l_bf16, a):
    d = lambda p: lax.dot_general(l_bf16, p, _NN, preferred_element_type=F32)
    hi, mid, lo = _split3(a)
    return d(hi) + (d(mid) + d(lo))


def _wkv(r, lw, k, v, an, b, heads_per_step=16):
    bsz, s, c = r.shape
    wb = heads_per_step * RWKV_HEAD
    spec = pl.BlockSpec((1, WKV_CHUNK, wb), lambda bi, hi, ci: (bi, ci, hi))
    return pl.pallas_call(
        _wkv_kernel,
        grid=(bsz, c // wb, s // WKV_CHUNK),
        in_specs=[spec] * 6,
        out_specs=spec,
        out_shape=jax.ShapeDtypeStruct((bsz, s, c), F32),
        scratch_shapes=[pltpu.VMEM((heads_per_step, RWKV_HEAD, RWKV_HEAD), F32)],
        compiler_params=_cparams("parallel", "parallel", "arbitrary"),
        name="wkv",
    )(r, lw, k, v, an, b)


def _rwkv_post_kernel(y_ref, r_ref, k_ref, v_ref, g_ref, rk_ref, lg_ref, lb_ref, hsum_ref, hbc_ref, o_ref):
    hsum = hsum_ref[...]
    hbc = hbc_ref[...]
    y = y_ref[0]
    inv_n = 1.0 / RWKV_HEAD
    mu = _mm_exact_rhs(_mm_exact_rhs(y, hsum) * inv_n, hbc)
    d = y - mu
    var = _mm_exact_rhs(d * d, hsum) * inv_n
    yn = d * _mm_exact_rhs(lax.rsqrt(var + LNX_EPS), hbc)
    yn = yn * lg_ref[...] + lb_ref[...]
    bonus = _mm_exact_rhs(_mm_exact_rhs(r_ref[0] * k_ref[0] * rk_ref[...], hsum), hbc) * v_ref[0]
    o_ref[0] = ((yn + bonus) * g_ref[0]).astype(o_ref.dtype)


def _rwkv_post(y, r, k, v, g, r_k, lnx_g, lnx_b, hsum, hbc, tm=256):
    bsz, s, c = y.shape
    row = lambda a: a.reshape(1, -1)
    full = lambda a: pl.BlockSpec(a.shape, lambda b, i: (0,) * a.ndim)
    spec = pl.BlockSpec((1, tm, c), lambda b, i: (b, i, 0))
    small = [row(r_k), row(lnx_g), row(lnx_b), hsum, hbc]
    return pl.pallas_call(
        _rwkv_post_kernel,
        grid=(bsz, s // tm),
        in_specs=[spec] * 5 + [full(a) for a in small],
        out_specs=spec,
        out_shape=jax.ShapeDtypeStruct((bsz, s, c), BF16),
        compiler_params=_cparams("parallel", "parallel"),
        name="rwkv_post",
    )(y, r, k, v, g, *small)


def _rope_kernel(q_ref, k_ref, v_ref, cos_ref, sin_ref, qt_ref, ko_ref, vt_ref, km_ref):
    cos = cos_ref[...]
    sin = sin_ref[...]
    nh = q_ref.shape[-1] // ATTN_HEAD
    inv_rows = 1.0 / q_ref.shape[1]
    for h in range(nh):
        sl = slice(h * ATTN_HEAD, (h + 1) * ATTN_HEAD)
        q = q_ref[0, :, sl]
        k = k_ref[0, :, sl]
        qr = q * cos + pltpu.roll(q, ATTN_HEAD // 2, axis=1) * sin
        kr = k * cos + pltpu.roll(k, ATTN_HEAD // 2, axis=1) * sin
        qt_ref[0, h, 0] = qr.T
        ko_ref[0, h, 0] = kr.astype(BF16)
        vt_ref[0, h, 0] = v_ref[0, :, sl].T.astype(BF16)
        km_ref[0, 0, :, sl] = jnp.sum(kr, axis=0, keepdims=True) * inv_rows


def _rope(z3, col0, width, cos2, sin2):
    bsz, s, _ = z3.shape
    nb = s // MOBA_BLOCK
    nh = width // ATTN_HEAD
    cb = col0 // width
    blk = lambda j: pl.BlockSpec((1, MOBA_BLOCK, width), lambda b, i: (b, i, cb + j))
    tab = pl.BlockSpec((MOBA_BLOCK, ATTN_HEAD), lambda b, i: (i, 0))
    t_spec = pl.BlockSpec((1, nh, 1, ATTN_HEAD, MOBA_BLOCK), lambda b, i: (b, 0, i, 0, 0))
    n_spec = pl.BlockSpec((1, nh, 1, MOBA_BLOCK, ATTN_HEAD), lambda b, i: (b, 0, i, 0, 0))
    return pl.pallas_call(
        _rope_kernel,
        grid=(bsz, nb),
        in_specs=[blk(0), blk(1), blk(2), tab, tab],
        out_specs=[t_spec, n_spec, t_spec, pl.BlockSpec((1, 1, 1, width), lambda b, i: (b, i, 0, 0))],
        out_shape=[jax.ShapeDtypeStruct((bsz, nh, nb, ATTN_HEAD, MOBA_BLOCK), F32),
                   jax.ShapeDtypeStruct((bsz, nh, nb, MOBA_BLOCK, ATTN_HEAD), BF16),
                   jax.ShapeDtypeStruct((bsz, nh, nb, ATTN_HEAD, MOBA_BLOCK), BF16),
                   jax.ShapeDtypeStruct((bsz, nb, 1, width), F32)],
        compiler_params=_cparams("parallel", "parallel"),
        name="rope",
    )(z3, z3, z3, cos2, sin2)


def _moba_kernel(qt_ref, k_ref, vt_ref, km_ref, o_ref, bias_ref):
    blk = MOBA_BLOCK
    dh = ATTN_HEAD
    hs = range(qt_ref.shape[1])
    qb = pl.program_id(2)
    nb = km_ref.shape[1]
    scale = ATTN_HEAD ** -0.5
    neg = -jnp.inf
    qt = [qt_ref[0, h, 0] for h in hs]
    blk_id = lax.broadcasted_iota(jnp.int32, (nb, blk), 0)
    past = blk_id < qb
    gate = [jnp.where(past, _mm(km_ref[0, :, h * dh:(h + 1) * dh], qt[h], passes=3), neg) for h in hs]
    for h in hs:
        rank = jnp.zeros((nb, blk), jnp.int32)
        for m in range(nb):
            gm = gate[h][m:m + 1, :]
            rank += ((gm > gate[h]) | ((gm == gate[h]) & (m < blk_id))).astype(jnp.int32)
        bias_ref[h] = jnp.where(past & (rank < MOBA_TOPK), 0.0, neg)

    qs = [(q * scale).astype(BF16) for q in qt]
    ki = lax.broadcasted_iota(jnp.int32, (blk, blk), 0)
    qi = lax.broadcasted_iota(jnp.int32, (blk, blk), 1)
    s_own = [jnp.where(ki <= qi, jnp.dot(k_ref[0, h, qb], qs[h], preferred_element_type=F32), neg) for h in hs]
    m0 = [jnp.max(s, axis=0, keepdims=True) for s in s_own]
    p0 = [jnp.exp(s - m) for s, m in zip(s_own, m0)]
    l0 = [jnp.sum(p, axis=0, keepdims=True) for p in p0]
    acc0 = [jnp.dot(vt_ref[0, h, qb], p0[h].astype(BF16), preferred_element_type=F32) for h in hs]

    def body(kb, carry):
        m_prev, l_prev, acc = carry
        s = [jnp.dot(k_ref[0, h, kb], qs[h], preferred_element_type=F32) + bias_ref[h, pl.ds(kb, 1), :] for h in hs]
        m_new = [jnp.maximum(mp, jnp.max(x, axis=0, keepdims=True)) for mp, x in zip(m_prev, s)]
        alpha = [jnp.exp(mp - mn) for mp, mn in zip(m_prev, m_new)]
        p = [jnp.exp(x - mn) for x, mn in zip(s, m_new)]
        l_new = [a * lp + jnp.sum(x, axis=0, keepdims=True) for a, lp, x in zip(alpha, l_prev, p)]
        pv = [jnp.dot(vt_ref[0, h, kb], p[h].astype(BF16), preferred_element_type=F32) for h in hs]
        acc = [a * c + x for a, c, x in zip(alpha, acc, pv)]
        return m_new, l_new, acc

    _, l_fin, acc = lax.fori_loop(0, qb, body, (m0, l0, acc0))
    for h in hs:
        o_ref[0, :, h * dh:(h + 1) * dh] = (acc[h] / l_fin[h]).T.astype(o_ref.dtype)


def _moba(qt, k, vt, kmean, heads_per_step=8):
    bsz, nh, nb, dh, blk = qt.shape
    hp = heads_per_step
    return pl.pallas_call(
        _moba_kernel,
        grid=(bsz, nh // hp, nb),
        in_specs=[pl.BlockSpec((1, hp, 1, dh, blk), lambda b, h, i: (b, h, i, 0, 0)),
                  pl.BlockSpec((1, hp, nb, blk, dh), lambda b, h, i: (b, h, 0, 0, 0)),
                  pl.BlockSpec((1, hp, nb, dh, blk), lambda b, h, i: (b, h, 0, 0, 0)),
                  pl.BlockSpec((1, nb, hp * dh), lambda b, h, i: (b, 0, h))],
        out_specs=pl.BlockSpec((1, blk, hp * dh), lambda b, h, i: (b, i, h)),
        out_shape=jax.ShapeDtypeStruct((bsz, nb * blk, nh * dh), BF16),
        scratch_shapes=[pltpu.VMEM((hp, nb, blk), F32)],
        compiler_params=_cparams("parallel", "parallel", "arbitrary"),
        name="moba",
    )(qt, k, vt, kmean)


def _out_proj_kernel(yr_ref, ya_ref, wr_ref, wa_ref, x_ref, g_ref, o_ref):
    y = jnp.dot(yr_ref[...], wr_ref[...], preferred_element_type=F32)
    y += jnp.dot(ya_ref[...], wa_ref[...], preferred_element_type=F32)
    ms = jnp.mean(y * y, axis=-1, keepdims=True)
    o_ref[...] = x_ref[...] + y * lax.rsqrt(ms + NORM_EPS) * g_ref[...]


def _out_proj(y_r, y_a, w_r, w_a, x2, gain, tm=256):
    m, d = x2.shape
    cr, ca = y_r.shape[1], y_a.shape[1]
    const = lambda shape: pl.BlockSpec(shape, lambda i: (0, 0))
    return pl.pallas_call(
        _out_proj_kernel,
        grid=(m // tm,),
        in_specs=[pl.BlockSpec((tm, cr), lambda i: (i, 0)), pl.BlockSpec((tm, ca), lambda i: (i, 0)),
                  const((cr, d)), const((ca, d)), pl.BlockSpec((tm, d), lambda i: (i, 0)), const((1, d))],
        out_specs=pl.BlockSpec((tm, d), lambda i: (i, 0)),
        out_shape=jax.ShapeDtypeStruct((m, d), F32),
        compiler_params=_cparams("parallel"),
        name="out_proj",
    )(y_r, y_a, w_r, w_a, x2, gain.reshape(1, d))


def _mlp_kernel(x_ref, gpre_ref, wu_ref, wd_ref, gpost_ref, o_ref, h_ref, acc_ref):
    f = pl.program_id(1)

    @pl.when(f == 0)
    def _():
        x = x_ref[...]
        ms = jnp.mean(x * x, axis=-1, keepdims=True)
        h_ref[...] = (x * lax.rsqrt(ms + NORM_EPS) * gpre_ref[...]).astype(BF16)
        acc_ref[...] = jnp.zeros_like(acc_ref)

    u = jnp.maximum(jnp.dot(h_ref[...], wu_ref[...], preferred_element_type=F32), 0.0)
    acc_ref[...] += jnp.dot((u * u).astype(BF16), wd_ref[...], preferred_element_type=F32)

    @pl.when(f == pl.num_programs(1) - 1)
    def _():
        mlp = acc_ref[...]
        ms = jnp.mean(mlp * mlp, axis=-1, keepdims=True)
        o_ref[...] = x_ref[...] + mlp * lax.rsqrt(ms + NORM_EPS) * gpost_ref[...]


def _mlp(x2, g_pre, w_up, w_down, g_post, tm=512, tf=512):
    m, d = x2.shape
    dff = w_up.shape[1]
    return pl.pallas_call(
        _mlp_kernel,
        grid=(m // tm, dff // tf),
        in_specs=[pl.BlockSpec((tm, d), lambda i, f: (i, 0)),
                  pl.BlockSpec((1, d), lambda i, f: (0, 0)),
                  pl.BlockSpec((d, tf), lambda i, f: (0, f)),
                  pl.BlockSpec((tf, d), lambda i, f: (f, 0)),
                  pl.BlockSpec((1, d), lambda i, f: (0, 0))],
        out_specs=pl.BlockSpec((tm, d), lambda i, f: (i, 0)),
        out_shape=jax.ShapeDtypeStruct((m, d), F32),
        scratch_shapes=[pltpu.VMEM((tm, d), BF16), pltpu.VMEM((tm, d), F32)],
        compiler_params=_cparams("parallel", "arbitrary"),
        name="mlp",
    )(x2, g_pre.reshape(1, d), w_up, w_down, g_post.reshape(1, d))


def _head_indicators(c):
    head = jnp.arange(c) // RWKV_HEAD
    hsum = (head[:, None] == jnp.arange(128)[None, :]).astype(BF16)
    return hsum, hsum.T


def _rope_tables(s):
    half = ATTN_HEAD // 2
    inv_freq = ROPE_THETA ** (-jnp.arange(half, dtype=F32) / half)
    ang = jnp.arange(s).astype(F32)[:, None] * inv_freq[None, :]
    cos, sin = jnp.cos(ang), jnp.sin(ang)
    return jnp.concatenate([cos, cos], axis=-1), jnp.concatenate([-sin, sin], axis=-1)


def kernel(x, norm_mix_pre, norm_mix_post, norm_mlp_pre, norm_mlp_post, w_in, w_in_vres, shift_mu, shift_mu_vres, decay_w0, decay_w2, iclr_a0, iclr_a2, vres_v0, vres_v2, gate_g2, k_k, k_a, r_k, lnx_gain, lnx_bias, w_out, w_up, w_down):
    bsz, s, d = x.shape
    depth = w_in.shape[0]
    c = decay_w0.shape[1]
    n_lora = DECAY_LORA + ICLR_LORA + GATE_LORA
    n_shift = 3 * c + n_lora
    ca = (w_in.shape[2] - n_shift) // 3
    hsum, hbc = _head_indicators(c)
    cos2, sin2 = _rope_tables(s)
    x2 = x.reshape(bsz * s, d)
    v_first = None
    for i in range(depth):
        lora_w = [w_in[i][:, 3 * c:n_shift]]
        lora_mu = [shift_mu[i][3 * c:]]
        n_used = n_lora
        if i > 0:
            lora_w.append(w_in_vres[i - 1])
            lora_mu.append(shift_mu_vres[i - 1])
            n_used += VRES_LORA
        lora_w.append(jnp.zeros((d, LORA_PAD - n_used), F32))
        lora_mu.append(jnp.zeros((LORA_PAD - n_used,), F32))
        w_comb = jnp.concatenate([w_in[i][:, :3 * c], w_in[i][:, n_shift:]] + lora_w, axis=1).astype(BF16)
        z = _norm_matmul(x2, norm_mix_pre[i], w_comb)
        z3 = z.reshape(bsz, s, -1)

        vres = None if i == 0 else (v_first, vres_v0[i - 1], vres_v2[i - 1])
        r, lw, k, v, an, b, g = _rwkv_prep(z3, c, 3 * c + 3 * ca, shift_mu[i][:3 * c],
                                           jnp.concatenate(lora_mu), decay_w0[i],
                                           decay_w2[i], iclr_a0[i], iclr_a2[i], gate_g2[i], k_k[i], k_a[i],
                                           hsum, hbc, vres)
        if i == 0:
            v_first = v
        y = _wkv(r, lw, k, v, an, b)
        y_r = _rwkv_post(y, r, k, v, g, r_k[i].reshape(-1), lnx_gain[i], lnx_bias[i], hsum, hbc)

        q_t, k_rot, v_t, kmean = _rope(z3, 3 * c, ca, cos2, sin2)
        y_a = _moba(q_t, k_rot, v_t, kmean.reshape(bsz, -1, ca))

        w_o = w_out[i].astype(BF16)
        x2 = _out_proj(y_r.reshape(bsz * s, c), y_a.reshape(bsz * s, ca), w_o[:c], w_o[c:], x2,
                       norm_mix_post[i])
        x2 = _mlp(x2, norm_mlp_pre[i], w_up[i].astype(BF16), w_down[i].astype(BF16), norm_mlp_post[i])
    return x2.reshape(bsz, s, d)
```

```python
import functools

import jax
import jax.numpy as jnp
from jax import lax
from jax.experimental import pallas as pl
from jax.experimental.pallas import tpu as pltpu

F32 = jnp.float32
BF16 = jnp.bfloat16

RWKV_HEAD = 64
DECAY_LORA = 64
ICLR_LORA = 64
VRES_LORA = 32
GATE_LORA = 128
ATTN_HEAD = 128
MOBA_BLOCK = 256
MOBA_TOPK = 3
ROPE_THETA = 10000.0
NORM_EPS = 1e-6
LNX_EPS = 64e-5

LORA_PAD = 512
WKV_CHUNK = 64
P_A, P_INV, P_PQ, P_OUT, P_STATE = 1, 1, 1, 1, 1
VMEM_LIMIT = 56 * 1024 * 1024


def _cparams(*sem):
    return pltpu.CompilerParams(dimension_semantics=sem, vmem_limit_bytes=VMEM_LIMIT)


_NN = (((1,), (0,)), ((), ()))
_NT = (((1,), (1,)), ((), ()))
_TN = (((0,), (0,)), ((), ()))


def _split2(x):
    hi = x.astype(BF16)
    lo = (x - hi.astype(F32)).astype(BF16)
    return hi, lo


def _split3(x):
    hi = x.astype(BF16)
    r1 = x - hi.astype(F32)
    mid = r1.astype(BF16)
    lo = (r1 - mid.astype(F32)).astype(BF16)
    return hi, mid, lo


def _mm(a, b, dims=_NN, passes=1):
    d = lambda p, q: lax.dot_general(p, q, dims, preferred_element_type=F32)
    if passes == 1:
        return d(a.astype(BF16), b.astype(BF16))
    ah, al = _split2(a)
    bh, bl = _split2(b)
    return d(ah, bh) + (d(ah, bl) + d(al, bh))


def _mm_exact_rhs(a, b_bf16, dims=_NN):
    d = lambda p: lax.dot_general(p, b_bf16, dims, preferred_element_type=F32)
    hi, mid, lo = _split3(a)
    return d(hi) + (d(mid) + d(lo))


def _sigmoid(x):
    return 1.0 / (1.0 + jnp.exp(-x))


def _softplus(x):
    return jnp.maximum(x, 0.0) + jnp.log(1.0 + jnp.exp(-jnp.abs(x)))


def _norm_matmul_kernel(x_ref, g_ref, w_ref, o_ref, h_ref):
    @pl.when(pl.program_id(1) == 0)
    def _():
        x = x_ref[...]
        ms = jnp.mean(x * x, axis=-1, keepdims=True)
        h_ref[...] = (x * lax.rsqrt(ms + NORM_EPS) * g_ref[...]).astype(BF16)

    o_ref[...] = jnp.dot(h_ref[...], w_ref[...], preferred_element_type=F32)


def _norm_matmul(x2, gain, w, tm=1024, tn=512):
    m, d = x2.shape
    n = w.shape[1]
    return pl.pallas_call(
        _norm_matmul_kernel,
        grid=(m // tm, n // tn),
        in_specs=[pl.BlockSpec((tm, d), lambda i, j: (i, 0)),
                  pl.BlockSpec((1, d), lambda i, j: (0, 0)),
                  pl.BlockSpec((d, tn), lambda i, j: (0, j))],
        out_specs=pl.BlockSpec((tm, tn), lambda i, j: (i, j)),
        out_shape=jax.ShapeDtypeStruct((m, n), F32),
        scratch_shapes=[pltpu.VMEM((tm, d), BF16)],
        compiler_params=_cparams("parallel", "arbitrary"),
        name="norm_matmul",
    )(x2, gain.reshape(1, d), w)


def _shifted(cur, halo, mu, first):
    rows = lax.broadcasted_iota(jnp.int32, cur.shape, 0)
    last = jnp.where(first, 0.0, halo[7:8, :])
    prev = jnp.where(rows == 0, last, pltpu.roll(cur, 1, axis=0))
    return cur + (prev - cur) * mu


def _rwkv_prep_kernel(has_vres, *refs):
    if has_vres:
        (zm_ref, zmh_ref, zl_ref, zlh_ref, mum_ref, mul_ref, w0_ref, w2_ref, a0_ref, a2_ref, g2_ref,
         kk_ref, ka_ref, hsum_ref, hbc_ref, vf_ref, v0_ref, v2_ref,
         r_ref, lw_ref, k_ref, v_ref, an_ref, b_ref, g_ref) = refs
    else:
        (zm_ref, zmh_ref, zl_ref, zlh_ref, mum_ref, mul_ref, w0_ref, w2_ref, a0_ref, a2_ref, g2_ref,
         kk_ref, ka_ref, hsum_ref, hbc_ref,
         r_ref, lw_ref, k_ref, v_ref, an_ref, b_ref, g_ref) = refs
    c = r_ref.shape[-1]
    first = pl.program_id(1) == 0
    zs = _shifted(zm_ref[0], zmh_ref[0], mum_ref[...], first)
    zl = _shifted(zl_ref[0], zlh_ref[0], mul_ref[...], first)
    r, k, v = zs[:, :c], zs[:, c:2 * c], zs[:, 2 * c:3 * c]
    o = 0
    wd = zl[:, o:o + DECAY_LORA]
    o += DECAY_LORA
    ad = zl[:, o:o + ICLR_LORA]
    o += ICLR_LORA
    gd = zl[:, o:o + GATE_LORA]
    o += GATE_LORA
    w_log = -_softplus(-(w0_ref[...] + _mm(jnp.tanh(wd), w2_ref[...], passes=3))) - 0.5
    lw_ref[0] = -jnp.exp(w_log)
    a = _sigmoid(a0_ref[...] + _mm(ad, a2_ref[...], passes=3))
    g_ref[0] = _mm(_sigmoid(gd), g2_ref[...], passes=3)
    if has_vres:
        vd = zl[:, o:o + VRES_LORA]
        v = v + (vf_ref[0] - v) * _sigmoid(v0_ref[...] + _mm(vd, v2_ref[...], passes=3))
    kk = k * kk_ref[...]
    ss = _mm_exact_rhs(kk * kk, hsum_ref[...])
    inv = 1.0 / jnp.maximum(jnp.sqrt(ss), 1e-12)
    kk = kk * _mm_exact_rhs(inv, hbc_ref[...])
    r_ref[0] = r
    k_ref[0] = k * (1.0 + (a - 1.0) * ka_ref[...])
    v_ref[0] = v
    an_ref[0] = -kk
    b_ref[0] = kk * a


def _rwkv_prep(z3, c, lora_col, mu_main, mu_lora, w0, w2, a0, a2, g2, k_k, k_a, hsum, hbc, vres, tm=256):
    bsz, s, _ = z3.shape
    has_vres = vres is not None
    row = lambda a: a.reshape(1, -1)
    hb = tm // 8
    lcb = lora_col // LORA_PAD
    halo = lambda b, i: (b, jnp.maximum(i * hb - 1, 0), 0)
    halo_l = lambda b, i: (b, jnp.maximum(i * hb - 1, 0), lcb)
    full = lambda a: pl.BlockSpec(a.shape, lambda b, i: (0,) * a.ndim)
    ins = [z3, z3, z3, z3, row(mu_main), row(mu_lora), row(w0), w2, row(a0), a2, g2, row(k_k), row(k_a),
           hsum, hbc]
    in_specs = [pl.BlockSpec((1, tm, 3 * c), lambda b, i: (b, i, 0)),
                pl.BlockSpec((1, 8, 3 * c), halo),
                pl.BlockSpec((1, tm, LORA_PAD), lambda b, i: (b, i, lcb)),
                pl.BlockSpec((1, 8, LORA_PAD), halo_l)] + [full(a) for a in ins[4:]]
    if has_vres:
        v_first, v0, v2 = vres
        extra = [v_first, row(v0), v2]
        ins += extra
        in_specs += [pl.BlockSpec((1, tm, c), lambda b, i: (b, i, 0)), full(extra[1]), full(extra[2])]
    out_spec = pl.BlockSpec((1, tm, c), lambda b, i: (b, i, 0))
    out_sd = jax.ShapeDtypeStruct((bsz, s, c), F32)
    return pl.pallas_call(
        functools.partial(_rwkv_prep_kernel, has_vres),
        grid=(bsz, s // tm),
        in_specs=in_specs,
        out_specs=[out_spec] * 7,
        out_shape=[out_sd] * 7,
        compiler_params=_cparams("parallel", "arbitrary"),
        name="rwkv_prep",
    )(*ins)


def _mm_heads(a_list, b_list, dims=_NN, passes=3):
    d = lambda p, q: lax.dot_general(p, q, dims, preferred_element_type=F32)
    if passes == 1:
        return [d(a.astype(BF16), b.astype(BF16)) for a, b in zip(a_list, b_list)]
    sa = [_split2(a) for a in a_list]
    sb = [_split2(b) for b in b_list]
    out = [d(x[0], y[0]) for x, y in zip(sa, sb)]
    out = [o + d(x[0], y[1]) for o, x, y in zip(out, sa, sb)]
    return [o + d(x[1], y[0]) for o, x, y in zip(out, sa, sb)]


def _unit_lower_inverse(a_list, rows, cols):
    n = rows.shape[0]
    lower = rows > cols
    base = lower & ((rows >> 1) == (cols >> 1))
    t = [jnp.where(rows == cols, 1.0, jnp.where(base, a, 0.0)) for a in a_list]
    sh = 1
    while (2 << sh) <= n:
        sub = lower & ((rows >> (sh + 1)) == (cols >> (sh + 1))) & ((rows >> sh) != (cols >> sh))
        off = [jnp.where(sub, a, 0.0) for a in a_list]
        upd = _mm_heads(_mm_heads(t, off, passes=P_INV), t, passes=P_INV)
        t = [x + u for x, u in zip(t, upd)]
        sh += 1
    return t


def _wkv_kernel(r_ref, lw_ref, k_ref, v_ref, an_ref, b_ref, y_ref, state_ref):
    ln = r_ref.shape[1]
    hg = r_ref.shape[2] // RWKV_HEAD
    n = RWKV_HEAD

    @pl.when(pl.program_id(2) == 0)
    def _():
        state_ref[...] = jnp.zeros_like(state_ref)

    rows = lax.broadcasted_iota(jnp.int32, (ln, ln), 0)
    cols = lax.broadcasted_iota(jnp.int32, (ln, ln), 1)
    tril = jnp.where(rows >= cols, 1.0, 0.0).astype(BF16)
    lw = lw_ref[0]
    cw = _mm_exact_rhs_left(tril, lw)
    cw_end = cw[ln - 1:ln, :]
    e_neg = jnp.exp(-cw)
    w_end = jnp.exp(cw_end)
    r_t = r_ref[0] * jnp.exp(cw)
    a_t = an_ref[0] * jnp.exp(cw - lw)
    b_t = b_ref[0] * e_neg
    k_t = k_ref[0] * e_neg
    e_end = jnp.exp(cw_end - cw)
    b_h = b_ref[0] * e_end
    k_h = k_ref[0] * e_end
    v = v_ref[0]
    eye_n = lax.broadcasted_iota(jnp.int32, (n, n), 0) == lax.broadcasted_iota(jnp.int32, (n, n), 1)
    zeros = jnp.zeros((ln, n), F32)
    sls = [slice(h * n, (h + 1) * n) for h in range(hg)]

    ar = [jnp.concatenate([a_t[:, s], r_t[:, s]], axis=0) for s in sls]
    bk = [jnp.concatenate([b_t[:, s], k_t[:, s]], axis=0) for s in sls]
    amat = _mm_heads(ar, bk, _NT, passes=P_A)
    strict = rows > cols
    incl = rows >= cols
    a_ab = [jnp.where(strict, m[:ln, :ln], 0.0) for m in amat]
    a_ak = [jnp.where(strict, m[:ln, ln:], 0.0) for m in amat]
    a_r = [jnp.concatenate([jnp.where(incl, m[ln:, :ln], 0.0), jnp.where(incl, m[ln:, ln:], 0.0)], axis=1)
           for m in amat]
    vh = [v[:, s] for s in sls]
    akv = _mm_heads(a_ak, vh, passes=P_PQ)
    t = _unit_lower_inverse(a_ab, rows, cols)
    rhs = [jnp.concatenate([a_t[:, s], x], axis=1) for s, x in zip(sls, akv)]
    pq = _mm_heads(t, rhs, passes=P_PQ)
    pqv = [jnp.concatenate([x, jnp.concatenate([zeros, u], axis=1)], axis=0)
           for x, u in zip(pq, vh)]
    ry = _mm_heads(a_r, pqv, passes=P_OUT)
    bkh = [jnp.concatenate([b_h[:, s], k_h[:, s]], axis=0) for s in sls]
    mn = _mm_heads(bkh, pqv, _TN, passes=P_OUT)
    lhs = [jnp.concatenate([r_t[:, s] + x[:, :n],
                            jnp.where(eye_n, jnp.broadcast_to(w_end[:, s], (n, n)), 0.0) + m[:, :n]], axis=0)
           for s, x, m in zip(sls, ry, mn)]
    st = [state_ref[h] for h in range(hg)]
    upd = _mm_heads(lhs, st, passes=P_STATE)
    for h in range(hg):
        state_ref[h] = upd[h][ln:, :] + mn[h][:, n:]
    y_ref[0] = jnp.concatenate([u[:ln, :] + x[:, n:] for u, x in zip(upd, ry)], axis=1)


def _mm_exact_rhs_left(l_bf16, a):
    d = lambda p: lax.dot_general(l_bf16, p, _NN, preferred_element_type=F32)
    hi, mid, lo = _split3(a)
    return d(hi) + (d(mid) + d(lo))


def _wkv(r, lw, k, v, an, b, heads_per_step=16):
    bsz, s, c = r.shape
    wb = heads_per_step * RWKV_HEAD
    spec = pl.BlockSpec((1, WKV_CHUNK, wb), lambda bi, hi, ci: (bi, ci, hi))
    return pl.pallas_call(
        _wkv_kernel,
        grid=(bsz, c // wb, s // WKV_CHUNK),
        in_specs=[spec] * 6,
        out_specs=spec,
        out_shape=jax.ShapeDtypeStruct((bsz, s, c), F32),
        scratch_shapes=[pltpu.VMEM((heads_per_step, RWKV_HEAD, RWKV_HEAD), F32)],
        compiler_params=_cparams("parallel", "parallel", "arbitrary"),
        name="wkv",
    )(r, lw, k, v, an, b)


def _rwkv_post_kernel(y_ref, r_ref, k_ref, v_ref, g_ref, rk_ref, lg_ref, lb_ref, hsum_ref, hbc_ref, o_ref):
    hsum = hsum_ref[...]
    hbc = hbc_ref[...]
    y = y_ref[0]
    inv_n = 1.0 / RWKV_HEAD
    mu = _mm_exact_rhs(_mm_exact_rhs(y, hsum) * inv_n, hbc)
    d = y - mu
    var = _mm_exact_rhs(d * d, hsum) * inv_n
    yn = d * _mm_exact_rhs(lax.rsqrt(var + LNX_EPS), hbc)
    yn = yn * lg_ref[...] + lb_ref[...]
    bonus = _mm_exact_rhs(_mm_exact_rhs(r_ref[0] * k_ref[0] * rk_ref[...], hsum), hbc) * v_ref[0]
    o_ref[0] = ((yn + bonus) * g_ref[0]).astype(o_ref.dtype)


def _rwkv_post(y, r, k, v, g, r_k, lnx_g, lnx_b, hsum, hbc, tm=256):
    bsz, s, c = y.shape
    row = lambda a: a.reshape(1, -1)
    full = lambda a: pl.BlockSpec(a.shape, lambda b, i: (0,) * a.ndim)
    spec = pl.BlockSpec((1, tm, c), lambda b, i: (b, i, 0))
    small = [row(r_k), row(lnx_g), row(lnx_b), hsum, hbc]
    return pl.pallas_call(
        _rwkv_post_kernel,
        grid=(bsz, s // tm),
        in_specs=[spec] * 5 + [full(a) for a in small],
        out_specs=spec,
        out_shape=jax.ShapeDtypeStruct((bsz, s, c), BF16),
        compiler_params=_cparams("parallel", "parallel"),
        name="rwkv_post",
    )(y, r, k, v, g, *small)


def _rope_kernel(q_ref, k_ref, v_ref, cos_ref, sin_ref, qt_ref, ko_ref, vt_ref, km_ref):
    cos = cos_ref[...]
    sin = sin_ref[...]
    nh = q_ref.shape[-1] // ATTN_HEAD
    inv_rows = 1.0 / q_ref.shape[1]
    for h in range(nh):
        sl = slice(h * ATTN_HEAD, (h + 1) * ATTN_HEAD)
        q = q_ref[0, :, sl]
        k = k_ref[0, :, sl]
        qr = q * cos + pltpu.roll(q, ATTN_HEAD // 2, axis=1) * sin
        kr = k * cos + pltpu.roll(k, ATTN_HEAD // 2, axis=1) * sin
        qt_ref[0, h, 0] = qr.T
        ko_ref[0, h, 0] = kr.astype(BF16)
        vt_ref[0, h, 0] = v_ref[0, :, sl].T.astype(BF16)
        km_ref[0, 0, :, sl] = jnp.sum(kr, axis=0, keepdims=True) * inv_rows


def _rope(z3, col0, width, cos2, sin2):
    bsz, s, _ = z3.shape
    nb = s // MOBA_BLOCK
    nh = width // ATTN_HEAD
    cb = col0 // width
    blk = lambda j: pl.BlockSpec((1, MOBA_BLOCK, width), lambda b, i: (b, i, cb + j))
    tab = pl.BlockSpec((MOBA_BLOCK, ATTN_HEAD), lambda b, i: (i, 0))
    t_spec = pl.BlockSpec((1, nh, 1, ATTN_HEAD, MOBA_BLOCK), lambda b, i: (b, 0, i, 0, 0))
    n_spec = pl.BlockSpec((1, nh, 1, MOBA_BLOCK, ATTN_HEAD), lambda b, i: (b, 0, i, 0, 0))
    return pl.pallas_call(
        _rope_kernel,
        grid=(bsz, nb),
        in_specs=[blk(0), blk(1), blk(2), tab, tab],
        out_specs=[t_spec, n_spec, t_spec, pl.BlockSpec((1, 1, 1, width), lambda b, i: (b, i, 0, 0))],
        out_shape=[jax.ShapeDtypeStruct((bsz, nh, nb, ATTN_HEAD, MOBA_BLOCK), F32),
                   jax.ShapeDtypeStruct((bsz, nh, nb, MOBA_BLOCK, ATTN_HEAD), BF16),
                   jax.ShapeDtypeStruct((bsz, nh, nb, ATTN_HEAD, MOBA_BLOCK), BF16),
                   jax.ShapeDtypeStruct((bsz, nb, 1, width), F32)],
        compiler_params=_cparams("parallel", "parallel"),
        name="rope",
    )(z3, z3, z3, cos2, sin2)


def _moba_kernel(qt_ref, k_ref, vt_ref, km_ref, o_ref, bias_ref):
    blk = MOBA_BLOCK
    dh = ATTN_HEAD
    hs = range(qt_ref.shape[1])
    qb = pl.program_id(2)
    nb = km_ref.shape[1]
    scale = ATTN_HEAD ** -0.5
    neg = -jnp.inf
    qt = [qt_ref[0, h, 0] for h in hs]
    blk_id = lax.broadcasted_iota(jnp.int32, (nb, blk), 0)
    past = blk_id < qb
    gate = [jnp.where(past, _mm(km_ref[0, :, h * dh:(h + 1) * dh], qt[h], passes=3), neg) for h in hs]
    for h in hs:
        rank = jnp.zeros((nb, blk), jnp.int32)
        for m in range(nb):
            gm = gate[h][m:m + 1, :]
            rank += ((gm > gate[h]) | ((gm == gate[h]) & (m < blk_id))).astype(jnp.int32)
        bias_ref[h] = jnp.where(past & (rank < MOBA_TOPK), 0.0, neg)

    qs = [(q * scale).astype(BF16) for q in qt]
    ki = lax.broadcasted_iota(jnp.int32, (blk, blk), 0)
    qi = lax.broadcasted_iota(jnp.int32, (blk, blk), 1)
    s_own = [jnp.where(ki <= qi, jnp.dot(k_ref[0, h, qb], qs[h], preferred_element_type=F32), neg) for h in hs]
    m0 = [jnp.max(s, axis=0, keepdims=True) for s in s_own]
    p0 = [jnp.exp(s - m) for s, m in zip(s_own, m0)]
    l0 = [jnp.sum(p, axis=0, keepdims=True) for p in p0]
    acc0 = [jnp.dot(vt_ref[0, h, qb], p0[h].astype(BF16), preferred_element_type=F32) for h in hs]

    def body(kb, carry):
        m_prev, l_prev, acc = carry
        s = [jnp.dot(k_ref[0, h, kb], qs[h], preferred_element_type=F32) + bias_ref[h, pl.ds(kb, 1), :] for h in hs]
        m_new = [jnp.maximum(mp, jnp.max(x, axis=0, keepdims=True)) for mp, x in zip(m_prev, s)]
        alpha = [jnp.exp(mp - mn) for mp, mn in zip(m_prev, m_new)]
        p = [jnp.exp(x - mn) for x, mn in zip(s, m_new)]
        l_new = [a * lp + jnp.sum(x, axis=0, keepdims=True) for a, lp, x in zip(alpha, l_prev, p)]
        pv = [jnp.dot(vt_ref[0, h, kb], p[h].astype(BF16), preferred_element_type=F32) for h in hs]
        acc = [a * c + x for a, c, x in zip(alpha, acc, pv)]
        return m_new, l_new, acc

    _, l_fin, acc = lax.fori_loop(0, qb, body, (m0, l0, acc0))
    for h in hs:
        o_ref[0, :, h * dh:(h + 1) * dh] = (acc[h] / l_fin[h]).T.astype(o_ref.dtype)


def _moba(qt, k, vt, kmean, heads_per_step=8):
    bsz, nh, nb, dh, blk = qt.shape
    hp = heads_per_step
    return pl.pallas_call(
        _moba_kernel,
        grid=(bsz, nh // hp, nb),
        in_specs=[pl.BlockSpec((1, hp, 1, dh, blk), lambda b, h, i: (b, h, i, 0, 0)),
                  pl.BlockSpec((1, hp, nb, blk, dh), lambda b, h, i: (b, h, 0, 0, 0)),
                  pl.BlockSpec((1, hp, nb, dh, blk), lambda b, h, i: (b, h, 0, 0, 0)),
                  pl.BlockSpec((1, nb, hp * dh), lambda b, h, i: (b, 0, h))],
        out_specs=pl.BlockSpec((1, blk, hp * dh), lambda b, h, i: (b, i, h)),
        out_shape=jax.ShapeDtypeStruct((bsz, nb * blk, nh * dh), BF16),
        scratch_shapes=[pltpu.VMEM((hp, nb, blk), F32)],
        compiler_params=_cparams("parallel", "parallel", "arbitrary"),
        name="moba",
    )(qt, k, vt, kmean)


def _out_proj_kernel(yr_ref, ya_ref, wr_ref, wa_ref, x_ref, g_ref, o_ref):
    y = jnp.dot(yr_ref[...], wr_ref[...], preferred_element_type=F32)
    y += jnp.dot(ya_ref[...], wa_ref[...], preferred_element_type=F32)
    ms = jnp.mean(y * y, axis=-1, keepdims=True)
    o_ref[...] = x_ref[...] + y * lax.rsqrt(ms + NORM_EPS) * g_ref[...]


def _out_proj(y_r, y_a, w_r, w_a, x2, gain, tm=256):
    m, d = x2.shape
    cr, ca = y_r.shape[1], y_a.shape[1]
    const = lambda shape: pl.BlockSpec(shape, lambda i: (0, 0))
    return pl.pallas_call(
        _out_proj_kernel,
        grid=(m // tm,),
        in_specs=[pl.BlockSpec((tm, cr), lambda i: (i, 0)), pl.BlockSpec((tm, ca), lambda i: (i, 0)),
                  const((cr, d)), const((ca, d)), pl.BlockSpec((tm, d), lambda i: (i, 0)), const((1, d))],
        out_specs=pl.BlockSpec((tm, d), lambda i: (i, 0)),
        out_shape=jax.ShapeDtypeStruct((m, d), F32),
        compiler_params=_cparams("parallel"),
        name="out_proj",
    )(y_r, y_a, w_r, w_a, x2, gain.reshape(1, d))


def _mlp_kernel(x_ref, gpre_ref, wu_ref, wd_ref, gpost_ref, o_ref, h_ref, acc_ref):
    f = pl.program_id(1)

    @pl.when(f == 0)
    def _():
        x = x_ref[...]
        ms = jnp.mean(x * x, axis=-1, keepdims=True)
        h_ref[...] = (x * lax.rsqrt(ms + NORM_EPS) * gpre_ref[...]).astype(BF16)
        acc_ref[...] = jnp.zeros_like(acc_ref)

    u = jnp.maximum(jnp.dot(h_ref[...], wu_ref[...], preferred_element_type=F32), 0.0)
    acc_ref[...] += jnp.dot((u * u).astype(BF16), wd_ref[...], preferred_element_type=F32)

    @pl.when(f == pl.num_programs(1) - 1)
    def _():
        mlp = acc_ref[...]
        ms = jnp.mean(mlp * mlp, axis=-1, keepdims=True)
        o_ref[...] = x_ref[...] + mlp * lax.rsqrt(ms + NORM_EPS) * gpost_ref[...]


def _mlp(x2, g_pre, w_up, w_down, g_post, tm=512, tf=1024):
    m, d = x2.shape
    dff = w_up.shape[1]
    return pl.pallas_call(
        _mlp_kernel,
        grid=(m // tm, dff // tf),
        in_specs=[pl.BlockSpec((tm, d), lambda i, f: (i, 0)),
                  pl.BlockSpec((1, d), lambda i, f: (0, 0)),
                  pl.BlockSpec((d, tf), lambda i, f: (0, f)),
                  pl.BlockSpec((tf, d), lambda i, f: (f, 0)),
                  pl.BlockSpec((1, d), lambda i, f: (0, 0))],
        out_specs=pl.BlockSpec((tm, d), lambda i, f: (i, 0)),
        out_shape=jax.ShapeDtypeStruct((m, d), F32),
        scratch_shapes=[pltpu.VMEM((tm, d), BF16), pltpu.VMEM((tm, d), F32)],
        compiler_params=_cparams("parallel", "arbitrary"),
        name="mlp",
    )(x2, g_pre.reshape(1, d), w_up, w_down, g_post.reshape(1, d))


def _head_indicators(c):
    head = jnp.arange(c) // RWKV_HEAD
    hsum = (head[:, None] == jnp.arange(128)[None, :]).astype(BF16)
    return hsum, hsum.T


def _rope_tables(s):
    half = ATTN_HEAD // 2
    inv_freq = ROPE_THETA ** (-jnp.arange(half, dtype=F32) / half)
    ang = jnp.arange(s).astype(F32)[:, None] * inv_freq[None, :]
    cos, sin = jnp.cos(ang), jnp.sin(ang)
    return jnp.concatenate([cos, cos], axis=-1), jnp.concatenate([-sin, sin], axis=-1)


def kernel(x, norm_mix_pre, norm_mix_post, norm_mlp_pre, norm_mlp_post, w_in, w_in_vres, shift_mu, shift_mu_vres, decay_w0, decay_w2, iclr_a0, iclr_a2, vres_v0, vres_v2, gate_g2, k_k, k_a, r_k, lnx_gain, lnx_bias, w_out, w_up, w_down):
    bsz, s, d = x.shape
    depth = w_in.shape[0]
    c = decay_w0.shape[1]
    n_lora = DECAY_LORA + ICLR_LORA + GATE_LORA
    n_shift = 3 * c + n_lora
    ca = (w_in.shape[2] - n_shift) // 3
    hsum, hbc = _head_indicators(c)
    cos2, sin2 = _rope_tables(s)
    x2 = x.reshape(bsz * s, d)
    v_first = None
    for i in range(depth):
        lora_w = [w_in[i][:, 3 * c:n_shift]]
        lora_mu = [shift_mu[i][3 * c:]]
        n_used = n_lora
        if i > 0:
            lora_w.append(w_in_vres[i - 1])
            lora_mu.append(shift_mu_vres[i - 1])
            n_used += VRES_LORA
        lora_w.append(jnp.zeros((d, LORA_PAD - n_used), F32))
        lora_mu.append(jnp.zeros((LORA_PAD - n_used,), F32))
        w_comb = jnp.concatenate([w_in[i][:, :3 * c], w_in[i][:, n_shift:]] + lora_w, axis=1).astype(BF16)
        z = _norm_matmul(x2, norm_mix_pre[i], w_comb)
        z3 = z.reshape(bsz, s, -1)

        vres = None if i == 0 else (v_first, vres_v0[i - 1], vres_v2[i - 1])
        r, lw, k, v, an, b, g = _rwkv_prep(z3, c, 3 * c + 3 * ca, shift_mu[i][:3 * c],
                                           jnp.concatenate(lora_mu), decay_w0[i],
                                           decay_w2[i], iclr_a0[i], iclr_a2[i], gate_g2[i], k_k[i], k_a[i],
                                           hsum, hbc, vres)
        if i == 0:
            v_first = v
        y = _wkv(r, lw, k, v, an, b)
        y_r = _rwkv_post(y, r, k, v, g, r_k[i].reshape(-1), lnx_gain[i], lnx_bias[i], hsum, hbc)

        q_t, k_rot, v_t, kmean = _rope(z3, 3 * c, ca, cos2, sin2)
        y_a = _moba(q_t, k_rot, v_t, kmean.reshape(bsz, -1, ca))

        w_o = w_out[i].astype(BF16)
        x2 = _out_proj(y_r.reshape(bsz * s, c), y_a.reshape(bsz * s, ca), w_o[:c], w_o[c:], x2,
                       norm_mix_post[i])
        x2 = _mlp(x2, norm_mlp_pre[i], w_up[i].astype(BF16), w_down[i].astype(BF16), norm_mlp_post[i])
    return x2.reshape(bsz, s, d)
```

```python
import functools

import jax
import jax.numpy as jnp
from jax import lax
from jax.experimental import pallas as pl
from jax.experimental.pallas import tpu as pltpu

F32 = jnp.float32
BF16 = jnp.bfloat16

RWKV_HEAD = 64
DECAY_LORA = 64
ICLR_LORA = 64
VRES_LORA = 32
GATE_LORA = 128
ATTN_HEAD = 128
MOBA_BLOCK = 256
MOBA_TOPK = 3
ROPE_THETA = 10000.0
NORM_EPS = 1e-6
LNX_EPS = 64e-5

LORA_PAD = 512
WKV_CHUNK = 64
P_A, P_INV, P_PQ, P_OUT, P_STATE = 1, 1, 1, 1, 1
VMEM_LIMIT = 56 * 1024 * 1024


def _cparams(*sem):
    return pltpu.CompilerParams(dimension_semantics=sem, vmem_limit_bytes=VMEM_LIMIT)


_NN = (((1,), (0,)), ((), ()))
_NT = (((1,), (1,)), ((), ()))
_TN = (((0,), (0,)), ((), ()))


def _split2(x):
    hi = x.astype(BF16)
    lo = (x - hi.astype(F32)).astype(BF16)
    return hi, lo


def _split3(x):
    hi = x.astype(BF16)
    r1 = x - hi.astype(F32)
    mid = r1.astype(BF16)
    lo = (r1 - mid.astype(F32)).astype(BF16)
    return hi, mid, lo


def _mm(a, b, dims=_NN, passes=1):
    d = lambda p, q: lax.dot_general(p, q, dims, preferred_element_type=F32)
    if passes == 1:
        return d(a.astype(BF16), b.astype(BF16))
    ah, al = _split2(a)
    bh, bl = _split2(b)
    return d(ah, bh) + (d(ah, bl) + d(al, bh))


def _mm_exact_rhs(a, b_bf16, dims=_NN):
    d = lambda p: lax.dot_general(p, b_bf16, dims, preferred_element_type=F32)
    hi, mid, lo = _split3(a)
    return d(hi) + (d(mid) + d(lo))


def _sigmoid(x):
    return 1.0 / (1.0 + jnp.exp(-x))


def _softplus(x):
    return jnp.maximum(x, 0.0) + jnp.log(1.0 + jnp.exp(-jnp.abs(x)))


def _norm_matmul_kernel(x_ref, g_ref, w_ref, o_ref, h_ref):
    @pl.when(pl.program_id(1) == 0)
    def _():
        x = x_ref[...]
        ms = jnp.mean(x * x, axis=-1, keepdims=True)
        h_ref[...] = (x * lax.rsqrt(ms + NORM_EPS) * g_ref[...]).astype(BF16)

    o_ref[...] = jnp.dot(h_ref[...], w_ref[...], preferred_element_type=F32)


def _norm_matmul(x2, gain, w, tm=1024, tn=512):
    m, d = x2.shape
    n = w.shape[1]
    assert m % tm == 0 and n % tn == 0
    return pl.pallas_call(
        _norm_matmul_kernel,
        grid=(m // tm, n // tn),
        in_specs=[pl.BlockSpec((tm, d), lambda i, j: (i, 0)),
                  pl.BlockSpec((1, d), lambda i, j: (0, 0)),
                  pl.BlockSpec((d, tn), lambda i, j: (0, j))],
        out_specs=pl.BlockSpec((tm, tn), lambda i, j: (i, j)),
        out_shape=jax.ShapeDtypeStruct((m, n), F32),
        scratch_shapes=[pltpu.VMEM((tm, d), BF16)],
        compiler_params=_cparams("parallel", "arbitrary"),
        name="norm_matmul",
    )(x2, gain.reshape(1, d), w)


def _shifted(cur, halo, mu, first):
    rows = lax.broadcasted_iota(jnp.int32, cur.shape, 0)
    last = jnp.where(first, 0.0, halo[7:8, :])
    prev = jnp.where(rows == 0, last, pltpu.roll(cur, 1, axis=0))
    return cur + (prev - cur) * mu


def _rwkv_prep_kernel(has_vres, *refs):
    if has_vres:
        (zm_ref, zmh_ref, zl_ref, zlh_ref, mum_ref, mul_ref, w0_ref, w2_ref, a0_ref, a2_ref, g2_ref,
         kk_ref, ka_ref, hsum_ref, hbc_ref, vf_ref, v0_ref, v2_ref,
         r_ref, lw_ref, k_ref, v_ref, an_ref, b_ref, g_ref) = refs
    else:
        (zm_ref, zmh_ref, zl_ref, zlh_ref, mum_ref, mul_ref, w0_ref, w2_ref, a0_ref, a2_ref, g2_ref,
         kk_ref, ka_ref, hsum_ref, hbc_ref,
         r_ref, lw_ref, k_ref, v_ref, an_ref, b_ref, g_ref, vf32_ref) = refs
    c = r_ref.shape[-1]
    first = pl.program_id(1) == 0
    zs = _shifted(zm_ref[0], zmh_ref[0], mum_ref[...], first)
    zl = _shifted(zl_ref[0], zlh_ref[0], mul_ref[...], first)
    r, k, v = zs[:, :c], zs[:, c:2 * c], zs[:, 2 * c:3 * c]
    o = 0
    wd = zl[:, o:o + DECAY_LORA]
    o += DECAY_LORA
    ad = zl[:, o:o + ICLR_LORA]
    o += ICLR_LORA
    gd = zl[:, o:o + GATE_LORA]
    o += GATE_LORA
    w_log = -_softplus(-(w0_ref[...] + _mm(jnp.tanh(wd), w2_ref[...], passes=3))) - 0.5
    lw_ref[0] = -jnp.exp(w_log)
    a = _sigmoid(a0_ref[...] + _mm(ad, a2_ref[...], passes=3))
    g_ref[0] = _mm(_sigmoid(gd), g2_ref[...], passes=3).astype(g_ref.dtype)
    if has_vres:
        vd = zl[:, o:o + VRES_LORA]
        v = v + (vf_ref[0] - v) * _sigmoid(v0_ref[...] + _mm(vd, v2_ref[...], passes=3))
    else:
        vf32_ref[0] = v
    kk = k * kk_ref[...]
    ss = _mm_exact_rhs(kk * kk, hsum_ref[...])
    inv = 1.0 / jnp.maximum(jnp.sqrt(ss), 1e-12)
    kk = kk * _mm_exact_rhs(inv, hbc_ref[...])
    r_ref[0] = r.astype(r_ref.dtype)
    k_ref[0] = (k * (1.0 + (a - 1.0) * ka_ref[...])).astype(k_ref.dtype)
    v_ref[0] = v.astype(v_ref.dtype)
    an_ref[0] = (-kk).astype(an_ref.dtype)
    b_ref[0] = (kk * a).astype(b_ref.dtype)


def _rwkv_prep(z3, c, lora_col, mu_main, mu_lora, w0, w2, a0, a2, g2, k_k, k_a, hsum, hbc, vres, tm=256):
    bsz, s, _ = z3.shape
    assert s % tm == 0
    has_vres = vres is not None
    row = lambda a: a.reshape(1, -1)
    hb = tm // 8
    lcb = lora_col // LORA_PAD
    halo = lambda b, i: (b, jnp.maximum(i * hb - 1, 0), 0)
    halo_l = lambda b, i: (b, jnp.maximum(i * hb - 1, 0), lcb)
    full = lambda a: pl.BlockSpec(a.shape, lambda b, i: (0,) * a.ndim)
    ins = [z3, z3, z3, z3, row(mu_main), row(mu_lora), row(w0), w2, row(a0), a2, g2, row(k_k), row(k_a),
           hsum, hbc]
    in_specs = [pl.BlockSpec((1, tm, 3 * c), lambda b, i: (b, i, 0)),
                pl.BlockSpec((1, 8, 3 * c), halo),
                pl.BlockSpec((1, tm, LORA_PAD), lambda b, i: (b, i, lcb)),
                pl.BlockSpec((1, 8, LORA_PAD), halo_l)] + [full(a) for a in ins[4:]]
    if has_vres:
        v_first, v0, v2 = vres
        extra = [v_first, row(v0), v2]
        ins += extra
        in_specs += [pl.BlockSpec((1, tm, c), lambda b, i: (b, i, 0)), full(extra[1]), full(extra[2])]
    out_spec = pl.BlockSpec((1, tm, c), lambda b, i: (b, i, 0))
    sd = lambda dt: jax.ShapeDtypeStruct((bsz, s, c), dt)
    out_dtypes = [BF16, F32, BF16, BF16, BF16, BF16, BF16] + ([] if has_vres else [F32])
    return pl.pallas_call(
        functools.partial(_rwkv_prep_kernel, has_vres),
        grid=(bsz, s // tm),
        in_specs=in_specs,
        out_specs=[out_spec] * len(out_dtypes),
        out_shape=[sd(dt) for dt in out_dtypes],
        compiler_params=_cparams("parallel", "arbitrary"),
        name="rwkv_prep",
    )(*ins)


def _mm_heads(a_list, b_list, dims=_NN, passes=3):
    d = lambda p, q: lax.dot_general(p, q, dims, preferred_element_type=F32)
    if passes == 1:
        return [d(a.astype(BF16), b.astype(BF16)) for a, b in zip(a_list, b_list)]
    sa = [_split2(a) for a in a_list]
    sb = [_split2(b) for b in b_list]
    out = [d(x[0], y[0]) for x, y in zip(sa, sb)]
    out = [o + d(x[0], y[1]) for o, x, y in zip(out, sa, sb)]
    return [o + d(x[1], y[0]) for o, x, y in zip(out, sa, sb)]


def _unit_lower_inverse(a_list, rows, cols):
    n = rows.shape[0]
    lower = rows > cols
    base = lower & ((rows >> 1) == (cols >> 1))
    t = [jnp.where(rows == cols, 1.0, jnp.where(base, a, 0.0)) for a in a_list]
    sh = 1
    while (2 << sh) <= n:
        sub = lower & ((rows >> (sh + 1)) == (cols >> (sh + 1))) & ((rows >> sh) != (cols >> sh))
        off = [jnp.where(sub, a, 0.0) for a in a_list]
        upd = _mm_heads(_mm_heads(t, off, passes=P_INV), t, passes=P_INV)
        t = [x + u for x, u in zip(t, upd)]
        sh += 1
    return t


def _wkv_kernel(r_ref, lw_ref, k_ref, v_ref, an_ref, b_ref, g_ref, rk_ref, lg_ref, lb_ref, y_ref, state_ref):
    ln = r_ref.shape[1]
    hg = r_ref.shape[2] // RWKV_HEAD
    n = RWKV_HEAD
    sls = [slice(h * n, (h + 1) * n) for h in range(hg)]

    @pl.when(pl.program_id(2) == 0)
    def _():
        state_ref[...] = jnp.zeros_like(state_ref)

    rows = lax.broadcasted_iota(jnp.int32, (ln, ln), 0)
    cols = lax.broadcasted_iota(jnp.int32, (ln, ln), 1)
    tril = jnp.where(rows >= cols, 1.0, 0.0).astype(BF16)
    lw = lw_ref[0]
    cw = _mm_exact_rhs_left(tril, lw)
    cw_end = cw[ln - 1:ln, :]
    e_neg = jnp.exp(-cw)
    w_end = jnp.exp(cw_end)
    r = r_ref[0].astype(F32)
    k = k_ref[0].astype(F32)
    bb = b_ref[0].astype(F32)
    v = v_ref[0].astype(F32)
    r_t = r * jnp.exp(cw)
    a_t = an_ref[0].astype(F32) * jnp.exp(cw - lw)
    b_t = bb * e_neg
    k_t = k * e_neg
    e_end = jnp.exp(cw_end - cw)
    b_h = bb * e_end
    k_h = k * e_end
    eye_n = lax.broadcasted_iota(jnp.int32, (n, n), 0) == lax.broadcasted_iota(jnp.int32, (n, n), 1)
    zeros = jnp.zeros((ln, n), F32)

    ar = [jnp.concatenate([a_t[:, s], r_t[:, s]], axis=0) for s in sls]
    bk = [jnp.concatenate([b_t[:, s], k_t[:, s]], axis=0) for s in sls]
    amat = _mm_heads(ar, bk, _NT, passes=P_A)
    strict = rows > cols
    incl = rows >= cols
    a_ab = [jnp.where(strict, m[:ln, :ln], 0.0) for m in amat]
    a_ak = [jnp.where(strict, m[:ln, ln:], 0.0) for m in amat]
    a_r = [jnp.concatenate([jnp.where(incl, m[ln:, :ln], 0.0), jnp.where(incl, m[ln:, ln:], 0.0)], axis=1)
           for m in amat]
    vh = [v[:, s] for s in sls]
    akv = _mm_heads(a_ak, vh, passes=P_PQ)
    t = _unit_lower_inverse(a_ab, rows, cols)
    rhs = [jnp.concatenate([a_t[:, s], x], axis=1) for s, x in zip(sls, akv)]
    pq = _mm_heads(t, rhs, passes=P_PQ)
    pqv = [jnp.concatenate([x, jnp.concatenate([zeros, u], axis=1)], axis=0)
           for x, u in zip(pq, vh)]
    ry = _mm_heads(a_r, pqv, passes=P_OUT)
    bkh = [jnp.concatenate([b_h[:, s], k_h[:, s]], axis=0) for s in sls]
    mn = _mm_heads(bkh, pqv, _TN, passes=P_OUT)
    lhs = [jnp.concatenate([r_t[:, s] + x[:, :n],
                            jnp.where(eye_n, jnp.broadcast_to(w_end[:, s], (n, n)), 0.0) + m[:, :n]], axis=0)
           for s, x, m in zip(sls, ry, mn)]
    st = [state_ref[h] for h in range(hg)]
    upd = _mm_heads(lhs, st, passes=P_STATE)
    for h in range(hg):
        state_ref[h] = upd[h][ln:, :] + mn[h][:, n:]
    yn = []
    for u, x in zip(upd, ry):
        yh = u[:ln, :] + x[:, n:]
        d = yh - jnp.mean(yh, axis=-1, keepdims=True)
        yn.append(d * lax.rsqrt(jnp.mean(d * d, axis=-1, keepdims=True) + LNX_EPS))
    rk = r * k * rk_ref[...]
    bonus = jnp.concatenate([jnp.sum(rk[:, s], axis=-1, keepdims=True) * v[:, s] for s in sls], axis=1)
    out = (jnp.concatenate(yn, axis=1) * lg_ref[...] + lb_ref[...] + bonus) * g_ref[0].astype(F32)
    y_ref[0] = out.astype(y_ref.dtype)


def _mm_exact_rhs_left(l_bf16, a):
    d = lambda p: lax.dot_general(l_bf16, p, _NN, preferred_element_type=F32)
    hi, mid, lo = _split3(a)
    return d(hi) + (d(mid) + d(lo))


def _wkv(r, lw, k, v, an, b, g, r_k, lnx_g, lnx_b, heads_per_step=16):
    bsz, s, c = r.shape
    wb = heads_per_step * RWKV_HEAD
    nc = s // WKV_CHUNK
    assert s % WKV_CHUNK == 0 and c % wb == 0
    spec = pl.BlockSpec((1, WKV_CHUNK, wb), lambda bi, hi, ci: (bi, ci, hi))
    pspec = pl.BlockSpec((1, wb), lambda bi, hi, ci: (0, hi))
    row = lambda a: a.reshape(1, -1)
    return pl.pallas_call(
        _wkv_kernel,
        grid=(bsz, c // wb, nc),
        in_specs=[spec] * 7 + [pspec] * 3,
        out_specs=spec,
        out_shape=jax.ShapeDtypeStruct((bsz, s, c), BF16),
        scratch_shapes=[pltpu.VMEM((heads_per_step, RWKV_HEAD, RWKV_HEAD), F32)],
        compiler_params=_cparams("parallel", "parallel", "arbitrary"),
        name="wkv",
    )(r, lw, k, v, an, b, g, row(r_k), row(lnx_g), row(lnx_b))


def _rope_kernel(q_ref, k_ref, v_ref, cos_ref, sin_ref, qt_ref, ko_ref, vt_ref, km_ref):
    cos = cos_ref[...]
    sin = sin_ref[...]
    nh = q_ref.shape[-1] // ATTN_HEAD
    inv_rows = 1.0 / q_ref.shape[1]
    for h in range(nh):
        sl = slice(h * ATTN_HEAD, (h + 1) * ATTN_HEAD)
        q = q_ref[0, :, sl]
        k = k_ref[0, :, sl]
        qr = q * cos + pltpu.roll(q, ATTN_HEAD // 2, axis=1) * sin
        kr = k * cos + pltpu.roll(k, ATTN_HEAD // 2, axis=1) * sin
        qt_ref[0, h, 0] = qr.T
        ko_ref[0, h, 0] = kr.astype(BF16)
        vt_ref[0, h, 0] = v_ref[0, :, sl].T.astype(BF16)
        km_ref[0, 0, :, sl] = jnp.sum(kr, axis=0, keepdims=True) * inv_rows


def _rope(z3, col0, width, cos2, sin2):
    bsz, s, _ = z3.shape
    assert s % MOBA_BLOCK == 0 and col0 % width == 0
    nb = s // MOBA_BLOCK
    nh = width // ATTN_HEAD
    cb = col0 // width
    blk = lambda j: pl.BlockSpec((1, MOBA_BLOCK, width), lambda b, i: (b, i, cb + j))
    tab = pl.BlockSpec((MOBA_BLOCK, ATTN_HEAD), lambda b, i: (i, 0))
    t_spec = pl.BlockSpec((1, nh, 1, ATTN_HEAD, MOBA_BLOCK), lambda b, i: (b, 0, i, 0, 0))
    n_spec = pl.BlockSpec((1, nh, 1, MOBA_BLOCK, ATTN_HEAD), lambda b, i: (b, 0, i, 0, 0))
    return pl.pallas_call(
        _rope_kernel,
        grid=(bsz, nb),
        in_specs=[blk(0), blk(1), blk(2), tab, tab],
        out_specs=[t_spec, n_spec, t_spec, pl.BlockSpec((1, 1, 1, width), lambda b, i: (b, i, 0, 0))],
        out_shape=[jax.ShapeDtypeStruct((bsz, nh, nb, ATTN_HEAD, MOBA_BLOCK), F32),
                   jax.ShapeDtypeStruct((bsz, nh, nb, MOBA_BLOCK, ATTN_HEAD), BF16),
                   jax.ShapeDtypeStruct((bsz, nh, nb, ATTN_HEAD, MOBA_BLOCK), BF16),
                   jax.ShapeDtypeStruct((bsz, nb, 1, width), F32)],
        compiler_params=_cparams("parallel", "parallel"),
        name="rope",
    )(z3, z3, z3, cos2, sin2)


def _moba_kernel(qt_ref, k_ref, vt_ref, km_ref, o_ref, bias_ref):
    blk = MOBA_BLOCK
    dh = ATTN_HEAD
    hs = range(qt_ref.shape[1])
    qb = pl.program_id(2)
    nb = km_ref.shape[1]
    scale = ATTN_HEAD ** -0.5
    neg = -jnp.inf
    qt = [qt_ref[0, h, 0] for h in hs]
    blk_id = lax.broadcasted_iota(jnp.int32, (nb, blk), 0)
    past = blk_id < qb
    gate = [jnp.where(past, _mm(km_ref[0, :, h * dh:(h + 1) * dh], qt[h], passes=3), neg) for h in hs]
    for h in hs:
        rank = jnp.zeros((nb, blk), jnp.int32)
        for m in range(nb):
            gm = gate[h][m:m + 1, :]
            rank += ((gm > gate[h]) | ((gm == gate[h]) & (m < blk_id))).astype(jnp.int32)
        bias_ref[h] = jnp.where(past & (rank < MOBA_TOPK), 0.0, neg)

    qs = [(q * scale).astype(BF16) for q in qt]
    ki = lax.broadcasted_iota(jnp.int32, (blk, blk), 0)
    qi = lax.broadcasted_iota(jnp.int32, (blk, blk), 1)
    s_own = [jnp.where(ki <= qi, jnp.dot(k_ref[0, h, qb], qs[h], preferred_element_type=F32), neg) for h in hs]
    m0 = [jnp.max(s, axis=0, keepdims=True) for s in s_own]
    p0 = [jnp.exp(s - m) for s, m in zip(s_own, m0)]
    l0 = [jnp.sum(p, axis=0, keepdims=True) for p in p0]
    acc0 = [jnp.dot(vt_ref[0, h, qb], p0[h].astype(BF16), preferred_element_type=F32) for h in hs]

    def body(kb, carry):
        m_prev, l_prev, acc = carry
        s = [jnp.dot(k_ref[0, h, kb], qs[h], preferred_element_type=F32) + bias_ref[h, pl.ds(kb, 1), :] for h in hs]
        m_new = [jnp.maximum(mp, jnp.max(x, axis=0, keepdims=True)) for mp, x in zip(m_prev, s)]
        alpha = [jnp.exp(mp - mn) for mp, mn in zip(m_prev, m_new)]
        p = [jnp.exp(x - mn) for x, mn in zip(s, m_new)]
        l_new = [a * lp + jnp.sum(x, axis=0, keepdims=True) for a, lp, x in zip(alpha, l_prev, p)]
        pv = [jnp.dot(vt_ref[0, h, kb], p[h].astype(BF16), preferred_element_type=F32) for h in hs]
        acc = [a * c + x for a, c, x in zip(alpha, acc, pv)]
        return m_new, l_new, acc

    _, l_fin, acc = lax.fori_loop(0, qb, body, (m0, l0, acc0))
    for h in hs:
        o_ref[0, :, h * dh:(h + 1) * dh] = (acc[h] / l_fin[h]).T.astype(o_ref.dtype)


def _moba(qt, k, vt, kmean, heads_per_step=8):
    bsz, nh, nb, dh, blk = qt.shape
    hp = heads_per_step
    assert nh % hp == 0
    return pl.pallas_call(
        _moba_kernel,
        grid=(bsz, nh // hp, nb),
        in_specs=[pl.BlockSpec((1, hp, 1, dh, blk), lambda b, h, i: (b, h, i, 0, 0)),
                  pl.BlockSpec((1, hp, nb, blk, dh), lambda b, h, i: (b, h, 0, 0, 0)),
                  pl.BlockSpec((1, hp, nb, dh, blk), lambda b, h, i: (b, h, 0, 0, 0)),
                  pl.BlockSpec((1, nb, hp * dh), lambda b, h, i: (b, 0, h))],
        out_specs=pl.BlockSpec((1, blk, hp * dh), lambda b, h, i: (b, i, h)),
        out_shape=jax.ShapeDtypeStruct((bsz, nb * blk, nh * dh), BF16),
        scratch_shapes=[pltpu.VMEM((hp, nb, blk), F32)],
        compiler_params=_cparams("parallel", "parallel", "arbitrary"),
        name="moba",
    )(qt, k, vt, kmean)


def _out_proj_kernel(yr_ref, ya_ref, wr_ref, wa_ref, x_ref, g_ref, o_ref):
    y = jnp.dot(yr_ref[...], wr_ref[...], preferred_element_type=F32)
    y += jnp.dot(ya_ref[...], wa_ref[...], preferred_element_type=F32)
    ms = jnp.mean(y * y, axis=-1, keepdims=True)
    o_ref[...] = x_ref[...] + y * lax.rsqrt(ms + NORM_EPS) * g_ref[...]


def _out_proj(y_r, y_a, w, x2, gain, tm=256):
    m, d = x2.shape
    cw = y_r.shape[1]
    assert y_a.shape[1] == cw and w.shape[0] == 2 * cw and m % tm == 0
    return pl.pallas_call(
        _out_proj_kernel,
        grid=(m // tm,),
        in_specs=[pl.BlockSpec((tm, cw), lambda i: (i, 0)), pl.BlockSpec((tm, cw), lambda i: (i, 0)),
                  pl.BlockSpec((cw, d), lambda i: (0, 0)), pl.BlockSpec((cw, d), lambda i: (1, 0)),
                  pl.BlockSpec((tm, d), lambda i: (i, 0)), pl.BlockSpec((1, d), lambda i: (0, 0))],
        out_specs=pl.BlockSpec((tm, d), lambda i: (i, 0)),
        out_shape=jax.ShapeDtypeStruct((m, d), F32),
        compiler_params=_cparams("parallel"),
        name="out_proj",
    )(y_r, y_a, w, w, x2, gain.reshape(1, d))


def _mlp_kernel(x_ref, gpre_ref, wu_ref, wd_ref, gpost_ref, o_ref, h_ref, acc_ref):
    f = pl.program_id(1)

    @pl.when(f == 0)
    def _():
        x = x_ref[...]
        ms = jnp.mean(x * x, axis=-1, keepdims=True)
        h_ref[...] = (x * lax.rsqrt(ms + NORM_EPS) * gpre_ref[...]).astype(BF16)
        acc_ref[...] = jnp.zeros_like(acc_ref)

    u = jnp.maximum(jnp.dot(h_ref[...], wu_ref[...], preferred_element_type=F32), 0.0)
    acc_ref[...] += jnp.dot((u * u).astype(BF16), wd_ref[...], preferred_element_type=F32)

    @pl.when(f == pl.num_programs(1) - 1)
    def _():
        mlp = acc_ref[...]
        ms = jnp.mean(mlp * mlp, axis=-1, keepdims=True)
        o_ref[...] = x_ref[...] + mlp * lax.rsqrt(ms + NORM_EPS) * gpost_ref[...]


def _mlp(x2, g_pre, w_up, w_down, g_post, tm=512, tf=1024):
    m, d = x2.shape
    dff = w_up.shape[1]
    assert m % tm == 0 and dff % tf == 0
    return pl.pallas_call(
        _mlp_kernel,
        grid=(m // tm, dff // tf),
        in_specs=[pl.BlockSpec((tm, d), lambda i, f: (i, 0)),
                  pl.BlockSpec((1, d), lambda i, f: (0, 0)),
                  pl.BlockSpec((d, tf), lambda i, f: (0, f)),
                  pl.BlockSpec((tf, d), lambda i, f: (f, 0)),
                  pl.BlockSpec((1, d), lambda i, f: (0, 0))],
        out_specs=pl.BlockSpec((tm, d), lambda i, f: (i, 0)),
        out_shape=jax.ShapeDtypeStruct((m, d), F32),
        scratch_shapes=[pltpu.VMEM((tm, d), BF16), pltpu.VMEM((tm, d), F32)],
        compiler_params=_cparams("parallel", "arbitrary"),
        name="mlp",
    )(x2, g_pre.reshape(1, d), w_up, w_down, g_post.reshape(1, d))


def _head_indicators(c):
    head = jnp.arange(c) // RWKV_HEAD
    hsum = (head[:, None] == jnp.arange(128)[None, :]).astype(BF16)
    return hsum, hsum.T


def _rope_tables(s):
    half = ATTN_HEAD // 2
    inv_freq = ROPE_THETA ** (-jnp.arange(half, dtype=F32) / half)
    ang = jnp.arange(s).astype(F32)[:, None] * inv_freq[None, :]
    cos, sin = jnp.cos(ang), jnp.sin(ang)
    return jnp.concatenate([cos, cos], axis=-1), jnp.concatenate([-sin, sin], axis=-1)


def kernel(x, norm_mix_pre, norm_mix_post, norm_mlp_pre, norm_mlp_post, w_in, w_in_vres, shift_mu, shift_mu_vres, decay_w0, decay_w2, iclr_a0, iclr_a2, vres_v0, vres_v2, gate_g2, k_k, k_a, r_k, lnx_gain, lnx_bias, w_out, w_up, w_down):
    bsz, s, d = x.shape
    depth = w_in.shape[0]
    c = decay_w0.shape[1]
    n_lora = DECAY_LORA + ICLR_LORA + GATE_LORA
    n_shift = 3 * c + n_lora
    ca = (w_in.shape[2] - n_shift) // 3
    hsum, hbc = _head_indicators(c)
    cos2, sin2 = _rope_tables(s)
    x2 = x.reshape(bsz * s, d)
    v_first = None
    for i in range(depth):
        lora_w = [w_in[i][:, 3 * c:n_shift]]
        lora_mu = [shift_mu[i][3 * c:]]
        n_used = n_lora
        if i > 0:
            lora_w.append(w_in_vres[i - 1])
            lora_mu.append(shift_mu_vres[i - 1])
            n_used += VRES_LORA
        lora_w.append(jnp.zeros((d, LORA_PAD - n_used), F32))
        lora_mu.append(jnp.zeros((LORA_PAD - n_used,), F32))
        w_comb = jnp.concatenate([w_in[i][:, :3 * c], w_in[i][:, n_shift:]] + lora_w, axis=1).astype(BF16)
        z = _norm_matmul(x2, norm_mix_pre[i], w_comb)
        z3 = z.reshape(bsz, s, -1)

        vres = None if i == 0 else (v_first, vres_v0[i - 1], vres_v2[i - 1])
        prep = _rwkv_prep(z3, c, 3 * c + 3 * ca, shift_mu[i][:3 * c], jnp.concatenate(lora_mu), decay_w0[i],
                          decay_w2[i], iclr_a0[i], iclr_a2[i], gate_g2[i], k_k[i], k_a[i], hsum, hbc, vres)
        if i == 0:
            v_first = prep[7]
        y_r = _wkv(*prep[:7], r_k[i].reshape(-1), lnx_gain[i], lnx_bias[i])

        q_t, k_rot, v_t, kmean = _rope(z3, 3 * c, ca, cos2, sin2)
        y_a = _moba(q_t, k_rot, v_t, kmean.reshape(bsz, -1, ca))

        x2 = _out_proj(y_r.reshape(bsz * s, c), y_a.reshape(bsz * s, ca), w_out[i].astype(BF16), x2,
                       norm_mix_post[i])
        x2 = _mlp(x2, norm_mlp_pre[i], w_up[i].astype(BF16), w_down[i].astype(BF16), norm_mlp_post[i])
    return x2.reshape(bsz, s, d)
```

```python
import functools

import jax
import jax.numpy as jnp
from jax import lax
from jax.experimental import pallas as pl
from jax.experimental.pallas import tpu as pltpu

F32 = jnp.float32
BF16 = jnp.bfloat16

RWKV_HEAD = 64
DECAY_LORA = 64
ICLR_LORA = 64
VRES_LORA = 32
GATE_LORA = 128
ATTN_HEAD = 128
MOBA_BLOCK = 256
MOBA_TOPK = 3
ROPE_THETA = 10000.0
NORM_EPS = 1e-6
LNX_EPS = 64e-5

VRES_PAD = 128
WKV_CHUNK = 64
P_A, P_INV, P_PQ, P_OUT, P_STATE = 1, 1, 1, 1, 1
VMEM_LIMIT = 56 * 1024 * 1024


def _cparams(*sem):
    return pltpu.CompilerParams(dimension_semantics=sem, vmem_limit_bytes=VMEM_LIMIT)


_NN = (((1,), (0,)), ((), ()))
_NT = (((1,), (1,)), ((), ()))
_TN = (((0,), (0,)), ((), ()))


def _split2(x):
    hi = x.astype(BF16)
    lo = (x - hi.astype(F32)).astype(BF16)
    return hi, lo


def _split3(x):
    hi = x.astype(BF16)
    r1 = x - hi.astype(F32)
    mid = r1.astype(BF16)
    lo = (r1 - mid.astype(F32)).astype(BF16)
    return hi, mid, lo


def _mm(a, b, dims=_NN, passes=1):
    d = lambda p, q: lax.dot_general(p, q, dims, preferred_element_type=F32)
    if passes == 1:
        return d(a.astype(BF16), b.astype(BF16))
    ah, al = _split2(a)
    bh, bl = _split2(b)
    return d(ah, bh) + (d(ah, bl) + d(al, bh))


def _mm_exact_rhs(a, b_bf16, dims=_NN):
    d = lambda p: lax.dot_general(p, b_bf16, dims, preferred_element_type=F32)
    hi, mid, lo = _split3(a)
    return d(hi) + (d(mid) + d(lo))


def _sigmoid(x):
    return 1.0 / (1.0 + jnp.exp(-x))


def _softplus(x):
    return jnp.maximum(x, 0.0) + jnp.log(1.0 + jnp.exp(-jnp.abs(x)))


def _norm_matmul_kernel(has_extra, *refs):
    if has_extra:
        x_ref, g_ref, w_ref, we_ref, o_ref, oe_ref, h_ref = refs
    else:
        x_ref, g_ref, w_ref, o_ref, h_ref = refs

    @pl.when(pl.program_id(1) == 0)
    def _():
        x = x_ref[...]
        ms = jnp.mean(x * x, axis=-1, keepdims=True)
        h_ref[...] = (x * lax.rsqrt(ms + NORM_EPS) * g_ref[...]).astype(BF16)
        if has_extra:
            oe_ref[...] = jnp.dot(h_ref[...], we_ref[...], preferred_element_type=F32)

    o_ref[...] = jnp.dot(h_ref[...], w_ref[...], preferred_element_type=F32)


def _norm_matmul(x2, gain, w_all, layer, w_extra=None, tm=1024, tn=1280):
    m, d = x2.shape
    n = w_all.shape[2]
    assert m % tm == 0 and n % tn == 0
    has_extra = w_extra is not None
    ins = [x2, gain.reshape(1, d), w_all]
    in_specs = [pl.BlockSpec((tm, d), lambda i, j: (i, 0)),
                pl.BlockSpec((1, d), lambda i, j: (0, 0)),
                pl.BlockSpec((None, d, tn), lambda i, j: (layer, 0, j))]
    out_specs = [pl.BlockSpec((tm, tn), lambda i, j: (i, j))]
    out_shape = [jax.ShapeDtypeStruct((m, n), F32)]
    if has_extra:
        ne = w_extra.shape[1]
        ins.append(w_extra)
        in_specs.append(pl.BlockSpec((d, ne), lambda i, j: (0, 0)))
        out_specs.append(pl.BlockSpec((tm, ne), lambda i, j: (i, 0)))
        out_shape.append(jax.ShapeDtypeStruct((m, ne), F32))
    return pl.pallas_call(
        functools.partial(_norm_matmul_kernel, has_extra),
        grid=(m // tm, n // tn),
        in_specs=in_specs,
        out_specs=out_specs,
        out_shape=out_shape,
        scratch_shapes=[pltpu.VMEM((tm, d), BF16)],
        compiler_params=_cparams("parallel", "arbitrary"),
        name="norm_matmul",
    )(*ins)


def _shifted(cur, halo, mu, first):
    rows = lax.broadcasted_iota(jnp.int32, cur.shape, 0)
    last = jnp.where(first, 0.0, halo[7:8, :])
    prev = jnp.where(rows == 0, last, pltpu.roll(cur, 1, axis=0))
    return cur + (prev - cur) * mu


def _rwkv_prep_kernel(has_vres, *refs):
    if has_vres:
        (zm_ref, zmh_ref, zl_ref, zlh_ref, mum_ref, mul_ref, w0_ref, w2_ref, a0_ref, a2_ref, g2_ref,
         kk_ref, ka_ref, hsum_ref, hbc_ref, zv_ref, zvh_ref, muv_ref, vf_ref, v0_ref, v2_ref,
         r_ref, lw_ref, k_ref, v_ref, an_ref, b_ref, g_ref) = refs
    else:
        (zm_ref, zmh_ref, zl_ref, zlh_ref, mum_ref, mul_ref, w0_ref, w2_ref, a0_ref, a2_ref, g2_ref,
         kk_ref, ka_ref, hsum_ref, hbc_ref,
         r_ref, lw_ref, k_ref, v_ref, an_ref, b_ref, g_ref, vf32_ref) = refs
    c = r_ref.shape[-1]
    first = pl.program_id(1) == 0
    zs = _shifted(zm_ref[0], zmh_ref[0], mum_ref[...], first)
    zl = _shifted(zl_ref[0], zlh_ref[0], mul_ref[...], first)
    r, k, v = zs[:, :c], zs[:, c:2 * c], zs[:, 2 * c:3 * c]
    o = 0
    wd = zl[:, o:o + DECAY_LORA]
    o += DECAY_LORA
    ad = zl[:, o:o + ICLR_LORA]
    o += ICLR_LORA
    gd = zl[:, o:o + GATE_LORA]
    o += GATE_LORA
    w_log = -_softplus(-(w0_ref[...] + _mm(jnp.tanh(wd), w2_ref[...], passes=3))) - 0.5
    lw_ref[0] = -jnp.exp(w_log)
    a = _sigmoid(a0_ref[...] + _mm(ad, a2_ref[...], passes=3))
    g_ref[0] = _mm(_sigmoid(gd), g2_ref[...], passes=3).astype(g_ref.dtype)
    if has_vres:
        vd = _shifted(zv_ref[0], zvh_ref[0], muv_ref[...], first)
        v = v + (vf_ref[0] - v) * _sigmoid(v0_ref[...] + _mm(vd, v2_ref[...], passes=3))
    else:
        vf32_ref[0] = v
    kk = k * kk_ref[...]
    ss = _mm_exact_rhs(kk * kk, hsum_ref[...])
    inv = 1.0 / jnp.maximum(jnp.sqrt(ss), 1e-12)
    kk = kk * _mm_exact_rhs(inv, hbc_ref[...])
    r_ref[0] = r.astype(r_ref.dtype)
    k_ref[0] = (k * (1.0 + (a - 1.0) * ka_ref[...])).astype(k_ref.dtype)
    v_ref[0] = v.astype(v_ref.dtype)
    an_ref[0] = (-kk).astype(an_ref.dtype)
    b_ref[0] = (kk * a).astype(b_ref.dtype)


def _rwkv_prep(z3, c, mu, w0, w2, a0, a2, g2, k_k, k_a, hsum, hbc, vres, tm=256):
    bsz, s, _ = z3.shape
    n_lora = w2.shape[0] + a2.shape[0] + g2.shape[0]
    assert s % tm == 0 and (3 * c) % n_lora == 0 and n_lora % 128 == 0
    has_vres = vres is not None
    row = lambda a: a.reshape(1, -1)
    hb = tm // 8
    lcb = 3 * c // n_lora
    halo = lambda b, i: (b, jnp.maximum(i * hb - 1, 0), 0)
    halo_l = lambda b, i: (b, jnp.maximum(i * hb - 1, 0), lcb)
    full = lambda a: pl.BlockSpec(a.shape, lambda b, i: (0,) * a.ndim)
    ins = [z3, z3, z3, z3, row(mu[:3 * c]), row(mu[3 * c:]), row(w0), w2, row(a0), a2, g2, row(k_k), row(k_a),
           hsum, hbc]
    in_specs = [pl.BlockSpec((1, tm, 3 * c), lambda b, i: (b, i, 0)),
                pl.BlockSpec((1, 8, 3 * c), halo),
                pl.BlockSpec((1, tm, n_lora), lambda b, i: (b, i, lcb)),
                pl.BlockSpec((1, 8, n_lora), halo_l)] + [full(a) for a in ins[4:]]
    if has_vres:
        zv3, mu_v, v_first, v0, v2 = vres
        nv = zv3.shape[-1]
        extra = [zv3, zv3, row(mu_v), v_first, row(v0), v2]
        ins += extra
        in_specs += [pl.BlockSpec((1, tm, nv), lambda b, i: (b, i, 0)), pl.BlockSpec((1, 8, nv), halo),
                     full(extra[2]), pl.BlockSpec((1, tm, c), lambda b, i: (b, i, 0)), full(extra[4]),
                     full(extra[5])]
    out_spec = pl.BlockSpec((1, tm, c), lambda b, i: (b, i, 0))
    sd = lambda dt: jax.ShapeDtypeStruct((bsz, s, c), dt)
    out_dtypes = [BF16, F32, BF16, BF16, BF16, BF16, BF16] + ([] if has_vres else [F32])
    return pl.pallas_call(
        functools.partial(_rwkv_prep_kernel, has_vres),
        grid=(bsz, s // tm),
        in_specs=in_specs,
        out_specs=[out_spec] * len(out_dtypes),
        out_shape=[sd(dt) for dt in out_dtypes],
        compiler_params=_cparams("parallel", "arbitrary"),
        name="rwkv_prep",
    )(*ins)


def _mm_heads(a_list, b_list, dims=_NN, passes=3):
    d = lambda p, q: lax.dot_general(p, q, dims, preferred_element_type=F32)
    if passes == 1:
        return [d(a.astype(BF16), b.astype(BF16)) for a, b in zip(a_list, b_list)]
    sa = [_split2(a) for a in a_list]
    sb = [_split2(b) for b in b_list]
    out = [d(x[0], y[0]) for x, y in zip(sa, sb)]
    out = [o + d(x[0], y[1]) for o, x, y in zip(out, sa, sb)]
    return [o + d(x[1], y[0]) for o, x, y in zip(out, sa, sb)]


def _unit_lower_inverse(a_list, rows, cols):
    n = rows.shape[0]
    lower = rows > cols
    base = lower & ((rows >> 1) == (cols >> 1))
    t = [jnp.where(rows == cols, 1.0, jnp.where(base, a, 0.0)) for a in a_list]
    sh = 1
    while (2 << sh) <= n:
        sub = lower & ((rows >> (sh + 1)) == (cols >> (sh + 1))) & ((rows >> sh) != (cols >> sh))
        off = [jnp.where(sub, a, 0.0) for a in a_list]
        upd = _mm_heads(_mm_heads(t, off, passes=P_INV), t, passes=P_INV)
        t = [x + u for x, u in zip(t, upd)]
        sh += 1
    return t


def _wkv_kernel(r_ref, lw_ref, k_ref, v_ref, an_ref, b_ref, g_ref, rk_ref, lg_ref, lb_ref, y_ref, state_ref):
    ln = r_ref.shape[1]
    hg = r_ref.shape[2] // RWKV_HEAD
    n = RWKV_HEAD
    sls = [slice(h * n, (h + 1) * n) for h in range(hg)]

    @pl.when(pl.program_id(2) == 0)
    def _():
        state_ref[...] = jnp.zeros_like(state_ref)

    rows = lax.broadcasted_iota(jnp.int32, (ln, ln), 0)
    cols = lax.broadcasted_iota(jnp.int32, (ln, ln), 1)
    tril = jnp.where(rows >= cols, 1.0, 0.0).astype(BF16)
    lw = lw_ref[0]
    cw = _mm_exact_rhs_left(tril, lw)
    cw_end = cw[ln - 1:ln, :]
    e_neg = jnp.exp(-cw)
    w_end = jnp.exp(cw_end)
    r = r_ref[0].astype(F32)
    k = k_ref[0].astype(F32)
    bb = b_ref[0].astype(F32)
    v = v_ref[0].astype(F32)
    r_t = r * jnp.exp(cw)
    a_t = an_ref[0].astype(F32) * jnp.exp(cw - lw)
    b_t = bb * e_neg
    k_t = k * e_neg
    e_end = jnp.exp(cw_end - cw)
    b_h = bb * e_end
    k_h = k * e_end
    eye_n = lax.broadcasted_iota(jnp.int32, (n, n), 0) == lax.broadcasted_iota(jnp.int32, (n, n), 1)
    zeros = jnp.zeros((ln, n), F32)

    ar = [jnp.concatenate([a_t[:, s], r_t[:, s]], axis=0) for s in sls]
    bk = [jnp.concatenate([b_t[:, s], k_t[:, s]], axis=0) for s in sls]
    amat = _mm_heads(ar, bk, _NT, passes=P_A)
    strict = rows > cols
    incl = rows >= cols
    a_ab = [jnp.where(strict, m[:ln, :ln], 0.0) for m in amat]
    a_ak = [jnp.where(strict, m[:ln, ln:], 0.0) for m in amat]
    a_r = [jnp.concatenate([jnp.where(incl, m[ln:, :ln], 0.0), jnp.where(incl, m[ln:, ln:], 0.0)], axis=1)
           for m in amat]
    vh = [v[:, s] for s in sls]
    akv = _mm_heads(a_ak, vh, passes=P_PQ)
    t = _unit_lower_inverse(a_ab, rows, cols)
    rhs = [jnp.concatenate([a_t[:, s], x], axis=1) for s, x in zip(sls, akv)]
    pq = _mm_heads(t, rhs, passes=P_PQ)
    pqv = [jnp.concatenate([x, jnp.concatenate([zeros, u], axis=1)], axis=0)
           for x, u in zip(pq, vh)]
    ry = _mm_heads(a_r, pqv, passes=P_OUT)
    bkh = [jnp.concatenate([b_h[:, s], k_h[:, s]], axis=0) for s in sls]
    mn = _mm_heads(bkh, pqv, _TN, passes=P_OUT)
    lhs = [jnp.concatenate([r_t[:, s] + x[:, :n],
                            jnp.where(eye_n, jnp.broadcast_to(w_end[:, s], (n, n)), 0.0) + m[:, :n]], axis=0)
           for s, x, m in zip(sls, ry, mn)]
    st = [state_ref[h] for h in range(hg)]
    upd = _mm_heads(lhs, st, passes=P_STATE)
    for h in range(hg):
        state_ref[h] = upd[h][ln:, :] + mn[h][:, n:]
    yn = []
    for u, x in zip(upd, ry):
        yh = u[:ln, :] + x[:, n:]
        d = yh - jnp.mean(yh, axis=-1, keepdims=True)
        yn.append(d * lax.rsqrt(jnp.mean(d * d, axis=-1, keepdims=True) + LNX_EPS))
    rk = r * k * rk_ref[...]
    bonus = jnp.concatenate([jnp.sum(rk[:, s], axis=-1, keepdims=True) * v[:, s] for s in sls], axis=1)
    out = (jnp.concatenate(yn, axis=1) * lg_ref[...] + lb_ref[...] + bonus) * g_ref[0].astype(F32)
    y_ref[0] = out.astype(y_ref.dtype)


def _mm_exact_rhs_left(l_bf16, a):
    d = lambda p: lax.dot_general(l_bf16, p, _NN, preferred_element_type=F32)
    hi, mid, lo = _split3(a)
    return d(hi) + (d(mid) + d(lo))


def _wkv(r, lw, k, v, an, b, g, r_k, lnx_g, lnx_b, heads_per_step=16):
    bsz, s, c = r.shape
    wb = heads_per_step * RWKV_HEAD
    nc = s // WKV_CHUNK
    assert s % WKV_CHUNK == 0 and c % wb == 0
    spec = pl.BlockSpec((1, WKV_CHUNK, wb), lambda bi, hi, ci: (bi, ci, hi))
    pspec = pl.BlockSpec((1, wb), lambda bi, hi, ci: (0, hi))
    row = lambda a: a.reshape(1, -1)
    return pl.pallas_call(
        _wkv_kernel,
        grid=(bsz, c // wb, nc),
        in_specs=[spec] * 7 + [pspec] * 3,
        out_specs=spec,
        out_shape=jax.ShapeDtypeStruct((bsz, s, c), BF16),
        scratch_shapes=[pltpu.VMEM((heads_per_step, RWKV_HEAD, RWKV_HEAD), F32)],
        compiler_params=_cparams("parallel", "parallel", "arbitrary"),
        name="wkv",
    )(r, lw, k, v, an, b, g, row(r_k), row(lnx_g), row(lnx_b))


def _rope_kernel(q_ref, k_ref, v_ref, cos_ref, sin_ref, qt_ref, ko_ref, vt_ref, km_ref):
    cos = cos_ref[...]
    sin = sin_ref[...]
    nh = q_ref.shape[-1] // ATTN_HEAD
    inv_rows = 1.0 / q_ref.shape[1]
    for h in range(nh):
        sl = slice(h * ATTN_HEAD, (h + 1) * ATTN_HEAD)
        q = q_ref[0, :, sl]
        k = k_ref[0, :, sl]
        qr = q * cos + pltpu.roll(q, ATTN_HEAD // 2, axis=1) * sin
        kr = k * cos + pltpu.roll(k, ATTN_HEAD // 2, axis=1) * sin
        qt_ref[0, h, 0] = qr.T
        ko_ref[0, h, 0] = kr.astype(BF16)
        vt_ref[0, h, 0] = v_ref[0, :, sl].T.astype(BF16)
        km_ref[0, 0, :, sl] = jnp.sum(kr, axis=0, keepdims=True) * inv_rows


def _rope(z3, col0, width, cos2, sin2):
    bsz, s, _ = z3.shape
    assert s % MOBA_BLOCK == 0 and col0 % ATTN_HEAD == 0
    nb = s // MOBA_BLOCK
    nh = width // ATTN_HEAD
    blk = lambda j: pl.BlockSpec((pl.Element(1), pl.Element(MOBA_BLOCK), pl.Element(width)),
                                 lambda b, i: (b, i * MOBA_BLOCK, col0 + j * width))
    tab = pl.BlockSpec((MOBA_BLOCK, ATTN_HEAD), lambda b, i: (i, 0))
    t_spec = pl.BlockSpec((1, nh, 1, ATTN_HEAD, MOBA_BLOCK), lambda b, i: (b, 0, i, 0, 0))
    n_spec = pl.BlockSpec((1, nh, 1, MOBA_BLOCK, ATTN_HEAD), lambda b, i: (b, 0, i, 0, 0))
    return pl.pallas_call(
        _rope_kernel,
        grid=(bsz, nb),
        in_specs=[blk(0), blk(1), blk(2), tab, tab],
        out_specs=[t_spec, n_spec, t_spec, pl.BlockSpec((1, 1, 1, width), lambda b, i: (b, i, 0, 0))],
        out_shape=[jax.ShapeDtypeStruct((bsz, nh, nb, ATTN_HEAD, MOBA_BLOCK), F32),
                   jax.ShapeDtypeStruct((bsz, nh, nb, MOBA_BLOCK, ATTN_HEAD), BF16),
                   jax.ShapeDtypeStruct((bsz, nh, nb, ATTN_HEAD, MOBA_BLOCK), BF16),
                   jax.ShapeDtypeStruct((bsz, nb, 1, width), F32)],
        compiler_params=_cparams("parallel", "parallel"),
        name="rope",
    )(z3, z3, z3, cos2, sin2)


def _moba_kernel(qt_ref, k_ref, vt_ref, km_ref, o_ref, bias_ref):
    blk = MOBA_BLOCK
    dh = ATTN_HEAD
    hs = range(qt_ref.shape[1])
    qb = pl.program_id(2)
    nb = km_ref.shape[1]
    scale = ATTN_HEAD ** -0.5
    neg = -jnp.inf
    qt = [qt_ref[0, h, 0] for h in hs]
    blk_id = lax.broadcasted_iota(jnp.int32, (nb, blk), 0)
    past = blk_id < qb
    gate = [jnp.where(past, _mm(km_ref[0, :, h * dh:(h + 1) * dh], qt[h], passes=3), neg) for h in hs]
    for h in hs:
        rank = jnp.zeros((nb, blk), jnp.int32)
        for m in range(nb):
            gm = gate[h][m:m + 1, :]
            rank += ((gm > gate[h]) | ((gm == gate[h]) & (m < blk_id))).astype(jnp.int32)
        bias_ref[h] = jnp.where(past & (rank < MOBA_TOPK), 0.0, neg)

    qs = [(q * scale).astype(BF16) for q in qt]
    ki = lax.broadcasted_iota(jnp.int32, (blk, blk), 0)
    qi = lax.broadcasted_iota(jnp.int32, (blk, blk), 1)
    s_own = [jnp.where(ki <= qi, jnp.dot(k_ref[0, h, qb], qs[h], preferred_element_type=F32), neg) for h in hs]
    m0 = [jnp.max(s, axis=0, keepdims=True) for s in s_own]
    p0 = [jnp.exp(s - m) for s, m in zip(s_own, m0)]
    l0 = [jnp.sum(p, axis=0, keepdims=True) for p in p0]
    acc0 = [jnp.dot(vt_ref[0, h, qb], p0[h].astype(BF16), preferred_element_type=F32) for h in hs]

    def body(kb, carry):
        m_prev, l_prev, acc = carry
        s = [jnp.dot(k_ref[0, h, kb], qs[h], preferred_element_type=F32) + bias_ref[h, pl.ds(kb, 1), :] for h in hs]
        m_new = [jnp.maximum(mp, jnp.max(x, axis=0, keepdims=True)) for mp, x in zip(m_prev, s)]
        alpha = [jnp.exp(mp - mn) for mp, mn in zip(m_prev, m_new)]
        p = [jnp.exp(x - mn) for x, mn in zip(s, m_new)]
        l_new = [a * lp + jnp.sum(x, axis=0, keepdims=True) for a, lp, x in zip(alpha, l_prev, p)]
        pv = [jnp.dot(vt_ref[0, h, kb], p[h].astype(BF16), preferred_element_type=F32) for h in hs]
        acc = [a * c + x for a, c, x in zip(alpha, acc, pv)]
        return m_new, l_new, acc

    _, l_fin, acc = lax.fori_loop(0, qb, body, (m0, l0, acc0))
    for h in hs:
        o_ref[0, :, h * dh:(h + 1) * dh] = (acc[h] / l_fin[h]).T.astype(o_ref.dtype)


def _moba(qt, k, vt, kmean, heads_per_step=8):
    bsz, nh, nb, dh, blk = qt.shape
    hp = heads_per_step
    assert nh % hp == 0
    return pl.pallas_call(
        _moba_kernel,
        grid=(bsz, nh // hp, nb),
        in_specs=[pl.BlockSpec((1, hp, 1, dh, blk), lambda b, h, i: (b, h, i, 0, 0)),
                  pl.BlockSpec((1, hp, nb, blk, dh), lambda b, h, i: (b, h, 0, 0, 0)),
                  pl.BlockSpec((1, hp, nb, dh, blk), lambda b, h, i: (b, h, 0, 0, 0)),
                  pl.BlockSpec((1, nb, hp * dh), lambda b, h, i: (b, 0, h))],
        out_specs=pl.BlockSpec((1, blk, hp * dh), lambda b, h, i: (b, i, h)),
        out_shape=jax.ShapeDtypeStruct((bsz, nb * blk, nh * dh), BF16),
        scratch_shapes=[pltpu.VMEM((hp, nb, blk), F32)],
        compiler_params=_cparams("parallel", "parallel", "arbitrary"),
        name="moba",
    )(qt, k, vt, kmean)


def _out_proj_kernel(yr_ref, ya_ref, wr_ref, wa_ref, x_ref, g_ref, o_ref):
    y = jnp.dot(yr_ref[...], wr_ref[...], preferred_element_type=F32)
    y += jnp.dot(ya_ref[...], wa_ref[...], preferred_element_type=F32)
    ms = jnp.mean(y * y, axis=-1, keepdims=True)
    o_ref[...] = x_ref[...] + y * lax.rsqrt(ms + NORM_EPS) * g_ref[...]


def _out_proj(y_r, y_a, w_all, layer, x2, gain, tm=256):
    m, d = x2.shape
    cw = y_r.shape[1]
    assert y_a.shape[1] == cw and w_all.shape[1] == 2 * cw and m % tm == 0
    return pl.pallas_call(
        _out_proj_kernel,
        grid=(m // tm,),
        in_specs=[pl.BlockSpec((tm, cw), lambda i: (i, 0)), pl.BlockSpec((tm, cw), lambda i: (i, 0)),
                  pl.BlockSpec((None, cw, d), lambda i: (layer, 0, 0)),
                  pl.BlockSpec((None, cw, d), lambda i: (layer, 1, 0)),
                  pl.BlockSpec((tm, d), lambda i: (i, 0)), pl.BlockSpec((1, d), lambda i: (0, 0))],
        out_specs=pl.BlockSpec((tm, d), lambda i: (i, 0)),
        out_shape=jax.ShapeDtypeStruct((m, d), F32),
        compiler_params=_cparams("parallel"),
        name="out_proj",
    )(y_r, y_a, w_all, w_all, x2, gain.reshape(1, d))


def _mlp_kernel(x_ref, gpre_ref, wu_ref, wd_ref, gpost_ref, o_ref, h_ref, acc_ref):
    f = pl.program_id(1)

    @pl.when(f == 0)
    def _():
        x = x_ref[...]
        ms = jnp.mean(x * x, axis=-1, keepdims=True)
        h_ref[...] = (x * lax.rsqrt(ms + NORM_EPS) * gpre_ref[...]).astype(BF16)
        acc_ref[...] = jnp.zeros_like(acc_ref)

    u = jnp.maximum(jnp.dot(h_ref[...], wu_ref[...], preferred_element_type=F32), 0.0)
    acc_ref[...] += jnp.dot((u * u).astype(BF16), wd_ref[...], preferred_element_type=F32)

    @pl.when(f == pl.num_programs(1) - 1)
    def _():
        mlp = acc_ref[...]
        ms = jnp.mean(mlp * mlp, axis=-1, keepdims=True)
        o_ref[...] = x_ref[...] + mlp * lax.rsqrt(ms + NORM_EPS) * gpost_ref[...]


def _mlp(x2, g_pre, w_up, w_down, layer, g_post, tm=512, tf=1024):
    m, d = x2.shape
    dff = w_up.shape[2]
    assert m % tm == 0 and dff % tf == 0
    return pl.pallas_call(
        _mlp_kernel,
        grid=(m // tm, dff // tf),
        in_specs=[pl.BlockSpec((tm, d), lambda i, f: (i, 0)),
                  pl.BlockSpec((1, d), lambda i, f: (0, 0)),
                  pl.BlockSpec((None, d, tf), lambda i, f: (layer, 0, f)),
                  pl.BlockSpec((None, tf, d), lambda i, f: (layer, f, 0)),
                  pl.BlockSpec((1, d), lambda i, f: (0, 0))],
        out_specs=pl.BlockSpec((tm, d), lambda i, f: (i, 0)),
        out_shape=jax.ShapeDtypeStruct((m, d), F32),
        scratch_shapes=[pltpu.VMEM((tm, d), BF16), pltpu.VMEM((tm, d), F32)],
        compiler_params=_cparams("parallel", "arbitrary"),
        name="mlp",
    )(x2, g_pre.reshape(1, d), w_up, w_down, g_post.reshape(1, d))


def _head_indicators(c):
    head = jnp.arange(c) // RWKV_HEAD
    hsum = (head[:, None] == jnp.arange(128)[None, :]).astype(BF16)
    return hsum, hsum.T


def _rope_tables(s):
    half = ATTN_HEAD // 2
    inv_freq = ROPE_THETA ** (-jnp.arange(half, dtype=F32) / half)
    ang = jnp.arange(s).astype(F32)[:, None] * inv_freq[None, :]
    cos, sin = jnp.cos(ang), jnp.sin(ang)
    return jnp.concatenate([cos, cos], axis=-1), jnp.concatenate([-sin, sin], axis=-1)


def kernel(x, norm_mix_pre, norm_mix_post, norm_mlp_pre, norm_mlp_post, w_in, w_in_vres, shift_mu, shift_mu_vres, decay_w0, decay_w2, iclr_a0, iclr_a2, vres_v0, vres_v2, gate_g2, k_k, k_a, r_k, lnx_gain, lnx_bias, w_out, w_up, w_down):
    bsz, s, d = x.shape
    depth = w_in.shape[0]
    c = decay_w0.shape[1]
    n_lora = DECAY_LORA + ICLR_LORA + GATE_LORA
    n_shift = 3 * c + n_lora
    ca = (w_in.shape[2] - n_shift) // 3
    hsum, hbc = _head_indicators(c)
    cos2, sin2 = _rope_tables(s)
    w_in16, w_out16, w_up16, w_down16 = (w.astype(BF16) for w in (w_in, w_out, w_up, w_down))
    pad_v = VRES_PAD - VRES_LORA
    x2 = x.reshape(bsz * s, d)
    v_first = None
    for i in range(depth):
        if i == 0:
            z, vres = _norm_matmul(x2, norm_mix_pre[i], w_in16, i)[0], None
        else:
            w_v = jnp.pad(w_in_vres[i - 1], ((0, 0), (0, pad_v))).astype(BF16)
            z, z_v = _norm_matmul(x2, norm_mix_pre[i], w_in16, i, w_v)
            vres = (z_v.reshape(bsz, s, VRES_PAD), jnp.pad(shift_mu_vres[i - 1], (0, pad_v)), v_first,
                    vres_v0[i - 1], jnp.pad(vres_v2[i - 1], ((0, pad_v), (0, 0))))
        z3 = z.reshape(bsz, s, -1)

        prep = _rwkv_prep(z3, c, shift_mu[i], decay_w0[i], decay_w2[i], iclr_a0[i], iclr_a2[i], gate_g2[i],
                          k_k[i], k_a[i], hsum, hbc, vres)
        if i == 0:
            v_first = prep[7]
        y_r = _wkv(*prep[:7], r_k[i].reshape(-1), lnx_gain[i], lnx_bias[i])

        q_t, k_rot, v_t, kmean = _rope(z3, n_shift, ca, cos2, sin2)
        y_a = _moba(q_t, k_rot, v_t, kmean.reshape(bsz, -1, ca))

        x2 = _out_proj(y_r.reshape(bsz * s, c), y_a.reshape(bsz * s, ca), w_out16, i, x2, norm_mix_post[i])
        x2 = _mlp(x2, norm_mlp_pre[i], w_up16, w_down16, i, norm_mlp_post[i])
    return x2.reshape(bsz, s, d)
```

```python
import functools

import jax
import jax.numpy as jnp
from jax import lax
from jax.experimental import pallas as pl
from jax.experimental.pallas import tpu as pltpu

F32 = jnp.float32
BF16 = jnp.bfloat16

RWKV_HEAD = 64
DECAY_LORA = 64
ICLR_LORA = 64
VRES_LORA = 32
GATE_LORA = 128
ATTN_HEAD = 128
MOBA_BLOCK = 256
MOBA_TOPK = 3
ROPE_THETA = 10000.0
NORM_EPS = 1e-6
LNX_EPS = 64e-5

VRES_PAD = 128
WKV_CHUNK = 64
VMEM_LIMIT = 56 * 1024 * 1024


def _cparams(*sem):
    return pltpu.CompilerParams(dimension_semantics=sem, vmem_limit_bytes=VMEM_LIMIT)


_NN = (((1,), (0,)), ((), ()))
_NT = (((1,), (1,)), ((), ()))
_TN = (((0,), (0,)), ((), ()))


def _split2(x):
    hi = x.astype(BF16)
    lo = (x - hi.astype(F32)).astype(BF16)
    return hi, lo


def _split3(x):
    hi = x.astype(BF16)
    r1 = x - hi.astype(F32)
    mid = r1.astype(BF16)
    lo = (r1 - mid.astype(F32)).astype(BF16)
    return hi, mid, lo


def _mm(a, b, dims=_NN, passes=1):
    d = lambda p, q: lax.dot_general(p, q, dims, preferred_element_type=F32)
    if passes == 1:
        return d(a.astype(BF16), b.astype(BF16))
    ah, al = _split2(a)
    bh, bl = _split2(b)
    return d(ah, bh) + (d(ah, bl) + d(al, bh))


def _mm_exact_rhs(a, b_bf16, dims=_NN):
    d = lambda p: lax.dot_general(p, b_bf16, dims, preferred_element_type=F32)
    hi, mid, lo = _split3(a)
    return d(hi) + (d(mid) + d(lo))


def _sigmoid(x):
    return 1.0 / (1.0 + jnp.exp(-x))


def _softplus(x):
    return jnp.maximum(x, 0.0) + jnp.log(1.0 + jnp.exp(-jnp.abs(x)))


def _norm_matmul_kernel(has_extra, *refs):
    if has_extra:
        x_ref, g_ref, w_ref, we_ref, o_ref, oe_ref, h_ref = refs
    else:
        x_ref, g_ref, w_ref, o_ref, h_ref = refs

    @pl.when(pl.program_id(1) == 0)
    def _():
        x = x_ref[...]
        ms = jnp.mean(x * x, axis=-1, keepdims=True)
        h_ref[...] = (x * lax.rsqrt(ms + NORM_EPS) * g_ref[...]).astype(BF16)
        if has_extra:
            oe_ref[...] = jnp.dot(h_ref[...], we_ref[...], preferred_element_type=F32)

    o_ref[...] = jnp.dot(h_ref[...], w_ref[...], preferred_element_type=F32)


def _norm_matmul(x2, gain, w_all, layer, w_extra=None, tm=1024, tn=1280):
    m, d = x2.shape
    n = w_all.shape[2]
    assert m % tm == 0 and n % tn == 0
    has_extra = w_extra is not None
    ins = [x2, gain.reshape(1, d), w_all]
    in_specs = [pl.BlockSpec((tm, d), lambda i, j: (i, 0)),
                pl.BlockSpec((1, d), lambda i, j: (0, 0)),
                pl.BlockSpec((None, d, tn), lambda i, j: (layer, 0, j))]
    out_specs = [pl.BlockSpec((tm, tn), lambda i, j: (i, j))]
    out_shape = [jax.ShapeDtypeStruct((m, n), F32)]
    if has_extra:
        ne = w_extra.shape[1]
        ins.append(w_extra)
        in_specs.append(pl.BlockSpec((d, ne), lambda i, j: (0, 0)))
        out_specs.append(pl.BlockSpec((tm, ne), lambda i, j: (i, 0)))
        out_shape.append(jax.ShapeDtypeStruct((m, ne), F32))
    return pl.pallas_call(
        functools.partial(_norm_matmul_kernel, has_extra),
        grid=(m // tm, n // tn),
        in_specs=in_specs,
        out_specs=out_specs,
        out_shape=out_shape,
        scratch_shapes=[pltpu.VMEM((tm, d), BF16)],
        compiler_params=_cparams("parallel", "arbitrary"),
        name="norm_matmul",
    )(*ins)


def _shifted(cur, halo, mu, first):
    rows = lax.broadcasted_iota(jnp.int32, cur.shape, 0)
    last = jnp.where(first, 0.0, halo[7:8, :])
    prev = jnp.where(rows == 0, last, pltpu.roll(cur, 1, axis=0))
    return cur + (prev - cur) * mu


def _rwkv_prep_kernel(has_vres, *refs):
    if has_vres:
        (zm_ref, zmh_ref, zl_ref, zlh_ref, mum_ref, mul_ref, w0_ref, w2_ref, a0_ref, a2_ref, g2_ref,
         kk_ref, ka_ref, hsum_ref, hbc_ref, zv_ref, zvh_ref, muv_ref, vf_ref, v0_ref, v2_ref,
         r_ref, lw_ref, k_ref, v_ref, an_ref, b_ref, g_ref) = refs
    else:
        (zm_ref, zmh_ref, zl_ref, zlh_ref, mum_ref, mul_ref, w0_ref, w2_ref, a0_ref, a2_ref, g2_ref,
         kk_ref, ka_ref, hsum_ref, hbc_ref,
         r_ref, lw_ref, k_ref, v_ref, an_ref, b_ref, g_ref, vf32_ref) = refs
    c = r_ref.shape[-1]
    first = pl.program_id(1) == 0
    zs = _shifted(zm_ref[0], zmh_ref[0], mum_ref[...], first)
    zl = _shifted(zl_ref[0], zlh_ref[0], mul_ref[...], first)
    r, k, v = zs[:, :c], zs[:, c:2 * c], zs[:, 2 * c:3 * c]
    o = 0
    wd = zl[:, o:o + DECAY_LORA]
    o += DECAY_LORA
    ad = zl[:, o:o + ICLR_LORA]
    o += ICLR_LORA
    gd = zl[:, o:o + GATE_LORA]
    o += GATE_LORA
    w_log = -_softplus(-(w0_ref[...] + _mm(jnp.tanh(wd), w2_ref[...], passes=3))) - 0.5
    lw_ref[0] = -jnp.exp(w_log)
    a = _sigmoid(a0_ref[...] + _mm(ad, a2_ref[...], passes=3))
    g_ref[0] = _mm(_sigmoid(gd), g2_ref[...], passes=3).astype(g_ref.dtype)
    if has_vres:
        vd = _shifted(zv_ref[0], zvh_ref[0], muv_ref[...], first)
        v = v + (vf_ref[0] - v) * _sigmoid(v0_ref[...] + _mm(vd, v2_ref[...], passes=3))
    else:
        vf32_ref[0] = v
    kk = k * kk_ref[...]
    ss = _mm_exact_rhs(kk * kk, hsum_ref[...])
    inv = 1.0 / jnp.maximum(jnp.sqrt(ss), 1e-12)
    kk = kk * _mm_exact_rhs(inv, hbc_ref[...])
    r_ref[0] = r.astype(r_ref.dtype)
    k_ref[0] = (k * (1.0 + (a - 1.0) * ka_ref[...])).astype(k_ref.dtype)
    v_ref[0] = v.astype(v_ref.dtype)
    an_ref[0] = (-kk).astype(an_ref.dtype)
    b_ref[0] = (kk * a).astype(b_ref.dtype)


def _rwkv_prep(z3, c, mu, w0, w2, a0, a2, g2, k_k, k_a, hsum, hbc, vres, tm=256):
    bsz, s, _ = z3.shape
    n_lora = w2.shape[0] + a2.shape[0] + g2.shape[0]
    assert s % tm == 0 and (3 * c) % n_lora == 0 and n_lora % 128 == 0
    has_vres = vres is not None
    row = lambda a: a.reshape(1, -1)
    hb = tm // 8
    lcb = 3 * c // n_lora
    halo = lambda b, i: (b, jnp.maximum(i * hb - 1, 0), 0)
    halo_l = lambda b, i: (b, jnp.maximum(i * hb - 1, 0), lcb)
    full = lambda a: pl.BlockSpec(a.shape, lambda b, i: (0,) * a.ndim)
    ins = [z3, z3, z3, z3, row(mu[:3 * c]), row(mu[3 * c:]), row(w0), w2, row(a0), a2, g2, row(k_k), row(k_a),
           hsum, hbc]
    in_specs = [pl.BlockSpec((1, tm, 3 * c), lambda b, i: (b, i, 0)),
                pl.BlockSpec((1, 8, 3 * c), halo),
                pl.BlockSpec((1, tm, n_lora), lambda b, i: (b, i, lcb)),
                pl.BlockSpec((1, 8, n_lora), halo_l)] + [full(a) for a in ins[4:]]
    if has_vres:
        zv3, mu_v, v_first, v0, v2 = vres
        nv = zv3.shape[-1]
        extra = [zv3, zv3, row(mu_v), v_first, row(v0), v2]
        ins += extra
        in_specs += [pl.BlockSpec((1, tm, nv), lambda b, i: (b, i, 0)), pl.BlockSpec((1, 8, nv), halo),
                     full(extra[2]), pl.BlockSpec((1, tm, c), lambda b, i: (b, i, 0)), full(extra[4]),
                     full(extra[5])]
    out_spec = pl.BlockSpec((1, tm, c), lambda b, i: (b, i, 0))
    sd = lambda dt: jax.ShapeDtypeStruct((bsz, s, c), dt)
    out_dtypes = [BF16, F32, BF16, BF16, BF16, BF16, BF16] + ([] if has_vres else [F32])
    return pl.pallas_call(
        functools.partial(_rwkv_prep_kernel, has_vres),
        grid=(bsz, s // tm),
        in_specs=in_specs,
        out_specs=[out_spec] * len(out_dtypes),
        out_shape=[sd(dt) for dt in out_dtypes],
        compiler_params=_cparams("parallel", "arbitrary"),
        name="rwkv_prep",
    )(*ins)


def _dots(a_list, b_list, dims=_NN):
    return [lax.dot_general(a.astype(BF16), b.astype(BF16), dims, preferred_element_type=F32)
            for a, b in zip(a_list, b_list)]


def _pair_diag(x, even):
    return jnp.concatenate([jnp.where(even, x, 0.0), jnp.where(even, 0.0, x)], axis=0)


def _unit_lower_inverse(a_list, row_w, col_w, even):
    n = row_w.shape[0]
    lower = row_w > col_w
    base = lower & ((row_w >> 1) == (col_w >> 1))
    t = [jnp.where(row_w == col_w, 1.0, jnp.where(base, a, 0.0)) for a in a_list]
    sh = 1
    while (2 << sh) <= n:
        sub = lower & ((row_w >> (sh + 1)) == (col_w >> (sh + 1))) & ((row_w >> sh) != (col_w >> sh))
        off = [_pair_diag(jnp.where(sub, a, 0.0), even) for a in a_list]
        upd = _dots(_dots(t, off), [_pair_diag(x, even) for x in t])
        t = [x + u for x, u in zip(t, upd)]
        sh += 1
    return t


def _mm_exact_rhs_left(l_bf16, a):
    d = lambda p: lax.dot_general(l_bf16, p, _NN, preferred_element_type=F32)
    hi, mid, lo = _split3(a)
    return d(hi) + (d(mid) + d(lo))


def _wkv_kernel(r_ref, lw_ref, k_ref, v_ref, an_ref, b_ref, g_ref, rk_ref, lg_ref, lb_ref, y_ref, state_ref):
    nb, ln = r_ref.shape[0], r_ref.shape[1]
    n = RWKV_HEAD
    pw = 2 * n
    npair = r_ref.shape[2] // pw
    ent = [(bi, slice(p * pw, (p + 1) * pw)) for bi in range(nb) for p in range(npair)]

    @pl.when(pl.program_id(1) == 0)
    def _():
        state_ref[...] = jnp.zeros_like(state_ref)

    rows = lax.broadcasted_iota(jnp.int32, (ln, ln), 0)
    cols = lax.broadcasted_iota(jnp.int32, (ln, ln), 1)
    tril = jnp.where(rows >= cols, 1.0, 0.0).astype(BF16)
    r, k, v, w_end, r_t, a_t, b_t, k_t, b_h, k_h = ([] for _ in range(10))
    for bi in range(nb):
        lw = lw_ref[bi]
        cw = _mm_exact_rhs_left(tril, lw)
        cw_end = cw[ln - 1:ln, :]
        e_neg = jnp.exp(-cw)
        e_end = jnp.exp(cw_end - cw)
        bb = b_ref[bi].astype(F32)
        r.append(r_ref[bi].astype(F32))
        k.append(k_ref[bi].astype(F32))
        v.append(v_ref[bi].astype(F32))
        w_end.append(jnp.exp(cw_end))
        r_t.append(r[bi] * jnp.exp(cw))
        a_t.append(an_ref[bi].astype(F32) * jnp.exp(cw - lw))
        b_t.append(bb * e_neg)
        k_t.append(k[bi] * e_neg)
        b_h.append(bb * e_end)
        k_h.append(k[bi] * e_end)

    lane = lax.broadcasted_iota(jnp.int32, (ln, pw), 1)
    row_w = lax.broadcasted_iota(jnp.int32, (ln, pw), 0)
    even = lane < n
    even2 = lax.broadcasted_iota(jnp.int32, (2 * ln, pw), 1) < n
    col_w = lane & (n - 1)
    strict_w = row_w > col_w
    incl_w = row_w >= col_w
    zeros_w = jnp.zeros((ln, pw), F32)
    diag = lambda x: _pair_diag(x, even)
    swap = lambda x: jnp.concatenate([x[x.shape[0] // 2:], x[:x.shape[0] // 2]], axis=0)

    ar_p = [jnp.concatenate([a_t[bi][:, ps], r_t[bi][:, ps]], axis=0) for bi, ps in ent]
    bk_p = [jnp.concatenate([b_t[bi][:, ps], k_t[bi][:, ps]], axis=0).astype(BF16) for bi, ps in ent]
    kb_p = [jnp.concatenate([k_t[bi][:, ps], b_t[bi][:, ps]], axis=0).astype(BF16) for bi, ps in ent]
    am_e = _dots([jnp.where(even2, x, 0.0) for x in ar_p], bk_p, _NT)
    am_o = _dots([jnp.where(even2, 0.0, x) for x in ar_p], kb_p, _NT)
    a_ab = [jnp.where(strict_w, jnp.where(even, e[:ln], o[:ln]), 0.0) for e, o in zip(am_e, am_o)]
    a_ak = [jnp.where(strict_w, jnp.where(even, o[:ln], e[:ln]), 0.0) for e, o in zip(am_e, am_o)]
    a_rb = [jnp.where(incl_w, jnp.where(even, e[ln:], o[ln:]), 0.0) for e, o in zip(am_e, am_o)]
    a_rk = [jnp.where(incl_w, jnp.where(even, o[ln:], e[ln:]), 0.0) for e, o in zip(am_e, am_o)]
    v_p = [v[bi][:, ps] for bi, ps in ent]
    v_d = [diag(x) for x in v_p]
    akv = _dots(a_ak, [swap(x) for x in v_d])
    t = _unit_lower_inverse(a_ab, row_w, col_w, even)
    rhs = [jnp.concatenate([diag(a_t[bi][:, ps]), diag(x)], axis=1) for (bi, ps), x in zip(ent, akv)]
    pq = _dots(t, rhs)
    ry = _dots([jnp.concatenate([x, y], axis=1) for x, y in zip(a_rb, a_rk)],
               [jnp.concatenate([jnp.concatenate([diag(x[:, :pw]), diag(x[:, pw:])], axis=1),
                                 jnp.concatenate([jnp.zeros((2 * ln, pw), F32), swap(u)], axis=1)], axis=0)
                for x, u in zip(pq, v_d)])
    pqv_p = [jnp.concatenate([x, jnp.concatenate([zeros_w, u], axis=1)], axis=0)
             for x, u in zip(pq, v_p)]
    bkh_p = [jnp.concatenate([b_h[bi][:, ps], k_h[bi][:, ps]], axis=0) for bi, ps in ent]
    mn_p = _dots(bkh_p, pqv_p, _TN)
    sq_r = lax.broadcasted_iota(jnp.int32, (pw, pw), 0)
    sq_c = lax.broadcasted_iota(jnp.int32, (pw, pw), 1)
    same_head = (sq_r < n) == (sq_c < n)
    lhs_p = [jnp.concatenate([r_t[bi][:, ps] + x[:, :pw],
                              jnp.where(sq_r == sq_c, jnp.broadcast_to(w_end[bi][:, ps], (pw, pw)),
                                        jnp.where(same_head, m[:, :pw], 0.0))], axis=0)
             for (bi, ps), x, m in zip(ent, ry, mn_p)]
    st = [state_ref[e] for e in range(len(ent))]
    upd = _dots(lhs_p, st)
    for e in range(len(ent)):
        state_ref[e] = upd[e][ln:, :] + jnp.where(same_head, mn_p[e][:, pw:], 0.0)
    y0_p = [x[:, pw:] for x in ry]

    hsum = lambda x: jnp.where(even, jnp.sum(jnp.where(even, x, 0.0), axis=-1, keepdims=True),
                               jnp.sum(jnp.where(even, 0.0, x), axis=-1, keepdims=True))
    inv_n = 1.0 / n
    yn = []
    for u, y0 in zip(upd, y0_p):
        y = u[:ln, :] + y0
        d = y - hsum(y) * inv_n
        yn.append(d * lax.rsqrt(hsum(d * d) * inv_n + LNX_EPS))
    for bi in range(nb):
        rk = r[bi] * k[bi] * rk_ref[...]
        mine = range(bi * npair, (bi + 1) * npair)
        bonus = jnp.concatenate([hsum(rk[:, ent[e][1]]) * v_p[e] for e in mine], axis=1)
        out = jnp.concatenate([yn[e] for e in mine], axis=1) * lg_ref[...] + lb_ref[...] + bonus
        y_ref[bi] = (out * g_ref[bi].astype(F32)).astype(y_ref.dtype)


def _wkv(r, lw, k, v, an, b, g, r_k, lnx_g, lnx_b, heads_per_step=16):
    bsz, s, c = r.shape
    wb = heads_per_step * RWKV_HEAD
    nc = s // WKV_CHUNK
    assert s % WKV_CHUNK == 0 and c % wb == 0 and heads_per_step % 2 == 0 and WKV_CHUNK == RWKV_HEAD
    spec = pl.BlockSpec((bsz, WKV_CHUNK, wb), lambda hi, ci: (0, ci, hi))
    pspec = pl.BlockSpec((1, wb), lambda hi, ci: (0, hi))
    row = lambda a: a.reshape(1, -1)
    return pl.pallas_call(
        _wkv_kernel,
        grid=(c // wb, nc),
        in_specs=[spec] * 7 + [pspec] * 3,
        out_specs=spec,
        out_shape=jax.ShapeDtypeStruct((bsz, s, c), BF16),
        scratch_shapes=[pltpu.VMEM((bsz * heads_per_step // 2, 2 * RWKV_HEAD, 2 * RWKV_HEAD), F32)],
        compiler_params=_cparams("parallel", "arbitrary"),
        name="wkv",
    )(r, lw, k, v, an, b, g, row(r_k), row(lnx_g), row(lnx_b))


def _rope_kernel(q_ref, k_ref, v_ref, cos_ref, sin_ref, qt_ref, ko_ref, vt_ref, km_ref):
    cos = cos_ref[...]
    sin = sin_ref[...]
    nh = q_ref.shape[-1] // ATTN_HEAD
    inv_rows = 1.0 / q_ref.shape[1]
    for h in range(nh):
        sl = slice(h * ATTN_HEAD, (h + 1) * ATTN_HEAD)
        q = q_ref[0, :, sl]
        k = k_ref[0, :, sl]
        qr = q * cos + pltpu.roll(q, ATTN_HEAD // 2, axis=1) * sin
        kr = k * cos + pltpu.roll(k, ATTN_HEAD // 2, axis=1) * sin
        qt_ref[0, h, 0] = qr.T
        ko_ref[0, h, 0] = kr.astype(BF16)
        vt_ref[0, h, 0] = v_ref[0, :, sl].T.astype(BF16)
        km_ref[0, 0, :, sl] = jnp.sum(kr, axis=0, keepdims=True) * inv_rows


def _rope(z3, col0, width, cos2, sin2):
    bsz, s, _ = z3.shape
    assert s % MOBA_BLOCK == 0 and col0 % ATTN_HEAD == 0
    nb = s // MOBA_BLOCK
    nh = width // ATTN_HEAD
    blk = lambda j: pl.BlockSpec((pl.Element(1), pl.Element(MOBA_BLOCK), pl.Element(width)),
                                 lambda b, i: (b, i * MOBA_BLOCK, col0 + j * width))
    tab = pl.BlockSpec((MOBA_BLOCK, ATTN_HEAD), lambda b, i: (i, 0))
    t_spec = pl.BlockSpec((1, nh, 1, ATTN_HEAD, MOBA_BLOCK), lambda b, i: (b, 0, i, 0, 0))
    n_spec = pl.BlockSpec((1, nh, 1, MOBA_BLOCK, ATTN_HEAD), lambda b, i: (b, 0, i, 0, 0))
    return pl.pallas_call(
        _rope_kernel,
        grid=(bsz, nb),
        in_specs=[blk(0), blk(1), blk(2), tab, tab],
        out_specs=[t_spec, n_spec, t_spec, pl.BlockSpec((1, 1, 1, width), lambda b, i: (b, i, 0, 0))],
        out_shape=[jax.ShapeDtypeStruct((bsz, nh, nb, ATTN_HEAD, MOBA_BLOCK), F32),
                   jax.ShapeDtypeStruct((bsz, nh, nb, MOBA_BLOCK, ATTN_HEAD), BF16),
                   jax.ShapeDtypeStruct((bsz, nh, nb, ATTN_HEAD, MOBA_BLOCK), BF16),
                   jax.ShapeDtypeStruct((bsz, nb, 1, width), F32)],
        compiler_params=_cparams("parallel", "parallel"),
        name="rope",
    )(z3, z3, z3, cos2, sin2)


def _moba_kernel(qt_ref, k_ref, vt_ref, km_ref, o_ref, bias_ref):
    blk = MOBA_BLOCK
    dh = ATTN_HEAD
    hs = range(qt_ref.shape[1])
    qb = pl.program_id(2)
    nb = km_ref.shape[1]
    scale = ATTN_HEAD ** -0.5
    neg = -jnp.inf
    qt = [qt_ref[0, h, 0] for h in hs]
    blk_id = lax.broadcasted_iota(jnp.int32, (nb, blk), 0)
    past = blk_id < qb
    gate = [jnp.where(past, _mm(km_ref[0, :, h * dh:(h + 1) * dh], qt[h], passes=3), neg) for h in hs]
    for h in hs:
        rank = jnp.zeros((nb, blk), jnp.int32)
        for m in range(nb):
            gm = gate[h][m:m + 1, :]
            rank += ((gm > gate[h]) | ((gm == gate[h]) & (m < blk_id))).astype(jnp.int32)
        bias_ref[h] = jnp.where(past & (rank < MOBA_TOPK), 0.0, neg)

    qs = [(q * scale).astype(BF16) for q in qt]
    ki = lax.broadcasted_iota(jnp.int32, (blk, blk), 0)
    qi = lax.broadcasted_iota(jnp.int32, (blk, blk), 1)
    s_own = [jnp.where(ki <= qi, jnp.dot(k_ref[0, h, qb], qs[h], preferred_element_type=F32), neg) for h in hs]
    m0 = [jnp.max(s, axis=0, keepdims=True) for s in s_own]
    p0 = [jnp.exp(s - m) for s, m in zip(s_own, m0)]
    l0 = [jnp.sum(p, axis=0, keepdims=True) for p in p0]
    acc0 = [jnp.dot(vt_ref[0, h, qb], p0[h].astype(BF16), preferred_element_type=F32) for h in hs]

    def body(kb, carry):
        m_prev, l_prev, acc = carry
        s = [jnp.dot(k_ref[0, h, kb], qs[h], preferred_element_type=F32) + bias_ref[h, pl.ds(kb, 1), :] for h in hs]
        m_new = [jnp.maximum(mp, jnp.max(x, axis=0, keepdims=True)) for mp, x in zip(m_prev, s)]
        alpha = [jnp.exp(mp - mn) for mp, mn in zip(m_prev, m_new)]
        p = [jnp.exp(x - mn) for x, mn in zip(s, m_new)]
        l_new = [a * lp + jnp.sum(x, axis=0, keepdims=True) for a, lp, x in zip(alpha, l_prev, p)]
        pv = [jnp.dot(vt_ref[0, h, kb], p[h].astype(BF16), preferred_element_type=F32) for h in hs]
        acc = [a * c + x for a, c, x in zip(alpha, acc, pv)]
        return m_new, l_new, acc

    _, l_fin, acc = lax.fori_loop(0, qb, body, (m0, l0, acc0))
    for h in hs:
        o_ref[0, :, h * dh:(h + 1) * dh] = (acc[h] / l_fin[h]).T.astype(o_ref.dtype)


def _moba(qt, k, vt, kmean, heads_per_step=8):
    bsz, nh, nb, dh, blk = qt.shape
    hp = heads_per_step
    assert nh % hp == 0
    return pl.pallas_call(
        _moba_kernel,
        grid=(bsz, nh // hp, nb),
        in_specs=[pl.BlockSpec((1, hp, 1, dh, blk), lambda b, h, i: (b, h, i, 0, 0)),
                  pl.BlockSpec((1, hp, nb, blk, dh), lambda b, h, i: (b, h, 0, 0, 0)),
                  pl.BlockSpec((1, hp, nb, dh, blk), lambda b, h, i: (b, h, 0, 0, 0)),
                  pl.BlockSpec((1, nb, hp * dh), lambda b, h, i: (b, 0, h))],
        out_specs=pl.BlockSpec((1, blk, hp * dh), lambda b, h, i: (b, i, h)),
        out_shape=jax.ShapeDtypeStruct((bsz, nb * blk, nh * dh), BF16),
        scratch_shapes=[pltpu.VMEM((hp, nb, blk), F32)],
        compiler_params=_cparams("parallel", "parallel", "arbitrary"),
        name="moba",
    )(qt, k, vt, kmean)


def _out_proj_kernel(yr_ref, ya_ref, wr_ref, wa_ref, x_ref, g_ref, o_ref):
    y = jnp.dot(yr_ref[...], wr_ref[...], preferred_element_type=F32)
    y += jnp.dot(ya_ref[...], wa_ref[...], preferred_element_type=F32)
    ms = jnp.mean(y * y, axis=-1, keepdims=True)
    o_ref[...] = x_ref[...] + y * lax.rsqrt(ms + NORM_EPS) * g_ref[...]


def _out_proj(y_r, y_a, w_all, layer, x2, gain, tm=256):
    m, d = x2.shape
    cw = y_r.shape[1]
    assert y_a.shape[1] == cw and w_all.shape[1] == 2 * cw and m % tm == 0
    return pl.pallas_call(
        _out_proj_kernel,
        grid=(m // tm,),
        in_specs=[pl.BlockSpec((tm, cw), lambda i: (i, 0)), pl.BlockSpec((tm, cw), lambda i: (i, 0)),
                  pl.BlockSpec((None, cw, d), lambda i: (layer, 0, 0)),
                  pl.BlockSpec((None, cw, d), lambda i: (layer, 1, 0)),
                  pl.BlockSpec((tm, d), lambda i: (i, 0)), pl.BlockSpec((1, d), lambda i: (0, 0))],
        out_specs=pl.BlockSpec((tm, d), lambda i: (i, 0)),
        out_shape=jax.ShapeDtypeStruct((m, d), F32),
        compiler_params=_cparams("parallel"),
        name="out_proj",
    )(y_r, y_a, w_all, w_all, x2, gain.reshape(1, d))


def _mlp_kernel(x_ref, gpre_ref, wu_ref, wd_ref, gpost_ref, o_ref, h_ref, acc_ref):
    f = pl.program_id(1)

    @pl.when(f == 0)
    def _():
        x = x_ref[...]
        ms = jnp.mean(x * x, axis=-1, keepdims=True)
        h_ref[...] = (x * lax.rsqrt(ms + NORM_EPS) * gpre_ref[...]).astype(BF16)
        acc_ref[...] = jnp.zeros_like(acc_ref)

    u = jnp.maximum(jnp.dot(h_ref[...], wu_ref[...], preferred_element_type=F32), 0.0)
    acc_ref[...] += jnp.dot((u * u).astype(BF16), wd_ref[...], preferred_element_type=F32)

    @pl.when(f == pl.num_programs(1) - 1)
    def _():
        mlp = acc_ref[...]
        ms = jnp.mean(mlp * mlp, axis=-1, keepdims=True)
        o_ref[...] = x_ref[...] + mlp * lax.rsqrt(ms + NORM_EPS) * gpost_ref[...]


def _mlp(x2, g_pre, w_up, w_down, layer, g_post, tm=512, tf=1024):
    m, d = x2.shape
    dff = w_up.shape[2]
    assert m % tm == 0 and dff % tf == 0
    return pl.pallas_call(
        _mlp_kernel,
        grid=(m // tm, dff // tf),
        in_specs=[pl.BlockSpec((tm, d), lambda i, f: (i, 0)),
                  pl.BlockSpec((1, d), lambda i, f: (0, 0)),
                  pl.BlockSpec((None, d, tf), lambda i, f: (layer, 0, f)),
                  pl.BlockSpec((None, tf, d), lambda i, f: (layer, f, 0)),
                  pl.BlockSpec((1, d), lambda i, f: (0, 0))],
        out_specs=pl.BlockSpec((tm, d), lambda i, f: (i, 0)),
        out_shape=jax.ShapeDtypeStruct((m, d), F32),
        scratch_shapes=[pltpu.VMEM((tm, d), BF16), pltpu.VMEM((tm, d), F32)],
        compiler_params=_cparams("parallel", "arbitrary"),
        name="mlp",
    )(x2, g_pre.reshape(1, d), w_up, w_down, g_post.reshape(1, d))


def _head_indicators(c):
    head = jnp.arange(c) // RWKV_HEAD
    hsum = (head[:, None] == jnp.arange(128)[None, :]).astype(BF16)
    return hsum, hsum.T


def _rope_tables(s):
    half = ATTN_HEAD // 2
    inv_freq = ROPE_THETA ** (-jnp.arange(half, dtype=F32) / half)
    ang = jnp.arange(s).astype(F32)[:, None] * inv_freq[None, :]
    cos, sin = jnp.cos(ang), jnp.sin(ang)
    return jnp.concatenate([cos, cos], axis=-1), jnp.concatenate([-sin, sin], axis=-1)


def kernel(x, norm_mix_pre, norm_mix_post, norm_mlp_pre, norm_mlp_post, w_in, w_in_vres, shift_mu, shift_mu_vres, decay_w0, decay_w2, iclr_a0, iclr_a2, vres_v0, vres_v2, gate_g2, k_k, k_a, r_k, lnx_gain, lnx_bias, w_out, w_up, w_down):
    bsz, s, d = x.shape
    depth = w_in.shape[0]
    c = decay_w0.shape[1]
    n_lora = DECAY_LORA + ICLR_LORA + GATE_LORA
    n_shift = 3 * c + n_lora
    ca = (w_in.shape[2] - n_shift) // 3
    hsum, hbc = _head_indicators(c)
    cos2, sin2 = _rope_tables(s)
    w_in16, w_out16, w_up16, w_down16 = (w.astype(BF16) for w in (w_in, w_out, w_up, w_down))
    pad_v = VRES_PAD - VRES_LORA
    x2 = x.reshape(bsz * s, d)
    v_first = None
    for i in range(depth):
        if i == 0:
            z, vres = _norm_matmul(x2, norm_mix_pre[i], w_in16, i)[0], None
        else:
            w_v = jnp.pad(w_in_vres[i - 1], ((0, 0), (0, pad_v))).astype(BF16)
            z, z_v = _norm_matmul(x2, norm_mix_pre[i], w_in16, i, w_v)
            vres = (z_v.reshape(bsz, s, VRES_PAD), jnp.pad(shift_mu_vres[i - 1], (0, pad_v)), v_first,
                    vres_v0[i - 1], jnp.pad(vres_v2[i - 1], ((0, pad_v), (0, 0))))
        z3 = z.reshape(bsz, s, -1)

        prep = _rwkv_prep(z3, c, shift_mu[i], decay_w0[i], decay_w2[i], iclr_a0[i], iclr_a2[i], gate_g2[i],
                          k_k[i], k_a[i], hsum, hbc, vres)
        if i == 0:
            v_first = prep[7]
        y_r = _wkv(*prep[:7], r_k[i].reshape(-1), lnx_gain[i], lnx_bias[i])

        q_t, k_rot, v_t, kmean = _rope(z3, n_shift, ca, cos2, sin2)
        y_a = _moba(q_t, k_rot, v_t, kmean.reshape(bsz, -1, ca))

        x2 = _out_proj(y_r.reshape(bsz * s, c), y_a.reshape(bsz * s, ca), w_out16, i, x2, norm_mix_post[i])
        x2 = _mlp(x2, norm_mlp_pre[i], w_up16, w_down16, i, norm_mlp_post[i])
    return x2.reshape(bsz, s, d)
```

```python
import functools

import jax
import jax.numpy as jnp
from jax import lax
from jax.experimental import pallas as pl
from jax.experimental.pallas import tpu as pltpu

F32 = jnp.float32
BF16 = jnp.bfloat16

RWKV_HEAD = 64
DECAY_LORA = 64
ICLR_LORA = 64
VRES_LORA = 32
GATE_LORA = 128
ATTN_HEAD = 128
MOBA_BLOCK = 256
MOBA_TOPK = 3
ROPE_THETA = 10000.0
NORM_EPS = 1e-6
LNX_EPS = 64e-5
LOG2E = 1.4426950408889634

VRES_PAD = 128
WKV_CHUNK = 64
MOBA_LOOKAHEAD = 6
MOBA_UNROLL = 4
VMEM_LIMIT = 56 * 1024 * 1024


def _cparams(*sem):
    return pltpu.CompilerParams(dimension_semantics=sem, vmem_limit_bytes=VMEM_LIMIT)


_NN = (((1,), (0,)), ((), ()))
_NT = (((1,), (1,)), ((), ()))
_TN = (((0,), (0,)), ((), ()))


def _split2(x):
    hi = x.astype(BF16)
    lo = (x - hi.astype(F32)).astype(BF16)
    return hi, lo


def _split3(x):
    hi = x.astype(BF16)
    r1 = x - hi.astype(F32)
    mid = r1.astype(BF16)
    lo = (r1 - mid.astype(F32)).astype(BF16)
    return hi, mid, lo


def _mm(a, b, dims=_NN, passes=1):
    d = lambda p, q: lax.dot_general(p, q, dims, preferred_element_type=F32)
    if passes == 1:
        return d(a.astype(BF16), b.astype(BF16))
    ah, al = _split2(a)
    bh, bl = _split2(b)
    return d(ah, bh) + (d(ah, bl) + d(al, bh))


def _mm_exact_rhs(a, b_bf16, dims=_NN):
    d = lambda p: lax.dot_general(p, b_bf16, dims, preferred_element_type=F32)
    hi, mid, lo = _split3(a)
    return d(hi) + (d(mid) + d(lo))


def _sigmoid(x):
    return 1.0 / (1.0 + jnp.exp(-x))


def _softplus(x):
    return jnp.maximum(x, 0.0) + jnp.log(1.0 + jnp.exp(-jnp.abs(x)))


def _norm_matmul_kernel(has_extra, *refs):
    if has_extra:
        x_ref, g_ref, w_ref, we_ref, o_ref, oe_ref, h_ref = refs
    else:
        x_ref, g_ref, w_ref, o_ref, h_ref = refs

    @pl.when(pl.program_id(1) == 0)
    def _():
        x = x_ref[...]
        ms = jnp.mean(x * x, axis=-1, keepdims=True)
        h_ref[...] = (x * lax.rsqrt(ms + NORM_EPS) * g_ref[...]).astype(BF16)
        if has_extra:
            oe_ref[...] = jnp.dot(h_ref[...], we_ref[...], preferred_element_type=F32)

    o_ref[...] = jnp.dot(h_ref[...], w_ref[...], preferred_element_type=F32)


def _norm_matmul(x2, gain, w_all, layer, w_extra=None, tm=1024, tn=1280):
    m, d = x2.shape
    n = w_all.shape[2]
    assert m % tm == 0 and n % tn == 0
    has_extra = w_extra is not None
    ins = [x2, gain.reshape(1, d), w_all]
    in_specs = [pl.BlockSpec((tm, d), lambda i, j: (i, 0)),
                pl.BlockSpec((1, d), lambda i, j: (0, 0)),
                pl.BlockSpec((None, d, tn), lambda i, j: (layer, 0, j))]
    out_specs = [pl.BlockSpec((tm, tn), lambda i, j: (i, j))]
    out_shape = [jax.ShapeDtypeStruct((m, n), F32)]
    if has_extra:
        ne = w_extra.shape[1]
        ins.append(w_extra)
        in_specs.append(pl.BlockSpec((d, ne), lambda i, j: (0, 0)))
        out_specs.append(pl.BlockSpec((tm, ne), lambda i, j: (i, 0)))
        out_shape.append(jax.ShapeDtypeStruct((m, ne), F32))
    return pl.pallas_call(
        functools.partial(_norm_matmul_kernel, has_extra),
        grid=(m // tm, n // tn),
        in_specs=in_specs,
        out_specs=out_specs,
        out_shape=out_shape,
        scratch_shapes=[pltpu.VMEM((tm, d), BF16)],
        compiler_params=_cparams("parallel", "arbitrary"),
        name="norm_matmul",
    )(*ins)


def _shifted(cur, halo, mu, first):
    rows = lax.broadcasted_iota(jnp.int32, cur.shape, 0)
    last = jnp.where(first, 0.0, halo[7:8, :])
    prev = jnp.where(rows == 0, last, pltpu.roll(cur, 1, axis=0))
    return cur + (prev - cur) * mu


def _rwkv_prep_kernel(has_vres, *refs):
    if has_vres:
        (zm_ref, zmh_ref, zl_ref, zlh_ref, mum_ref, mul_ref, w0_ref, w2_ref, a0_ref, a2_ref, g2_ref,
         kk_ref, ka_ref, hsum_ref, hbc_ref, zv_ref, zvh_ref, muv_ref, vf_ref, v0_ref, v2_ref,
         r_ref, lw_ref, k_ref, v_ref, an_ref, b_ref, g_ref) = refs
    else:
        (zm_ref, zmh_ref, zl_ref, zlh_ref, mum_ref, mul_ref, w0_ref, w2_ref, a0_ref, a2_ref, g2_ref,
         kk_ref, ka_ref, hsum_ref, hbc_ref,
         r_ref, lw_ref, k_ref, v_ref, an_ref, b_ref, g_ref, vf32_ref) = refs
    c = r_ref.shape[-1]
    first = pl.program_id(1) == 0
    zs = _shifted(zm_ref[0], zmh_ref[0], mum_ref[...], first)
    zl = _shifted(zl_ref[0], zlh_ref[0], mul_ref[...], first)
    r, k, v = zs[:, :c], zs[:, c:2 * c], zs[:, 2 * c:3 * c]
    o = 0
    wd = zl[:, o:o + DECAY_LORA]
    o += DECAY_LORA
    ad = zl[:, o:o + ICLR_LORA]
    o += ICLR_LORA
    gd = zl[:, o:o + GATE_LORA]
    o += GATE_LORA
    w_log = -_softplus(-(w0_ref[...] + _mm(jnp.tanh(wd), w2_ref[...], passes=3))) - 0.5
    lw_ref[0] = -jnp.exp(w_log)
    a = _sigmoid(a0_ref[...] + _mm(ad, a2_ref[...], passes=3))
    g_ref[0] = _mm(_sigmoid(gd), g2_ref[...], passes=3).astype(g_ref.dtype)
    if has_vres:
        vd = _shifted(zv_ref[0], zvh_ref[0], muv_ref[...], first)
        v = v + (vf_ref[0] - v) * _sigmoid(v0_ref[...] + _mm(vd, v2_ref[...], passes=3))
    else:
        vf32_ref[0] = v
    kk = k * kk_ref[...]
    ss = _mm_exact_rhs(kk * kk, hsum_ref[...])
    inv = 1.0 / jnp.maximum(jnp.sqrt(ss), 1e-12)
    kk = kk * _mm_exact_rhs(inv, hbc_ref[...])
    r_ref[0] = r.astype(r_ref.dtype)
    k_ref[0] = (k * (1.0 + (a - 1.0) * ka_ref[...])).astype(k_ref.dtype)
    v_ref[0] = v.astype(v_ref.dtype)
    an_ref[0] = (-kk).astype(an_ref.dtype)
    b_ref[0] = (kk * a).astype(b_ref.dtype)


def _rwkv_prep(z3, c, mu, w0, w2, a0, a2, g2, k_k, k_a, hsum, hbc, vres, tm=256):
    bsz, s, _ = z3.shape
    n_lora = w2.shape[0] + a2.shape[0] + g2.shape[0]
    assert s % tm == 0 and (3 * c) % n_lora == 0 and n_lora % 128 == 0
    has_vres = vres is not None
    row = lambda a: a.reshape(1, -1)
    hb = tm // 8
    lcb = 3 * c // n_lora
    halo = lambda b, i: (b, jnp.maximum(i * hb - 1, 0), 0)
    halo_l = lambda b, i: (b, jnp.maximum(i * hb - 1, 0), lcb)
    full = lambda a: pl.BlockSpec(a.shape, lambda b, i: (0,) * a.ndim)
    ins = [z3, z3, z3, z3, row(mu[:3 * c]), row(mu[3 * c:]), row(w0), w2, row(a0), a2, g2, row(k_k), row(k_a),
           hsum, hbc]
    in_specs = [pl.BlockSpec((1, tm, 3 * c), lambda b, i: (b, i, 0)),
                pl.BlockSpec((1, 8, 3 * c), halo),
                pl.BlockSpec((1, tm, n_lora), lambda b, i: (b, i, lcb)),
                pl.BlockSpec((1, 8, n_lora), halo_l)] + [full(a) for a in ins[4:]]
    if has_vres:
        zv3, mu_v, v_first, v0, v2 = vres
        nv = zv3.shape[-1]
        extra = [zv3, zv3, row(mu_v), v_first, row(v0), v2]
        ins += extra
        in_specs += [pl.BlockSpec((1, tm, nv), lambda b, i: (b, i, 0)), pl.BlockSpec((1, 8, nv), halo),
                     full(extra[2]), pl.BlockSpec((1, tm, c), lambda b, i: (b, i, 0)), full(extra[4]),
                     full(extra[5])]
    out_spec = pl.BlockSpec((1, tm, c), lambda b, i: (b, i, 0))
    sd = lambda dt: jax.ShapeDtypeStruct((bsz, s, c), dt)
    out_dtypes = [BF16, F32, BF16, BF16, BF16, BF16, BF16] + ([] if has_vres else [F32])
    return pl.pallas_call(
        functools.partial(_rwkv_prep_kernel, has_vres),
        grid=(bsz, s // tm),
        in_specs=in_specs,
        out_specs=[out_spec] * len(out_dtypes),
        out_shape=[sd(dt) for dt in out_dtypes],
        compiler_params=_cparams("parallel", "arbitrary"),
        name="rwkv_prep",
    )(*ins)


def _dots(a_list, b_list, dims=_NN):
    return [lax.dot_general(a.astype(BF16), b.astype(BF16), dims, preferred_element_type=F32)
            for a, b in zip(a_list, b_list)]


def _pair_diag(x, even):
    return jnp.concatenate([jnp.where(even, x, 0.0), jnp.where(even, 0.0, x)], axis=0)


def _unit_lower_inverse(a_list, row_w, col_w, even):
    n = row_w.shape[0]
    lower = row_w > col_w
    base = lower & ((row_w >> 1) == (col_w >> 1))
    t = [jnp.where(row_w == col_w, 1.0, jnp.where(base, a, 0.0)) for a in a_list]
    sh = 1
    while (2 << sh) <= n:
        sub = lower & ((row_w >> (sh + 1)) == (col_w >> (sh + 1))) & ((row_w >> sh) != (col_w >> sh))
        off = [_pair_diag(jnp.where(sub, a, 0.0), even) for a in a_list]
        upd = _dots(_dots(t, off), [_pair_diag(x, even) for x in t])
        t = [x + u for x, u in zip(t, upd)]
        sh += 1
    return t


def _mm_exact_rhs_left(l_bf16, a):
    d = lambda p: lax.dot_general(l_bf16, p, _NN, preferred_element_type=F32)
    hi, mid, lo = _split3(a)
    return d(hi) + (d(mid) + d(lo))


def _wkv_kernel(r_ref, lw_ref, k_ref, v_ref, an_ref, b_ref, g_ref, rk_ref, lg_ref, lb_ref, y_ref, state_ref):
    nb, ln = r_ref.shape[0], r_ref.shape[1]
    n = RWKV_HEAD
    pw = 2 * n
    npair = r_ref.shape[2] // pw
    ent = [(bi, slice(p * pw, (p + 1) * pw)) for bi in range(nb) for p in range(npair)]

    @pl.when(pl.program_id(1) == 0)
    def _():
        state_ref[...] = jnp.zeros_like(state_ref)

    rows = lax.broadcasted_iota(jnp.int32, (ln, ln), 0)
    cols = lax.broadcasted_iota(jnp.int32, (ln, ln), 1)
    tril = jnp.where(rows >= cols, 1.0, 0.0).astype(BF16)
    r, k, v, w_end, r_t, a_t, b_t, k_t, b_h, k_h = ([] for _ in range(10))
    for bi in range(nb):
        lw = lw_ref[bi]
        cw = _mm_exact_rhs_left(tril, lw)
        cw_end = cw[ln - 1:ln, :]
        e_neg = jnp.exp(-cw)
        e_end = jnp.exp(cw_end - cw)
        bb = b_ref[bi].astype(F32)
        r.append(r_ref[bi].astype(F32))
        k.append(k_ref[bi].astype(F32))
        v.append(v_ref[bi].astype(F32))
        w_end.append(jnp.exp(cw_end))
        r_t.append(r[bi] * jnp.exp(cw))
        a_t.append(an_ref[bi].astype(F32) * jnp.exp(cw - lw))
        b_t.append(bb * e_neg)
        k_t.append(k[bi] * e_neg)
        b_h.append(bb * e_end)
        k_h.append(k[bi] * e_end)

    lane = lax.broadcasted_iota(jnp.int32, (ln, pw), 1)
    row_w = lax.broadcasted_iota(jnp.int32, (ln, pw), 0)
    even = lane < n
    even2 = lax.broadcasted_iota(jnp.int32, (2 * ln, pw), 1) < n
    col_w = lane & (n - 1)
    strict_w = row_w > col_w
    incl_w = row_w >= col_w
    zeros_w = jnp.zeros((ln, pw), F32)
    diag = lambda x: _pair_diag(x, even)
    swap = lambda x: jnp.concatenate([x[x.shape[0] // 2:], x[:x.shape[0] // 2]], axis=0)

    ar_p = [jnp.concatenate([a_t[bi][:, ps], r_t[bi][:, ps]], axis=0) for bi, ps in ent]
    bk_p = [jnp.concatenate([b_t[bi][:, ps], k_t[bi][:, ps]], axis=0).astype(BF16) for bi, ps in ent]
    kb_p = [jnp.concatenate([k_t[bi][:, ps], b_t[bi][:, ps]], axis=0).astype(BF16) for bi, ps in ent]
    am_e = _dots([jnp.where(even2, x, 0.0) for x in ar_p], bk_p, _NT)
    am_o = _dots([jnp.where(even2, 0.0, x) for x in ar_p], kb_p, _NT)
    a_ab = [jnp.where(strict_w, jnp.where(even, e[:ln], o[:ln]), 0.0) for e, o in zip(am_e, am_o)]
    a_ak = [jnp.where(strict_w, jnp.where(even, o[:ln], e[:ln]), 0.0) for e, o in zip(am_e, am_o)]
    a_rb = [jnp.where(incl_w, jnp.where(even, e[ln:], o[ln:]), 0.0) for e, o in zip(am_e, am_o)]
    a_rk = [jnp.where(incl_w, jnp.where(even, o[ln:], e[ln:]), 0.0) for e, o in zip(am_e, am_o)]
    v_p = [v[bi][:, ps] for bi, ps in ent]
    v_d = [diag(x) for x in v_p]
    akv = _dots(a_ak, [swap(x) for x in v_d])
    t = _unit_lower_inverse(a_ab, row_w, col_w, even)
    rhs = [jnp.concatenate([diag(a_t[bi][:, ps]), diag(x)], axis=1) for (bi, ps), x in zip(ent, akv)]
    pq = _dots(t, rhs)
    ry = _dots([jnp.concatenate([x, y], axis=1) for x, y in zip(a_rb, a_rk)],
               [jnp.concatenate([jnp.concatenate([diag(x[:, :pw]), diag(x[:, pw:])], axis=1),
                                 jnp.concatenate([jnp.zeros((2 * ln, pw), F32), swap(u)], axis=1)], axis=0)
                for x, u in zip(pq, v_d)])
    pqv_p = [jnp.concatenate([x, jnp.concatenate([zeros_w, u], axis=1)], axis=0)
             for x, u in zip(pq, v_p)]
    bkh_p = [jnp.concatenate([b_h[bi][:, ps], k_h[bi][:, ps]], axis=0) for bi, ps in ent]
    mn_p = _dots(bkh_p, pqv_p, _TN)
    sq_r = lax.broadcasted_iota(jnp.int32, (pw, pw), 0)
    sq_c = lax.broadcasted_iota(jnp.int32, (pw, pw), 1)
    same_head = (sq_r < n) == (sq_c < n)
    lhs_p = [jnp.concatenate([r_t[bi][:, ps] + x[:, :pw],
                              jnp.where(sq_r == sq_c, jnp.broadcast_to(w_end[bi][:, ps], (pw, pw)),
                                        jnp.where(same_head, m[:, :pw], 0.0))], axis=0)
             for (bi, ps), x, m in zip(ent, ry, mn_p)]
    st = [state_ref[e] for e in range(len(ent))]
    upd = _dots(lhs_p, st)
    for e in range(len(ent)):
        state_ref[e] = upd[e][ln:, :] + jnp.where(same_head, mn_p[e][:, pw:], 0.0)
    y0_p = [x[:, pw:] for x in ry]

    hsum = lambda x: jnp.where(even, jnp.sum(jnp.where(even, x, 0.0), axis=-1, keepdims=True),
                               jnp.sum(jnp.where(even, 0.0, x), axis=-1, keepdims=True))
    inv_n = 1.0 / n
    yn = []
    for u, y0 in zip(upd, y0_p):
        y = u[:ln, :] + y0
        d = y - hsum(y) * inv_n
        yn.append(d * lax.rsqrt(hsum(d * d) * inv_n + LNX_EPS))
    for bi in range(nb):
        rk = r[bi] * k[bi] * rk_ref[...]
        mine = range(bi * npair, (bi + 1) * npair)
        bonus = jnp.concatenate([hsum(rk[:, ent[e][1]]) * v_p[e] for e in mine], axis=1)
        out = jnp.concatenate([yn[e] for e in mine], axis=1) * lg_ref[...] + lb_ref[...] + bonus
        y_ref[bi] = (out * g_ref[bi].astype(F32)).astype(y_ref.dtype)


def _wkv(r, lw, k, v, an, b, g, r_k, lnx_g, lnx_b, heads_per_step=16):
    bsz, s, c = r.shape
    wb = heads_per_step * RWKV_HEAD
    nc = s // WKV_CHUNK
    assert s % WKV_CHUNK == 0 and c % wb == 0 and heads_per_step % 2 == 0 and WKV_CHUNK == RWKV_HEAD
    spec = pl.BlockSpec((bsz, WKV_CHUNK, wb), lambda hi, ci: (0, ci, hi))
    pspec = pl.BlockSpec((1, wb), lambda hi, ci: (0, hi))
    row = lambda a: a.reshape(1, -1)
    return pl.pallas_call(
        _wkv_kernel,
        grid=(c // wb, nc),
        in_specs=[spec] * 7 + [pspec] * 3,
        out_specs=spec,
        out_shape=jax.ShapeDtypeStruct((bsz, s, c), BF16),
        scratch_shapes=[pltpu.VMEM((bsz * heads_per_step // 2, 2 * RWKV_HEAD, 2 * RWKV_HEAD), F32)],
        compiler_params=_cparams("parallel", "arbitrary"),
        name="wkv",
    )(r, lw, k, v, an, b, g, row(r_k), row(lnx_g), row(lnx_b))


def _rope_kernel(q_ref, k_ref, v_ref, cos_ref, sin_ref, qt_ref, ko_ref, vt_ref, km_ref):
    cos = cos_ref[...]
    sin = sin_ref[...]
    nh = q_ref.shape[-1] // ATTN_HEAD
    inv_rows = 1.0 / q_ref.shape[1]
    for h in range(nh):
        sl = slice(h * ATTN_HEAD, (h + 1) * ATTN_HEAD)
        q = q_ref[0, :, sl]
        k = k_ref[0, :, sl]
        qr = q * cos + pltpu.roll(q, ATTN_HEAD // 2, axis=1) * sin
        kr = k * cos + pltpu.roll(k, ATTN_HEAD // 2, axis=1) * sin
        qt_ref[0, h, 0] = qr.T
        ko_ref[0, h, 0] = kr.astype(BF16)
        vt_ref[0, h, 0] = v_ref[0, :, sl].T.astype(BF16)
        km_ref[0, 0, :, sl] = jnp.sum(kr, axis=0, keepdims=True) * inv_rows


def _rope(z3, col0, width, cos2, sin2):
    bsz, s, _ = z3.shape
    assert s % MOBA_BLOCK == 0 and col0 % ATTN_HEAD == 0
    nb = s // MOBA_BLOCK
    nh = width // ATTN_HEAD
    blk = lambda j: pl.BlockSpec((pl.Element(1), pl.Element(MOBA_BLOCK), pl.Element(width)),
                                 lambda b, i: (b, i * MOBA_BLOCK, col0 + j * width))
    tab = pl.BlockSpec((MOBA_BLOCK, ATTN_HEAD), lambda b, i: (i, 0))
    t_spec = pl.BlockSpec((1, nh, 1, ATTN_HEAD, MOBA_BLOCK), lambda b, i: (b, 0, i, 0, 0))
    n_spec = pl.BlockSpec((1, nh, 1, MOBA_BLOCK, ATTN_HEAD), lambda b, i: (b, 0, i, 0, 0))
    return pl.pallas_call(
        _rope_kernel,
        grid=(bsz, nb),
        in_specs=[blk(0), blk(1), blk(2), tab, tab],
        out_specs=[t_spec, n_spec, t_spec, pl.BlockSpec((1, 1, 1, width), lambda b, i: (b, i, 0, 0))],
        out_shape=[jax.ShapeDtypeStruct((bsz, nh, nb, ATTN_HEAD, MOBA_BLOCK), F32),
                   jax.ShapeDtypeStruct((bsz, nh, nb, MOBA_BLOCK, ATTN_HEAD), BF16),
                   jax.ShapeDtypeStruct((bsz, nh, nb, ATTN_HEAD, MOBA_BLOCK), BF16),
                   jax.ShapeDtypeStruct((bsz, nb, 1, width), F32)],
        compiler_params=_cparams("parallel", "parallel"),
        name="rope",
    )(z3, z3, z3, cos2, sin2)


def _moba_kernel(qt_ref, k_ref, vt_ref, km_ref, o_ref, bias_ref):
    blk = MOBA_BLOCK
    dh = ATTN_HEAD
    hs = range(qt_ref.shape[1])
    qb = pl.program_id(2)
    nb = km_ref.shape[1]
    scale = ATTN_HEAD ** -0.5
    neg = -jnp.inf
    qt = [qt_ref[0, h, 0] for h in hs]
    blk_id = lax.broadcasted_iota(jnp.int32, (nb, blk), 0)
    past = blk_id < qb
    gate = [jnp.where(past, _mm(km_ref[0, :, h * dh:(h + 1) * dh], qt[h], passes=3), neg) for h in hs]
    for h in hs:
        rank = jnp.zeros((nb, blk), jnp.int32)
        for m in range(nb):
            gm = gate[h][m:m + 1, :]
            rank += ((gm > gate[h]) | ((gm == gate[h]) & (m < blk_id))).astype(jnp.int32)
        bias_ref[h] = jnp.where(past & (rank < MOBA_TOPK), 0.0, neg)

    qs = [(q * (scale * LOG2E)).astype(BF16) for q in qt]
    ki = lax.broadcasted_iota(jnp.int32, (blk, blk), 0)
    qi = lax.broadcasted_iota(jnp.int32, (blk, blk), 1)
    causal = ki <= qi

    def pipelined(work, stage):
        scores = lambda kb, h: jnp.dot(k_ref[0, h, kb], qs[h], preferred_element_type=F32)
        ahead = [scores(*w) for w in work[:MOBA_LOOKAHEAD]]
        for i, (kb, h) in enumerate(work):
            if i + MOBA_LOOKAHEAD < len(work):
                ahead.append(scores(*work[i + MOBA_LOOKAHEAD]))
            stage(kb, h, ahead[i])

    m_run, l_run, acc = [None] * len(hs), [None] * len(hs), [None] * len(hs)

    def own_block(kb, h, s):
        s = jnp.where(causal, s, neg)
        m_run[h] = jnp.max(s, axis=0, keepdims=True)
        p = jnp.exp2(s - m_run[h])
        l_run[h] = jnp.sum(p, axis=0, keepdims=True)
        acc[h] = jnp.dot(vt_ref[0, h, kb], p.astype(BF16), preferred_element_type=F32)

    pipelined([(qb, h) for h in hs], own_block)

    def past_blocks(kbs, carry):
        m_c, l_c, acc_c = (list(c) for c in carry)

        def stage(kb, h, s):
            b = bias_ref[h, pl.ds(kb, 1), :]
            m_new = jnp.where(b == 0.0, jnp.maximum(m_c[h], jnp.max(s, axis=0, keepdims=True)), m_c[h])
            alpha = jnp.exp2(m_c[h] - m_new)
            p = jnp.exp2(s - (m_new - b))
            pv = jnp.dot(vt_ref[0, h, kb], p.astype(BF16), preferred_element_type=F32)
            m_c[h] = m_new
            l_c[h] = alpha * l_c[h] + jnp.sum(p, axis=0, keepdims=True)
            acc_c[h] = alpha * acc_c[h] + pv

        pipelined([(kb, h) for kb in kbs for h in hs], stage)
        return tuple(m_c), tuple(l_c), tuple(acc_c)

    u = MOBA_UNROLL
    carry = lax.fori_loop(0, qb // u, lambda j, c: past_blocks([j * u + i for i in range(u)], c),
                          (tuple(m_run), tuple(l_run), tuple(acc)))
    _, l_fin, acc = lax.fori_loop((qb // u) * u, qb, lambda kb, c: past_blocks([kb], c), carry)
    for h in hs:
        o_ref[0, :, h * dh:(h + 1) * dh] = (acc[h] / l_fin[h]).T.astype(o_ref.dtype)


def _moba(qt, k, vt, kmean, heads_per_step=8):
    bsz, nh, nb, dh, blk = qt.shape
    hp = heads_per_step
    assert nh % hp == 0
    return pl.pallas_call(
        _moba_kernel,
        grid=(bsz, nh // hp, nb),
        in_specs=[pl.BlockSpec((1, hp, 1, dh, blk), lambda b, h, i: (b, h, i, 0, 0)),
                  pl.BlockSpec((1, hp, nb, blk, dh), lambda b, h, i: (b, h, 0, 0, 0)),
                  pl.BlockSpec((1, hp, nb, dh, blk), lambda b, h, i: (b, h, 0, 0, 0)),
                  pl.BlockSpec((1, nb, hp * dh), lambda b, h, i: (b, 0, h))],
        out_specs=pl.BlockSpec((1, blk, hp * dh), lambda b, h, i: (b, i, h)),
        out_shape=jax.ShapeDtypeStruct((bsz, nb * blk, nh * dh), BF16),
        scratch_shapes=[pltpu.VMEM((hp, nb, blk), F32)],
        compiler_params=_cparams("parallel", "parallel", "arbitrary"),
        name="moba",
    )(qt, k, vt, kmean)


def _out_proj_kernel(yr_ref, ya_ref, wr_ref, wa_ref, x_ref, g_ref, o_ref):
    y = jnp.dot(yr_ref[...], wr_ref[...], preferred_element_type=F32)
    y += jnp.dot(ya_ref[...], wa_ref[...], preferred_element_type=F32)
    ms = jnp.mean(y * y, axis=-1, keepdims=True)
    o_ref[...] = x_ref[...] + y * lax.rsqrt(ms + NORM_EPS) * g_ref[...]


def _out_proj(y_r, y_a, w_all, layer, x2, gain, tm=256):
    m, d = x2.shape
    cw = y_r.shape[1]
    assert y_a.shape[1] == cw and w_all.shape[1] == 2 * cw and m % tm == 0
    return pl.pallas_call(
        _out_proj_kernel,
        grid=(m // tm,),
        in_specs=[pl.BlockSpec((tm, cw), lambda i: (i, 0)), pl.BlockSpec((tm, cw), lambda i: (i, 0)),
                  pl.BlockSpec((None, cw, d), lambda i: (layer, 0, 0)),
                  pl.BlockSpec((None, cw, d), lambda i: (layer, 1, 0)),
                  pl.BlockSpec((tm, d), lambda i: (i, 0)), pl.BlockSpec((1, d), lambda i: (0, 0))],
        out_specs=pl.BlockSpec((tm, d), lambda i: (i, 0)),
        out_shape=jax.ShapeDtypeStruct((m, d), F32),
        compiler_params=_cparams("parallel"),
        name="out_proj",
    )(y_r, y_a, w_all, w_all, x2, gain.reshape(1, d))


def _mlp_kernel(x_ref, gpre_ref, wu_ref, wd_ref, gpost_ref, o_ref, h_ref, acc_ref):
    f = pl.program_id(1)

    @pl.when(f == 0)
    def _():
        x = x_ref[...]
        ms = jnp.mean(x * x, axis=-1, keepdims=True)
        h_ref[...] = (x * lax.rsqrt(ms + NORM_EPS) * gpre_ref[...]).astype(BF16)
        acc_ref[...] = jnp.zeros_like(acc_ref)

    u = jnp.maximum(jnp.dot(h_ref[...], wu_ref[...], preferred_element_type=F32), 0.0)
    acc_ref[...] += jnp.dot((u * u).astype(BF16), wd_ref[...], preferred_element_type=F32)

    @pl.when(f == pl.num_programs(1) - 1)
    def _():
        mlp = acc_ref[...]
        ms = jnp.mean(mlp * mlp, axis=-1, keepdims=True)
        o_ref[...] = x_ref[...] + mlp * lax.rsqrt(ms + NORM_EPS) * gpost_ref[...]


def _mlp(x2, g_pre, w_up, w_down, layer, g_post, tm=512, tf=1024):
    m, d = x2.shape
    dff = w_up.shape[2]
    assert m % tm == 0 and dff % tf == 0
    return pl.pallas_call(
        _mlp_kernel,
        grid=(m // tm, dff // tf),
        in_specs=[pl.BlockSpec((tm, d), lambda i, f: (i, 0)),
                  pl.BlockSpec((1, d), lambda i, f: (0, 0)),
                  pl.BlockSpec((None, d, tf), lambda i, f: (layer, 0, f)),
                  pl.BlockSpec((None, tf, d), lambda i, f: (layer, f, 0)),
                  pl.BlockSpec((1, d), lambda i, f: (0, 0))],
        out_specs=pl.BlockSpec((tm, d), lambda i, f: (i, 0)),
        out_shape=jax.ShapeDtypeStruct((m, d), F32),
        scratch_shapes=[pltpu.VMEM((tm, d), BF16), pltpu.VMEM((tm, d), F32)],
        compiler_params=_cparams("parallel", "arbitrary"),
        name="mlp",
    )(x2, g_pre.reshape(1, d), w_up, w_down, g_post.reshape(1, d))


def _head_indicators(c):
    head = jnp.arange(c) // RWKV_HEAD
    hsum = (head[:, None] == jnp.arange(128)[None, :]).astype(BF16)
    return hsum, hsum.T


def _rope_tables(s):
    half = ATTN_HEAD // 2
    inv_freq = ROPE_THETA ** (-jnp.arange(half, dtype=F32) / half)
    ang = jnp.arange(s).astype(F32)[:, None] * inv_freq[None, :]
    cos, sin = jnp.cos(ang), jnp.sin(ang)
    return jnp.concatenate([cos, cos], axis=-1), jnp.concatenate([-sin, sin], axis=-1)


def kernel(x, norm_mix_pre, norm_mix_post, norm_mlp_pre, norm_mlp_post, w_in, w_in_vres, shift_mu, shift_mu_vres, decay_w0, decay_w2, iclr_a0, iclr_a2, vres_v0, vres_v2, gate_g2, k_k, k_a, r_k, lnx_gain, lnx_bias, w_out, w_up, w_down):
    bsz, s, d = x.shape
    depth = w_in.shape[0]
    c = decay_w0.shape[1]
    n_lora = DECAY_LORA + ICLR_LORA + GATE_LORA
    n_shift = 3 * c + n_lora
    ca = (w_in.shape[2] - n_shift) // 3
    hsum, hbc = _head_indicators(c)
    cos2, sin2 = _rope_tables(s)
    w_in16, w_out16, w_up16, w_down16 = (w.astype(BF16) for w in (w_in, w_out, w_up, w_down))
    pad_v = VRES_PAD - VRES_LORA
    x2 = x.reshape(bsz * s, d)
    v_first = None
    for i in range(depth):
        if i == 0:
            z, vres = _norm_matmul(x2, norm_mix_pre[i], w_in16, i)[0], None
        else:
            w_v = jnp.pad(w_in_vres[i - 1], ((0, 0), (0, pad_v))).astype(BF16)
            z, z_v = _norm_matmul(x2, norm_mix_pre[i], w_in16, i, w_v)
            vres = (z_v.reshape(bsz, s, VRES_PAD), jnp.pad(shift_mu_vres[i - 1], (0, pad_v)), v_first,
                    vres_v0[i - 1], jnp.pad(vres_v2[i - 1], ((0, pad_v), (0, 0))))
        z3 = z.reshape(bsz, s, -1)

        prep = _rwkv_prep(z3, c, shift_mu[i], decay_w0[i], decay_w2[i], iclr_a0[i], iclr_a2[i], gate_g2[i],
                          k_k[i], k_a[i], hsum, hbc, vres)
        if i == 0:
            v_first = prep[7]
        y_r = _wkv(*prep[:7], r_k[i].reshape(-1), lnx_gain[i], lnx_bias[i])

        q_t, k_rot, v_t, kmean = _rope(z3, n_shift, ca, cos2, sin2)
        y_a = _moba(q_t, k_rot, v_t, kmean.reshape(bsz, -1, ca))

        x2 = _out_proj(y_r.reshape(bsz * s, c), y_a.reshape(bsz * s, ca), w_out16, i, x2, norm_mix_post[i])
        x2 = _mlp(x2, norm_mlp_pre[i], w_up16, w_down16, i, norm_mlp_post[i])
    return x2.reshape(bsz, s, d)
```

```python
import functools

import jax
import jax.numpy as jnp
from jax import lax
from jax.experimental import pallas as pl
from jax.experimental.pallas import tpu as pltpu

F32 = jnp.float32
BF16 = jnp.bfloat16

RWKV_HEAD = 64
DECAY_LORA = 64
ICLR_LORA = 64
VRES_LORA = 32
GATE_LORA = 128
ATTN_HEAD = 128
MOBA_BLOCK = 256
MOBA_TOPK = 3
ROPE_THETA = 10000.0
NORM_EPS = 1e-6
LNX_EPS = 64e-5
LOG2E = 1.4426950408889634

VRES_PAD = 128
WKV_CHUNK = 64
MOBA_LOOKAHEAD = 6
MOBA_UNROLL = 4
ROW_CHUNK = 256
VMEM_LIMIT = 56 * 1024 * 1024


def _cparams(*sem):
    return pltpu.CompilerParams(dimension_semantics=sem, vmem_limit_bytes=VMEM_LIMIT)


_NN = (((1,), (0,)), ((), ()))
_NT = (((1,), (1,)), ((), ()))
_TN = (((0,), (0,)), ((), ()))


def _split2(x):
    hi = x.astype(BF16)
    lo = (x - hi.astype(F32)).astype(BF16)
    return hi, lo


def _split3(x):
    hi = x.astype(BF16)
    r1 = x - hi.astype(F32)
    mid = r1.astype(BF16)
    lo = (r1 - mid.astype(F32)).astype(BF16)
    return hi, mid, lo


def _mm(a, b, dims=_NN, passes=1):
    d = lambda p, q: lax.dot_general(p, q, dims, preferred_element_type=F32)
    if passes == 1:
        return d(a.astype(BF16), b.astype(BF16))
    ah, al = _split2(a)
    bh, bl = _split2(b)
    return d(ah, bh) + (d(ah, bl) + d(al, bh))


def _mm_exact_rhs(a, b_bf16, dims=_NN):
    d = lambda p: lax.dot_general(p, b_bf16, dims, preferred_element_type=F32)
    hi, mid, lo = _split3(a)
    return d(hi) + (d(mid) + d(lo))


def _sigmoid(x):
    return 1.0 / (1.0 + jnp.exp(-x))


def _softplus(x):
    return jnp.maximum(x, 0.0) + jnp.log(1.0 + jnp.exp(-jnp.abs(x)))


def _norm_matmul_kernel(has_extra, *refs):
    if has_extra:
        x_ref, g_ref, w_ref, we_ref, o_ref, oe_ref, h_ref = refs
    else:
        x_ref, g_ref, w_ref, o_ref, h_ref = refs

    first = pl.program_id(1) == 0

    @pl.when(first)
    def _():
        for r in range(0, x_ref.shape[0], ROW_CHUNK):
            rows = pl.ds(r, ROW_CHUNK)
            x = x_ref[rows, :]
            ms = jnp.mean(x * x, axis=-1, keepdims=True)
            h = (x * lax.rsqrt(ms + NORM_EPS) * g_ref[...]).astype(BF16)
            h_ref[rows, :] = h
            o_ref[rows, :] = jnp.dot(h, w_ref[...], preferred_element_type=F32)
            if has_extra:
                oe_ref[rows, :] = jnp.dot(h, we_ref[...], preferred_element_type=F32)

    @pl.when(jnp.logical_not(first))
    def _():
        o_ref[...] = jnp.dot(h_ref[...], w_ref[...], preferred_element_type=F32)


def _norm_matmul(x2, gain, w_all, layer, w_extra=None, tm=1024, tn=1280):
    m, d = x2.shape
    n = w_all.shape[2]
    assert m % tm == 0 and n % tn == 0
    has_extra = w_extra is not None
    ins = [x2, gain.reshape(1, d), w_all]
    in_specs = [pl.BlockSpec((tm, d), lambda i, j: (i, 0)),
                pl.BlockSpec((1, d), lambda i, j: (0, 0)),
                pl.BlockSpec((None, d, tn), lambda i, j: (layer, 0, j))]
    out_specs = [pl.BlockSpec((tm, tn), lambda i, j: (i, j))]
    out_shape = [jax.ShapeDtypeStruct((m, n), F32)]
    if has_extra:
        ne = w_extra.shape[1]
        ins.append(w_extra)
        in_specs.append(pl.BlockSpec((d, ne), lambda i, j: (0, 0)))
        out_specs.append(pl.BlockSpec((tm, ne), lambda i, j: (i, 0)))
        out_shape.append(jax.ShapeDtypeStruct((m, ne), F32))
    return pl.pallas_call(
        functools.partial(_norm_matmul_kernel, has_extra),
        grid=(m // tm, n // tn),
        in_specs=in_specs,
        out_specs=out_specs,
        out_shape=out_shape,
        scratch_shapes=[pltpu.VMEM((tm, d), BF16)],
        compiler_params=_cparams("parallel", "arbitrary"),
        name="norm_matmul",
    )(*ins)


def _shifted(cur, halo, mu, first):
    rows = lax.broadcasted_iota(jnp.int32, cur.shape, 0)
    last = jnp.where(first, 0.0, halo[7:8, :])
    prev = jnp.where(rows == 0, last, pltpu.roll(cur, 1, axis=0))
    return cur + (prev - cur) * mu


def _rwkv_prep_kernel(has_vres, *refs):
    if has_vres:
        (zm_ref, zmh_ref, zl_ref, zlh_ref, mum_ref, mul_ref, w0_ref, w2_ref, a0_ref, a2_ref, g2_ref,
         kk_ref, ka_ref, hsum_ref, hbc_ref, zv_ref, zvh_ref, muv_ref, vf_ref, v0_ref, v2_ref,
         r_ref, lw_ref, k_ref, v_ref, an_ref, b_ref, g_ref) = refs
    else:
        (zm_ref, zmh_ref, zl_ref, zlh_ref, mum_ref, mul_ref, w0_ref, w2_ref, a0_ref, a2_ref, g2_ref,
         kk_ref, ka_ref, hsum_ref, hbc_ref,
         r_ref, lw_ref, k_ref, v_ref, an_ref, b_ref, g_ref, vf32_ref) = refs
    c = r_ref.shape[-1]
    first = pl.program_id(1) == 0
    zs = _shifted(zm_ref[0], zmh_ref[0], mum_ref[...], first)
    zl = _shifted(zl_ref[0], zlh_ref[0], mul_ref[...], first)
    r, k, v = zs[:, :c], zs[:, c:2 * c], zs[:, 2 * c:3 * c]
    o = 0
    wd = zl[:, o:o + DECAY_LORA]
    o += DECAY_LORA
    ad = zl[:, o:o + ICLR_LORA]
    o += ICLR_LORA
    gd = zl[:, o:o + GATE_LORA]
    o += GATE_LORA
    w_log = -_softplus(-(w0_ref[...] + _mm(jnp.tanh(wd), w2_ref[...], passes=3))) - 0.5
    lw_ref[0] = -jnp.exp(w_log)
    a = _sigmoid(a0_ref[...] + _mm(ad, a2_ref[...], passes=3))
    g_ref[0] = _mm(_sigmoid(gd), g2_ref[...], passes=3).astype(g_ref.dtype)
    if has_vres:
        vd = _shifted(zv_ref[0], zvh_ref[0], muv_ref[...], first)
        v = v + (vf_ref[0] - v) * _sigmoid(v0_ref[...] + _mm(vd, v2_ref[...], passes=3))
    else:
        vf32_ref[0] = v
    kk = k * kk_ref[...]
    ss = _mm_exact_rhs(kk * kk, hsum_ref[...])
    inv = 1.0 / jnp.maximum(jnp.sqrt(ss), 1e-12)
    kk = kk * _mm_exact_rhs(inv, hbc_ref[...])
    r_ref[0] = r.astype(r_ref.dtype)
    k_ref[0] = (k * (1.0 + (a - 1.0) * ka_ref[...])).astype(k_ref.dtype)
    v_ref[0] = v.astype(v_ref.dtype)
    an_ref[0] = (-kk).astype(an_ref.dtype)
    b_ref[0] = (kk * a).astype(b_ref.dtype)


def _rwkv_prep(z3, c, mu, w0, w2, a0, a2, g2, k_k, k_a, hsum, hbc, vres, tm=256):
    bsz, s, _ = z3.shape
    n_lora = w2.shape[0] + a2.shape[0] + g2.shape[0]
    assert s % tm == 0 and (3 * c) % n_lora == 0 and n_lora % 128 == 0
    has_vres = vres is not None
    row = lambda a: a.reshape(1, -1)
    hb = tm // 8
    lcb = 3 * c // n_lora
    halo = lambda b, i: (b, jnp.maximum(i * hb - 1, 0), 0)
    halo_l = lambda b, i: (b, jnp.maximum(i * hb - 1, 0), lcb)
    full = lambda a: pl.BlockSpec(a.shape, lambda b, i: (0,) * a.ndim)
    ins = [z3, z3, z3, z3, row(mu[:3 * c]), row(mu[3 * c:]), row(w0), w2, row(a0), a2, g2, row(k_k), row(k_a),
           hsum, hbc]
    in_specs = [pl.BlockSpec((1, tm, 3 * c), lambda b, i: (b, i, 0)),
                pl.BlockSpec((1, 8, 3 * c), halo),
                pl.BlockSpec((1, tm, n_lora), lambda b, i: (b, i, lcb)),
                pl.BlockSpec((1, 8, n_lora), halo_l)] + [full(a) for a in ins[4:]]
    if has_vres:
        zv3, mu_v, v_first, v0, v2 = vres
        nv = zv3.shape[-1]
        extra = [zv3, zv3, row(mu_v), v_first, row(v0), v2]
        ins += extra
        in_specs += [pl.BlockSpec((1, tm, nv), lambda b, i: (b, i, 0)), pl.BlockSpec((1, 8, nv), halo),
                     full(extra[2]), pl.BlockSpec((1, tm, c), lambda b, i: (b, i, 0)), full(extra[4]),
                     full(extra[5])]
    out_spec = pl.BlockSpec((1, tm, c), lambda b, i: (b, i, 0))
    sd = lambda dt: jax.ShapeDtypeStruct((bsz, s, c), dt)
    out_dtypes = [BF16, F32, BF16, BF16, BF16, BF16, BF16] + ([] if has_vres else [F32])
    return pl.pallas_call(
        functools.partial(_rwkv_prep_kernel, has_vres),
        grid=(bsz, s // tm),
        in_specs=in_specs,
        out_specs=[out_spec] * len(out_dtypes),
        out_shape=[sd(dt) for dt in out_dtypes],
        compiler_params=_cparams("parallel", "arbitrary"),
        name="rwkv_prep",
    )(*ins)


def _dots(a_list, b_list, dims=_NN):
    return [lax.dot_general(a.astype(BF16), b.astype(BF16), dims, preferred_element_type=F32)
            for a, b in zip(a_list, b_list)]


def _pair_diag(x, even):
    return jnp.concatenate([jnp.where(even, x, 0.0), jnp.where(even, 0.0, x)], axis=0)


def _unit_lower_inverse(a_list, row_w, col_w, even):
    n = row_w.shape[0]
    lower = row_w > col_w
    base = lower & ((row_w >> 1) == (col_w >> 1))
    t = [jnp.where(row_w == col_w, 1.0, jnp.where(base, a, 0.0)) for a in a_list]
    sh = 1
    while (2 << sh) <= n:
        sub = lower & ((row_w >> (sh + 1)) == (col_w >> (sh + 1))) & ((row_w >> sh) != (col_w >> sh))
        off = [_pair_diag(jnp.where(sub, a, 0.0), even) for a in a_list]
        upd = _dots(_dots(t, off), [_pair_diag(x, even) for x in t])
        t = [x + u for x, u in zip(t, upd)]
        sh += 1
    return t


def _mm_exact_rhs_left(l_bf16, a):
    d = lambda p: lax.dot_general(l_bf16, p, _NN, preferred_element_type=F32)
    hi, mid, lo = _split3(a)
    return d(hi) + (d(mid) + d(lo))


def _wkv_kernel(r_ref, lw_ref, k_ref, v_ref, an_ref, b_ref, g_ref, rk_ref, lg_ref, lb_ref, y_ref, state_ref):
    nb, ln = r_ref.shape[0], r_ref.shape[1]
    n = RWKV_HEAD
    pw = 2 * n
    npair = r_ref.shape[2] // pw
    ent = [(bi, slice(p * pw, (p + 1) * pw)) for bi in range(nb) for p in range(npair)]

    @pl.when(pl.program_id(1) == 0)
    def _():
        state_ref[...] = jnp.zeros_like(state_ref)

    rows = lax.broadcasted_iota(jnp.int32, (ln, ln), 0)
    cols = lax.broadcasted_iota(jnp.int32, (ln, ln), 1)
    tril = jnp.where(rows >= cols, 1.0, 0.0).astype(BF16)
    r, k, v, w_end, r_t, a_t, b_t, k_t, b_h, k_h = ([] for _ in range(10))
    for bi in range(nb):
        lw = lw_ref[bi]
        cw = _mm_exact_rhs_left(tril, lw)
        cw_end = cw[ln - 1:ln, :]
        e_neg = jnp.exp(-cw)
        e_end = jnp.exp(cw_end - cw)
        bb = b_ref[bi].astype(F32)
        r.append(r_ref[bi].astype(F32))
        k.append(k_ref[bi].astype(F32))
        v.append(v_ref[bi].astype(F32))
        w_end.append(jnp.exp(cw_end))
        r_t.append(r[bi] * jnp.exp(cw))
        a_t.append(an_ref[bi].astype(F32) * jnp.exp(cw - lw))
        b_t.append(bb * e_neg)
        k_t.append(k[bi] * e_neg)
        b_h.append(bb * e_end)
        k_h.append(k[bi] * e_end)

    lane = lax.broadcasted_iota(jnp.int32, (ln, pw), 1)
    row_w = lax.broadcasted_iota(jnp.int32, (ln, pw), 0)
    even = lane < n
    even2 = lax.broadcasted_iota(jnp.int32, (2 * ln, pw), 1) < n
    col_w = lane & (n - 1)
    strict_w = row_w > col_w
    incl_w = row_w >= col_w
    zeros_w = jnp.zeros((ln, pw), F32)
    diag = lambda x: _pair_diag(x, even)
    swap = lambda x: jnp.concatenate([x[x.shape[0] // 2:], x[:x.shape[0] // 2]], axis=0)

    ar_p = [jnp.concatenate([a_t[bi][:, ps], r_t[bi][:, ps]], axis=0) for bi, ps in ent]
    bk_p = [jnp.concatenate([b_t[bi][:, ps], k_t[bi][:, ps]], axis=0).astype(BF16) for bi, ps in ent]
    kb_p = [jnp.concatenate([k_t[bi][:, ps], b_t[bi][:, ps]], axis=0).astype(BF16) for bi, ps in ent]
    am_e = _dots([jnp.where(even2, x, 0.0) for x in ar_p], bk_p, _NT)
    am_o = _dots([jnp.where(even2, 0.0, x) for x in ar_p], kb_p, _NT)
    a_ab = [jnp.where(strict_w, jnp.where(even, e[:ln], o[:ln]), 0.0) for e, o in zip(am_e, am_o)]
    a_ak = [jnp.where(strict_w, jnp.where(even, o[:ln], e[:ln]), 0.0) for e, o in zip(am_e, am_o)]
    a_rb = [jnp.where(incl_w, jnp.where(even, e[ln:], o[ln:]), 0.0) for e, o in zip(am_e, am_o)]
    a_rk = [jnp.where(incl_w, jnp.where(even, o[ln:], e[ln:]), 0.0) for e, o in zip(am_e, am_o)]
    v_p = [v[bi][:, ps] for bi, ps in ent]
    v_d = [diag(x) for x in v_p]
    akv = _dots(a_ak, [swap(x) for x in v_d])
    t = _unit_lower_inverse(a_ab, row_w, col_w, even)
    rhs = [jnp.concatenate([diag(a_t[bi][:, ps]), diag(x)], axis=1) for (bi, ps), x in zip(ent, akv)]
    pq = _dots(t, rhs)
    ry = _dots([jnp.concatenate([x, y], axis=1) for x, y in zip(a_rb, a_rk)],
               [jnp.concatenate([jnp.concatenate([diag(x[:, :pw]), diag(x[:, pw:])], axis=1),
                                 jnp.concatenate([jnp.zeros((2 * ln, pw), F32), swap(u)], axis=1)], axis=0)
                for x, u in zip(pq, v_d)])
    pqv_p = [jnp.concatenate([x, jnp.concatenate([zeros_w, u], axis=1)], axis=0)
             for x, u in zip(pq, v_p)]
    bkh_p = [jnp.concatenate([b_h[bi][:, ps], k_h[bi][:, ps]], axis=0) for bi, ps in ent]
    mn_p = _dots(bkh_p, pqv_p, _TN)
    sq_r = lax.broadcasted_iota(jnp.int32, (pw, pw), 0)
    sq_c = lax.broadcasted_iota(jnp.int32, (pw, pw), 1)
    same_head = (sq_r < n) == (sq_c < n)
    lhs_p = [jnp.concatenate([r_t[bi][:, ps] + x[:, :pw],
                              jnp.where(sq_r == sq_c, jnp.broadcast_to(w_end[bi][:, ps], (pw, pw)),
                                        jnp.where(same_head, m[:, :pw], 0.0))], axis=0)
             for (bi, ps), x, m in zip(ent, ry, mn_p)]
    st = [state_ref[e] for e in range(len(ent))]
    upd = _dots(lhs_p, st)
    for e in range(len(ent)):
        state_ref[e] = upd[e][ln:, :] + jnp.where(same_head, mn_p[e][:, pw:], 0.0)
    y0_p = [x[:, pw:] for x in ry]

    hsum = lambda x: jnp.where(even, jnp.sum(jnp.where(even, x, 0.0), axis=-1, keepdims=True),
                               jnp.sum(jnp.where(even, 0.0, x), axis=-1, keepdims=True))
    inv_n = 1.0 / n
    yn = []
    for u, y0 in zip(upd, y0_p):
        y = u[:ln, :] + y0
        d = y - hsum(y) * inv_n
        yn.append(d * lax.rsqrt(hsum(d * d) * inv_n + LNX_EPS))
    for bi in range(nb):
        rk = r[bi] * k[bi] * rk_ref[...]
        mine = range(bi * npair, (bi + 1) * npair)
        bonus = jnp.concatenate([hsum(rk[:, ent[e][1]]) * v_p[e] for e in mine], axis=1)
        out = jnp.concatenate([yn[e] for e in mine], axis=1) * lg_ref[...] + lb_ref[...] + bonus
        y_ref[bi] = (out * g_ref[bi].astype(F32)).astype(y_ref.dtype)


def _wkv(r, lw, k, v, an, b, g, r_k, lnx_g, lnx_b, heads_per_step=16):
    bsz, s, c = r.shape
    wb = heads_per_step * RWKV_HEAD
    nc = s // WKV_CHUNK
    assert s % WKV_CHUNK == 0 and c % wb == 0 and heads_per_step % 2 == 0 and WKV_CHUNK == RWKV_HEAD
    spec = pl.BlockSpec((bsz, WKV_CHUNK, wb), lambda hi, ci: (0, ci, hi))
    pspec = pl.BlockSpec((1, wb), lambda hi, ci: (0, hi))
    row = lambda a: a.reshape(1, -1)
    return pl.pallas_call(
        _wkv_kernel,
        grid=(c // wb, nc),
        in_specs=[spec] * 7 + [pspec] * 3,
        out_specs=spec,
        out_shape=jax.ShapeDtypeStruct((bsz, s, c), BF16),
        scratch_shapes=[pltpu.VMEM((bsz * heads_per_step // 2, 2 * RWKV_HEAD, 2 * RWKV_HEAD), F32)],
        compiler_params=_cparams("parallel", "arbitrary"),
        name="wkv",
    )(r, lw, k, v, an, b, g, row(r_k), row(lnx_g), row(lnx_b))


def _rope_kernel(q_ref, k_ref, v_ref, cos_ref, sin_ref, qt_ref, ko_ref, vt_ref, km_ref):
    cos = cos_ref[...]
    sin = sin_ref[...]
    nh = q_ref.shape[-1] // ATTN_HEAD
    inv_rows = 1.0 / q_ref.shape[1]
    for h in range(nh):
        sl = slice(h * ATTN_HEAD, (h + 1) * ATTN_HEAD)
        q = q_ref[0, :, sl]
        k = k_ref[0, :, sl]
        qr = q * cos + pltpu.roll(q, ATTN_HEAD // 2, axis=1) * sin
        kr = k * cos + pltpu.roll(k, ATTN_HEAD // 2, axis=1) * sin
        qt_ref[0, h, 0] = qr.T
        ko_ref[0, h, 0] = kr.astype(BF16)
        vt_ref[0, h, 0] = v_ref[0, :, sl].T.astype(BF16)
        km_ref[0, 0, :, sl] = jnp.sum(kr, axis=0, keepdims=True) * inv_rows


def _rope(z3, col0, width, cos2, sin2):
    bsz, s, _ = z3.shape
    assert s % MOBA_BLOCK == 0 and col0 % ATTN_HEAD == 0
    nb = s // MOBA_BLOCK
    nh = width // ATTN_HEAD
    blk = lambda j: pl.BlockSpec((pl.Element(1), pl.Element(MOBA_BLOCK), pl.Element(width)),
                                 lambda b, i: (b, i * MOBA_BLOCK, col0 + j * width))
    tab = pl.BlockSpec((MOBA_BLOCK, ATTN_HEAD), lambda b, i: (i, 0))
    t_spec = pl.BlockSpec((1, nh, 1, ATTN_HEAD, MOBA_BLOCK), lambda b, i: (b, 0, i, 0, 0))
    n_spec = pl.BlockSpec((1, nh, 1, MOBA_BLOCK, ATTN_HEAD), lambda b, i: (b, 0, i, 0, 0))
    return pl.pallas_call(
        _rope_kernel,
        grid=(bsz, nb),
        in_specs=[blk(0), blk(1), blk(2), tab, tab],
        out_specs=[t_spec, n_spec, t_spec, pl.BlockSpec((1, 1, 1, width), lambda b, i: (b, i, 0, 0))],
        out_shape=[jax.ShapeDtypeStruct((bsz, nh, nb, ATTN_HEAD, MOBA_BLOCK), F32),
                   jax.ShapeDtypeStruct((bsz, nh, nb, MOBA_BLOCK, ATTN_HEAD), BF16),
                   jax.ShapeDtypeStruct((bsz, nh, nb, ATTN_HEAD, MOBA_BLOCK), BF16),
                   jax.ShapeDtypeStruct((bsz, nb, 1, width), F32)],
        compiler_params=_cparams("parallel", "parallel"),
        name="rope",
    )(z3, z3, z3, cos2, sin2)


def _moba_kernel(qt_ref, k_ref, vt_ref, km_ref, o_ref, bias_ref):
    blk = MOBA_BLOCK
    dh = ATTN_HEAD
    hs = range(qt_ref.shape[1])
    qb = pl.program_id(2)
    nb = km_ref.shape[1]
    scale = ATTN_HEAD ** -0.5
    neg = -jnp.inf
    qt = [qt_ref[0, h, 0] for h in hs]
    blk_id = lax.broadcasted_iota(jnp.int32, (nb, blk), 0)
    past = blk_id < qb
    gate = [jnp.where(past, _mm(km_ref[0, :, h * dh:(h + 1) * dh], qt[h], passes=3), neg) for h in hs]
    for h in hs:
        rank = jnp.zeros((nb, blk), jnp.int32)
        for m in range(nb):
            gm = gate[h][m:m + 1, :]
            rank += ((gm > gate[h]) | ((gm == gate[h]) & (m < blk_id))).astype(jnp.int32)
        bias_ref[h] = jnp.where(past & (rank < MOBA_TOPK), 0.0, neg)

    qs = [(q * (scale * LOG2E)).astype(BF16) for q in qt]
    ki = lax.broadcasted_iota(jnp.int32, (blk, blk), 0)
    qi = lax.broadcasted_iota(jnp.int32, (blk, blk), 1)
    causal = ki <= qi

    def pipelined(work, stage):
        scores = lambda kb, h: jnp.dot(k_ref[0, h, kb], qs[h], preferred_element_type=F32)
        ahead = [scores(*w) for w in work[:MOBA_LOOKAHEAD]]
        for i, (kb, h) in enumerate(work):
            if i + MOBA_LOOKAHEAD < len(work):
                ahead.append(scores(*work[i + MOBA_LOOKAHEAD]))
            stage(kb, h, ahead[i])

    m_run, l_run, acc = [None] * len(hs), [None] * len(hs), [None] * len(hs)

    def own_block(kb, h, s):
        s = jnp.where(causal, s, neg)
        m_run[h] = jnp.max(s, axis=0, keepdims=True)
        p = jnp.exp2(s - m_run[h])
        l_run[h] = jnp.sum(p, axis=0, keepdims=True)
        acc[h] = jnp.dot(vt_ref[0, h, kb], p.astype(BF16), preferred_element_type=F32)

    pipelined([(qb, h) for h in hs], own_block)

    def past_blocks(kbs, carry):
        m_c, l_c, acc_c = (list(c) for c in carry)

        def stage(kb, h, s):
            b = bias_ref[h, pl.ds(kb, 1), :]
            m_new = jnp.where(b == 0.0, jnp.maximum(m_c[h], jnp.max(s, axis=0, keepdims=True)), m_c[h])
            alpha = jnp.exp2(m_c[h] - m_new)
            p = jnp.exp2(s - (m_new - b))
            pv = jnp.dot(vt_ref[0, h, kb], p.astype(BF16), preferred_element_type=F32)
            m_c[h] = m_new
            l_c[h] = alpha * l_c[h] + jnp.sum(p, axis=0, keepdims=True)
            acc_c[h] = alpha * acc_c[h] + pv

        pipelined([(kb, h) for kb in kbs for h in hs], stage)
        return tuple(m_c), tuple(l_c), tuple(acc_c)

    u = MOBA_UNROLL
    carry = lax.fori_loop(0, qb // u, lambda j, c: past_blocks([j * u + i for i in range(u)], c),
                          (tuple(m_run), tuple(l_run), tuple(acc)))
    _, l_fin, acc = lax.fori_loop((qb // u) * u, qb, lambda kb, c: past_blocks([kb], c), carry)
    for h in hs:
        o_ref[0, :, h * dh:(h + 1) * dh] = (acc[h] / l_fin[h]).T.astype(o_ref.dtype)


def _moba(qt, k, vt, kmean, heads_per_step=8):
    bsz, nh, nb, dh, blk = qt.shape
    hp = heads_per_step
    assert nh % hp == 0
    return pl.pallas_call(
        _moba_kernel,
        grid=(bsz, nh // hp, nb),
        in_specs=[pl.BlockSpec((1, hp, 1, dh, blk), lambda b, h, i: (b, h, i, 0, 0)),
                  pl.BlockSpec((1, hp, nb, blk, dh), lambda b, h, i: (b, h, 0, 0, 0)),
                  pl.BlockSpec((1, hp, nb, dh, blk), lambda b, h, i: (b, h, 0, 0, 0)),
                  pl.BlockSpec((1, nb, hp * dh), lambda b, h, i: (b, 0, h))],
        out_specs=pl.BlockSpec((1, blk, hp * dh), lambda b, h, i: (b, i, h)),
        out_shape=jax.ShapeDtypeStruct((bsz, nb * blk, nh * dh), BF16),
        scratch_shapes=[pltpu.VMEM((hp, nb, blk), F32)],
        compiler_params=_cparams("parallel", "parallel", "arbitrary"),
        name="moba",
    )(qt, k, vt, kmean)


def _out_proj_kernel(yr_ref, ya_ref, wr_ref, wa_ref, x_ref, g_ref, o_ref):
    for r in range(0, x_ref.shape[0], ROW_CHUNK):
        rows = pl.ds(r, ROW_CHUNK)
        y = jnp.dot(yr_ref[rows, :], wr_ref[...], preferred_element_type=F32)
        y += jnp.dot(ya_ref[rows, :], wa_ref[...], preferred_element_type=F32)
        ms = jnp.mean(y * y, axis=-1, keepdims=True)
        o_ref[rows, :] = x_ref[rows, :] + y * lax.rsqrt(ms + NORM_EPS) * g_ref[...]


def _out_proj(y_r, y_a, w_all, layer, x2, gain, tm=512):
    m, d = x2.shape
    cw = y_r.shape[1]
    assert y_a.shape[1] == cw and w_all.shape[1] == 2 * cw and m % tm == 0
    return pl.pallas_call(
        _out_proj_kernel,
        grid=(m // tm,),
        in_specs=[pl.BlockSpec((tm, cw), lambda i: (i, 0)), pl.BlockSpec((tm, cw), lambda i: (i, 0)),
                  pl.BlockSpec((None, cw, d), lambda i: (layer, 0, 0)),
                  pl.BlockSpec((None, cw, d), lambda i: (layer, 1, 0)),
                  pl.BlockSpec((tm, d), lambda i: (i, 0)), pl.BlockSpec((1, d), lambda i: (0, 0))],
        out_specs=pl.BlockSpec((tm, d), lambda i: (i, 0)),
        out_shape=jax.ShapeDtypeStruct((m, d), F32),
        compiler_params=_cparams("parallel"),
        name="out_proj",
    )(y_r, y_a, w_all, w_all, x2, gain.reshape(1, d))


def _mlp_kernel(x_ref, gpre_ref, wu_ref, wd_ref, gpost_ref, o_ref, h_ref, acc_ref):
    f = pl.program_id(1)
    last = pl.num_programs(1) - 1
    chunks = [pl.ds(r, ROW_CHUNK) for r in range(0, x_ref.shape[0], ROW_CHUNK)]

    def part(h):
        u = jnp.maximum(jnp.dot(h, wu_ref[...], preferred_element_type=F32), 0.0)
        return jnp.dot((u * u).astype(BF16), wd_ref[...], preferred_element_type=F32)

    @pl.when(f == 0)
    def _():
        for rows in chunks:
            x = x_ref[rows, :]
            ms = jnp.mean(x * x, axis=-1, keepdims=True)
            h = (x * lax.rsqrt(ms + NORM_EPS) * gpre_ref[...]).astype(BF16)
            h_ref[rows, :] = h
            acc_ref[rows, :] = part(h)

    @pl.when((f > 0) & (f < last))
    def _():
        acc_ref[...] += part(h_ref[...])

    @pl.when(f == last)
    def _():
        for rows in chunks:
            mlp = acc_ref[rows, :] + part(h_ref[rows, :])
            ms = jnp.mean(mlp * mlp, axis=-1, keepdims=True)
            o_ref[rows, :] = x_ref[rows, :] + mlp * lax.rsqrt(ms + NORM_EPS) * gpost_ref[...]


def _mlp(x2, g_pre, w_up, w_down, layer, g_post, tm=512, tf=1024):
    m, d = x2.shape
    dff = w_up.shape[2]
    assert m % tm == 0 and dff % tf == 0 and dff // tf >= 2 and tm % ROW_CHUNK == 0
    return pl.pallas_call(
        _mlp_kernel,
        grid=(m // tm, dff // tf),
        in_specs=[pl.BlockSpec((tm, d), lambda i, f: (i, 0)),
                  pl.BlockSpec((1, d), lambda i, f: (0, 0)),
                  pl.BlockSpec((None, d, tf), lambda i, f: (layer, 0, f)),
                  pl.BlockSpec((None, tf, d), lambda i, f: (layer, f, 0)),
                  pl.BlockSpec((1, d), lambda i, f: (0, 0))],
        out_specs=pl.BlockSpec((tm, d), lambda i, f: (i, 0)),
        out_shape=jax.ShapeDtypeStruct((m, d), F32),
        scratch_shapes=[pltpu.VMEM((tm, d), BF16), pltpu.VMEM((tm, d), F32)],
        compiler_params=_cparams("parallel", "arbitrary"),
        name="mlp",
    )(x2, g_pre.reshape(1, d), w_up, w_down, g_post.reshape(1, d))


def _head_indicators(c):
    head = jnp.arange(c) // RWKV_HEAD
    hsum = (head[:, None] == jnp.arange(128)[None, :]).astype(BF16)
    return hsum, hsum.T


def _rope_tables(s):
    half = ATTN_HEAD // 2
    inv_freq = ROPE_THETA ** (-jnp.arange(half, dtype=F32) / half)
    ang = jnp.arange(s).astype(F32)[:, None] * inv_freq[None, :]
    cos, sin = jnp.cos(ang), jnp.sin(ang)
    return jnp.concatenate([cos, cos], axis=-1), jnp.concatenate([-sin, sin], axis=-1)


def kernel(x, norm_mix_pre, norm_mix_post, norm_mlp_pre, norm_mlp_post, w_in, w_in_vres, shift_mu, shift_mu_vres, decay_w0, decay_w2, iclr_a0, iclr_a2, vres_v0, vres_v2, gate_g2, k_k, k_a, r_k, lnx_gain, lnx_bias, w_out, w_up, w_down):
    bsz, s, d = x.shape
    depth = w_in.shape[0]
    c = decay_w0.shape[1]
    n_lora = DECAY_LORA + ICLR_LORA + GATE_LORA
    n_shift = 3 * c + n_lora
    ca = (w_in.shape[2] - n_shift) // 3
    hsum, hbc = _head_indicators(c)
    cos2, sin2 = _rope_tables(s)
    w_in16, w_out16, w_up16, w_down16 = (w.astype(BF16) for w in (w_in, w_out, w_up, w_down))
    pad_v = VRES_PAD - VRES_LORA
    x2 = x.reshape(bsz * s, d)
    v_first = None
    for i in range(depth):
        if i == 0:
            z, vres = _norm_matmul(x2, norm_mix_pre[i], w_in16, i)[0], None
        else:
            w_v = jnp.pad(w_in_vres[i - 1], ((0, 0), (0, pad_v))).astype(BF16)
            z, z_v = _norm_matmul(x2, norm_mix_pre[i], w_in16, i, w_v)
            vres = (z_v.reshape(bsz, s, VRES_PAD), jnp.pad(shift_mu_vres[i - 1], (0, pad_v)), v_first,
                    vres_v0[i - 1], jnp.pad(vres_v2[i - 1], ((0, pad_v), (0, 0))))
        z3 = z.reshape(bsz, s, -1)

        prep = _rwkv_prep(z3, c, shift_mu[i], decay_w0[i], decay_w2[i], iclr_a0[i], iclr_a2[i], gate_g2[i],
                          k_k[i], k_a[i], hsum, hbc, vres)
        if i == 0:
            v_first = prep[7]
        y_r = _wkv(*prep[:7], r_k[i].reshape(-1), lnx_gain[i], lnx_bias[i])

        q_t, k_rot, v_t, kmean = _rope(z3, n_shift, ca, cos2, sin2)
        y_a = _moba(q_t, k_rot, v_t, kmean.reshape(bsz, -1, ca))

        x2 = _out_proj(y_r.reshape(bsz * s, c), y_a.reshape(bsz * s, ca), w_out16, i, x2, norm_mix_post[i])
        x2 = _mlp(x2, norm_mlp_pre[i], w_up16, w_down16, i, norm_mlp_post[i])
    return x2.reshape(bsz, s, d)
```

```python
import functools

import jax
import jax.numpy as jnp
from jax import lax
from jax.experimental import pallas as pl
from jax.experimental.pallas import tpu as pltpu

F32 = jnp.float32
BF16 = jnp.bfloat16

RWKV_HEAD = 64
DECAY_LORA = 64
ICLR_LORA = 64
VRES_LORA = 32
GATE_LORA = 128
ATTN_HEAD = 128
MOBA_BLOCK = 256
MOBA_TOPK = 3
ROPE_THETA = 10000.0
NORM_EPS = 1e-6
LNX_EPS = 64e-5
LOG2E = 1.4426950408889634

VRES_PAD = 128
WKV_CHUNK = 64
MOBA_LOOKAHEAD = 6
MOBA_UNROLL = 4
ROW_CHUNK = 256
VMEM_LIMIT = 56 * 1024 * 1024


def _cparams(*sem):
    return pltpu.CompilerParams(dimension_semantics=sem, vmem_limit_bytes=VMEM_LIMIT)


_NN = (((1,), (0,)), ((), ()))
_NT = (((1,), (1,)), ((), ()))
_TN = (((0,), (0,)), ((), ()))


def _split2(x):
    hi = x.astype(BF16)
    lo = (x - hi.astype(F32)).astype(BF16)
    return hi, lo


def _split3(x):
    hi = x.astype(BF16)
    r1 = x - hi.astype(F32)
    mid = r1.astype(BF16)
    lo = (r1 - mid.astype(F32)).astype(BF16)
    return hi, mid, lo


def _mm(a, b, dims=_NN, passes=1):
    d = lambda p, q: lax.dot_general(p, q, dims, preferred_element_type=F32)
    if passes == 1:
        return d(a.astype(BF16), b.astype(BF16))
    ah, al = _split2(a)
    bh, bl = _split2(b)
    return d(ah, bh) + (d(ah, bl) + d(al, bh))


def _sigmoid(x):
    return 1.0 / (1.0 + jnp.exp(-x))


def _softplus(x):
    return jnp.maximum(x, 0.0) + jnp.log(1.0 + jnp.exp(-jnp.abs(x)))


def _norm_matmul_kernel(has_extra, *refs):
    if has_extra:
        x_ref, g_ref, w_ref, we_ref, o_ref, oe_ref, h_ref = refs
    else:
        x_ref, g_ref, w_ref, o_ref, h_ref = refs

    first = pl.program_id(1) == 0

    @pl.when(first)
    def _():
        for r in range(0, x_ref.shape[0], ROW_CHUNK):
            rows = pl.ds(r, ROW_CHUNK)
            x = x_ref[rows, :]
            ms = jnp.mean(x * x, axis=-1, keepdims=True)
            h = (x * lax.rsqrt(ms + NORM_EPS) * g_ref[...]).astype(BF16)
            h_ref[rows, :] = h
            o_ref[rows, :] = jnp.dot(h, w_ref[...], preferred_element_type=F32)
            if has_extra:
                oe_ref[rows, :] = jnp.dot(h, we_ref[...], preferred_element_type=F32)

    @pl.when(jnp.logical_not(first))
    def _():
        o_ref[...] = jnp.dot(h_ref[...], w_ref[...], preferred_element_type=F32)


def _norm_matmul(x2, gain, w_all, layer, w_extra=None, tm=1024, tn=1280):
    m, d = x2.shape
    n = w_all.shape[2]
    assert m % tm == 0 and n % tn == 0
    has_extra = w_extra is not None
    ins = [x2, gain.reshape(1, d), w_all]
    in_specs = [pl.BlockSpec((tm, d), lambda i, j: (i, 0)),
                pl.BlockSpec((1, d), lambda i, j: (0, 0)),
                pl.BlockSpec((None, d, tn), lambda i, j: (layer, 0, j))]
    out_specs = [pl.BlockSpec((tm, tn), lambda i, j: (i, j))]
    out_shape = [jax.ShapeDtypeStruct((m, n), F32)]
    if has_extra:
        ne = w_extra.shape[1]
        ins.append(w_extra)
        in_specs.append(pl.BlockSpec((d, ne), lambda i, j: (0, 0)))
        out_specs.append(pl.BlockSpec((tm, ne), lambda i, j: (i, 0)))
        out_shape.append(jax.ShapeDtypeStruct((m, ne), F32))
    return pl.pallas_call(
        functools.partial(_norm_matmul_kernel, has_extra),
        grid=(m // tm, n // tn),
        in_specs=in_specs,
        out_specs=out_specs,
        out_shape=out_shape,
        scratch_shapes=[pltpu.VMEM((tm, d), BF16)],
        compiler_params=_cparams("parallel", "arbitrary"),
        name="norm_matmul",
    )(*ins)


def _shifted(cur, halo, mu, first):
    rows = lax.broadcasted_iota(jnp.int32, cur.shape, 0)
    last = jnp.where(first, 0.0, halo[7:8, :])
    prev = jnp.where(rows == 0, last, pltpu.roll(cur, 1, axis=0))
    return cur + (prev - cur) * mu


def _rwkv_prep_kernel(has_vres, *refs):
    if has_vres:
        (zm_ref, zmh_ref, zl_ref, zlh_ref, mum_ref, mul_ref, w0_ref, w2_ref, a0_ref, a2_ref, g2_ref,
         kk_ref, ka_ref, zv_ref, zvh_ref, muv_ref, vf_ref, v0_ref, v2_ref,
         r_ref, lw_ref, k_ref, v_ref, an_ref, b_ref, g_ref) = refs
    else:
        (zm_ref, zmh_ref, zl_ref, zlh_ref, mum_ref, mul_ref, w0_ref, w2_ref, a0_ref, a2_ref, g2_ref,
         kk_ref, ka_ref,
         r_ref, lw_ref, k_ref, v_ref, an_ref, b_ref, g_ref, vf32_ref) = refs
    c = r_ref.shape[-1]
    first = pl.program_id(1) == 0
    zs = _shifted(zm_ref[0], zmh_ref[0], mum_ref[...], first)
    zl = _shifted(zl_ref[0], zlh_ref[0], mul_ref[...], first)
    r, k, v = zs[:, :c], zs[:, c:2 * c], zs[:, 2 * c:3 * c]
    o = 0
    wd = zl[:, o:o + DECAY_LORA]
    o += DECAY_LORA
    ad = zl[:, o:o + ICLR_LORA]
    o += ICLR_LORA
    gd = zl[:, o:o + GATE_LORA]
    o += GATE_LORA
    w_log = -_softplus(-(w0_ref[...] + _mm(jnp.tanh(wd), w2_ref[...], passes=3))) - 0.5
    lw_ref[0] = -jnp.exp(w_log)
    a = _sigmoid(a0_ref[...] + _mm(ad, a2_ref[...]))
    g_ref[0] = _mm(_sigmoid(gd), g2_ref[...]).astype(g_ref.dtype)
    if has_vres:
        vd = _shifted(zv_ref[0], zvh_ref[0], muv_ref[...], first)
        v = v + (vf_ref[0] - v) * _sigmoid(v0_ref[...] + _mm(vd, v2_ref[...]))
    else:
        vf32_ref[0] = v
    kk = k * kk_ref[...]
    pw = 2 * RWKV_HEAD
    even = lax.broadcasted_iota(jnp.int32, (kk.shape[0], pw), 1) < RWKV_HEAD
    unit = []
    for p in range(c // pw):
        x = kk[:, p * pw:(p + 1) * pw]
        sq = x * x
        ss = jnp.where(even, jnp.sum(jnp.where(even, sq, 0.0), axis=-1, keepdims=True),
                       jnp.sum(jnp.where(even, 0.0, sq), axis=-1, keepdims=True))
        unit.append(x / jnp.maximum(jnp.sqrt(ss), 1e-12))
    kk = jnp.concatenate(unit, axis=1)
    r_ref[0] = r.astype(r_ref.dtype)
    k_ref[0] = (k * (1.0 + (a - 1.0) * ka_ref[...])).astype(k_ref.dtype)
    v_ref[0] = v.astype(v_ref.dtype)
    an_ref[0] = (-kk).astype(an_ref.dtype)
    b_ref[0] = (kk * a).astype(b_ref.dtype)


def _rwkv_prep(z3, c, mu, w0, w2, a0, a2, g2, k_k, k_a, vres, tm=256):
    bsz, s, _ = z3.shape
    n_lora = w2.shape[0] + a2.shape[0] + g2.shape[0]
    assert s % tm == 0 and (3 * c) % n_lora == 0 and n_lora % 128 == 0
    has_vres = vres is not None
    row = lambda a: a.reshape(1, -1)
    hb = tm // 8
    lcb = 3 * c // n_lora
    halo = lambda b, i: (b, jnp.maximum(i * hb - 1, 0), 0)
    halo_l = lambda b, i: (b, jnp.maximum(i * hb - 1, 0), lcb)
    full = lambda a: pl.BlockSpec(a.shape, lambda b, i: (0,) * a.ndim)
    ins = [z3, z3, z3, z3, row(mu[:3 * c]), row(mu[3 * c:]), row(w0), w2, row(a0), a2, g2, row(k_k), row(k_a)]
    in_specs = [pl.BlockSpec((1, tm, 3 * c), lambda b, i: (b, i, 0)),
                pl.BlockSpec((1, 8, 3 * c), halo),
                pl.BlockSpec((1, tm, n_lora), lambda b, i: (b, i, lcb)),
                pl.BlockSpec((1, 8, n_lora), halo_l)] + [full(a) for a in ins[4:]]
    if has_vres:
        zv3, mu_v, v_first, v0, v2 = vres
        nv = zv3.shape[-1]
        extra = [zv3, zv3, row(mu_v), v_first, row(v0), v2]
        ins += extra
        in_specs += [pl.BlockSpec((1, tm, nv), lambda b, i: (b, i, 0)), pl.BlockSpec((1, 8, nv), halo),
                     full(extra[2]), pl.BlockSpec((1, tm, c), lambda b, i: (b, i, 0)), full(extra[4]),
                     full(extra[5])]
    out_spec = pl.BlockSpec((1, tm, c), lambda b, i: (b, i, 0))
    sd = lambda dt: jax.ShapeDtypeStruct((bsz, s, c), dt)
    out_dtypes = [BF16, F32, BF16, BF16, BF16, BF16, BF16] + ([] if has_vres else [F32])
    return pl.pallas_call(
        functools.partial(_rwkv_prep_kernel, has_vres),
        grid=(bsz, s // tm),
        in_specs=in_specs,
        out_specs=[out_spec] * len(out_dtypes),
        out_shape=[sd(dt) for dt in out_dtypes],
        compiler_params=_cparams("parallel", "arbitrary"),
        name="rwkv_prep",
    )(*ins)


def _dots(a_list, b_list, dims=_NN):
    return [lax.dot_general(a.astype(BF16), b.astype(BF16), dims, preferred_element_type=F32)
            for a, b in zip(a_list, b_list)]


def _pair_diag(x, even):
    return jnp.concatenate([jnp.where(even, x, 0.0), jnp.where(even, 0.0, x)], axis=0)


def _unit_lower_inverse(a_list, row_w, col_w, even):
    n = row_w.shape[0]
    lower = row_w > col_w
    base = lower & ((row_w >> 1) == (col_w >> 1))
    t = [jnp.where(row_w == col_w, 1.0, jnp.where(base, a, 0.0)) for a in a_list]
    sh = 1
    while (2 << sh) <= n:
        sub = lower & ((row_w >> (sh + 1)) == (col_w >> (sh + 1))) & ((row_w >> sh) != (col_w >> sh))
        off = [_pair_diag(jnp.where(sub, a, 0.0), even) for a in a_list]
        upd = _dots(_dots(t, off), [_pair_diag(x, even) for x in t])
        t = [x + u for x, u in zip(t, upd)]
        sh += 1
    return t


def _mm_exact_rhs_left(l_bf16, a):
    d = lambda p: lax.dot_general(l_bf16, p, _NN, preferred_element_type=F32)
    hi, mid, lo = _split3(a)
    return d(hi) + (d(mid) + d(lo))


def _wkv_kernel(r_ref, lw_ref, k_ref, v_ref, an_ref, b_ref, g_ref, rk_ref, lg_ref, lb_ref, y_ref, state_ref):
    nb, ln = r_ref.shape[0], r_ref.shape[1]
    n = RWKV_HEAD
    pw = 2 * n
    npair = r_ref.shape[2] // pw
    ent = [(bi, slice(p * pw, (p + 1) * pw)) for bi in range(nb) for p in range(npair)]

    @pl.when(pl.program_id(1) == 0)
    def _():
        state_ref[...] = jnp.zeros_like(state_ref)

    rows = lax.broadcasted_iota(jnp.int32, (ln, ln), 0)
    cols = lax.broadcasted_iota(jnp.int32, (ln, ln), 1)
    tril = jnp.where(rows >= cols, 1.0, 0.0).astype(BF16)
    r, k, v, w_end, r_t, a_t, b_t, k_t, b_h, k_h = ([] for _ in range(10))
    for bi in range(nb):
        lw = lw_ref[bi]
        cw = _mm_exact_rhs_left(tril, lw)
        cw_end = cw[ln - 1:ln, :]
        e_neg = jnp.exp(-cw)
        e_end = jnp.exp(cw_end - cw)
        bb = b_ref[bi].astype(F32)
        r.append(r_ref[bi].astype(F32))
        k.append(k_ref[bi].astype(F32))
        v.append(v_ref[bi].astype(F32))
        w_end.append(jnp.exp(cw_end))
        r_t.append(r[bi] * jnp.exp(cw))
        a_t.append(an_ref[bi].astype(F32) * jnp.exp(cw - lw))
        b_t.append(bb * e_neg)
        k_t.append(k[bi] * e_neg)
        b_h.append(bb * e_end)
        k_h.append(k[bi] * e_end)

    lane = lax.broadcasted_iota(jnp.int32, (ln, pw), 1)
    row_w = lax.broadcasted_iota(jnp.int32, (ln, pw), 0)
    even = lane < n
    even2 = lax.broadcasted_iota(jnp.int32, (2 * ln, pw), 1) < n
    col_w = lane & (n - 1)
    strict_w = row_w > col_w
    incl_w = row_w >= col_w
    zeros_w = jnp.zeros((ln, pw), F32)
    diag = lambda x: _pair_diag(x, even)
    swap = lambda x: jnp.concatenate([x[x.shape[0] // 2:], x[:x.shape[0] // 2]], axis=0)

    ar_p = [jnp.concatenate([a_t[bi][:, ps], r_t[bi][:, ps]], axis=0) for bi, ps in ent]
    bk_p = [jnp.concatenate([b_t[bi][:, ps], k_t[bi][:, ps]], axis=0).astype(BF16) for bi, ps in ent]
    kb_p = [jnp.concatenate([k_t[bi][:, ps], b_t[bi][:, ps]], axis=0).astype(BF16) for bi, ps in ent]
    am_e = _dots([jnp.where(even2, x, 0.0) for x in ar_p], bk_p, _NT)
    am_o = _dots([jnp.where(even2, 0.0, x) for x in ar_p], kb_p, _NT)
    a_ab = [jnp.where(strict_w, jnp.where(even, e[:ln], o[:ln]), 0.0) for e, o in zip(am_e, am_o)]
    a_ak = [jnp.where(strict_w, jnp.where(even, o[:ln], e[:ln]), 0.0) for e, o in zip(am_e, am_o)]
    a_rb = [jnp.where(incl_w, jnp.where(even, e[ln:], o[ln:]), 0.0) for e, o in zip(am_e, am_o)]
    a_rk = [jnp.where(incl_w, jnp.where(even, o[ln:], e[ln:]), 0.0) for e, o in zip(am_e, am_o)]
    v_p = [v[bi][:, ps] for bi, ps in ent]
    v_d = [diag(x) for x in v_p]
    akv = _dots(a_ak, [swap(x) for x in v_d])
    t = _unit_lower_inverse(a_ab, row_w, col_w, even)
    rhs = [jnp.concatenate([diag(a_t[bi][:, ps]), diag(x)], axis=1) for (bi, ps), x in zip(ent, akv)]
    pq = _dots(t, rhs)
    ry = _dots([jnp.concatenate([x, y], axis=1) for x, y in zip(a_rb, a_rk)],
               [jnp.concatenate([jnp.concatenate([diag(x[:, :pw]), diag(x[:, pw:])], axis=1),
                                 jnp.concatenate([jnp.zeros((2 * ln, pw), F32), swap(u)], axis=1)], axis=0)
                for x, u in zip(pq, v_d)])
    pqv_p = [jnp.concatenate([x, jnp.concatenate([zeros_w, u], axis=1)], axis=0)
             for x, u in zip(pq, v_p)]
    bkh_p = [jnp.concatenate([b_h[bi][:, ps], k_h[bi][:, ps]], axis=0) for bi, ps in ent]
    mn_p = _dots(bkh_p, pqv_p, _TN)
    sq_r = lax.broadcasted_iota(jnp.int32, (pw, pw), 0)
    sq_c = lax.broadcasted_iota(jnp.int32, (pw, pw), 1)
    same_head = (sq_r < n) == (sq_c < n)
    lhs_p = [jnp.concatenate([r_t[bi][:, ps] + x[:, :pw],
                              jnp.where(sq_r == sq_c, jnp.broadcast_to(w_end[bi][:, ps], (pw, pw)),
                                        jnp.where(same_head, m[:, :pw], 0.0))], axis=0)
             for (bi, ps), x, m in zip(ent, ry, mn_p)]
    st = [state_ref[e] for e in range(len(ent))]
    upd = _dots(lhs_p, st)
    for e in range(len(ent)):
        state_ref[e] = upd[e][ln:, :] + jnp.where(same_head, mn_p[e][:, pw:], 0.0)
    y0_p = [x[:, pw:] for x in ry]

    hsum = lambda x: jnp.where(even, jnp.sum(jnp.where(even, x, 0.0), axis=-1, keepdims=True),
                               jnp.sum(jnp.where(even, 0.0, x), axis=-1, keepdims=True))
    inv_n = 1.0 / n
    yn = []
    for u, y0 in zip(upd, y0_p):
        y = u[:ln, :] + y0
        d = y - hsum(y) * inv_n
        yn.append(d * lax.rsqrt(hsum(d * d) * inv_n + LNX_EPS))
    for bi in range(nb):
        rk = r[bi] * k[bi] * rk_ref[...]
        mine = range(bi * npair, (bi + 1) * npair)
        bonus = jnp.concatenate([hsum(rk[:, ent[e][1]]) * v_p[e] for e in mine], axis=1)
        out = jnp.concatenate([yn[e] for e in mine], axis=1) * lg_ref[...] + lb_ref[...] + bonus
        y_ref[bi] = (out * g_ref[bi].astype(F32)).astype(y_ref.dtype)


def _wkv(r, lw, k, v, an, b, g, r_k, lnx_g, lnx_b, heads_per_step=16):
    bsz, s, c = r.shape
    wb = heads_per_step * RWKV_HEAD
    nc = s // WKV_CHUNK
    assert s % WKV_CHUNK == 0 and c % wb == 0 and heads_per_step % 2 == 0 and WKV_CHUNK == RWKV_HEAD
    spec = pl.BlockSpec((bsz, WKV_CHUNK, wb), lambda hi, ci: (0, ci, hi))
    pspec = pl.BlockSpec((1, wb), lambda hi, ci: (0, hi))
    row = lambda a: a.reshape(1, -1)
    return pl.pallas_call(
        _wkv_kernel,
        grid=(c // wb, nc),
        in_specs=[spec] * 7 + [pspec] * 3,
        out_specs=spec,
        out_shape=jax.ShapeDtypeStruct((bsz, s, c), BF16),
        scratch_shapes=[pltpu.VMEM((bsz * heads_per_step // 2, 2 * RWKV_HEAD, 2 * RWKV_HEAD), F32)],
        compiler_params=_cparams("parallel", "arbitrary"),
        name="wkv",
    )(r, lw, k, v, an, b, g, row(r_k), row(lnx_g), row(lnx_b))


def _rope_kernel(q_ref, k_ref, v_ref, cos_ref, sin_ref, qt_ref, ko_ref, vt_ref, km_ref):
    cos = cos_ref[...]
    sin = sin_ref[...]
    nh = q_ref.shape[-1] // ATTN_HEAD
    inv_rows = 1.0 / q_ref.shape[1]
    for h in range(nh):
        sl = slice(h * ATTN_HEAD, (h + 1) * ATTN_HEAD)
        q = q_ref[0, :, sl]
        k = k_ref[0, :, sl]
        qr = q * cos + pltpu.roll(q, ATTN_HEAD // 2, axis=1) * sin
        kr = k * cos + pltpu.roll(k, ATTN_HEAD // 2, axis=1) * sin
        qt_ref[0, h, 0] = qr.T
        ko_ref[0, h, 0] = kr.astype(BF16)
        vt_ref[0, h, 0] = v_ref[0, :, sl].T.astype(BF16)
        km_ref[0, 0, :, sl] = jnp.sum(kr, axis=0, keepdims=True) * inv_rows


def _rope(z3, col0, width, cos2, sin2):
    bsz, s, _ = z3.shape
    assert s % MOBA_BLOCK == 0 and col0 % ATTN_HEAD == 0
    nb = s // MOBA_BLOCK
    nh = width // ATTN_HEAD
    blk = lambda j: pl.BlockSpec((pl.Element(1), pl.Element(MOBA_BLOCK), pl.Element(width)),
                                 lambda b, i: (b, i * MOBA_BLOCK, col0 + j * width))
    tab = pl.BlockSpec((MOBA_BLOCK, ATTN_HEAD), lambda b, i: (i, 0))
    t_spec = pl.BlockSpec((1, nh, 1, ATTN_HEAD, MOBA_BLOCK), lambda b, i: (b, 0, i, 0, 0))
    n_spec = pl.BlockSpec((1, nh, 1, MOBA_BLOCK, ATTN_HEAD), lambda b, i: (b, 0, i, 0, 0))
    return pl.pallas_call(
        _rope_kernel,
        grid=(bsz, nb),
        in_specs=[blk(0), blk(1), blk(2), tab, tab],
        out_specs=[t_spec, n_spec, t_spec, pl.BlockSpec((1, 1, 1, width), lambda b, i: (b, i, 0, 0))],
        out_shape=[jax.ShapeDtypeStruct((bsz, nh, nb, ATTN_HEAD, MOBA_BLOCK), F32),
                   jax.ShapeDtypeStruct((bsz, nh, nb, MOBA_BLOCK, ATTN_HEAD), BF16),
                   jax.ShapeDtypeStruct((bsz, nh, nb, ATTN_HEAD, MOBA_BLOCK), BF16),
                   jax.ShapeDtypeStruct((bsz, nb, 1, width), F32)],
        compiler_params=_cparams("parallel", "parallel"),
        name="rope",
    )(z3, z3, z3, cos2, sin2)


def _moba_kernel(qt_ref, k_ref, vt_ref, km_ref, o_ref, bias_ref):
    blk = MOBA_BLOCK
    dh = ATTN_HEAD
    hs = range(qt_ref.shape[1])
    qb = pl.program_id(2)
    nb = km_ref.shape[1]
    scale = ATTN_HEAD ** -0.5
    neg = -jnp.inf
    qt = [qt_ref[0, h, 0] for h in hs]
    blk_id = lax.broadcasted_iota(jnp.int32, (nb, blk), 0)
    past = blk_id < qb
    gate = [jnp.where(past, _mm(km_ref[0, :, h * dh:(h + 1) * dh], qt[h], passes=3), neg) for h in hs]
    for h in hs:
        rank = jnp.zeros((nb, blk), jnp.int32)
        for m in range(nb):
            gm = gate[h][m:m + 1, :]
            rank += ((gm > gate[h]) | ((gm == gate[h]) & (m < blk_id))).astype(jnp.int32)
        bias_ref[h] = jnp.where(past & (rank < MOBA_TOPK), 0.0, neg)

    qs = [(q * (scale * LOG2E)).astype(BF16) for q in qt]
    ki = lax.broadcasted_iota(jnp.int32, (blk, blk), 0)
    qi = lax.broadcasted_iota(jnp.int32, (blk, blk), 1)
    causal = ki <= qi

    def pipelined(work, stage):
        scores = lambda kb, h: jnp.dot(k_ref[0, h, kb], qs[h], preferred_element_type=F32)
        ahead = [scores(*w) for w in work[:MOBA_LOOKAHEAD]]
        for i, (kb, h) in enumerate(work):
            if i + MOBA_LOOKAHEAD < len(work):
                ahead.append(scores(*work[i + MOBA_LOOKAHEAD]))
            stage(kb, h, ahead[i])

    m_run, l_run, acc = [None] * len(hs), [None] * len(hs), [None] * len(hs)

    def own_block(kb, h, s):
        s = jnp.where(causal, s, neg)
        m_run[h] = jnp.max(s, axis=0, keepdims=True)
        p = jnp.exp2(s - m_run[h])
        l_run[h] = jnp.sum(p, axis=0, keepdims=True)
        acc[h] = jnp.dot(vt_ref[0, h, kb], p.astype(BF16), preferred_element_type=F32)

    pipelined([(qb, h) for h in hs], own_block)

    def past_blocks(kbs, carry):
        m_c, l_c, acc_c = (list(c) for c in carry)

        def stage(kb, h, s):
            b = bias_ref[h, pl.ds(kb, 1), :]
            m_new = jnp.where(b == 0.0, jnp.maximum(m_c[h], jnp.max(s, axis=0, keepdims=True)), m_c[h])
            alpha = jnp.exp2(m_c[h] - m_new)
            p = jnp.exp2(s - (m_new - b))
            pv = jnp.dot(vt_ref[0, h, kb], p.astype(BF16), preferred_element_type=F32)
            m_c[h] = m_new
            l_c[h] = alpha * l_c[h] + jnp.sum(p, axis=0, keepdims=True)
            acc_c[h] = alpha * acc_c[h] + pv

        pipelined([(kb, h) for kb in kbs for h in hs], stage)
        return tuple(m_c), tuple(l_c), tuple(acc_c)

    u = MOBA_UNROLL
    carry = lax.fori_loop(0, qb // u, lambda j, c: past_blocks([j * u + i for i in range(u)], c),
                          (tuple(m_run), tuple(l_run), tuple(acc)))
    _, l_fin, acc = lax.fori_loop((qb // u) * u, qb, lambda kb, c: past_blocks([kb], c), carry)
    for h in hs:
        o_ref[0, :, h * dh:(h + 1) * dh] = (acc[h] / l_fin[h]).T.astype(o_ref.dtype)


def _moba(qt, k, vt, kmean, heads_per_step=8):
    bsz, nh, nb, dh, blk = qt.shape
    hp = heads_per_step
    assert nh % hp == 0
    return pl.pallas_call(
        _moba_kernel,
        grid=(bsz, nh // hp, nb),
        in_specs=[pl.BlockSpec((1, hp, 1, dh, blk), lambda b, h, i: (b, h, i, 0, 0)),
                  pl.BlockSpec((1, hp, nb, blk, dh), lambda b, h, i: (b, h, 0, 0, 0)),
                  pl.BlockSpec((1, hp, nb, dh, blk), lambda b, h, i: (b, h, 0, 0, 0)),
                  pl.BlockSpec((1, nb, hp * dh), lambda b, h, i: (b, 0, h))],
        out_specs=pl.BlockSpec((1, blk, hp * dh), lambda b, h, i: (b, i, h)),
        out_shape=jax.ShapeDtypeStruct((bsz, nb * blk, nh * dh), BF16),
        scratch_shapes=[pltpu.VMEM((hp, nb, blk), F32)],
        compiler_params=_cparams("parallel", "parallel", "arbitrary"),
        name="moba",
    )(qt, k, vt, kmean)


def _out_proj_kernel(yr_ref, ya_ref, wr_ref, wa_ref, x_ref, g_ref, o_ref):
    for r in range(0, x_ref.shape[0], ROW_CHUNK):
        rows = pl.ds(r, ROW_CHUNK)
        y = jnp.dot(yr_ref[rows, :], wr_ref[...], preferred_element_type=F32)
        y += jnp.dot(ya_ref[rows, :], wa_ref[...], preferred_element_type=F32)
        ms = jnp.mean(y * y, axis=-1, keepdims=True)
        o_ref[rows, :] = x_ref[rows, :] + y * lax.rsqrt(ms + NORM_EPS) * g_ref[...]


def _out_proj(y_r, y_a, w_all, layer, x2, gain, tm=512):
    m, d = x2.shape
    cw = y_r.shape[1]
    assert y_a.shape[1] == cw and w_all.shape[1] == 2 * cw and m % tm == 0
    return pl.pallas_call(
        _out_proj_kernel,
        grid=(m // tm,),
        in_specs=[pl.BlockSpec((tm, cw), lambda i: (i, 0)), pl.BlockSpec((tm, cw), lambda i: (i, 0)),
                  pl.BlockSpec((None, cw, d), lambda i: (layer, 0, 0)),
                  pl.BlockSpec((None, cw, d), lambda i: (layer, 1, 0)),
                  pl.BlockSpec((tm, d), lambda i: (i, 0)), pl.BlockSpec((1, d), lambda i: (0, 0))],
        out_specs=pl.BlockSpec((tm, d), lambda i: (i, 0)),
        out_shape=jax.ShapeDtypeStruct((m, d), F32),
        compiler_params=_cparams("parallel"),
        name="out_proj",
    )(y_r, y_a, w_all, w_all, x2, gain.reshape(1, d))


def _mlp_kernel(x_ref, gpre_ref, wu_ref, wd_ref, gpost_ref, o_ref, h_ref, acc_ref):
    f = pl.program_id(1)
    last = pl.num_programs(1) - 1
    chunks = [pl.ds(r, ROW_CHUNK) for r in range(0, x_ref.shape[0], ROW_CHUNK)]

    def part(h):
        u = jnp.maximum(jnp.dot(h, wu_ref[...], preferred_element_type=F32), 0.0)
        return jnp.dot((u * u).astype(BF16), wd_ref[...], preferred_element_type=F32)

    @pl.when(f == 0)
    def _():
        for rows in chunks:
            x = x_ref[rows, :]
            ms = jnp.mean(x * x, axis=-1, keepdims=True)
            h = (x * lax.rsqrt(ms + NORM_EPS) * gpre_ref[...]).astype(BF16)
            h_ref[rows, :] = h
            acc_ref[rows, :] = part(h)

    @pl.when((f > 0) & (f < last))
    def _():
        acc_ref[...] += part(h_ref[...])

    @pl.when(f == last)
    def _():
        for rows in chunks:
            mlp = acc_ref[rows, :] + part(h_ref[rows, :])
            ms = jnp.mean(mlp * mlp, axis=-1, keepdims=True)
            o_ref[rows, :] = x_ref[rows, :] + mlp * lax.rsqrt(ms + NORM_EPS) * gpost_ref[...]


def _mlp(x2, g_pre, w_up, w_down, layer, g_post, tm=512, tf=1024):
    m, d = x2.shape
    dff = w_up.shape[2]
    assert m % tm == 0 and dff % tf == 0 and dff // tf >= 2 and tm % ROW_CHUNK == 0
    return pl.pallas_call(
        _mlp_kernel,
        grid=(m // tm, dff // tf),
        in_specs=[pl.BlockSpec((tm, d), lambda i, f: (i, 0)),
                  pl.BlockSpec((1, d), lambda i, f: (0, 0)),
                  pl.BlockSpec((None, d, tf), lambda i, f: (layer, 0, f)),
                  pl.BlockSpec((None, tf, d), lambda i, f: (layer, f, 0)),
                  pl.BlockSpec((1, d), lambda i, f: (0, 0))],
        out_specs=pl.BlockSpec((tm, d), lambda i, f: (i, 0)),
        out_shape=jax.ShapeDtypeStruct((m, d), F32),
        scratch_shapes=[pltpu.VMEM((tm, d), BF16), pltpu.VMEM((tm, d), F32)],
        compiler_params=_cparams("parallel", "arbitrary"),
        name="mlp",
    )(x2, g_pre.reshape(1, d), w_up, w_down, g_post.reshape(1, d))


def _rope_tables(s):
    half = ATTN_HEAD // 2
    inv_freq = ROPE_THETA ** (-jnp.arange(half, dtype=F32) / half)
    ang = jnp.arange(s).astype(F32)[:, None] * inv_freq[None, :]
    cos, sin = jnp.cos(ang), jnp.sin(ang)
    return jnp.concatenate([cos, cos], axis=-1), jnp.concatenate([-sin, sin], axis=-1)


def kernel(x, norm_mix_pre, norm_mix_post, norm_mlp_pre, norm_mlp_post, w_in, w_in_vres, shift_mu, shift_mu_vres, decay_w0, decay_w2, iclr_a0, iclr_a2, vres_v0, vres_v2, gate_g2, k_k, k_a, r_k, lnx_gain, lnx_bias, w_out, w_up, w_down):
    bsz, s, d = x.shape
    depth = w_in.shape[0]
    c = decay_w0.shape[1]
    n_lora = DECAY_LORA + ICLR_LORA + GATE_LORA
    n_shift = 3 * c + n_lora
    ca = (w_in.shape[2] - n_shift) // 3
    cos2, sin2 = _rope_tables(s)
    w_in16, w_out16, w_up16, w_down16 = (w.astype(BF16) for w in (w_in, w_out, w_up, w_down))
    pad_v = VRES_PAD - VRES_LORA
    x2 = x.reshape(bsz * s, d)
    v_first = None
    for i in range(depth):
        if i == 0:
            z, vres = _norm_matmul(x2, norm_mix_pre[i], w_in16, i)[0], None
        else:
            w_v = jnp.pad(w_in_vres[i - 1], ((0, 0), (0, pad_v))).astype(BF16)
            z, z_v = _norm_matmul(x2, norm_mix_pre[i], w_in16, i, w_v)
            vres = (z_v.reshape(bsz, s, VRES_PAD), jnp.pad(shift_mu_vres[i - 1], (0, pad_v)), v_first,
                    vres_v0[i - 1], jnp.pad(vres_v2[i - 1], ((0, pad_v), (0, 0))))
        z3 = z.reshape(bsz, s, -1)

        prep = _rwkv_prep(z3, c, shift_mu[i], decay_w0[i], decay_w2[i], iclr_a0[i], iclr_a2[i], gate_g2[i],
                          k_k[i], k_a[i], vres)
        if i == 0:
            v_first = prep[7]
        y_r = _wkv(*prep[:7], r_k[i].reshape(-1), lnx_gain[i], lnx_bias[i])

        q_t, k_rot, v_t, kmean = _rope(z3, n_shift, ca, cos2, sin2)
        y_a = _moba(q_t, k_rot, v_t, kmean.reshape(bsz, -1, ca))

        x2 = _out_proj(y_r.reshape(bsz * s, c), y_a.reshape(bsz * s, ca), w_out16, i, x2, norm_mix_post[i])
        x2 = _mlp(x2, norm_mlp_pre[i], w_up16, w_down16, i, norm_mlp_post[i])
    return x2.reshape(bsz, s, d)
```

```python
import functools

import jax
import jax.numpy as jnp
from jax import lax
from jax.experimental import pallas as pl
from jax.experimental.pallas import tpu as pltpu

F32 = jnp.float32
BF16 = jnp.bfloat16

RWKV_HEAD = 64
DECAY_LORA = 64
ICLR_LORA = 64
VRES_LORA = 32
GATE_LORA = 128
ATTN_HEAD = 128
MOBA_BLOCK = 256
MOBA_TOPK = 3
ROPE_THETA = 10000.0
NORM_EPS = 1e-6
LNX_EPS = 64e-5
LOG2E = 1.4426950408889634

VRES_PAD = 128
WKV_CHUNK = 64
MOBA_LOOKAHEAD = 6
MOBA_UNROLL = 4
ROW_CHUNK = 256
VMEM_LIMIT = 56 * 1024 * 1024


def _cparams(*sem):
    return pltpu.CompilerParams(dimension_semantics=sem, vmem_limit_bytes=VMEM_LIMIT)


_NN = (((1,), (0,)), ((), ()))
_NT = (((1,), (1,)), ((), ()))
_TN = (((0,), (0,)), ((), ()))


def _split2(x):
    hi = x.astype(BF16)
    lo = (x - hi.astype(F32)).astype(BF16)
    return hi, lo


def _split3(x):
    hi = x.astype(BF16)
    r1 = x - hi.astype(F32)
    mid = r1.astype(BF16)
    lo = (r1 - mid.astype(F32)).astype(BF16)
    return hi, mid, lo


def _mm(a, b, dims=_NN, passes=1):
    d = lambda p, q: lax.dot_general(p, q, dims, preferred_element_type=F32)
    if passes == 1:
        return d(a.astype(BF16), b.astype(BF16))
    ah, al = _split2(a)
    bh, bl = _split2(b)
    return d(ah, bh) + (d(ah, bl) + d(al, bh))


def _sigmoid(x):
    return 1.0 / (1.0 + jnp.exp(-x))


def _softplus(x):
    return jnp.maximum(x, 0.0) + jnp.log(1.0 + jnp.exp(-jnp.abs(x)))


def _norm_matmul_kernel(has_extra, *refs):
    if has_extra:
        x_ref, g_ref, w_ref, we_ref, o_ref, oe_ref, h_ref = refs
    else:
        x_ref, g_ref, w_ref, o_ref, h_ref = refs

    first = pl.program_id(1) == 0

    @pl.when(first)
    def _():
        for r in range(0, x_ref.shape[0], ROW_CHUNK):
            rows = pl.ds(r, ROW_CHUNK)
            x = x_ref[rows, :]
            ms = jnp.mean(x * x, axis=-1, keepdims=True)
            h = (x * lax.rsqrt(ms + NORM_EPS) * g_ref[...]).astype(BF16)
            h_ref[rows, :] = h
            o_ref[rows, :] = jnp.dot(h, w_ref[...], preferred_element_type=F32)
            if has_extra:
                oe_ref[rows, :] = jnp.dot(h, we_ref[...], preferred_element_type=F32)

    @pl.when(jnp.logical_not(first))
    def _():
        o_ref[...] = jnp.dot(h_ref[...], w_ref[...], preferred_element_type=F32)


def _norm_matmul(x2, gain, w, w_extra=None, tm=1024, tn=1280):
    m, d = x2.shape
    n = w.shape[1]
    assert m % tm == 0 and n % tn == 0
    has_extra = w_extra is not None
    ins = [x2, gain.reshape(1, d), w]
    in_specs = [pl.BlockSpec((tm, d), lambda i, j: (i, 0)),
                pl.BlockSpec((1, d), lambda i, j: (0, 0)),
                pl.BlockSpec((d, tn), lambda i, j: (0, j))]
    out_specs = [pl.BlockSpec((tm, tn), lambda i, j: (i, j))]
    out_shape = [jax.ShapeDtypeStruct((m, n), F32)]
    if has_extra:
        ne = w_extra.shape[1]
        ins.append(w_extra)
        in_specs.append(pl.BlockSpec((d, ne), lambda i, j: (0, 0)))
        out_specs.append(pl.BlockSpec((tm, ne), lambda i, j: (i, 0)))
        out_shape.append(jax.ShapeDtypeStruct((m, ne), F32))
    return pl.pallas_call(
        functools.partial(_norm_matmul_kernel, has_extra),
        grid=(m // tm, n // tn),
        in_specs=in_specs,
        out_specs=out_specs,
        out_shape=out_shape,
        scratch_shapes=[pltpu.VMEM((tm, d), BF16)],
        compiler_params=_cparams("parallel", "arbitrary"),
        name="norm_matmul",
    )(*ins)


def _shifted(cur, halo, mu, first):
    rows = lax.broadcasted_iota(jnp.int32, cur.shape, 0)
    last = jnp.where(first, 0.0, halo[7:8, :])
    prev = jnp.where(rows == 0, last, pltpu.roll(cur, 1, axis=0))
    return cur + (prev - cur) * mu


def _rwkv_prep_kernel(has_vres, *refs):
    if has_vres:
        (zm_ref, zmh_ref, zl_ref, zlh_ref, mum_ref, mul_ref, w0_ref, w2_ref, a0_ref, a2_ref, g2_ref,
         kk_ref, ka_ref, zv_ref, zvh_ref, muv_ref, vf_ref, v0_ref, v2_ref,
         r_ref, lw_ref, k_ref, v_ref, an_ref, b_ref, g_ref) = refs
    else:
        (zm_ref, zmh_ref, zl_ref, zlh_ref, mum_ref, mul_ref, w0_ref, w2_ref, a0_ref, a2_ref, g2_ref,
         kk_ref, ka_ref,
         r_ref, lw_ref, k_ref, v_ref, an_ref, b_ref, g_ref, vf32_ref) = refs
    c = r_ref.shape[-1]
    first = pl.program_id(1) == 0
    zs = _shifted(zm_ref[0], zmh_ref[0], mum_ref[...], first)
    zl = _shifted(zl_ref[0], zlh_ref[0], mul_ref[...], first)
    r, k, v = zs[:, :c], zs[:, c:2 * c], zs[:, 2 * c:3 * c]
    o = 0
    wd = zl[:, o:o + DECAY_LORA]
    o += DECAY_LORA
    ad = zl[:, o:o + ICLR_LORA]
    o += ICLR_LORA
    gd = zl[:, o:o + GATE_LORA]
    o += GATE_LORA
    w_log = -_softplus(-(w0_ref[...] + _mm(jnp.tanh(wd), w2_ref[...], passes=3))) - 0.5
    lw_ref[0] = -jnp.exp(w_log)
    a = _sigmoid(a0_ref[...] + _mm(ad, a2_ref[...]))
    g_ref[0] = _mm(_sigmoid(gd), g2_ref[...]).astype(g_ref.dtype)
    if has_vres:
        vd = _shifted(zv_ref[0], zvh_ref[0], muv_ref[...], first)
        v = v + (vf_ref[0] - v) * _sigmoid(v0_ref[...] + _mm(vd, v2_ref[...]))
    else:
        vf32_ref[0] = v
    kk = k * kk_ref[...]
    pw = 2 * RWKV_HEAD
    even = lax.broadcasted_iota(jnp.int32, (kk.shape[0], pw), 1) < RWKV_HEAD
    unit = []
    for p in range(c // pw):
        x = kk[:, p * pw:(p + 1) * pw]
        sq = x * x
        ss = jnp.where(even, jnp.sum(jnp.where(even, sq, 0.0), axis=-1, keepdims=True),
                       jnp.sum(jnp.where(even, 0.0, sq), axis=-1, keepdims=True))
        unit.append(x / jnp.maximum(jnp.sqrt(ss), 1e-12))
    kk = jnp.concatenate(unit, axis=1)
    r_ref[0] = r.astype(r_ref.dtype)
    k_ref[0] = (k * (1.0 + (a - 1.0) * ka_ref[...])).astype(k_ref.dtype)
    v_ref[0] = v.astype(v_ref.dtype)
    an_ref[0] = (-kk).astype(an_ref.dtype)
    b_ref[0] = (kk * a).astype(b_ref.dtype)


def _rwkv_prep(z3, c, mu, w0, w2, a0, a2, g2, k_k, k_a, vres, tm=256):
    bsz, s, _ = z3.shape
    n_lora = w2.shape[0] + a2.shape[0] + g2.shape[0]
    assert s % tm == 0 and (3 * c) % n_lora == 0 and n_lora % 128 == 0
    has_vres = vres is not None
    row = lambda a: a.reshape(1, -1)
    hb = tm // 8
    lcb = 3 * c // n_lora
    halo = lambda b, i: (b, jnp.maximum(i * hb - 1, 0), 0)
    halo_l = lambda b, i: (b, jnp.maximum(i * hb - 1, 0), lcb)
    full = lambda a: pl.BlockSpec(a.shape, lambda b, i: (0,) * a.ndim)
    ins = [z3, z3, z3, z3, row(mu[:3 * c]), row(mu[3 * c:]), row(w0), w2, row(a0), a2, g2, row(k_k), row(k_a)]
    in_specs = [pl.BlockSpec((1, tm, 3 * c), lambda b, i: (b, i, 0)),
                pl.BlockSpec((1, 8, 3 * c), halo),
                pl.BlockSpec((1, tm, n_lora), lambda b, i: (b, i, lcb)),
                pl.BlockSpec((1, 8, n_lora), halo_l)] + [full(a) for a in ins[4:]]
    if has_vres:
        zv3, mu_v, v_first, v0, v2 = vres
        nv = zv3.shape[-1]
        extra = [zv3, zv3, row(mu_v), v_first, row(v0), v2]
        ins += extra
        in_specs += [pl.BlockSpec((1, tm, nv), lambda b, i: (b, i, 0)), pl.BlockSpec((1, 8, nv), halo),
                     full(extra[2]), pl.BlockSpec((1, tm, c), lambda b, i: (b, i, 0)), full(extra[4]),
                     full(extra[5])]
    out_spec = pl.BlockSpec((1, tm, c), lambda b, i: (b, i, 0))
    sd = lambda dt: jax.ShapeDtypeStruct((bsz, s, c), dt)
    out_dtypes = [BF16, F32, BF16, BF16, BF16, BF16, BF16] + ([] if has_vres else [F32])
    return pl.pallas_call(
        functools.partial(_rwkv_prep_kernel, has_vres),
        grid=(bsz, s // tm),
        in_specs=in_specs,
        out_specs=[out_spec] * len(out_dtypes),
        out_shape=[sd(dt) for dt in out_dtypes],
        compiler_params=_cparams("parallel", "arbitrary"),
        name="rwkv_prep",
    )(*ins)


def _dots(a_list, b_list, dims=_NN):
    return [lax.dot_general(a.astype(BF16), b.astype(BF16), dims, preferred_element_type=F32)
            for a, b in zip(a_list, b_list)]


def _pair_diag(x, even):
    return jnp.concatenate([jnp.where(even, x, 0.0), jnp.where(even, 0.0, x)], axis=0)


def _unit_lower_inverse(a_list, row_w, col_w, even):
    n = row_w.shape[0]
    lower = row_w > col_w
    base = lower & ((row_w >> 1) == (col_w >> 1))
    t = [jnp.where(row_w == col_w, 1.0, jnp.where(base, a, 0.0)) for a in a_list]
    sh = 1
    while (2 << sh) <= n:
        sub = lower & ((row_w >> (sh + 1)) == (col_w >> (sh + 1))) & ((row_w >> sh) != (col_w >> sh))
        off = [_pair_diag(jnp.where(sub, a, 0.0), even) for a in a_list]
        upd = _dots(_dots(t, off), [_pair_diag(x, even) for x in t])
        t = [x + u for x, u in zip(t, upd)]
        sh += 1
    return t


def _mm_exact_rhs_left(l_bf16, a):
    d = lambda p: lax.dot_general(l_bf16, p, _NN, preferred_element_type=F32)
    hi, mid, lo = _split3(a)
    return d(hi) + (d(mid) + d(lo))


def _wkv_kernel(n_cast, *refs):
    r_ref, lw_ref, k_ref, v_ref, an_ref, b_ref, g_ref, rk_ref, lg_ref, lb_ref = refs[:10]
    cast_in, y_ref = refs[10:10 + n_cast], refs[10 + n_cast]
    cast_out, state_ref = refs[11 + n_cast:11 + 2 * n_cast], refs[11 + 2 * n_cast]
    for src, dst in zip(cast_in, cast_out):
        dst[...] = src[...].astype(dst.dtype)

    nb, ln = r_ref.shape[0], r_ref.shape[1]
    n = RWKV_HEAD
    pw = 2 * n
    npair = r_ref.shape[2] // pw
    ent = [(bi, slice(p * pw, (p + 1) * pw)) for bi in range(nb) for p in range(npair)]

    @pl.when(pl.program_id(1) == 0)
    def _():
        state_ref[...] = jnp.zeros_like(state_ref)

    rows = lax.broadcasted_iota(jnp.int32, (ln, ln), 0)
    cols = lax.broadcasted_iota(jnp.int32, (ln, ln), 1)
    tril = jnp.where(rows >= cols, 1.0, 0.0).astype(BF16)
    r, k, v, w_end, r_t, a_t, b_t, k_t, b_h, k_h = ([] for _ in range(10))
    for bi in range(nb):
        lw = lw_ref[bi]
        cw = _mm_exact_rhs_left(tril, lw)
        cw_end = cw[ln - 1:ln, :]
        e_neg = jnp.exp(-cw)
        e_end = jnp.exp(cw_end - cw)
        bb = b_ref[bi].astype(F32)
        r.append(r_ref[bi].astype(F32))
        k.append(k_ref[bi].astype(F32))
        v.append(v_ref[bi].astype(F32))
        w_end.append(jnp.exp(cw_end))
        r_t.append(r[bi] * jnp.exp(cw))
        a_t.append(an_ref[bi].astype(F32) * jnp.exp(cw - lw))
        b_t.append(bb * e_neg)
        k_t.append(k[bi] * e_neg)
        b_h.append(bb * e_end)
        k_h.append(k[bi] * e_end)

    lane = lax.broadcasted_iota(jnp.int32, (ln, pw), 1)
    row_w = lax.broadcasted_iota(jnp.int32, (ln, pw), 0)
    even = lane < n
    even2 = lax.broadcasted_iota(jnp.int32, (2 * ln, pw), 1) < n
    col_w = lane & (n - 1)
    strict_w = row_w > col_w
    incl_w = row_w >= col_w
    zeros_w = jnp.zeros((ln, pw), F32)
    diag = lambda x: _pair_diag(x, even)
    swap = lambda x: jnp.concatenate([x[x.shape[0] // 2:], x[:x.shape[0] // 2]], axis=0)

    ar_p = [jnp.concatenate([a_t[bi][:, ps], r_t[bi][:, ps]], axis=0) for bi, ps in ent]
    bk_p = [jnp.concatenate([b_t[bi][:, ps], k_t[bi][:, ps]], axis=0).astype(BF16) for bi, ps in ent]
    kb_p = [jnp.concatenate([k_t[bi][:, ps], b_t[bi][:, ps]], axis=0).astype(BF16) for bi, ps in ent]
    am_e = _dots([jnp.where(even2, x, 0.0) for x in ar_p], bk_p, _NT)
    am_o = _dots([jnp.where(even2, 0.0, x) for x in ar_p], kb_p, _NT)
    a_ab = [jnp.where(strict_w, jnp.where(even, e[:ln], o[:ln]), 0.0) for e, o in zip(am_e, am_o)]
    a_ak = [jnp.where(strict_w, jnp.where(even, o[:ln], e[:ln]), 0.0) for e, o in zip(am_e, am_o)]
    a_rb = [jnp.where(incl_w, jnp.where(even, e[ln:], o[ln:]), 0.0) for e, o in zip(am_e, am_o)]
    a_rk = [jnp.where(incl_w, jnp.where(even, o[ln:], e[ln:]), 0.0) for e, o in zip(am_e, am_o)]
    v_p = [v[bi][:, ps] for bi, ps in ent]
    v_d = [diag(x) for x in v_p]
    akv = _dots(a_ak, [swap(x) for x in v_d])
    t = _unit_lower_inverse(a_ab, row_w, col_w, even)
    rhs = [jnp.concatenate([diag(a_t[bi][:, ps]), diag(x)], axis=1) for (bi, ps), x in zip(ent, akv)]
    pq = _dots(t, rhs)
    ry = _dots([jnp.concatenate([x, y], axis=1) for x, y in zip(a_rb, a_rk)],
               [jnp.concatenate([jnp.concatenate([diag(x[:, :pw]), diag(x[:, pw:])], axis=1),
                                 jnp.concatenate([jnp.zeros((2 * ln, pw), F32), swap(u)], axis=1)], axis=0)
                for x, u in zip(pq, v_d)])
    pqv_p = [jnp.concatenate([x, jnp.concatenate([zeros_w, u], axis=1)], axis=0)
             for x, u in zip(pq, v_p)]
    bkh_p = [jnp.concatenate([b_h[bi][:, ps], k_h[bi][:, ps]], axis=0) for bi, ps in ent]
    mn_p = _dots(bkh_p, pqv_p, _TN)
    sq_r = lax.broadcasted_iota(jnp.int32, (pw, pw), 0)
    sq_c = lax.broadcasted_iota(jnp.int32, (pw, pw), 1)
    same_head = (sq_r < n) == (sq_c < n)
    lhs_p = [jnp.concatenate([r_t[bi][:, ps] + x[:, :pw],
                              jnp.where(sq_r == sq_c, jnp.broadcast_to(w_end[bi][:, ps], (pw, pw)),
                                        jnp.where(same_head, m[:, :pw], 0.0))], axis=0)
             for (bi, ps), x, m in zip(ent, ry, mn_p)]
    st = [state_ref[e] for e in range(len(ent))]
    upd = _dots(lhs_p, st)
    for e in range(len(ent)):
        state_ref[e] = upd[e][ln:, :] + jnp.where(same_head, mn_p[e][:, pw:], 0.0)
    y0_p = [x[:, pw:] for x in ry]

    hsum = lambda x: jnp.where(even, jnp.sum(jnp.where(even, x, 0.0), axis=-1, keepdims=True),
                               jnp.sum(jnp.where(even, 0.0, x), axis=-1, keepdims=True))
    inv_n = 1.0 / n
    yn = []
    for u, y0 in zip(upd, y0_p):
        y = u[:ln, :] + y0
        d = y - hsum(y) * inv_n
        yn.append(d * lax.rsqrt(hsum(d * d) * inv_n + LNX_EPS))
    for bi in range(nb):
        rk = r[bi] * k[bi] * rk_ref[...]
        mine = range(bi * npair, (bi + 1) * npair)
        bonus = jnp.concatenate([hsum(rk[:, ent[e][1]]) * v_p[e] for e in mine], axis=1)
        out = jnp.concatenate([yn[e] for e in mine], axis=1) * lg_ref[...] + lb_ref[...] + bonus
        y_ref[bi] = (out * g_ref[bi].astype(F32)).astype(y_ref.dtype)


def _wkv(r, lw, k, v, an, b, g, r_k, lnx_g, lnx_b, casts=(), heads_per_step=16):
    bsz, s, c = r.shape
    wb = heads_per_step * RWKV_HEAD
    nc = s // WKV_CHUNK
    assert s % WKV_CHUNK == 0 and c % wb == 0 and heads_per_step % 2 == 0 and WKV_CHUNK == RWKV_HEAD
    assert not casts or c == wb
    spec = pl.BlockSpec((bsz, WKV_CHUNK, wb), lambda hi, ci: (0, ci, hi))
    pspec = pl.BlockSpec((1, wb), lambda hi, ci: (0, hi))
    row = lambda a: a.reshape(1, -1)
    cast_ins, cast_in_specs, cast_out_specs, cast_out_shape = [], [], [], []
    for w, layer in casts:
        nl, rows, cols = w.shape
        assert rows % (16 * nc) == 0
        slab = rows // nc
        cast_ins.append(w.reshape(nl, nc, slab, cols))
        cast_in_specs.append(pl.BlockSpec((None, None, slab, cols), lambda hi, ci, layer=layer: (layer, ci, 0, 0)))
        cast_out_specs.append(pl.BlockSpec((None, slab, cols), lambda hi, ci: (ci, 0, 0)))
        cast_out_shape.append(jax.ShapeDtypeStruct((nc, slab, cols), BF16))
    out = pl.pallas_call(
        functools.partial(_wkv_kernel, len(casts)),
        grid=(c // wb, nc),
        in_specs=[spec] * 7 + [pspec] * 3 + cast_in_specs,
        out_specs=[spec] + cast_out_specs,
        out_shape=[jax.ShapeDtypeStruct((bsz, s, c), BF16)] + cast_out_shape,
        scratch_shapes=[pltpu.VMEM((bsz * heads_per_step // 2, 2 * RWKV_HEAD, 2 * RWKV_HEAD), F32)],
        compiler_params=_cparams("parallel", "arbitrary"),
        name="wkv",
    )(r, lw, k, v, an, b, g, row(r_k), row(lnx_g), row(lnx_b), *cast_ins)
    return out[0], [o.reshape(w.shape[1], w.shape[2]) for o, (w, _) in zip(out[1:], casts)]


def _rope_kernel(q_ref, k_ref, v_ref, cos_ref, sin_ref, qt_ref, ko_ref, vt_ref, km_ref):
    cos = cos_ref[...]
    sin = sin_ref[...]
    nh = q_ref.shape[-1] // ATTN_HEAD
    inv_rows = 1.0 / q_ref.shape[1]
    for h in range(nh):
        sl = slice(h * ATTN_HEAD, (h + 1) * ATTN_HEAD)
        q = q_ref[0, :, sl]
        k = k_ref[0, :, sl]
        qr = q * cos + pltpu.roll(q, ATTN_HEAD // 2, axis=1) * sin
        kr = k * cos + pltpu.roll(k, ATTN_HEAD // 2, axis=1) * sin
        qt_ref[0, h, 0] = qr.T
        ko_ref[0, h, 0] = kr.astype(BF16)
        vt_ref[0, h, 0] = v_ref[0, :, sl].T.astype(BF16)
        km_ref[0, 0, :, sl] = jnp.sum(kr, axis=0, keepdims=True) * inv_rows


def _rope(z3, col0, width, cos2, sin2):
    bsz, s, _ = z3.shape
    assert s % MOBA_BLOCK == 0 and col0 % ATTN_HEAD == 0
    nb = s // MOBA_BLOCK
    nh = width // ATTN_HEAD
    blk = lambda j: pl.BlockSpec((pl.Element(1), pl.Element(MOBA_BLOCK), pl.Element(width)),
                                 lambda b, i: (b, i * MOBA_BLOCK, col0 + j * width))
    tab = pl.BlockSpec((MOBA_BLOCK, ATTN_HEAD), lambda b, i: (i, 0))
    t_spec = pl.BlockSpec((1, nh, 1, ATTN_HEAD, MOBA_BLOCK), lambda b, i: (b, 0, i, 0, 0))
    n_spec = pl.BlockSpec((1, nh, 1, MOBA_BLOCK, ATTN_HEAD), lambda b, i: (b, 0, i, 0, 0))
    return pl.pallas_call(
        _rope_kernel,
        grid=(bsz, nb),
        in_specs=[blk(0), blk(1), blk(2), tab, tab],
        out_specs=[t_spec, n_spec, t_spec, pl.BlockSpec((1, 1, 1, width), lambda b, i: (b, i, 0, 0))],
        out_shape=[jax.ShapeDtypeStruct((bsz, nh, nb, ATTN_HEAD, MOBA_BLOCK), F32),
                   jax.ShapeDtypeStruct((bsz, nh, nb, MOBA_BLOCK, ATTN_HEAD), BF16),
                   jax.ShapeDtypeStruct((bsz, nh, nb, ATTN_HEAD, MOBA_BLOCK), BF16),
                   jax.ShapeDtypeStruct((bsz, nb, 1, width), F32)],
        compiler_params=_cparams("parallel", "parallel"),
        name="rope",
    )(z3, z3, z3, cos2, sin2)


def _moba_kernel(qt_ref, k_ref, vt_ref, km_ref, o_ref, bias_ref):
    blk = MOBA_BLOCK
    dh = ATTN_HEAD
    hs = range(qt_ref.shape[1])
    qb = pl.program_id(2)
    nb = km_ref.shape[1]
    scale = ATTN_HEAD ** -0.5
    neg = -jnp.inf
    qt = [qt_ref[0, h, 0] for h in hs]
    blk_id = lax.broadcasted_iota(jnp.int32, (nb, blk), 0)
    past = blk_id < qb
    gate = [jnp.where(past, _mm(km_ref[0, :, h * dh:(h + 1) * dh], qt[h], passes=3), neg) for h in hs]
    for h in hs:
        rank = jnp.zeros((nb, blk), jnp.int32)
        for m in range(nb):
            gm = gate[h][m:m + 1, :]
            rank += ((gm > gate[h]) | ((gm == gate[h]) & (m < blk_id))).astype(jnp.int32)
        bias_ref[h] = jnp.where(past & (rank < MOBA_TOPK), 0.0, neg)

    qs = [(q * (scale * LOG2E)).astype(BF16) for q in qt]
    ki = lax.broadcasted_iota(jnp.int32, (blk, blk), 0)
    qi = lax.broadcasted_iota(jnp.int32, (blk, blk), 1)
    causal = ki <= qi

    def pipelined(work, stage):
        scores = lambda kb, h: jnp.dot(k_ref[0, h, kb], qs[h], preferred_element_type=F32)
        ahead = [scores(*w) for w in work[:MOBA_LOOKAHEAD]]
        for i, (kb, h) in enumerate(work):
            if i + MOBA_LOOKAHEAD < len(work):
                ahead.append(scores(*work[i + MOBA_LOOKAHEAD]))
            stage(kb, h, ahead[i])

    m_run, l_run, acc = [None] * len(hs), [None] * len(hs), [None] * len(hs)

    def own_block(kb, h, s):
        s = jnp.where(causal, s, neg)
        m_run[h] = jnp.max(s, axis=0, keepdims=True)
        p = jnp.exp2(s - m_run[h])
        l_run[h] = jnp.sum(p, axis=0, keepdims=True)
        acc[h] = jnp.dot(vt_ref[0, h, kb], p.astype(BF16), preferred_element_type=F32)

    pipelined([(qb, h) for h in hs], own_block)

    def past_blocks(kbs, carry):
        m_c, l_c, acc_c = (list(c) for c in carry)

        def stage(kb, h, s):
            b = bias_ref[h, pl.ds(kb, 1), :]
            m_new = jnp.where(b == 0.0, jnp.maximum(m_c[h], jnp.max(s, axis=0, keepdims=True)), m_c[h])
            alpha = jnp.exp2(m_c[h] - m_new)
            p = jnp.exp2(s - (m_new - b))
            pv = jnp.dot(vt_ref[0, h, kb], p.astype(BF16), preferred_element_type=F32)
            m_c[h] = m_new
            l_c[h] = alpha * l_c[h] + jnp.sum(p, axis=0, keepdims=True)
            acc_c[h] = alpha * acc_c[h] + pv

        pipelined([(kb, h) for kb in kbs for h in hs], stage)
        return tuple(m_c), tuple(l_c), tuple(acc_c)

    u = MOBA_UNROLL
    carry = lax.fori_loop(0, qb // u, lambda j, c: past_blocks([j * u + i for i in range(u)], c),
                          (tuple(m_run), tuple(l_run), tuple(acc)))
    _, l_fin, acc = lax.fori_loop((qb // u) * u, qb, lambda kb, c: past_blocks([kb], c), carry)
    for h in hs:
        o_ref[0, :, h * dh:(h + 1) * dh] = (acc[h] / l_fin[h]).T.astype(o_ref.dtype)


def _moba(qt, k, vt, kmean, heads_per_step=8):
    bsz, nh, nb, dh, blk = qt.shape
    hp = heads_per_step
    assert nh % hp == 0
    return pl.pallas_call(
        _moba_kernel,
        grid=(bsz, nh // hp, nb),
        in_specs=[pl.BlockSpec((1, hp, 1, dh, blk), lambda b, h, i: (b, h, i, 0, 0)),
                  pl.BlockSpec((1, hp, nb, blk, dh), lambda b, h, i: (b, h, 0, 0, 0)),
                  pl.BlockSpec((1, hp, nb, dh, blk), lambda b, h, i: (b, h, 0, 0, 0)),
                  pl.BlockSpec((1, nb, hp * dh), lambda b, h, i: (b, 0, h))],
        out_specs=pl.BlockSpec((1, blk, hp * dh), lambda b, h, i: (b, i, h)),
        out_shape=jax.ShapeDtypeStruct((bsz, nb * blk, nh * dh), BF16),
        scratch_shapes=[pltpu.VMEM((hp, nb, blk), F32)],
        compiler_params=_cparams("parallel", "parallel", "arbitrary"),
        name="moba",
    )(qt, k, vt, kmean)


def _out_proj_kernel(yr_ref, ya_ref, wr_ref, wa_ref, x_ref, g_ref, o_ref):
    for r in range(0, x_ref.shape[0], ROW_CHUNK):
        rows = pl.ds(r, ROW_CHUNK)
        y = jnp.dot(yr_ref[rows, :], wr_ref[...], preferred_element_type=F32)
        y += jnp.dot(ya_ref[rows, :], wa_ref[...], preferred_element_type=F32)
        ms = jnp.mean(y * y, axis=-1, keepdims=True)
        o_ref[rows, :] = x_ref[rows, :] + y * lax.rsqrt(ms + NORM_EPS) * g_ref[...]


def _out_proj(y_r, y_a, w, x2, gain, tm=512):
    m, d = x2.shape
    cw = y_r.shape[1]
    assert y_a.shape[1] == cw and w.shape[0] == 2 * cw and m % tm == 0
    return pl.pallas_call(
        _out_proj_kernel,
        grid=(m // tm,),
        in_specs=[pl.BlockSpec((tm, cw), lambda i: (i, 0)), pl.BlockSpec((tm, cw), lambda i: (i, 0)),
                  pl.BlockSpec((cw, d), lambda i: (0, 0)), pl.BlockSpec((cw, d), lambda i: (1, 0)),
                  pl.BlockSpec((tm, d), lambda i: (i, 0)), pl.BlockSpec((1, d), lambda i: (0, 0))],
        out_specs=pl.BlockSpec((tm, d), lambda i: (i, 0)),
        out_shape=jax.ShapeDtypeStruct((m, d), F32),
        compiler_params=_cparams("parallel"),
        name="out_proj",
    )(y_r, y_a, w, w, x2, gain.reshape(1, d))


def _mlp_kernel(x_ref, gpre_ref, wu_ref, wd_ref, gpost_ref, o_ref, h_ref, acc_ref):
    f = pl.program_id(1)
    last = pl.num_programs(1) - 1
    chunks = [pl.ds(r, ROW_CHUNK) for r in range(0, x_ref.shape[0], ROW_CHUNK)]

    def part(h):
        u = jnp.maximum(jnp.dot(h, wu_ref[...], preferred_element_type=F32), 0.0)
        return jnp.dot((u * u).astype(BF16), wd_ref[...], preferred_element_type=F32)

    @pl.when(f == 0)
    def _():
        for rows in chunks:
            x = x_ref[rows, :]
            ms = jnp.mean(x * x, axis=-1, keepdims=True)
            h = (x * lax.rsqrt(ms + NORM_EPS) * gpre_ref[...]).astype(BF16)
            h_ref[rows, :] = h
            acc_ref[rows, :] = part(h)

    @pl.when((f > 0) & (f < last))
    def _():
        acc_ref[...] += part(h_ref[...])

    @pl.when(f == last)
    def _():
        for rows in chunks:
            mlp = acc_ref[rows, :] + part(h_ref[rows, :])
            ms = jnp.mean(mlp * mlp, axis=-1, keepdims=True)
            o_ref[rows, :] = x_ref[rows, :] + mlp * lax.rsqrt(ms + NORM_EPS) * gpost_ref[...]


def _mlp(x2, g_pre, w_up, w_down, g_post, tm=512, tf=1024):
    m, d = x2.shape
    dff = w_up.shape[1]
    assert m % tm == 0 and dff % tf == 0 and dff // tf >= 2 and tm % ROW_CHUNK == 0
    return pl.pallas_call(
        _mlp_kernel,
        grid=(m // tm, dff // tf),
        in_specs=[pl.BlockSpec((tm, d), lambda i, f: (i, 0)),
                  pl.BlockSpec((1, d), lambda i, f: (0, 0)),
                  pl.BlockSpec((d, tf), lambda i, f: (0, f)),
                  pl.BlockSpec((tf, d), lambda i, f: (f, 0)),
                  pl.BlockSpec((1, d), lambda i, f: (0, 0))],
        out_specs=pl.BlockSpec((tm, d), lambda i, f: (i, 0)),
        out_shape=jax.ShapeDtypeStruct((m, d), F32),
        scratch_shapes=[pltpu.VMEM((tm, d), BF16), pltpu.VMEM((tm, d), F32)],
        compiler_params=_cparams("parallel", "arbitrary"),
        name="mlp",
    )(x2, g_pre.reshape(1, d), w_up, w_down, g_post.reshape(1, d))


def _rope_tables(s):
    half = ATTN_HEAD // 2
    inv_freq = ROPE_THETA ** (-jnp.arange(half, dtype=F32) / half)
    ang = jnp.arange(s).astype(F32)[:, None] * inv_freq[None, :]
    cos, sin = jnp.cos(ang), jnp.sin(ang)
    return jnp.concatenate([cos, cos], axis=-1), jnp.concatenate([-sin, sin], axis=-1)


def kernel(x, norm_mix_pre, norm_mix_post, norm_mlp_pre, norm_mlp_post, w_in, w_in_vres, shift_mu, shift_mu_vres, decay_w0, decay_w2, iclr_a0, iclr_a2, vres_v0, vres_v2, gate_g2, k_k, k_a, r_k, lnx_gain, lnx_bias, w_out, w_up, w_down):
    bsz, s, d = x.shape
    depth = w_in.shape[0]
    c = decay_w0.shape[1]
    n_lora = DECAY_LORA + ICLR_LORA + GATE_LORA
    n_shift = 3 * c + n_lora
    ca = (w_in.shape[2] - n_shift) // 3
    cos2, sin2 = _rope_tables(s)
    w_in16 = w_in[0].astype(BF16)
    pad_v = VRES_PAD - VRES_LORA
    x2 = x.reshape(bsz * s, d)
    v_first = None
    for i in range(depth):
        if i == 0:
            z, vres = _norm_matmul(x2, norm_mix_pre[i], w_in16)[0], None
        else:
            w_v = jnp.pad(w_in_vres[i - 1], ((0, 0), (0, pad_v))).astype(BF16)
            z, z_v = _norm_matmul(x2, norm_mix_pre[i], w_in16, w_v)
            vres = (z_v.reshape(bsz, s, VRES_PAD), jnp.pad(shift_mu_vres[i - 1], (0, pad_v)), v_first,
                    vres_v0[i - 1], jnp.pad(vres_v2[i - 1], ((0, pad_v), (0, 0))))
        z3 = z.reshape(bsz, s, -1)

        prep = _rwkv_prep(z3, c, shift_mu[i], decay_w0[i], decay_w2[i], iclr_a0[i], iclr_a2[i], gate_g2[i],
                          k_k[i], k_a[i], vres)
        if i == 0:
            v_first = prep[7]
        casts = [(w_out, i), (w_up, i), (w_down, i)] + ([(w_in, i + 1)] if i + 1 < depth else [])
        y_r, w16 = _wkv(*prep[:7], r_k[i].reshape(-1), lnx_gain[i], lnx_bias[i], casts)
        w_out16, w_up16, w_down16 = w16[:3]

        q_t, k_rot, v_t, kmean = _rope(z3, n_shift, ca, cos2, sin2)
        y_a = _moba(q_t, k_rot, v_t, kmean.reshape(bsz, -1, ca))

        x2 = _out_proj(y_r.reshape(bsz * s, c), y_a.reshape(bsz * s, ca), w_out16, x2, norm_mix_post[i])
        x2 = _mlp(x2, norm_mlp_pre[i], w_up16, w_down16, norm_mlp_post[i])
        if i + 1 < depth:
            w_in16 = w16[3]
    return x2.reshape(bsz, s, d)
```

```python
import functools

import jax
import jax.numpy as jnp
from jax import lax
from jax.experimental import pallas as pl
from jax.experimental.pallas import tpu as pltpu

F32 = jnp.float32
BF16 = jnp.bfloat16

RWKV_HEAD = 64
DECAY_LORA = 64
ICLR_LORA = 64
VRES_LORA = 32
GATE_LORA = 128
ATTN_HEAD = 128
MOBA_BLOCK = 256
MOBA_TOPK = 3
ROPE_THETA = 10000.0
NORM_EPS = 1e-6
LNX_EPS = 64e-5
LOG2E = 1.4426950408889634

VRES_PAD = 128
WKV_CHUNK = 64
MOBA_LOOKAHEAD = 6
MOBA_UNROLL = 4
HALO_ROWS = 16
ROW_CHUNK = 256
VMEM_LIMIT = 56 * 1024 * 1024


def _cparams(*sem):
    return pltpu.CompilerParams(dimension_semantics=sem, vmem_limit_bytes=VMEM_LIMIT)


_NN = (((1,), (0,)), ((), ()))
_NT = (((1,), (1,)), ((), ()))
_TN = (((0,), (0,)), ((), ()))


def _split2(x):
    hi = x.astype(BF16)
    lo = (x - hi.astype(F32)).astype(BF16)
    return hi, lo


def _split3(x):
    hi = x.astype(BF16)
    r1 = x - hi.astype(F32)
    mid = r1.astype(BF16)
    lo = (r1 - mid.astype(F32)).astype(BF16)
    return hi, mid, lo


def _mm(a, b, dims=_NN, passes=1):
    d = lambda p, q: lax.dot_general(p, q, dims, preferred_element_type=F32)
    if passes == 1:
        return d(a.astype(BF16), b.astype(BF16))
    ah, al = _split2(a)
    bh, bl = _split2(b)
    return d(ah, bh) + (d(ah, bl) + d(al, bh))


def _sigmoid(x):
    return 1.0 / (1.0 + jnp.exp(-x))


def _softplus(x):
    return jnp.maximum(x, 0.0) + jnp.log(1.0 + jnp.exp(-jnp.abs(x)))


def _norm_matmul_kernel(has_extra, *refs):
    if has_extra:
        x_ref, g_ref, w_ref, we_ref, o_ref, oe_ref, h_ref = refs
    else:
        x_ref, g_ref, w_ref, o_ref, h_ref = refs

    first = pl.program_id(1) == 0

    @pl.when(first)
    def _():
        for r in range(0, x_ref.shape[0], ROW_CHUNK):
            rows = pl.ds(r, ROW_CHUNK)
            x = x_ref[rows, :]
            ms = jnp.mean(x * x, axis=-1, keepdims=True)
            h = (x * lax.rsqrt(ms + NORM_EPS) * g_ref[...]).astype(BF16)
            h_ref[rows, :] = h
            o_ref[rows, :] = jnp.dot(h, w_ref[...], preferred_element_type=F32).astype(o_ref.dtype)
            if has_extra:
                oe_ref[rows, :] = jnp.dot(h, we_ref[...], preferred_element_type=F32).astype(oe_ref.dtype)

    @pl.when(jnp.logical_not(first))
    def _():
        o_ref[...] = jnp.dot(h_ref[...], w_ref[...], preferred_element_type=F32).astype(o_ref.dtype)


def _norm_matmul(x2, gain, w, w_extra=None, tm=1024, tn=1280):
    m, d = x2.shape
    n = w.shape[1]
    assert m % tm == 0 and n % tn == 0
    has_extra = w_extra is not None
    ins = [x2, gain.reshape(1, d), w]
    in_specs = [pl.BlockSpec((tm, d), lambda i, j: (i, 0)),
                pl.BlockSpec((1, d), lambda i, j: (0, 0)),
                pl.BlockSpec((d, tn), lambda i, j: (0, j))]
    out_specs = [pl.BlockSpec((tm, tn), lambda i, j: (i, j))]
    out_shape = [jax.ShapeDtypeStruct((m, n), BF16)]
    if has_extra:
        ne = w_extra.shape[1]
        ins.append(w_extra)
        in_specs.append(pl.BlockSpec((d, ne), lambda i, j: (0, 0)))
        out_specs.append(pl.BlockSpec((tm, ne), lambda i, j: (i, 0)))
        out_shape.append(jax.ShapeDtypeStruct((m, ne), BF16))
    return pl.pallas_call(
        functools.partial(_norm_matmul_kernel, has_extra),
        grid=(m // tm, n // tn),
        in_specs=in_specs,
        out_specs=out_specs,
        out_shape=out_shape,
        scratch_shapes=[pltpu.VMEM((tm, d), BF16)],
        compiler_params=_cparams("parallel", "arbitrary"),
        name="norm_matmul",
    )(*ins)


def _shifted(cur, halo, mu, first):
    cur = cur.astype(F32)
    rows = lax.broadcasted_iota(jnp.int32, cur.shape, 0)
    last = jnp.where(first, 0.0, halo[HALO_ROWS - 1:HALO_ROWS, :].astype(F32))
    prev = jnp.where(rows == 0, last, pltpu.roll(cur, 1, axis=0))
    return cur + (prev - cur) * mu


def _rwkv_prep_kernel(has_vres, *refs):
    if has_vres:
        (zm_ref, zmh_ref, zl_ref, zlh_ref, mum_ref, mul_ref, w0_ref, w2_ref, a0_ref, a2_ref, g2_ref,
         kk_ref, ka_ref, zv_ref, zvh_ref, muv_ref, vf_ref, v0_ref, v2_ref,
         r_ref, lw_ref, k_ref, v_ref, an_ref, b_ref, g_ref) = refs
    else:
        (zm_ref, zmh_ref, zl_ref, zlh_ref, mum_ref, mul_ref, w0_ref, w2_ref, a0_ref, a2_ref, g2_ref,
         kk_ref, ka_ref,
         r_ref, lw_ref, k_ref, v_ref, an_ref, b_ref, g_ref, vf32_ref) = refs
    c = r_ref.shape[-1]
    first = pl.program_id(1) == 0
    zs = _shifted(zm_ref[0], zmh_ref[0], mum_ref[...], first)
    zl = _shifted(zl_ref[0], zlh_ref[0], mul_ref[...], first)
    r, k, v = zs[:, :c], zs[:, c:2 * c], zs[:, 2 * c:3 * c]
    o = 0
    wd = zl[:, o:o + DECAY_LORA]
    o += DECAY_LORA
    ad = zl[:, o:o + ICLR_LORA]
    o += ICLR_LORA
    gd = zl[:, o:o + GATE_LORA]
    o += GATE_LORA
    w_log = -_softplus(-(w0_ref[...] + _mm(jnp.tanh(wd), w2_ref[...], passes=3))) - 0.5
    lw_ref[0] = -jnp.exp(w_log)
    a = _sigmoid(a0_ref[...] + _mm(ad, a2_ref[...]))
    g_ref[0] = _mm(_sigmoid(gd), g2_ref[...]).astype(g_ref.dtype)
    if has_vres:
        vd = _shifted(zv_ref[0], zvh_ref[0], muv_ref[...], first)
        v = v + (vf_ref[0] - v) * _sigmoid(v0_ref[...] + _mm(vd, v2_ref[...]))
    else:
        vf32_ref[0] = v
    kk = k * kk_ref[...]
    pw = 2 * RWKV_HEAD
    even = lax.broadcasted_iota(jnp.int32, (kk.shape[0], pw), 1) < RWKV_HEAD
    unit = []
    for p in range(c // pw):
        x = kk[:, p * pw:(p + 1) * pw]
        sq = x * x
        ss = jnp.where(even, jnp.sum(jnp.where(even, sq, 0.0), axis=-1, keepdims=True),
                       jnp.sum(jnp.where(even, 0.0, sq), axis=-1, keepdims=True))
        unit.append(x / jnp.maximum(jnp.sqrt(ss), 1e-12))
    kk = jnp.concatenate(unit, axis=1)
    r_ref[0] = r.astype(r_ref.dtype)
    k_ref[0] = (k * (1.0 + (a - 1.0) * ka_ref[...])).astype(k_ref.dtype)
    v_ref[0] = v.astype(v_ref.dtype)
    an_ref[0] = (-kk).astype(an_ref.dtype)
    b_ref[0] = (kk * a).astype(b_ref.dtype)


def _rwkv_prep(z3, c, mu, w0, w2, a0, a2, g2, k_k, k_a, vres, tm=256):
    bsz, s, _ = z3.shape
    n_lora = w2.shape[0] + a2.shape[0] + g2.shape[0]
    assert s % tm == 0 and (3 * c) % n_lora == 0 and n_lora % 128 == 0
    has_vres = vres is not None
    row = lambda a: a.reshape(1, -1)
    hb = tm // HALO_ROWS
    lcb = 3 * c // n_lora
    halo = lambda b, i: (b, jnp.maximum(i * hb - 1, 0), 0)
    halo_l = lambda b, i: (b, jnp.maximum(i * hb - 1, 0), lcb)
    full = lambda a: pl.BlockSpec(a.shape, lambda b, i: (0,) * a.ndim)
    ins = [z3, z3, z3, z3, row(mu[:3 * c]), row(mu[3 * c:]), row(w0), w2, row(a0), a2, g2, row(k_k), row(k_a)]
    in_specs = [pl.BlockSpec((1, tm, 3 * c), lambda b, i: (b, i, 0)),
                pl.BlockSpec((1, HALO_ROWS, 3 * c), halo),
                pl.BlockSpec((1, tm, n_lora), lambda b, i: (b, i, lcb)),
                pl.BlockSpec((1, HALO_ROWS, n_lora), halo_l)] + [full(a) for a in ins[4:]]
    if has_vres:
        zv3, mu_v, v_first, v0, v2 = vres
        nv = zv3.shape[-1]
        extra = [zv3, zv3, row(mu_v), v_first, row(v0), v2]
        ins += extra
        in_specs += [pl.BlockSpec((1, tm, nv), lambda b, i: (b, i, 0)), pl.BlockSpec((1, HALO_ROWS, nv), halo),
                     full(extra[2]), pl.BlockSpec((1, tm, c), lambda b, i: (b, i, 0)), full(extra[4]),
                     full(extra[5])]
    out_spec = pl.BlockSpec((1, tm, c), lambda b, i: (b, i, 0))
    sd = lambda dt: jax.ShapeDtypeStruct((bsz, s, c), dt)
    out_dtypes = [BF16, F32, BF16, BF16, BF16, BF16, BF16] + ([] if has_vres else [F32])
    return pl.pallas_call(
        functools.partial(_rwkv_prep_kernel, has_vres),
        grid=(bsz, s // tm),
        in_specs=in_specs,
        out_specs=[out_spec] * len(out_dtypes),
        out_shape=[sd(dt) for dt in out_dtypes],
        compiler_params=_cparams("parallel", "arbitrary"),
        name="rwkv_prep",
    )(*ins)


def _dots(a_list, b_list, dims=_NN):
    return [lax.dot_general(a.astype(BF16), b.astype(BF16), dims, preferred_element_type=F32)
            for a, b in zip(a_list, b_list)]


def _pair_diag(x, even):
    return jnp.concatenate([jnp.where(even, x, 0.0), jnp.where(even, 0.0, x)], axis=0)


def _unit_lower_inverse(a_list, row_w, col_w, even):
    n = row_w.shape[0]
    lower = row_w > col_w
    base = lower & ((row_w >> 1) == (col_w >> 1))
    t = [jnp.where(row_w == col_w, 1.0, jnp.where(base, a, 0.0)) for a in a_list]
    sh = 1
    while (2 << sh) <= n:
        sub = lower & ((row_w >> (sh + 1)) == (col_w >> (sh + 1))) & ((row_w >> sh) != (col_w >> sh))
        off = [_pair_diag(jnp.where(sub, a, 0.0), even) for a in a_list]
        upd = _dots(_dots(t, off), [_pair_diag(x, even) for x in t])
        t = [x + u for x, u in zip(t, upd)]
        sh += 1
    return t


def _mm_exact_rhs_left(l_bf16, a):
    d = lambda p: lax.dot_general(l_bf16, p, _NN, preferred_element_type=F32)
    hi, mid, lo = _split3(a)
    return d(hi) + (d(mid) + d(lo))


def _wkv_kernel(n_cast, *refs):
    r_ref, lw_ref, k_ref, v_ref, an_ref, b_ref, g_ref, rk_ref, lg_ref, lb_ref = refs[:10]
    cast_in, y_ref = refs[10:10 + n_cast], refs[10 + n_cast]
    cast_out, state_ref = refs[11 + n_cast:11 + 2 * n_cast], refs[11 + 2 * n_cast]
    for src, dst in zip(cast_in, cast_out):
        dst[...] = src[...].astype(dst.dtype)

    nb, ln = r_ref.shape[0], r_ref.shape[1]
    n = RWKV_HEAD
    pw = 2 * n
    npair = r_ref.shape[2] // pw
    ent = [(bi, slice(p * pw, (p + 1) * pw)) for bi in range(nb) for p in range(npair)]

    @pl.when(pl.program_id(1) == 0)
    def _():
        state_ref[...] = jnp.zeros_like(state_ref)

    rows = lax.broadcasted_iota(jnp.int32, (ln, ln), 0)
    cols = lax.broadcasted_iota(jnp.int32, (ln, ln), 1)
    tril = jnp.where(rows >= cols, 1.0, 0.0).astype(BF16)
    r, k, v, w_end, r_t, a_t, b_t, k_t, b_h, k_h = ([] for _ in range(10))
    for bi in range(nb):
        lw = lw_ref[bi]
        cw = _mm_exact_rhs_left(tril, lw)
        cw_end = cw[ln - 1:ln, :]
        e_neg = jnp.exp(-cw)
        e_end = jnp.exp(cw_end - cw)
        bb = b_ref[bi].astype(F32)
        r.append(r_ref[bi].astype(F32))
        k.append(k_ref[bi].astype(F32))
        v.append(v_ref[bi].astype(F32))
        w_end.append(jnp.exp(cw_end))
        r_t.append(r[bi] * jnp.exp(cw))
        a_t.append(an_ref[bi].astype(F32) * jnp.exp(cw - lw))
        b_t.append(bb * e_neg)
        k_t.append(k[bi] * e_neg)
        b_h.append(bb * e_end)
        k_h.append(k[bi] * e_end)

    lane = lax.broadcasted_iota(jnp.int32, (ln, pw), 1)
    row_w = lax.broadcasted_iota(jnp.int32, (ln, pw), 0)
    even = lane < n
    even2 = lax.broadcasted_iota(jnp.int32, (2 * ln, pw), 1) < n
    col_w = lane & (n - 1)
    strict_w = row_w > col_w
    incl_w = row_w >= col_w
    zeros_w = jnp.zeros((ln, pw), F32)
    diag = lambda x: _pair_diag(x, even)
    swap = lambda x: jnp.concatenate([x[x.shape[0] // 2:], x[:x.shape[0] // 2]], axis=0)

    ar_p = [jnp.concatenate([a_t[bi][:, ps], r_t[bi][:, ps]], axis=0) for bi, ps in ent]
    bk_p = [jnp.concatenate([b_t[bi][:, ps], k_t[bi][:, ps]], axis=0).astype(BF16) for bi, ps in ent]
    kb_p = [jnp.concatenate([k_t[bi][:, ps], b_t[bi][:, ps]], axis=0).astype(BF16) for bi, ps in ent]
    am_e = _dots([jnp.where(even2, x, 0.0) for x in ar_p], bk_p, _NT)
    am_o = _dots([jnp.where(even2, 0.0, x) for x in ar_p], kb_p, _NT)
    a_ab = [jnp.where(strict_w, jnp.where(even, e[:ln], o[:ln]), 0.0) for e, o in zip(am_e, am_o)]
    a_ak = [jnp.where(strict_w, jnp.where(even, o[:ln], e[:ln]), 0.0) for e, o in zip(am_e, am_o)]
    a_rb = [jnp.where(incl_w, jnp.where(even, e[ln:], o[ln:]), 0.0) for e, o in zip(am_e, am_o)]
    a_rk = [jnp.where(incl_w, jnp.where(even, o[ln:], e[ln:]), 0.0) for e, o in zip(am_e, am_o)]
    v_p = [v[bi][:, ps] for bi, ps in ent]
    v_d = [diag(x) for x in v_p]
    akv = _dots(a_ak, [swap(x) for x in v_d])
    t = _unit_lower_inverse(a_ab, row_w, col_w, even)
    rhs = [jnp.concatenate([diag(a_t[bi][:, ps]), diag(x)], axis=1) for (bi, ps), x in zip(ent, akv)]
    pq = _dots(t, rhs)
    ry = _dots([jnp.concatenate([x, y], axis=1) for x, y in zip(a_rb, a_rk)],
               [jnp.concatenate([jnp.concatenate([diag(x[:, :pw]), diag(x[:, pw:])], axis=1),
                                 jnp.concatenate([jnp.zeros((2 * ln, pw), F32), swap(u)], axis=1)], axis=0)
                for x, u in zip(pq, v_d)])
    pqv_p = [jnp.concatenate([x, jnp.concatenate([zeros_w, u], axis=1)], axis=0)
             for x, u in zip(pq, v_p)]
    bkh_p = [jnp.concatenate([b_h[bi][:, ps], k_h[bi][:, ps]], axis=0) for bi, ps in ent]
    mn_p = _dots(bkh_p, pqv_p, _TN)
    sq_r = lax.broadcasted_iota(jnp.int32, (pw, pw), 0)
    sq_c = lax.broadcasted_iota(jnp.int32, (pw, pw), 1)
    same_head = (sq_r < n) == (sq_c < n)
    lhs_p = [jnp.concatenate([r_t[bi][:, ps] + x[:, :pw],
                              jnp.where(sq_r == sq_c, jnp.broadcast_to(w_end[bi][:, ps], (pw, pw)),
                                        jnp.where(same_head, m[:, :pw], 0.0))], axis=0)
             for (bi, ps), x, m in zip(ent, ry, mn_p)]
    st = [state_ref[e] for e in range(len(ent))]
    upd = _dots(lhs_p, st)
    for e in range(len(ent)):
        state_ref[e] = upd[e][ln:, :] + jnp.where(same_head, mn_p[e][:, pw:], 0.0)
    y0_p = [x[:, pw:] for x in ry]

    hsum = lambda x: jnp.where(even, jnp.sum(jnp.where(even, x, 0.0), axis=-1, keepdims=True),
                               jnp.sum(jnp.where(even, 0.0, x), axis=-1, keepdims=True))
    inv_n = 1.0 / n
    yn = []
    for u, y0 in zip(upd, y0_p):
        y = u[:ln, :] + y0
        d = y - hsum(y) * inv_n
        yn.append(d * lax.rsqrt(hsum(d * d) * inv_n + LNX_EPS))
    for bi in range(nb):
        rk = r[bi] * k[bi] * rk_ref[...]
        mine = range(bi * npair, (bi + 1) * npair)
        bonus = jnp.concatenate([hsum(rk[:, ent[e][1]]) * v_p[e] for e in mine], axis=1)
        out = jnp.concatenate([yn[e] for e in mine], axis=1) * lg_ref[...] + lb_ref[...] + bonus
        y_ref[bi] = (out * g_ref[bi].astype(F32)).astype(y_ref.dtype)


def _wkv(r, lw, k, v, an, b, g, r_k, lnx_g, lnx_b, casts=(), heads_per_step=16):
    bsz, s, c = r.shape
    wb = heads_per_step * RWKV_HEAD
    nc = s // WKV_CHUNK
    assert s % WKV_CHUNK == 0 and c % wb == 0 and heads_per_step % 2 == 0 and WKV_CHUNK == RWKV_HEAD
    assert not casts or c == wb
    spec = pl.BlockSpec((bsz, WKV_CHUNK, wb), lambda hi, ci: (0, ci, hi))
    pspec = pl.BlockSpec((1, wb), lambda hi, ci: (0, hi))
    row = lambda a: a.reshape(1, -1)
    cast_ins, cast_in_specs, cast_out_specs, cast_out_shape = [], [], [], []
    for w, layer in casts:
        nl, rows, cols = w.shape
        assert rows % (16 * nc) == 0
        slab = rows // nc
        cast_ins.append(w.reshape(nl, nc, slab, cols))
        cast_in_specs.append(pl.BlockSpec((None, None, slab, cols), lambda hi, ci, layer=layer: (layer, ci, 0, 0)))
        cast_out_specs.append(pl.BlockSpec((None, slab, cols), lambda hi, ci: (ci, 0, 0)))
        cast_out_shape.append(jax.ShapeDtypeStruct((nc, slab, cols), BF16))
    out = pl.pallas_call(
        functools.partial(_wkv_kernel, len(casts)),
        grid=(c // wb, nc),
        in_specs=[spec] * 7 + [pspec] * 3 + cast_in_specs,
        out_specs=[spec] + cast_out_specs,
        out_shape=[jax.ShapeDtypeStruct((bsz, s, c), BF16)] + cast_out_shape,
        scratch_shapes=[pltpu.VMEM((bsz * heads_per_step // 2, 2 * RWKV_HEAD, 2 * RWKV_HEAD), F32)],
        compiler_params=_cparams("parallel", "arbitrary"),
        name="wkv",
    )(r, lw, k, v, an, b, g, row(r_k), row(lnx_g), row(lnx_b), *cast_ins)
    return out[0], [o.reshape(w.shape[1], w.shape[2]) for o, (w, _) in zip(out[1:], casts)]


def _rope_kernel(q_ref, k_ref, v_ref, cos_ref, sin_ref, qt_ref, ko_ref, vt_ref, km_ref):
    cos = cos_ref[...]
    sin = sin_ref[...]
    nh = q_ref.shape[-1] // ATTN_HEAD
    inv_rows = 1.0 / q_ref.shape[1]
    for h in range(nh):
        sl = slice(h * ATTN_HEAD, (h + 1) * ATTN_HEAD)
        q = q_ref[0, :, sl].astype(F32)
        k = k_ref[0, :, sl].astype(F32)
        qr = q * cos + pltpu.roll(q, ATTN_HEAD // 2, axis=1) * sin
        kr = k * cos + pltpu.roll(k, ATTN_HEAD // 2, axis=1) * sin
        qt_ref[0, h, 0] = qr.T
        ko_ref[0, h, 0] = kr.astype(BF16)
        vt_ref[0, h, 0] = v_ref[0, :, sl].astype(F32).T.astype(BF16)
        km_ref[0, 0, :, sl] = jnp.sum(kr, axis=0, keepdims=True) * inv_rows


def _rope(z3, col0, width, cos2, sin2):
    bsz, s, _ = z3.shape
    assert s % MOBA_BLOCK == 0 and col0 % ATTN_HEAD == 0
    nb = s // MOBA_BLOCK
    nh = width // ATTN_HEAD
    blk = lambda j: pl.BlockSpec((pl.Element(1), pl.Element(MOBA_BLOCK), pl.Element(width)),
                                 lambda b, i: (b, i * MOBA_BLOCK, col0 + j * width))
    tab = pl.BlockSpec((MOBA_BLOCK, ATTN_HEAD), lambda b, i: (i, 0))
    t_spec = pl.BlockSpec((1, nh, 1, ATTN_HEAD, MOBA_BLOCK), lambda b, i: (b, 0, i, 0, 0))
    n_spec = pl.BlockSpec((1, nh, 1, MOBA_BLOCK, ATTN_HEAD), lambda b, i: (b, 0, i, 0, 0))
    return pl.pallas_call(
        _rope_kernel,
        grid=(bsz, nb),
        in_specs=[blk(0), blk(1), blk(2), tab, tab],
        out_specs=[t_spec, n_spec, t_spec, pl.BlockSpec((1, 1, 1, width), lambda b, i: (b, i, 0, 0))],
        out_shape=[jax.ShapeDtypeStruct((bsz, nh, nb, ATTN_HEAD, MOBA_BLOCK), F32),
                   jax.ShapeDtypeStruct((bsz, nh, nb, MOBA_BLOCK, ATTN_HEAD), BF16),
                   jax.ShapeDtypeStruct((bsz, nh, nb, ATTN_HEAD, MOBA_BLOCK), BF16),
                   jax.ShapeDtypeStruct((bsz, nb, 1, width), F32)],
        compiler_params=_cparams("parallel", "parallel"),
        name="rope",
    )(z3, z3, z3, cos2, sin2)


def _moba_kernel(qt_ref, k_ref, vt_ref, km_ref, o_ref, bias_ref):
    blk = MOBA_BLOCK
    dh = ATTN_HEAD
    hs = range(qt_ref.shape[1])
    qb = pl.program_id(2)
    nb = km_ref.shape[1]
    scale = ATTN_HEAD ** -0.5
    neg = -jnp.inf
    qt = [qt_ref[0, h, 0] for h in hs]
    blk_id = lax.broadcasted_iota(jnp.int32, (nb, blk), 0)
    past = blk_id < qb
    gate = [jnp.where(past, _mm(km_ref[0, :, h * dh:(h + 1) * dh], qt[h], passes=3), neg) for h in hs]
    for h in hs:
        rank = jnp.zeros((nb, blk), jnp.int32)
        for m in range(nb):
            gm = gate[h][m:m + 1, :]
            rank += ((gm > gate[h]) | ((gm == gate[h]) & (m < blk_id))).astype(jnp.int32)
        bias_ref[h] = jnp.where(past & (rank < MOBA_TOPK), 0.0, neg)

    qs = [(q * (scale * LOG2E)).astype(BF16) for q in qt]
    ki = lax.broadcasted_iota(jnp.int32, (blk, blk), 0)
    qi = lax.broadcasted_iota(jnp.int32, (blk, blk), 1)
    causal = ki <= qi

    def pipelined(work, stage):
        scores = lambda kb, h: jnp.dot(k_ref[0, h, kb], qs[h], preferred_element_type=F32)
        ahead = [scores(*w) for w in work[:MOBA_LOOKAHEAD]]
        for i, (kb, h) in enumerate(work):
            if i + MOBA_LOOKAHEAD < len(work):
                ahead.append(scores(*work[i + MOBA_LOOKAHEAD]))
            stage(kb, h, ahead[i])

    m_run, l_run, acc = [None] * len(hs), [None] * len(hs), [None] * len(hs)

    def own_block(kb, h, s):
        s = jnp.where(causal, s, neg)
        m_run[h] = jnp.max(s, axis=0, keepdims=True)
        p = jnp.exp2(s - m_run[h])
        l_run[h] = jnp.sum(p, axis=0, keepdims=True)
        acc[h] = jnp.dot(vt_ref[0, h, kb], p.astype(BF16), preferred_element_type=F32)

    pipelined([(qb, h) for h in hs], own_block)

    def past_blocks(kbs, carry):
        m_c, l_c, acc_c = (list(c) for c in carry)

        def stage(kb, h, s):
            b = bias_ref[h, pl.ds(kb, 1), :]
            m_new = jnp.where(b == 0.0, jnp.maximum(m_c[h], jnp.max(s, axis=0, keepdims=True)), m_c[h])
            alpha = jnp.exp2(m_c[h] - m_new)
            p = jnp.exp2(s - (m_new - b))
            pv = jnp.dot(vt_ref[0, h, kb], p.astype(BF16), preferred_element_type=F32)
            m_c[h] = m_new
            l_c[h] = alpha * l_c[h] + jnp.sum(p, axis=0, keepdims=True)
            acc_c[h] = alpha * acc_c[h] + pv

        pipelined([(kb, h) for kb in kbs for h in hs], stage)
        return tuple(m_c), tuple(l_c), tuple(acc_c)

    u = MOBA_UNROLL
    carry = lax.fori_loop(0, qb // u, lambda j, c: past_blocks([j * u + i for i in range(u)], c),
                          (tuple(m_run), tuple(l_run), tuple(acc)))
    _, l_fin, acc = lax.fori_loop((qb // u) * u, qb, lambda kb, c: past_blocks([kb], c), carry)
    for h in hs:
        o_ref[0, :, h * dh:(h + 1) * dh] = (acc[h] / l_fin[h]).T.astype(o_ref.dtype)


def _moba(qt, k, vt, kmean, heads_per_step=8):
    bsz, nh, nb, dh, blk = qt.shape
    hp = heads_per_step
    assert nh % hp == 0
    return pl.pallas_call(
        _moba_kernel,
        grid=(bsz, nh // hp, nb),
        in_specs=[pl.BlockSpec((1, hp, 1, dh, blk), lambda b, h, i: (b, h, i, 0, 0)),
                  pl.BlockSpec((1, hp, nb, blk, dh), lambda b, h, i: (b, h, 0, 0, 0)),
                  pl.BlockSpec((1, hp, nb, dh, blk), lambda b, h, i: (b, h, 0, 0, 0)),
                  pl.BlockSpec((1, nb, hp * dh), lambda b, h, i: (b, 0, h))],
        out_specs=pl.BlockSpec((1, blk, hp * dh), lambda b, h, i: (b, i, h)),
        out_shape=jax.ShapeDtypeStruct((bsz, nb * blk, nh * dh), BF16),
        scratch_shapes=[pltpu.VMEM((hp, nb, blk), F32)],
        compiler_params=_cparams("parallel", "parallel", "arbitrary"),
        name="moba",
    )(qt, k, vt, kmean)


def _out_proj_kernel(yr_ref, ya_ref, wr_ref, wa_ref, x_ref, g_ref, o_ref):
    for r in range(0, x_ref.shape[0], ROW_CHUNK):
        rows = pl.ds(r, ROW_CHUNK)
        y = jnp.dot(yr_ref[rows, :], wr_ref[...], preferred_element_type=F32)
        y += jnp.dot(ya_ref[rows, :], wa_ref[...], preferred_element_type=F32)
        ms = jnp.mean(y * y, axis=-1, keepdims=True)
        o_ref[rows, :] = x_ref[rows, :] + y * lax.rsqrt(ms + NORM_EPS) * g_ref[...]


def _out_proj(y_r, y_a, w, x2, gain, tm=512):
    m, d = x2.shape
    cw = y_r.shape[1]
    assert y_a.shape[1] == cw and w.shape[0] == 2 * cw and m % tm == 0
    return pl.pallas_call(
        _out_proj_kernel,
        grid=(m // tm,),
        in_specs=[pl.BlockSpec((tm, cw), lambda i: (i, 0)), pl.BlockSpec((tm, cw), lambda i: (i, 0)),
                  pl.BlockSpec((cw, d), lambda i: (0, 0)), pl.BlockSpec((cw, d), lambda i: (1, 0)),
                  pl.BlockSpec((tm, d), lambda i: (i, 0)), pl.BlockSpec((1, d), lambda i: (0, 0))],
        out_specs=pl.BlockSpec((tm, d), lambda i: (i, 0)),
        out_shape=jax.ShapeDtypeStruct((m, d), F32),
        compiler_params=_cparams("parallel"),
        name="out_proj",
    )(y_r, y_a, w, w, x2, gain.reshape(1, d))


def _mlp_kernel(x_ref, gpre_ref, wu_ref, wd_ref, gpost_ref, o_ref, h_ref, acc_ref):
    f = pl.program_id(1)
    last = pl.num_programs(1) - 1
    chunks = [pl.ds(r, ROW_CHUNK) for r in range(0, x_ref.shape[0], ROW_CHUNK)]

    def part(h):
        u = jnp.maximum(jnp.dot(h, wu_ref[...], preferred_element_type=F32), 0.0)
        return jnp.dot((u * u).astype(BF16), wd_ref[...], preferred_element_type=F32)

    @pl.when(f == 0)
    def _():
        for rows in chunks:
            x = x_ref[rows, :]
            ms = jnp.mean(x * x, axis=-1, keepdims=True)
            h = (x * lax.rsqrt(ms + NORM_EPS) * gpre_ref[...]).astype(BF16)
            h_ref[rows, :] = h
            acc_ref[rows, :] = part(h)

    @pl.when((f > 0) & (f < last))
    def _():
        acc_ref[...] += part(h_ref[...])

    @pl.when(f == last)
    def _():
        for rows in chunks:
            mlp = acc_ref[rows, :] + part(h_ref[rows, :])
            ms = jnp.mean(mlp * mlp, axis=-1, keepdims=True)
            o_ref[rows, :] = x_ref[rows, :] + mlp * lax.rsqrt(ms + NORM_EPS) * gpost_ref[...]


def _mlp(x2, g_pre, w_up, w_down, g_post, tm=1024, tf=512):
    m, d = x2.shape
    dff = w_up.shape[1]
    assert m % tm == 0 and dff % tf == 0 and dff // tf >= 2 and tm % ROW_CHUNK == 0
    return pl.pallas_call(
        _mlp_kernel,
        grid=(m // tm, dff // tf),
        in_specs=[pl.BlockSpec((tm, d), lambda i, f: (i, 0)),
                  pl.BlockSpec((1, d), lambda i, f: (0, 0)),
                  pl.BlockSpec((d, tf), lambda i, f: (0, f)),
                  pl.BlockSpec((tf, d), lambda i, f: (f, 0)),
                  pl.BlockSpec((1, d), lambda i, f: (0, 0))],
        out_specs=pl.BlockSpec((tm, d), lambda i, f: (i, 0)),
        out_shape=jax.ShapeDtypeStruct((m, d), F32),
        scratch_shapes=[pltpu.VMEM((tm, d), BF16), pltpu.VMEM((tm, d), F32)],
        compiler_params=_cparams("parallel", "arbitrary"),
        name="mlp",
    )(x2, g_pre.reshape(1, d), w_up, w_down, g_post.reshape(1, d))


def _rope_tables(s):
    half = ATTN_HEAD // 2
    inv_freq = ROPE_THETA ** (-jnp.arange(half, dtype=F32) / half)
    ang = jnp.arange(s).astype(F32)[:, None] * inv_freq[None, :]
    cos, sin = jnp.cos(ang), jnp.sin(ang)
    return jnp.concatenate([cos, cos], axis=-1), jnp.concatenate([-sin, sin], axis=-1)


def kernel(x, norm_mix_pre, norm_mix_post, norm_mlp_pre, norm_mlp_post, w_in, w_in_vres, shift_mu, shift_mu_vres, decay_w0, decay_w2, iclr_a0, iclr_a2, vres_v0, vres_v2, gate_g2, k_k, k_a, r_k, lnx_gain, lnx_bias, w_out, w_up, w_down):
    bsz, s, d = x.shape
    depth = w_in.shape[0]
    c = decay_w0.shape[1]
    n_lora = DECAY_LORA + ICLR_LORA + GATE_LORA
    n_shift = 3 * c + n_lora
    ca = (w_in.shape[2] - n_shift) // 3
    cos2, sin2 = _rope_tables(s)
    w_in16 = w_in[0].astype(BF16)
    pad_v = VRES_PAD - VRES_LORA
    x2 = x.reshape(bsz * s, d)
    v_first = None
    for i in range(depth):
        if i == 0:
            z, vres = _norm_matmul(x2, norm_mix_pre[i], w_in16)[0], None
        else:
            w_v = jnp.pad(w_in_vres[i - 1], ((0, 0), (0, pad_v))).astype(BF16)
            z, z_v = _norm_matmul(x2, norm_mix_pre[i], w_in16, w_v)
            vres = (z_v.reshape(bsz, s, VRES_PAD), jnp.pad(shift_mu_vres[i - 1], (0, pad_v)), v_first,
                    vres_v0[i - 1], jnp.pad(vres_v2[i - 1], ((0, pad_v), (0, 0))))
        z3 = z.reshape(bsz, s, -1)

        prep = _rwkv_prep(z3, c, shift_mu[i], decay_w0[i], decay_w2[i], iclr_a0[i], iclr_a2[i], gate_g2[i],
                          k_k[i], k_a[i], vres)
        if i == 0:
            v_first = prep[7]
        casts = [(w_out, i), (w_up, i), (w_down, i)] + ([(w_in, i + 1)] if i + 1 < depth else [])
        y_r, w16 = _wkv(*prep[:7], r_k[i].reshape(-1), lnx_gain[i], lnx_bias[i], casts)
        w_out16, w_up16, w_down16 = w16[:3]

        q_t, k_rot, v_t, kmean = _rope(z3, n_shift, ca, cos2, sin2)
        y_a = _moba(q_t, k_rot, v_t, kmean.reshape(bsz, -1, ca))

        x2 = _out_proj(y_r.reshape(bsz * s, c), y_a.reshape(bsz * s, ca), w_out16, x2, norm_mix_post[i])
        x2 = _mlp(x2, norm_mlp_pre[i], w_up16, w_down16, norm_mlp_post[i])
        if i + 1 < depth:
            w_in16 = w16[3]
    return x2.reshape(bsz, s, d)
```

```python
import functools

import jax
import jax.numpy as jnp
from jax import lax
from jax.experimental import pallas as pl
from jax.experimental.pallas import tpu as pltpu

F32 = jnp.float32
BF16 = jnp.bfloat16

RWKV_HEAD = 64
DECAY_LORA = 64
ICLR_LORA = 64
VRES_LORA = 32
GATE_LORA = 128
ATTN_HEAD = 128
MOBA_BLOCK = 256
MOBA_TOPK = 3
ROPE_THETA = 10000.0
NORM_EPS = 1e-6
LNX_EPS = 64e-5
LOG2E = 1.4426950408889634

VRES_PAD = 128
WKV_CHUNK = 64
MOBA_LOOKAHEAD = 6
MOBA_UNROLL = 4
HALO_ROWS = 16
ROW_CHUNK = 256
VMEM_LIMIT = 56 * 1024 * 1024


def _cparams(*sem):
    return pltpu.CompilerParams(dimension_semantics=sem, vmem_limit_bytes=VMEM_LIMIT)


_NN = (((1,), (0,)), ((), ()))
_NT = (((1,), (1,)), ((), ()))
_TN = (((0,), (0,)), ((), ()))


def _split2(x):
    hi = x.astype(BF16)
    lo = (x - hi.astype(F32)).astype(BF16)
    return hi, lo


def _split3(x):
    hi = x.astype(BF16)
    r1 = x - hi.astype(F32)
    mid = r1.astype(BF16)
    lo = (r1 - mid.astype(F32)).astype(BF16)
    return hi, mid, lo


def _mm(a, b, dims=_NN, passes=1):
    d = lambda p, q: lax.dot_general(p, q, dims, preferred_element_type=F32)
    if passes == 1:
        return d(a.astype(BF16), b.astype(BF16))
    ah, al = _split2(a)
    bh, bl = _split2(b)
    return d(ah, bh) + (d(ah, bl) + d(al, bh))


def _sigmoid(x):
    return 1.0 / (1.0 + jnp.exp(-x))


def _softplus(x):
    return jnp.maximum(x, 0.0) + jnp.log(1.0 + jnp.exp(-jnp.abs(x)))


def _norm_matmul_kernel(has_extra, *refs):
    if has_extra:
        x_ref, g_ref, w_ref, we_ref, o_ref, oe_ref, h_ref = refs
    else:
        x_ref, g_ref, w_ref, o_ref, h_ref = refs

    first = pl.program_id(1) == 0

    @pl.when(first)
    def _():
        for r in range(0, x_ref.shape[0], ROW_CHUNK):
            rows = pl.ds(r, ROW_CHUNK)
            x = x_ref[rows, :]
            ms = jnp.mean(x * x, axis=-1, keepdims=True)
            h = (x * lax.rsqrt(ms + NORM_EPS) * g_ref[...]).astype(BF16)
            h_ref[rows, :] = h
            o_ref[rows, :] = jnp.dot(h, w_ref[...], preferred_element_type=F32).astype(o_ref.dtype)
            if has_extra:
                oe_ref[rows, :] = jnp.dot(h, we_ref[...], preferred_element_type=F32).astype(oe_ref.dtype)

    @pl.when(jnp.logical_not(first))
    def _():
        o_ref[...] = jnp.dot(h_ref[...], w_ref[...], preferred_element_type=F32).astype(o_ref.dtype)


def _norm_matmul(x2, gain, w, w_extra=None, tm=1024, tn=1280):
    m, d = x2.shape
    n = w.shape[1]
    assert m % tm == 0 and n % tn == 0
    has_extra = w_extra is not None
    ins = [x2, gain.reshape(1, d), w]
    in_specs = [pl.BlockSpec((tm, d), lambda i, j: (i, 0)),
                pl.BlockSpec((1, d), lambda i, j: (0, 0)),
                pl.BlockSpec((d, tn), lambda i, j: (0, j))]
    out_specs = [pl.BlockSpec((tm, tn), lambda i, j: (i, j))]
    out_shape = [jax.ShapeDtypeStruct((m, n), BF16)]
    if has_extra:
        ne = w_extra.shape[1]
        ins.append(w_extra)
        in_specs.append(pl.BlockSpec((d, ne), lambda i, j: (0, 0)))
        out_specs.append(pl.BlockSpec((tm, ne), lambda i, j: (i, 0)))
        out_shape.append(jax.ShapeDtypeStruct((m, ne), BF16))
    return pl.pallas_call(
        functools.partial(_norm_matmul_kernel, has_extra),
        grid=(m // tm, n // tn),
        in_specs=in_specs,
        out_specs=out_specs,
        out_shape=out_shape,
        scratch_shapes=[pltpu.VMEM((tm, d), BF16)],
        compiler_params=_cparams("parallel", "arbitrary"),
        name="norm_matmul",
    )(*ins)


def _shifted(cur, halo, mu, first):
    cur = cur.astype(F32)
    rows = lax.broadcasted_iota(jnp.int32, cur.shape, 0)
    last = jnp.where(first, 0.0, halo[HALO_ROWS - 1:HALO_ROWS, :].astype(F32))
    prev = jnp.where(rows == 0, last, pltpu.roll(cur, 1, axis=0))
    return cur + (prev - cur) * mu


def _rwkv_prep_kernel(has_vres, *refs):
    if has_vres:
        (zm_ref, zmh_ref, zl_ref, zlh_ref, mum_ref, mul_ref, w0_ref, w2_ref, a0_ref, a2_ref, g2_ref,
         kk_ref, ka_ref, zv_ref, zvh_ref, muv_ref, vf_ref, v0_ref, v2_ref,
         r_ref, lw_ref, k_ref, v_ref, an_ref, b_ref, g_ref) = refs
    else:
        (zm_ref, zmh_ref, zl_ref, zlh_ref, mum_ref, mul_ref, w0_ref, w2_ref, a0_ref, a2_ref, g2_ref,
         kk_ref, ka_ref,
         r_ref, lw_ref, k_ref, v_ref, an_ref, b_ref, g_ref, vf32_ref) = refs
    c = r_ref.shape[-1]
    first = pl.program_id(1) == 0
    zs = _shifted(zm_ref[0], zmh_ref[0], mum_ref[...], first)
    zl = _shifted(zl_ref[0], zlh_ref[0], mul_ref[...], first)
    r, k, v = zs[:, :c], zs[:, c:2 * c], zs[:, 2 * c:3 * c]
    o = 0
    wd = zl[:, o:o + DECAY_LORA]
    o += DECAY_LORA
    ad = zl[:, o:o + ICLR_LORA]
    o += ICLR_LORA
    gd = zl[:, o:o + GATE_LORA]
    o += GATE_LORA
    w_log = -_softplus(-(w0_ref[...] + _mm(jnp.tanh(wd), w2_ref[...], passes=3))) - 0.5
    lw_ref[0] = -jnp.exp(w_log)
    a = _sigmoid(a0_ref[...] + _mm(ad, a2_ref[...]))
    g_ref[0] = _mm(_sigmoid(gd), g2_ref[...]).astype(g_ref.dtype)
    if has_vres:
        vd = _shifted(zv_ref[0], zvh_ref[0], muv_ref[...], first)
        v = v + (vf_ref[0] - v) * _sigmoid(v0_ref[...] + _mm(vd, v2_ref[...]))
    else:
        vf32_ref[0] = v
    kk = k * kk_ref[...]
    pw = 2 * RWKV_HEAD
    even = lax.broadcasted_iota(jnp.int32, (kk.shape[0], pw), 1) < RWKV_HEAD
    unit = []
    for p in range(c // pw):
        x = kk[:, p * pw:(p + 1) * pw]
        sq = x * x
        ss = jnp.where(even, jnp.sum(jnp.where(even, sq, 0.0), axis=-1, keepdims=True),
                       jnp.sum(jnp.where(even, 0.0, sq), axis=-1, keepdims=True))
        unit.append(x / jnp.maximum(jnp.sqrt(ss), 1e-12))
    kk = jnp.concatenate(unit, axis=1)
    r_ref[0] = r.astype(r_ref.dtype)
    k_ref[0] = (k * (1.0 + (a - 1.0) * ka_ref[...])).astype(k_ref.dtype)
    v_ref[0] = v.astype(v_ref.dtype)
    an_ref[0] = (-kk).astype(an_ref.dtype)
    b_ref[0] = (kk * a).astype(b_ref.dtype)


def _rwkv_prep(z3, c, mu, w0, w2, a0, a2, g2, k_k, k_a, vres, tm=256):
    bsz, s, _ = z3.shape
    n_lora = w2.shape[0] + a2.shape[0] + g2.shape[0]
    assert s % tm == 0 and (3 * c) % n_lora == 0 and n_lora % 128 == 0
    has_vres = vres is not None
    row = lambda a: a.reshape(1, -1)
    hb = tm // HALO_ROWS
    lcb = 3 * c // n_lora
    halo = lambda b, i: (b, jnp.maximum(i * hb - 1, 0), 0)
    halo_l = lambda b, i: (b, jnp.maximum(i * hb - 1, 0), lcb)
    full = lambda a: pl.BlockSpec(a.shape, lambda b, i: (0,) * a.ndim)
    ins = [z3, z3, z3, z3, row(mu[:3 * c]), row(mu[3 * c:]), row(w0), w2, row(a0), a2, g2, row(k_k), row(k_a)]
    in_specs = [pl.BlockSpec((1, tm, 3 * c), lambda b, i: (b, i, 0)),
                pl.BlockSpec((1, HALO_ROWS, 3 * c), halo),
                pl.BlockSpec((1, tm, n_lora), lambda b, i: (b, i, lcb)),
                pl.BlockSpec((1, HALO_ROWS, n_lora), halo_l)] + [full(a) for a in ins[4:]]
    if has_vres:
        zv3, mu_v, v_first, v0, v2 = vres
        nv = zv3.shape[-1]
        extra = [zv3, zv3, row(mu_v), v_first, row(v0), v2]
        ins += extra
        in_specs += [pl.BlockSpec((1, tm, nv), lambda b, i: (b, i, 0)), pl.BlockSpec((1, HALO_ROWS, nv), halo),
                     full(extra[2]), pl.BlockSpec((1, tm, c), lambda b, i: (b, i, 0)), full(extra[4]),
                     full(extra[5])]
    out_spec = pl.BlockSpec((1, tm, c), lambda b, i: (b, i, 0))
    sd = lambda dt: jax.ShapeDtypeStruct((bsz, s, c), dt)
    out_dtypes = [BF16, F32, BF16, BF16, BF16, BF16, BF16] + ([] if has_vres else [F32])
    return pl.pallas_call(
        functools.partial(_rwkv_prep_kernel, has_vres),
        grid=(bsz, s // tm),
        in_specs=in_specs,
        out_specs=[out_spec] * len(out_dtypes),
        out_shape=[sd(dt) for dt in out_dtypes],
        compiler_params=_cparams("parallel", "arbitrary"),
        name="rwkv_prep",
    )(*ins)


def _dots(a_list, b_list, dims=_NN):
    return [lax.dot_general(a.astype(BF16), b.astype(BF16), dims, preferred_element_type=F32)
            for a, b in zip(a_list, b_list)]


def _pair_diag(x, even):
    return jnp.concatenate([jnp.where(even, x, 0.0), jnp.where(even, 0.0, x)], axis=0)


def _unit_lower_inverse(a_list, row_w, col_w, even):
    n = row_w.shape[0]
    lower = row_w > col_w
    base = lower & ((row_w >> 1) == (col_w >> 1))
    t = [jnp.where(row_w == col_w, 1.0, jnp.where(base, a, 0.0)) for a in a_list]
    sh = 1
    while (2 << sh) <= n:
        sub = lower & ((row_w >> (sh + 1)) == (col_w >> (sh + 1))) & ((row_w >> sh) != (col_w >> sh))
        off = [_pair_diag(jnp.where(sub, a, 0.0), even) for a in a_list]
        upd = _dots(_dots(t, off), [_pair_diag(x, even) for x in t])
        t = [x + u for x, u in zip(t, upd)]
        sh += 1
    return t


def _mm_exact_rhs_left(l_bf16, a):
    d = lambda p: lax.dot_general(l_bf16, p, _NN, preferred_element_type=F32)
    hi, mid, lo = _split3(a)
    return d(hi) + (d(mid) + d(lo))


def _wkv_kernel(n_cast, *refs):
    r_ref, lw_ref, k_ref, v_ref, an_ref, b_ref, g_ref, rk_ref, lg_ref, lb_ref = refs[:10]
    cast_in, y_ref = refs[10:10 + n_cast], refs[10 + n_cast]
    cast_out, state_ref = refs[11 + n_cast:11 + 2 * n_cast], refs[11 + 2 * n_cast]
    for src, dst in zip(cast_in, cast_out):
        dst[...] = src[...].astype(dst.dtype)

    nb, ln = r_ref.shape[0], r_ref.shape[1]
    n = RWKV_HEAD
    pw = 2 * n
    npair = r_ref.shape[2] // pw
    ent = [(bi, slice(p * pw, (p + 1) * pw)) for bi in range(nb) for p in range(npair)]

    @pl.when(pl.program_id(1) == 0)
    def _():
        state_ref[...] = jnp.zeros_like(state_ref)

    rows = lax.broadcasted_iota(jnp.int32, (ln, ln), 0)
    cols = lax.broadcasted_iota(jnp.int32, (ln, ln), 1)
    tril = jnp.where(rows >= cols, 1.0, 0.0).astype(BF16)
    r, k, v, w_end, r_t, a_t, b_t, k_t, b_h, k_h = ([] for _ in range(10))
    for bi in range(nb):
        lw = lw_ref[bi]
        cw = _mm_exact_rhs_left(tril, lw)
        cw_end = cw[ln - 1:ln, :]
        e_neg = jnp.exp(-cw)
        e_end = jnp.exp(cw_end - cw)
        bb = b_ref[bi].astype(F32)
        r.append(r_ref[bi].astype(F32))
        k.append(k_ref[bi].astype(F32))
        v.append(v_ref[bi].astype(F32))
        w_end.append(jnp.exp(cw_end))
        r_t.append(r[bi] * jnp.exp(cw))
        a_t.append(an_ref[bi].astype(F32) * jnp.exp(cw - lw))
        b_t.append(bb * e_neg)
        k_t.append(k[bi] * e_neg)
        b_h.append(bb * e_end)
        k_h.append(k[bi] * e_end)

    lane = lax.broadcasted_iota(jnp.int32, (ln, pw), 1)
    row_w = lax.broadcasted_iota(jnp.int32, (ln, pw), 0)
    even = lane < n
    even2 = lax.broadcasted_iota(jnp.int32, (2 * ln, pw), 1) < n
    col_w = lane & (n - 1)
    strict_w = row_w > col_w
    incl_w = row_w >= col_w
    zeros_w = jnp.zeros((ln, pw), F32)
    diag = lambda x: _pair_diag(x, even)
    swap = lambda x: jnp.concatenate([x[x.shape[0] // 2:], x[:x.shape[0] // 2]], axis=0)

    ar_p = [jnp.concatenate([a_t[bi][:, ps], r_t[bi][:, ps]], axis=0) for bi, ps in ent]
    bk_p = [jnp.concatenate([b_t[bi][:, ps], k_t[bi][:, ps]], axis=0).astype(BF16) for bi, ps in ent]
    kb_p = [jnp.concatenate([k_t[bi][:, ps], b_t[bi][:, ps]], axis=0).astype(BF16) for bi, ps in ent]
    am_e = _dots([jnp.where(even2, x, 0.0) for x in ar_p], bk_p, _NT)
    am_o = _dots([jnp.where(even2, 0.0, x) for x in ar_p], kb_p, _NT)
    a_ab = [jnp.where(strict_w, jnp.where(even, e[:ln], o[:ln]), 0.0) for e, o in zip(am_e, am_o)]
    a_ak = [jnp.where(strict_w, jnp.where(even, o[:ln], e[:ln]), 0.0) for e, o in zip(am_e, am_o)]
    a_rb = [jnp.where(incl_w, jnp.where(even, e[ln:], o[ln:]), 0.0) for e, o in zip(am_e, am_o)]
    a_rk = [jnp.where(incl_w, jnp.where(even, o[ln:], e[ln:]), 0.0) for e, o in zip(am_e, am_o)]
    v_p = [v[bi][:, ps] for bi, ps in ent]
    v_d = [diag(x) for x in v_p]
    akv = _dots(a_ak, [swap(x) for x in v_d])
    t = _unit_lower_inverse(a_ab, row_w, col_w, even)
    rhs = [jnp.concatenate([diag(a_t[bi][:, ps]), diag(x)], axis=1) for (bi, ps), x in zip(ent, akv)]
    pq = _dots(t, rhs)
    ry = _dots([jnp.concatenate([x, y], axis=1) for x, y in zip(a_rb, a_rk)],
               [jnp.concatenate([jnp.concatenate([diag(x[:, :pw]), diag(x[:, pw:])], axis=1),
                                 jnp.concatenate([jnp.zeros((2 * ln, pw), F32), swap(u)], axis=1)], axis=0)
                for x, u in zip(pq, v_d)])
    pqv_p = [jnp.concatenate([x, jnp.concatenate([zeros_w, u], axis=1)], axis=0)
             for x, u in zip(pq, v_p)]
    bkh_p = [jnp.concatenate([b_h[bi][:, ps], k_h[bi][:, ps]], axis=0) for bi, ps in ent]
    mn_p = _dots(bkh_p, pqv_p, _TN)
    sq_r = lax.broadcasted_iota(jnp.int32, (pw, pw), 0)
    sq_c = lax.broadcasted_iota(jnp.int32, (pw, pw), 1)
    same_head = (sq_r < n) == (sq_c < n)
    lhs_p = [jnp.concatenate([r_t[bi][:, ps] + x[:, :pw],
                              jnp.where(sq_r == sq_c, jnp.broadcast_to(w_end[bi][:, ps], (pw, pw)),
                                        jnp.where(same_head, m[:, :pw], 0.0))], axis=0)
             for (bi, ps), x, m in zip(ent, ry, mn_p)]
    st = [state_ref[e] for e in range(len(ent))]
    upd = _dots(lhs_p, st)
    for e in range(len(ent)):
        state_ref[e] = upd[e][ln:, :] + jnp.where(same_head, mn_p[e][:, pw:], 0.0)
    y0_p = [x[:, pw:] for x in ry]

    hsum = lambda x: jnp.where(even, jnp.sum(jnp.where(even, x, 0.0), axis=-1, keepdims=True),
                               jnp.sum(jnp.where(even, 0.0, x), axis=-1, keepdims=True))
    inv_n = 1.0 / n
    yn = []
    for u, y0 in zip(upd, y0_p):
        y = u[:ln, :] + y0
        d = y - hsum(y) * inv_n
        yn.append(d * lax.rsqrt(hsum(d * d) * inv_n + LNX_EPS))
    for bi in range(nb):
        rk = r[bi] * k[bi] * rk_ref[...]
        mine = range(bi * npair, (bi + 1) * npair)
        bonus = jnp.concatenate([hsum(rk[:, ent[e][1]]) * v_p[e] for e in mine], axis=1)
        out = jnp.concatenate([yn[e] for e in mine], axis=1) * lg_ref[...] + lb_ref[...] + bonus
        y_ref[bi] = (out * g_ref[bi].astype(F32)).astype(y_ref.dtype)


def _wkv(r, lw, k, v, an, b, g, r_k, lnx_g, lnx_b, casts=(), heads_per_step=16):
    bsz, s, c = r.shape
    wb = heads_per_step * RWKV_HEAD
    nc = s // WKV_CHUNK
    assert s % WKV_CHUNK == 0 and c % wb == 0 and heads_per_step % 2 == 0 and WKV_CHUNK == RWKV_HEAD
    assert not casts or c == wb
    spec = pl.BlockSpec((bsz, WKV_CHUNK, wb), lambda hi, ci: (0, ci, hi))
    pspec = pl.BlockSpec((1, wb), lambda hi, ci: (0, hi))
    row = lambda a: a.reshape(1, -1)
    cast_ins, cast_in_specs, cast_out_specs, cast_out_shape = [], [], [], []
    for w, layer in casts:
        nl, rows, cols = w.shape
        assert rows % (16 * nc) == 0
        slab = rows // nc
        cast_ins.append(w.reshape(nl, nc, slab, cols))
        cast_in_specs.append(pl.BlockSpec((None, None, slab, cols), lambda hi, ci, layer=layer: (layer, ci, 0, 0)))
        cast_out_specs.append(pl.BlockSpec((None, slab, cols), lambda hi, ci: (ci, 0, 0)))
        cast_out_shape.append(jax.ShapeDtypeStruct((nc, slab, cols), BF16))
    out = pl.pallas_call(
        functools.partial(_wkv_kernel, len(casts)),
        grid=(c // wb, nc),
        in_specs=[spec] * 7 + [pspec] * 3 + cast_in_specs,
        out_specs=[spec] + cast_out_specs,
        out_shape=[jax.ShapeDtypeStruct((bsz, s, c), BF16)] + cast_out_shape,
        scratch_shapes=[pltpu.VMEM((bsz * heads_per_step // 2, 2 * RWKV_HEAD, 2 * RWKV_HEAD), F32)],
        compiler_params=_cparams("parallel", "arbitrary"),
        name="wkv",
    )(r, lw, k, v, an, b, g, row(r_k), row(lnx_g), row(lnx_b), *cast_ins)
    return out[0], [o.reshape(w.shape[1], w.shape[2]) for o, (w, _) in zip(out[1:], casts)]


def _rope_kernel(q_ref, k_ref, v_ref, cos_ref, sin_ref, qs_ref, ko_ref, vt_ref, bias_ref, km_ref):
    i = pl.program_id(1)
    nb = km_ref.shape[0]
    blk = q_ref.shape[1]
    nh = q_ref.shape[-1] // ATTN_HEAD
    cos = cos_ref[...]
    sin = sin_ref[...]

    @pl.when(i == 0)
    def _():
        km_ref[...] = jnp.zeros_like(km_ref)

    blk_id = lax.broadcasted_iota(jnp.int32, (nb, blk), 0)
    past = blk_id < i
    qk_scale = (ATTN_HEAD ** -0.5) * LOG2E
    for h in range(nh):
        sl = slice(h * ATTN_HEAD, (h + 1) * ATTN_HEAD)
        q = q_ref[0, :, sl].astype(F32)
        k = k_ref[0, :, sl].astype(F32)
        qr = (q * cos + pltpu.roll(q, ATTN_HEAD // 2, axis=1) * sin).T
        kr = k * cos + pltpu.roll(k, ATTN_HEAD // 2, axis=1) * sin
        qs_ref[0, h, 0] = (qr * qk_scale).astype(BF16)
        ko_ref[0, h, 0] = kr.astype(BF16)
        vt_ref[0, h, 0] = v_ref[0, :, sl].astype(F32).T.astype(BF16)
        gate = jnp.where(past, _mm(km_ref[:, sl], qr, passes=3), -jnp.inf)
        rank = jnp.zeros((nb, blk), jnp.int32)
        for m in range(nb):
            gm = gate[m:m + 1, :]
            rank += ((gm > gate) | ((gm == gate) & (m < blk_id))).astype(jnp.int32)
        bias_ref[0, h, 0] = jnp.where(past & (rank < MOBA_TOPK), 0.0, -jnp.inf)
        km_new = jnp.sum(kr, axis=0, keepdims=True) * (1.0 / blk)
        km_ref[:, sl] = jnp.where(lax.broadcasted_iota(jnp.int32, (nb, ATTN_HEAD), 0) == i, km_new, km_ref[:, sl])


def _rope(z3, col0, width, cos2, sin2):
    bsz, s, _ = z3.shape
    assert s % MOBA_BLOCK == 0 and col0 % ATTN_HEAD == 0
    nb = s // MOBA_BLOCK
    nh = width // ATTN_HEAD
    blk = lambda j: pl.BlockSpec((pl.Element(1), pl.Element(MOBA_BLOCK), pl.Element(width)),
                                 lambda b, i: (b, i * MOBA_BLOCK, col0 + j * width))
    tab = pl.BlockSpec((MOBA_BLOCK, ATTN_HEAD), lambda b, i: (i, 0))
    t_spec = pl.BlockSpec((1, nh, 1, ATTN_HEAD, MOBA_BLOCK), lambda b, i: (b, 0, i, 0, 0))
    n_spec = pl.BlockSpec((1, nh, 1, MOBA_BLOCK, ATTN_HEAD), lambda b, i: (b, 0, i, 0, 0))
    return pl.pallas_call(
        _rope_kernel,
        grid=(bsz, nb),
        in_specs=[blk(0), blk(1), blk(2), tab, tab],
        out_specs=[t_spec, n_spec, t_spec,
                   pl.BlockSpec((1, nh, 1, nb, MOBA_BLOCK), lambda b, i: (b, 0, i, 0, 0))],
        out_shape=[jax.ShapeDtypeStruct((bsz, nh, nb, ATTN_HEAD, MOBA_BLOCK), BF16),
                   jax.ShapeDtypeStruct((bsz, nh, nb, MOBA_BLOCK, ATTN_HEAD), BF16),
                   jax.ShapeDtypeStruct((bsz, nh, nb, ATTN_HEAD, MOBA_BLOCK), BF16),
                   jax.ShapeDtypeStruct((bsz, nh, nb, nb, MOBA_BLOCK), F32)],
        scratch_shapes=[pltpu.VMEM((nb, width), F32)],
        compiler_params=_cparams("parallel", "arbitrary"),
        name="rope",
    )(z3, z3, z3, cos2, sin2)


def _moba_kernel(qs_ref, k_ref, vt_ref, bias_ref, o_ref):
    blk = MOBA_BLOCK
    dh = ATTN_HEAD
    hs = range(qs_ref.shape[1])
    qb = pl.program_id(2)
    neg = -jnp.inf
    qs = [qs_ref[0, h, 0] for h in hs]
    ki = lax.broadcasted_iota(jnp.int32, (blk, blk), 0)
    qi = lax.broadcasted_iota(jnp.int32, (blk, blk), 1)
    causal = ki <= qi

    def pipelined(work, stage):
        scores = lambda kb, h: jnp.dot(k_ref[0, h, kb], qs[h], preferred_element_type=F32)
        ahead = [scores(*w) for w in work[:MOBA_LOOKAHEAD]]
        for i, (kb, h) in enumerate(work):
            if i + MOBA_LOOKAHEAD < len(work):
                ahead.append(scores(*work[i + MOBA_LOOKAHEAD]))
            stage(kb, h, ahead[i])

    m_run, l_run, acc = [None] * len(hs), [None] * len(hs), [None] * len(hs)

    def own_block(kb, h, s):
        s = jnp.where(causal, s, neg)
        m_run[h] = jnp.max(s, axis=0, keepdims=True)
        p = jnp.exp2(s - m_run[h])
        l_run[h] = jnp.sum(p, axis=0, keepdims=True)
        acc[h] = jnp.dot(vt_ref[0, h, kb], p.astype(BF16), preferred_element_type=F32)

    pipelined([(qb, h) for h in hs], own_block)

    def past_blocks(kbs, carry):
        m_c, l_c, acc_c = (list(c) for c in carry)

        def stage(kb, h, s):
            b = bias_ref[0, h, 0, pl.ds(kb, 1), :]
            m_new = jnp.where(b == 0.0, jnp.maximum(m_c[h], jnp.max(s, axis=0, keepdims=True)), m_c[h])
            alpha = jnp.exp2(m_c[h] - m_new)
            p = jnp.exp2(s - (m_new - b))
            pv = jnp.dot(vt_ref[0, h, kb], p.astype(BF16), preferred_element_type=F32)
            m_c[h] = m_new
            l_c[h] = alpha * l_c[h] + jnp.sum(p, axis=0, keepdims=True)
            acc_c[h] = alpha * acc_c[h] + pv

        pipelined([(kb, h) for kb in kbs for h in hs], stage)
        return tuple(m_c), tuple(l_c), tuple(acc_c)

    u = MOBA_UNROLL
    carry = lax.fori_loop(0, qb // u, lambda j, c: past_blocks([j * u + i for i in range(u)], c),
                          (tuple(m_run), tuple(l_run), tuple(acc)))
    _, l_fin, acc = lax.fori_loop((qb // u) * u, qb, lambda kb, c: past_blocks([kb], c), carry)
    for h in hs:
        o_ref[0, :, h * dh:(h + 1) * dh] = (acc[h] / l_fin[h]).T.astype(o_ref.dtype)


def _moba(qs, k, vt, bias, heads_per_step=8):
    bsz, nh, nb, dh, blk = qs.shape
    hp = heads_per_step
    assert nh % hp == 0
    return pl.pallas_call(
        _moba_kernel,
        grid=(bsz, nh // hp, nb),
        in_specs=[pl.BlockSpec((1, hp, 1, dh, blk), lambda b, h, i: (b, h, i, 0, 0)),
                  pl.BlockSpec((1, hp, nb, blk, dh), lambda b, h, i: (b, h, 0, 0, 0)),
                  pl.BlockSpec((1, hp, nb, dh, blk), lambda b, h, i: (b, h, 0, 0, 0)),
                  pl.BlockSpec((1, hp, 1, nb, blk), lambda b, h, i: (b, h, i, 0, 0))],
        out_specs=pl.BlockSpec((1, blk, hp * dh), lambda b, h, i: (b, i, h)),
        out_shape=jax.ShapeDtypeStruct((bsz, nb * blk, nh * dh), BF16),
        compiler_params=_cparams("parallel", "parallel", "arbitrary"),
        name="moba",
    )(qs, k, vt, bias)


def _out_proj_kernel(yr_ref, ya_ref, wr_ref, wa_ref, x_ref, g_ref, o_ref):
    for r in range(0, x_ref.shape[0], ROW_CHUNK):
        rows = pl.ds(r, ROW_CHUNK)
        y = jnp.dot(yr_ref[rows, :], wr_ref[...], preferred_element_type=F32)
        y += jnp.dot(ya_ref[rows, :], wa_ref[...], preferred_element_type=F32)
        ms = jnp.mean(y * y, axis=-1, keepdims=True)
        o_ref[rows, :] = x_ref[rows, :] + y * lax.rsqrt(ms + NORM_EPS) * g_ref[...]


def _out_proj(y_r, y_a, w, x2, gain, tm=512):
    m, d = x2.shape
    cw = y_r.shape[1]
    assert y_a.shape[1] == cw and w.shape[0] == 2 * cw and m % tm == 0
    return pl.pallas_call(
        _out_proj_kernel,
        grid=(m // tm,),
        in_specs=[pl.BlockSpec((tm, cw), lambda i: (i, 0)), pl.BlockSpec((tm, cw), lambda i: (i, 0)),
                  pl.BlockSpec((cw, d), lambda i: (0, 0)), pl.BlockSpec((cw, d), lambda i: (1, 0)),
                  pl.BlockSpec((tm, d), lambda i: (i, 0)), pl.BlockSpec((1, d), lambda i: (0, 0))],
        out_specs=pl.BlockSpec((tm, d), lambda i: (i, 0)),
        out_shape=jax.ShapeDtypeStruct((m, d), F32),
        compiler_params=_cparams("parallel"),
        name="out_proj",
    )(y_r, y_a, w, w, x2, gain.reshape(1, d))


def _mlp_kernel(x_ref, gpre_ref, wu_ref, wd_ref, gpost_ref, o_ref, h_ref, acc_ref):
    f = pl.program_id(1)
    last = pl.num_programs(1) - 1
    chunks = [pl.ds(r, ROW_CHUNK) for r in range(0, x_ref.shape[0], ROW_CHUNK)]

    def part(h):
        u = jnp.maximum(jnp.dot(h, wu_ref[...], preferred_element_type=F32), 0.0)
        return jnp.dot((u * u).astype(BF16), wd_ref[...], preferred_element_type=F32)

    @pl.when(f == 0)
    def _():
        for rows in chunks:
            x = x_ref[rows, :]
            ms = jnp.mean(x * x, axis=-1, keepdims=True)
            h = (x * lax.rsqrt(ms + NORM_EPS) * gpre_ref[...]).astype(BF16)
            h_ref[rows, :] = h
            acc_ref[rows, :] = part(h)

    @pl.when((f > 0) & (f < last))
    def _():
        acc_ref[...] += part(h_ref[...])

    @pl.when(f == last)
    def _():
        for rows in chunks:
            mlp = acc_ref[rows, :] + part(h_ref[rows, :])
            ms = jnp.mean(mlp * mlp, axis=-1, keepdims=True)
            o_ref[rows, :] = x_ref[rows, :] + mlp * lax.rsqrt(ms + NORM_EPS) * gpost_ref[...]


def _mlp(x2, g_pre, w_up, w_down, g_post, tm=512, tf=1024):
    m, d = x2.shape
    dff = w_up.shape[1]
    assert m % tm == 0 and dff % tf == 0 and dff // tf >= 2 and tm % ROW_CHUNK == 0
    return pl.pallas_call(
        _mlp_kernel,
        grid=(m // tm, dff // tf),
        in_specs=[pl.BlockSpec((tm, d), lambda i, f: (i, 0)),
                  pl.BlockSpec((1, d), lambda i, f: (0, 0)),
                  pl.BlockSpec((d, tf), lambda i, f: (0, f)),
                  pl.BlockSpec((tf, d), lambda i, f: (f, 0)),
                  pl.BlockSpec((1, d), lambda i, f: (0, 0))],
        out_specs=pl.BlockSpec((tm, d), lambda i, f: (i, 0)),
        out_shape=jax.ShapeDtypeStruct((m, d), F32),
        scratch_shapes=[pltpu.VMEM((tm, d), BF16), pltpu.VMEM((tm, d), F32)],
        compiler_params=_cparams("parallel", "arbitrary"),
        name="mlp",
    )(x2, g_pre.reshape(1, d), w_up, w_down, g_post.reshape(1, d))


def _rope_tables(s):
    half = ATTN_HEAD // 2
    inv_freq = ROPE_THETA ** (-jnp.arange(half, dtype=F32) / half)
    ang = jnp.arange(s).astype(F32)[:, None] * inv_freq[None, :]
    cos, sin = jnp.cos(ang), jnp.sin(ang)
    return jnp.concatenate([cos, cos], axis=-1), jnp.concatenate([-sin, sin], axis=-1)


def kernel(x, norm_mix_pre, norm_mix_post, norm_mlp_pre, norm_mlp_post, w_in, w_in_vres, shift_mu, shift_mu_vres, decay_w0, decay_w2, iclr_a0, iclr_a2, vres_v0, vres_v2, gate_g2, k_k, k_a, r_k, lnx_gain, lnx_bias, w_out, w_up, w_down):
    bsz, s, d = x.shape
    depth = w_in.shape[0]
    c = decay_w0.shape[1]
    n_lora = DECAY_LORA + ICLR_LORA + GATE_LORA
    n_shift = 3 * c + n_lora
    ca = (w_in.shape[2] - n_shift) // 3
    cos2, sin2 = _rope_tables(s)
    w_in16 = w_in[0].astype(BF16)
    pad_v = VRES_PAD - VRES_LORA
    x2 = x.reshape(bsz * s, d)
    v_first = None
    for i in range(depth):
        if i == 0:
            z, vres = _norm_matmul(x2, norm_mix_pre[i], w_in16)[0], None
        else:
            w_v = jnp.pad(w_in_vres[i - 1], ((0, 0), (0, pad_v))).astype(BF16)
            z, z_v = _norm_matmul(x2, norm_mix_pre[i], w_in16, w_v)
            vres = (z_v.reshape(bsz, s, VRES_PAD), jnp.pad(shift_mu_vres[i - 1], (0, pad_v)), v_first,
                    vres_v0[i - 1], jnp.pad(vres_v2[i - 1], ((0, pad_v), (0, 0))))
        z3 = z.reshape(bsz, s, -1)

        prep = _rwkv_prep(z3, c, shift_mu[i], decay_w0[i], decay_w2[i], iclr_a0[i], iclr_a2[i], gate_g2[i],
                          k_k[i], k_a[i], vres)
        if i == 0:
            v_first = prep[7]
        casts = [(w_out, i), (w_up, i), (w_down, i)] + ([(w_in, i + 1)] if i + 1 < depth else [])
        y_r, w16 = _wkv(*prep[:7], r_k[i].reshape(-1), lnx_gain[i], lnx_bias[i], casts)
        w_out16, w_up16, w_down16 = w16[:3]

        y_a = _moba(*_rope(z3, n_shift, ca, cos2, sin2))

        x2 = _out_proj(y_r.reshape(bsz * s, c), y_a.reshape(bsz * s, ca), w_out16, x2, norm_mix_post[i])
        x2 = _mlp(x2, norm_mlp_pre[i], w_up16, w_down16, norm_mlp_post[i])
        if i + 1 < depth:
            w_in16 = w16[3]
    return x2.reshape(bsz, s, d)
```

```python
import functools

import jax
import jax.numpy as jnp
from jax import lax
from jax.experimental import pallas as pl
from jax.experimental.pallas import tpu as pltpu

F32 = jnp.float32
BF16 = jnp.bfloat16

RWKV_HEAD = 64
DECAY_LORA = 64
ICLR_LORA = 64
VRES_LORA = 32
GATE_LORA = 128
ATTN_HEAD = 128
MOBA_BLOCK = 256
MOBA_TOPK = 3
ROPE_THETA = 10000.0
NORM_EPS = 1e-6
LNX_EPS = 64e-5
LOG2E = 1.4426950408889634

LANES = 128
BF16_SUBLANES = 16
VRES_PAD = LANES
WKV_CHUNK = 64
MOBA_LOOKAHEAD = 6
MOBA_UNROLL = 4
HALO_ROWS = BF16_SUBLANES
ROW_CHUNK = 256
VMEM_LIMIT = 56 * 1024 * 1024


def _cparams(*sem):
    return pltpu.CompilerParams(dimension_semantics=sem, vmem_limit_bytes=VMEM_LIMIT)


_NN = (((1,), (0,)), ((), ()))
_NT = (((1,), (1,)), ((), ()))
_TN = (((0,), (0,)), ((), ()))


def _split2(x):
    hi = x.astype(BF16)
    lo = (x - hi.astype(F32)).astype(BF16)
    return hi, lo


def _split3(x):
    hi = x.astype(BF16)
    r1 = x - hi.astype(F32)
    mid = r1.astype(BF16)
    lo = (r1 - mid.astype(F32)).astype(BF16)
    return hi, mid, lo


def _mm(a, b, dims=_NN, passes=1):
    d = lambda p, q: lax.dot_general(p, q, dims, preferred_element_type=F32)
    if passes == 1:
        return d(a.astype(BF16), b.astype(BF16))
    ah, al = _split2(a)
    bh, bl = _split2(b)
    return d(ah, bh) + (d(ah, bl) + d(al, bh))


def _sigmoid(x):
    return 1.0 / (1.0 + jnp.exp(-x))


def _softplus(x):
    return jnp.maximum(x, 0.0) + jnp.log(1.0 + jnp.exp(-jnp.abs(x)))


def _norm_matmul_kernel(has_extra, *refs):
    if has_extra:
        x_ref, g_ref, w_ref, we_ref, o_ref, oe_ref, h_ref = refs
    else:
        x_ref, g_ref, w_ref, o_ref, h_ref = refs

    first = pl.program_id(1) == 0

    @pl.when(first)
    def _():
        for r in range(0, x_ref.shape[0], ROW_CHUNK):
            rows = pl.ds(r, ROW_CHUNK)
            x = x_ref[rows, :]
            ms = jnp.mean(x * x, axis=-1, keepdims=True)
            h = (x * lax.rsqrt(ms + NORM_EPS) * g_ref[...]).astype(BF16)
            h_ref[rows, :] = h
            o_ref[rows, :] = jnp.dot(h, w_ref[...], preferred_element_type=F32).astype(o_ref.dtype)
            if has_extra:
                oe_ref[rows, :] = jnp.dot(h, we_ref[...], preferred_element_type=F32).astype(oe_ref.dtype)

    @pl.when(jnp.logical_not(first))
    def _():
        o_ref[...] = jnp.dot(h_ref[...], w_ref[...], preferred_element_type=F32).astype(o_ref.dtype)


def _norm_matmul(x2, gain, w, w_extra=None, tm=1024, tn=1280):
    m, d = x2.shape
    n = w.shape[1]
    assert m % tm == 0 and n % tn == 0
    has_extra = w_extra is not None
    ins = [x2, gain.reshape(1, d), w]
    in_specs = [pl.BlockSpec((tm, d), lambda i, j: (i, 0)),
                pl.BlockSpec((1, d), lambda i, j: (0, 0)),
                pl.BlockSpec((d, tn), lambda i, j: (0, j))]
    out_specs = [pl.BlockSpec((tm, tn), lambda i, j: (i, j))]
    out_shape = [jax.ShapeDtypeStruct((m, n), BF16)]
    if has_extra:
        ne = w_extra.shape[1]
        ins.append(w_extra)
        in_specs.append(pl.BlockSpec((d, ne), lambda i, j: (0, 0)))
        out_specs.append(pl.BlockSpec((tm, ne), lambda i, j: (i, 0)))
        out_shape.append(jax.ShapeDtypeStruct((m, ne), BF16))
    return pl.pallas_call(
        functools.partial(_norm_matmul_kernel, has_extra),
        grid=(m // tm, n // tn),
        in_specs=in_specs,
        out_specs=out_specs,
        out_shape=out_shape,
        scratch_shapes=[pltpu.VMEM((tm, d), BF16)],
        compiler_params=_cparams("parallel", "arbitrary"),
        name="norm_matmul",
    )(*ins)


def _shifted(cur, halo, mu, first):
    cur = cur.astype(F32)
    rows = lax.broadcasted_iota(jnp.int32, cur.shape, 0)
    last = jnp.where(first, 0.0, halo[HALO_ROWS - 1:HALO_ROWS, :].astype(F32))
    prev = jnp.where(rows == 0, last, pltpu.roll(cur, 1, axis=0))
    return cur + (prev - cur) * mu


def _rwkv_prep_kernel(has_vres, *refs):
    if has_vres:
        (zm_ref, zmh_ref, zl_ref, zlh_ref, mum_ref, mul_ref, w0_ref, w2_ref, a0_ref, a2_ref, g2_ref,
         kk_ref, ka_ref, zv_ref, zvh_ref, muv_ref, vf_ref, v0_ref, v2_ref,
         r_ref, lw_ref, k_ref, v_ref, an_ref, b_ref, g_ref) = refs
    else:
        (zm_ref, zmh_ref, zl_ref, zlh_ref, mum_ref, mul_ref, w0_ref, w2_ref, a0_ref, a2_ref, g2_ref,
         kk_ref, ka_ref,
         r_ref, lw_ref, k_ref, v_ref, an_ref, b_ref, g_ref, vf32_ref) = refs
    c = r_ref.shape[-1]
    first = pl.program_id(1) == 0
    zs = _shifted(zm_ref[0], zmh_ref[0], mum_ref[...], first)
    zl = _shifted(zl_ref[0], zlh_ref[0], mul_ref[...], first)
    r, k, v = zs[:, :c], zs[:, c:2 * c], zs[:, 2 * c:3 * c]
    o = 0
    wd = zl[:, o:o + DECAY_LORA]
    o += DECAY_LORA
    ad = zl[:, o:o + ICLR_LORA]
    o += ICLR_LORA
    gd = zl[:, o:o + GATE_LORA]
    o += GATE_LORA
    w_log = -_softplus(-(w0_ref[...] + _mm(jnp.tanh(wd), w2_ref[...], passes=3))) - 0.5
    lw_ref[0] = -jnp.exp(w_log)
    a = _sigmoid(a0_ref[...] + _mm(ad, a2_ref[...]))
    g_ref[0] = _mm(_sigmoid(gd), g2_ref[...]).astype(g_ref.dtype)
    if has_vres:
        vd = _shifted(zv_ref[0], zvh_ref[0], muv_ref[...], first)
        v = v + (vf_ref[0] - v) * _sigmoid(v0_ref[...] + _mm(vd, v2_ref[...]))
    else:
        vf32_ref[0] = v
    kk = k * kk_ref[...]
    pw = 2 * RWKV_HEAD
    even = lax.broadcasted_iota(jnp.int32, (kk.shape[0], pw), 1) < RWKV_HEAD
    unit = []
    for p in range(c // pw):
        x = kk[:, p * pw:(p + 1) * pw]
        sq = x * x
        ss = jnp.where(even, jnp.sum(jnp.where(even, sq, 0.0), axis=-1, keepdims=True),
                       jnp.sum(jnp.where(even, 0.0, sq), axis=-1, keepdims=True))
        unit.append(x / jnp.maximum(jnp.sqrt(ss), 1e-12))
    kk = jnp.concatenate(unit, axis=1)
    r_ref[0] = r.astype(r_ref.dtype)
    k_ref[0] = (k * (1.0 + (a - 1.0) * ka_ref[...])).astype(k_ref.dtype)
    v_ref[0] = v.astype(v_ref.dtype)
    an_ref[0] = (-kk).astype(an_ref.dtype)
    b_ref[0] = (kk * a).astype(b_ref.dtype)


def _rwkv_prep(z3, c, mu, w0, w2, a0, a2, g2, k_k, k_a, vres, tm=512):
    bsz, s, _ = z3.shape
    n_lora = w2.shape[0] + a2.shape[0] + g2.shape[0]
    assert s % tm == 0 and (3 * c) % n_lora == 0 and n_lora % LANES == 0
    has_vres = vres is not None
    row = lambda a: a.reshape(1, -1)
    hb = tm // HALO_ROWS
    lcb = 3 * c // n_lora
    halo = lambda b, i: (b, jnp.maximum(i * hb - 1, 0), 0)
    halo_l = lambda b, i: (b, jnp.maximum(i * hb - 1, 0), lcb)
    full = lambda a: pl.BlockSpec(a.shape, lambda b, i: (0,) * a.ndim)
    ins = [z3, z3, z3, z3, row(mu[:3 * c]), row(mu[3 * c:]), row(w0), w2, row(a0), a2, g2, row(k_k), row(k_a)]
    in_specs = [pl.BlockSpec((1, tm, 3 * c), lambda b, i: (b, i, 0)),
                pl.BlockSpec((1, HALO_ROWS, 3 * c), halo),
                pl.BlockSpec((1, tm, n_lora), lambda b, i: (b, i, lcb)),
                pl.BlockSpec((1, HALO_ROWS, n_lora), halo_l)] + [full(a) for a in ins[4:]]
    if has_vres:
        zv3, mu_v, v_first, v0, v2 = vres
        nv = zv3.shape[-1]
        extra = [zv3, zv3, row(mu_v), v_first, row(v0), v2]
        ins += extra
        in_specs += [pl.BlockSpec((1, tm, nv), lambda b, i: (b, i, 0)), pl.BlockSpec((1, HALO_ROWS, nv), halo),
                     full(extra[2]), pl.BlockSpec((1, tm, c), lambda b, i: (b, i, 0)), full(extra[4]),
                     full(extra[5])]
    out_spec = pl.BlockSpec((1, tm, c), lambda b, i: (b, i, 0))
    sd = lambda dt: jax.ShapeDtypeStruct((bsz, s, c), dt)
    out_dtypes = [BF16, F32, BF16, BF16, BF16, BF16, BF16] + ([] if has_vres else [F32])
    return pl.pallas_call(
        functools.partial(_rwkv_prep_kernel, has_vres),
        grid=(bsz, s // tm),
        in_specs=in_specs,
        out_specs=[out_spec] * len(out_dtypes),
        out_shape=[sd(dt) for dt in out_dtypes],
        compiler_params=_cparams("parallel", "arbitrary"),
        name="rwkv_prep",
    )(*ins)


def _dots(a_list, b_list, dims=_NN):
    return [lax.dot_general(a.astype(BF16), b.astype(BF16), dims, preferred_element_type=F32)
            for a, b in zip(a_list, b_list)]


def _pair_diag(x, even):
    return jnp.concatenate([jnp.where(even, x, 0.0), jnp.where(even, 0.0, x)], axis=0)


def _unit_lower_inverse(a_list, row_w, col_w, even):
    n = row_w.shape[0]
    lower = row_w > col_w
    base = lower & ((row_w >> 1) == (col_w >> 1))
    t = [jnp.where(row_w == col_w, 1.0, jnp.where(base, a, 0.0)) for a in a_list]
    sh = 1
    while (2 << sh) <= n:
        sub = lower & ((row_w >> (sh + 1)) == (col_w >> (sh + 1))) & ((row_w >> sh) != (col_w >> sh))
        off = [_pair_diag(jnp.where(sub, a, 0.0), even) for a in a_list]
        upd = _dots(_dots(t, off), [_pair_diag(x, even) for x in t])
        t = [x + u for x, u in zip(t, upd)]
        sh += 1
    return t


def _mm_exact_rhs_left(l_bf16, a):
    d = lambda p: lax.dot_general(l_bf16, p, _NN, preferred_element_type=F32)
    hi, mid, lo = _split3(a)
    return d(hi) + (d(mid) + d(lo))


def _wkv_kernel(n_cast, *refs):
    r_ref, lw_ref, k_ref, v_ref, an_ref, b_ref, g_ref, rk_ref, lg_ref, lb_ref = refs[:10]
    cast_in, y_ref = refs[10:10 + n_cast], refs[10 + n_cast]
    cast_out, state_ref = refs[11 + n_cast:11 + 2 * n_cast], refs[11 + 2 * n_cast]
    for src, dst in zip(cast_in, cast_out):
        dst[...] = src[...].astype(dst.dtype)

    nb, ln = r_ref.shape[0], r_ref.shape[1]
    n = RWKV_HEAD
    pw = 2 * n
    npair = r_ref.shape[2] // pw
    ent = [(bi, slice(p * pw, (p + 1) * pw)) for bi in range(nb) for p in range(npair)]

    @pl.when(pl.program_id(1) == 0)
    def _():
        state_ref[...] = jnp.zeros_like(state_ref)

    rows = lax.broadcasted_iota(jnp.int32, (ln, ln), 0)
    cols = lax.broadcasted_iota(jnp.int32, (ln, ln), 1)
    tril = jnp.where(rows >= cols, 1.0, 0.0).astype(BF16)
    r, k, v, w_end, r_t, a_t, b_t, k_t, b_h, k_h = ([] for _ in range(10))
    for bi in range(nb):
        lw = lw_ref[bi]
        cw = _mm_exact_rhs_left(tril, lw)
        cw_end = cw[ln - 1:ln, :]
        e_neg = jnp.exp(-cw)
        e_end = jnp.exp(cw_end - cw)
        bb = b_ref[bi].astype(F32)
        r.append(r_ref[bi].astype(F32))
        k.append(k_ref[bi].astype(F32))
        v.append(v_ref[bi].astype(F32))
        w_end.append(jnp.exp(cw_end))
        r_t.append(r[bi] * jnp.exp(cw))
        a_t.append(an_ref[bi].astype(F32) * jnp.exp(cw - lw))
        b_t.append(bb * e_neg)
        k_t.append(k[bi] * e_neg)
        b_h.append(bb * e_end)
        k_h.append(k[bi] * e_end)

    lane = lax.broadcasted_iota(jnp.int32, (ln, pw), 1)
    row_w = lax.broadcasted_iota(jnp.int32, (ln, pw), 0)
    even = lane < n
    even2 = lax.broadcasted_iota(jnp.int32, (2 * ln, pw), 1) < n
    col_w = lane & (n - 1)
    strict_w = row_w > col_w
    incl_w = row_w >= col_w
    zeros_w = jnp.zeros((ln, pw), F32)
    diag = lambda x: _pair_diag(x, even)
    swap = lambda x: jnp.concatenate([x[x.shape[0] // 2:], x[:x.shape[0] // 2]], axis=0)

    ar_p = [jnp.concatenate([a_t[bi][:, ps], r_t[bi][:, ps]], axis=0) for bi, ps in ent]
    bk_p = [jnp.concatenate([b_t[bi][:, ps], k_t[bi][:, ps]], axis=0).astype(BF16) for bi, ps in ent]
    kb_p = [jnp.concatenate([k_t[bi][:, ps], b_t[bi][:, ps]], axis=0).astype(BF16) for bi, ps in ent]
    am_e = _dots([jnp.where(even2, x, 0.0) for x in ar_p], bk_p, _NT)
    am_o = _dots([jnp.where(even2, 0.0, x) for x in ar_p], kb_p, _NT)
    a_ab = [jnp.where(strict_w, jnp.where(even, e[:ln], o[:ln]), 0.0) for e, o in zip(am_e, am_o)]
    a_ak = [jnp.where(strict_w, jnp.where(even, o[:ln], e[:ln]), 0.0) for e, o in zip(am_e, am_o)]
    a_rb = [jnp.where(incl_w, jnp.where(even, e[ln:], o[ln:]), 0.0) for e, o in zip(am_e, am_o)]
    a_rk = [jnp.where(incl_w, jnp.where(even, o[ln:], e[ln:]), 0.0) for e, o in zip(am_e, am_o)]
    v_p = [v[bi][:, ps] for bi, ps in ent]
    v_d = [diag(x) for x in v_p]
    akv = _dots(a_ak, [swap(x) for x in v_d])
    t = _unit_lower_inverse(a_ab, row_w, col_w, even)
    rhs = [jnp.concatenate([diag(a_t[bi][:, ps]), diag(x)], axis=1) for (bi, ps), x in zip(ent, akv)]
    pq = _dots(t, rhs)
    ry = _dots([jnp.concatenate([x, y], axis=1) for x, y in zip(a_rb, a_rk)],
               [jnp.concatenate([jnp.concatenate([diag(x[:, :pw]), diag(x[:, pw:])], axis=1),
                                 jnp.concatenate([jnp.zeros((2 * ln, pw), F32), swap(u)], axis=1)], axis=0)
                for x, u in zip(pq, v_d)])
    pqv_p = [jnp.concatenate([x, jnp.concatenate([zeros_w, u], axis=1)], axis=0)
             for x, u in zip(pq, v_p)]
    bkh_p = [jnp.concatenate([b_h[bi][:, ps], k_h[bi][:, ps]], axis=0) for bi, ps in ent]
    mn_p = _dots(bkh_p, pqv_p, _TN)
    sq_r = lax.broadcasted_iota(jnp.int32, (pw, pw), 0)
    sq_c = lax.broadcasted_iota(jnp.int32, (pw, pw), 1)
    same_head = (sq_r < n) == (sq_c < n)
    lhs_p = [jnp.concatenate([r_t[bi][:, ps] + x[:, :pw],
                              jnp.where(sq_r == sq_c, jnp.broadcast_to(w_end[bi][:, ps], (pw, pw)),
                                        jnp.where(same_head, m[:, :pw], 0.0))], axis=0)
             for (bi, ps), x, m in zip(ent, ry, mn_p)]
    st = [state_ref[e] for e in range(len(ent))]
    upd = _dots(lhs_p, st)
    for e in range(len(ent)):
        state_ref[e] = upd[e][ln:, :] + jnp.where(same_head, mn_p[e][:, pw:], 0.0)
    y0_p = [x[:, pw:] for x in ry]

    hsum = lambda x: jnp.where(even, jnp.sum(jnp.where(even, x, 0.0), axis=-1, keepdims=True),
                               jnp.sum(jnp.where(even, 0.0, x), axis=-1, keepdims=True))
    inv_n = 1.0 / n
    yn = []
    for u, y0 in zip(upd, y0_p):
        y = u[:ln, :] + y0
        d = y - hsum(y) * inv_n
        yn.append(d * lax.rsqrt(hsum(d * d) * inv_n + LNX_EPS))
    for bi in range(nb):
        rk = r[bi] * k[bi] * rk_ref[...]
        mine = range(bi * npair, (bi + 1) * npair)
        bonus = jnp.concatenate([hsum(rk[:, ent[e][1]]) * v_p[e] for e in mine], axis=1)
        out = jnp.concatenate([yn[e] for e in mine], axis=1) * lg_ref[...] + lb_ref[...] + bonus
        y_ref[bi] = (out * g_ref[bi].astype(F32)).astype(y_ref.dtype)


def _wkv(r, lw, k, v, an, b, g, r_k, lnx_g, lnx_b, casts=(), heads_per_step=16):
    bsz, s, c = r.shape
    wb = heads_per_step * RWKV_HEAD
    nc = s // WKV_CHUNK
    assert s % WKV_CHUNK == 0 and c % wb == 0 and heads_per_step % 2 == 0 and WKV_CHUNK == RWKV_HEAD
    assert not casts or c == wb
    spec = pl.BlockSpec((bsz, WKV_CHUNK, wb), lambda hi, ci: (0, ci, hi))
    pspec = pl.BlockSpec((1, wb), lambda hi, ci: (0, hi))
    row = lambda a: a.reshape(1, -1)
    cast_ins, cast_in_specs, cast_out_specs, cast_out_shape = [], [], [], []
    for w, layer in casts:
        nl, rows, cols = w.shape
        assert rows % (BF16_SUBLANES * nc) == 0
        slab = rows // nc
        cast_ins.append(w.reshape(nl, nc, slab, cols))
        cast_in_specs.append(pl.BlockSpec((None, None, slab, cols), lambda hi, ci, layer=layer: (layer, ci, 0, 0)))
        cast_out_specs.append(pl.BlockSpec((None, slab, cols), lambda hi, ci: (ci, 0, 0)))
        cast_out_shape.append(jax.ShapeDtypeStruct((nc, slab, cols), BF16))
    out = pl.pallas_call(
        functools.partial(_wkv_kernel, len(casts)),
        grid=(c // wb, nc),
        in_specs=[spec] * 7 + [pspec] * 3 + cast_in_specs,
        out_specs=[spec] + cast_out_specs,
        out_shape=[jax.ShapeDtypeStruct((bsz, s, c), BF16)] + cast_out_shape,
        scratch_shapes=[pltpu.VMEM((bsz * heads_per_step // 2, 2 * RWKV_HEAD, 2 * RWKV_HEAD), F32)],
        compiler_params=_cparams("parallel", "arbitrary"),
        name="wkv",
    )(r, lw, k, v, an, b, g, row(r_k), row(lnx_g), row(lnx_b), *cast_ins)
    return out[0], [o.reshape(w.shape[1], w.shape[2]) for o, (w, _) in zip(out[1:], casts)]


def _rope_kernel(q_ref, k_ref, v_ref, cos_ref, sin_ref, qs_ref, ko_ref, vt_ref, bias_ref, km_ref):
    i = pl.program_id(1)
    nb = km_ref.shape[0]
    blk = q_ref.shape[1]
    nh = q_ref.shape[-1] // ATTN_HEAD
    cos = cos_ref[...]
    sin = sin_ref[...]

    @pl.when(i == 0)
    def _():
        km_ref[...] = jnp.zeros_like(km_ref)

    blk_id = lax.broadcasted_iota(jnp.int32, (nb, blk), 0)
    past = blk_id < i
    qk_scale = (ATTN_HEAD ** -0.5) * LOG2E
    for h in range(nh):
        sl = slice(h * ATTN_HEAD, (h + 1) * ATTN_HEAD)
        q = q_ref[0, :, sl].astype(F32)
        k = k_ref[0, :, sl].astype(F32)
        qr = (q * cos + pltpu.roll(q, ATTN_HEAD // 2, axis=1) * sin).T
        kr = k * cos + pltpu.roll(k, ATTN_HEAD // 2, axis=1) * sin
        qs_ref[0, h, 0] = (qr * qk_scale).astype(BF16)
        ko_ref[0, h, 0] = kr.astype(BF16)
        vt_ref[0, h, 0] = v_ref[0, :, sl].astype(F32).T.astype(BF16)
        gate = jnp.where(past, _mm(km_ref[:, sl], qr, passes=3), -jnp.inf)
        rank = jnp.zeros((nb, blk), jnp.int32)
        for m in range(nb):
            gm = gate[m:m + 1, :]
            rank += ((gm > gate) | ((gm == gate) & (m < blk_id))).astype(jnp.int32)
        bias_ref[0, h, 0] = jnp.where(past & (rank < MOBA_TOPK), 0.0, -jnp.inf)
        km_new = jnp.sum(kr, axis=0, keepdims=True) * (1.0 / blk)
        km_ref[:, sl] = jnp.where(lax.broadcasted_iota(jnp.int32, (nb, ATTN_HEAD), 0) == i, km_new, km_ref[:, sl])


def _rope(z3, col0, width, cos2, sin2):
    bsz, s, _ = z3.shape
    assert s % MOBA_BLOCK == 0 and col0 % ATTN_HEAD == 0
    nb = s // MOBA_BLOCK
    nh = width // ATTN_HEAD
    blk = lambda j: pl.BlockSpec((pl.Element(1), pl.Element(MOBA_BLOCK), pl.Element(width)),
                                 lambda b, i: (b, i * MOBA_BLOCK, col0 + j * width))
    tab = pl.BlockSpec((MOBA_BLOCK, ATTN_HEAD), lambda b, i: (i, 0))
    t_spec = pl.BlockSpec((1, nh, 1, ATTN_HEAD, MOBA_BLOCK), lambda b, i: (b, 0, i, 0, 0))
    n_spec = pl.BlockSpec((1, nh, 1, MOBA_BLOCK, ATTN_HEAD), lambda b, i: (b, 0, i, 0, 0))
    return pl.pallas_call(
        _rope_kernel,
        grid=(bsz, nb),
        in_specs=[blk(0), blk(1), blk(2), tab, tab],
        out_specs=[t_spec, n_spec, t_spec,
                   pl.BlockSpec((1, nh, 1, nb, MOBA_BLOCK), lambda b, i: (b, 0, i, 0, 0))],
        out_shape=[jax.ShapeDtypeStruct((bsz, nh, nb, ATTN_HEAD, MOBA_BLOCK), BF16),
                   jax.ShapeDtypeStruct((bsz, nh, nb, MOBA_BLOCK, ATTN_HEAD), BF16),
                   jax.ShapeDtypeStruct((bsz, nh, nb, ATTN_HEAD, MOBA_BLOCK), BF16),
                   jax.ShapeDtypeStruct((bsz, nh, nb, nb, MOBA_BLOCK), F32)],
        scratch_shapes=[pltpu.VMEM((nb, width), F32)],
        compiler_params=_cparams("parallel", "arbitrary"),
        name="rope",
    )(z3, z3, z3, cos2, sin2)


def _moba_kernel(qs_ref, k_ref, vt_ref, bias_ref, o_ref):
    blk = MOBA_BLOCK
    dh = ATTN_HEAD
    hs = range(qs_ref.shape[1])
    qb = pl.program_id(2)
    neg = -jnp.inf
    qs = [qs_ref[0, h, 0] for h in hs]
    ki = lax.broadcasted_iota(jnp.int32, (blk, blk), 0)
    qi = lax.broadcasted_iota(jnp.int32, (blk, blk), 1)
    causal = ki <= qi

    def pipelined(work, stage):
        scores = lambda kb, h: jnp.dot(k_ref[0, h, kb], qs[h], preferred_element_type=F32)
        ahead = [scores(*w) for w in work[:MOBA_LOOKAHEAD]]
        for i, (kb, h) in enumerate(work):
            if i + MOBA_LOOKAHEAD < len(work):
                ahead.append(scores(*work[i + MOBA_LOOKAHEAD]))
            stage(kb, h, ahead[i])

    m_run, l_run, acc = [None] * len(hs), [None] * len(hs), [None] * len(hs)

    def own_block(kb, h, s):
        s = jnp.where(causal, s, neg)
        m_run[h] = jnp.max(s, axis=0, keepdims=True)
        p = jnp.exp2(s - m_run[h])
        l_run[h] = jnp.sum(p, axis=0, keepdims=True)
        acc[h] = jnp.dot(vt_ref[0, h, kb], p.astype(BF16), preferred_element_type=F32)

    pipelined([(qb, h) for h in hs], own_block)

    def past_blocks(kbs, carry):
        m_c, l_c, acc_c = (list(c) for c in carry)

        def stage(kb, h, s):
            b = bias_ref[0, h, 0, pl.ds(kb, 1), :]
            m_new = jnp.where(b == 0.0, jnp.maximum(m_c[h], jnp.max(s, axis=0, keepdims=True)), m_c[h])
            alpha = jnp.exp2(m_c[h] - m_new)
            p = jnp.exp2(s - (m_new - b))
            pv = jnp.dot(vt_ref[0, h, kb], p.astype(BF16), preferred_element_type=F32)
            m_c[h] = m_new
            l_c[h] = alpha * l_c[h] + jnp.sum(p, axis=0, keepdims=True)
            acc_c[h] = alpha * acc_c[h] + pv

        pipelined([(kb, h) for kb in kbs for h in hs], stage)
        return tuple(m_c), tuple(l_c), tuple(acc_c)

    u = MOBA_UNROLL
    carry = lax.fori_loop(0, qb // u, lambda j, c: past_blocks([j * u + i for i in range(u)], c),
                          (tuple(m_run), tuple(l_run), tuple(acc)))
    _, l_fin, acc = lax.fori_loop((qb // u) * u, qb, lambda kb, c: past_blocks([kb], c), carry)
    for h in hs:
        o_ref[0, :, h * dh:(h + 1) * dh] = (acc[h] / l_fin[h]).T.astype(o_ref.dtype)


def _moba(qs, k, vt, bias, heads_per_step=8):
    bsz, nh, nb, dh, blk = qs.shape
    hp = heads_per_step
    assert nh % hp == 0
    return pl.pallas_call(
        _moba_kernel,
        grid=(bsz, nh // hp, nb),
        in_specs=[pl.BlockSpec((1, hp, 1, dh, blk), lambda b, h, i: (b, h, i, 0, 0)),
                  pl.BlockSpec((1, hp, nb, blk, dh), lambda b, h, i: (b, h, 0, 0, 0)),
                  pl.BlockSpec((1, hp, nb, dh, blk), lambda b, h, i: (b, h, 0, 0, 0)),
                  pl.BlockSpec((1, hp, 1, nb, blk), lambda b, h, i: (b, h, i, 0, 0))],
        out_specs=pl.BlockSpec((1, blk, hp * dh), lambda b, h, i: (b, i, h)),
        out_shape=jax.ShapeDtypeStruct((bsz, nb * blk, nh * dh), BF16),
        compiler_params=_cparams("parallel", "parallel", "arbitrary"),
        name="moba",
    )(qs, k, vt, bias)


def _out_proj_kernel(yr_ref, ya_ref, wr_ref, wa_ref, x_ref, g_ref, o_ref):
    for r in range(0, x_ref.shape[0], ROW_CHUNK):
        rows = pl.ds(r, ROW_CHUNK)
        y = jnp.dot(yr_ref[rows, :], wr_ref[...], preferred_element_type=F32)
        y += jnp.dot(ya_ref[rows, :], wa_ref[...], preferred_element_type=F32)
        ms = jnp.mean(y * y, axis=-1, keepdims=True)
        o_ref[rows, :] = x_ref[rows, :] + y * lax.rsqrt(ms + NORM_EPS) * g_ref[...]


def _out_proj(y_r, y_a, w, x2, gain, tm=512):
    m, d = x2.shape
    cw = y_r.shape[1]
    assert y_a.shape[1] == cw and w.shape[0] == 2 * cw and m % tm == 0
    return pl.pallas_call(
        _out_proj_kernel,
        grid=(m // tm,),
        in_specs=[pl.BlockSpec((tm, cw), lambda i: (i, 0)), pl.BlockSpec((tm, cw), lambda i: (i, 0)),
                  pl.BlockSpec((cw, d), lambda i: (0, 0)), pl.BlockSpec((cw, d), lambda i: (1, 0)),
                  pl.BlockSpec((tm, d), lambda i: (i, 0)), pl.BlockSpec((1, d), lambda i: (0, 0))],
        out_specs=pl.BlockSpec((tm, d), lambda i: (i, 0)),
        out_shape=jax.ShapeDtypeStruct((m, d), F32),
        compiler_params=_cparams("parallel"),
        name="out_proj",
    )(y_r, y_a, w, w, x2, gain.reshape(1, d))


def _mlp_kernel(x_ref, gpre_ref, wu_ref, wd_ref, gpost_ref, o_ref, h_ref, acc_ref):
    f = pl.program_id(1)
    last = pl.num_programs(1) - 1
    chunks = [pl.ds(r, ROW_CHUNK) for r in range(0, x_ref.shape[0], ROW_CHUNK)]

    def part(h):
        u = jnp.maximum(jnp.dot(h, wu_ref[...], preferred_element_type=F32), 0.0)
        return jnp.dot((u * u).astype(BF16), wd_ref[...], preferred_element_type=F32)

    @pl.when(f == 0)
    def _():
        for rows in chunks:
            x = x_ref[rows, :]
            ms = jnp.mean(x * x, axis=-1, keepdims=True)
            h = (x * lax.rsqrt(ms + NORM_EPS) * gpre_ref[...]).astype(BF16)
            h_ref[rows, :] = h
            acc_ref[rows, :] = part(h)

    @pl.when((f > 0) & (f < last))
    def _():
        acc_ref[...] += part(h_ref[...])

    @pl.when(f == last)
    def _():
        for rows in chunks:
            mlp = acc_ref[rows, :] + part(h_ref[rows, :])
            ms = jnp.mean(mlp * mlp, axis=-1, keepdims=True)
            o_ref[rows, :] = x_ref[rows, :] + mlp * lax.rsqrt(ms + NORM_EPS) * gpost_ref[...]


def _mlp(x2, g_pre, w_up, w_down, g_post, tm=512, tf=1024):
    m, d = x2.shape
    dff = w_up.shape[1]
    assert m % tm == 0 and dff % tf == 0 and dff // tf >= 2 and tm % ROW_CHUNK == 0
    return pl.pallas_call(
        _mlp_kernel,
        grid=(m // tm, dff // tf),
        in_specs=[pl.BlockSpec((tm, d), lambda i, f: (i, 0)),
                  pl.BlockSpec((1, d), lambda i, f: (0, 0)),
                  pl.BlockSpec((d, tf), lambda i, f: (0, f)),
                  pl.BlockSpec((tf, d), lambda i, f: (f, 0)),
                  pl.BlockSpec((1, d), lambda i, f: (0, 0))],
        out_specs=pl.BlockSpec((tm, d), lambda i, f: (i, 0)),
        out_shape=jax.ShapeDtypeStruct((m, d), F32),
        scratch_shapes=[pltpu.VMEM((tm, d), BF16), pltpu.VMEM((tm, d), F32)],
        compiler_params=_cparams("parallel", "arbitrary"),
        name="mlp",
    )(x2, g_pre.reshape(1, d), w_up, w_down, g_post.reshape(1, d))


def _rope_tables(s):
    half = ATTN_HEAD // 2
    inv_freq = ROPE_THETA ** (-jnp.arange(half, dtype=F32) / half)
    ang = jnp.arange(s).astype(F32)[:, None] * inv_freq[None, :]
    cos, sin = jnp.cos(ang), jnp.sin(ang)
    return jnp.concatenate([cos, cos], axis=-1), jnp.concatenate([-sin, sin], axis=-1)


def kernel(x, norm_mix_pre, norm_mix_post, norm_mlp_pre, norm_mlp_post, w_in, w_in_vres, shift_mu, shift_mu_vres, decay_w0, decay_w2, iclr_a0, iclr_a2, vres_v0, vres_v2, gate_g2, k_k, k_a, r_k, lnx_gain, lnx_bias, w_out, w_up, w_down):
    bsz, s, d = x.shape
    depth = w_in.shape[0]
    c = decay_w0.shape[1]
    n_lora = DECAY_LORA + ICLR_LORA + GATE_LORA
    n_shift = 3 * c + n_lora
    ca = (w_in.shape[2] - n_shift) // 3
    cos2, sin2 = _rope_tables(s)
    w_in16 = w_in[0].astype(BF16)
    pad_v = VRES_PAD - VRES_LORA
    x2 = x.reshape(bsz * s, d)
    v_first = None
    for i in range(depth):
        if i == 0:
            z, vres = _norm_matmul(x2, norm_mix_pre[i], w_in16)[0], None
        else:
            w_v = jnp.pad(w_in_vres[i - 1], ((0, 0), (0, pad_v))).astype(BF16)
            z, z_v = _norm_matmul(x2, norm_mix_pre[i], w_in16, w_v)
            vres = (z_v.reshape(bsz, s, VRES_PAD), jnp.pad(shift_mu_vres[i - 1], (0, pad_v)), v_first,
                    vres_v0[i - 1], jnp.pad(vres_v2[i - 1], ((0, pad_v), (0, 0))))
        z3 = z.reshape(bsz, s, -1)

        prep = _rwkv_prep(z3, c, shift_mu[i], decay_w0[i], decay_w2[i], iclr_a0[i], iclr_a2[i], gate_g2[i],
                          k_k[i], k_a[i], vres)
        if i == 0:
            v_first = prep[7]
        casts = [(w_out, i), (w_up, i), (w_down, i)] + ([(w_in, i + 1)] if i + 1 < depth else [])
        y_r, w16 = _wkv(*prep[:7], r_k[i].reshape(-1), lnx_gain[i], lnx_bias[i], casts)
        w_out16, w_up16, w_down16 = w16[:3]

        y_a = _moba(*_rope(z3, n_shift, ca, cos2, sin2))

        x2 = _out_proj(y_r.reshape(bsz * s, c), y_a.reshape(bsz * s, ca), w_out16, x2, norm_mix_post[i])
        x2 = _mlp(x2, norm_mlp_pre[i], w_up16, w_down16, norm_mlp_post[i])
        if i + 1 < depth:
            w_in16 = w16[3]
    return x2.reshape(bsz, s, d)
```

```python
import functools

import jax
import jax.numpy as jnp
from jax import lax
from jax.experimental import pallas as pl
from jax.experimental.pallas import tpu as pltpu

F32 = jnp.float32
BF16 = jnp.bfloat16

RWKV_HEAD = 64
DECAY_LORA = 64
ICLR_LORA = 64
VRES_LORA = 32
GATE_LORA = 128
ATTN_HEAD = 128
MOBA_BLOCK = 256
MOBA_TOPK = 3
ROPE_THETA = 10000.0
NORM_EPS = 1e-6
LNX_EPS = 64e-5
LOG2E = 1.4426950408889634

LANES = 128
BF16_SUBLANES = 16
VRES_PAD = LANES
WKV_CHUNK = 64
MOBA_LOOKAHEAD = 6
MOBA_UNROLL = 4
ROW_CHUNK = 256
VMEM_LIMIT = 56 * 1024 * 1024


def _cparams(*sem):
    return pltpu.CompilerParams(dimension_semantics=sem, vmem_limit_bytes=VMEM_LIMIT)


_NN = (((1,), (0,)), ((), ()))
_NT = (((1,), (1,)), ((), ()))
_TN = (((0,), (0,)), ((), ()))


def _split2(x):
    hi = x.astype(BF16)
    lo = (x - hi.astype(F32)).astype(BF16)
    return hi, lo


def _split3(x):
    hi = x.astype(BF16)
    r1 = x - hi.astype(F32)
    mid = r1.astype(BF16)
    lo = (r1 - mid.astype(F32)).astype(BF16)
    return hi, mid, lo


def _mm(a, b, dims=_NN, passes=1):
    d = lambda p, q: lax.dot_general(p, q, dims, preferred_element_type=F32)
    if passes == 1:
        return d(a.astype(BF16), b.astype(BF16))
    ah, al = _split2(a)
    bh, bl = _split2(b)
    return d(ah, bh) + (d(ah, bl) + d(al, bh))


def _sigmoid(x):
    return 1.0 / (1.0 + jnp.exp(-x))


def _softplus(x):
    return jnp.maximum(x, 0.0) + jnp.log(1.0 + jnp.exp(-jnp.abs(x)))


def _norm_matmul_kernel(has_extra, *refs):
    if has_extra:
        x_ref, g_ref, w_ref, we_ref, o_ref, oe_ref, h_ref = refs
    else:
        x_ref, g_ref, w_ref, o_ref, h_ref = refs

    first = pl.program_id(1) == 0

    @pl.when(first)
    def _():
        for r in range(0, x_ref.shape[0], ROW_CHUNK):
            rows = pl.ds(r, ROW_CHUNK)
            x = x_ref[rows, :]
            ms = jnp.mean(x * x, axis=-1, keepdims=True)
            h = (x * lax.rsqrt(ms + NORM_EPS) * g_ref[...]).astype(BF16)
            h_ref[rows, :] = h
            o_ref[rows, :] = jnp.dot(h, w_ref[...], preferred_element_type=F32).astype(o_ref.dtype)
            if has_extra:
                oe_ref[rows, :] = jnp.dot(h, we_ref[...], preferred_element_type=F32).astype(oe_ref.dtype)

    @pl.when(jnp.logical_not(first))
    def _():
        o_ref[...] = jnp.dot(h_ref[...], w_ref[...], preferred_element_type=F32).astype(o_ref.dtype)


def _norm_matmul(x2, gain, w, w_extra=None, tm=1024, tn=1280):
    m, d = x2.shape
    n = w.shape[1]
    assert m % tm == 0 and n % tn == 0
    has_extra = w_extra is not None
    ins = [x2, gain.reshape(1, d), w]
    in_specs = [pl.BlockSpec((tm, d), lambda i, j: (i, 0)),
                pl.BlockSpec((1, d), lambda i, j: (0, 0)),
                pl.BlockSpec((d, tn), lambda i, j: (0, j))]
    out_specs = [pl.BlockSpec((tm, tn), lambda i, j: (i, j))]
    out_shape = [jax.ShapeDtypeStruct((m, n), BF16)]
    if has_extra:
        ne = w_extra.shape[1]
        ins.append(w_extra)
        in_specs.append(pl.BlockSpec((d, ne), lambda i, j: (0, 0)))
        out_specs.append(pl.BlockSpec((tm, ne), lambda i, j: (i, 0)))
        out_shape.append(jax.ShapeDtypeStruct((m, ne), BF16))
    return pl.pallas_call(
        functools.partial(_norm_matmul_kernel, has_extra),
        grid=(m // tm, n // tn),
        in_specs=in_specs,
        out_specs=out_specs,
        out_shape=out_shape,
        scratch_shapes=[pltpu.VMEM((tm, d), BF16)],
        compiler_params=_cparams("parallel", "arbitrary"),
        name="norm_matmul",
    )(*ins)


def _dots(a_list, b_list, dims=_NN):
    return [lax.dot_general(a.astype(BF16), b.astype(BF16), dims, preferred_element_type=F32)
            for a, b in zip(a_list, b_list)]


def _pair_diag(x, even):
    return jnp.concatenate([jnp.where(even, x, 0.0), jnp.where(even, 0.0, x)], axis=0)


def _unit_lower_inverse(a_list, row_w, col_w, even):
    n = row_w.shape[0]
    lower = row_w > col_w
    base = lower & ((row_w >> 1) == (col_w >> 1))
    t = [jnp.where(row_w == col_w, 1.0, jnp.where(base, a, 0.0)) for a in a_list]
    sh = 1
    while (2 << sh) <= n:
        sub = lower & ((row_w >> (sh + 1)) == (col_w >> (sh + 1))) & ((row_w >> sh) != (col_w >> sh))
        off = [_pair_diag(jnp.where(sub, a, 0.0), even) for a in a_list]
        upd = _dots(_dots(t, off), [_pair_diag(x, even) for x in t])
        t = [x + u for x, u in zip(t, upd)]
        sh += 1
    return t


def _mm_exact_rhs_left(l_bf16, a):
    d = lambda p: lax.dot_general(l_bf16, p, _NN, preferred_element_type=F32)
    hi, mid, lo = _split3(a)
    return d(hi) + (d(mid) + d(lo))


def _rwkv_kernel(has_vres, n_cast, nc, *refs):
    it = iter(refs)
    take = lambda cnt: [next(it) for _ in range(cnt)]
    zm_ref, zl_ref, mum_ref, mul_ref, w0_ref, w2_ref, a0_ref, a2_ref, g2_ref, kk_ref, ka_ref = take(11)
    if has_vres:
        zv_ref, muv_ref, vf_ref, v0_ref, v2_ref = take(5)
    rk_ref, lg_ref, lb_ref = take(3)
    cast_in = take(n_cast)
    y_ref, = take(1)
    if not has_vres:
        vout_ref, = take(1)
    cast_out = take(n_cast)
    state_ref, cm_ref, cl_ref = take(3)
    if has_vres:
        cv_ref, = take(1)
    r_s, lw_s, k_s, v_s, an_s, b_s, g_s = operands = take(7)

    ci = pl.program_id(1)
    ln = WKV_CHUNK
    nb = zm_ref.shape[0]
    c = r_s.shape[2]
    n = RWKV_HEAD
    pw = 2 * n
    npair = c // pw
    ent = [(bi, slice(p * pw, (p + 1) * pw)) for bi in range(nb) for p in range(npair)]

    @pl.when(ci == 0)
    def _():
        for ref in [state_ref, cm_ref, cl_ref] + ([cv_ref] if has_vres else []) + operands:
            ref[...] = jnp.zeros_like(ref)

    rows = lax.broadcasted_iota(jnp.int32, (ln, ln), 0)
    cols = lax.broadcasted_iota(jnp.int32, (ln, ln), 1)
    tril = jnp.where(rows >= cols, 1.0, 0.0).astype(BF16)
    r, k, v, w_end, r_t, a_t, b_t, k_t, b_h, k_h = ([] for _ in range(10))
    for bi in range(nb):
        lw = lw_s[bi]
        cw = _mm_exact_rhs_left(tril, lw)
        cw_end = cw[ln - 1:ln, :]
        e_neg = jnp.exp(-cw)
        e_end = jnp.exp(cw_end - cw)
        bb = b_s[bi].astype(F32)
        r.append(r_s[bi].astype(F32))
        k.append(k_s[bi].astype(F32))
        v.append(v_s[bi].astype(F32))
        w_end.append(jnp.exp(cw_end))
        r_t.append(r[bi] * jnp.exp(cw))
        a_t.append(an_s[bi].astype(F32) * jnp.exp(cw - lw))
        b_t.append(bb * e_neg)
        k_t.append(k[bi] * e_neg)
        b_h.append(bb * e_end)
        k_h.append(k[bi] * e_end)

    lane = lax.broadcasted_iota(jnp.int32, (ln, pw), 1)
    row_w = lax.broadcasted_iota(jnp.int32, (ln, pw), 0)
    even = lane < n
    even2 = lax.broadcasted_iota(jnp.int32, (2 * ln, pw), 1) < n
    col_w = lane & (n - 1)
    strict_w = row_w > col_w
    incl_w = row_w >= col_w
    zeros_w = jnp.zeros((ln, pw), F32)
    diag = lambda x: _pair_diag(x, even)
    swap = lambda x: jnp.concatenate([x[x.shape[0] // 2:], x[:x.shape[0] // 2]], axis=0)

    ar_p = [jnp.concatenate([a_t[bi][:, ps], r_t[bi][:, ps]], axis=0) for bi, ps in ent]
    bk_p = [jnp.concatenate([b_t[bi][:, ps], k_t[bi][:, ps]], axis=0).astype(BF16) for bi, ps in ent]
    kb_p = [jnp.concatenate([k_t[bi][:, ps], b_t[bi][:, ps]], axis=0).astype(BF16) for bi, ps in ent]
    am_e = _dots([jnp.where(even2, x, 0.0) for x in ar_p], bk_p, _NT)
    am_o = _dots([jnp.where(even2, 0.0, x) for x in ar_p], kb_p, _NT)
    a_ab = [jnp.where(strict_w, jnp.where(even, e[:ln], o[:ln]), 0.0) for e, o in zip(am_e, am_o)]
    a_ak = [jnp.where(strict_w, jnp.where(even, o[:ln], e[:ln]), 0.0) for e, o in zip(am_e, am_o)]
    a_rb = [jnp.where(incl_w, jnp.where(even, e[ln:], o[ln:]), 0.0) for e, o in zip(am_e, am_o)]
    a_rk = [jnp.where(incl_w, jnp.where(even, o[ln:], e[ln:]), 0.0) for e, o in zip(am_e, am_o)]
    v_p = [v[bi][:, ps] for bi, ps in ent]
    v_d = [diag(x) for x in v_p]
    akv = _dots(a_ak, [swap(x) for x in v_d])
    t = _unit_lower_inverse(a_ab, row_w, col_w, even)
    rhs = [jnp.concatenate([diag(a_t[bi][:, ps]), diag(x)], axis=1) for (bi, ps), x in zip(ent, akv)]
    pq = _dots(t, rhs)
    ry = _dots([jnp.concatenate([x, y], axis=1) for x, y in zip(a_rb, a_rk)],
               [jnp.concatenate([jnp.concatenate([diag(x[:, :pw]), diag(x[:, pw:])], axis=1),
                                 jnp.concatenate([jnp.zeros((2 * ln, pw), F32), swap(u)], axis=1)], axis=0)
                for x, u in zip(pq, v_d)])
    pqv_p = [jnp.concatenate([x, jnp.concatenate([zeros_w, u], axis=1)], axis=0)
             for x, u in zip(pq, v_p)]
    bkh_p = [jnp.concatenate([b_h[bi][:, ps], k_h[bi][:, ps]], axis=0) for bi, ps in ent]
    mn_p = _dots(bkh_p, pqv_p, _TN)
    sq_r = lax.broadcasted_iota(jnp.int32, (pw, pw), 0)
    sq_c = lax.broadcasted_iota(jnp.int32, (pw, pw), 1)
    same_head = (sq_r < n) == (sq_c < n)
    lhs_p = [jnp.concatenate([r_t[bi][:, ps] + x[:, :pw],
                              jnp.where(sq_r == sq_c, jnp.broadcast_to(w_end[bi][:, ps], (pw, pw)),
                                        jnp.where(same_head, m[:, :pw], 0.0))], axis=0)
             for (bi, ps), x, m in zip(ent, ry, mn_p)]
    st = [state_ref[e] for e in range(len(ent))]
    upd = _dots(lhs_p, st)
    for e in range(len(ent)):
        state_ref[e] = upd[e][ln:, :] + jnp.where(same_head, mn_p[e][:, pw:], 0.0)
    y0_p = [x[:, pw:] for x in ry]

    hsum = lambda x: jnp.where(even, jnp.sum(jnp.where(even, x, 0.0), axis=-1, keepdims=True),
                               jnp.sum(jnp.where(even, 0.0, x), axis=-1, keepdims=True))
    inv_n = 1.0 / n
    yn = []
    for u, y0 in zip(upd, y0_p):
        y = u[:ln, :] + y0
        d = y - hsum(y) * inv_n
        yn.append(d * lax.rsqrt(hsum(d * d) * inv_n + LNX_EPS))
    for bi in range(nb):
        rk = r[bi] * k[bi] * rk_ref[...]
        mine = range(bi * npair, (bi + 1) * npair)
        bonus = jnp.concatenate([hsum(rk[:, ent[e][1]]) * v_p[e] for e in mine], axis=1)
        out = jnp.concatenate([yn[e] for e in mine], axis=1) * lg_ref[...] + lb_ref[...] + bonus
        y_ref[bi] = (out * g_s[bi].astype(F32)).astype(y_ref.dtype)

    advance = ci < nc - 1

    def shifted(z_ref, carry_ref, mu_ref):
        out = []
        for bi in range(nb):
            cur = z_ref[bi].astype(F32)
            first_row = lax.broadcasted_iota(jnp.int32, cur.shape, 0) == 0
            prev = jnp.where(first_row, carry_ref[bi:bi + 1, :], pltpu.roll(cur, 1, axis=0))
            carry_ref[bi:bi + 1, :] = jnp.where(advance, cur[ln - 1:ln, :], carry_ref[bi:bi + 1, :])
            out.append(cur + (prev - cur) * mu_ref[...])
        return jnp.concatenate(out, axis=0)

    zs = shifted(zm_ref, cm_ref, mum_ref)
    zl = shifted(zl_ref, cl_ref, mul_ref)
    r_n, k_n, v_n = zs[:, :c], zs[:, c:2 * c], zs[:, 2 * c:3 * c]
    o = 0
    wd = zl[:, o:o + DECAY_LORA]
    o += DECAY_LORA
    ad = zl[:, o:o + ICLR_LORA]
    o += ICLR_LORA
    gd = zl[:, o:o + GATE_LORA]
    w_log = -_softplus(-(w0_ref[...] + _mm(jnp.tanh(wd), w2_ref[...], passes=3))) - 0.5
    lw_n = -jnp.exp(w_log)
    a_n = _sigmoid(a0_ref[...] + _mm(ad, a2_ref[...]))
    g_n = _mm(_sigmoid(gd), g2_ref[...])
    if has_vres:
        vd = shifted(zv_ref, cv_ref, muv_ref)
        v_first = jnp.concatenate([vf_ref[bi] for bi in range(nb)], axis=0)
        v_n = v_n + (v_first - v_n) * _sigmoid(v0_ref[...] + _mm(vd, v2_ref[...]))
    kk = k_n * kk_ref[...]
    even_n = lax.broadcasted_iota(jnp.int32, (kk.shape[0], pw), 1) < n
    unit = []
    for p in range(npair):
        x = kk[:, p * pw:(p + 1) * pw]
        sq = x * x
        ss = jnp.where(even_n, jnp.sum(jnp.where(even_n, sq, 0.0), axis=-1, keepdims=True),
                       jnp.sum(jnp.where(even_n, 0.0, sq), axis=-1, keepdims=True))
        unit.append(x / jnp.maximum(jnp.sqrt(ss), 1e-12))
    kk = jnp.concatenate(unit, axis=1)
    k_n = k_n * (1.0 + (a_n - 1.0) * ka_ref[...])
    kka = kk * a_n
    for bi in range(nb):
        rs = slice(bi * ln, (bi + 1) * ln)
        r_s[bi] = r_n[rs]
        lw_s[bi] = lw_n[rs]
        k_s[bi] = k_n[rs]
        v_s[bi] = v_n[rs]
        an_s[bi] = -kk[rs]
        b_s[bi] = kka[rs]
        g_s[bi] = g_n[rs]
        if not has_vres:
            vout_ref[bi] = v_n[rs]

    for src, dst in zip(cast_in, cast_out):
        dst[...] = src[...].astype(dst.dtype)


def _rwkv(z3, c, mu, w0, w2, a0, a2, g2, k_k, k_a, vres, r_k, lnx_g, lnx_b, casts=()):
    bsz, s, _ = z3.shape
    n_lora = w2.shape[0] + a2.shape[0] + g2.shape[0]
    nc = s // WKV_CHUNK
    assert s % WKV_CHUNK == 0 and WKV_CHUNK == RWKV_HEAD and c % (2 * RWKV_HEAD) == 0
    assert (3 * c) % n_lora == 0 and n_lora % LANES == 0
    has_vres = vres is not None
    row = lambda a: a.reshape(1, -1)
    lcb = 3 * c // n_lora
    cur = lambda ci: jnp.minimum(ci, nc - 1)
    full = lambda a: pl.BlockSpec(a.shape, lambda hi, ci: (0,) * a.ndim)
    chunk = lambda w, cb=0: pl.BlockSpec((bsz, WKV_CHUNK, w), lambda hi, ci: (0, cur(ci), cb))
    ins = [z3, z3, row(mu[:3 * c]), row(mu[3 * c:]), row(w0), w2, row(a0), a2, g2, row(k_k), row(k_a)]
    in_specs = [chunk(3 * c), chunk(n_lora, lcb)] + [full(a) for a in ins[2:]]
    scratch = [pltpu.VMEM((bsz * c // (2 * RWKV_HEAD), 2 * RWKV_HEAD, 2 * RWKV_HEAD), F32),
               pltpu.VMEM((bsz, 3 * c), F32), pltpu.VMEM((bsz, n_lora), F32)]
    if has_vres:
        zv3, mu_v, v_first, v0, v2 = vres
        extra = [zv3, row(mu_v), v_first, row(v0), v2]
        ins += extra
        in_specs += [chunk(zv3.shape[-1]), full(extra[1]), chunk(c), full(extra[3]), full(extra[4])]
        scratch.append(pltpu.VMEM((bsz, zv3.shape[-1]), F32))
    tail = [row(r_k), row(lnx_g), row(lnx_b)]
    ins += tail
    in_specs += [full(a) for a in tail]
    scratch += [pltpu.VMEM((bsz, WKV_CHUNK, c), F32)] * 7
    out_specs = [pl.BlockSpec((bsz, WKV_CHUNK, c), lambda hi, ci: (0, jnp.maximum(ci - 1, 0), 0))]
    out_shape = [jax.ShapeDtypeStruct((bsz, s, c), BF16)]
    if not has_vres:
        out_specs.append(chunk(c))
        out_shape.append(jax.ShapeDtypeStruct((bsz, s, c), F32))
    for w, layer in casts:
        nl, rows, cols = w.shape
        assert rows % (BF16_SUBLANES * nc) == 0
        slab = rows // nc
        ins.append(w.reshape(nl, nc, slab, cols))
        in_specs.append(pl.BlockSpec((None, None, slab, cols), lambda hi, ci, layer=layer: (layer, cur(ci), 0, 0)))
        out_specs.append(pl.BlockSpec((None, slab, cols), lambda hi, ci: (cur(ci), 0, 0)))
        out_shape.append(jax.ShapeDtypeStruct((nc, slab, cols), BF16))
    out = pl.pallas_call(
        functools.partial(_rwkv_kernel, has_vres, len(casts), nc),
        grid=(1, nc + 1),
        in_specs=in_specs,
        out_specs=out_specs,
        out_shape=out_shape,
        scratch_shapes=scratch,
        compiler_params=_cparams("arbitrary", "arbitrary"),
        name="rwkv",
    )(*ins)
    n_main = 1 if has_vres else 2
    w16 = [o.reshape(w.shape[1], w.shape[2]) for o, (w, _) in zip(out[n_main:], casts)]
    return out[0], (None if has_vres else out[1]), w16


def _rope_kernel(q_ref, k_ref, v_ref, cos_ref, sin_ref, qs_ref, ko_ref, vt_ref, bias_ref, km_ref):
    i = pl.program_id(1)
    nb = km_ref.shape[0]
    blk = q_ref.shape[1]
    nh = q_ref.shape[-1] // ATTN_HEAD
    cos = cos_ref[...]
    sin = sin_ref[...]

    @pl.when(i == 0)
    def _():
        km_ref[...] = jnp.zeros_like(km_ref)

    blk_id = lax.broadcasted_iota(jnp.int32, (nb, blk), 0)
    past = blk_id < i
    qk_scale = (ATTN_HEAD ** -0.5) * LOG2E
    for h in range(nh):
        sl = slice(h * ATTN_HEAD, (h + 1) * ATTN_HEAD)
        q = q_ref[0, :, sl].astype(F32)
        k = k_ref[0, :, sl].astype(F32)
        qr = (q * cos + pltpu.roll(q, ATTN_HEAD // 2, axis=1) * sin).T
        kr = k * cos + pltpu.roll(k, ATTN_HEAD // 2, axis=1) * sin
        qs_ref[0, h, 0] = (qr * qk_scale).astype(BF16)
        ko_ref[0, h, 0] = kr.astype(BF16)
        vt_ref[0, h, 0] = v_ref[0, :, sl].astype(F32).T.astype(BF16)
        gate = jnp.where(past, _mm(km_ref[:, sl], qr, passes=3), -jnp.inf)
        rank = jnp.zeros((nb, blk), jnp.int32)
        for m in range(nb):
            gm = gate[m:m + 1, :]
            rank += ((gm > gate) | ((gm == gate) & (m < blk_id))).astype(jnp.int32)
        bias_ref[0, h, 0] = jnp.where(past & (rank < MOBA_TOPK), 0.0, -jnp.inf)
        km_new = jnp.sum(kr, axis=0, keepdims=True) * (1.0 / blk)
        km_ref[:, sl] = jnp.where(lax.broadcasted_iota(jnp.int32, (nb, ATTN_HEAD), 0) == i, km_new, km_ref[:, sl])


def _rope(z3, col0, width, cos2, sin2):
    bsz, s, _ = z3.shape
    assert s % MOBA_BLOCK == 0 and col0 % ATTN_HEAD == 0
    nb = s // MOBA_BLOCK
    nh = width // ATTN_HEAD
    blk = lambda j: pl.BlockSpec((pl.Element(1), pl.Element(MOBA_BLOCK), pl.Element(width)),
                                 lambda b, i: (b, i * MOBA_BLOCK, col0 + j * width))
    tab = pl.BlockSpec((MOBA_BLOCK, ATTN_HEAD), lambda b, i: (i, 0))
    t_spec = pl.BlockSpec((1, nh, 1, ATTN_HEAD, MOBA_BLOCK), lambda b, i: (b, 0, i, 0, 0))
    n_spec = pl.BlockSpec((1, nh, 1, MOBA_BLOCK, ATTN_HEAD), lambda b, i: (b, 0, i, 0, 0))
    return pl.pallas_call(
        _rope_kernel,
        grid=(bsz, nb),
        in_specs=[blk(0), blk(1), blk(2), tab, tab],
        out_specs=[t_spec, n_spec, t_spec,
                   pl.BlockSpec((1, nh, 1, nb, MOBA_BLOCK), lambda b, i: (b, 0, i, 0, 0))],
        out_shape=[jax.ShapeDtypeStruct((bsz, nh, nb, ATTN_HEAD, MOBA_BLOCK), BF16),
                   jax.ShapeDtypeStruct((bsz, nh, nb, MOBA_BLOCK, ATTN_HEAD), BF16),
                   jax.ShapeDtypeStruct((bsz, nh, nb, ATTN_HEAD, MOBA_BLOCK), BF16),
                   jax.ShapeDtypeStruct((bsz, nh, nb, nb, MOBA_BLOCK), F32)],
        scratch_shapes=[pltpu.VMEM((nb, width), F32)],
        compiler_params=_cparams("parallel", "arbitrary"),
        name="rope",
    )(z3, z3, z3, cos2, sin2)


def _moba_kernel(qs_ref, k_ref, vt_ref, bias_ref, o_ref):
    blk = MOBA_BLOCK
    dh = ATTN_HEAD
    hs = range(qs_ref.shape[1])
    qb = pl.program_id(2)
    neg = -jnp.inf
    qs = [qs_ref[0, h, 0] for h in hs]
    ki = lax.broadcasted_iota(jnp.int32, (blk, blk), 0)
    qi = lax.broadcasted_iota(jnp.int32, (blk, blk), 1)
    causal = ki <= qi

    def pipelined(work, stage):
        scores = lambda kb, h: jnp.dot(k_ref[0, h, kb], qs[h], preferred_element_type=F32)
        ahead = [scores(*w) for w in work[:MOBA_LOOKAHEAD]]
        for i, (kb, h) in enumerate(work):
            if i + MOBA_LOOKAHEAD < len(work):
                ahead.append(scores(*work[i + MOBA_LOOKAHEAD]))
            stage(kb, h, ahead[i])

    m_run, l_run, acc = [None] * len(hs), [None] * len(hs), [None] * len(hs)

    def own_block(kb, h, s):
        s = jnp.where(causal, s, neg)
        m_run[h] = jnp.max(s, axis=0, keepdims=True)
        p = jnp.exp2(s - m_run[h])
        l_run[h] = jnp.sum(p, axis=0, keepdims=True)
        acc[h] = jnp.dot(vt_ref[0, h, kb], p.astype(BF16), preferred_element_type=F32)

    pipelined([(qb, h) for h in hs], own_block)

    def past_blocks(kbs, carry):
        m_c, l_c, acc_c = (list(c) for c in carry)

        def stage(kb, h, s):
            b = bias_ref[0, h, 0, pl.ds(kb, 1), :]
            m_new = jnp.where(b == 0.0, jnp.maximum(m_c[h], jnp.max(s, axis=0, keepdims=True)), m_c[h])
            alpha = jnp.exp2(m_c[h] - m_new)
            p = jnp.exp2(s - (m_new - b))
            pv = jnp.dot(vt_ref[0, h, kb], p.astype(BF16), preferred_element_type=F32)
            m_c[h] = m_new
            l_c[h] = alpha * l_c[h] + jnp.sum(p, axis=0, keepdims=True)
            acc_c[h] = alpha * acc_c[h] + pv

        pipelined([(kb, h) for kb in kbs for h in hs], stage)
        return tuple(m_c), tuple(l_c), tuple(acc_c)

    u = MOBA_UNROLL
    carry = lax.fori_loop(0, qb // u, lambda j, c: past_blocks([j * u + i for i in range(u)], c),
                          (tuple(m_run), tuple(l_run), tuple(acc)))
    _, l_fin, acc = lax.fori_loop((qb // u) * u, qb, lambda kb, c: past_blocks([kb], c), carry)
    for h in hs:
        o_ref[0, :, h * dh:(h + 1) * dh] = (acc[h] / l_fin[h]).T.astype(o_ref.dtype)


def _moba(qs, k, vt, bias, heads_per_step=8):
    bsz, nh, nb, dh, blk = qs.shape
    hp = heads_per_step
    assert nh % hp == 0
    return pl.pallas_call(
        _moba_kernel,
        grid=(bsz, nh // hp, nb),
        in_specs=[pl.BlockSpec((1, hp, 1, dh, blk), lambda b, h, i: (b, h, i, 0, 0)),
                  pl.BlockSpec((1, hp, nb, blk, dh), lambda b, h, i: (b, h, 0, 0, 0)),
                  pl.BlockSpec((1, hp, nb, dh, blk), lambda b, h, i: (b, h, 0, 0, 0)),
                  pl.BlockSpec((1, hp, 1, nb, blk), lambda b, h, i: (b, h, i, 0, 0))],
        out_specs=pl.BlockSpec((1, blk, hp * dh), lambda b, h, i: (b, i, h)),
        out_shape=jax.ShapeDtypeStruct((bsz, nb * blk, nh * dh), BF16),
        compiler_params=_cparams("parallel", "parallel", "arbitrary"),
        name="moba",
    )(qs, k, vt, bias)


def _out_proj_kernel(yr_ref, ya_ref, wr_ref, wa_ref, x_ref, g_ref, o_ref):
    for r in range(0, x_ref.shape[0], ROW_CHUNK):
        rows = pl.ds(r, ROW_CHUNK)
        y = jnp.dot(yr_ref[rows, :], wr_ref[...], preferred_element_type=F32)
        y += jnp.dot(ya_ref[rows, :], wa_ref[...], preferred_element_type=F32)
        ms = jnp.mean(y * y, axis=-1, keepdims=True)
        o_ref[rows, :] = x_ref[rows, :] + y * lax.rsqrt(ms + NORM_EPS) * g_ref[...]


def _out_proj(y_r, y_a, w, x2, gain, tm=512):
    m, d = x2.shape
    cw = y_r.shape[1]
    assert y_a.shape[1] == cw and w.shape[0] == 2 * cw and m % tm == 0
    return pl.pallas_call(
        _out_proj_kernel,
        grid=(m // tm,),
        in_specs=[pl.BlockSpec((tm, cw), lambda i: (i, 0)), pl.BlockSpec((tm, cw), lambda i: (i, 0)),
                  pl.BlockSpec((cw, d), lambda i: (0, 0)), pl.BlockSpec((cw, d), lambda i: (1, 0)),
                  pl.BlockSpec((tm, d), lambda i: (i, 0)), pl.BlockSpec((1, d), lambda i: (0, 0))],
        out_specs=pl.BlockSpec((tm, d), lambda i: (i, 0)),
        out_shape=jax.ShapeDtypeStruct((m, d), F32),
        compiler_params=_cparams("parallel"),
        name="out_proj",
    )(y_r, y_a, w, w, x2, gain.reshape(1, d))


def _mlp_kernel(x_ref, gpre_ref, wu_ref, wd_ref, gpost_ref, o_ref, h_ref, acc_ref):
    f = pl.program_id(1)
    last = pl.num_programs(1) - 1
    chunks = [pl.ds(r, ROW_CHUNK) for r in range(0, x_ref.shape[0], ROW_CHUNK)]

    def part(h):
        u = jnp.maximum(jnp.dot(h, wu_ref[...], preferred_element_type=F32), 0.0)
        return jnp.dot((u * u).astype(BF16), wd_ref[...], preferred_element_type=F32)

    @pl.when(f == 0)
    def _():
        for rows in chunks:
            x = x_ref[rows, :]
            ms = jnp.mean(x * x, axis=-1, keepdims=True)
            h = (x * lax.rsqrt(ms + NORM_EPS) * gpre_ref[...]).astype(BF16)
            h_ref[rows, :] = h
            acc_ref[rows, :] = part(h)

    @pl.when((f > 0) & (f < last))
    def _():
        acc_ref[...] += part(h_ref[...])

    @pl.when(f == last)
    def _():
        for rows in chunks:
            mlp = acc_ref[rows, :] + part(h_ref[rows, :])
            ms = jnp.mean(mlp * mlp, axis=-1, keepdims=True)
            o_ref[rows, :] = x_ref[rows, :] + mlp * lax.rsqrt(ms + NORM_EPS) * gpost_ref[...]


def _mlp(x2, g_pre, w_up, w_down, g_post, tm=512, tf=1024):
    m, d = x2.shape
    dff = w_up.shape[1]
    assert m % tm == 0 and dff % tf == 0 and dff // tf >= 2 and tm % ROW_CHUNK == 0
    return pl.pallas_call(
        _mlp_kernel,
        grid=(m // tm, dff // tf),
        in_specs=[pl.BlockSpec((tm, d), lambda i, f: (i, 0)),
                  pl.BlockSpec((1, d), lambda i, f: (0, 0)),
                  pl.BlockSpec((d, tf), lambda i, f: (0, f)),
                  pl.BlockSpec((tf, d), lambda i, f: (f, 0)),
                  pl.BlockSpec((1, d), lambda i, f: (0, 0))],
        out_specs=pl.BlockSpec((tm, d), lambda i, f: (i, 0)),
        out_shape=jax.ShapeDtypeStruct((m, d), F32),
        scratch_shapes=[pltpu.VMEM((tm, d), BF16), pltpu.VMEM((tm, d), F32)],
        compiler_params=_cparams("parallel", "arbitrary"),
        name="mlp",
    )(x2, g_pre.reshape(1, d), w_up, w_down, g_post.reshape(1, d))


def _rope_tables(s):
    half = ATTN_HEAD // 2
    inv_freq = ROPE_THETA ** (-jnp.arange(half, dtype=F32) / half)
    ang = jnp.arange(s).astype(F32)[:, None] * inv_freq[None, :]
    cos, sin = jnp.cos(ang), jnp.sin(ang)
    return jnp.concatenate([cos, cos], axis=-1), jnp.concatenate([-sin, sin], axis=-1)


def kernel(x, norm_mix_pre, norm_mix_post, norm_mlp_pre, norm_mlp_post, w_in, w_in_vres, shift_mu, shift_mu_vres, decay_w0, decay_w2, iclr_a0, iclr_a2, vres_v0, vres_v2, gate_g2, k_k, k_a, r_k, lnx_gain, lnx_bias, w_out, w_up, w_down):
    bsz, s, d = x.shape
    depth = w_in.shape[0]
    c = decay_w0.shape[1]
    n_lora = DECAY_LORA + ICLR_LORA + GATE_LORA
    n_shift = 3 * c + n_lora
    ca = (w_in.shape[2] - n_shift) // 3
    cos2, sin2 = _rope_tables(s)
    w_in16 = w_in[0].astype(BF16)
    pad_v = VRES_PAD - VRES_LORA
    x2 = x.reshape(bsz * s, d)
    v_first = None
    for i in range(depth):
        if i == 0:
            z, vres = _norm_matmul(x2, norm_mix_pre[i], w_in16)[0], None
        else:
            w_v = jnp.pad(w_in_vres[i - 1], ((0, 0), (0, pad_v))).astype(BF16)
            z, z_v = _norm_matmul(x2, norm_mix_pre[i], w_in16, w_v)
            vres = (z_v.reshape(bsz, s, VRES_PAD), jnp.pad(shift_mu_vres[i - 1], (0, pad_v)), v_first,
                    vres_v0[i - 1], jnp.pad(vres_v2[i - 1], ((0, pad_v), (0, 0))))
        z3 = z.reshape(bsz, s, -1)

        casts = [(w_out, i), (w_up, i), (w_down, i)] + ([(w_in, i + 1)] if i + 1 < depth else [])
        y_r, v_layer, w16 = _rwkv(z3, c, shift_mu[i], decay_w0[i], decay_w2[i], iclr_a0[i], iclr_a2[i], gate_g2[i],
                                  k_k[i], k_a[i], vres, r_k[i].reshape(-1), lnx_gain[i], lnx_bias[i], casts)
        if i == 0:
            v_first = v_layer
        w_out16, w_up16, w_down16 = w16[:3]

        y_a = _moba(*_rope(z3, n_shift, ca, cos2, sin2))

        x2 = _out_proj(y_r.reshape(bsz * s, c), y_a.reshape(bsz * s, ca), w_out16, x2, norm_mix_post[i])
        x2 = _mlp(x2, norm_mlp_pre[i], w_up16, w_down16, norm_mlp_post[i])
        if i + 1 < depth:
            w_in16 = w16[3]
    return x2.reshape(bsz, s, d)
```

```python
import functools

import jax
import jax.numpy as jnp
from jax import lax
from jax.experimental import pallas as pl
from jax.experimental.pallas import tpu as pltpu

F32 = jnp.float32
BF16 = jnp.bfloat16

RWKV_HEAD = 64
DECAY_LORA = 64
ICLR_LORA = 64
VRES_LORA = 32
GATE_LORA = 128
ATTN_HEAD = 128
MOBA_BLOCK = 256
MOBA_TOPK = 3
ROPE_THETA = 10000.0
NORM_EPS = 1e-6
LNX_EPS = 64e-5
LOG2E = 1.4426950408889634

LANES = 128
BF16_SUBLANES = 16
VRES_PAD = LANES
WKV_CHUNK = 64
MOBA_LOOKAHEAD = 6
MOBA_UNROLL = 4
ROW_CHUNK = 256
VMEM_LIMIT = 56 * 1024 * 1024


def _cparams(*sem):
    return pltpu.CompilerParams(dimension_semantics=sem, vmem_limit_bytes=VMEM_LIMIT)


_NN = (((1,), (0,)), ((), ()))
_NT = (((1,), (1,)), ((), ()))
_TN = (((0,), (0,)), ((), ()))


def _split2(x):
    hi = x.astype(BF16)
    lo = (x - hi.astype(F32)).astype(BF16)
    return hi, lo


def _split3(x):
    hi = x.astype(BF16)
    r1 = x - hi.astype(F32)
    mid = r1.astype(BF16)
    lo = (r1 - mid.astype(F32)).astype(BF16)
    return hi, mid, lo


def _mm(a, b, dims=_NN, passes=1):
    d = lambda p, q: lax.dot_general(p, q, dims, preferred_element_type=F32)
    if passes == 1:
        return d(a.astype(BF16), b.astype(BF16))
    ah, al = _split2(a)
    bh, bl = _split2(b)
    return d(ah, bh) + (d(ah, bl) + d(al, bh))


def _sigmoid(x):
    return 1.0 / (1.0 + jnp.exp(-x))


def _softplus(x):
    return jnp.maximum(x, 0.0) + jnp.log(1.0 + jnp.exp(-jnp.abs(x)))


def _norm_matmul_kernel(has_extra, *refs):
    if has_extra:
        x_ref, g_ref, w_ref, we_ref, o_ref, oe_ref, h_ref = refs
    else:
        x_ref, g_ref, w_ref, o_ref, h_ref = refs

    first = pl.program_id(1) == 0

    @pl.when(first)
    def _():
        for r in range(0, x_ref.shape[0], ROW_CHUNK):
            rows = pl.ds(r, ROW_CHUNK)
            x = x_ref[rows, :]
            ms = jnp.mean(x * x, axis=-1, keepdims=True)
            h = (x * lax.rsqrt(ms + NORM_EPS) * g_ref[...]).astype(BF16)
            h_ref[rows, :] = h
            o_ref[rows, :] = jnp.dot(h, w_ref[...], preferred_element_type=F32).astype(o_ref.dtype)
            if has_extra:
                oe_ref[rows, :] = jnp.dot(h, we_ref[...], preferred_element_type=F32).astype(oe_ref.dtype)

    @pl.when(jnp.logical_not(first))
    def _():
        o_ref[...] = jnp.dot(h_ref[...], w_ref[...], preferred_element_type=F32).astype(o_ref.dtype)


def _norm_matmul(x2, gain, w, w_extra=None, tm=1024, tn=1280):
    m, d = x2.shape
    n = w.shape[1]
    assert m % tm == 0 and n % tn == 0
    has_extra = w_extra is not None
    ins = [x2, gain.reshape(1, d), w]
    in_specs = [pl.BlockSpec((tm, d), lambda i, j: (i, 0)),
                pl.BlockSpec((1, d), lambda i, j: (0, 0)),
                pl.BlockSpec((d, tn), lambda i, j: (0, j))]
    out_specs = [pl.BlockSpec((tm, tn), lambda i, j: (i, j))]
    out_shape = [jax.ShapeDtypeStruct((m, n), BF16)]
    if has_extra:
        ne = w_extra.shape[1]
        ins.append(w_extra)
        in_specs.append(pl.BlockSpec((d, ne), lambda i, j: (0, 0)))
        out_specs.append(pl.BlockSpec((tm, ne), lambda i, j: (i, 0)))
        out_shape.append(jax.ShapeDtypeStruct((m, ne), BF16))
    return pl.pallas_call(
        functools.partial(_norm_matmul_kernel, has_extra),
        grid=(m // tm, n // tn),
        in_specs=in_specs,
        out_specs=out_specs,
        out_shape=out_shape,
        scratch_shapes=[pltpu.VMEM((tm, d), BF16)],
        compiler_params=_cparams("parallel", "arbitrary"),
        name="norm_matmul",
    )(*ins)


def _dots(a_list, b_list, dims=_NN):
    return [lax.dot_general(a.astype(BF16), b.astype(BF16), dims, preferred_element_type=F32)
            for a, b in zip(a_list, b_list)]


def _pair_diag(x, even):
    return jnp.concatenate([jnp.where(even, x, 0.0), jnp.where(even, 0.0, x)], axis=0)


def _unit_lower_inverse(a_list, row_w, col_w, even):
    n = row_w.shape[0]
    lower = row_w > col_w
    base = lower & ((row_w >> 1) == (col_w >> 1))
    t = [jnp.where(row_w == col_w, 1.0, jnp.where(base, a, 0.0)) for a in a_list]
    sh = 1
    while (2 << sh) <= n:
        sub = lower & ((row_w >> (sh + 1)) == (col_w >> (sh + 1))) & ((row_w >> sh) != (col_w >> sh))
        off = [_pair_diag(jnp.where(sub, a, 0.0), even) for a in a_list]
        upd = _dots(_dots(t, off), [_pair_diag(x, even) for x in t])
        t = [x + u for x, u in zip(t, upd)]
        sh += 1
    return t


def _mm_exact_rhs_left(l_bf16, a):
    d = lambda p: lax.dot_general(l_bf16, p, _NN, preferred_element_type=F32)
    hi, mid, lo = _split3(a)
    return d(hi) + (d(mid) + d(lo))


def _rwkv_kernel(has_vres, n_cast, nc, *refs):
    it = iter(refs)
    take = lambda cnt: [next(it) for _ in range(cnt)]
    zm_ref, zl_ref, mum_ref, mul_ref, w0_ref, w2_ref, a0_ref, a2_ref, g2_ref, kk_ref, ka_ref = take(11)
    if has_vres:
        zv_ref, muv_ref, vf_ref, v0_ref, v2_ref = take(5)
    rk_ref, lg_ref, lb_ref = take(3)
    cast_in = take(n_cast)
    y_ref, = take(1)
    if not has_vres:
        vout_ref, = take(1)
    cast_out = take(n_cast)
    state_ref, cm_ref, cl_ref = take(3)
    if has_vres:
        cv_ref, = take(1)
    r_s, lw_s, k_s, v_s, an_s, b_s, g_s = operands = take(7)

    ci = pl.program_id(1)
    ln = WKV_CHUNK
    nb = zm_ref.shape[0]
    c = r_s.shape[2]
    n = RWKV_HEAD
    pw = 2 * n
    npair = c // pw
    ent = [(bi, slice(p * pw, (p + 1) * pw)) for bi in range(nb) for p in range(npair)]

    @pl.when(ci == 0)
    def _():
        for ref in [state_ref, cm_ref, cl_ref] + ([cv_ref] if has_vres else []) + operands:
            ref[...] = jnp.zeros_like(ref)

    rows = lax.broadcasted_iota(jnp.int32, (ln, ln), 0)
    cols = lax.broadcasted_iota(jnp.int32, (ln, ln), 1)
    tril = jnp.where(rows >= cols, 1.0, 0.0).astype(BF16)
    r, k, v, w_end, r_t, a_t, b_t, k_t, b_h, k_h = ([] for _ in range(10))
    for bi in range(nb):
        lw = lw_s[bi]
        cw = _mm_exact_rhs_left(tril, lw)
        cw_end = cw[ln - 1:ln, :]
        e_neg = jnp.exp(-cw)
        e_end = jnp.exp(cw_end - cw)
        bb = b_s[bi].astype(F32)
        r.append(r_s[bi].astype(F32))
        k.append(k_s[bi].astype(F32))
        v.append(v_s[bi].astype(F32))
        w_end.append(jnp.exp(cw_end))
        r_t.append(r[bi] * jnp.exp(cw))
        a_t.append(an_s[bi].astype(F32) * jnp.exp(cw - lw))
        b_t.append(bb * e_neg)
        k_t.append(k[bi] * e_neg)
        b_h.append(bb * e_end)
        k_h.append(k[bi] * e_end)

    lane = lax.broadcasted_iota(jnp.int32, (ln, pw), 1)
    row_w = lax.broadcasted_iota(jnp.int32, (ln, pw), 0)
    even = lane < n
    even2 = lax.broadcasted_iota(jnp.int32, (2 * ln, pw), 1) < n
    col_w = lane & (n - 1)
    strict_w = row_w > col_w
    incl_w = row_w >= col_w
    zeros_w = jnp.zeros((ln, pw), F32)
    diag = lambda x: _pair_diag(x, even)
    swap = lambda x: jnp.concatenate([x[x.shape[0] // 2:], x[:x.shape[0] // 2]], axis=0)

    ar_p = [jnp.concatenate([a_t[bi][:, ps], r_t[bi][:, ps]], axis=0) for bi, ps in ent]
    bk_p = [jnp.concatenate([b_t[bi][:, ps], k_t[bi][:, ps]], axis=0).astype(BF16) for bi, ps in ent]
    kb_p = [jnp.concatenate([k_t[bi][:, ps], b_t[bi][:, ps]], axis=0).astype(BF16) for bi, ps in ent]
    am_e = _dots([jnp.where(even2, x, 0.0) for x in ar_p], bk_p, _NT)
    am_o = _dots([jnp.where(even2, 0.0, x) for x in ar_p], kb_p, _NT)
    a_ab = [jnp.where(strict_w, jnp.where(even, e[:ln], o[:ln]), 0.0) for e, o in zip(am_e, am_o)]
    a_ak = [jnp.where(strict_w, jnp.where(even, o[:ln], e[:ln]), 0.0) for e, o in zip(am_e, am_o)]
    a_rb = [jnp.where(incl_w, jnp.where(even, e[ln:], o[ln:]), 0.0) for e, o in zip(am_e, am_o)]
    a_rk = [jnp.where(incl_w, jnp.where(even, o[ln:], e[ln:]), 0.0) for e, o in zip(am_e, am_o)]
    v_p = [v[bi][:, ps] for bi, ps in ent]
    v_d = [diag(x) for x in v_p]
    akv = _dots(a_ak, [swap(x) for x in v_d])
    t = _unit_lower_inverse(a_ab, row_w, col_w, even)
    rhs = [jnp.concatenate([diag(a_t[bi][:, ps]), diag(x)], axis=1) for (bi, ps), x in zip(ent, akv)]
    pq = _dots(t, rhs)
    ry = _dots([jnp.concatenate([x, y], axis=1) for x, y in zip(a_rb, a_rk)],
               [jnp.concatenate([jnp.concatenate([diag(x[:, :pw]), diag(x[:, pw:])], axis=1),
                                 jnp.concatenate([jnp.zeros((2 * ln, pw), F32), swap(u)], axis=1)], axis=0)
                for x, u in zip(pq, v_d)])
    pqv_p = [jnp.concatenate([x, jnp.concatenate([zeros_w, u], axis=1)], axis=0)
             for x, u in zip(pq, v_p)]
    bkh_p = [jnp.concatenate([b_h[bi][:, ps], k_h[bi][:, ps]], axis=0) for bi, ps in ent]
    mn_p = _dots(bkh_p, pqv_p, _TN)
    sq_r = lax.broadcasted_iota(jnp.int32, (pw, pw), 0)
    sq_c = lax.broadcasted_iota(jnp.int32, (pw, pw), 1)
    same_head = (sq_r < n) == (sq_c < n)
    lhs_p = [jnp.concatenate([r_t[bi][:, ps] + x[:, :pw],
                              jnp.where(sq_r == sq_c, jnp.broadcast_to(w_end[bi][:, ps], (pw, pw)),
                                        jnp.where(same_head, m[:, :pw], 0.0))], axis=0)
             for (bi, ps), x, m in zip(ent, ry, mn_p)]
    st = [state_ref[e] for e in range(len(ent))]
    upd = _dots(lhs_p, st)
    for e in range(len(ent)):
        state_ref[e] = upd[e][ln:, :] + jnp.where(same_head, mn_p[e][:, pw:], 0.0)
    y0_p = [x[:, pw:] for x in ry]

    hsum = lambda x: jnp.where(even, jnp.sum(jnp.where(even, x, 0.0), axis=-1, keepdims=True),
                               jnp.sum(jnp.where(even, 0.0, x), axis=-1, keepdims=True))
    inv_n = 1.0 / n
    yn = []
    for u, y0 in zip(upd, y0_p):
        y = u[:ln, :] + y0
        d = y - hsum(y) * inv_n
        yn.append(d * lax.rsqrt(hsum(d * d) * inv_n + LNX_EPS))
    for bi in range(nb):
        rk = r[bi] * k[bi] * rk_ref[...]
        mine = range(bi * npair, (bi + 1) * npair)
        bonus = jnp.concatenate([hsum(rk[:, ent[e][1]]) * v_p[e] for e in mine], axis=1)
        out = jnp.concatenate([yn[e] for e in mine], axis=1) * lg_ref[...] + lb_ref[...] + bonus
        y_ref[bi] = (out * g_s[bi].astype(F32)).astype(y_ref.dtype)

    advance = ci < nc - 1

    def shifted(z_ref, carry_ref, mu_ref):
        out = []
        for bi in range(nb):
            cur = z_ref[bi].astype(F32)
            first_row = lax.broadcasted_iota(jnp.int32, cur.shape, 0) == 0
            prev = jnp.where(first_row, carry_ref[bi:bi + 1, :], pltpu.roll(cur, 1, axis=0))
            carry_ref[bi:bi + 1, :] = jnp.where(advance, cur[ln - 1:ln, :], carry_ref[bi:bi + 1, :])
            out.append(cur + (prev - cur) * mu_ref[...])
        return jnp.concatenate(out, axis=0)

    zs = shifted(zm_ref, cm_ref, mum_ref)
    zl = shifted(zl_ref, cl_ref, mul_ref)
    r_n, k_n, v_n = zs[:, :c], zs[:, c:2 * c], zs[:, 2 * c:3 * c]
    o = 0
    wd = zl[:, o:o + DECAY_LORA]
    o += DECAY_LORA
    ad = zl[:, o:o + ICLR_LORA]
    o += ICLR_LORA
    gd = zl[:, o:o + GATE_LORA]
    w_log = -_softplus(-(w0_ref[...] + _mm(jnp.tanh(wd), w2_ref[...], passes=3))) - 0.5
    lw_n = -jnp.exp(w_log)
    a_n = _sigmoid(a0_ref[...] + _mm(ad, a2_ref[...]))
    g_n = _mm(_sigmoid(gd), g2_ref[...])
    if has_vres:
        vd = shifted(zv_ref, cv_ref, muv_ref)
        v_first = jnp.concatenate([vf_ref[bi] for bi in range(nb)], axis=0)
        v_n = v_n + (v_first - v_n) * _sigmoid(v0_ref[...] + _mm(vd, v2_ref[...]))
    kk = k_n * kk_ref[...]
    even_n = lax.broadcasted_iota(jnp.int32, (kk.shape[0], pw), 1) < n
    unit = []
    for p in range(npair):
        x = kk[:, p * pw:(p + 1) * pw]
        sq = x * x
        ss = jnp.where(even_n, jnp.sum(jnp.where(even_n, sq, 0.0), axis=-1, keepdims=True),
                       jnp.sum(jnp.where(even_n, 0.0, sq), axis=-1, keepdims=True))
        unit.append(x / jnp.maximum(jnp.sqrt(ss), 1e-12))
    kk = jnp.concatenate(unit, axis=1)
    k_n = k_n * (1.0 + (a_n - 1.0) * ka_ref[...])
    kka = kk * a_n
    for bi in range(nb):
        rs = slice(bi * ln, (bi + 1) * ln)
        r_s[bi] = r_n[rs]
        lw_s[bi] = lw_n[rs]
        k_s[bi] = k_n[rs]
        v_s[bi] = v_n[rs]
        an_s[bi] = -kk[rs]
        b_s[bi] = kka[rs]
        g_s[bi] = g_n[rs]
        if not has_vres:
            vout_ref[bi] = v_n[rs]

    for src, dst in zip(cast_in, cast_out):
        dst[...] = src[...].astype(dst.dtype)


def _rwkv(z3, c, mu, w0, w2, a0, a2, g2, k_k, k_a, vres, r_k, lnx_g, lnx_b, casts=()):
    bsz, s, _ = z3.shape
    n_lora = w2.shape[0] + a2.shape[0] + g2.shape[0]
    nc = s // WKV_CHUNK
    assert s % WKV_CHUNK == 0 and WKV_CHUNK == RWKV_HEAD and c % (2 * RWKV_HEAD) == 0
    assert (3 * c) % n_lora == 0 and n_lora % LANES == 0
    has_vres = vres is not None
    row = lambda a: a.reshape(1, -1)
    lcb = 3 * c // n_lora
    cur = lambda ci: jnp.minimum(ci, nc - 1)
    full = lambda a: pl.BlockSpec(a.shape, lambda hi, ci: (0,) * a.ndim)
    chunk = lambda w, cb=0: pl.BlockSpec((bsz, WKV_CHUNK, w), lambda hi, ci: (0, cur(ci), cb))
    ins = [z3, z3, row(mu[:3 * c]), row(mu[3 * c:]), row(w0), w2, row(a0), a2, g2, row(k_k), row(k_a)]
    in_specs = [chunk(3 * c), chunk(n_lora, lcb)] + [full(a) for a in ins[2:]]
    scratch = [pltpu.VMEM((bsz * c // (2 * RWKV_HEAD), 2 * RWKV_HEAD, 2 * RWKV_HEAD), F32),
               pltpu.VMEM((bsz, 3 * c), F32), pltpu.VMEM((bsz, n_lora), F32)]
    if has_vres:
        zv3, mu_v, v_first, v0, v2 = vres
        extra = [zv3, row(mu_v), v_first, row(v0), v2]
        ins += extra
        in_specs += [chunk(zv3.shape[-1]), full(extra[1]), chunk(c), full(extra[3]), full(extra[4])]
        scratch.append(pltpu.VMEM((bsz, zv3.shape[-1]), F32))
    tail = [row(r_k), row(lnx_g), row(lnx_b)]
    ins += tail
    in_specs += [full(a) for a in tail]
    scratch += [pltpu.VMEM((bsz, WKV_CHUNK, c), F32)] * 7
    out_specs = [pl.BlockSpec((bsz, WKV_CHUNK, c), lambda hi, ci: (0, jnp.maximum(ci - 1, 0), 0))]
    out_shape = [jax.ShapeDtypeStruct((bsz, s, c), BF16)]
    if not has_vres:
        out_specs.append(chunk(c))
        out_shape.append(jax.ShapeDtypeStruct((bsz, s, c), F32))
    for w, layer in casts:
        nl, rows, cols = w.shape
        assert rows % (BF16_SUBLANES * nc) == 0
        slab = rows // nc
        ins.append(w.reshape(nl, nc, slab, cols))
        in_specs.append(pl.BlockSpec((None, None, slab, cols), lambda hi, ci, layer=layer: (layer, cur(ci), 0, 0)))
        out_specs.append(pl.BlockSpec((None, slab, cols), lambda hi, ci: (cur(ci), 0, 0)))
        out_shape.append(jax.ShapeDtypeStruct((nc, slab, cols), BF16))
    out = pl.pallas_call(
        functools.partial(_rwkv_kernel, has_vres, len(casts), nc),
        grid=(1, nc + 1),
        in_specs=in_specs,
        out_specs=out_specs,
        out_shape=out_shape,
        scratch_shapes=scratch,
        compiler_params=_cparams("arbitrary", "arbitrary"),
        name="rwkv",
    )(*ins)
    n_main = 1 if has_vres else 2
    w16 = [o.reshape(w.shape[1], w.shape[2]) for o, (w, _) in zip(out[n_main:], casts)]
    return out[0], (None if has_vres else out[1]), w16


def _rope_kernel(q_ref, k_ref, v_ref, cos_ref, sin_ref, qs_ref, ko_ref, vt_ref, bias_ref, km_ref):
    i = pl.program_id(1)
    nb = km_ref.shape[0]
    blk = q_ref.shape[1]
    nh = q_ref.shape[-1] // ATTN_HEAD
    cos = cos_ref[...]
    sin = sin_ref[...]

    @pl.when(i == 0)
    def _():
        km_ref[...] = jnp.zeros_like(km_ref)

    blk_id = lax.broadcasted_iota(jnp.int32, (nb, blk), 0)
    past = blk_id < i
    qk_scale = (ATTN_HEAD ** -0.5) * LOG2E
    for h in range(nh):
        sl = slice(h * ATTN_HEAD, (h + 1) * ATTN_HEAD)
        q = q_ref[0, :, sl].astype(F32)
        k = k_ref[0, :, sl].astype(F32)
        qr = (q * cos + pltpu.roll(q, ATTN_HEAD // 2, axis=1) * sin).T
        kr = k * cos + pltpu.roll(k, ATTN_HEAD // 2, axis=1) * sin
        qs_ref[0, h, 0] = (qr * qk_scale).astype(BF16)
        ko_ref[0, h, 0] = kr.astype(BF16)
        vt_ref[0, h, 0] = v_ref[0, :, sl].astype(F32).T.astype(BF16)
        gate = jnp.where(past, _mm(km_ref[:, sl], qr, passes=3), -jnp.inf)
        rank = jnp.zeros((nb, blk), jnp.int32)
        for m in range(nb):
            gm = gate[m:m + 1, :]
            rank += ((gm > gate) | ((gm == gate) & (m < blk_id))).astype(jnp.int32)
        bias_ref[0, h, 0] = jnp.where(past & (rank < MOBA_TOPK), 0.0, -jnp.inf)
        km_new = jnp.sum(kr, axis=0, keepdims=True) * (1.0 / blk)
        km_ref[:, sl] = jnp.where(lax.broadcasted_iota(jnp.int32, (nb, ATTN_HEAD), 0) == i, km_new, km_ref[:, sl])


def _rope(z3, col0, width, cos2, sin2):
    bsz, s, _ = z3.shape
    assert s % MOBA_BLOCK == 0 and col0 % ATTN_HEAD == 0
    nb = s // MOBA_BLOCK
    nh = width // ATTN_HEAD
    blk = lambda j: pl.BlockSpec((pl.Element(1), pl.Element(MOBA_BLOCK), pl.Element(width)),
                                 lambda b, i: (b, i * MOBA_BLOCK, col0 + j * width))
    tab = pl.BlockSpec((MOBA_BLOCK, ATTN_HEAD), lambda b, i: (i, 0))
    t_spec = pl.BlockSpec((1, nh, 1, ATTN_HEAD, MOBA_BLOCK), lambda b, i: (b, 0, i, 0, 0))
    n_spec = pl.BlockSpec((1, nh, 1, MOBA_BLOCK, ATTN_HEAD), lambda b, i: (b, 0, i, 0, 0))
    return pl.pallas_call(
        _rope_kernel,
        grid=(bsz, nb),
        in_specs=[blk(0), blk(1), blk(2), tab, tab],
        out_specs=[t_spec, n_spec, t_spec,
                   pl.BlockSpec((1, nh, 1, nb, MOBA_BLOCK), lambda b, i: (b, 0, i, 0, 0))],
        out_shape=[jax.ShapeDtypeStruct((bsz, nh, nb, ATTN_HEAD, MOBA_BLOCK), BF16),
                   jax.ShapeDtypeStruct((bsz, nh, nb, MOBA_BLOCK, ATTN_HEAD), BF16),
                   jax.ShapeDtypeStruct((bsz, nh, nb, ATTN_HEAD, MOBA_BLOCK), BF16),
                   jax.ShapeDtypeStruct((bsz, nh, nb, nb, MOBA_BLOCK), F32)],
        scratch_shapes=[pltpu.VMEM((nb, width), F32)],
        compiler_params=_cparams("parallel", "arbitrary"),
        name="rope",
    )(z3, z3, z3, cos2, sin2)


def _moba_kernel(qs_ref, k_ref, vt_ref, bias_ref, o_ref):
    blk = MOBA_BLOCK
    dh = ATTN_HEAD
    hs = range(qs_ref.shape[1])
    qb = pl.program_id(2)
    neg = -jnp.inf
    qs = [qs_ref[0, h, 0] for h in hs]
    ki = lax.broadcasted_iota(jnp.int32, (blk, blk), 0)
    qi = lax.broadcasted_iota(jnp.int32, (blk, blk), 1)
    causal = ki <= qi

    def pipelined(work, stage):
        scores = lambda kb, h: jnp.dot(k_ref[0, h, kb], qs[h], preferred_element_type=F32)
        ahead = [scores(*w) for w in work[:MOBA_LOOKAHEAD]]
        for i, (kb, h) in enumerate(work):
            if i + MOBA_LOOKAHEAD < len(work):
                ahead.append(scores(*work[i + MOBA_LOOKAHEAD]))
            stage(kb, h, ahead[i])

    m_run, l_run, acc = [None] * len(hs), [None] * len(hs), [None] * len(hs)

    def own_block(kb, h, s):
        s = jnp.where(causal, s, neg)
        m_run[h] = jnp.max(s, axis=0, keepdims=True)
        p = jnp.exp2(s - m_run[h])
        l_run[h] = jnp.sum(p, axis=0, keepdims=True)
        acc[h] = jnp.dot(vt_ref[0, h, kb], p.astype(BF16), preferred_element_type=F32)

    pipelined([(qb, h) for h in hs], own_block)

    def past_blocks(kbs, carry):
        m_c, l_c, acc_c = (list(c) for c in carry)

        def stage(kb, h, s):
            b = bias_ref[0, h, 0, pl.ds(kb, 1), :]
            m_new = jnp.where(b == 0.0, jnp.maximum(m_c[h], jnp.max(s, axis=0, keepdims=True)), m_c[h])
            alpha = jnp.exp2(m_c[h] - m_new)
            p = jnp.exp2(s - (m_new - b))
            pv = jnp.dot(vt_ref[0, h, kb], p.astype(BF16), preferred_element_type=F32)
            m_c[h] = m_new
            l_c[h] = alpha * l_c[h] + jnp.sum(p, axis=0, keepdims=True)
            acc_c[h] = alpha * acc_c[h] + pv

        pipelined([(kb, h) for kb in kbs for h in hs], stage)
        return tuple(m_c), tuple(l_c), tuple(acc_c)

    u = MOBA_UNROLL
    carry = lax.fori_loop(0, qb // u, lambda j, c: past_blocks([j * u + i for i in range(u)], c),
                          (tuple(m_run), tuple(l_run), tuple(acc)))
    _, l_fin, acc = lax.fori_loop((qb // u) * u, qb, lambda kb, c: past_blocks([kb], c), carry)
    for h in hs:
        o_ref[0, :, h * dh:(h + 1) * dh] = (acc[h] / l_fin[h]).T.astype(o_ref.dtype)


def _moba(qs, k, vt, bias, heads_per_step=8):
    bsz, nh, nb, dh, blk = qs.shape
    hp = heads_per_step
    assert nh % hp == 0
    return pl.pallas_call(
        _moba_kernel,
        grid=(bsz, nh // hp, nb),
        in_specs=[pl.BlockSpec((1, hp, 1, dh, blk), lambda b, h, i: (b, h, i, 0, 0)),
                  pl.BlockSpec((1, hp, nb, blk, dh), lambda b, h, i: (b, h, 0, 0, 0)),
                  pl.BlockSpec((1, hp, nb, dh, blk), lambda b, h, i: (b, h, 0, 0, 0)),
                  pl.BlockSpec((1, hp, 1, nb, blk), lambda b, h, i: (b, h, i, 0, 0))],
        out_specs=pl.BlockSpec((1, blk, hp * dh), lambda b, h, i: (b, i, h)),
        out_shape=jax.ShapeDtypeStruct((bsz, nb * blk, nh * dh), BF16),
        compiler_params=_cparams("parallel", "parallel", "arbitrary"),
        name="moba",
    )(qs, k, vt, bias)


def _out_proj_kernel(yr_ref, ya_ref, wr_ref, wa_ref, x_ref, g_ref, o_ref):
    for r in range(0, x_ref.shape[0], ROW_CHUNK):
        rows = pl.ds(r, ROW_CHUNK)
        y = jnp.dot(yr_ref[rows, :], wr_ref[...], preferred_element_type=F32)
        y += jnp.dot(ya_ref[rows, :], wa_ref[...], preferred_element_type=F32)
        ms = jnp.mean(y * y, axis=-1, keepdims=True)
        o_ref[rows, :] = x_ref[rows, :] + y * lax.rsqrt(ms + NORM_EPS) * g_ref[...]


def _out_proj(y_r, y_a, w, x2, gain, tm=512):
    m, d = x2.shape
    cw = y_r.shape[1]
    assert y_a.shape[1] == cw and w.shape[0] == 2 * cw and m % tm == 0
    return pl.pallas_call(
        _out_proj_kernel,
        grid=(m // tm,),
        in_specs=[pl.BlockSpec((tm, cw), lambda i: (i, 0)), pl.BlockSpec((tm, cw), lambda i: (i, 0)),
                  pl.BlockSpec((cw, d), lambda i: (0, 0)), pl.BlockSpec((cw, d), lambda i: (1, 0)),
                  pl.BlockSpec((tm, d), lambda i: (i, 0)), pl.BlockSpec((1, d), lambda i: (0, 0))],
        out_specs=pl.BlockSpec((tm, d), lambda i: (i, 0)),
        out_shape=jax.ShapeDtypeStruct((m, d), F32),
        compiler_params=_cparams("parallel"),
        name="out_proj",
    )(y_r, y_a, w, w, x2, gain.reshape(1, d))


def _mlp_kernel(tf, x_ref, gpre_ref, wu_hbm, wd_hbm, gpost_ref, o_ref, h_ref, acc_ref, wu_buf, wd_buf, sem):
    i = pl.program_id(0)
    nf = wu_hbm.shape[1] // tf
    chunks = [pl.ds(r, ROW_CHUNK) for r in range(0, x_ref.shape[0], ROW_CHUNK)]

    def tile_copies(f, slot):
        return (pltpu.make_async_copy(wu_hbm.at[:, pl.ds(f * tf, tf)], wu_buf.at[slot], sem.at[0, slot]),
                pltpu.make_async_copy(wd_hbm.at[pl.ds(f * tf, tf), :], wd_buf.at[slot], sem.at[1, slot]))

    def part(h, slot):
        u = jnp.maximum(jnp.dot(h, wu_buf[slot], preferred_element_type=F32), 0.0)
        return jnp.dot((u * u).astype(BF16), wd_buf[slot], preferred_element_type=F32)

    @pl.when(i == 0)
    def _():
        for cp in tile_copies(0, 0):
            cp.start()

    for f in range(nf):
        slot = f % 2
        if f + 1 < nf:
            for cp in tile_copies(f + 1, 1 - slot):
                cp.start()
        else:
            @pl.when(i + 1 < pl.num_programs(0))
            def _():
                for cp in tile_copies(0, 1 - slot):
                    cp.start()
        for cp in tile_copies(f, slot):
            cp.wait()
        if f == 0:
            for rows in chunks:
                x = x_ref[rows, :]
                ms = jnp.mean(x * x, axis=-1, keepdims=True)
                h = (x * lax.rsqrt(ms + NORM_EPS) * gpre_ref[...]).astype(BF16)
                h_ref[rows, :] = h
                acc_ref[rows, :] = part(h, slot)
        elif f < nf - 1:
            acc_ref[...] += part(h_ref[...], slot)
        else:
            for rows in chunks:
                mlp = acc_ref[rows, :] + part(h_ref[rows, :], slot)
                ms = jnp.mean(mlp * mlp, axis=-1, keepdims=True)
                o_ref[rows, :] = x_ref[rows, :] + mlp * lax.rsqrt(ms + NORM_EPS) * gpost_ref[...]


def _mlp(x2, g_pre, w_up, w_down, g_post, tm=512, tf=1024):
    m, d = x2.shape
    dff = w_up.shape[1]
    nf = dff // tf
    assert m % tm == 0 and dff % tf == 0 and nf >= 2 and nf % 2 == 0 and tm % ROW_CHUNK == 0
    return pl.pallas_call(
        functools.partial(_mlp_kernel, tf),
        grid=(m // tm,),
        in_specs=[pl.BlockSpec((tm, d), lambda i: (i, 0)),
                  pl.BlockSpec((1, d), lambda i: (0, 0)),
                  pl.BlockSpec(memory_space=pl.ANY),
                  pl.BlockSpec(memory_space=pl.ANY),
                  pl.BlockSpec((1, d), lambda i: (0, 0))],
        out_specs=pl.BlockSpec((tm, d), lambda i: (i, 0)),
        out_shape=jax.ShapeDtypeStruct((m, d), F32),
        scratch_shapes=[pltpu.VMEM((tm, d), BF16), pltpu.VMEM((tm, d), F32),
                        pltpu.VMEM((2, d, tf), BF16), pltpu.VMEM((2, tf, d), BF16),
                        pltpu.SemaphoreType.DMA((2, 2))],
        compiler_params=_cparams("arbitrary"),
        name="mlp",
    )(x2, g_pre.reshape(1, d), w_up, w_down, g_post.reshape(1, d))


def _rope_tables(s):
    half = ATTN_HEAD // 2
    inv_freq = ROPE_THETA ** (-jnp.arange(half, dtype=F32) / half)
    ang = jnp.arange(s).astype(F32)[:, None] * inv_freq[None, :]
    cos, sin = jnp.cos(ang), jnp.sin(ang)
    return jnp.concatenate([cos, cos], axis=-1), jnp.concatenate([-sin, sin], axis=-1)


def kernel(x, norm_mix_pre, norm_mix_post, norm_mlp_pre, norm_mlp_post, w_in, w_in_vres, shift_mu, shift_mu_vres, decay_w0, decay_w2, iclr_a0, iclr_a2, vres_v0, vres_v2, gate_g2, k_k, k_a, r_k, lnx_gain, lnx_bias, w_out, w_up, w_down):
    bsz, s, d = x.shape
    depth = w_in.shape[0]
    c = decay_w0.shape[1]
    n_lora = DECAY_LORA + ICLR_LORA + GATE_LORA
    n_shift = 3 * c + n_lora
    ca = (w_in.shape[2] - n_shift) // 3
    cos2, sin2 = _rope_tables(s)
    w_in16 = w_in[0].astype(BF16)
    pad_v = VRES_PAD - VRES_LORA
    x2 = x.reshape(bsz * s, d)
    v_first = None
    for i in range(depth):
        if i == 0:
            z, vres = _norm_matmul(x2, norm_mix_pre[i], w_in16)[0], None
        else:
            w_v = jnp.pad(w_in_vres[i - 1], ((0, 0), (0, pad_v))).astype(BF16)
            z, z_v = _norm_matmul(x2, norm_mix_pre[i], w_in16, w_v)
            vres = (z_v.reshape(bsz, s, VRES_PAD), jnp.pad(shift_mu_vres[i - 1], (0, pad_v)), v_first,
                    vres_v0[i - 1], jnp.pad(vres_v2[i - 1], ((0, pad_v), (0, 0))))
        z3 = z.reshape(bsz, s, -1)

        casts = [(w_out, i), (w_up, i), (w_down, i)] + ([(w_in, i + 1)] if i + 1 < depth else [])
        y_r, v_layer, w16 = _rwkv(z3, c, shift_mu[i], decay_w0[i], decay_w2[i], iclr_a0[i], iclr_a2[i], gate_g2[i],
                                  k_k[i], k_a[i], vres, r_k[i].reshape(-1), lnx_gain[i], lnx_bias[i], casts)
        if i == 0:
            v_first = v_layer
        w_out16, w_up16, w_down16 = w16[:3]

        y_a = _moba(*_rope(z3, n_shift, ca, cos2, sin2))

        x2 = _out_proj(y_r.reshape(bsz * s, c), y_a.reshape(bsz * s, ca), w_out16, x2, norm_mix_post[i])
        x2 = _mlp(x2, norm_mlp_pre[i], w_up16, w_down16, norm_mlp_post[i])
        if i + 1 < depth:
            w_in16 = w16[3]
    return x2.reshape(bsz, s, d)
```

```python
import functools

import jax
import jax.numpy as jnp
from jax import lax
from jax.experimental import pallas as pl
from jax.experimental.pallas import tpu as pltpu

F32 = jnp.float32
BF16 = jnp.bfloat16

RWKV_HEAD = 64
DECAY_LORA = 64
ICLR_LORA = 64
VRES_LORA = 32
GATE_LORA = 128
ATTN_HEAD = 128
MOBA_BLOCK = 256
MOBA_TOPK = 3
ROPE_THETA = 10000.0
NORM_EPS = 1e-6
LNX_EPS = 64e-5
LOG2E = 1.4426950408889634

LANES = 128
BF16_SUBLANES = 16
VRES_PAD = LANES
WKV_CHUNK = 64
MOBA_LOOKAHEAD = 6
MOBA_UNROLL = 4
ROW_CHUNK = 256
VMEM_LIMIT = 56 * 1024 * 1024


def _cparams(*sem):
    return pltpu.CompilerParams(dimension_semantics=sem, vmem_limit_bytes=VMEM_LIMIT)


_NN = (((1,), (0,)), ((), ()))
_NT = (((1,), (1,)), ((), ()))
_TN = (((0,), (0,)), ((), ()))


def _split2(x):
    hi = x.astype(BF16)
    lo = (x - hi.astype(F32)).astype(BF16)
    return hi, lo


def _split3(x):
    hi = x.astype(BF16)
    r1 = x - hi.astype(F32)
    mid = r1.astype(BF16)
    lo = (r1 - mid.astype(F32)).astype(BF16)
    return hi, mid, lo


def _mm(a, b, dims=_NN, passes=1):
    d = lambda p, q: lax.dot_general(p, q, dims, preferred_element_type=F32)
    if passes == 1:
        return d(a.astype(BF16), b.astype(BF16))
    ah, al = _split2(a)
    bh, bl = _split2(b)
    return d(ah, bh) + (d(ah, bl) + d(al, bh))


def _sigmoid(x):
    return 1.0 / (1.0 + jnp.exp(-x))


def _softplus(x):
    return jnp.maximum(x, 0.0) + jnp.log(1.0 + jnp.exp(-jnp.abs(x)))


def _norm_matmul_kernel(has_extra, *refs):
    if has_extra:
        x_ref, g_ref, w_ref, we_ref, o_ref, oe_ref, h_ref = refs
    else:
        x_ref, g_ref, w_ref, o_ref, h_ref = refs

    first = pl.program_id(1) == 0

    @pl.when(first)
    def _():
        for r in range(0, x_ref.shape[0], ROW_CHUNK):
            rows = pl.ds(r, ROW_CHUNK)
            x = x_ref[rows, :]
            ms = jnp.mean(x * x, axis=-1, keepdims=True)
            h = (x * lax.rsqrt(ms + NORM_EPS) * g_ref[...]).astype(BF16)
            h_ref[rows, :] = h
            o_ref[rows, :] = jnp.dot(h, w_ref[...], preferred_element_type=F32).astype(o_ref.dtype)
            if has_extra:
                oe_ref[rows, :] = jnp.dot(h, we_ref[...], preferred_element_type=F32).astype(oe_ref.dtype)

    @pl.when(jnp.logical_not(first))
    def _():
        o_ref[...] = jnp.dot(h_ref[...], w_ref[...], preferred_element_type=F32).astype(o_ref.dtype)


def _norm_matmul(x2, gain, w, w_extra=None, tm=1024, tn=1280):
    m, d = x2.shape
    n = w.shape[1]
    assert m % tm == 0 and n % tn == 0
    has_extra = w_extra is not None
    ins = [x2, gain.reshape(1, d), w]
    in_specs = [pl.BlockSpec((tm, d), lambda i, j: (i, 0)),
                pl.BlockSpec((1, d), lambda i, j: (0, 0)),
                pl.BlockSpec((d, tn), lambda i, j: (0, j))]
    out_specs = [pl.BlockSpec((tm, tn), lambda i, j: (i, j))]
    out_shape = [jax.ShapeDtypeStruct((m, n), BF16)]
    if has_extra:
        ne = w_extra.shape[1]
        ins.append(w_extra)
        in_specs.append(pl.BlockSpec((d, ne), lambda i, j: (0, 0)))
        out_specs.append(pl.BlockSpec((tm, ne), lambda i, j: (i, 0)))
        out_shape.append(jax.ShapeDtypeStruct((m, ne), BF16))
    return pl.pallas_call(
        functools.partial(_norm_matmul_kernel, has_extra),
        grid=(m // tm, n // tn),
        in_specs=in_specs,
        out_specs=out_specs,
        out_shape=out_shape,
        scratch_shapes=[pltpu.VMEM((tm, d), BF16)],
        compiler_params=_cparams("parallel", "arbitrary"),
        name="norm_matmul",
    )(*ins)


def _dots(a_list, b_list, dims=_NN):
    return [lax.dot_general(a.astype(BF16), b.astype(BF16), dims, preferred_element_type=F32)
            for a, b in zip(a_list, b_list)]


def _pair_diag(x, even):
    return jnp.concatenate([jnp.where(even, x, 0.0), jnp.where(even, 0.0, x)], axis=0)


def _unit_lower_inverse(a_list, row_w, col_w, even):
    n = row_w.shape[0]
    lower = row_w > col_w
    base = lower & ((row_w >> 1) == (col_w >> 1))
    t = [jnp.where(row_w == col_w, 1.0, jnp.where(base, a, 0.0)) for a in a_list]
    sh = 1
    while (2 << sh) <= n:
        sub = lower & ((row_w >> (sh + 1)) == (col_w >> (sh + 1))) & ((row_w >> sh) != (col_w >> sh))
        off = [_pair_diag(jnp.where(sub, a, 0.0), even) for a in a_list]
        upd = _dots(_dots(t, off), [_pair_diag(x, even) for x in t])
        t = [x + u for x, u in zip(t, upd)]
        sh += 1
    return t


def _mm_exact_rhs_left(l_bf16, a):
    d = lambda p: lax.dot_general(l_bf16, p, _NN, preferred_element_type=F32)
    hi, mid, lo = _split3(a)
    return d(hi) + (d(mid) + d(lo))


def _rwkv_kernel(has_vres, n_cast, nc, *refs):
    it = iter(refs)
    take = lambda cnt: [next(it) for _ in range(cnt)]
    zm_ref, zl_ref, mum_ref, mul_ref, w0_ref, w2_ref, a0_ref, a2_ref, g2_ref, kk_ref, ka_ref = take(11)
    if has_vres:
        zv_ref, muv_ref, vf_ref, v0_ref, v2_ref = take(5)
    rk_ref, lg_ref, lb_ref = take(3)
    cast_in = take(n_cast)
    y_ref, = take(1)
    if not has_vres:
        vout_ref, = take(1)
    cast_out = take(n_cast)
    state_ref, cm_ref, cl_ref = take(3)
    if has_vres:
        cv_ref, = take(1)
    r_s, lw_s, k_s, v_s, an_s, b_s, g_s = operands = take(7)

    ci = pl.program_id(1)
    ln = WKV_CHUNK
    nb = zm_ref.shape[0]
    c = r_s.shape[2]
    n = RWKV_HEAD
    pw = 2 * n
    npair = c // pw
    ent = [(bi, slice(p * pw, (p + 1) * pw)) for bi in range(nb) for p in range(npair)]

    @pl.when(ci == 0)
    def _():
        for ref in [state_ref, cm_ref, cl_ref] + ([cv_ref] if has_vres else []) + operands:
            ref[...] = jnp.zeros_like(ref)

    rows = lax.broadcasted_iota(jnp.int32, (ln, ln), 0)
    cols = lax.broadcasted_iota(jnp.int32, (ln, ln), 1)
    tril = jnp.where(rows >= cols, 1.0, 0.0).astype(BF16)
    r, k, v, w_end, r_t, a_t, b_t, k_t, b_h, k_h = ([] for _ in range(10))
    for bi in range(nb):
        lw = lw_s[bi]
        cw = _mm_exact_rhs_left(tril, lw)
        cw_end = cw[ln - 1:ln, :]
        e_neg = jnp.exp(-cw)
        e_end = jnp.exp(cw_end - cw)
        bb = b_s[bi].astype(F32)
        r.append(r_s[bi].astype(F32))
        k.append(k_s[bi].astype(F32))
        v.append(v_s[bi].astype(F32))
        w_end.append(jnp.exp(cw_end))
        r_t.append(r[bi] * jnp.exp(cw))
        a_t.append(an_s[bi].astype(F32) * jnp.exp(cw - lw))
        b_t.append(bb * e_neg)
        k_t.append(k[bi] * e_neg)
        b_h.append(bb * e_end)
        k_h.append(k[bi] * e_end)

    lane = lax.broadcasted_iota(jnp.int32, (ln, pw), 1)
    row_w = lax.broadcasted_iota(jnp.int32, (ln, pw), 0)
    even = lane < n
    even2 = lax.broadcasted_iota(jnp.int32, (2 * ln, pw), 1) < n
    col_w = lane & (n - 1)
    strict_w = row_w > col_w
    incl_w = row_w >= col_w
    zeros_w = jnp.zeros((ln, pw), F32)
    diag = lambda x: _pair_diag(x, even)
    swap = lambda x: jnp.concatenate([x[x.shape[0] // 2:], x[:x.shape[0] // 2]], axis=0)

    ar_p = [jnp.concatenate([a_t[bi][:, ps], r_t[bi][:, ps]], axis=0) for bi, ps in ent]
    bk_p = [jnp.concatenate([b_t[bi][:, ps], k_t[bi][:, ps]], axis=0).astype(BF16) for bi, ps in ent]
    kb_p = [jnp.concatenate([k_t[bi][:, ps], b_t[bi][:, ps]], axis=0).astype(BF16) for bi, ps in ent]
    am_e = _dots([jnp.where(even2, x, 0.0) for x in ar_p], bk_p, _NT)
    am_o = _dots([jnp.where(even2, 0.0, x) for x in ar_p], kb_p, _NT)
    a_ab = [jnp.where(strict_w, jnp.where(even, e[:ln], o[:ln]), 0.0) for e, o in zip(am_e, am_o)]
    a_ak = [jnp.where(strict_w, jnp.where(even, o[:ln], e[:ln]), 0.0) for e, o in zip(am_e, am_o)]
    a_rb = [jnp.where(incl_w, jnp.where(even, e[ln:], o[ln:]), 0.0) for e, o in zip(am_e, am_o)]
    a_rk = [jnp.where(incl_w, jnp.where(even, o[ln:], e[ln:]), 0.0) for e, o in zip(am_e, am_o)]
    v_p = [v[bi][:, ps] for bi, ps in ent]
    v_d = [diag(x) for x in v_p]
    akv = _dots(a_ak, [swap(x) for x in v_d])
    t = _unit_lower_inverse(a_ab, row_w, col_w, even)
    rhs = [jnp.concatenate([diag(a_t[bi][:, ps]), diag(x)], axis=1) for (bi, ps), x in zip(ent, akv)]
    pq = _dots(t, rhs)
    ry = _dots([jnp.concatenate([x, y], axis=1) for x, y in zip(a_rb, a_rk)],
               [jnp.concatenate([jnp.concatenate([diag(x[:, :pw]), diag(x[:, pw:])], axis=1),
                                 jnp.concatenate([jnp.zeros((2 * ln, pw), F32), swap(u)], axis=1)], axis=0)
                for x, u in zip(pq, v_d)])
    pqv_p = [jnp.concatenate([x, jnp.concatenate([zeros_w, u], axis=1)], axis=0)
             for x, u in zip(pq, v_p)]
    bkh_p = [jnp.concatenate([b_h[bi][:, ps], k_h[bi][:, ps]], axis=0) for bi, ps in ent]
    mn_p = _dots(bkh_p, pqv_p, _TN)
    sq_r = lax.broadcasted_iota(jnp.int32, (pw, pw), 0)
    sq_c = lax.broadcasted_iota(jnp.int32, (pw, pw), 1)
    same_head = (sq_r < n) == (sq_c < n)
    lhs_p = [jnp.concatenate([r_t[bi][:, ps] + x[:, :pw],
                              jnp.where(sq_r == sq_c, jnp.broadcast_to(w_end[bi][:, ps], (pw, pw)),
                                        jnp.where(same_head, m[:, :pw], 0.0))], axis=0)
             for (bi, ps), x, m in zip(ent, ry, mn_p)]
    st = [state_ref[e] for e in range(len(ent))]
    upd = _dots(lhs_p, st)
    for e in range(len(ent)):
        state_ref[e] = upd[e][ln:, :] + jnp.where(same_head, mn_p[e][:, pw:], 0.0)
    y0_p = [x[:, pw:] for x in ry]

    hsum = lambda x: jnp.where(even, jnp.sum(jnp.where(even, x, 0.0), axis=-1, keepdims=True),
                               jnp.sum(jnp.where(even, 0.0, x), axis=-1, keepdims=True))
    inv_n = 1.0 / n
    yn = []
    for u, y0 in zip(upd, y0_p):
        y = u[:ln, :] + y0
        d = y - hsum(y) * inv_n
        yn.append(d * lax.rsqrt(hsum(d * d) * inv_n + LNX_EPS))
    for bi in range(nb):
        rk = r[bi] * k[bi] * rk_ref[...]
        mine = range(bi * npair, (bi + 1) * npair)
        bonus = jnp.concatenate([hsum(rk[:, ent[e][1]]) * v_p[e] for e in mine], axis=1)
        out = jnp.concatenate([yn[e] for e in mine], axis=1) * lg_ref[...] + lb_ref[...] + bonus
        y_ref[bi] = (out * g_s[bi].astype(F32)).astype(y_ref.dtype)

    advance = ci < nc - 1

    def shifted(z_ref, carry_ref, mu_ref):
        out = []
        for bi in range(nb):
            cur = z_ref[bi].astype(F32)
            first_row = lax.broadcasted_iota(jnp.int32, cur.shape, 0) == 0
            prev = jnp.where(first_row, carry_ref[bi:bi + 1, :], pltpu.roll(cur, 1, axis=0))
            carry_ref[bi:bi + 1, :] = jnp.where(advance, cur[ln - 1:ln, :], carry_ref[bi:bi + 1, :])
            out.append(cur + (prev - cur) * mu_ref[...])
        return jnp.concatenate(out, axis=0)

    zs = shifted(zm_ref, cm_ref, mum_ref)
    zl = shifted(zl_ref, cl_ref, mul_ref)
    r_n, k_n, v_n = zs[:, :c], zs[:, c:2 * c], zs[:, 2 * c:3 * c]
    o = 0
    wd = zl[:, o:o + DECAY_LORA]
    o += DECAY_LORA
    ad = zl[:, o:o + ICLR_LORA]
    o += ICLR_LORA
    gd = zl[:, o:o + GATE_LORA]
    w_log = -_softplus(-(w0_ref[...] + _mm(jnp.tanh(wd), w2_ref[...], passes=3))) - 0.5
    lw_n = -jnp.exp(w_log)
    a_n = _sigmoid(a0_ref[...] + _mm(ad, a2_ref[...]))
    g_n = _mm(_sigmoid(gd), g2_ref[...])
    if has_vres:
        vd = shifted(zv_ref, cv_ref, muv_ref)
        v_first = jnp.concatenate([vf_ref[bi] for bi in range(nb)], axis=0)
        v_n = v_n + (v_first - v_n) * _sigmoid(v0_ref[...] + _mm(vd, v2_ref[...]))
    kk = k_n * kk_ref[...]
    even_n = lax.broadcasted_iota(jnp.int32, (kk.shape[0], pw), 1) < n
    unit = []
    for p in range(npair):
        x = kk[:, p * pw:(p + 1) * pw]
        sq = x * x
        ss = jnp.where(even_n, jnp.sum(jnp.where(even_n, sq, 0.0), axis=-1, keepdims=True),
                       jnp.sum(jnp.where(even_n, 0.0, sq), axis=-1, keepdims=True))
        unit.append(x / jnp.maximum(jnp.sqrt(ss), 1e-12))
    kk = jnp.concatenate(unit, axis=1)
    k_n = k_n * (1.0 + (a_n - 1.0) * ka_ref[...])
    kka = kk * a_n
    for bi in range(nb):
        rs = slice(bi * ln, (bi + 1) * ln)
        r_s[bi] = r_n[rs]
        lw_s[bi] = lw_n[rs]
        k_s[bi] = k_n[rs]
        v_s[bi] = v_n[rs]
        an_s[bi] = -kk[rs]
        b_s[bi] = kka[rs]
        g_s[bi] = g_n[rs]
        if not has_vres:
            vout_ref[bi] = v_n[rs]

    for src, dst in zip(cast_in, cast_out):
        dst[...] = src[...].astype(dst.dtype)


def _rwkv(z3, c, mu, w0, w2, a0, a2, g2, k_k, k_a, vres, r_k, lnx_g, lnx_b, casts=()):
    bsz, s, _ = z3.shape
    n_lora = w2.shape[0] + a2.shape[0] + g2.shape[0]
    nc = s // WKV_CHUNK
    assert s % WKV_CHUNK == 0 and WKV_CHUNK == RWKV_HEAD and c % (2 * RWKV_HEAD) == 0
    assert (3 * c) % n_lora == 0 and n_lora % LANES == 0
    has_vres = vres is not None
    row = lambda a: a.reshape(1, -1)
    lcb = 3 * c // n_lora
    cur = lambda ci: jnp.minimum(ci, nc - 1)
    full = lambda a: pl.BlockSpec(a.shape, lambda hi, ci: (0,) * a.ndim)
    chunk = lambda w, cb=0: pl.BlockSpec((bsz, WKV_CHUNK, w), lambda hi, ci: (0, cur(ci), cb))
    ins = [z3, z3, row(mu[:3 * c]), row(mu[3 * c:]), row(w0), w2, row(a0), a2, g2, row(k_k), row(k_a)]
    in_specs = [chunk(3 * c), chunk(n_lora, lcb)] + [full(a) for a in ins[2:]]
    scratch = [pltpu.VMEM((bsz * c // (2 * RWKV_HEAD), 2 * RWKV_HEAD, 2 * RWKV_HEAD), F32),
               pltpu.VMEM((bsz, 3 * c), F32), pltpu.VMEM((bsz, n_lora), F32)]
    if has_vres:
        zv3, mu_v, v_first, v0, v2 = vres
        extra = [zv3, row(mu_v), v_first, row(v0), v2]
        ins += extra
        in_specs += [chunk(zv3.shape[-1]), full(extra[1]), chunk(c), full(extra[3]), full(extra[4])]
        scratch.append(pltpu.VMEM((bsz, zv3.shape[-1]), F32))
    tail = [row(r_k), row(lnx_g), row(lnx_b)]
    ins += tail
    in_specs += [full(a) for a in tail]
    scratch += [pltpu.VMEM((bsz, WKV_CHUNK, c), F32)] * 7
    out_specs = [pl.BlockSpec((bsz, WKV_CHUNK, c), lambda hi, ci: (0, jnp.maximum(ci - 1, 0), 0))]
    out_shape = [jax.ShapeDtypeStruct((bsz, s, c), BF16)]
    if not has_vres:
        out_specs.append(chunk(c))
        out_shape.append(jax.ShapeDtypeStruct((bsz, s, c), F32))
    for w, layer in casts:
        nl, rows, cols = w.shape
        assert rows % (BF16_SUBLANES * nc) == 0
        slab = rows // nc
        ins.append(w.reshape(nl, nc, slab, cols))
        in_specs.append(pl.BlockSpec((None, None, slab, cols), lambda hi, ci, layer=layer: (layer, cur(ci), 0, 0)))
        out_specs.append(pl.BlockSpec((None, slab, cols), lambda hi, ci: (cur(ci), 0, 0)))
        out_shape.append(jax.ShapeDtypeStruct((nc, slab, cols), BF16))
    out = pl.pallas_call(
        functools.partial(_rwkv_kernel, has_vres, len(casts), nc),
        grid=(1, nc + 1),
        in_specs=in_specs,
        out_specs=out_specs,
        out_shape=out_shape,
        scratch_shapes=scratch,
        compiler_params=_cparams("arbitrary", "arbitrary"),
        name="rwkv",
    )(*ins)
    n_main = 1 if has_vres else 2
    w16 = [o.reshape(w.shape[1], w.shape[2]) for o, (w, _) in zip(out[n_main:], casts)]
    return out[0], (None if has_vres else out[1]), w16


def _rope_kernel(q_ref, k_ref, v_ref, cos_ref, sin_ref, qs_ref, ko_ref, vt_ref, bias_ref, km_ref):
    i = pl.program_id(1)
    nb = km_ref.shape[0]
    blk = q_ref.shape[1]
    nh = q_ref.shape[-1] // ATTN_HEAD
    cos = cos_ref[...]
    sin = sin_ref[...]

    @pl.when(i == 0)
    def _():
        km_ref[...] = jnp.zeros_like(km_ref)

    blk_id = lax.broadcasted_iota(jnp.int32, (nb, blk), 0)
    past = blk_id < i
    qk_scale = (ATTN_HEAD ** -0.5) * LOG2E
    for h in range(nh):
        sl = slice(h * ATTN_HEAD, (h + 1) * ATTN_HEAD)
        q = q_ref[0, :, sl].astype(F32)
        k = k_ref[0, :, sl].astype(F32)
        qr = (q * cos + pltpu.roll(q, ATTN_HEAD // 2, axis=1) * sin).T
        kr = k * cos + pltpu.roll(k, ATTN_HEAD // 2, axis=1) * sin
        qs_ref[0, 0, h] = (qr * qk_scale).astype(BF16)
        ko_ref[0, 0, h] = kr.astype(BF16)
        vt_ref[0, 0, h] = v_ref[0, :, sl].astype(F32).T.astype(BF16)
        gate = jnp.where(past, _mm(km_ref[:, sl], qr, passes=3), -jnp.inf)
        rank = jnp.zeros((nb, blk), jnp.int32)
        for m in range(nb):
            gm = gate[m:m + 1, :]
            rank += ((gm > gate) | ((gm == gate) & (m < blk_id))).astype(jnp.int32)
        bias_ref[0, 0, h] = jnp.where(past & (rank < MOBA_TOPK), 0.0, -jnp.inf)
        km_new = jnp.sum(kr, axis=0, keepdims=True) * (1.0 / blk)
        km_ref[:, sl] = jnp.where(lax.broadcasted_iota(jnp.int32, (nb, ATTN_HEAD), 0) == i, km_new, km_ref[:, sl])


def _rope(z3, col0, width, cos2, sin2):
    bsz, s, _ = z3.shape
    assert s % MOBA_BLOCK == 0 and col0 % ATTN_HEAD == 0
    nb = s // MOBA_BLOCK
    nh = width // ATTN_HEAD
    blk = lambda j: pl.BlockSpec((pl.Element(1), pl.Element(MOBA_BLOCK), pl.Element(width)),
                                 lambda b, i: (b, i * MOBA_BLOCK, col0 + j * width))
    tab = pl.BlockSpec((MOBA_BLOCK, ATTN_HEAD), lambda b, i: (i, 0))
    t_spec = pl.BlockSpec((1, 1, nh, ATTN_HEAD, MOBA_BLOCK), lambda b, i: (b, i, 0, 0, 0))
    n_spec = pl.BlockSpec((1, 1, nh, MOBA_BLOCK, ATTN_HEAD), lambda b, i: (b, i, 0, 0, 0))
    return pl.pallas_call(
        _rope_kernel,
        grid=(bsz, nb),
        in_specs=[blk(0), blk(1), blk(2), tab, tab],
        out_specs=[t_spec, n_spec, t_spec,
                   pl.BlockSpec((1, 1, nh, nb, MOBA_BLOCK), lambda b, i: (b, i, 0, 0, 0))],
        out_shape=[jax.ShapeDtypeStruct((bsz, nb, nh, ATTN_HEAD, MOBA_BLOCK), BF16),
                   jax.ShapeDtypeStruct((bsz, nb, nh, MOBA_BLOCK, ATTN_HEAD), BF16),
                   jax.ShapeDtypeStruct((bsz, nb, nh, ATTN_HEAD, MOBA_BLOCK), BF16),
                   jax.ShapeDtypeStruct((bsz, nb, nh, nb, MOBA_BLOCK), F32)],
        scratch_shapes=[pltpu.VMEM((nb, width), F32)],
        compiler_params=_cparams("parallel", "arbitrary"),
        name="rope",
    )(z3, z3, z3, cos2, sin2)


def _moba_kernel(qs_ref, k_ref, vt_ref, bias_ref, o_ref):
    blk = MOBA_BLOCK
    dh = ATTN_HEAD
    hs = range(qs_ref.shape[2])
    qb = pl.program_id(2)
    neg = -jnp.inf
    qs = [qs_ref[0, 0, h] for h in hs]
    ki = lax.broadcasted_iota(jnp.int32, (blk, blk), 0)
    qi = lax.broadcasted_iota(jnp.int32, (blk, blk), 1)
    causal = ki <= qi

    def pipelined(work, stage):
        scores = lambda kb, h: jnp.dot(k_ref[0, kb, h], qs[h], preferred_element_type=F32)
        ahead = [scores(*w) for w in work[:MOBA_LOOKAHEAD]]
        for i, (kb, h) in enumerate(work):
            if i + MOBA_LOOKAHEAD < len(work):
                ahead.append(scores(*work[i + MOBA_LOOKAHEAD]))
            stage(kb, h, ahead[i])

    m_run, l_run, acc = [None] * len(hs), [None] * len(hs), [None] * len(hs)

    def own_block(kb, h, s):
        s = jnp.where(causal, s, neg)
        m_run[h] = jnp.max(s, axis=0, keepdims=True)
        p = jnp.exp2(s - m_run[h])
        l_run[h] = jnp.sum(p, axis=0, keepdims=True)
        acc[h] = jnp.dot(vt_ref[0, kb, h], p.astype(BF16), preferred_element_type=F32)

    pipelined([(qb, h) for h in hs], own_block)

    def past_blocks(kbs, carry):
        m_c, l_c, acc_c = (list(c) for c in carry)

        def stage(kb, h, s):
            b = bias_ref[0, 0, h, pl.ds(kb, 1), :]
            m_new = jnp.where(b == 0.0, jnp.maximum(m_c[h], jnp.max(s, axis=0, keepdims=True)), m_c[h])
            alpha = jnp.exp2(m_c[h] - m_new)
            p = jnp.exp2(s - (m_new - b))
            pv = jnp.dot(vt_ref[0, kb, h], p.astype(BF16), preferred_element_type=F32)
            m_c[h] = m_new
            l_c[h] = alpha * l_c[h] + jnp.sum(p, axis=0, keepdims=True)
            acc_c[h] = alpha * acc_c[h] + pv

        pipelined([(kb, h) for kb in kbs for h in hs], stage)
        return tuple(m_c), tuple(l_c), tuple(acc_c)

    u = MOBA_UNROLL
    carry = lax.fori_loop(0, qb // u, lambda j, c: past_blocks([j * u + i for i in range(u)], c),
                          (tuple(m_run), tuple(l_run), tuple(acc)))
    _, l_fin, acc = lax.fori_loop((qb // u) * u, qb, lambda kb, c: past_blocks([kb], c), carry)
    for h in hs:
        o_ref[0, :, h * dh:(h + 1) * dh] = (acc[h] / l_fin[h]).T.astype(o_ref.dtype)


def _moba(qs, k, vt, bias, heads_per_step=8):
    bsz, nb, nh, dh, blk = qs.shape
    hp = heads_per_step
    assert nh % hp == 0
    return pl.pallas_call(
        _moba_kernel,
        grid=(bsz, nh // hp, nb),
        in_specs=[pl.BlockSpec((1, 1, hp, dh, blk), lambda b, h, i: (b, i, h, 0, 0)),
                  pl.BlockSpec((1, nb, hp, blk, dh), lambda b, h, i: (b, 0, h, 0, 0)),
                  pl.BlockSpec((1, nb, hp, dh, blk), lambda b, h, i: (b, 0, h, 0, 0)),
                  pl.BlockSpec((1, 1, hp, nb, blk), lambda b, h, i: (b, i, h, 0, 0))],
        out_specs=pl.BlockSpec((1, blk, hp * dh), lambda b, h, i: (b, i, h)),
        out_shape=jax.ShapeDtypeStruct((bsz, nb * blk, nh * dh), BF16),
        compiler_params=_cparams("parallel", "parallel", "arbitrary"),
        name="moba",
    )(qs, k, vt, bias)


def _out_proj_kernel(yr_ref, ya_ref, wr_ref, wa_ref, x_ref, g_ref, o_ref):
    for r in range(0, x_ref.shape[0], ROW_CHUNK):
        rows = pl.ds(r, ROW_CHUNK)
        y = jnp.dot(yr_ref[rows, :], wr_ref[...], preferred_element_type=F32)
        y += jnp.dot(ya_ref[rows, :], wa_ref[...], preferred_element_type=F32)
        ms = jnp.mean(y * y, axis=-1, keepdims=True)
        o_ref[rows, :] = x_ref[rows, :] + y * lax.rsqrt(ms + NORM_EPS) * g_ref[...]


def _out_proj(y_r, y_a, w, x2, gain, tm=512):
    m, d = x2.shape
    cw = y_r.shape[1]
    assert y_a.shape[1] == cw and w.shape[0] == 2 * cw and m % tm == 0
    return pl.pallas_call(
        _out_proj_kernel,
        grid=(m // tm,),
        in_specs=[pl.BlockSpec((tm, cw), lambda i: (i, 0)), pl.BlockSpec((tm, cw), lambda i: (i, 0)),
                  pl.BlockSpec((cw, d), lambda i: (0, 0)), pl.BlockSpec((cw, d), lambda i: (1, 0)),
                  pl.BlockSpec((tm, d), lambda i: (i, 0)), pl.BlockSpec((1, d), lambda i: (0, 0))],
        out_specs=pl.BlockSpec((tm, d), lambda i: (i, 0)),
        out_shape=jax.ShapeDtypeStruct((m, d), F32),
        compiler_params=_cparams("parallel"),
        name="out_proj",
    )(y_r, y_a, w, w, x2, gain.reshape(1, d))


def _mlp_kernel(x_ref, gpre_ref, wu_ref, wd_ref, gpost_ref, o_ref, h_ref, acc_ref):
    f = pl.program_id(1)
    last = pl.num_programs(1) - 1
    chunks = [pl.ds(r, ROW_CHUNK) for r in range(0, x_ref.shape[0], ROW_CHUNK)]

    def part(h):
        u = jnp.maximum(jnp.dot(h, wu_ref[...], preferred_element_type=F32), 0.0)
        return jnp.dot((u * u).astype(BF16), wd_ref[...], preferred_element_type=F32)

    @pl.when(f == 0)
    def _():
        for rows in chunks:
            x = x_ref[rows, :]
            ms = jnp.mean(x * x, axis=-1, keepdims=True)
            h = (x * lax.rsqrt(ms + NORM_EPS) * gpre_ref[...]).astype(BF16)
            h_ref[rows, :] = h
            acc_ref[rows, :] = part(h)

    @pl.when((f > 0) & (f < last))
    def _():
        acc_ref[...] += part(h_ref[...])

    @pl.when(f == last)
    def _():
        for rows in chunks:
            mlp = acc_ref[rows, :] + part(h_ref[rows, :])
            ms = jnp.mean(mlp * mlp, axis=-1, keepdims=True)
            o_ref[rows, :] = x_ref[rows, :] + mlp * lax.rsqrt(ms + NORM_EPS) * gpost_ref[...]


def _mlp(x2, g_pre, w_up, w_down, g_post, tm=512, tf=1024):
    m, d = x2.shape
    dff = w_up.shape[1]
    assert m % tm == 0 and dff % tf == 0 and dff // tf >= 2 and tm % ROW_CHUNK == 0
    return pl.pallas_call(
        _mlp_kernel,
        grid=(m // tm, dff // tf),
        in_specs=[pl.BlockSpec((tm, d), lambda i, f: (i, 0)),
                  pl.BlockSpec((1, d), lambda i, f: (0, 0)),
                  pl.BlockSpec((d, tf), lambda i, f: (0, f)),
                  pl.BlockSpec((tf, d), lambda i, f: (f, 0)),
                  pl.BlockSpec((1, d), lambda i, f: (0, 0))],
        out_specs=pl.BlockSpec((tm, d), lambda i, f: (i, 0)),
        out_shape=jax.ShapeDtypeStruct((m, d), F32),
        scratch_shapes=[pltpu.VMEM((tm, d), BF16), pltpu.VMEM((tm, d), F32)],
        compiler_params=_cparams("parallel", "arbitrary"),
        name="mlp",
    )(x2, g_pre.reshape(1, d), w_up, w_down, g_post.reshape(1, d))


def _rope_tables(s):
    half = ATTN_HEAD // 2
    inv_freq = ROPE_THETA ** (-jnp.arange(half, dtype=F32) / half)
    ang = jnp.arange(s).astype(F32)[:, None] * inv_freq[None, :]
    cos, sin = jnp.cos(ang), jnp.sin(ang)
    return jnp.concatenate([cos, cos], axis=-1), jnp.concatenate([-sin, sin], axis=-1)


def kernel(x, norm_mix_pre, norm_mix_post, norm_mlp_pre, norm_mlp_post, w_in, w_in_vres, shift_mu, shift_mu_vres, decay_w0, decay_w2, iclr_a0, iclr_a2, vres_v0, vres_v2, gate_g2, k_k, k_a, r_k, lnx_gain, lnx_bias, w_out, w_up, w_down):
    bsz, s, d = x.shape
    depth = w_in.shape[0]
    c = decay_w0.shape[1]
    n_lora = DECAY_LORA + ICLR_LORA + GATE_LORA
    n_shift = 3 * c + n_lora
    ca = (w_in.shape[2] - n_shift) // 3
    cos2, sin2 = _rope_tables(s)
    w_in16 = w_in[0].astype(BF16)
    pad_v = VRES_PAD - VRES_LORA
    x2 = x.reshape(bsz * s, d)
    v_first = None
    for i in range(depth):
        if i == 0:
            z, vres = _norm_matmul(x2, norm_mix_pre[i], w_in16)[0], None
        else:
            w_v = jnp.pad(w_in_vres[i - 1], ((0, 0), (0, pad_v))).astype(BF16)
            z, z_v = _norm_matmul(x2, norm_mix_pre[i], w_in16, w_v)
            vres = (z_v.reshape(bsz, s, VRES_PAD), jnp.pad(shift_mu_vres[i - 1], (0, pad_v)), v_first,
                    vres_v0[i - 1], jnp.pad(vres_v2[i - 1], ((0, pad_v), (0, 0))))
        z3 = z.reshape(bsz, s, -1)

        casts = [(w_out, i), (w_up, i), (w_down, i)] + ([(w_in, i + 1)] if i + 1 < depth else [])
        y_r, v_layer, w16 = _rwkv(z3, c, shift_mu[i], decay_w0[i], decay_w2[i], iclr_a0[i], iclr_a2[i], gate_g2[i],
                                  k_k[i], k_a[i], vres, r_k[i].reshape(-1), lnx_gain[i], lnx_bias[i], casts)
        if i == 0:
            v_first = v_layer
        w_out16, w_up16, w_down16 = w16[:3]

        y_a = _moba(*_rope(z3, n_shift, ca, cos2, sin2))

        x2 = _out_proj(y_r.reshape(bsz * s, c), y_a.reshape(bsz * s, ca), w_out16, x2, norm_mix_post[i])
        x2 = _mlp(x2, norm_mlp_pre[i], w_up16, w_down16, norm_mlp_post[i])
        if i + 1 < depth:
            w_in16 = w16[3]
    return x2.reshape(bsz, s, d)
```

```python
import functools

import jax
import jax.numpy as jnp
from jax import lax
from jax.experimental import pallas as pl
from jax.experimental.pallas import tpu as pltpu

F32 = jnp.float32
BF16 = jnp.bfloat16

RWKV_HEAD = 64
DECAY_LORA = 64
ICLR_LORA = 64
VRES_LORA = 32
GATE_LORA = 128
ATTN_HEAD = 128
MOBA_BLOCK = 256
MOBA_TOPK = 3
ROPE_THETA = 10000.0
NORM_EPS = 1e-6
LNX_EPS = 64e-5
LOG2E = 1.4426950408889634

LANES = 128
BF16_SUBLANES = 16
VRES_PAD = LANES
WKV_CHUNK = 64
MOBA_LOOKAHEAD = 6
MOBA_UNROLL = 4
ROW_CHUNK = 256
VMEM_LIMIT = 56 * 1024 * 1024


def _cparams(*sem):
    return pltpu.CompilerParams(dimension_semantics=sem, vmem_limit_bytes=VMEM_LIMIT)


_NN = (((1,), (0,)), ((), ()))
_NT = (((1,), (1,)), ((), ()))
_TN = (((0,), (0,)), ((), ()))


def _split2(x):
    hi = x.astype(BF16)
    lo = (x - hi.astype(F32)).astype(BF16)
    return hi, lo


def _split3(x):
    hi = x.astype(BF16)
    r1 = x - hi.astype(F32)
    mid = r1.astype(BF16)
    lo = (r1 - mid.astype(F32)).astype(BF16)
    return hi, mid, lo


def _mm(a, b, dims=_NN, passes=1):
    d = lambda p, q: lax.dot_general(p, q, dims, preferred_element_type=F32)
    if passes == 1:
        return d(a.astype(BF16), b.astype(BF16))
    ah, al = _split2(a)
    bh, bl = _split2(b)
    return d(ah, bh) + (d(ah, bl) + d(al, bh))


def _sigmoid(x):
    return 1.0 / (1.0 + jnp.exp(-x))


def _softplus(x):
    return jnp.maximum(x, 0.0) + jnp.log(1.0 + jnp.exp(-jnp.abs(x)))


def _norm_matmul_kernel(has_extra, tn, *refs):
    if has_extra:
        x_ref, g_ref, w_ref, we_ref, o_ref, oe_ref = refs
    else:
        x_ref, g_ref, w_ref, o_ref = refs
    for r in range(0, x_ref.shape[0], ROW_CHUNK):
        rows = pl.ds(r, ROW_CHUNK)
        x = x_ref[rows, :]
        ms = jnp.mean(x * x, axis=-1, keepdims=True)
        h = (x * lax.rsqrt(ms + NORM_EPS) * g_ref[...]).astype(BF16)
        for c0 in range(0, w_ref.shape[1], tn):
            o_ref[rows, c0:c0 + tn] = jnp.dot(h, w_ref[:, c0:c0 + tn],
                                              preferred_element_type=F32).astype(o_ref.dtype)
        if has_extra:
            oe_ref[rows, :] = jnp.dot(h, we_ref[...], preferred_element_type=F32).astype(oe_ref.dtype)


def _norm_matmul(x2, gain, w, w_extra=None, tm=512, tn=1280):
    m, d = x2.shape
    n = w.shape[1]
    assert m % tm == 0 and n % tn == 0 and tm % ROW_CHUNK == 0
    has_extra = w_extra is not None
    once = pl.Buffered(1)
    ins = [x2, gain.reshape(1, d), w]
    in_specs = [pl.BlockSpec((tm, d), lambda i: (i, 0)),
                pl.BlockSpec((1, d), lambda i: (0, 0)),
                pl.BlockSpec((d, n), lambda i: (0, 0), pipeline_mode=once)]
    out_specs = [pl.BlockSpec((tm, n), lambda i: (i, 0))]
    out_shape = [jax.ShapeDtypeStruct((m, n), BF16)]
    if has_extra:
        ne = w_extra.shape[1]
        ins.append(w_extra)
        in_specs.append(pl.BlockSpec((d, ne), lambda i: (0, 0), pipeline_mode=once))
        out_specs.append(pl.BlockSpec((tm, ne), lambda i: (i, 0)))
        out_shape.append(jax.ShapeDtypeStruct((m, ne), BF16))
    return pl.pallas_call(
        functools.partial(_norm_matmul_kernel, has_extra, tn),
        grid=(m // tm,),
        in_specs=in_specs,
        out_specs=out_specs,
        out_shape=out_shape,
        compiler_params=_cparams("parallel"),
        name="norm_matmul",
    )(*ins)


def _dots(a_list, b_list, dims=_NN):
    return [lax.dot_general(a.astype(BF16), b.astype(BF16), dims, preferred_element_type=F32)
            for a, b in zip(a_list, b_list)]


def _pair_diag(x, even):
    return jnp.concatenate([jnp.where(even, x, 0.0), jnp.where(even, 0.0, x)], axis=0)


def _unit_lower_inverse(a_list, row_w, col_w, even):
    n = row_w.shape[0]
    lower = row_w > col_w
    base = lower & ((row_w >> 1) == (col_w >> 1))
    t = [jnp.where(row_w == col_w, 1.0, jnp.where(base, a, 0.0)) for a in a_list]
    sh = 1
    while (2 << sh) <= n:
        sub = lower & ((row_w >> (sh + 1)) == (col_w >> (sh + 1))) & ((row_w >> sh) != (col_w >> sh))
        off = [_pair_diag(jnp.where(sub, a, 0.0), even) for a in a_list]
        upd = _dots(_dots(t, off), [_pair_diag(x, even) for x in t])
        t = [x + u for x, u in zip(t, upd)]
        sh += 1
    return t


def _mm_exact_rhs_left(l_bf16, a):
    d = lambda p: lax.dot_general(l_bf16, p, _NN, preferred_element_type=F32)
    hi, mid, lo = _split3(a)
    return d(hi) + (d(mid) + d(lo))


def _rwkv_kernel(has_vres, n_cast, nc, *refs):
    it = iter(refs)
    take = lambda cnt: [next(it) for _ in range(cnt)]
    zm_ref, zl_ref, mum_ref, mul_ref, w0_ref, w2_ref, a0_ref, a2_ref, g2_ref, kk_ref, ka_ref = take(11)
    if has_vres:
        zv_ref, muv_ref, vf_ref, v0_ref, v2_ref = take(5)
    rk_ref, lg_ref, lb_ref = take(3)
    cast_in = take(n_cast)
    y_ref, = take(1)
    if not has_vres:
        vout_ref, = take(1)
    cast_out = take(n_cast)
    state_ref, cm_ref, cl_ref = take(3)
    if has_vres:
        cv_ref, = take(1)
    r_s, lw_s, k_s, v_s, an_s, b_s, g_s = operands = take(7)

    ci = pl.program_id(1)
    ln = WKV_CHUNK
    nb = zm_ref.shape[0]
    c = r_s.shape[2]
    n = RWKV_HEAD
    pw = 2 * n
    npair = c // pw
    ent = [(bi, slice(p * pw, (p + 1) * pw)) for bi in range(nb) for p in range(npair)]

    @pl.when(ci == 0)
    def _():
        for ref in [state_ref, cm_ref, cl_ref] + ([cv_ref] if has_vres else []) + operands:
            ref[...] = jnp.zeros_like(ref)

    rows = lax.broadcasted_iota(jnp.int32, (ln, ln), 0)
    cols = lax.broadcasted_iota(jnp.int32, (ln, ln), 1)
    tril = jnp.where(rows >= cols, 1.0, 0.0).astype(BF16)
    r, k, v, w_end, r_t, a_t, b_t, k_t, b_h, k_h = ([] for _ in range(10))
    for bi in range(nb):
        lw = lw_s[bi]
        cw = _mm_exact_rhs_left(tril, lw)
        cw_end = cw[ln - 1:ln, :]
        e_neg = jnp.exp(-cw)
        e_end = jnp.exp(cw_end - cw)
        bb = b_s[bi].astype(F32)
        r.append(r_s[bi].astype(F32))
        k.append(k_s[bi].astype(F32))
        v.append(v_s[bi].astype(F32))
        w_end.append(jnp.exp(cw_end))
        r_t.append(r[bi] * jnp.exp(cw))
        a_t.append(an_s[bi].astype(F32) * jnp.exp(cw - lw))
        b_t.append(bb * e_neg)
        k_t.append(k[bi] * e_neg)
        b_h.append(bb * e_end)
        k_h.append(k[bi] * e_end)

    lane = lax.broadcasted_iota(jnp.int32, (ln, pw), 1)
    row_w = lax.broadcasted_iota(jnp.int32, (ln, pw), 0)
    even = lane < n
    even2 = lax.broadcasted_iota(jnp.int32, (2 * ln, pw), 1) < n
    col_w = lane & (n - 1)
    strict_w = row_w > col_w
    incl_w = row_w >= col_w
    zeros_w = jnp.zeros((ln, pw), F32)
    diag = lambda x: _pair_diag(x, even)
    swap = lambda x: jnp.concatenate([x[x.shape[0] // 2:], x[:x.shape[0] // 2]], axis=0)

    ar_p = [jnp.concatenate([a_t[bi][:, ps], r_t[bi][:, ps]], axis=0) for bi, ps in ent]
    bk_p = [jnp.concatenate([b_t[bi][:, ps], k_t[bi][:, ps]], axis=0).astype(BF16) for bi, ps in ent]
    kb_p = [jnp.concatenate([k_t[bi][:, ps], b_t[bi][:, ps]], axis=0).astype(BF16) for bi, ps in ent]
    am_e = _dots([jnp.where(even2, x, 0.0) for x in ar_p], bk_p, _NT)
    am_o = _dots([jnp.where(even2, 0.0, x) for x in ar_p], kb_p, _NT)
    a_ab = [jnp.where(strict_w, jnp.where(even, e[:ln], o[:ln]), 0.0) for e, o in zip(am_e, am_o)]
    a_ak = [jnp.where(strict_w, jnp.where(even, o[:ln], e[:ln]), 0.0) for e, o in zip(am_e, am_o)]
    a_rb = [jnp.where(incl_w, jnp.where(even, e[ln:], o[ln:]), 0.0) for e, o in zip(am_e, am_o)]
    a_rk = [jnp.where(incl_w, jnp.where(even, o[ln:], e[ln:]), 0.0) for e, o in zip(am_e, am_o)]
    v_p = [v[bi][:, ps] for bi, ps in ent]
    v_d = [diag(x) for x in v_p]
    akv = _dots(a_ak, [swap(x) for x in v_d])
    t = _unit_lower_inverse(a_ab, row_w, col_w, even)
    rhs = [jnp.concatenate([diag(a_t[bi][:, ps]), diag(x)], axis=1) for (bi, ps), x in zip(ent, akv)]
    pq = _dots(t, rhs)
    ry = _dots([jnp.concatenate([x, y], axis=1) for x, y in zip(a_rb, a_rk)],
               [jnp.concatenate([jnp.concatenate([diag(x[:, :pw]), diag(x[:, pw:])], axis=1),
                                 jnp.concatenate([jnp.zeros((2 * ln, pw), F32), swap(u)], axis=1)], axis=0)
                for x, u in zip(pq, v_d)])
    pqv_p = [jnp.concatenate([x, jnp.concatenate([zeros_w, u], axis=1)], axis=0)
             for x, u in zip(pq, v_p)]
    bkh_p = [jnp.concatenate([b_h[bi][:, ps], k_h[bi][:, ps]], axis=0) for bi, ps in ent]
    mn_p = _dots(bkh_p, pqv_p, _TN)
    sq_r = lax.broadcasted_iota(jnp.int32, (pw, pw), 0)
    sq_c = lax.broadcasted_iota(jnp.int32, (pw, pw), 1)
    same_head = (sq_r < n) == (sq_c < n)
    lhs_p = [jnp.concatenate([r_t[bi][:, ps] + x[:, :pw],
                              jnp.where(sq_r == sq_c, jnp.broadcast_to(w_end[bi][:, ps], (pw, pw)),
                                        jnp.where(same_head, m[:, :pw], 0.0))], axis=0)
             for (bi, ps), x, m in zip(ent, ry, mn_p)]
    st = [state_ref[e] for e in range(len(ent))]
    upd = _dots(lhs_p, st)
    for e in range(len(ent)):
        state_ref[e] = upd[e][ln:, :] + jnp.where(same_head, mn_p[e][:, pw:], 0.0)
    y0_p = [x[:, pw:] for x in ry]

    hsum = lambda x: jnp.where(even, jnp.sum(jnp.where(even, x, 0.0), axis=-1, keepdims=True),
                               jnp.sum(jnp.where(even, 0.0, x), axis=-1, keepdims=True))
    inv_n = 1.0 / n
    yn = []
    for u, y0 in zip(upd, y0_p):
        y = u[:ln, :] + y0
        d = y - hsum(y) * inv_n
        yn.append(d * lax.rsqrt(hsum(d * d) * inv_n + LNX_EPS))
    for bi in range(nb):
        rk = r[bi] * k[bi] * rk_ref[...]
        mine = range(bi * npair, (bi + 1) * npair)
        bonus = jnp.concatenate([hsum(rk[:, ent[e][1]]) * v_p[e] for e in mine], axis=1)
        out = jnp.concatenate([yn[e] for e in mine], axis=1) * lg_ref[...] + lb_ref[...] + bonus
        y_ref[bi] = (out * g_s[bi].astype(F32)).astype(y_ref.dtype)

    advance = ci < nc - 1

    def shifted(z_ref, carry_ref, mu_ref):
        out = []
        for bi in range(nb):
            cur = z_ref[bi].astype(F32)
            first_row = lax.broadcasted_iota(jnp.int32, cur.shape, 0) == 0
            prev = jnp.where(first_row, carry_ref[bi:bi + 1, :], pltpu.roll(cur, 1, axis=0))
            carry_ref[bi:bi + 1, :] = jnp.where(advance, cur[ln - 1:ln, :], carry_ref[bi:bi + 1, :])
            out.append(cur + (prev - cur) * mu_ref[...])
        return jnp.concatenate(out, axis=0)

    zs = shifted(zm_ref, cm_ref, mum_ref)
    zl = shifted(zl_ref, cl_ref, mul_ref)
    r_n, k_n, v_n = zs[:, :c], zs[:, c:2 * c], zs[:, 2 * c:3 * c]
    o = 0
    wd = zl[:, o:o + DECAY_LORA]
    o += DECAY_LORA
    ad = zl[:, o:o + ICLR_LORA]
    o += ICLR_LORA
    gd = zl[:, o:o + GATE_LORA]
    w_log = -_softplus(-(w0_ref[...] + _mm(jnp.tanh(wd), w2_ref[...], passes=3))) - 0.5
    lw_n = -jnp.exp(w_log)
    a_n = _sigmoid(a0_ref[...] + _mm(ad, a2_ref[...]))
    g_n = _mm(_sigmoid(gd), g2_ref[...])
    if has_vres:
        vd = shifted(zv_ref, cv_ref, muv_ref)
        v_first = jnp.concatenate([vf_ref[bi] for bi in range(nb)], axis=0)
        v_n = v_n + (v_first - v_n) * _sigmoid(v0_ref[...] + _mm(vd, v2_ref[...]))
    kk = k_n * kk_ref[...]
    even_n = lax.broadcasted_iota(jnp.int32, (kk.shape[0], pw), 1) < n
    unit = []
    for p in range(npair):
        x = kk[:, p * pw:(p + 1) * pw]
        sq = x * x
        ss = jnp.where(even_n, jnp.sum(jnp.where(even_n, sq, 0.0), axis=-1, keepdims=True),
                       jnp.sum(jnp.where(even_n, 0.0, sq), axis=-1, keepdims=True))
        unit.append(x / jnp.maximum(jnp.sqrt(ss), 1e-12))
    kk = jnp.concatenate(unit, axis=1)
    k_n = k_n * (1.0 + (a_n - 1.0) * ka_ref[...])
    kka = kk * a_n
    for bi in range(nb):
        rs = slice(bi * ln, (bi + 1) * ln)
        r_s[bi] = r_n[rs]
        lw_s[bi] = lw_n[rs]
        k_s[bi] = k_n[rs]
        v_s[bi] = v_n[rs]
        an_s[bi] = -kk[rs]
        b_s[bi] = kka[rs]
        g_s[bi] = g_n[rs]
        if not has_vres:
            vout_ref[bi] = v_n[rs]

    for src, dst in zip(cast_in, cast_out):
        dst[...] = src[...].astype(dst.dtype)


def _rwkv(z3, c, mu, w0, w2, a0, a2, g2, k_k, k_a, vres, r_k, lnx_g, lnx_b, casts=()):
    bsz, s, _ = z3.shape
    n_lora = w2.shape[0] + a2.shape[0] + g2.shape[0]
    nc = s // WKV_CHUNK
    assert s % WKV_CHUNK == 0 and WKV_CHUNK == RWKV_HEAD and c % (2 * RWKV_HEAD) == 0
    assert (3 * c) % n_lora == 0 and n_lora % LANES == 0
    has_vres = vres is not None
    row = lambda a: a.reshape(1, -1)
    lcb = 3 * c // n_lora
    cur = lambda ci: jnp.minimum(ci, nc - 1)
    full = lambda a: pl.BlockSpec(a.shape, lambda hi, ci: (0,) * a.ndim)
    chunk = lambda w, cb=0: pl.BlockSpec((bsz, WKV_CHUNK, w), lambda hi, ci: (0, cur(ci), cb))
    ins = [z3, z3, row(mu[:3 * c]), row(mu[3 * c:]), row(w0), w2, row(a0), a2, g2, row(k_k), row(k_a)]
    in_specs = [chunk(3 * c), chunk(n_lora, lcb)] + [full(a) for a in ins[2:]]
    scratch = [pltpu.VMEM((bsz * c // (2 * RWKV_HEAD), 2 * RWKV_HEAD, 2 * RWKV_HEAD), F32),
               pltpu.VMEM((bsz, 3 * c), F32), pltpu.VMEM((bsz, n_lora), F32)]
    if has_vres:
        zv3, mu_v, v_first, v0, v2 = vres
        extra = [zv3, row(mu_v), v_first, row(v0), v2]
        ins += extra
        in_specs += [chunk(zv3.shape[-1]), full(extra[1]), chunk(c), full(extra[3]), full(extra[4])]
        scratch.append(pltpu.VMEM((bsz, zv3.shape[-1]), F32))
    tail = [row(r_k), row(lnx_g), row(lnx_b)]
    ins += tail
    in_specs += [full(a) for a in tail]
    scratch += [pltpu.VMEM((bsz, WKV_CHUNK, c), F32)] * 7
    out_specs = [pl.BlockSpec((bsz, WKV_CHUNK, c), lambda hi, ci: (0, jnp.maximum(ci - 1, 0), 0))]
    out_shape = [jax.ShapeDtypeStruct((bsz, s, c), BF16)]
    if not has_vres:
        out_specs.append(chunk(c))
        out_shape.append(jax.ShapeDtypeStruct((bsz, s, c), F32))
    for w, layer in casts:
        nl, rows, cols = w.shape
        assert rows % (BF16_SUBLANES * nc) == 0
        slab = rows // nc
        ins.append(w.reshape(nl, nc, slab, cols))
        in_specs.append(pl.BlockSpec((None, None, slab, cols), lambda hi, ci, layer=layer: (layer, cur(ci), 0, 0)))
        out_specs.append(pl.BlockSpec((None, slab, cols), lambda hi, ci: (cur(ci), 0, 0)))
        out_shape.append(jax.ShapeDtypeStruct((nc, slab, cols), BF16))
    out = pl.pallas_call(
        functools.partial(_rwkv_kernel, has_vres, len(casts), nc),
        grid=(1, nc + 1),
        in_specs=in_specs,
        out_specs=out_specs,
        out_shape=out_shape,
        scratch_shapes=scratch,
        compiler_params=_cparams("arbitrary", "arbitrary"),
        name="rwkv",
    )(*ins)
    n_main = 1 if has_vres else 2
    w16 = [o.reshape(w.shape[1], w.shape[2]) for o, (w, _) in zip(out[n_main:], casts)]
    return out[0], (None if has_vres else out[1]), w16


def _rope_kernel(q_ref, k_ref, v_ref, cos_ref, sin_ref, qs_ref, ko_ref, vt_ref, bias_ref, km_ref):
    i = pl.program_id(1)
    nb = km_ref.shape[0]
    blk = q_ref.shape[1]
    nh = q_ref.shape[-1] // ATTN_HEAD
    cos = cos_ref[...]
    sin = sin_ref[...]

    @pl.when(i == 0)
    def _():
        km_ref[...] = jnp.zeros_like(km_ref)

    blk_id = lax.broadcasted_iota(jnp.int32, (nb, blk), 0)
    past = blk_id < i
    qk_scale = (ATTN_HEAD ** -0.5) * LOG2E
    for h in range(nh):
        sl = slice(h * ATTN_HEAD, (h + 1) * ATTN_HEAD)
        q = q_ref[0, :, sl].astype(F32)
        k = k_ref[0, :, sl].astype(F32)
        qr = (q * cos + pltpu.roll(q, ATTN_HEAD // 2, axis=1) * sin).T
        kr = k * cos + pltpu.roll(k, ATTN_HEAD // 2, axis=1) * sin
        qs_ref[0, 0, h] = (qr * qk_scale).astype(BF16)
        ko_ref[0, 0, h] = kr.astype(BF16)
        vt_ref[0, 0, h] = v_ref[0, :, sl].astype(F32).T.astype(BF16)
        gate = jnp.where(past, _mm(km_ref[:, sl], qr, passes=3), -jnp.inf)
        rank = jnp.zeros((nb, blk), jnp.int32)
        for m in range(nb):
            gm = gate[m:m + 1, :]
            rank += ((gm > gate) | ((gm == gate) & (m < blk_id))).astype(jnp.int32)
        bias_ref[0, 0, h] = jnp.where(past & (rank < MOBA_TOPK), 0.0, -jnp.inf)
        km_new = jnp.sum(kr, axis=0, keepdims=True) * (1.0 / blk)
        km_ref[:, sl] = jnp.where(lax.broadcasted_iota(jnp.int32, (nb, ATTN_HEAD), 0) == i, km_new, km_ref[:, sl])


def _rope(z3, col0, width, cos2, sin2):
    bsz, s, _ = z3.shape
    assert s % MOBA_BLOCK == 0 and col0 % ATTN_HEAD == 0
    nb = s // MOBA_BLOCK
    nh = width // ATTN_HEAD
    blk = lambda j: pl.BlockSpec((pl.Element(1), pl.Element(MOBA_BLOCK), pl.Element(width)),
                                 lambda b, i: (b, i * MOBA_BLOCK, col0 + j * width))
    tab = pl.BlockSpec((MOBA_BLOCK, ATTN_HEAD), lambda b, i: (i, 0))
    t_spec = pl.BlockSpec((1, 1, nh, ATTN_HEAD, MOBA_BLOCK), lambda b, i: (b, i, 0, 0, 0))
    n_spec = pl.BlockSpec((1, 1, nh, MOBA_BLOCK, ATTN_HEAD), lambda b, i: (b, i, 0, 0, 0))
    return pl.pallas_call(
        _rope_kernel,
        grid=(bsz, nb),
        in_specs=[blk(0), blk(1), blk(2), tab, tab],
        out_specs=[t_spec, n_spec, t_spec,
                   pl.BlockSpec((1, 1, nh, nb, MOBA_BLOCK), lambda b, i: (b, i, 0, 0, 0))],
        out_shape=[jax.ShapeDtypeStruct((bsz, nb, nh, ATTN_HEAD, MOBA_BLOCK), BF16),
                   jax.ShapeDtypeStruct((bsz, nb, nh, MOBA_BLOCK, ATTN_HEAD), BF16),
                   jax.ShapeDtypeStruct((bsz, nb, nh, ATTN_HEAD, MOBA_BLOCK), BF16),
                   jax.ShapeDtypeStruct((bsz, nb, nh, nb, MOBA_BLOCK), F32)],
        scratch_shapes=[pltpu.VMEM((nb, width), F32)],
        compiler_params=_cparams("parallel", "arbitrary"),
        name="rope",
    )(z3, z3, z3, cos2, sin2)


def _moba_kernel(qs_ref, k_ref, vt_ref, bias_ref, o_ref):
    blk = MOBA_BLOCK
    dh = ATTN_HEAD
    hs = range(qs_ref.shape[2])
    qb = pl.program_id(2)
    neg = -jnp.inf
    qs = [qs_ref[0, 0, h] for h in hs]
    ki = lax.broadcasted_iota(jnp.int32, (blk, blk), 0)
    qi = lax.broadcasted_iota(jnp.int32, (blk, blk), 1)
    causal = ki <= qi

    def pipelined(work, stage):
        scores = lambda kb, h: jnp.dot(k_ref[0, kb, h], qs[h], preferred_element_type=F32)
        ahead = [scores(*w) for w in work[:MOBA_LOOKAHEAD]]
        for i, (kb, h) in enumerate(work):
            if i + MOBA_LOOKAHEAD < len(work):
                ahead.append(scores(*work[i + MOBA_LOOKAHEAD]))
            stage(kb, h, ahead[i])

    m_run, l_run, acc = [None] * len(hs), [None] * len(hs), [None] * len(hs)

    def own_block(kb, h, s):
        s = jnp.where(causal, s, neg)
        m_run[h] = jnp.max(s, axis=0, keepdims=True)
        p = jnp.exp2(s - m_run[h])
        l_run[h] = jnp.sum(p, axis=0, keepdims=True)
        acc[h] = jnp.dot(vt_ref[0, kb, h], p.astype(BF16), preferred_element_type=F32)

    pipelined([(qb, h) for h in hs], own_block)

    def past_blocks(kbs, carry):
        m_c, l_c, acc_c = (list(c) for c in carry)

        def stage(kb, h, s):
            b = bias_ref[0, 0, h, pl.ds(kb, 1), :]
            m_new = jnp.where(b == 0.0, jnp.maximum(m_c[h], jnp.max(s, axis=0, keepdims=True)), m_c[h])
            alpha = jnp.exp2(m_c[h] - m_new)
            p = jnp.exp2(s - (m_new - b))
            pv = jnp.dot(vt_ref[0, kb, h], p.astype(BF16), preferred_element_type=F32)
            m_c[h] = m_new
            l_c[h] = alpha * l_c[h] + jnp.sum(p, axis=0, keepdims=True)
            acc_c[h] = alpha * acc_c[h] + pv

        pipelined([(kb, h) for kb in kbs for h in hs], stage)
        return tuple(m_c), tuple(l_c), tuple(acc_c)

    u = MOBA_UNROLL
    carry = lax.fori_loop(0, qb // u, lambda j, c: past_blocks([j * u + i for i in range(u)], c),
                          (tuple(m_run), tuple(l_run), tuple(acc)))
    _, l_fin, acc = lax.fori_loop((qb // u) * u, qb, lambda kb, c: past_blocks([kb], c), carry)
    for h in hs:
        o_ref[0, :, h * dh:(h + 1) * dh] = (acc[h] / l_fin[h]).T.astype(o_ref.dtype)


def _moba(qs, k, vt, bias, heads_per_step=8):
    bsz, nb, nh, dh, blk = qs.shape
    hp = heads_per_step
    assert nh % hp == 0
    return pl.pallas_call(
        _moba_kernel,
        grid=(bsz, nh // hp, nb),
        in_specs=[pl.BlockSpec((1, 1, hp, dh, blk), lambda b, h, i: (b, i, h, 0, 0)),
                  pl.BlockSpec((1, nb, hp, blk, dh), lambda b, h, i: (b, 0, h, 0, 0)),
                  pl.BlockSpec((1, nb, hp, dh, blk), lambda b, h, i: (b, 0, h, 0, 0)),
                  pl.BlockSpec((1, 1, hp, nb, blk), lambda b, h, i: (b, i, h, 0, 0))],
        out_specs=pl.BlockSpec((1, blk, hp * dh), lambda b, h, i: (b, i, h)),
        out_shape=jax.ShapeDtypeStruct((bsz, nb * blk, nh * dh), BF16),
        compiler_params=_cparams("parallel", "parallel", "arbitrary"),
        name="moba",
    )(qs, k, vt, bias)


def _out_proj_kernel(yr_ref, ya_ref, wr_ref, wa_ref, x_ref, g_ref, o_ref):
    for r in range(0, x_ref.shape[0], ROW_CHUNK):
        rows = pl.ds(r, ROW_CHUNK)
        y = jnp.dot(yr_ref[rows, :], wr_ref[...], preferred_element_type=F32)
        y += jnp.dot(ya_ref[rows, :], wa_ref[...], preferred_element_type=F32)
        ms = jnp.mean(y * y, axis=-1, keepdims=True)
        o_ref[rows, :] = x_ref[rows, :] + y * lax.rsqrt(ms + NORM_EPS) * g_ref[...]


def _out_proj(y_r, y_a, w, x2, gain, tm=512):
    m, d = x2.shape
    cw = y_r.shape[1]
    assert y_a.shape[1] == cw and w.shape[0] == 2 * cw and m % tm == 0
    return pl.pallas_call(
        _out_proj_kernel,
        grid=(m // tm,),
        in_specs=[pl.BlockSpec((tm, cw), lambda i: (i, 0)), pl.BlockSpec((tm, cw), lambda i: (i, 0)),
                  pl.BlockSpec((cw, d), lambda i: (0, 0)), pl.BlockSpec((cw, d), lambda i: (1, 0)),
                  pl.BlockSpec((tm, d), lambda i: (i, 0)), pl.BlockSpec((1, d), lambda i: (0, 0))],
        out_specs=pl.BlockSpec((tm, d), lambda i: (i, 0)),
        out_shape=jax.ShapeDtypeStruct((m, d), F32),
        compiler_params=_cparams("parallel"),
        name="out_proj",
    )(y_r, y_a, w, w, x2, gain.reshape(1, d))


def _mlp_kernel(x_ref, gpre_ref, wu_ref, wd_ref, gpost_ref, o_ref, h_ref, acc_ref):
    f = pl.program_id(1)
    last = pl.num_programs(1) - 1
    chunks = [pl.ds(r, ROW_CHUNK) for r in range(0, x_ref.shape[0], ROW_CHUNK)]

    def part(h):
        u = jnp.maximum(jnp.dot(h, wu_ref[...], preferred_element_type=F32), 0.0)
        return jnp.dot((u * u).astype(BF16), wd_ref[...], preferred_element_type=F32)

    @pl.when(f == 0)
    def _():
        for rows in chunks:
            x = x_ref[rows, :]
            ms = jnp.mean(x * x, axis=-1, keepdims=True)
            h = (x * lax.rsqrt(ms + NORM_EPS) * gpre_ref[...]).astype(BF16)
            h_ref[rows, :] = h
            acc_ref[rows, :] = part(h)

    @pl.when((f > 0) & (f < last))
    def _():
        acc_ref[...] += part(h_ref[...])

    @pl.when(f == last)
    def _():
        for rows in chunks:
            mlp = acc_ref[rows, :] + part(h_ref[rows, :])
            ms = jnp.mean(mlp * mlp, axis=-1, keepdims=True)
            o_ref[rows, :] = x_ref[rows, :] + mlp * lax.rsqrt(ms + NORM_EPS) * gpost_ref[...]


def _mlp(x2, g_pre, w_up, w_down, g_post, tm=512, tf=1024):
    m, d = x2.shape
    dff = w_up.shape[1]
    assert m % tm == 0 and dff % tf == 0 and dff // tf >= 2 and tm % ROW_CHUNK == 0
    return pl.pallas_call(
        _mlp_kernel,
        grid=(m // tm, dff // tf),
        in_specs=[pl.BlockSpec((tm, d), lambda i, f: (i, 0)),
                  pl.BlockSpec((1, d), lambda i, f: (0, 0)),
                  pl.BlockSpec((d, tf), lambda i, f: (0, f)),
                  pl.BlockSpec((tf, d), lambda i, f: (f, 0)),
                  pl.BlockSpec((1, d), lambda i, f: (0, 0))],
        out_specs=pl.BlockSpec((tm, d), lambda i, f: (i, 0)),
        out_shape=jax.ShapeDtypeStruct((m, d), F32),
        scratch_shapes=[pltpu.VMEM((tm, d), BF16), pltpu.VMEM((tm, d), F32)],
        compiler_params=_cparams("parallel", "arbitrary"),
        name="mlp",
    )(x2, g_pre.reshape(1, d), w_up, w_down, g_post.reshape(1, d))


def _rope_tables(s):
    half = ATTN_HEAD // 2
    inv_freq = ROPE_THETA ** (-jnp.arange(half, dtype=F32) / half)
    ang = jnp.arange(s).astype(F32)[:, None] * inv_freq[None, :]
    cos, sin = jnp.cos(ang), jnp.sin(ang)
    return jnp.concatenate([cos, cos], axis=-1), jnp.concatenate([-sin, sin], axis=-1)


def kernel(x, norm_mix_pre, norm_mix_post, norm_mlp_pre, norm_mlp_post, w_in, w_in_vres, shift_mu, shift_mu_vres, decay_w0, decay_w2, iclr_a0, iclr_a2, vres_v0, vres_v2, gate_g2, k_k, k_a, r_k, lnx_gain, lnx_bias, w_out, w_up, w_down):
    bsz, s, d = x.shape
    depth = w_in.shape[0]
    c = decay_w0.shape[1]
    n_lora = DECAY_LORA + ICLR_LORA + GATE_LORA
    n_shift = 3 * c + n_lora
    ca = (w_in.shape[2] - n_shift) // 3
    cos2, sin2 = _rope_tables(s)
    w_in16 = w_in[0].astype(BF16)
    pad_v = VRES_PAD - VRES_LORA
    x2 = x.reshape(bsz * s, d)
    v_first = None
    for i in range(depth):
        if i == 0:
            z, vres = _norm_matmul(x2, norm_mix_pre[i], w_in16)[0], None
        else:
            w_v = jnp.pad(w_in_vres[i - 1], ((0, 0), (0, pad_v))).astype(BF16)
            z, z_v = _norm_matmul(x2, norm_mix_pre[i], w_in16, w_v)
            vres = (z_v.reshape(bsz, s, VRES_PAD), jnp.pad(shift_mu_vres[i - 1], (0, pad_v)), v_first,
                    vres_v0[i - 1], jnp.pad(vres_v2[i - 1], ((0, pad_v), (0, 0))))
        z3 = z.reshape(bsz, s, -1)

        casts = [(w_out, i), (w_up, i), (w_down, i)] + ([(w_in, i + 1)] if i + 1 < depth else [])
        y_r, v_layer, w16 = _rwkv(z3, c, shift_mu[i], decay_w0[i], decay_w2[i], iclr_a0[i], iclr_a2[i], gate_g2[i],
                                  k_k[i], k_a[i], vres, r_k[i].reshape(-1), lnx_gain[i], lnx_bias[i], casts)
        if i == 0:
            v_first = v_layer
        w_out16, w_up16, w_down16 = w16[:3]

        y_a = _moba(*_rope(z3, n_shift, ca, cos2, sin2))

        x2 = _out_proj(y_r.reshape(bsz * s, c), y_a.reshape(bsz * s, ca), w_out16, x2, norm_mix_post[i])
        x2 = _mlp(x2, norm_mlp_pre[i], w_up16, w_down16, norm_mlp_post[i])
        if i + 1 < depth:
            w_in16 = w16[3]
    return x2.reshape(bsz, s, d)
```

```python
import functools

import jax
import jax.numpy as jnp
from jax import lax
from jax.experimental import pallas as pl
from jax.experimental.pallas import tpu as pltpu

F32 = jnp.float32
BF16 = jnp.bfloat16

RWKV_HEAD = 64
DECAY_LORA = 64
ICLR_LORA = 64
VRES_LORA = 32
GATE_LORA = 128
ATTN_HEAD = 128
MOBA_BLOCK = 256
MOBA_TOPK = 3
ROPE_THETA = 10000.0
NORM_EPS = 1e-6
LNX_EPS = 64e-5
LOG2E = 1.4426950408889634

LANES = 128
BF16_SUBLANES = 16
VRES_PAD = LANES
WKV_CHUNK = 64
MOBA_LOOKAHEAD = 6
MOBA_UNROLL = 4
ROW_CHUNK = 256
VMEM_LIMIT = 56 * 1024 * 1024


def _cparams(*sem):
    return pltpu.CompilerParams(dimension_semantics=sem, vmem_limit_bytes=VMEM_LIMIT)


_NN = (((1,), (0,)), ((), ()))
_NT = (((1,), (1,)), ((), ()))
_TN = (((0,), (0,)), ((), ()))


def _split2(x):
    hi = x.astype(BF16)
    lo = (x - hi.astype(F32)).astype(BF16)
    return hi, lo


def _split3(x):
    hi = x.astype(BF16)
    r1 = x - hi.astype(F32)
    mid = r1.astype(BF16)
    lo = (r1 - mid.astype(F32)).astype(BF16)
    return hi, mid, lo


def _mm(a, b, dims=_NN, passes=1):
    d = lambda p, q: lax.dot_general(p, q, dims, preferred_element_type=F32)
    if passes == 1:
        return d(a.astype(BF16), b.astype(BF16))
    ah, al = _split2(a)
    bh, bl = _split2(b)
    return d(ah, bh) + (d(ah, bl) + d(al, bh))


def _sigmoid(x):
    return 1.0 / (1.0 + jnp.exp(-x))


def _softplus(x):
    return jnp.maximum(x, 0.0) + jnp.log(1.0 + jnp.exp(-jnp.abs(x)))


def _in_proj_kernel(has_extra, n_mix, spb, tn, *refs):
    if has_extra:
        x_ref, g_ref, w_ref, cos_ref, sin_ref, we_ref, z_ref, qs_ref, ko_ref, vt_ref, bias_ref, oe_ref, km_ref = refs
    else:
        x_ref, g_ref, w_ref, cos_ref, sin_ref, z_ref, qs_ref, ko_ref, vt_ref, bias_ref, km_ref = refs
    i = pl.program_id(0)
    nb = km_ref.shape[0]
    blk = MOBA_BLOCK
    width = km_ref.shape[1]
    nh = width // ATTN_HEAD

    @pl.when(i == 0)
    def _():
        km_ref[...] = jnp.zeros_like(km_ref)

    blk_id = lax.broadcasted_iota(jnp.int32, (nb, blk), 0)
    km_row = lax.broadcasted_iota(jnp.int32, (nb, ATTN_HEAD), 0)
    qk_scale = (ATTN_HEAD ** -0.5) * LOG2E
    for j in range(x_ref.shape[0] // blk):
        rows = pl.ds(j * blk, blk)
        x = x_ref[rows, :]
        ms = jnp.mean(x * x, axis=-1, keepdims=True)
        h = (x * lax.rsqrt(ms + NORM_EPS) * g_ref[...]).astype(BF16)
        proj = lambda c0, c1: jnp.dot(h, w_ref[:, c0:c1], preferred_element_type=F32)
        for c0 in range(0, n_mix, tn):
            c1 = min(c0 + tn, n_mix)
            z_ref[rows, c0:c1] = proj(c0, c1).astype(z_ref.dtype)
        if has_extra:
            oe_ref[rows, :] = jnp.dot(h, we_ref[...], preferred_element_type=F32).astype(oe_ref.dtype)
        q_all, k_all, v_all = (proj(n_mix + t * width, n_mix + (t + 1) * width) for t in range(3))
        cos = cos_ref[rows, :]
        sin = sin_ref[rows, :]
        bid = (i % spb) * (x_ref.shape[0] // blk) + j
        past = blk_id < bid
        for hd in range(nh):
            sl = slice(hd * ATTN_HEAD, (hd + 1) * ATTN_HEAD)
            q, k = q_all[:, sl], k_all[:, sl]
            qr = (q * cos + pltpu.roll(q, ATTN_HEAD // 2, axis=1) * sin).T
            kr = k * cos + pltpu.roll(k, ATTN_HEAD // 2, axis=1) * sin
            qs_ref[0, j, hd] = (qr * qk_scale).astype(BF16)
            ko_ref[0, j, hd] = kr.astype(BF16)
            vt_ref[0, j, hd] = v_all[:, sl].T.astype(BF16)
            gate = jnp.where(past, _mm(km_ref[:, sl], qr, passes=3), -jnp.inf)
            rank = jnp.zeros((nb, blk), jnp.int32)
            for m in range(nb):
                gm = gate[m:m + 1, :]
                rank += ((gm > gate) | ((gm == gate) & (m < blk_id))).astype(jnp.int32)
            bias_ref[0, j, hd] = jnp.where(past & (rank < MOBA_TOPK), 0.0, -jnp.inf)
            km_new = jnp.sum(kr, axis=0, keepdims=True) * (1.0 / blk)
            km_ref[:, sl] = jnp.where(km_row == bid, km_new, km_ref[:, sl])


def _in_proj(x3, gain, w, n_mix, cos2, sin2, w_extra=None, tm=512, tn=1280):
    bsz, s, d = x3.shape
    n = w.shape[1]
    width = (n - n_mix) // 3
    nh = width // ATTN_HEAD
    nb = s // MOBA_BLOCK
    bpt = tm // MOBA_BLOCK
    spb = s // tm
    assert s % tm == 0 and tm % MOBA_BLOCK == 0 and n_mix % (2 * LANES) == 0 and (n - n_mix) % (3 * ATTN_HEAD) == 0
    has_extra = w_extra is not None
    once = pl.Buffered(1)
    x2 = x3.reshape(bsz * s, d)
    ins = [x2, gain.reshape(1, d), w, cos2, sin2]
    in_specs = [pl.BlockSpec((tm, d), lambda i: (i, 0)),
                pl.BlockSpec((1, d), lambda i: (0, 0)),
                pl.BlockSpec((d, n), lambda i: (0, 0), pipeline_mode=once),
                pl.BlockSpec((tm, ATTN_HEAD), lambda i: (i % spb, 0)),
                pl.BlockSpec((tm, ATTN_HEAD), lambda i: (i % spb, 0))]
    t_spec = pl.BlockSpec((1, bpt, nh, ATTN_HEAD, MOBA_BLOCK), lambda i: (i // spb, i % spb, 0, 0, 0))
    n_spec = pl.BlockSpec((1, bpt, nh, MOBA_BLOCK, ATTN_HEAD), lambda i: (i // spb, i % spb, 0, 0, 0))
    out_specs = [pl.BlockSpec((tm, n_mix), lambda i: (i, 0)), t_spec, n_spec, t_spec,
                 pl.BlockSpec((1, bpt, nh, nb, MOBA_BLOCK), lambda i: (i // spb, i % spb, 0, 0, 0))]
    out_shape = [jax.ShapeDtypeStruct((bsz * s, n_mix), BF16),
                 jax.ShapeDtypeStruct((bsz, nb, nh, ATTN_HEAD, MOBA_BLOCK), BF16),
                 jax.ShapeDtypeStruct((bsz, nb, nh, MOBA_BLOCK, ATTN_HEAD), BF16),
                 jax.ShapeDtypeStruct((bsz, nb, nh, ATTN_HEAD, MOBA_BLOCK), BF16),
                 jax.ShapeDtypeStruct((bsz, nb, nh, nb, MOBA_BLOCK), F32)]
    if has_extra:
        ne = w_extra.shape[1]
        ins.append(w_extra)
        in_specs.append(pl.BlockSpec((d, ne), lambda i: (0, 0), pipeline_mode=once))
        out_specs.append(pl.BlockSpec((tm, ne), lambda i: (i, 0)))
        out_shape.append(jax.ShapeDtypeStruct((bsz * s, ne), BF16))
    return pl.pallas_call(
        functools.partial(_in_proj_kernel, has_extra, n_mix, spb, tn),
        grid=(bsz * spb,),
        in_specs=in_specs,
        out_specs=out_specs,
        out_shape=out_shape,
        scratch_shapes=[pltpu.VMEM((nb, width), F32)],
        compiler_params=_cparams("arbitrary"),
        name="in_proj",
    )(*ins)


def _dots(a_list, b_list, dims=_NN):
    return [lax.dot_general(a.astype(BF16), b.astype(BF16), dims, preferred_element_type=F32)
            for a, b in zip(a_list, b_list)]


def _pair_diag(x, even):
    return jnp.concatenate([jnp.where(even, x, 0.0), jnp.where(even, 0.0, x)], axis=0)


def _unit_lower_inverse(a_list, row_w, col_w, even):
    n = row_w.shape[0]
    lower = row_w > col_w
    base = lower & ((row_w >> 1) == (col_w >> 1))
    t = [jnp.where(row_w == col_w, 1.0, jnp.where(base, a, 0.0)) for a in a_list]
    sh = 1
    while (2 << sh) <= n:
        sub = lower & ((row_w >> (sh + 1)) == (col_w >> (sh + 1))) & ((row_w >> sh) != (col_w >> sh))
        off = [_pair_diag(jnp.where(sub, a, 0.0), even) for a in a_list]
        upd = _dots(_dots(t, off), [_pair_diag(x, even) for x in t])
        t = [x + u for x, u in zip(t, upd)]
        sh += 1
    return t


def _mm_exact_rhs_left(l_bf16, a):
    d = lambda p: lax.dot_general(l_bf16, p, _NN, preferred_element_type=F32)
    hi, mid, lo = _split3(a)
    return d(hi) + (d(mid) + d(lo))


def _rwkv_kernel(has_vres, n_cast, nc, *refs):
    it = iter(refs)
    take = lambda cnt: [next(it) for _ in range(cnt)]
    zm_ref, zl_ref, mum_ref, mul_ref, w0_ref, w2_ref, a0_ref, a2_ref, g2_ref, kk_ref, ka_ref = take(11)
    if has_vres:
        zv_ref, muv_ref, vf_ref, v0_ref, v2_ref = take(5)
    rk_ref, lg_ref, lb_ref = take(3)
    cast_in = take(n_cast)
    y_ref, = take(1)
    if not has_vres:
        vout_ref, = take(1)
    cast_out = take(n_cast)
    state_ref, cm_ref, cl_ref = take(3)
    if has_vres:
        cv_ref, = take(1)
    r_s, lw_s, k_s, v_s, an_s, b_s, g_s = operands = take(7)

    ci = pl.program_id(1)
    ln = WKV_CHUNK
    nb = zm_ref.shape[0]
    c = r_s.shape[2]
    n = RWKV_HEAD
    pw = 2 * n
    npair = c // pw
    ent = [(bi, slice(p * pw, (p + 1) * pw)) for bi in range(nb) for p in range(npair)]

    @pl.when(ci == 0)
    def _():
        for ref in [state_ref, cm_ref, cl_ref] + ([cv_ref] if has_vres else []) + operands:
            ref[...] = jnp.zeros_like(ref)

    rows = lax.broadcasted_iota(jnp.int32, (ln, ln), 0)
    cols = lax.broadcasted_iota(jnp.int32, (ln, ln), 1)
    tril = jnp.where(rows >= cols, 1.0, 0.0).astype(BF16)
    r, k, v, w_end, r_t, a_t, b_t, k_t, b_h, k_h = ([] for _ in range(10))
    for bi in range(nb):
        lw = lw_s[bi]
        cw = _mm_exact_rhs_left(tril, lw)
        cw_end = cw[ln - 1:ln, :]
        e_neg = jnp.exp(-cw)
        e_end = jnp.exp(cw_end - cw)
        bb = b_s[bi].astype(F32)
        r.append(r_s[bi].astype(F32))
        k.append(k_s[bi].astype(F32))
        v.append(v_s[bi].astype(F32))
        w_end.append(jnp.exp(cw_end))
        r_t.append(r[bi] * jnp.exp(cw))
        a_t.append(an_s[bi].astype(F32) * jnp.exp(cw - lw))
        b_t.append(bb * e_neg)
        k_t.append(k[bi] * e_neg)
        b_h.append(bb * e_end)
        k_h.append(k[bi] * e_end)

    lane = lax.broadcasted_iota(jnp.int32, (ln, pw), 1)
    row_w = lax.broadcasted_iota(jnp.int32, (ln, pw), 0)
    even = lane < n
    even2 = lax.broadcasted_iota(jnp.int32, (2 * ln, pw), 1) < n
    col_w = lane & (n - 1)
    strict_w = row_w > col_w
    incl_w = row_w >= col_w
    zeros_w = jnp.zeros((ln, pw), F32)
    diag = lambda x: _pair_diag(x, even)
    swap = lambda x: jnp.concatenate([x[x.shape[0] // 2:], x[:x.shape[0] // 2]], axis=0)

    ar_p = [jnp.concatenate([a_t[bi][:, ps], r_t[bi][:, ps]], axis=0) for bi, ps in ent]
    bk_p = [jnp.concatenate([b_t[bi][:, ps], k_t[bi][:, ps]], axis=0).astype(BF16) for bi, ps in ent]
    kb_p = [jnp.concatenate([k_t[bi][:, ps], b_t[bi][:, ps]], axis=0).astype(BF16) for bi, ps in ent]
    am_e = _dots([jnp.where(even2, x, 0.0) for x in ar_p], bk_p, _NT)
    am_o = _dots([jnp.where(even2, 0.0, x) for x in ar_p], kb_p, _NT)
    a_ab = [jnp.where(strict_w, jnp.where(even, e[:ln], o[:ln]), 0.0) for e, o in zip(am_e, am_o)]
    a_ak = [jnp.where(strict_w, jnp.where(even, o[:ln], e[:ln]), 0.0) for e, o in zip(am_e, am_o)]
    a_rb = [jnp.where(incl_w, jnp.where(even, e[ln:], o[ln:]), 0.0) for e, o in zip(am_e, am_o)]
    a_rk = [jnp.where(incl_w, jnp.where(even, o[ln:], e[ln:]), 0.0) for e, o in zip(am_e, am_o)]
    v_p = [v[bi][:, ps] for bi, ps in ent]
    v_d = [diag(x) for x in v_p]
    akv = _dots(a_ak, [swap(x) for x in v_d])
    t = _unit_lower_inverse(a_ab, row_w, col_w, even)
    rhs = [jnp.concatenate([diag(a_t[bi][:, ps]), diag(x)], axis=1) for (bi, ps), x in zip(ent, akv)]
    pq = _dots(t, rhs)
    ry = _dots([jnp.concatenate([x, y], axis=1) for x, y in zip(a_rb, a_rk)],
               [jnp.concatenate([jnp.concatenate([diag(x[:, :pw]), diag(x[:, pw:])], axis=1),
                                 jnp.concatenate([jnp.zeros((2 * ln, pw), F32), swap(u)], axis=1)], axis=0)
                for x, u in zip(pq, v_d)])
    pqv_p = [jnp.concatenate([x, jnp.concatenate([zeros_w, u], axis=1)], axis=0)
             for x, u in zip(pq, v_p)]
    bkh_p = [jnp.concatenate([b_h[bi][:, ps], k_h[bi][:, ps]], axis=0) for bi, ps in ent]
    mn_p = _dots(bkh_p, pqv_p, _TN)
    sq_r = lax.broadcasted_iota(jnp.int32, (pw, pw), 0)
    sq_c = lax.broadcasted_iota(jnp.int32, (pw, pw), 1)
    same_head = (sq_r < n) == (sq_c < n)
    lhs_p = [jnp.concatenate([r_t[bi][:, ps] + x[:, :pw],
                              jnp.where(sq_r == sq_c, jnp.broadcast_to(w_end[bi][:, ps], (pw, pw)),
                                        jnp.where(same_head, m[:, :pw], 0.0))], axis=0)
             for (bi, ps), x, m in zip(ent, ry, mn_p)]
    st = [state_ref[e] for e in range(len(ent))]
    upd = _dots(lhs_p, st)
    for e in range(len(ent)):
        state_ref[e] = upd[e][ln:, :] + jnp.where(same_head, mn_p[e][:, pw:], 0.0)
    y0_p = [x[:, pw:] for x in ry]

    hsum = lambda x: jnp.where(even, jnp.sum(jnp.where(even, x, 0.0), axis=-1, keepdims=True),
                               jnp.sum(jnp.where(even, 0.0, x), axis=-1, keepdims=True))
    inv_n = 1.0 / n
    yn = []
    for u, y0 in zip(upd, y0_p):
        y = u[:ln, :] + y0
        d = y - hsum(y) * inv_n
        yn.append(d * lax.rsqrt(hsum(d * d) * inv_n + LNX_EPS))
    for bi in range(nb):
        rk = r[bi] * k[bi] * rk_ref[...]
        mine = range(bi * npair, (bi + 1) * npair)
        bonus = jnp.concatenate([hsum(rk[:, ent[e][1]]) * v_p[e] for e in mine], axis=1)
        out = jnp.concatenate([yn[e] for e in mine], axis=1) * lg_ref[...] + lb_ref[...] + bonus
        y_ref[bi] = (out * g_s[bi].astype(F32)).astype(y_ref.dtype)

    advance = ci < nc - 1

    def shifted(z_ref, carry_ref, mu_ref):
        out = []
        for bi in range(nb):
            cur = z_ref[bi].astype(F32)
            first_row = lax.broadcasted_iota(jnp.int32, cur.shape, 0) == 0
            prev = jnp.where(first_row, carry_ref[bi:bi + 1, :], pltpu.roll(cur, 1, axis=0))
            carry_ref[bi:bi + 1, :] = jnp.where(advance, cur[ln - 1:ln, :], carry_ref[bi:bi + 1, :])
            out.append(cur + (prev - cur) * mu_ref[...])
        return jnp.concatenate(out, axis=0)

    zs = shifted(zm_ref, cm_ref, mum_ref)
    zl = shifted(zl_ref, cl_ref, mul_ref)
    r_n, k_n, v_n = zs[:, :c], zs[:, c:2 * c], zs[:, 2 * c:3 * c]
    o = 0
    wd = zl[:, o:o + DECAY_LORA]
    o += DECAY_LORA
    ad = zl[:, o:o + ICLR_LORA]
    o += ICLR_LORA
    gd = zl[:, o:o + GATE_LORA]
    w_log = -_softplus(-(w0_ref[...] + _mm(jnp.tanh(wd), w2_ref[...], passes=3))) - 0.5
    lw_n = -jnp.exp(w_log)
    a_n = _sigmoid(a0_ref[...] + _mm(ad, a2_ref[...]))
    g_n = _mm(_sigmoid(gd), g2_ref[...])
    if has_vres:
        vd = shifted(zv_ref, cv_ref, muv_ref)
        v_first = jnp.concatenate([vf_ref[bi] for bi in range(nb)], axis=0)
        v_n = v_n + (v_first - v_n) * _sigmoid(v0_ref[...] + _mm(vd, v2_ref[...]))
    kk = k_n * kk_ref[...]
    even_n = lax.broadcasted_iota(jnp.int32, (kk.shape[0], pw), 1) < n
    unit = []
    for p in range(npair):
        x = kk[:, p * pw:(p + 1) * pw]
        sq = x * x
        ss = jnp.where(even_n, jnp.sum(jnp.where(even_n, sq, 0.0), axis=-1, keepdims=True),
                       jnp.sum(jnp.where(even_n, 0.0, sq), axis=-1, keepdims=True))
        unit.append(x / jnp.maximum(jnp.sqrt(ss), 1e-12))
    kk = jnp.concatenate(unit, axis=1)
    k_n = k_n * (1.0 + (a_n - 1.0) * ka_ref[...])
    kka = kk * a_n
    for bi in range(nb):
        rs = slice(bi * ln, (bi + 1) * ln)
        r_s[bi] = r_n[rs]
        lw_s[bi] = lw_n[rs]
        k_s[bi] = k_n[rs]
        v_s[bi] = v_n[rs]
        an_s[bi] = -kk[rs]
        b_s[bi] = kka[rs]
        g_s[bi] = g_n[rs]
        if not has_vres:
            vout_ref[bi] = v_n[rs]

    for src, dst in zip(cast_in, cast_out):
        dst[...] = src[...].astype(dst.dtype)


def _rwkv(z3, c, mu, w0, w2, a0, a2, g2, k_k, k_a, vres, r_k, lnx_g, lnx_b, casts=()):
    bsz, s, _ = z3.shape
    n_lora = w2.shape[0] + a2.shape[0] + g2.shape[0]
    nc = s // WKV_CHUNK
    assert s % WKV_CHUNK == 0 and WKV_CHUNK == RWKV_HEAD and c % (2 * RWKV_HEAD) == 0
    assert (3 * c) % n_lora == 0 and n_lora % LANES == 0
    has_vres = vres is not None
    row = lambda a: a.reshape(1, -1)
    lcb = 3 * c // n_lora
    cur = lambda ci: jnp.minimum(ci, nc - 1)
    full = lambda a: pl.BlockSpec(a.shape, lambda hi, ci: (0,) * a.ndim)
    chunk = lambda w, cb=0: pl.BlockSpec((bsz, WKV_CHUNK, w), lambda hi, ci: (0, cur(ci), cb))
    ins = [z3, z3, row(mu[:3 * c]), row(mu[3 * c:]), row(w0), w2, row(a0), a2, g2, row(k_k), row(k_a)]
    in_specs = [chunk(3 * c), chunk(n_lora, lcb)] + [full(a) for a in ins[2:]]
    scratch = [pltpu.VMEM((bsz * c // (2 * RWKV_HEAD), 2 * RWKV_HEAD, 2 * RWKV_HEAD), F32),
               pltpu.VMEM((bsz, 3 * c), F32), pltpu.VMEM((bsz, n_lora), F32)]
    if has_vres:
        zv3, mu_v, v_first, v0, v2 = vres
        extra = [zv3, row(mu_v), v_first, row(v0), v2]
        ins += extra
        in_specs += [chunk(zv3.shape[-1]), full(extra[1]), chunk(c), full(extra[3]), full(extra[4])]
        scratch.append(pltpu.VMEM((bsz, zv3.shape[-1]), F32))
    tail = [row(r_k), row(lnx_g), row(lnx_b)]
    ins += tail
    in_specs += [full(a) for a in tail]
    scratch += [pltpu.VMEM((bsz, WKV_CHUNK, c), F32)] * 7
    out_specs = [pl.BlockSpec((bsz, WKV_CHUNK, c), lambda hi, ci: (0, jnp.maximum(ci - 1, 0), 0))]
    out_shape = [jax.ShapeDtypeStruct((bsz, s, c), BF16)]
    if not has_vres:
        out_specs.append(chunk(c))
        out_shape.append(jax.ShapeDtypeStruct((bsz, s, c), F32))
    for w, layer in casts:
        nl, rows, cols = w.shape
        assert rows % (BF16_SUBLANES * nc) == 0
        slab = rows // nc
        ins.append(w.reshape(nl, nc, slab, cols))
        in_specs.append(pl.BlockSpec((None, None, slab, cols), lambda hi, ci, layer=layer: (layer, cur(ci), 0, 0)))
        out_specs.append(pl.BlockSpec((None, slab, cols), lambda hi, ci: (cur(ci), 0, 0)))
        out_shape.append(jax.ShapeDtypeStruct((nc, slab, cols), BF16))
    out = pl.pallas_call(
        functools.partial(_rwkv_kernel, has_vres, len(casts), nc),
        grid=(1, nc + 1),
        in_specs=in_specs,
        out_specs=out_specs,
        out_shape=out_shape,
        scratch_shapes=scratch,
        compiler_params=_cparams("arbitrary", "arbitrary"),
        name="rwkv",
    )(*ins)
    n_main = 1 if has_vres else 2
    w16 = [o.reshape(w.shape[1], w.shape[2]) for o, (w, _) in zip(out[n_main:], casts)]
    return out[0], (None if has_vres else out[1]), w16


def _moba_kernel(qs_ref, k_ref, vt_ref, bias_ref, o_ref):
    blk = MOBA_BLOCK
    dh = ATTN_HEAD
    hs = range(qs_ref.shape[2])
    qb = pl.program_id(2)
    neg = -jnp.inf
    qs = [qs_ref[0, 0, h] for h in hs]
    ki = lax.broadcasted_iota(jnp.int32, (blk, blk), 0)
    qi = lax.broadcasted_iota(jnp.int32, (blk, blk), 1)
    causal = ki <= qi

    def pipelined(work, stage):
        scores = lambda kb, h: jnp.dot(k_ref[0, kb, h], qs[h], preferred_element_type=F32)
        ahead = [scores(*w) for w in work[:MOBA_LOOKAHEAD]]
        for i, (kb, h) in enumerate(work):
            if i + MOBA_LOOKAHEAD < len(work):
                ahead.append(scores(*work[i + MOBA_LOOKAHEAD]))
            stage(kb, h, ahead[i])

    m_run, l_run, acc = [None] * len(hs), [None] * len(hs), [None] * len(hs)

    def own_block(kb, h, s):
        s = jnp.where(causal, s, neg)
        m_run[h] = jnp.max(s, axis=0, keepdims=True)
        p = jnp.exp2(s - m_run[h])
        l_run[h] = jnp.sum(p, axis=0, keepdims=True)
        acc[h] = jnp.dot(vt_ref[0, kb, h], p.astype(BF16), preferred_element_type=F32)

    pipelined([(qb, h) for h in hs], own_block)

    def past_blocks(kbs, carry):
        m_c, l_c, acc_c = (list(c) for c in carry)

        def stage(kb, h, s):
            b = bias_ref[0, 0, h, pl.ds(kb, 1), :]
            m_new = jnp.where(b == 0.0, jnp.maximum(m_c[h], jnp.max(s, axis=0, keepdims=True)), m_c[h])
            alpha = jnp.exp2(m_c[h] - m_new)
            p = jnp.exp2(s - (m_new - b))
            pv = jnp.dot(vt_ref[0, kb, h], p.astype(BF16), preferred_element_type=F32)
            m_c[h] = m_new
            l_c[h] = alpha * l_c[h] + jnp.sum(p, axis=0, keepdims=True)
            acc_c[h] = alpha * acc_c[h] + pv

        pipelined([(kb, h) for kb in kbs for h in hs], stage)
        return tuple(m_c), tuple(l_c), tuple(acc_c)

    u = MOBA_UNROLL
    carry = lax.fori_loop(0, qb // u, lambda j, c: past_blocks([j * u + i for i in range(u)], c),
                          (tuple(m_run), tuple(l_run), tuple(acc)))
    _, l_fin, acc = lax.fori_loop((qb // u) * u, qb, lambda kb, c: past_blocks([kb], c), carry)
    for h in hs:
        o_ref[0, :, h * dh:(h + 1) * dh] = (acc[h] / l_fin[h]).T.astype(o_ref.dtype)


def _moba(qs, k, vt, bias, heads_per_step=8):
    bsz, nb, nh, dh, blk = qs.shape
    hp = heads_per_step
    assert nh % hp == 0
    return pl.pallas_call(
        _moba_kernel,
        grid=(bsz, nh // hp, nb),
        in_specs=[pl.BlockSpec((1, 1, hp, dh, blk), lambda b, h, i: (b, i, h, 0, 0)),
                  pl.BlockSpec((1, nb, hp, blk, dh), lambda b, h, i: (b, 0, h, 0, 0)),
                  pl.BlockSpec((1, nb, hp, dh, blk), lambda b, h, i: (b, 0, h, 0, 0)),
                  pl.BlockSpec((1, 1, hp, nb, blk), lambda b, h, i: (b, i, h, 0, 0))],
        out_specs=pl.BlockSpec((1, blk, hp * dh), lambda b, h, i: (b, i, h)),
        out_shape=jax.ShapeDtypeStruct((bsz, nb * blk, nh * dh), BF16),
        compiler_params=_cparams("parallel", "parallel", "arbitrary"),
        name="moba",
    )(qs, k, vt, bias)


def _out_proj_kernel(yr_ref, ya_ref, wr_ref, wa_ref, x_ref, g_ref, o_ref):
    for r in range(0, x_ref.shape[0], ROW_CHUNK):
        rows = pl.ds(r, ROW_CHUNK)
        y = jnp.dot(yr_ref[rows, :], wr_ref[...], preferred_element_type=F32)
        y += jnp.dot(ya_ref[rows, :], wa_ref[...], preferred_element_type=F32)
        ms = jnp.mean(y * y, axis=-1, keepdims=True)
        o_ref[rows, :] = x_ref[rows, :] + y * lax.rsqrt(ms + NORM_EPS) * g_ref[...]


def _out_proj(y_r, y_a, w, x2, gain, tm=512):
    m, d = x2.shape
    cw = y_r.shape[1]
    assert y_a.shape[1] == cw and w.shape[0] == 2 * cw and m % tm == 0
    return pl.pallas_call(
        _out_proj_kernel,
        grid=(m // tm,),
        in_specs=[pl.BlockSpec((tm, cw), lambda i: (i, 0)), pl.BlockSpec((tm, cw), lambda i: (i, 0)),
                  pl.BlockSpec((cw, d), lambda i: (0, 0)), pl.BlockSpec((cw, d), lambda i: (1, 0)),
                  pl.BlockSpec((tm, d), lambda i: (i, 0)), pl.BlockSpec((1, d), lambda i: (0, 0))],
        out_specs=pl.BlockSpec((tm, d), lambda i: (i, 0)),
        out_shape=jax.ShapeDtypeStruct((m, d), F32),
        compiler_params=_cparams("parallel"),
        name="out_proj",
    )(y_r, y_a, w, w, x2, gain.reshape(1, d))


def _mlp_kernel(x_ref, gpre_ref, wu_ref, wd_ref, gpost_ref, o_ref, h_ref, acc_ref):
    f = pl.program_id(1)
    last = pl.num_programs(1) - 1
    chunks = [pl.ds(r, ROW_CHUNK) for r in range(0, x_ref.shape[0], ROW_CHUNK)]

    def part(h):
        u = jnp.maximum(jnp.dot(h, wu_ref[...], preferred_element_type=F32), 0.0)
        return jnp.dot((u * u).astype(BF16), wd_ref[...], preferred_element_type=F32)

    @pl.when(f == 0)
    def _():
        for rows in chunks:
            x = x_ref[rows, :]
            ms = jnp.mean(x * x, axis=-1, keepdims=True)
            h = (x * lax.rsqrt(ms + NORM_EPS) * gpre_ref[...]).astype(BF16)
            h_ref[rows, :] = h
            acc_ref[rows, :] = part(h)

    @pl.when((f > 0) & (f < last))
    def _():
        acc_ref[...] += part(h_ref[...])

    @pl.when(f == last)
    def _():
        for rows in chunks:
            mlp = acc_ref[rows, :] + part(h_ref[rows, :])
            ms = jnp.mean(mlp * mlp, axis=-1, keepdims=True)
            o_ref[rows, :] = x_ref[rows, :] + mlp * lax.rsqrt(ms + NORM_EPS) * gpost_ref[...]


def _mlp(x2, g_pre, w_up, w_down, g_post, tm=512, tf=1024):
    m, d = x2.shape
    dff = w_up.shape[1]
    assert m % tm == 0 and dff % tf == 0 and dff // tf >= 2 and tm % ROW_CHUNK == 0
    return pl.pallas_call(
        _mlp_kernel,
        grid=(m // tm, dff // tf),
        in_specs=[pl.BlockSpec((tm, d), lambda i, f: (i, 0)),
                  pl.BlockSpec((1, d), lambda i, f: (0, 0)),
                  pl.BlockSpec((d, tf), lambda i, f: (0, f)),
                  pl.BlockSpec((tf, d), lambda i, f: (f, 0)),
                  pl.BlockSpec((1, d), lambda i, f: (0, 0))],
        out_specs=pl.BlockSpec((tm, d), lambda i, f: (i, 0)),
        out_shape=jax.ShapeDtypeStruct((m, d), F32),
        scratch_shapes=[pltpu.VMEM((tm, d), BF16), pltpu.VMEM((tm, d), F32)],
        compiler_params=_cparams("parallel", "arbitrary"),
        name="mlp",
    )(x2, g_pre.reshape(1, d), w_up, w_down, g_post.reshape(1, d))


def _rope_tables(s):
    half = ATTN_HEAD // 2
    inv_freq = ROPE_THETA ** (-jnp.arange(half, dtype=F32) / half)
    ang = jnp.arange(s).astype(F32)[:, None] * inv_freq[None, :]
    cos, sin = jnp.cos(ang), jnp.sin(ang)
    return jnp.concatenate([cos, cos], axis=-1), jnp.concatenate([-sin, sin], axis=-1)


def kernel(x, norm_mix_pre, norm_mix_post, norm_mlp_pre, norm_mlp_post, w_in, w_in_vres, shift_mu, shift_mu_vres, decay_w0, decay_w2, iclr_a0, iclr_a2, vres_v0, vres_v2, gate_g2, k_k, k_a, r_k, lnx_gain, lnx_bias, w_out, w_up, w_down):
    bsz, s, d = x.shape
    depth = w_in.shape[0]
    c = decay_w0.shape[1]
    n_lora = DECAY_LORA + ICLR_LORA + GATE_LORA
    n_shift = 3 * c + n_lora
    ca = (w_in.shape[2] - n_shift) // 3
    cos2, sin2 = _rope_tables(s)
    w_in16 = w_in[0].astype(BF16)
    pad_v = VRES_PAD - VRES_LORA
    x2 = x.reshape(bsz * s, d)
    v_first = None
    for i in range(depth):
        if i == 0:
            z, *attn = _in_proj(x2.reshape(bsz, s, d), norm_mix_pre[i], w_in16, n_shift, cos2, sin2)
            vres = None
        else:
            w_v = jnp.pad(w_in_vres[i - 1], ((0, 0), (0, pad_v))).astype(BF16)
            z, *attn, z_v = _in_proj(x2.reshape(bsz, s, d), norm_mix_pre[i], w_in16, n_shift, cos2, sin2, w_v)
            vres = (z_v.reshape(bsz, s, VRES_PAD), jnp.pad(shift_mu_vres[i - 1], (0, pad_v)), v_first,
                    vres_v0[i - 1], jnp.pad(vres_v2[i - 1], ((0, pad_v), (0, 0))))
        z3 = z.reshape(bsz, s, -1)

        casts = [(w_out, i), (w_up, i), (w_down, i)] + ([(w_in, i + 1)] if i + 1 < depth else [])
        y_r, v_layer, w16 = _rwkv(z3, c, shift_mu[i], decay_w0[i], decay_w2[i], iclr_a0[i], iclr_a2[i], gate_g2[i],
                                  k_k[i], k_a[i], vres, r_k[i].reshape(-1), lnx_gain[i], lnx_bias[i], casts)
        if i == 0:
            v_first = v_layer
        w_out16, w_up16, w_down16 = w16[:3]

        y_a = _moba(*attn)

        x2 = _out_proj(y_r.reshape(bsz * s, c), y_a.reshape(bsz * s, ca), w_out16, x2, norm_mix_post[i])
        x2 = _mlp(x2, norm_mlp_pre[i], w_up16, w_down16, norm_mlp_post[i])
        if i + 1 < depth:
            w_in16 = w16[3]
    return x2.reshape(bsz, s, d)
```

```python
import functools

import jax
import jax.numpy as jnp
from jax import lax
from jax.experimental import pallas as pl
from jax.experimental.pallas import tpu as pltpu

F32 = jnp.float32
BF16 = jnp.bfloat16

RWKV_HEAD = 64
DECAY_LORA = 64
ICLR_LORA = 64
VRES_LORA = 32
GATE_LORA = 128
ATTN_HEAD = 128
MOBA_BLOCK = 256
MOBA_TOPK = 3
ROPE_THETA = 10000.0
NORM_EPS = 1e-6
LNX_EPS = 64e-5
LOG2E = 1.4426950408889634

LANES = 128
BF16_SUBLANES = 16
VRES_PAD = LANES
WKV_CHUNK = 64
MOBA_LOOKAHEAD = 6
MOBA_UNROLL = 4
ROW_CHUNK = 256
VMEM_LIMIT = 56 * 1024 * 1024


def _cparams(*sem):
    return pltpu.CompilerParams(dimension_semantics=sem, vmem_limit_bytes=VMEM_LIMIT)


_NN = (((1,), (0,)), ((), ()))
_NT = (((1,), (1,)), ((), ()))
_TN = (((0,), (0,)), ((), ()))


def _split2(x):
    hi = x.astype(BF16)
    lo = (x - hi.astype(F32)).astype(BF16)
    return hi, lo


def _split3(x):
    hi = x.astype(BF16)
    r1 = x - hi.astype(F32)
    mid = r1.astype(BF16)
    lo = (r1 - mid.astype(F32)).astype(BF16)
    return hi, mid, lo


def _mm(a, b, dims=_NN, passes=1):
    d = lambda p, q: lax.dot_general(p, q, dims, preferred_element_type=F32)
    if passes == 1:
        return d(a.astype(BF16), b.astype(BF16))
    ah, al = _split2(a)
    bh, bl = _split2(b)
    return d(ah, bh) + (d(ah, bl) + d(al, bh))


def _sigmoid(x):
    return 1.0 / (1.0 + jnp.exp(-x))


def _softplus(x):
    return jnp.maximum(x, 0.0) + jnp.log(1.0 + jnp.exp(-jnp.abs(x)))


def _in_proj_kernel(has_vres, c, spb, *refs):
    it = iter(refs)
    take = lambda cnt: [next(it) for _ in range(cnt)]
    x_ref, g_ref, w_ref, cos_ref, sin_ref, mum_ref, mul_ref, w0_ref, w2_ref, a0_ref, a2_ref, g2_ref, kk_ref, ka_ref = take(14)
    if has_vres:
        we_ref, muv_ref, vf_ref, v0_ref, v2_ref = take(5)
    r_ref, lw_ref, k_ref, v_ref, an_ref, b_ref, gate_ref = take(7)
    if not has_vres:
        vf32_ref, = take(1)
    qs_ref, ko_ref, vt_ref, bias_ref = take(4)
    km_ref, cm_ref, cl_ref = take(3)
    if has_vres:
        cv_ref, = take(1)

    i = pl.program_id(0)
    nb = km_ref.shape[0]
    blk = MOBA_BLOCK
    width = km_ref.shape[1]
    nh = width // ATTN_HEAD
    n_lora = mul_ref.shape[1]
    n_mix = 3 * c + n_lora
    pw = 2 * RWKV_HEAD

    @pl.when(i == 0)
    def _():
        for ref in [km_ref, cm_ref, cl_ref] + ([cv_ref] if has_vres else []):
            ref[...] = jnp.zeros_like(ref)

    blk_id = lax.broadcasted_iota(jnp.int32, (nb, blk), 0)
    km_row = lax.broadcasted_iota(jnp.int32, (nb, ATTN_HEAD), 0)
    first_row = lax.broadcasted_iota(jnp.int32, (blk, 1), 0) == 0
    even = lax.broadcasted_iota(jnp.int32, (blk, pw), 1) < RWKV_HEAD
    qk_scale = (ATTN_HEAD ** -0.5) * LOG2E

    def shifted(cur, carry_ref, lo, hi, mu_ref, start):
        last = jnp.where(start, 0.0, carry_ref[:, lo:hi])
        prev = jnp.where(first_row, last, pltpu.roll(cur, 1, axis=0))
        carry_ref[:, lo:hi] = cur[blk - 1:blk, :]
        return cur + (prev - cur) * mu_ref[:, lo:hi]

    for j in range(x_ref.shape[0] // blk):
        rows = pl.ds(j * blk, blk)
        x = x_ref[rows, :]
        ms = jnp.mean(x * x, axis=-1, keepdims=True)
        h = (x * lax.rsqrt(ms + NORM_EPS) * g_ref[...]).astype(BF16)
        proj = lambda c0, c1: jnp.dot(h, w_ref[:, c0:c1], preferred_element_type=F32)
        bid = (i % spb) * (x_ref.shape[0] // blk) + j
        start = bid == 0

        z_r, z_k, z_v, z_l = proj(0, c), proj(c, 2 * c), proj(2 * c, 3 * c), proj(3 * c, n_mix)
        if has_vres:
            z_e = jnp.dot(h, we_ref[...], preferred_element_type=F32)
        q_all, k_all, v_all = (proj(n_mix + t * width, n_mix + (t + 1) * width) for t in range(3))

        r = shifted(z_r, cm_ref, 0, c, mum_ref, start)
        k = shifted(z_k, cm_ref, c, 2 * c, mum_ref, start)
        v = shifted(z_v, cm_ref, 2 * c, 3 * c, mum_ref, start)
        zl = shifted(z_l, cl_ref, 0, n_lora, mul_ref, start)
        o = 0
        wd = zl[:, o:o + DECAY_LORA]
        o += DECAY_LORA
        ad = zl[:, o:o + ICLR_LORA]
        o += ICLR_LORA
        gd = zl[:, o:o + GATE_LORA]
        w_log = -_softplus(-(w0_ref[...] + _mm(jnp.tanh(wd), w2_ref[...], passes=3))) - 0.5
        lw_ref[rows, :] = -jnp.exp(w_log)
        a = _sigmoid(a0_ref[...] + _mm(ad, a2_ref[...]))
        gate_ref[rows, :] = _mm(_sigmoid(gd), g2_ref[...]).astype(gate_ref.dtype)
        if has_vres:
            vd = shifted(z_e, cv_ref, 0, z_e.shape[1], muv_ref, start)
            v = v + (vf_ref[rows, :] - v) * _sigmoid(v0_ref[...] + _mm(vd, v2_ref[...]))
        else:
            vf32_ref[rows, :] = v
        kk = k * kk_ref[...]
        unit = []
        for p in range(c // pw):
            xk = kk[:, p * pw:(p + 1) * pw]
            sq = xk * xk
            ss = jnp.where(even, jnp.sum(jnp.where(even, sq, 0.0), axis=-1, keepdims=True),
                           jnp.sum(jnp.where(even, 0.0, sq), axis=-1, keepdims=True))
            unit.append(xk / jnp.maximum(jnp.sqrt(ss), 1e-12))
        kk = jnp.concatenate(unit, axis=1)
        r_ref[rows, :] = r.astype(r_ref.dtype)
        k_ref[rows, :] = (k * (1.0 + (a - 1.0) * ka_ref[...])).astype(k_ref.dtype)
        v_ref[rows, :] = v.astype(v_ref.dtype)
        an_ref[rows, :] = (-kk).astype(an_ref.dtype)
        b_ref[rows, :] = (kk * a).astype(b_ref.dtype)

        cos = cos_ref[rows, :]
        sin = sin_ref[rows, :]
        past = blk_id < bid
        for hd in range(nh):
            sl = slice(hd * ATTN_HEAD, (hd + 1) * ATTN_HEAD)
            q, ka = q_all[:, sl], k_all[:, sl]
            qr = (q * cos + pltpu.roll(q, ATTN_HEAD // 2, axis=1) * sin).T
            kr = ka * cos + pltpu.roll(ka, ATTN_HEAD // 2, axis=1) * sin
            qs_ref[0, j, hd] = (qr * qk_scale).astype(BF16)
            ko_ref[0, j, hd] = kr.astype(BF16)
            vt_ref[0, j, hd] = v_all[:, sl].T.astype(BF16)
            gate = jnp.where(past, _mm(km_ref[:, sl], qr, passes=3), -jnp.inf)
            rank = jnp.zeros((nb, blk), jnp.int32)
            for m in range(nb):
                gm = gate[m:m + 1, :]
                rank += ((gm > gate) | ((gm == gate) & (m < blk_id))).astype(jnp.int32)
            bias_ref[0, j, hd] = jnp.where(past & (rank < MOBA_TOPK), 0.0, -jnp.inf)
            km_new = jnp.sum(kr, axis=0, keepdims=True) * (1.0 / blk)
            km_ref[:, sl] = jnp.where(km_row == bid, km_new, km_ref[:, sl])


def _in_proj(x3, gain, w, c, mu, w0, w2, a0, a2, g2, k_k, k_a, cos2, sin2, vres=None, tm=256):
    bsz, s, d = x3.shape
    n = w.shape[1]
    n_lora = w2.shape[0] + a2.shape[0] + g2.shape[0]
    n_mix = 3 * c + n_lora
    width = (n - n_mix) // 3
    nh = width // ATTN_HEAD
    nb = s // MOBA_BLOCK
    bpt = tm // MOBA_BLOCK
    spb = s // tm
    assert s % tm == 0 and tm % MOBA_BLOCK == 0 and c % (2 * RWKV_HEAD) == 0 and n_lora % LANES == 0
    assert (n - n_mix) % (3 * ATTN_HEAD) == 0
    has_vres = vres is not None
    once = pl.Buffered(1)
    row = lambda a: a.reshape(1, -1)
    full = lambda a: pl.BlockSpec(a.shape, lambda i: (0,) * a.ndim)
    tile = lambda wd_: pl.BlockSpec((tm, wd_), lambda i: (i, 0))
    small = [row(mu[:3 * c]), row(mu[3 * c:]), row(w0), w2, row(a0), a2, g2, row(k_k), row(k_a)]
    ins = [x3.reshape(bsz * s, d), gain.reshape(1, d), w, cos2, sin2] + small
    in_specs = [tile(d), pl.BlockSpec((1, d), lambda i: (0, 0)),
                pl.BlockSpec((d, n), lambda i: (0, 0), pipeline_mode=once),
                pl.BlockSpec((tm, ATTN_HEAD), lambda i: (i % spb, 0)),
                pl.BlockSpec((tm, ATTN_HEAD), lambda i: (i % spb, 0))] + [full(a) for a in small]
    scratch = [pltpu.VMEM((nb, width), F32), pltpu.VMEM((1, 3 * c), F32), pltpu.VMEM((1, n_lora), F32)]
    if has_vres:
        w_v, mu_v, v_first, v0, v2 = vres
        extra = [w_v, row(mu_v), v_first, row(v0), v2]
        ins += extra
        in_specs += [pl.BlockSpec(w_v.shape, lambda i: (0, 0), pipeline_mode=once), full(extra[1]), tile(c),
                     full(extra[3]), full(extra[4])]
        scratch.append(pltpu.VMEM((1, w_v.shape[1]), F32))
    mix_dtypes = [BF16, F32, BF16, BF16, BF16, BF16, BF16] + ([] if has_vres else [F32])
    t_spec = pl.BlockSpec((1, bpt, nh, ATTN_HEAD, MOBA_BLOCK), lambda i: (i // spb, i % spb, 0, 0, 0))
    n_spec = pl.BlockSpec((1, bpt, nh, MOBA_BLOCK, ATTN_HEAD), lambda i: (i // spb, i % spb, 0, 0, 0))
    out_specs = [tile(c)] * len(mix_dtypes) + [
        t_spec, n_spec, t_spec, pl.BlockSpec((1, bpt, nh, nb, MOBA_BLOCK), lambda i: (i // spb, i % spb, 0, 0, 0))]
    out_shape = [jax.ShapeDtypeStruct((bsz * s, c), dt) for dt in mix_dtypes] + [
        jax.ShapeDtypeStruct((bsz, nb, nh, ATTN_HEAD, MOBA_BLOCK), BF16),
        jax.ShapeDtypeStruct((bsz, nb, nh, MOBA_BLOCK, ATTN_HEAD), BF16),
        jax.ShapeDtypeStruct((bsz, nb, nh, ATTN_HEAD, MOBA_BLOCK), BF16),
        jax.ShapeDtypeStruct((bsz, nb, nh, nb, MOBA_BLOCK), F32)]
    out = pl.pallas_call(
        functools.partial(_in_proj_kernel, has_vres, c, spb),
        grid=(bsz * spb,),
        in_specs=in_specs,
        out_specs=out_specs,
        out_shape=out_shape,
        scratch_shapes=scratch,
        compiler_params=_cparams("arbitrary"),
        name="in_proj",
    )(*ins)
    mix = [o.reshape(bsz, s, c) for o in out[:7]]
    return mix, (None if has_vres else out[7]), out[-4:]


def _dots(a_list, b_list, dims=_NN):
    return [lax.dot_general(a.astype(BF16), b.astype(BF16), dims, preferred_element_type=F32)
            for a, b in zip(a_list, b_list)]


def _pair_diag(x, even):
    return jnp.concatenate([jnp.where(even, x, 0.0), jnp.where(even, 0.0, x)], axis=0)


def _unit_lower_inverse(a_list, row_w, col_w, even):
    n = row_w.shape[0]
    lower = row_w > col_w
    base = lower & ((row_w >> 1) == (col_w >> 1))
    t = [jnp.where(row_w == col_w, 1.0, jnp.where(base, a, 0.0)) for a in a_list]
    sh = 1
    while (2 << sh) <= n:
        sub = lower & ((row_w >> (sh + 1)) == (col_w >> (sh + 1))) & ((row_w >> sh) != (col_w >> sh))
        off = [_pair_diag(jnp.where(sub, a, 0.0), even) for a in a_list]
        upd = _dots(_dots(t, off), [_pair_diag(x, even) for x in t])
        t = [x + u for x, u in zip(t, upd)]
        sh += 1
    return t


def _mm_exact_rhs_left(l_bf16, a):
    d = lambda p: lax.dot_general(l_bf16, p, _NN, preferred_element_type=F32)
    hi, mid, lo = _split3(a)
    return d(hi) + (d(mid) + d(lo))


def _wkv_kernel(n_cast, *refs):
    r_ref, lw_ref, k_ref, v_ref, an_ref, b_ref, g_ref, rk_ref, lg_ref, lb_ref = refs[:10]
    cast_in, y_ref = refs[10:10 + n_cast], refs[10 + n_cast]
    cast_out, state_ref = refs[11 + n_cast:11 + 2 * n_cast], refs[11 + 2 * n_cast]
    for src, dst in zip(cast_in, cast_out):
        dst[...] = src[...].astype(dst.dtype)

    nb, ln = r_ref.shape[0], r_ref.shape[1]
    n = RWKV_HEAD
    pw = 2 * n
    npair = r_ref.shape[2] // pw
    ent = [(bi, slice(p * pw, (p + 1) * pw)) for bi in range(nb) for p in range(npair)]

    @pl.when(pl.program_id(1) == 0)
    def _():
        state_ref[...] = jnp.zeros_like(state_ref)

    rows = lax.broadcasted_iota(jnp.int32, (ln, ln), 0)
    cols = lax.broadcasted_iota(jnp.int32, (ln, ln), 1)
    tril = jnp.where(rows >= cols, 1.0, 0.0).astype(BF16)
    r, k, v, w_end, r_t, a_t, b_t, k_t, b_h, k_h = ([] for _ in range(10))
    for bi in range(nb):
        lw = lw_ref[bi]
        cw = _mm_exact_rhs_left(tril, lw)
        cw_end = cw[ln - 1:ln, :]
        e_neg = jnp.exp(-cw)
        e_end = jnp.exp(cw_end - cw)
        bb = b_ref[bi].astype(F32)
        r.append(r_ref[bi].astype(F32))
        k.append(k_ref[bi].astype(F32))
        v.append(v_ref[bi].astype(F32))
        w_end.append(jnp.exp(cw_end))
        r_t.append(r[bi] * jnp.exp(cw))
        a_t.append(an_ref[bi].astype(F32) * jnp.exp(cw - lw))
        b_t.append(bb * e_neg)
        k_t.append(k[bi] * e_neg)
        b_h.append(bb * e_end)
        k_h.append(k[bi] * e_end)

    lane = lax.broadcasted_iota(jnp.int32, (ln, pw), 1)
    row_w = lax.broadcasted_iota(jnp.int32, (ln, pw), 0)
    even = lane < n
    even2 = lax.broadcasted_iota(jnp.int32, (2 * ln, pw), 1) < n
    col_w = lane & (n - 1)
    strict_w = row_w > col_w
    incl_w = row_w >= col_w
    zeros_w = jnp.zeros((ln, pw), F32)
    diag = lambda x: _pair_diag(x, even)
    swap = lambda x: jnp.concatenate([x[x.shape[0] // 2:], x[:x.shape[0] // 2]], axis=0)

    ar_p = [jnp.concatenate([a_t[bi][:, ps], r_t[bi][:, ps]], axis=0) for bi, ps in ent]
    bk_p = [jnp.concatenate([b_t[bi][:, ps], k_t[bi][:, ps]], axis=0).astype(BF16) for bi, ps in ent]
    kb_p = [jnp.concatenate([k_t[bi][:, ps], b_t[bi][:, ps]], axis=0).astype(BF16) for bi, ps in ent]
    am_e = _dots([jnp.where(even2, x, 0.0) for x in ar_p], bk_p, _NT)
    am_o = _dots([jnp.where(even2, 0.0, x) for x in ar_p], kb_p, _NT)
    a_ab = [jnp.where(strict_w, jnp.where(even, e[:ln], o[:ln]), 0.0) for e, o in zip(am_e, am_o)]
    a_ak = [jnp.where(strict_w, jnp.where(even, o[:ln], e[:ln]), 0.0) for e, o in zip(am_e, am_o)]
    a_rb = [jnp.where(incl_w, jnp.where(even, e[ln:], o[ln:]), 0.0) for e, o in zip(am_e, am_o)]
    a_rk = [jnp.where(incl_w, jnp.where(even, o[ln:], e[ln:]), 0.0) for e, o in zip(am_e, am_o)]
    v_p = [v[bi][:, ps] for bi, ps in ent]
    v_d = [diag(x) for x in v_p]
    akv = _dots(a_ak, [swap(x) for x in v_d])
    t = _unit_lower_inverse(a_ab, row_w, col_w, even)
    rhs = [jnp.concatenate([diag(a_t[bi][:, ps]), diag(x)], axis=1) for (bi, ps), x in zip(ent, akv)]
    pq = _dots(t, rhs)
    ry = _dots([jnp.concatenate([x, y], axis=1) for x, y in zip(a_rb, a_rk)],
               [jnp.concatenate([jnp.concatenate([diag(x[:, :pw]), diag(x[:, pw:])], axis=1),
                                 jnp.concatenate([jnp.zeros((2 * ln, pw), F32), swap(u)], axis=1)], axis=0)
                for x, u in zip(pq, v_d)])
    pqv_p = [jnp.concatenate([x, jnp.concatenate([zeros_w, u], axis=1)], axis=0)
             for x, u in zip(pq, v_p)]
    bkh_p = [jnp.concatenate([b_h[bi][:, ps], k_h[bi][:, ps]], axis=0) for bi, ps in ent]
    mn_p = _dots(bkh_p, pqv_p, _TN)
    sq_r = lax.broadcasted_iota(jnp.int32, (pw, pw), 0)
    sq_c = lax.broadcasted_iota(jnp.int32, (pw, pw), 1)
    same_head = (sq_r < n) == (sq_c < n)
    lhs_p = [jnp.concatenate([r_t[bi][:, ps] + x[:, :pw],
                              jnp.where(sq_r == sq_c, jnp.broadcast_to(w_end[bi][:, ps], (pw, pw)),
                                        jnp.where(same_head, m[:, :pw], 0.0))], axis=0)
             for (bi, ps), x, m in zip(ent, ry, mn_p)]
    st = [state_ref[e] for e in range(len(ent))]
    upd = _dots(lhs_p, st)
    for e in range(len(ent)):
        state_ref[e] = upd[e][ln:, :] + jnp.where(same_head, mn_p[e][:, pw:], 0.0)
    y0_p = [x[:, pw:] for x in ry]

    hsum = lambda x: jnp.where(even, jnp.sum(jnp.where(even, x, 0.0), axis=-1, keepdims=True),
                               jnp.sum(jnp.where(even, 0.0, x), axis=-1, keepdims=True))
    inv_n = 1.0 / n
    yn = []
    for u, y0 in zip(upd, y0_p):
        y = u[:ln, :] + y0
        d = y - hsum(y) * inv_n
        yn.append(d * lax.rsqrt(hsum(d * d) * inv_n + LNX_EPS))
    for bi in range(nb):
        rk = r[bi] * k[bi] * rk_ref[...]
        mine = range(bi * npair, (bi + 1) * npair)
        bonus = jnp.concatenate([hsum(rk[:, ent[e][1]]) * v_p[e] for e in mine], axis=1)
        out = jnp.concatenate([yn[e] for e in mine], axis=1) * lg_ref[...] + lb_ref[...] + bonus
        y_ref[bi] = (out * g_ref[bi].astype(F32)).astype(y_ref.dtype)


def _wkv(r, lw, k, v, an, b, g, r_k, lnx_g, lnx_b, casts=(), heads_per_step=16):
    bsz, s, c = r.shape
    wb = heads_per_step * RWKV_HEAD
    nc = s // WKV_CHUNK
    assert s % WKV_CHUNK == 0 and c % wb == 0 and heads_per_step % 2 == 0 and WKV_CHUNK == RWKV_HEAD
    assert not casts or c == wb
    spec = pl.BlockSpec((bsz, WKV_CHUNK, wb), lambda hi, ci: (0, ci, hi))
    pspec = pl.BlockSpec((1, wb), lambda hi, ci: (0, hi))
    row = lambda a: a.reshape(1, -1)
    cast_ins, cast_in_specs, cast_out_specs, cast_out_shape = [], [], [], []
    for w, layer in casts:
        nl, rows, cols = w.shape
        assert rows % (BF16_SUBLANES * nc) == 0
        slab = rows // nc
        cast_ins.append(w.reshape(nl, nc, slab, cols))
        cast_in_specs.append(pl.BlockSpec((None, None, slab, cols), lambda hi, ci, layer=layer: (layer, ci, 0, 0)))
        cast_out_specs.append(pl.BlockSpec((None, slab, cols), lambda hi, ci: (ci, 0, 0)))
        cast_out_shape.append(jax.ShapeDtypeStruct((nc, slab, cols), BF16))
    out = pl.pallas_call(
        functools.partial(_wkv_kernel, len(casts)),
        grid=(c // wb, nc),
        in_specs=[spec] * 7 + [pspec] * 3 + cast_in_specs,
        out_specs=[spec] + cast_out_specs,
        out_shape=[jax.ShapeDtypeStruct((bsz, s, c), BF16)] + cast_out_shape,
        scratch_shapes=[pltpu.VMEM((bsz * heads_per_step // 2, 2 * RWKV_HEAD, 2 * RWKV_HEAD), F32)],
        compiler_params=_cparams("parallel", "arbitrary"),
        name="wkv",
    )(r, lw, k, v, an, b, g, row(r_k), row(lnx_g), row(lnx_b), *cast_ins)
    return out[0], [o.reshape(w.shape[1], w.shape[2]) for o, (w, _) in zip(out[1:], casts)]


def _moba_kernel(qs_ref, k_ref, vt_ref, bias_ref, o_ref):
    blk = MOBA_BLOCK
    dh = ATTN_HEAD
    hs = range(qs_ref.shape[2])
    qb = pl.program_id(2)
    neg = -jnp.inf
    qs = [qs_ref[0, 0, h] for h in hs]
    ki = lax.broadcasted_iota(jnp.int32, (blk, blk), 0)
    qi = lax.broadcasted_iota(jnp.int32, (blk, blk), 1)
    causal = ki <= qi

    def pipelined(work, stage):
        scores = lambda kb, h: jnp.dot(k_ref[0, kb, h], qs[h], preferred_element_type=F32)
        ahead = [scores(*w) for w in work[:MOBA_LOOKAHEAD]]
        for i, (kb, h) in enumerate(work):
            if i + MOBA_LOOKAHEAD < len(work):
                ahead.append(scores(*work[i + MOBA_LOOKAHEAD]))
            stage(kb, h, ahead[i])

    m_run, l_run, acc = [None] * len(hs), [None] * len(hs), [None] * len(hs)

    def own_block(kb, h, s):
        s = jnp.where(causal, s, neg)
        m_run[h] = jnp.max(s, axis=0, keepdims=True)
        p = jnp.exp2(s - m_run[h])
        l_run[h] = jnp.sum(p, axis=0, keepdims=True)
        acc[h] = jnp.dot(vt_ref[0, kb, h], p.astype(BF16), preferred_element_type=F32)

    pipelined([(qb, h) for h in hs], own_block)

    def past_blocks(kbs, carry):
        m_c, l_c, acc_c = (list(c) for c in carry)

        def stage(kb, h, s):
            b = bias_ref[0, 0, h, pl.ds(kb, 1), :]
            m_new = jnp.where(b == 0.0, jnp.maximum(m_c[h], jnp.max(s, axis=0, keepdims=True)), m_c[h])
            alpha = jnp.exp2(m_c[h] - m_new)
            p = jnp.exp2(s - (m_new - b))
            pv = jnp.dot(vt_ref[0, kb, h], p.astype(BF16), preferred_element_type=F32)
            m_c[h] = m_new
            l_c[h] = alpha * l_c[h] + jnp.sum(p, axis=0, keepdims=True)
            acc_c[h] = alpha * acc_c[h] + pv

        pipelined([(kb, h) for kb in kbs for h in hs], stage)
        return tuple(m_c), tuple(l_c), tuple(acc_c)

    u = MOBA_UNROLL
    carry = lax.fori_loop(0, qb // u, lambda j, c: past_blocks([j * u + i for i in range(u)], c),
                          (tuple(m_run), tuple(l_run), tuple(acc)))
    _, l_fin, acc = lax.fori_loop((qb // u) * u, qb, lambda kb, c: past_blocks([kb], c), carry)
    for h in hs:
        o_ref[0, :, h * dh:(h + 1) * dh] = (acc[h] / l_fin[h]).T.astype(o_ref.dtype)


def _moba(qs, k, vt, bias, heads_per_step=8):
    bsz, nb, nh, dh, blk = qs.shape
    hp = heads_per_step
    assert nh % hp == 0
    return pl.pallas_call(
        _moba_kernel,
        grid=(bsz, nh // hp, nb),
        in_specs=[pl.BlockSpec((1, 1, hp, dh, blk), lambda b, h, i: (b, i, h, 0, 0)),
                  pl.BlockSpec((1, nb, hp, blk, dh), lambda b, h, i: (b, 0, h, 0, 0)),
                  pl.BlockSpec((1, nb, hp, dh, blk), lambda b, h, i: (b, 0, h, 0, 0)),
                  pl.BlockSpec((1, 1, hp, nb, blk), lambda b, h, i: (b, i, h, 0, 0))],
        out_specs=pl.BlockSpec((1, blk, hp * dh), lambda b, h, i: (b, i, h)),
        out_shape=jax.ShapeDtypeStruct((bsz, nb * blk, nh * dh), BF16),
        compiler_params=_cparams("parallel", "parallel", "arbitrary"),
        name="moba",
    )(qs, k, vt, bias)


def _out_proj_kernel(yr_ref, ya_ref, wr_ref, wa_ref, x_ref, g_ref, o_ref):
    for r in range(0, x_ref.shape[0], ROW_CHUNK):
        rows = pl.ds(r, ROW_CHUNK)
        y = jnp.dot(yr_ref[rows, :], wr_ref[...], preferred_element_type=F32)
        y += jnp.dot(ya_ref[rows, :], wa_ref[...], preferred_element_type=F32)
        ms = jnp.mean(y * y, axis=-1, keepdims=True)
        o_ref[rows, :] = x_ref[rows, :] + y * lax.rsqrt(ms + NORM_EPS) * g_ref[...]


def _out_proj(y_r, y_a, w, x2, gain, tm=512):
    m, d = x2.shape
    cw = y_r.shape[1]
    assert y_a.shape[1] == cw and w.shape[0] == 2 * cw and m % tm == 0
    return pl.pallas_call(
        _out_proj_kernel,
        grid=(m // tm,),
        in_specs=[pl.BlockSpec((tm, cw), lambda i: (i, 0)), pl.BlockSpec((tm, cw), lambda i: (i, 0)),
                  pl.BlockSpec((cw, d), lambda i: (0, 0)), pl.BlockSpec((cw, d), lambda i: (1, 0)),
                  pl.BlockSpec((tm, d), lambda i: (i, 0)), pl.BlockSpec((1, d), lambda i: (0, 0))],
        out_specs=pl.BlockSpec((tm, d), lambda i: (i, 0)),
        out_shape=jax.ShapeDtypeStruct((m, d), F32),
        compiler_params=_cparams("parallel"),
        name="out_proj",
    )(y_r, y_a, w, w, x2, gain.reshape(1, d))


def _mlp_kernel(x_ref, gpre_ref, wu_ref, wd_ref, gpost_ref, o_ref, h_ref, acc_ref):
    f = pl.program_id(1)
    last = pl.num_programs(1) - 1
    chunks = [pl.ds(r, ROW_CHUNK) for r in range(0, x_ref.shape[0], ROW_CHUNK)]

    def part(h):
        u = jnp.maximum(jnp.dot(h, wu_ref[...], preferred_element_type=F32), 0.0)
        return jnp.dot((u * u).astype(BF16), wd_ref[...], preferred_element_type=F32)

    @pl.when(f == 0)
    def _():
        for rows in chunks:
            x = x_ref[rows, :]
            ms = jnp.mean(x * x, axis=-1, keepdims=True)
            h = (x * lax.rsqrt(ms + NORM_EPS) * gpre_ref[...]).astype(BF16)
            h_ref[rows, :] = h
            acc_ref[rows, :] = part(h)

    @pl.when((f > 0) & (f < last))
    def _():
        acc_ref[...] += part(h_ref[...])

    @pl.when(f == last)
    def _():
        for rows in chunks:
            mlp = acc_ref[rows, :] + part(h_ref[rows, :])
            ms = jnp.mean(mlp * mlp, axis=-1, keepdims=True)
            o_ref[rows, :] = x_ref[rows, :] + mlp * lax.rsqrt(ms + NORM_EPS) * gpost_ref[...]


def _mlp(x2, g_pre, w_up, w_down, g_post, tm=512, tf=1024):
    m, d = x2.shape
    dff = w_up.shape[1]
    assert m % tm == 0 and dff % tf == 0 and dff // tf >= 2 and tm % ROW_CHUNK == 0
    return pl.pallas_call(
        _mlp_kernel,
        grid=(m // tm, dff // tf),
        in_specs=[pl.BlockSpec((tm, d), lambda i, f: (i, 0)),
                  pl.BlockSpec((1, d), lambda i, f: (0, 0)),
                  pl.BlockSpec((d, tf), lambda i, f: (0, f)),
                  pl.BlockSpec((tf, d), lambda i, f: (f, 0)),
                  pl.BlockSpec((1, d), lambda i, f: (0, 0))],
        out_specs=pl.BlockSpec((tm, d), lambda i, f: (i, 0)),
        out_shape=jax.ShapeDtypeStruct((m, d), F32),
        scratch_shapes=[pltpu.VMEM((tm, d), BF16), pltpu.VMEM((tm, d), F32)],
        compiler_params=_cparams("parallel", "arbitrary"),
        name="mlp",
    )(x2, g_pre.reshape(1, d), w_up, w_down, g_post.reshape(1, d))


def _rope_tables(s):
    half = ATTN_HEAD // 2
    inv_freq = ROPE_THETA ** (-jnp.arange(half, dtype=F32) / half)
    ang = jnp.arange(s).astype(F32)[:, None] * inv_freq[None, :]
    cos, sin = jnp.cos(ang), jnp.sin(ang)
    return jnp.concatenate([cos, cos], axis=-1), jnp.concatenate([-sin, sin], axis=-1)


def kernel(x, norm_mix_pre, norm_mix_post, norm_mlp_pre, norm_mlp_post, w_in, w_in_vres, shift_mu, shift_mu_vres, decay_w0, decay_w2, iclr_a0, iclr_a2, vres_v0, vres_v2, gate_g2, k_k, k_a, r_k, lnx_gain, lnx_bias, w_out, w_up, w_down):
    bsz, s, d = x.shape
    depth = w_in.shape[0]
    c = decay_w0.shape[1]
    n_lora = DECAY_LORA + ICLR_LORA + GATE_LORA
    n_shift = 3 * c + n_lora
    ca = (w_in.shape[2] - n_shift) // 3
    cos2, sin2 = _rope_tables(s)
    w_in16 = w_in[0].astype(BF16)
    pad_v = VRES_PAD - VRES_LORA
    x2 = x.reshape(bsz * s, d)
    v_first = None
    for i in range(depth):
        if i == 0:
            vres = None
        else:
            vres = (jnp.pad(w_in_vres[i - 1], ((0, 0), (0, pad_v))).astype(BF16),
                    jnp.pad(shift_mu_vres[i - 1], (0, pad_v)), v_first, vres_v0[i - 1],
                    jnp.pad(vres_v2[i - 1], ((0, pad_v), (0, 0))))
        mix, v_layer, attn = _in_proj(x2.reshape(bsz, s, d), norm_mix_pre[i], w_in16, c, shift_mu[i], decay_w0[i],
                                      decay_w2[i], iclr_a0[i], iclr_a2[i], gate_g2[i], k_k[i], k_a[i], cos2, sin2,
                                      vres)
        if i == 0:
            v_first = v_layer

        casts = [(w_out, i), (w_up, i), (w_down, i)] + ([(w_in, i + 1)] if i + 1 < depth else [])
        y_r, w16 = _wkv(*mix, r_k[i].reshape(-1), lnx_gain[i], lnx_bias[i], casts)
        w_out16, w_up16, w_down16 = w16[:3]

        y_a = _moba(*attn)

        x2 = _out_proj(y_r.reshape(bsz * s, c), y_a.reshape(bsz * s, ca), w_out16, x2, norm_mix_post[i])
        x2 = _mlp(x2, norm_mlp_pre[i], w_up16, w_down16, norm_mlp_post[i])
        if i + 1 < depth:
            w_in16 = w16[3]
    return x2.reshape(bsz, s, d)
```

```python
import functools

import jax
import jax.numpy as jnp
from jax import lax
from jax.experimental import pallas as pl
from jax.experimental.pallas import tpu as pltpu

F32 = jnp.float32
BF16 = jnp.bfloat16

RWKV_HEAD = 64
DECAY_LORA = 64
ICLR_LORA = 64
VRES_LORA = 32
GATE_LORA = 128
ATTN_HEAD = 128
MOBA_BLOCK = 256
MOBA_TOPK = 3
ROPE_THETA = 10000.0
NORM_EPS = 1e-6
LNX_EPS = 64e-5
LOG2E = 1.4426950408889634

LANES = 128
BF16_SUBLANES = 16
VRES_PAD = LANES
WKV_CHUNK = 64
MOBA_LOOKAHEAD = 6
MOBA_UNROLL = 4
ROW_CHUNK = 256
VMEM_LIMIT = 56 * 1024 * 1024


def _cparams(*sem):
    return pltpu.CompilerParams(dimension_semantics=sem, vmem_limit_bytes=VMEM_LIMIT)


_NN = (((1,), (0,)), ((), ()))
_NT = (((1,), (1,)), ((), ()))
_TN = (((0,), (0,)), ((), ()))


def _split2(x):
    hi = x.astype(BF16)
    lo = (x - hi.astype(F32)).astype(BF16)
    return hi, lo


def _split3(x):
    hi = x.astype(BF16)
    r1 = x - hi.astype(F32)
    mid = r1.astype(BF16)
    lo = (r1 - mid.astype(F32)).astype(BF16)
    return hi, mid, lo


def _mm(a, b, dims=_NN, passes=1):
    d = lambda p, q: lax.dot_general(p, q, dims, preferred_element_type=F32)
    if passes == 1:
        return d(a.astype(BF16), b.astype(BF16))
    ah, al = _split2(a)
    bh, bl = _split2(b)
    return d(ah, bh) + (d(ah, bl) + d(al, bh))


def _sigmoid(x):
    return 1.0 / (1.0 + jnp.exp(-x))


def _softplus(x):
    return jnp.maximum(x, 0.0) + jnp.log(1.0 + jnp.exp(-jnp.abs(x)))


def _in_proj_kernel(has_vres, c, spb, *refs):
    it = iter(refs)
    take = lambda cnt: [next(it) for _ in range(cnt)]
    x_ref, g_ref, w_ref, cos_ref, sin_ref, mum_ref, mul_ref, w0_ref, w2_ref, a0_ref, a2_ref, g2_ref, kk_ref, ka_ref = take(14)
    if has_vres:
        we_ref, muv_ref, vf_ref, v0_ref, v2_ref = take(5)
    r_ref, lw_ref, k_ref, v_ref, an_ref, b_ref, gate_ref = take(7)
    if not has_vres:
        vf32_ref, = take(1)
    qs_ref, ko_ref, vt_ref, bias_ref = take(4)
    km_ref, cm_ref, cl_ref = take(3)
    if has_vres:
        cv_ref, = take(1)

    i = pl.program_id(0)
    nb = km_ref.shape[0]
    blk = MOBA_BLOCK
    width = km_ref.shape[1]
    nh = width // ATTN_HEAD
    n_lora = mul_ref.shape[1]
    n_mix = 3 * c + n_lora
    pw = 2 * RWKV_HEAD

    @pl.when(i == 0)
    def _():
        for ref in [km_ref, cm_ref, cl_ref] + ([cv_ref] if has_vres else []):
            ref[...] = jnp.zeros_like(ref)

    blk_id = lax.broadcasted_iota(jnp.int32, (nb, blk), 0)
    km_row = lax.broadcasted_iota(jnp.int32, (nb, ATTN_HEAD), 0)
    first_row = lax.broadcasted_iota(jnp.int32, (blk, 1), 0) == 0
    even = lax.broadcasted_iota(jnp.int32, (blk, pw), 1) < RWKV_HEAD
    qk_scale = (ATTN_HEAD ** -0.5) * LOG2E

    def shifted(cur, carry_ref, lo, hi, mu_ref, start):
        last = jnp.where(start, 0.0, carry_ref[:, lo:hi])
        prev = jnp.where(first_row, last, pltpu.roll(cur, 1, axis=0))
        carry_ref[:, lo:hi] = cur[blk - 1:blk, :]
        return cur + (prev - cur) * mu_ref[:, lo:hi]

    for j in range(x_ref.shape[0] // blk):
        rows = pl.ds(j * blk, blk)
        x = x_ref[rows, :]
        ms = jnp.mean(x * x, axis=-1, keepdims=True)
        h = (x * lax.rsqrt(ms + NORM_EPS) * g_ref[...]).astype(BF16)
        proj = lambda c0, c1: jnp.dot(h, w_ref[:, c0:c1], preferred_element_type=F32)
        bid = (i % spb) * (x_ref.shape[0] // blk) + j
        start = bid == 0

        z_r, z_k, z_v, z_l = proj(0, c), proj(c, 2 * c), proj(2 * c, 3 * c), proj(3 * c, n_mix)
        if has_vres:
            z_e = jnp.dot(h, we_ref[...], preferred_element_type=F32)
        q_all, k_all, v_all = (proj(n_mix + t * width, n_mix + (t + 1) * width) for t in range(3))

        r = shifted(z_r, cm_ref, 0, c, mum_ref, start)
        k = shifted(z_k, cm_ref, c, 2 * c, mum_ref, start)
        v = shifted(z_v, cm_ref, 2 * c, 3 * c, mum_ref, start)
        zl = shifted(z_l, cl_ref, 0, n_lora, mul_ref, start)
        o = 0
        wd = zl[:, o:o + DECAY_LORA]
        o += DECAY_LORA
        ad = zl[:, o:o + ICLR_LORA]
        o += ICLR_LORA
        gd = zl[:, o:o + GATE_LORA]
        w_log = -_softplus(-(w0_ref[...] + _mm(jnp.tanh(wd), w2_ref[...], passes=3))) - 0.5
        lw_ref[rows, :] = -jnp.exp(w_log)
        a = _sigmoid(a0_ref[...] + _mm(ad, a2_ref[...]))
        gate_ref[rows, :] = _mm(_sigmoid(gd), g2_ref[...]).astype(gate_ref.dtype)
        if has_vres:
            vd = shifted(z_e, cv_ref, 0, z_e.shape[1], muv_ref, start)
            v = v + (vf_ref[rows, :] - v) * _sigmoid(v0_ref[...] + _mm(vd, v2_ref[...]))
        else:
            vf32_ref[rows, :] = v
        kk = k * kk_ref[...]
        unit = []
        for p in range(c // pw):
            xk = kk[:, p * pw:(p + 1) * pw]
            sq = xk * xk
            ss = jnp.where(even, jnp.sum(jnp.where(even, sq, 0.0), axis=-1, keepdims=True),
                           jnp.sum(jnp.where(even, 0.0, sq), axis=-1, keepdims=True))
            unit.append(xk / jnp.maximum(jnp.sqrt(ss), 1e-12))
        kk = jnp.concatenate(unit, axis=1)
        r_ref[rows, :] = r.astype(r_ref.dtype)
        k_ref[rows, :] = (k * (1.0 + (a - 1.0) * ka_ref[...])).astype(k_ref.dtype)
        v_ref[rows, :] = v.astype(v_ref.dtype)
        an_ref[rows, :] = (-kk).astype(an_ref.dtype)
        b_ref[rows, :] = (kk * a).astype(b_ref.dtype)

        cos = cos_ref[rows, :]
        sin = sin_ref[rows, :]
        past = blk_id < bid
        for hd in range(nh):
            sl = slice(hd * ATTN_HEAD, (hd + 1) * ATTN_HEAD)
            q, ka = q_all[:, sl], k_all[:, sl]
            qr = (q * cos + pltpu.roll(q, ATTN_HEAD // 2, axis=1) * sin).T
            kr = ka * cos + pltpu.roll(ka, ATTN_HEAD // 2, axis=1) * sin
            qs_ref[0, j, hd] = (qr * qk_scale).astype(BF16)
            ko_ref[0, j, hd] = kr.astype(BF16)
            vt_ref[0, j, hd] = v_all[:, sl].T.astype(BF16)
            gate = jnp.where(past, _mm(km_ref[:, sl], qr, passes=3), -jnp.inf)
            rank = jnp.zeros((nb, blk), jnp.int32)
            for m in range(nb):
                gm = gate[m:m + 1, :]
                rank += ((gm > gate) | ((gm == gate) & (m < blk_id))).astype(jnp.int32)
            bias_ref[0, j, hd] = jnp.where(past & (rank < MOBA_TOPK), 0.0, -jnp.inf)
            km_new = jnp.sum(kr, axis=0, keepdims=True) * (1.0 / blk)
            km_ref[:, sl] = jnp.where(km_row == bid, km_new, km_ref[:, sl])


def _in_proj(x3, gain, w, c, mu, w0, w2, a0, a2, g2, k_k, k_a, cos2, sin2, vres=None, tm=256):
    bsz, s, d = x3.shape
    n = w.shape[1]
    n_lora = w2.shape[0] + a2.shape[0] + g2.shape[0]
    n_mix = 3 * c + n_lora
    width = (n - n_mix) // 3
    nh = width // ATTN_HEAD
    nb = s // MOBA_BLOCK
    bpt = tm // MOBA_BLOCK
    spb = s // tm
    assert s % tm == 0 and tm % MOBA_BLOCK == 0 and c % (2 * RWKV_HEAD) == 0 and n_lora % LANES == 0
    assert (n - n_mix) % (3 * ATTN_HEAD) == 0
    has_vres = vres is not None
    once = pl.Buffered(1)
    row = lambda a: a.reshape(1, -1)
    full = lambda a: pl.BlockSpec(a.shape, lambda i: (0,) * a.ndim)
    tile = lambda wd_: pl.BlockSpec((tm, wd_), lambda i: (i, 0))
    small = [row(mu[:3 * c]), row(mu[3 * c:]), row(w0), w2, row(a0), a2, g2, row(k_k), row(k_a)]
    ins = [x3.reshape(bsz * s, d), gain.reshape(1, d), w, cos2, sin2] + small
    in_specs = [tile(d), pl.BlockSpec((1, d), lambda i: (0, 0)),
                pl.BlockSpec((d, n), lambda i: (0, 0), pipeline_mode=once),
                pl.BlockSpec((tm, ATTN_HEAD), lambda i: (i % spb, 0)),
                pl.BlockSpec((tm, ATTN_HEAD), lambda i: (i % spb, 0))] + [full(a) for a in small]
    scratch = [pltpu.VMEM((nb, width), F32), pltpu.VMEM((1, 3 * c), F32), pltpu.VMEM((1, n_lora), F32)]
    if has_vres:
        w_v, mu_v, v_first, v0, v2 = vres
        extra = [w_v, row(mu_v), v_first, row(v0), v2]
        ins += extra
        in_specs += [pl.BlockSpec(w_v.shape, lambda i: (0, 0), pipeline_mode=once), full(extra[1]), tile(c),
                     full(extra[3]), full(extra[4])]
        scratch.append(pltpu.VMEM((1, w_v.shape[1]), F32))
    mix_dtypes = [BF16, F32, BF16, BF16, BF16, BF16, BF16] + ([] if has_vres else [F32])
    t_spec = pl.BlockSpec((1, bpt, nh, ATTN_HEAD, MOBA_BLOCK), lambda i: (i // spb, i % spb, 0, 0, 0))
    n_spec = pl.BlockSpec((1, bpt, nh, MOBA_BLOCK, ATTN_HEAD), lambda i: (i // spb, i % spb, 0, 0, 0))
    out_specs = [tile(c)] * len(mix_dtypes) + [
        t_spec, n_spec, t_spec, pl.BlockSpec((1, bpt, nh, nb, MOBA_BLOCK), lambda i: (i // spb, i % spb, 0, 0, 0))]
    out_shape = [jax.ShapeDtypeStruct((bsz * s, c), dt) for dt in mix_dtypes] + [
        jax.ShapeDtypeStruct((bsz, nb, nh, ATTN_HEAD, MOBA_BLOCK), BF16),
        jax.ShapeDtypeStruct((bsz, nb, nh, MOBA_BLOCK, ATTN_HEAD), BF16),
        jax.ShapeDtypeStruct((bsz, nb, nh, ATTN_HEAD, MOBA_BLOCK), BF16),
        jax.ShapeDtypeStruct((bsz, nb, nh, nb, MOBA_BLOCK), F32)]
    out = pl.pallas_call(
        functools.partial(_in_proj_kernel, has_vres, c, spb),
        grid=(bsz * spb,),
        in_specs=in_specs,
        out_specs=out_specs,
        out_shape=out_shape,
        scratch_shapes=scratch,
        compiler_params=_cparams("arbitrary"),
        name="in_proj",
    )(*ins)
    mix = [o.reshape(bsz, s, c) for o in out[:7]]
    return mix, (None if has_vres else out[7]), out[-4:]


def _dots(a_list, b_list, dims=_NN):
    return [lax.dot_general(a.astype(BF16), b.astype(BF16), dims, preferred_element_type=F32)
            for a, b in zip(a_list, b_list)]


def _pair_diag(x, even):
    return jnp.concatenate([jnp.where(even, x, 0.0), jnp.where(even, 0.0, x)], axis=0)


def _unit_lower_inverse(a_list, row_w, col_w, even):
    n = row_w.shape[0]
    lower = row_w > col_w
    base = lower & ((row_w >> 1) == (col_w >> 1))
    t = [jnp.where(row_w == col_w, 1.0, jnp.where(base, a, 0.0)) for a in a_list]
    sh = 1
    while (2 << sh) <= n:
        sub = lower & ((row_w >> (sh + 1)) == (col_w >> (sh + 1))) & ((row_w >> sh) != (col_w >> sh))
        off = [_pair_diag(jnp.where(sub, a, 0.0), even) for a in a_list]
        upd = _dots(_dots(t, off), [_pair_diag(x, even) for x in t])
        t = [x + u for x, u in zip(t, upd)]
        sh += 1
    return t


def _mm_exact_rhs_left(l_bf16, a):
    d = lambda p: lax.dot_general(l_bf16, p, _NN, preferred_element_type=F32)
    hi, mid, lo = _split3(a)
    return d(hi) + (d(mid) + d(lo))


def _wkv_kernel(n_cast, *refs):
    r_ref, lw_ref, k_ref, v_ref, an_ref, b_ref, g_ref, rk_ref, lg_ref, lb_ref = refs[:10]
    cast_in, y_ref = refs[10:10 + n_cast], refs[10 + n_cast]
    cast_out, state_ref = refs[11 + n_cast:11 + 2 * n_cast], refs[11 + 2 * n_cast]
    for src, dst in zip(cast_in, cast_out):
        dst[...] = src[...].astype(dst.dtype)

    nb, ln = r_ref.shape[0], r_ref.shape[1]
    n = RWKV_HEAD
    pw = 2 * n
    npair = r_ref.shape[2] // pw
    ent = [(bi, slice(p * pw, (p + 1) * pw)) for bi in range(nb) for p in range(npair)]

    @pl.when(pl.program_id(1) == 0)
    def _():
        state_ref[...] = jnp.zeros_like(state_ref)

    rows = lax.broadcasted_iota(jnp.int32, (ln, ln), 0)
    cols = lax.broadcasted_iota(jnp.int32, (ln, ln), 1)
    tril = jnp.where(rows >= cols, 1.0, 0.0).astype(BF16)
    r, k, v, w_end, r_t, a_t, b_t, k_t, b_h, k_h = ([] for _ in range(10))
    for bi in range(nb):
        lw = lw_ref[bi]
        cw = _mm_exact_rhs_left(tril, lw)
        cw_end = cw[ln - 1:ln, :]
        e_neg = jnp.exp(-cw)
        e_end = jnp.exp(cw_end - cw)
        bb = b_ref[bi].astype(F32)
        r.append(r_ref[bi].astype(F32))
        k.append(k_ref[bi].astype(F32))
        v.append(v_ref[bi].astype(F32))
        w_end.append(jnp.exp(cw_end))
        r_t.append(r[bi] * jnp.exp(cw))
        a_t.append(an_ref[bi].astype(F32) * jnp.exp(cw - lw))
        b_t.append(bb * e_neg)
        k_t.append(k[bi] * e_neg)
        b_h.append(bb * e_end)
        k_h.append(k[bi] * e_end)

    lane = lax.broadcasted_iota(jnp.int32, (ln, pw), 1)
    row_w = lax.broadcasted_iota(jnp.int32, (ln, pw), 0)
    even = lane < n
    even2 = lax.broadcasted_iota(jnp.int32, (2 * ln, pw), 1) < n
    col_w = lane & (n - 1)
    strict_w = row_w > col_w
    incl_w = row_w >= col_w
    zeros_w = jnp.zeros((ln, pw), F32)
    diag = lambda x: _pair_diag(x, even)
    swap = lambda x: jnp.concatenate([x[x.shape[0] // 2:], x[:x.shape[0] // 2]], axis=0)

    ar_p = [jnp.concatenate([a_t[bi][:, ps], r_t[bi][:, ps]], axis=0) for bi, ps in ent]
    bk_p = [jnp.concatenate([b_t[bi][:, ps], k_t[bi][:, ps]], axis=0).astype(BF16) for bi, ps in ent]
    kb_p = [jnp.concatenate([k_t[bi][:, ps], b_t[bi][:, ps]], axis=0).astype(BF16) for bi, ps in ent]
    am_e = _dots([jnp.where(even2, x, 0.0) for x in ar_p], bk_p, _NT)
    am_o = _dots([jnp.where(even2, 0.0, x) for x in ar_p], kb_p, _NT)
    a_ab = [jnp.where(strict_w, jnp.where(even, e[:ln], o[:ln]), 0.0) for e, o in zip(am_e, am_o)]
    a_ak = [jnp.where(strict_w, jnp.where(even, o[:ln], e[:ln]), 0.0) for e, o in zip(am_e, am_o)]
    a_rb = [jnp.where(incl_w, jnp.where(even, e[ln:], o[ln:]), 0.0) for e, o in zip(am_e, am_o)]
    a_rk = [jnp.where(incl_w, jnp.where(even, o[ln:], e[ln:]), 0.0) for e, o in zip(am_e, am_o)]
    v_p = [v[bi][:, ps] for bi, ps in ent]
    v_d = [diag(x) for x in v_p]
    akv = _dots(a_ak, [swap(x) for x in v_d])
    t = _unit_lower_inverse(a_ab, row_w, col_w, even)
    rhs = [jnp.concatenate([diag(a_t[bi][:, ps]), diag(x)], axis=1) for (bi, ps), x in zip(ent, akv)]
    pq = _dots(t, rhs)
    ry = _dots([jnp.concatenate([x, y], axis=1) for x, y in zip(a_rb, a_rk)],
               [jnp.concatenate([jnp.concatenate([diag(x[:, :pw]), diag(x[:, pw:])], axis=1),
                                 jnp.concatenate([jnp.zeros((2 * ln, pw), F32), swap(u)], axis=1)], axis=0)
                for x, u in zip(pq, v_d)])
    pqv_p = [jnp.concatenate([x, jnp.concatenate([zeros_w, u], axis=1)], axis=0)
             for x, u in zip(pq, v_p)]
    bkh_p = [jnp.concatenate([b_h[bi][:, ps], k_h[bi][:, ps]], axis=0) for bi, ps in ent]
    mn_p = _dots(bkh_p, pqv_p, _TN)
    sq_r = lax.broadcasted_iota(jnp.int32, (pw, pw), 0)
    sq_c = lax.broadcasted_iota(jnp.int32, (pw, pw), 1)
    same_head = (sq_r < n) == (sq_c < n)
    lhs_p = [jnp.concatenate([r_t[bi][:, ps] + x[:, :pw],
                              jnp.where(sq_r == sq_c, jnp.broadcast_to(w_end[bi][:, ps], (pw, pw)),
                                        jnp.where(same_head, m[:, :pw], 0.0))], axis=0)
             for (bi, ps), x, m in zip(ent, ry, mn_p)]
    st = [state_ref[e] for e in range(len(ent))]
    upd = _dots(lhs_p, st)
    for e in range(len(ent)):
        state_ref[e] = upd[e][ln:, :] + jnp.where(same_head, mn_p[e][:, pw:], 0.0)
    y0_p = [x[:, pw:] for x in ry]

    hsum = lambda x: jnp.where(even, jnp.sum(jnp.where(even, x, 0.0), axis=-1, keepdims=True),
                               jnp.sum(jnp.where(even, 0.0, x), axis=-1, keepdims=True))
    inv_n = 1.0 / n
    yn = []
    for u, y0 in zip(upd, y0_p):
        y = u[:ln, :] + y0
        d = y - hsum(y) * inv_n
        yn.append(d * lax.rsqrt(hsum(d * d) * inv_n + LNX_EPS))
    for bi in range(nb):
        rk = r[bi] * k[bi] * rk_ref[...]
        mine = range(bi * npair, (bi + 1) * npair)
        bonus = jnp.concatenate([hsum(rk[:, ent[e][1]]) * v_p[e] for e in mine], axis=1)
        out = jnp.concatenate([yn[e] for e in mine], axis=1) * lg_ref[...] + lb_ref[...] + bonus
        y_ref[bi] = (out * g_ref[bi].astype(F32)).astype(y_ref.dtype)


def _wkv(r, lw, k, v, an, b, g, r_k, lnx_g, lnx_b, casts=(), heads_per_step=16):
    bsz, s, c = r.shape
    wb = heads_per_step * RWKV_HEAD
    nc = s // WKV_CHUNK
    assert s % WKV_CHUNK == 0 and c % wb == 0 and heads_per_step % 2 == 0 and WKV_CHUNK == RWKV_HEAD
    assert not casts or c == wb
    spec = pl.BlockSpec((bsz, WKV_CHUNK, wb), lambda hi, ci: (0, ci, hi))
    pspec = pl.BlockSpec((1, wb), lambda hi, ci: (0, hi))
    row = lambda a: a.reshape(1, -1)
    cast_ins, cast_in_specs, cast_out_specs, cast_out_shape = [], [], [], []
    for w, layer in casts:
        nl, rows, cols = w.shape
        assert rows % (BF16_SUBLANES * nc) == 0
        slab = rows // nc
        cast_ins.append(w.reshape(nl, nc, slab, cols))
        cast_in_specs.append(pl.BlockSpec((None, None, slab, cols), lambda hi, ci, layer=layer: (layer, ci, 0, 0)))
        cast_out_specs.append(pl.BlockSpec((None, slab, cols), lambda hi, ci: (ci, 0, 0)))
        cast_out_shape.append(jax.ShapeDtypeStruct((nc, slab, cols), BF16))
    out = pl.pallas_call(
        functools.partial(_wkv_kernel, len(casts)),
        grid=(c // wb, nc),
        in_specs=[spec] * 7 + [pspec] * 3 + cast_in_specs,
        out_specs=[spec] + cast_out_specs,
        out_shape=[jax.ShapeDtypeStruct((bsz, s, c), BF16)] + cast_out_shape,
        scratch_shapes=[pltpu.VMEM((bsz * heads_per_step // 2, 2 * RWKV_HEAD, 2 * RWKV_HEAD), F32)],
        compiler_params=_cparams("parallel", "arbitrary"),
        name="wkv",
    )(r, lw, k, v, an, b, g, row(r_k), row(lnx_g), row(lnx_b), *cast_ins)
    return out[0], [o.reshape(w.shape[1], w.shape[2]) for o, (w, _) in zip(out[1:], casts)]


def _moba_kernel(qs_ref, k_ref, vt_ref, bias_ref, o_ref):
    blk = MOBA_BLOCK
    dh = ATTN_HEAD
    hs = range(qs_ref.shape[2])
    qb = pl.program_id(2)
    neg = -jnp.inf
    qs = [qs_ref[0, 0, h] for h in hs]
    ki = lax.broadcasted_iota(jnp.int32, (blk, blk), 0)
    qi = lax.broadcasted_iota(jnp.int32, (blk, blk), 1)
    causal = ki <= qi

    def pipelined(work, stage):
        scores = lambda kb, h: jnp.dot(k_ref[0, kb, h], qs[h], preferred_element_type=F32)
        ahead = [scores(*w) for w in work[:MOBA_LOOKAHEAD]]
        for i, (kb, h) in enumerate(work):
            if i + MOBA_LOOKAHEAD < len(work):
                ahead.append(scores(*work[i + MOBA_LOOKAHEAD]))
            stage(kb, h, ahead[i])

    m_run, l_run, acc = [None] * len(hs), [None] * len(hs), [None] * len(hs)

    def own_block(kb, h, s):
        s = jnp.where(causal, s, neg)
        m_run[h] = jnp.max(s, axis=0, keepdims=True)
        p = jnp.exp2(s - m_run[h])
        l_run[h] = jnp.sum(p, axis=0, keepdims=True)
        acc[h] = jnp.dot(vt_ref[0, kb, h], p.astype(BF16), preferred_element_type=F32)

    pipelined([(qb, h) for h in hs], own_block)

    def past_blocks(kbs, carry):
        m_c, l_c, acc_c = (list(c) for c in carry)

        def stage(kb, h, s):
            b = bias_ref[0, 0, h, pl.ds(kb, 1), :]
            m_new = jnp.where(b == 0.0, jnp.maximum(m_c[h], jnp.max(s, axis=0, keepdims=True)), m_c[h])
            alpha = jnp.exp2(m_c[h] - m_new)
            p = jnp.exp2(s - (m_new - b))
            pv = jnp.dot(vt_ref[0, kb, h], p.astype(BF16), preferred_element_type=F32)
            m_c[h] = m_new
            l_c[h] = alpha * l_c[h] + jnp.sum(p, axis=0, keepdims=True)
            acc_c[h] = alpha * acc_c[h] + pv

        pipelined([(kb, h) for kb in kbs for h in hs], stage)
        return tuple(m_c), tuple(l_c), tuple(acc_c)

    u = MOBA_UNROLL
    carry = lax.fori_loop(0, qb // u, lambda j, c: past_blocks([j * u + i for i in range(u)], c),
                          (tuple(m_run), tuple(l_run), tuple(acc)))
    _, l_fin, acc = lax.fori_loop((qb // u) * u, qb, lambda kb, c: past_blocks([kb], c), carry)
    for h in hs:
        o_ref[0, :, h * dh:(h + 1) * dh] = (acc[h] / l_fin[h]).T.astype(o_ref.dtype)


def _moba(qs, k, vt, bias, heads_per_step=8):
    bsz, nb, nh, dh, blk = qs.shape
    hp = heads_per_step
    assert nh % hp == 0
    return pl.pallas_call(
        _moba_kernel,
        grid=(bsz, nh // hp, nb),
        in_specs=[pl.BlockSpec((1, 1, hp, dh, blk), lambda b, h, i: (b, i, h, 0, 0)),
                  pl.BlockSpec((1, nb, hp, blk, dh), lambda b, h, i: (b, 0, h, 0, 0)),
                  pl.BlockSpec((1, nb, hp, dh, blk), lambda b, h, i: (b, 0, h, 0, 0)),
                  pl.BlockSpec((1, 1, hp, nb, blk), lambda b, h, i: (b, i, h, 0, 0))],
        out_specs=pl.BlockSpec((1, blk, hp * dh), lambda b, h, i: (b, i, h)),
        out_shape=jax.ShapeDtypeStruct((bsz, nb * blk, nh * dh), BF16),
        compiler_params=_cparams("parallel", "parallel", "arbitrary"),
        name="moba",
    )(qs, k, vt, bias)


def _out_proj_kernel(yr_ref, ya_ref, wr_ref, wa_ref, x_ref, g_ref, o_ref):
    for r in range(0, x_ref.shape[0], ROW_CHUNK):
        rows = pl.ds(r, ROW_CHUNK)
        y = jnp.dot(yr_ref[rows, :], wr_ref[...], preferred_element_type=F32)
        y += jnp.dot(ya_ref[rows, :], wa_ref[...], preferred_element_type=F32)
        ms = jnp.mean(y * y, axis=-1, keepdims=True)
        o_ref[rows, :] = x_ref[rows, :] + y * lax.rsqrt(ms + NORM_EPS) * g_ref[...]


def _out_proj(y_r, y_a, w, x2, gain, tm=512):
    m, d = x2.shape
    cw = y_r.shape[1]
    assert y_a.shape[1] == cw and w.shape[0] == 2 * cw and m % tm == 0
    return pl.pallas_call(
        _out_proj_kernel,
        grid=(m // tm,),
        in_specs=[pl.BlockSpec((tm, cw), lambda i: (i, 0)), pl.BlockSpec((tm, cw), lambda i: (i, 0)),
                  pl.BlockSpec((cw, d), lambda i: (0, 0)), pl.BlockSpec((cw, d), lambda i: (1, 0)),
                  pl.BlockSpec((tm, d), lambda i: (i, 0)), pl.BlockSpec((1, d), lambda i: (0, 0))],
        out_specs=pl.BlockSpec((tm, d), lambda i: (i, 0)),
        out_shape=jax.ShapeDtypeStruct((m, d), F32),
        compiler_params=_cparams("parallel"),
        name="out_proj",
    )(y_r, y_a, w, w, x2, gain.reshape(1, d))


def _mlp_kernel(x_ref, gpre_ref, wu_ref, wd_ref, gpost_ref, o_ref, h_ref):
    f = pl.program_id(1)
    last = pl.num_programs(1) - 1
    chunks = [pl.ds(r, ROW_CHUNK) for r in range(0, x_ref.shape[0], ROW_CHUNK)]

    def part(h):
        u = jnp.maximum(jnp.dot(h, wu_ref[...], preferred_element_type=F32), 0.0)
        return jnp.dot((u * u).astype(BF16), wd_ref[...], preferred_element_type=F32)

    @pl.when(f == 0)
    def _():
        for rows in chunks:
            x = x_ref[rows, :]
            ms = jnp.mean(x * x, axis=-1, keepdims=True)
            h = (x * lax.rsqrt(ms + NORM_EPS) * gpre_ref[...]).astype(BF16)
            h_ref[rows, :] = h
            o_ref[rows, :] = part(h)

    @pl.when((f > 0) & (f < last))
    def _():
        for rows in chunks:
            o_ref[rows, :] += part(h_ref[rows, :])

    @pl.when(f == last)
    def _():
        for rows in chunks:
            mlp = o_ref[rows, :] + part(h_ref[rows, :])
            ms = jnp.mean(mlp * mlp, axis=-1, keepdims=True)
            o_ref[rows, :] = x_ref[rows, :] + mlp * lax.rsqrt(ms + NORM_EPS) * gpost_ref[...]


def _mlp(x2, g_pre, w_up, w_down, g_post, tm=1024, tf=1024):
    m, d = x2.shape
    dff = w_up.shape[1]
    assert m % tm == 0 and dff % tf == 0 and dff // tf >= 2 and tm % ROW_CHUNK == 0
    return pl.pallas_call(
        _mlp_kernel,
        grid=(m // tm, dff // tf),
        in_specs=[pl.BlockSpec((tm, d), lambda i, f: (i, 0)),
                  pl.BlockSpec((1, d), lambda i, f: (0, 0)),
                  pl.BlockSpec((d, tf), lambda i, f: (0, f)),
                  pl.BlockSpec((tf, d), lambda i, f: (f, 0)),
                  pl.BlockSpec((1, d), lambda i, f: (0, 0))],
        out_specs=pl.BlockSpec((tm, d), lambda i, f: (i, 0)),
        out_shape=jax.ShapeDtypeStruct((m, d), F32),
        scratch_shapes=[pltpu.VMEM((tm, d), BF16)],
        compiler_params=_cparams("parallel", "arbitrary"),
        name="mlp",
    )(x2, g_pre.reshape(1, d), w_up, w_down, g_post.reshape(1, d))


def _rope_tables(s):
    half = ATTN_HEAD // 2
    inv_freq = ROPE_THETA ** (-jnp.arange(half, dtype=F32) / half)
    ang = jnp.arange(s).astype(F32)[:, None] * inv_freq[None, :]
    cos, sin = jnp.cos(ang), jnp.sin(ang)
    return jnp.concatenate([cos, cos], axis=-1), jnp.concatenate([-sin, sin], axis=-1)


def kernel(x, norm_mix_pre, norm_mix_post, norm_mlp_pre, norm_mlp_post, w_in, w_in_vres, shift_mu, shift_mu_vres, decay_w0, decay_w2, iclr_a0, iclr_a2, vres_v0, vres_v2, gate_g2, k_k, k_a, r_k, lnx_gain, lnx_bias, w_out, w_up, w_down):
    bsz, s, d = x.shape
    depth = w_in.shape[0]
    c = decay_w0.shape[1]
    n_lora = DECAY_LORA + ICLR_LORA + GATE_LORA
    n_shift = 3 * c + n_lora
    ca = (w_in.shape[2] - n_shift) // 3
    cos2, sin2 = _rope_tables(s)
    w_in16 = w_in[0].astype(BF16)
    pad_v = VRES_PAD - VRES_LORA
    x2 = x.reshape(bsz * s, d)
    v_first = None
    for i in range(depth):
        if i == 0:
            vres = None
        else:
            vres = (jnp.pad(w_in_vres[i - 1], ((0, 0), (0, pad_v))).astype(BF16),
                    jnp.pad(shift_mu_vres[i - 1], (0, pad_v)), v_first, vres_v0[i - 1],
                    jnp.pad(vres_v2[i - 1], ((0, pad_v), (0, 0))))
        mix, v_layer, attn = _in_proj(x2.reshape(bsz, s, d), norm_mix_pre[i], w_in16, c, shift_mu[i], decay_w0[i],
                                      decay_w2[i], iclr_a0[i], iclr_a2[i], gate_g2[i], k_k[i], k_a[i], cos2, sin2,
                                      vres)
        if i == 0:
            v_first = v_layer

        casts = [(w_out, i), (w_up, i), (w_down, i)] + ([(w_in, i + 1)] if i + 1 < depth else [])
        y_r, w16 = _wkv(*mix, r_k[i].reshape(-1), lnx_gain[i], lnx_bias[i], casts)
        w_out16, w_up16, w_down16 = w16[:3]

        y_a = _moba(*attn)

        x2 = _out_proj(y_r.reshape(bsz * s, c), y_a.reshape(bsz * s, ca), w_out16, x2, norm_mix_post[i])
        x2 = _mlp(x2, norm_mlp_pre[i], w_up16, w_down16, norm_mlp_post[i])
        if i + 1 < depth:
            w_in16 = w16[3]
    return x2.reshape(bsz, s, d)
```

```python
import functools

import jax
import jax.numpy as jnp
from jax import lax
from jax.experimental import pallas as pl
from jax.experimental.pallas import tpu as pltpu

F32 = jnp.float32
BF16 = jnp.bfloat16

RWKV_HEAD = 64
DECAY_LORA = 64
ICLR_LORA = 64
VRES_LORA = 32
GATE_LORA = 128
ATTN_HEAD = 128
MOBA_BLOCK = 256
MOBA_TOPK = 3
ROPE_THETA = 10000.0
NORM_EPS = 1e-6
LNX_EPS = 64e-5
LOG2E = 1.4426950408889634

LANES = 128
BF16_SUBLANES = 16
VRES_PAD = LANES
WKV_CHUNK = 64
MOBA_LOOKAHEAD = 6
MOBA_UNROLL = 4
ROW_CHUNK = 256
VMEM_LIMIT = 56 * 1024 * 1024


def _cparams(*sem):
    return pltpu.CompilerParams(dimension_semantics=sem, vmem_limit_bytes=VMEM_LIMIT)


_NN = (((1,), (0,)), ((), ()))
_NT = (((1,), (1,)), ((), ()))
_TN = (((0,), (0,)), ((), ()))


def _split2(x):
    hi = x.astype(BF16)
    lo = (x - hi.astype(F32)).astype(BF16)
    return hi, lo


def _split3(x):
    hi = x.astype(BF16)
    r1 = x - hi.astype(F32)
    mid = r1.astype(BF16)
    lo = (r1 - mid.astype(F32)).astype(BF16)
    return hi, mid, lo


def _mm(a, b, dims=_NN, passes=1):
    d = lambda p, q: lax.dot_general(p, q, dims, preferred_element_type=F32)
    if passes == 1:
        return d(a.astype(BF16), b.astype(BF16))
    ah, al = _split2(a)
    bh, bl = _split2(b)
    return d(ah, bh) + (d(ah, bl) + d(al, bh))


def _sigmoid(x):
    return 1.0 / (1.0 + jnp.exp(-x))


def _softplus(x):
    return jnp.maximum(x, 0.0) + jnp.log(1.0 + jnp.exp(-jnp.abs(x)))


def _in_proj_kernel(has_vres, c, spb, *refs):
    it = iter(refs)
    take = lambda cnt: [next(it) for _ in range(cnt)]
    x_ref, g_ref, w_ref, cos_ref, sin_ref, mum_ref, mul_ref, w0_ref, w2_ref, a0_ref, a2_ref, g2_ref, kk_ref, ka_ref = take(14)
    if has_vres:
        we_ref, muv_ref, vf_ref, v0_ref, v2_ref = take(5)
    r_ref, lw_ref, k_ref, v_ref, an_ref, b_ref, gate_ref = take(7)
    if not has_vres:
        vf32_ref, = take(1)
    qs_ref, ko_ref, vt_ref, bias_ref = take(4)
    km_ref, cm_ref, cl_ref = take(3)
    if has_vres:
        cv_ref, = take(1)

    i = pl.program_id(0)
    nb = km_ref.shape[0]
    blk = MOBA_BLOCK
    width = km_ref.shape[1]
    nh = width // ATTN_HEAD
    n_lora = mul_ref.shape[1]
    n_mix = 3 * c + n_lora
    pw = 2 * RWKV_HEAD

    @pl.when(i == 0)
    def _():
        for ref in [km_ref, cm_ref, cl_ref] + ([cv_ref] if has_vres else []):
            ref[...] = jnp.zeros_like(ref)

    blk_id = lax.broadcasted_iota(jnp.int32, (nb, blk), 0)
    km_row = lax.broadcasted_iota(jnp.int32, (nb, ATTN_HEAD), 0)
    first_row = lax.broadcasted_iota(jnp.int32, (blk, 1), 0) == 0
    even = lax.broadcasted_iota(jnp.int32, (blk, pw), 1) < RWKV_HEAD
    qk_scale = (ATTN_HEAD ** -0.5) * LOG2E

    def shifted(cur, carry_ref, lo, hi, mu_ref, start):
        last = jnp.where(start, 0.0, carry_ref[:, lo:hi])
        prev = jnp.where(first_row, last, pltpu.roll(cur, 1, axis=0))
        carry_ref[:, lo:hi] = cur[blk - 1:blk, :]
        return cur + (prev - cur) * mu_ref[:, lo:hi]

    for j in range(x_ref.shape[0] // blk):
        rows = pl.ds(j * blk, blk)
        x = x_ref[rows, :]
        ms = jnp.mean(x * x, axis=-1, keepdims=True)
        h = (x * lax.rsqrt(ms + NORM_EPS) * g_ref[...]).astype(BF16)
        proj = lambda c0, c1: jnp.dot(h, w_ref[:, c0:c1], preferred_element_type=F32)
        bid = (i % spb) * (x_ref.shape[0] // blk) + j
        start = bid == 0

        z_l = proj(3 * c, n_mix)
        if has_vres:
            z_e = jnp.dot(h, we_ref[...], preferred_element_type=F32)
        z_r = proj(0, c)

        zl = shifted(z_l, cl_ref, 0, n_lora, mul_ref, start)
        o = 0
        wd = zl[:, o:o + DECAY_LORA]
        o += DECAY_LORA
        ad = zl[:, o:o + ICLR_LORA]
        o += ICLR_LORA
        gd = zl[:, o:o + GATE_LORA]
        w_log = -_softplus(-(w0_ref[...] + _mm(jnp.tanh(wd), w2_ref[...], passes=3))) - 0.5
        lw_ref[rows, :] = -jnp.exp(w_log)
        a = _sigmoid(a0_ref[...] + _mm(ad, a2_ref[...]))
        gate_ref[rows, :] = _mm(_sigmoid(gd), g2_ref[...]).astype(gate_ref.dtype)
        if has_vres:
            vd = shifted(z_e, cv_ref, 0, z_e.shape[1], muv_ref, start)
            v_mix = _sigmoid(v0_ref[...] + _mm(vd, v2_ref[...]))
        z_k, z_v = proj(c, 2 * c), proj(2 * c, 3 * c)

        r_ref[rows, :] = shifted(z_r, cm_ref, 0, c, mum_ref, start).astype(r_ref.dtype)
        k = shifted(z_k, cm_ref, c, 2 * c, mum_ref, start)
        v = shifted(z_v, cm_ref, 2 * c, 3 * c, mum_ref, start)
        if has_vres:
            v = v + (vf_ref[rows, :] - v) * v_mix
        else:
            vf32_ref[rows, :] = v
        v_ref[rows, :] = v.astype(v_ref.dtype)
        kk = k * kk_ref[...]
        unit = []
        for p in range(c // pw):
            xk = kk[:, p * pw:(p + 1) * pw]
            sq = xk * xk
            ss = jnp.where(even, jnp.sum(jnp.where(even, sq, 0.0), axis=-1, keepdims=True),
                           jnp.sum(jnp.where(even, 0.0, sq), axis=-1, keepdims=True))
            unit.append(xk / jnp.maximum(jnp.sqrt(ss), 1e-12))
        kk = jnp.concatenate(unit, axis=1)
        k_ref[rows, :] = (k * (1.0 + (a - 1.0) * ka_ref[...])).astype(k_ref.dtype)
        an_ref[rows, :] = (-kk).astype(an_ref.dtype)
        b_ref[rows, :] = (kk * a).astype(b_ref.dtype)
        q_all, k_all = (proj(n_mix + t * width, n_mix + (t + 1) * width) for t in range(2))

        cos = cos_ref[rows, :]
        sin = sin_ref[rows, :]
        past = blk_id < bid
        gates = []
        for hd in range(nh):
            sl = slice(hd * ATTN_HEAD, (hd + 1) * ATTN_HEAD)
            q, ka = q_all[:, sl], k_all[:, sl]
            qr = (q * cos + pltpu.roll(q, ATTN_HEAD // 2, axis=1) * sin).T
            kr = ka * cos + pltpu.roll(ka, ATTN_HEAD // 2, axis=1) * sin
            qs_ref[0, j, hd] = (qr * qk_scale).astype(BF16)
            ko_ref[0, j, hd] = kr.astype(BF16)
            gates.append(jnp.where(past, _mm(km_ref[:, sl], qr, passes=3), -jnp.inf))
            km_new = jnp.sum(kr, axis=0, keepdims=True) * (1.0 / blk)
            km_ref[:, sl] = jnp.where(km_row == bid, km_new, km_ref[:, sl])
        v_all = proj(n_mix + 2 * width, n_mix + 3 * width)

        for hd, gate in enumerate(gates):
            rank = jnp.zeros((nb, blk), jnp.int32)
            for m in range(nb):
                gm = gate[m:m + 1, :]
                rank += ((gm > gate) | ((gm == gate) & (m < blk_id))).astype(jnp.int32)
            bias_ref[0, j, hd] = jnp.where(past & (rank < MOBA_TOPK), 0.0, -jnp.inf)
            vt_ref[0, j, hd] = v_all[:, hd * ATTN_HEAD:(hd + 1) * ATTN_HEAD].T.astype(BF16)


def _in_proj(x3, gain, w, c, mu, w0, w2, a0, a2, g2, k_k, k_a, cos2, sin2, vres=None, tm=256):
    bsz, s, d = x3.shape
    n = w.shape[1]
    n_lora = w2.shape[0] + a2.shape[0] + g2.shape[0]
    n_mix = 3 * c + n_lora
    width = (n - n_mix) // 3
    nh = width // ATTN_HEAD
    nb = s // MOBA_BLOCK
    bpt = tm // MOBA_BLOCK
    spb = s // tm
    assert s % tm == 0 and tm % MOBA_BLOCK == 0 and c % (2 * RWKV_HEAD) == 0 and n_lora % LANES == 0
    assert (n - n_mix) % (3 * ATTN_HEAD) == 0
    has_vres = vres is not None
    once = pl.Buffered(1)
    row = lambda a: a.reshape(1, -1)
    full = lambda a: pl.BlockSpec(a.shape, lambda i: (0,) * a.ndim)
    tile = lambda wd_: pl.BlockSpec((tm, wd_), lambda i: (i, 0))
    small = [row(mu[:3 * c]), row(mu[3 * c:]), row(w0), w2, row(a0), a2, g2, row(k_k), row(k_a)]
    ins = [x3.reshape(bsz * s, d), gain.reshape(1, d), w, cos2, sin2] + small
    in_specs = [tile(d), pl.BlockSpec((1, d), lambda i: (0, 0)),
                pl.BlockSpec((d, n), lambda i: (0, 0), pipeline_mode=once),
                pl.BlockSpec((tm, ATTN_HEAD), lambda i: (i % spb, 0)),
                pl.BlockSpec((tm, ATTN_HEAD), lambda i: (i % spb, 0))] + [full(a) for a in small]
    scratch = [pltpu.VMEM((nb, width), F32), pltpu.VMEM((1, 3 * c), F32), pltpu.VMEM((1, n_lora), F32)]
    if has_vres:
        w_v, mu_v, v_first, v0, v2 = vres
        extra = [w_v, row(mu_v), v_first, row(v0), v2]
        ins += extra
        in_specs += [pl.BlockSpec(w_v.shape, lambda i: (0, 0), pipeline_mode=once), full(extra[1]), tile(c),
                     full(extra[3]), full(extra[4])]
        scratch.append(pltpu.VMEM((1, w_v.shape[1]), F32))
    mix_dtypes = [BF16, F32, BF16, BF16, BF16, BF16, BF16] + ([] if has_vres else [F32])
    t_spec = pl.BlockSpec((1, bpt, nh, ATTN_HEAD, MOBA_BLOCK), lambda i: (i // spb, i % spb, 0, 0, 0))
    n_spec = pl.BlockSpec((1, bpt, nh, MOBA_BLOCK, ATTN_HEAD), lambda i: (i // spb, i % spb, 0, 0, 0))
    out_specs = [tile(c)] * len(mix_dtypes) + [
        t_spec, n_spec, t_spec, pl.BlockSpec((1, bpt, nh, nb, MOBA_BLOCK), lambda i: (i // spb, i % spb, 0, 0, 0))]
    out_shape = [jax.ShapeDtypeStruct((bsz * s, c), dt) for dt in mix_dtypes] + [
        jax.ShapeDtypeStruct((bsz, nb, nh, ATTN_HEAD, MOBA_BLOCK), BF16),
        jax.ShapeDtypeStruct((bsz, nb, nh, MOBA_BLOCK, ATTN_HEAD), BF16),
        jax.ShapeDtypeStruct((bsz, nb, nh, ATTN_HEAD, MOBA_BLOCK), BF16),
        jax.ShapeDtypeStruct((bsz, nb, nh, nb, MOBA_BLOCK), F32)]
    out = pl.pallas_call(
        functools.partial(_in_proj_kernel, has_vres, c, spb),
        grid=(bsz * spb,),
        in_specs=in_specs,
        out_specs=out_specs,
        out_shape=out_shape,
        scratch_shapes=scratch,
        compiler_params=_cparams("arbitrary"),
        name="in_proj",
    )(*ins)
    mix = [o.reshape(bsz, s, c) for o in out[:7]]
    return mix, (None if has_vres else out[7]), out[-4:]


def _dots(a_list, b_list, dims=_NN):
    return [lax.dot_general(a.astype(BF16), b.astype(BF16), dims, preferred_element_type=F32)
            for a, b in zip(a_list, b_list)]


def _pair_diag(x, even):
    return jnp.concatenate([jnp.where(even, x, 0.0), jnp.where(even, 0.0, x)], axis=0)


def _unit_lower_inverse(a_list, row_w, col_w, even):
    n = row_w.shape[0]
    lower = row_w > col_w
    base = lower & ((row_w >> 1) == (col_w >> 1))
    t = [jnp.where(row_w == col_w, 1.0, jnp.where(base, a, 0.0)) for a in a_list]
    sh = 1
    while (2 << sh) <= n:
        sub = lower & ((row_w >> (sh + 1)) == (col_w >> (sh + 1))) & ((row_w >> sh) != (col_w >> sh))
        off = [_pair_diag(jnp.where(sub, a, 0.0), even) for a in a_list]
        upd = _dots(_dots(t, off), [_pair_diag(x, even) for x in t])
        t = [x + u for x, u in zip(t, upd)]
        sh += 1
    return t


def _mm_exact_rhs_left(l_bf16, a):
    d = lambda p: lax.dot_general(l_bf16, p, _NN, preferred_element_type=F32)
    hi, mid, lo = _split3(a)
    return d(hi) + (d(mid) + d(lo))


def _wkv_kernel(n_cast, *refs):
    r_ref, lw_ref, k_ref, v_ref, an_ref, b_ref, g_ref, rk_ref, lg_ref, lb_ref = refs[:10]
    cast_in, y_ref = refs[10:10 + n_cast], refs[10 + n_cast]
    cast_out, state_ref = refs[11 + n_cast:11 + 2 * n_cast], refs[11 + 2 * n_cast]
    for src, dst in zip(cast_in, cast_out):
        dst[...] = src[...].astype(dst.dtype)

    nb, ln = r_ref.shape[0], r_ref.shape[1]
    n = RWKV_HEAD
    pw = 2 * n
    npair = r_ref.shape[2] // pw
    ent = [(bi, slice(p * pw, (p + 1) * pw)) for bi in range(nb) for p in range(npair)]

    @pl.when(pl.program_id(1) == 0)
    def _():
        state_ref[...] = jnp.zeros_like(state_ref)

    rows = lax.broadcasted_iota(jnp.int32, (ln, ln), 0)
    cols = lax.broadcasted_iota(jnp.int32, (ln, ln), 1)
    tril = jnp.where(rows >= cols, 1.0, 0.0).astype(BF16)
    r, k, v, w_end, r_t, a_t, b_t, k_t, b_h, k_h = ([] for _ in range(10))
    for bi in range(nb):
        lw = lw_ref[bi]
        cw = _mm_exact_rhs_left(tril, lw)
        cw_end = cw[ln - 1:ln, :]
        e_neg = jnp.exp(-cw)
        e_end = jnp.exp(cw_end - cw)
        bb = b_ref[bi].astype(F32)
        r.append(r_ref[bi].astype(F32))
        k.append(k_ref[bi].astype(F32))
        v.append(v_ref[bi].astype(F32))
        w_end.append(jnp.exp(cw_end))
        r_t.append(r[bi] * jnp.exp(cw))
        a_t.append(an_ref[bi].astype(F32) * jnp.exp(cw - lw))
        b_t.append(bb * e_neg)
        k_t.append(k[bi] * e_neg)
        b_h.append(bb * e_end)
        k_h.append(k[bi] * e_end)

    lane = lax.broadcasted_iota(jnp.int32, (ln, pw), 1)
    row_w = lax.broadcasted_iota(jnp.int32, (ln, pw), 0)
    even = lane < n
    even2 = lax.broadcasted_iota(jnp.int32, (2 * ln, pw), 1) < n
    col_w = lane & (n - 1)
    strict_w = row_w > col_w
    incl_w = row_w >= col_w
    zeros_w = jnp.zeros((ln, pw), F32)
    diag = lambda x: _pair_diag(x, even)
    swap = lambda x: jnp.concatenate([x[x.shape[0] // 2:], x[:x.shape[0] // 2]], axis=0)

    ar_p = [jnp.concatenate([a_t[bi][:, ps], r_t[bi][:, ps]], axis=0) for bi, ps in ent]
    bk_p = [jnp.concatenate([b_t[bi][:, ps], k_t[bi][:, ps]], axis=0).astype(BF16) for bi, ps in ent]
    kb_p = [jnp.concatenate([k_t[bi][:, ps], b_t[bi][:, ps]], axis=0).astype(BF16) for bi, ps in ent]
    am_e = _dots([jnp.where(even2, x, 0.0) for x in ar_p], bk_p, _NT)
    am_o = _dots([jnp.where(even2, 0.0, x) for x in ar_p], kb_p, _NT)
    a_ab = [jnp.where(strict_w, jnp.where(even, e[:ln], o[:ln]), 0.0) for e, o in zip(am_e, am_o)]
    a_ak = [jnp.where(strict_w, jnp.where(even, o[:ln], e[:ln]), 0.0) for e, o in zip(am_e, am_o)]
    a_rb = [jnp.where(incl_w, jnp.where(even, e[ln:], o[ln:]), 0.0) for e, o in zip(am_e, am_o)]
    a_rk = [jnp.where(incl_w, jnp.where(even, o[ln:], e[ln:]), 0.0) for e, o in zip(am_e, am_o)]
    v_p = [v[bi][:, ps] for bi, ps in ent]
    v_d = [diag(x) for x in v_p]
    akv = _dots(a_ak, [swap(x) for x in v_d])
    t = _unit_lower_inverse(a_ab, row_w, col_w, even)
    rhs = [jnp.concatenate([diag(a_t[bi][:, ps]), diag(x)], axis=1) for (bi, ps), x in zip(ent, akv)]
    pq = _dots(t, rhs)
    ry = _dots([jnp.concatenate([x, y], axis=1) for x, y in zip(a_rb, a_rk)],
               [jnp.concatenate([jnp.concatenate([diag(x[:, :pw]), diag(x[:, pw:])], axis=1),
                                 jnp.concatenate([jnp.zeros((2 * ln, pw), F32), swap(u)], axis=1)], axis=0)
                for x, u in zip(pq, v_d)])
    pqv_p = [jnp.concatenate([x, jnp.concatenate([zeros_w, u], axis=1)], axis=0)
             for x, u in zip(pq, v_p)]
    bkh_p = [jnp.concatenate([b_h[bi][:, ps], k_h[bi][:, ps]], axis=0) for bi, ps in ent]
    mn_p = _dots(bkh_p, pqv_p, _TN)
    sq_r = lax.broadcasted_iota(jnp.int32, (pw, pw), 0)
    sq_c = lax.broadcasted_iota(jnp.int32, (pw, pw), 1)
    same_head = (sq_r < n) == (sq_c < n)
    lhs_p = [jnp.concatenate([r_t[bi][:, ps] + x[:, :pw],
                              jnp.where(sq_r == sq_c, jnp.broadcast_to(w_end[bi][:, ps], (pw, pw)),
                                        jnp.where(same_head, m[:, :pw], 0.0))], axis=0)
             for (bi, ps), x, m in zip(ent, ry, mn_p)]
    st = [state_ref[e] for e in range(len(ent))]
    upd = _dots(lhs_p, st)
    for e in range(len(ent)):
        state_ref[e] = upd[e][ln:, :] + jnp.where(same_head, mn_p[e][:, pw:], 0.0)
    y0_p = [x[:, pw:] for x in ry]

    hsum = lambda x: jnp.where(even, jnp.sum(jnp.where(even, x, 0.0), axis=-1, keepdims=True),
                               jnp.sum(jnp.where(even, 0.0, x), axis=-1, keepdims=True))
    inv_n = 1.0 / n
    yn = []
    for u, y0 in zip(upd, y0_p):
        y = u[:ln, :] + y0
        d = y - hsum(y) * inv_n
        yn.append(d * lax.rsqrt(hsum(d * d) * inv_n + LNX_EPS))
    for bi in range(nb):
        rk = r[bi] * k[bi] * rk_ref[...]
        mine = range(bi * npair, (bi + 1) * npair)
        bonus = jnp.concatenate([hsum(rk[:, ent[e][1]]) * v_p[e] for e in mine], axis=1)
        out = jnp.concatenate([yn[e] for e in mine], axis=1) * lg_ref[...] + lb_ref[...] + bonus
        y_ref[bi] = (out * g_ref[bi].astype(F32)).astype(y_ref.dtype)


def _wkv(r, lw, k, v, an, b, g, r_k, lnx_g, lnx_b, casts=(), heads_per_step=16):
    bsz, s, c = r.shape
    wb = heads_per_step * RWKV_HEAD
    nc = s // WKV_CHUNK
    assert s % WKV_CHUNK == 0 and c % wb == 0 and heads_per_step % 2 == 0 and WKV_CHUNK == RWKV_HEAD
    assert not casts or c == wb
    spec = pl.BlockSpec((bsz, WKV_CHUNK, wb), lambda hi, ci: (0, ci, hi))
    pspec = pl.BlockSpec((1, wb), lambda hi, ci: (0, hi))
    row = lambda a: a.reshape(1, -1)
    cast_ins, cast_in_specs, cast_out_specs, cast_out_shape = [], [], [], []
    for w, layer in casts:
        nl, rows, cols = w.shape
        assert rows % (BF16_SUBLANES * nc) == 0
        slab = rows // nc
        cast_ins.append(w.reshape(nl, nc, slab, cols))
        cast_in_specs.append(pl.BlockSpec((None, None, slab, cols), lambda hi, ci, layer=layer: (layer, ci, 0, 0)))
        cast_out_specs.append(pl.BlockSpec((None, slab, cols), lambda hi, ci: (ci, 0, 0)))
        cast_out_shape.append(jax.ShapeDtypeStruct((nc, slab, cols), BF16))
    out = pl.pallas_call(
        functools.partial(_wkv_kernel, len(casts)),
        grid=(c // wb, nc),
        in_specs=[spec] * 7 + [pspec] * 3 + cast_in_specs,
        out_specs=[spec] + cast_out_specs,
        out_shape=[jax.ShapeDtypeStruct((bsz, s, c), BF16)] + cast_out_shape,
        scratch_shapes=[pltpu.VMEM((bsz * heads_per_step // 2, 2 * RWKV_HEAD, 2 * RWKV_HEAD), F32)],
        compiler_params=_cparams("parallel", "arbitrary"),
        name="wkv",
    )(r, lw, k, v, an, b, g, row(r_k), row(lnx_g), row(lnx_b), *cast_ins)
    return out[0], [o.reshape(w.shape[1], w.shape[2]) for o, (w, _) in zip(out[1:], casts)]


def _moba_kernel(qs_ref, k_ref, vt_ref, bias_ref, o_ref):
    blk = MOBA_BLOCK
    dh = ATTN_HEAD
    hs = range(qs_ref.shape[2])
    qb = pl.program_id(2)
    neg = -jnp.inf
    qs = [qs_ref[0, 0, h] for h in hs]
    ki = lax.broadcasted_iota(jnp.int32, (blk, blk), 0)
    qi = lax.broadcasted_iota(jnp.int32, (blk, blk), 1)
    causal = ki <= qi

    def pipelined(work, stage):
        scores = lambda kb, h: jnp.dot(k_ref[0, kb, h], qs[h], preferred_element_type=F32)
        ahead = [scores(*w) for w in work[:MOBA_LOOKAHEAD]]
        for i, (kb, h) in enumerate(work):
            if i + MOBA_LOOKAHEAD < len(work):
                ahead.append(scores(*work[i + MOBA_LOOKAHEAD]))
            stage(kb, h, ahead[i])

    m_run, l_run, acc = [None] * len(hs), [None] * len(hs), [None] * len(hs)

    def own_block(kb, h, s):
        s = jnp.where(causal, s, neg)
        m_run[h] = jnp.max(s, axis=0, keepdims=True)
        p = jnp.exp2(s - m_run[h])
        l_run[h] = jnp.sum(p, axis=0, keepdims=True)
        acc[h] = jnp.dot(vt_ref[0, kb, h], p.astype(BF16), preferred_element_type=F32)

    pipelined([(qb, h) for h in hs], own_block)

    def past_blocks(kbs, carry):
        m_c, l_c, acc_c = (list(c) for c in carry)

        def stage(kb, h, s):
            b = bias_ref[0, 0, h, pl.ds(kb, 1), :]
            m_new = jnp.where(b == 0.0, jnp.maximum(m_c[h], jnp.max(s, axis=0, keepdims=True)), m_c[h])
            alpha = jnp.exp2(m_c[h] - m_new)
            p = jnp.exp2(s - (m_new - b))
            pv = jnp.dot(vt_ref[0, kb, h], p.astype(BF16), preferred_element_type=F32)
            m_c[h] = m_new
            l_c[h] = alpha * l_c[h] + jnp.sum(p, axis=0, keepdims=True)
            acc_c[h] = alpha * acc_c[h] + pv

        pipelined([(kb, h) for kb in kbs for h in hs], stage)
        return tuple(m_c), tuple(l_c), tuple(acc_c)

    u = MOBA_UNROLL
    carry = lax.fori_loop(0, qb // u, lambda j, c: past_blocks([j * u + i for i in range(u)], c),
                          (tuple(m_run), tuple(l_run), tuple(acc)))
    _, l_fin, acc = lax.fori_loop((qb // u) * u, qb, lambda kb, c: past_blocks([kb], c), carry)
    for h in hs:
        o_ref[0, :, h * dh:(h + 1) * dh] = (acc[h] / l_fin[h]).T.astype(o_ref.dtype)


def _moba(qs, k, vt, bias, heads_per_step=8):
    bsz, nb, nh, dh, blk = qs.shape
    hp = heads_per_step
    assert nh % hp == 0
    return pl.pallas_call(
        _moba_kernel,
        grid=(bsz, nh // hp, nb),
        in_specs=[pl.BlockSpec((1, 1, hp, dh, blk), lambda b, h, i: (b, i, h, 0, 0)),
                  pl.BlockSpec((1, nb, hp, blk, dh), lambda b, h, i: (b, 0, h, 0, 0)),
                  pl.BlockSpec((1, nb, hp, dh, blk), lambda b, h, i: (b, 0, h, 0, 0)),
                  pl.BlockSpec((1, 1, hp, nb, blk), lambda b, h, i: (b, i, h, 0, 0))],
        out_specs=pl.BlockSpec((1, blk, hp * dh), lambda b, h, i: (b, i, h)),
        out_shape=jax.ShapeDtypeStruct((bsz, nb * blk, nh * dh), BF16),
        compiler_params=_cparams("parallel", "parallel", "arbitrary"),
        name="moba",
    )(qs, k, vt, bias)


def _out_proj_kernel(yr_ref, ya_ref, wr_ref, wa_ref, x_ref, g_ref, o_ref):
    for r in range(0, x_ref.shape[0], ROW_CHUNK):
        rows = pl.ds(r, ROW_CHUNK)
        y = jnp.dot(yr_ref[rows, :], wr_ref[...], preferred_element_type=F32)
        y += jnp.dot(ya_ref[rows, :], wa_ref[...], preferred_element_type=F32)
        ms = jnp.mean(y * y, axis=-1, keepdims=True)
        o_ref[rows, :] = x_ref[rows, :] + y * lax.rsqrt(ms + NORM_EPS) * g_ref[...]


def _out_proj(y_r, y_a, w, x2, gain, tm=1024):
    m, d = x2.shape
    cw = y_r.shape[1]
    assert y_a.shape[1] == cw and w.shape[0] == 2 * cw and m % tm == 0
    once = pl.Buffered(1)
    return pl.pallas_call(
        _out_proj_kernel,
        grid=(m // tm,),
        in_specs=[pl.BlockSpec((tm, cw), lambda i: (i, 0)), pl.BlockSpec((tm, cw), lambda i: (i, 0)),
                  pl.BlockSpec((cw, d), lambda i: (0, 0), pipeline_mode=once),
                  pl.BlockSpec((cw, d), lambda i: (1, 0), pipeline_mode=once),
                  pl.BlockSpec((tm, d), lambda i: (i, 0)), pl.BlockSpec((1, d), lambda i: (0, 0))],
        out_specs=pl.BlockSpec((tm, d), lambda i: (i, 0)),
        out_shape=jax.ShapeDtypeStruct((m, d), F32),
        compiler_params=_cparams("parallel"),
        name="out_proj",
    )(y_r, y_a, w, w, x2, gain.reshape(1, d))


def _mlp_kernel(x_ref, gpre_ref, wu_ref, wd_ref, gpost_ref, o_ref, h_ref):
    f = pl.program_id(1)
    last = pl.num_programs(1) - 1
    chunks = [pl.ds(r, ROW_CHUNK) for r in range(0, x_ref.shape[0], ROW_CHUNK)]

    def part(h):
        u = jnp.maximum(jnp.dot(h, wu_ref[...], preferred_element_type=F32), 0.0)
        return jnp.dot((u * u).astype(BF16), wd_ref[...], preferred_element_type=F32)

    @pl.when(f == 0)
    def _():
        for rows in chunks:
            x = x_ref[rows, :]
            ms = jnp.mean(x * x, axis=-1, keepdims=True)
            h = (x * lax.rsqrt(ms + NORM_EPS) * gpre_ref[...]).astype(BF16)
            h_ref[rows, :] = h
            o_ref[rows, :] = part(h)

    @pl.when((f > 0) & (f < last))
    def _():
        for rows in chunks:
            o_ref[rows, :] += part(h_ref[rows, :])

    @pl.when(f == last)
    def _():
        for rows in chunks:
            mlp = o_ref[rows, :] + part(h_ref[rows, :])
            ms = jnp.mean(mlp * mlp, axis=-1, keepdims=True)
            o_ref[rows, :] = x_ref[rows, :] + mlp * lax.rsqrt(ms + NORM_EPS) * gpost_ref[...]


def _mlp(x2, g_pre, w_up, w_down, g_post, tm=1024, tf=1024):
    m, d = x2.shape
    dff = w_up.shape[1]
    assert m % tm == 0 and dff % tf == 0 and dff // tf >= 2 and tm % ROW_CHUNK == 0
    return pl.pallas_call(
        _mlp_kernel,
        grid=(m // tm, dff // tf),
        in_specs=[pl.BlockSpec((tm, d), lambda i, f: (i, 0)),
                  pl.BlockSpec((1, d), lambda i, f: (0, 0)),
                  pl.BlockSpec((d, tf), lambda i, f: (0, f)),
                  pl.BlockSpec((tf, d), lambda i, f: (f, 0)),
                  pl.BlockSpec((1, d), lambda i, f: (0, 0))],
        out_specs=pl.BlockSpec((tm, d), lambda i, f: (i, 0)),
        out_shape=jax.ShapeDtypeStruct((m, d), F32),
        scratch_shapes=[pltpu.VMEM((tm, d), BF16)],
        compiler_params=_cparams("parallel", "arbitrary"),
        name="mlp",
    )(x2, g_pre.reshape(1, d), w_up, w_down, g_post.reshape(1, d))


def _rope_tables(s):
    half = ATTN_HEAD // 2
    inv_freq = ROPE_THETA ** (-jnp.arange(half, dtype=F32) / half)
    ang = jnp.arange(s).astype(F32)[:, None] * inv_freq[None, :]
    cos, sin = jnp.cos(ang), jnp.sin(ang)
    return jnp.concatenate([cos, cos], axis=-1), jnp.concatenate([-sin, sin], axis=-1)


def kernel(x, norm_mix_pre, norm_mix_post, norm_mlp_pre, norm_mlp_post, w_in, w_in_vres, shift_mu, shift_mu_vres, decay_w0, decay_w2, iclr_a0, iclr_a2, vres_v0, vres_v2, gate_g2, k_k, k_a, r_k, lnx_gain, lnx_bias, w_out, w_up, w_down):
    bsz, s, d = x.shape
    depth = w_in.shape[0]
    c = decay_w0.shape[1]
    n_lora = DECAY_LORA + ICLR_LORA + GATE_LORA
    n_shift = 3 * c + n_lora
    ca = (w_in.shape[2] - n_shift) // 3
    cos2, sin2 = _rope_tables(s)
    w_in16 = w_in[0].astype(BF16)
    pad_v = VRES_PAD - VRES_LORA
    x2 = x.reshape(bsz * s, d)
    v_first = None
    for i in range(depth):
        if i == 0:
            vres = None
        else:
            vres = (jnp.pad(w_in_vres[i - 1], ((0, 0), (0, pad_v))).astype(BF16),
                    jnp.pad(shift_mu_vres[i - 1], (0, pad_v)), v_first, vres_v0[i - 1],
                    jnp.pad(vres_v2[i - 1], ((0, pad_v), (0, 0))))
        mix, v_layer, attn = _in_proj(x2.reshape(bsz, s, d), norm_mix_pre[i], w_in16, c, shift_mu[i], decay_w0[i],
                                      decay_w2[i], iclr_a0[i], iclr_a2[i], gate_g2[i], k_k[i], k_a[i], cos2, sin2,
                                      vres)
        if i == 0:
            v_first = v_layer

        casts = [(w_out, i), (w_up, i), (w_down, i)] + ([(w_in, i + 1)] if i + 1 < depth else [])
        y_r, w16 = _wkv(*mix, r_k[i].reshape(-1), lnx_gain[i], lnx_bias[i], casts)
        w_out16, w_up16, w_down16 = w16[:3]

        y_a = _moba(*attn)

        x2 = _out_proj(y_r.reshape(bsz * s, c), y_a.reshape(bsz * s, ca), w_out16, x2, norm_mix_post[i])
        x2 = _mlp(x2, norm_mlp_pre[i], w_up16, w_down16, norm_mlp_post[i])
        if i + 1 < depth:
            w_in16 = w16[3]
    return x2.reshape(bsz, s, d)
```

```python
import functools

import jax
import jax.numpy as jnp
from jax import lax
from jax.experimental import pallas as pl
from jax.experimental.pallas import tpu as pltpu

F32 = jnp.float32
BF16 = jnp.bfloat16

RWKV_HEAD = 64
DECAY_LORA = 64
ICLR_LORA = 64
VRES_LORA = 32
GATE_LORA = 128
ATTN_HEAD = 128
MOBA_BLOCK = 256
MOBA_TOPK = 3
ROPE_THETA = 10000.0
NORM_EPS = 1e-6
LNX_EPS = 64e-5
LOG2E = 1.4426950408889634

LANES = 128
BF16_SUBLANES = 16
VRES_PAD = LANES
WKV_CHUNK = 64
MOBA_LOOKAHEAD = 6
MOBA_UNROLL = 4
ROW_CHUNK = 256
VMEM_LIMIT = 56 * 1024 * 1024


def _cparams(*sem):
    return pltpu.CompilerParams(dimension_semantics=sem, vmem_limit_bytes=VMEM_LIMIT)


_NN = (((1,), (0,)), ((), ()))
_NT = (((1,), (1,)), ((), ()))
_TN = (((0,), (0,)), ((), ()))


def _split2(x):
    hi = x.astype(BF16)
    lo = (x - hi.astype(F32)).astype(BF16)
    return hi, lo


def _split3(x):
    hi = x.astype(BF16)
    r1 = x - hi.astype(F32)
    mid = r1.astype(BF16)
    lo = (r1 - mid.astype(F32)).astype(BF16)
    return hi, mid, lo


def _mm(a, b, dims=_NN, passes=1):
    d = lambda p, q: lax.dot_general(p, q, dims, preferred_element_type=F32)
    if passes == 1:
        return d(a.astype(BF16), b.astype(BF16))
    ah, al = _split2(a)
    bh, bl = _split2(b)
    return d(ah, bh) + (d(ah, bl) + d(al, bh))


def _sigmoid(x):
    return 1.0 / (1.0 + jnp.exp(-x))


def _softplus(x):
    return jnp.maximum(x, 0.0) + jnp.log(1.0 + jnp.exp(-jnp.abs(x)))


def _in_proj_kernel(has_vres, c, spb, *refs):
    it = iter(refs)
    take = lambda cnt: [next(it) for _ in range(cnt)]
    x_ref, g_ref, w_ref, cos_ref, sin_ref, mum_ref, mul_ref, w0_ref, w2_ref, a0_ref, a2_ref, g2_ref, kk_ref, ka_ref = take(14)
    if has_vres:
        we_ref, muv_ref, vf_ref, v0_ref, v2_ref = take(5)
    r_ref, lw_ref, k_ref, v_ref, an_ref, b_ref, gate_ref = take(7)
    if not has_vres:
        vf32_ref, = take(1)
    qs_ref, ko_ref, vt_ref, bias_ref = take(4)
    km_ref, cm_ref, cl_ref = take(3)
    if has_vres:
        cv_ref, = take(1)

    i = pl.program_id(0)
    nb = km_ref.shape[0]
    blk = MOBA_BLOCK
    width = km_ref.shape[1]
    nh = width // ATTN_HEAD
    n_lora = mul_ref.shape[1]
    n_mix = 3 * c + n_lora
    pw = 2 * RWKV_HEAD

    @pl.when(i == 0)
    def _():
        for ref in [km_ref, cm_ref, cl_ref] + ([cv_ref] if has_vres else []):
            ref[...] = jnp.zeros_like(ref)

    blk_id = lax.broadcasted_iota(jnp.int32, (nb, blk), 0)
    km_row = lax.broadcasted_iota(jnp.int32, (nb, ATTN_HEAD), 0)
    first_row = lax.broadcasted_iota(jnp.int32, (blk, 1), 0) == 0
    even = lax.broadcasted_iota(jnp.int32, (blk, pw), 1) < RWKV_HEAD
    qk_scale = (ATTN_HEAD ** -0.5) * LOG2E

    def shifted(cur, carry_ref, lo, hi, mu_ref, start):
        last = jnp.where(start, 0.0, carry_ref[:, lo:hi])
        prev = jnp.where(first_row, last, pltpu.roll(cur, 1, axis=0))
        carry_ref[:, lo:hi] = cur[blk - 1:blk, :]
        return cur + (prev - cur) * mu_ref[:, lo:hi]

    for j in range(x_ref.shape[0] // blk):
        rows = pl.ds(j * blk, blk)
        x = x_ref[rows, :]
        ms = jnp.mean(x * x, axis=-1, keepdims=True)
        h = (x * lax.rsqrt(ms + NORM_EPS) * g_ref[...]).astype(BF16)
        proj = lambda c0, c1: jnp.dot(h, w_ref[:, c0:c1], preferred_element_type=F32)
        bid = (i % spb) * (x_ref.shape[0] // blk) + j
        start = bid == 0

        z_l = proj(3 * c, n_mix)
        if has_vres:
            z_e = jnp.dot(h, we_ref[...], preferred_element_type=F32)
        z_r = proj(0, c)

        zl = shifted(z_l, cl_ref, 0, n_lora, mul_ref, start)
        o = 0
        wd = zl[:, o:o + DECAY_LORA]
        o += DECAY_LORA
        ad = zl[:, o:o + ICLR_LORA]
        o += ICLR_LORA
        gd = zl[:, o:o + GATE_LORA]
        w_log = -_softplus(-(w0_ref[...] + _mm(jnp.tanh(wd), w2_ref[...], passes=3))) - 0.5
        lw_ref[rows, :] = -jnp.exp(w_log)
        a = _sigmoid(a0_ref[...] + _mm(ad, a2_ref[...]))
        gate_ref[rows, :] = _mm(_sigmoid(gd), g2_ref[...]).astype(gate_ref.dtype)
        if has_vres:
            vd = shifted(z_e, cv_ref, 0, z_e.shape[1], muv_ref, start)
            v_mix = _sigmoid(v0_ref[...] + _mm(vd, v2_ref[...]))
        z_k, z_v = proj(c, 2 * c), proj(2 * c, 3 * c)

        r_ref[rows, :] = shifted(z_r, cm_ref, 0, c, mum_ref, start).astype(r_ref.dtype)
        k = shifted(z_k, cm_ref, c, 2 * c, mum_ref, start)
        v = shifted(z_v, cm_ref, 2 * c, 3 * c, mum_ref, start)
        if has_vres:
            v = v + (vf_ref[rows, :] - v) * v_mix
        else:
            vf32_ref[rows, :] = v
        v_ref[rows, :] = v.astype(v_ref.dtype)
        kk = k * kk_ref[...]
        unit = []
        for p in range(c // pw):
            xk = kk[:, p * pw:(p + 1) * pw]
            sq = xk * xk
            ss = jnp.where(even, jnp.sum(jnp.where(even, sq, 0.0), axis=-1, keepdims=True),
                           jnp.sum(jnp.where(even, 0.0, sq), axis=-1, keepdims=True))
            unit.append(xk / jnp.maximum(jnp.sqrt(ss), 1e-12))
        kk = jnp.concatenate(unit, axis=1)
        k_ref[rows, :] = (k * (1.0 + (a - 1.0) * ka_ref[...])).astype(k_ref.dtype)
        an_ref[rows, :] = (-kk).astype(an_ref.dtype)
        b_ref[rows, :] = (kk * a).astype(b_ref.dtype)
        q_all, k_all = (proj(n_mix + t * width, n_mix + (t + 1) * width) for t in range(2))

        cos = cos_ref[rows, :]
        sin = sin_ref[rows, :]
        past = blk_id < bid
        gates = []
        for hd in range(nh):
            sl = slice(hd * ATTN_HEAD, (hd + 1) * ATTN_HEAD)
            q, ka = q_all[:, sl], k_all[:, sl]
            qr = (q * cos + pltpu.roll(q, ATTN_HEAD // 2, axis=1) * sin).T
            kr = ka * cos + pltpu.roll(ka, ATTN_HEAD // 2, axis=1) * sin
            qs_ref[0, j, hd] = (qr * qk_scale).astype(BF16)
            ko_ref[0, j, hd] = kr.astype(BF16)
            gates.append(jnp.where(past, _mm(km_ref[:, sl], qr, passes=3), -jnp.inf))
            km_new = jnp.sum(kr, axis=0, keepdims=True) * (1.0 / blk)
            km_ref[:, sl] = jnp.where(km_row == bid, km_new, km_ref[:, sl])
        v_all = proj(n_mix + 2 * width, n_mix + 3 * width)

        for hd, gate in enumerate(gates):
            rank = jnp.zeros((nb, blk), jnp.int32)
            for m in range(nb):
                gm = gate[m:m + 1, :]
                rank += ((gm > gate) | ((gm == gate) & (m < blk_id))).astype(jnp.int32)
            bias_ref[0, j, hd] = jnp.where(past & (rank < MOBA_TOPK), 0.0, -jnp.inf)
            vt_ref[0, j, hd] = v_all[:, hd * ATTN_HEAD:(hd + 1) * ATTN_HEAD].T.astype(BF16)


def _in_proj(x3, gain, w, c, mu, w0, w2, a0, a2, g2, k_k, k_a, cos2, sin2, vres=None, tm=256):
    bsz, s, d = x3.shape
    n = w.shape[1]
    n_lora = w2.shape[0] + a2.shape[0] + g2.shape[0]
    n_mix = 3 * c + n_lora
    width = (n - n_mix) // 3
    nh = width // ATTN_HEAD
    nb = s // MOBA_BLOCK
    bpt = tm // MOBA_BLOCK
    spb = s // tm
    assert s % tm == 0 and tm % MOBA_BLOCK == 0 and c % (2 * RWKV_HEAD) == 0 and n_lora % LANES == 0
    assert (n - n_mix) % (3 * ATTN_HEAD) == 0
    has_vres = vres is not None
    once = pl.Buffered(1)
    row = lambda a: a.reshape(1, -1)
    full = lambda a: pl.BlockSpec(a.shape, lambda i: (0,) * a.ndim)
    tile = lambda wd_: pl.BlockSpec((tm, wd_), lambda i: (i, 0))
    small = [row(mu[:3 * c]), row(mu[3 * c:]), row(w0), w2, row(a0), a2, g2, row(k_k), row(k_a)]
    ins = [x3.reshape(bsz * s, d), gain.reshape(1, d), w, cos2, sin2] + small
    in_specs = [tile(d), pl.BlockSpec((1, d), lambda i: (0, 0)),
                pl.BlockSpec((d, n), lambda i: (0, 0), pipeline_mode=once),
                pl.BlockSpec((tm, ATTN_HEAD), lambda i: (i % spb, 0)),
                pl.BlockSpec((tm, ATTN_HEAD), lambda i: (i % spb, 0))] + [full(a) for a in small]
    scratch = [pltpu.VMEM((nb, width), F32), pltpu.VMEM((1, 3 * c), F32), pltpu.VMEM((1, n_lora), F32)]
    if has_vres:
        w_v, mu_v, v_first, v0, v2 = vres
        extra = [w_v, row(mu_v), v_first, row(v0), v2]
        ins += extra
        in_specs += [pl.BlockSpec(w_v.shape, lambda i: (0, 0), pipeline_mode=once), full(extra[1]), tile(c),
                     full(extra[3]), full(extra[4])]
        scratch.append(pltpu.VMEM((1, w_v.shape[1]), F32))
    mix_dtypes = [BF16, F32, BF16, BF16, BF16, BF16, BF16] + ([] if has_vres else [F32])
    t_spec = pl.BlockSpec((1, bpt, nh, ATTN_HEAD, MOBA_BLOCK), lambda i: (i // spb, i % spb, 0, 0, 0))
    n_spec = pl.BlockSpec((1, bpt, nh, MOBA_BLOCK, ATTN_HEAD), lambda i: (i // spb, i % spb, 0, 0, 0))
    out_specs = [tile(c)] * len(mix_dtypes) + [
        t_spec, n_spec, t_spec, pl.BlockSpec((1, bpt, nh, nb, MOBA_BLOCK), lambda i: (i // spb, i % spb, 0, 0, 0))]
    out_shape = [jax.ShapeDtypeStruct((bsz * s, c), dt) for dt in mix_dtypes] + [
        jax.ShapeDtypeStruct((bsz, nb, nh, ATTN_HEAD, MOBA_BLOCK), BF16),
        jax.ShapeDtypeStruct((bsz, nb, nh, MOBA_BLOCK, ATTN_HEAD), BF16),
        jax.ShapeDtypeStruct((bsz, nb, nh, ATTN_HEAD, MOBA_BLOCK), BF16),
        jax.ShapeDtypeStruct((bsz, nb, nh, nb, MOBA_BLOCK), F32)]
    out = pl.pallas_call(
        functools.partial(_in_proj_kernel, has_vres, c, spb),
        grid=(bsz * spb,),
        in_specs=in_specs,
        out_specs=out_specs,
        out_shape=out_shape,
        scratch_shapes=scratch,
        compiler_params=_cparams("arbitrary"),
        name="in_proj",
    )(*ins)
    mix = [o.reshape(bsz, s, c) for o in out[:7]]
    return mix, (None if has_vres else out[7]), out[-4:]


def _dots(a_list, b_list, dims=_NN):
    return [lax.dot_general(a.astype(BF16), b.astype(BF16), dims, preferred_element_type=F32)
            for a, b in zip(a_list, b_list)]


def _pair_diag(x, even):
    return jnp.concatenate([jnp.where(even, x, 0.0), jnp.where(even, 0.0, x)], axis=0)


def _unit_lower_inverse(a_list, row_w, col_w, even):
    n = row_w.shape[0]
    lower = row_w > col_w
    base = lower & ((row_w >> 1) == (col_w >> 1))
    t = [jnp.where(row_w == col_w, 1.0, jnp.where(base, a, 0.0)) for a in a_list]
    sh = 1
    while (2 << sh) <= n:
        sub = lower & ((row_w >> (sh + 1)) == (col_w >> (sh + 1))) & ((row_w >> sh) != (col_w >> sh))
        off = [_pair_diag(jnp.where(sub, a, 0.0), even) for a in a_list]
        upd = _dots(_dots(t, off), [_pair_diag(x, even) for x in t])
        t = [x + u for x, u in zip(t, upd)]
        sh += 1
    return t


def _mm_exact_rhs_left(l_bf16, a):
    d = lambda p: lax.dot_general(l_bf16, p, _NN, preferred_element_type=F32)
    hi, mid, lo = _split3(a)
    return d(hi) + (d(mid) + d(lo))


def _wkv_kernel(n_cast, *refs):
    r_ref, lw_ref, k_ref, v_ref, an_ref, b_ref, g_ref, rk_ref, lg_ref, lb_ref = refs[:10]
    cast_in, y_ref = refs[10:10 + n_cast], refs[10 + n_cast]
    cast_out, state_ref = refs[11 + n_cast:11 + 2 * n_cast], refs[11 + 2 * n_cast]
    for src, dst in zip(cast_in, cast_out):
        dst[...] = src[...].astype(dst.dtype)

    nb, ln = r_ref.shape[0], r_ref.shape[1]
    n = RWKV_HEAD
    pw = 2 * n
    npair = r_ref.shape[2] // pw
    ent = [(bi, slice(p * pw, (p + 1) * pw)) for bi in range(nb) for p in range(npair)]

    @pl.when(pl.program_id(1) == 0)
    def _():
        state_ref[...] = jnp.zeros_like(state_ref)

    rows = lax.broadcasted_iota(jnp.int32, (ln, ln), 0)
    cols = lax.broadcasted_iota(jnp.int32, (ln, ln), 1)
    tril = jnp.where(rows >= cols, 1.0, 0.0).astype(BF16)
    r, k, v, w_end, r_t, a_t, b_t, k_t, b_h, k_h = ([] for _ in range(10))
    for bi in range(nb):
        lw = lw_ref[bi]
        cw = _mm_exact_rhs_left(tril, lw)
        cw_end = cw[ln - 1:ln, :]
        e_neg = jnp.exp(-cw)
        e_end = jnp.exp(cw_end - cw)
        bb = b_ref[bi].astype(F32)
        r.append(r_ref[bi].astype(F32))
        k.append(k_ref[bi].astype(F32))
        v.append(v_ref[bi].astype(F32))
        w_end.append(jnp.exp(cw_end))
        r_t.append(r[bi] * jnp.exp(cw))
        a_t.append(an_ref[bi].astype(F32) * jnp.exp(cw - lw))
        b_t.append(bb * e_neg)
        k_t.append(k[bi] * e_neg)
        b_h.append(bb * e_end)
        k_h.append(k[bi] * e_end)

    lane = lax.broadcasted_iota(jnp.int32, (ln, pw), 1)
    row_w = lax.broadcasted_iota(jnp.int32, (ln, pw), 0)
    even = lane < n
    even2 = lax.broadcasted_iota(jnp.int32, (2 * ln, pw), 1) < n
    col_w = lane & (n - 1)
    strict_w = row_w > col_w
    incl_w = row_w >= col_w
    zeros_w = jnp.zeros((ln, pw), F32)
    diag = lambda x: _pair_diag(x, even)
    swap = lambda x: jnp.concatenate([x[x.shape[0] // 2:], x[:x.shape[0] // 2]], axis=0)

    ar_p = [jnp.concatenate([a_t[bi][:, ps], r_t[bi][:, ps]], axis=0) for bi, ps in ent]
    bk_p = [jnp.concatenate([b_t[bi][:, ps], k_t[bi][:, ps]], axis=0).astype(BF16) for bi, ps in ent]
    kb_p = [jnp.concatenate([k_t[bi][:, ps], b_t[bi][:, ps]], axis=0).astype(BF16) for bi, ps in ent]
    am_e = _dots([jnp.where(even2, x, 0.0) for x in ar_p], bk_p, _NT)
    am_o = _dots([jnp.where(even2, 0.0, x) for x in ar_p], kb_p, _NT)
    a_ab = [jnp.where(strict_w, jnp.where(even, e[:ln], o[:ln]), 0.0) for e, o in zip(am_e, am_o)]
    a_ak = [jnp.where(strict_w, jnp.where(even, o[:ln], e[:ln]), 0.0) for e, o in zip(am_e, am_o)]
    a_rb = [jnp.where(incl_w, jnp.where(even, e[ln:], o[ln:]), 0.0) for e, o in zip(am_e, am_o)]
    a_rk = [jnp.where(incl_w, jnp.where(even, o[ln:], e[ln:]), 0.0) for e, o in zip(am_e, am_o)]
    v_p = [v[bi][:, ps] for bi, ps in ent]
    v_d = [diag(x) for x in v_p]
    akv = _dots(a_ak, [swap(x) for x in v_d])
    t = _unit_lower_inverse(a_ab, row_w, col_w, even)
    rhs = [jnp.concatenate([diag(a_t[bi][:, ps]), diag(x)], axis=1) for (bi, ps), x in zip(ent, akv)]
    pq = _dots(t, rhs)
    ry = _dots([jnp.concatenate([x, y], axis=1) for x, y in zip(a_rb, a_rk)],
               [jnp.concatenate([jnp.concatenate([diag(x[:, :pw]), diag(x[:, pw:])], axis=1),
                                 jnp.concatenate([jnp.zeros((2 * ln, pw), F32), swap(u)], axis=1)], axis=0)
                for x, u in zip(pq, v_d)])
    pqv_p = [jnp.concatenate([x, jnp.concatenate([zeros_w, u], axis=1)], axis=0)
             for x, u in zip(pq, v_p)]
    bkh_p = [jnp.concatenate([b_h[bi][:, ps], k_h[bi][:, ps]], axis=0) for bi, ps in ent]
    mn_p = _dots(bkh_p, pqv_p, _TN)
    sq_r = lax.broadcasted_iota(jnp.int32, (pw, pw), 0)
    sq_c = lax.broadcasted_iota(jnp.int32, (pw, pw), 1)
    same_head = (sq_r < n) == (sq_c < n)
    lhs_p = [jnp.concatenate([r_t[bi][:, ps] + x[:, :pw],
                              jnp.where(sq_r == sq_c, jnp.broadcast_to(w_end[bi][:, ps], (pw, pw)),
                                        jnp.where(same_head, m[:, :pw], 0.0))], axis=0)
             for (bi, ps), x, m in zip(ent, ry, mn_p)]
    st = [state_ref[e] for e in range(len(ent))]
    upd = _dots(lhs_p, st)
    for e in range(len(ent)):
        state_ref[e] = upd[e][ln:, :] + jnp.where(same_head, mn_p[e][:, pw:], 0.0)
    y0_p = [x[:, pw:] for x in ry]

    hsum = lambda x: jnp.where(even, jnp.sum(jnp.where(even, x, 0.0), axis=-1, keepdims=True),
                               jnp.sum(jnp.where(even, 0.0, x), axis=-1, keepdims=True))
    inv_n = 1.0 / n
    yn = []
    for u, y0 in zip(upd, y0_p):
        y = u[:ln, :] + y0
        d = y - hsum(y) * inv_n
        yn.append(d * lax.rsqrt(hsum(d * d) * inv_n + LNX_EPS))
    for bi in range(nb):
        rk = r[bi] * k[bi] * rk_ref[...]
        mine = range(bi * npair, (bi + 1) * npair)
        bonus = jnp.concatenate([hsum(rk[:, ent[e][1]]) * v_p[e] for e in mine], axis=1)
        out = jnp.concatenate([yn[e] for e in mine], axis=1) * lg_ref[...] + lb_ref[...] + bonus
        y_ref[bi] = (out * g_ref[bi].astype(F32)).astype(y_ref.dtype)


def _wkv(r, lw, k, v, an, b, g, r_k, lnx_g, lnx_b, casts=(), heads_per_step=16):
    bsz, s, c = r.shape
    wb = heads_per_step * RWKV_HEAD
    nc = s // WKV_CHUNK
    assert s % WKV_CHUNK == 0 and c % wb == 0 and heads_per_step % 2 == 0 and WKV_CHUNK == RWKV_HEAD
    assert not casts or c == wb
    spec = pl.BlockSpec((bsz, WKV_CHUNK, wb), lambda hi, ci: (0, ci, hi))
    pspec = pl.BlockSpec((1, wb), lambda hi, ci: (0, hi))
    row = lambda a: a.reshape(1, -1)
    cast_ins, cast_in_specs, cast_out_specs, cast_out_shape = [], [], [], []
    for w, layer in casts:
        nl, rows, cols = w.shape
        assert rows % (BF16_SUBLANES * nc) == 0
        slab = rows // nc
        cast_ins.append(w.reshape(nl, nc, slab, cols))
        cast_in_specs.append(pl.BlockSpec((None, None, slab, cols), lambda hi, ci, layer=layer: (layer, ci, 0, 0)))
        cast_out_specs.append(pl.BlockSpec((None, slab, cols), lambda hi, ci: (ci, 0, 0)))
        cast_out_shape.append(jax.ShapeDtypeStruct((nc, slab, cols), BF16))
    out = pl.pallas_call(
        functools.partial(_wkv_kernel, len(casts)),
        grid=(c // wb, nc),
        in_specs=[spec] * 7 + [pspec] * 3 + cast_in_specs,
        out_specs=[spec] + cast_out_specs,
        out_shape=[jax.ShapeDtypeStruct((bsz, s, c), BF16)] + cast_out_shape,
        scratch_shapes=[pltpu.VMEM((bsz * heads_per_step // 2, 2 * RWKV_HEAD, 2 * RWKV_HEAD), F32)],
        compiler_params=_cparams("parallel", "arbitrary"),
        name="wkv",
    )(r, lw, k, v, an, b, g, row(r_k), row(lnx_g), row(lnx_b), *cast_ins)
    return out[0], [o.reshape(w.shape[1], w.shape[2]) for o, (w, _) in zip(out[1:], casts)]


def _moba_kernel(qs_ref, k_ref, vt_ref, bias_ref, o_ref):
    blk = MOBA_BLOCK
    dh = ATTN_HEAD
    hs = range(qs_ref.shape[2])
    qb = pl.program_id(2)
    neg = -jnp.inf
    qs = [qs_ref[0, 0, h] for h in hs]
    ki = lax.broadcasted_iota(jnp.int32, (blk, blk), 0)
    qi = lax.broadcasted_iota(jnp.int32, (blk, blk), 1)
    causal = ki <= qi

    def pipelined(work, stage):
        scores = lambda kb, h: jnp.dot(k_ref[0, kb, h], qs[h], preferred_element_type=F32)
        ahead = [scores(*w) for w in work[:MOBA_LOOKAHEAD]]
        for i, (kb, h) in enumerate(work):
            if i + MOBA_LOOKAHEAD < len(work):
                ahead.append(scores(*work[i + MOBA_LOOKAHEAD]))
            stage(kb, h, ahead[i])

    m_run, l_run, acc = [None] * len(hs), [None] * len(hs), [None] * len(hs)

    def own_block(kb, h, s):
        s = jnp.where(causal, s, neg)
        m_run[h] = jnp.max(s, axis=0, keepdims=True)
        p = jnp.exp2(s - m_run[h])
        l_run[h] = jnp.sum(p, axis=0, keepdims=True)
        acc[h] = jnp.dot(vt_ref[0, kb, h], p.astype(BF16), preferred_element_type=F32)

    pipelined([(qb, h) for h in hs], own_block)

    def past_blocks(kbs, carry):
        m_c, l_c, acc_c = (list(c) for c in carry)

        def stage(kb, h, s):
            b = bias_ref[0, 0, h, pl.ds(kb, 1), :]
            m_new = jnp.where(b == 0.0, jnp.maximum(m_c[h], jnp.max(s, axis=0, keepdims=True)), m_c[h])
            alpha = jnp.exp2(m_c[h] - m_new)
            p = jnp.exp2(s - (m_new - b))
            pv = jnp.dot(vt_ref[0, kb, h], p.astype(BF16), preferred_element_type=F32)
            m_c[h] = m_new
            l_c[h] = alpha * l_c[h] + jnp.sum(p, axis=0, keepdims=True)
            acc_c[h] = alpha * acc_c[h] + pv

        pipelined([(kb, h) for kb in kbs for h in hs], stage)
        return tuple(m_c), tuple(l_c), tuple(acc_c)

    u = MOBA_UNROLL
    carry = lax.fori_loop(0, qb // u, lambda j, c: past_blocks([j * u + i for i in range(u)], c),
                          (tuple(m_run), tuple(l_run), tuple(acc)))
    _, l_fin, acc = lax.fori_loop((qb // u) * u, qb, lambda kb, c: past_blocks([kb], c), carry)
    for h in hs:
        o_ref[0, :, h * dh:(h + 1) * dh] = (acc[h] / l_fin[h]).T.astype(o_ref.dtype)


def _moba(qs, k, vt, bias, heads_per_step=8):
    bsz, nb, nh, dh, blk = qs.shape
    hp = heads_per_step
    assert nh % hp == 0
    return pl.pallas_call(
        _moba_kernel,
        grid=(bsz, nh // hp, nb),
        in_specs=[pl.BlockSpec((1, 1, hp, dh, blk), lambda b, h, i: (b, i, h, 0, 0)),
                  pl.BlockSpec((1, nb, hp, blk, dh), lambda b, h, i: (b, 0, h, 0, 0)),
                  pl.BlockSpec((1, nb, hp, dh, blk), lambda b, h, i: (b, 0, h, 0, 0)),
                  pl.BlockSpec((1, 1, hp, nb, blk), lambda b, h, i: (b, i, h, 0, 0))],
        out_specs=pl.BlockSpec((1, blk, hp * dh), lambda b, h, i: (b, i, h)),
        out_shape=jax.ShapeDtypeStruct((bsz, nb * blk, nh * dh), BF16),
        compiler_params=_cparams("parallel", "parallel", "arbitrary"),
        name="moba",
    )(qs, k, vt, bias)


def _out_proj_kernel(yr_ref, ya_ref, wr_ref, wa_ref, x_ref, g_ref, o_ref):
    for r in range(0, x_ref.shape[0], ROW_CHUNK):
        rows = pl.ds(r, ROW_CHUNK)
        y = jnp.dot(yr_ref[rows, :], wr_ref[...], preferred_element_type=F32)
        y += jnp.dot(ya_ref[rows, :], wa_ref[...], preferred_element_type=F32)
        ms = jnp.mean(y * y, axis=-1, keepdims=True)
        o_ref[rows, :] = x_ref[rows, :] + y * lax.rsqrt(ms + NORM_EPS) * g_ref[...]


def _out_proj(y_r, y_a, w, x2, gain, tm=512):
    m, d = x2.shape
    cw = y_r.shape[1]
    assert y_a.shape[1] == cw and w.shape[0] == 2 * cw and m % tm == 0
    return pl.pallas_call(
        _out_proj_kernel,
        grid=(m // tm,),
        in_specs=[pl.BlockSpec((tm, cw), lambda i: (i, 0)), pl.BlockSpec((tm, cw), lambda i: (i, 0)),
                  pl.BlockSpec((cw, d), lambda i: (0, 0)), pl.BlockSpec((cw, d), lambda i: (1, 0)),
                  pl.BlockSpec((tm, d), lambda i: (i, 0)), pl.BlockSpec((1, d), lambda i: (0, 0))],
        out_specs=pl.BlockSpec((tm, d), lambda i: (i, 0)),
        out_shape=jax.ShapeDtypeStruct((m, d), F32),
        compiler_params=_cparams("parallel"),
        name="out_proj",
    )(y_r, y_a, w, w, x2, gain.reshape(1, d))


def _mlp_kernel(x_ref, gpre_ref, wu_ref, wd_ref, gpost_ref, o_ref, h_ref):
    f = pl.program_id(1)
    last = pl.num_programs(1) - 1
    chunks = [pl.ds(r, ROW_CHUNK) for r in range(0, x_ref.shape[0], ROW_CHUNK)]

    def part(h):
        u = jnp.maximum(jnp.dot(h, wu_ref[...], preferred_element_type=F32), 0.0)
        return jnp.dot((u * u).astype(BF16), wd_ref[...], preferred_element_type=F32)

    @pl.when(f == 0)
    def _():
        for rows in chunks:
            x = x_ref[rows, :]
            ms = jnp.mean(x * x, axis=-1, keepdims=True)
            h = (x * lax.rsqrt(ms + NORM_EPS) * gpre_ref[...]).astype(BF16)
            h_ref[rows, :] = h
            o_ref[rows, :] = part(h)

    @pl.when((f > 0) & (f < last))
    def _():
        for rows in chunks:
            o_ref[rows, :] += part(h_ref[rows, :])

    @pl.when(f == last)
    def _():
        for rows in chunks:
            mlp = o_ref[rows, :] + part(h_ref[rows, :])
            ms = jnp.mean(mlp * mlp, axis=-1, keepdims=True)
            o_ref[rows, :] = x_ref[rows, :] + mlp * lax.rsqrt(ms + NORM_EPS) * gpost_ref[...]


def _mlp(x2, g_pre, w_up, w_down, g_post, tm=1024, tf=1024):
    m, d = x2.shape
    dff = w_up.shape[1]
    assert m % tm == 0 and dff % tf == 0 and dff // tf >= 2 and tm % ROW_CHUNK == 0
    return pl.pallas_call(
        _mlp_kernel,
        grid=(m // tm, dff // tf),
        in_specs=[pl.BlockSpec((tm, d), lambda i, f: (i, 0)),
                  pl.BlockSpec((1, d), lambda i, f: (0, 0)),
                  pl.BlockSpec((d, tf), lambda i, f: (0, f)),
                  pl.BlockSpec((tf, d), lambda i, f: (f, 0)),
                  pl.BlockSpec((1, d), lambda i, f: (0, 0))],
        out_specs=pl.BlockSpec((tm, d), lambda i, f: (i, 0)),
        out_shape=jax.ShapeDtypeStruct((m, d), F32),
        scratch_shapes=[pltpu.VMEM((tm, d), BF16)],
        compiler_params=_cparams("parallel", "arbitrary"),
        name="mlp",
    )(x2, g_pre.reshape(1, d), w_up, w_down, g_post.reshape(1, d))


def _rope_tables(s):
    half = ATTN_HEAD // 2
    inv_freq = ROPE_THETA ** (-jnp.arange(half, dtype=F32) / half)
    ang = jnp.arange(s).astype(F32)[:, None] * inv_freq[None, :]
    cos, sin = jnp.cos(ang), jnp.sin(ang)
    return jnp.concatenate([cos, cos], axis=-1), jnp.concatenate([-sin, sin], axis=-1)


def kernel(x, norm_mix_pre, norm_mix_post, norm_mlp_pre, norm_mlp_post, w_in, w_in_vres, shift_mu, shift_mu_vres, decay_w0, decay_w2, iclr_a0, iclr_a2, vres_v0, vres_v2, gate_g2, k_k, k_a, r_k, lnx_gain, lnx_bias, w_out, w_up, w_down):
    bsz, s, d = x.shape
    depth = w_in.shape[0]
    c = decay_w0.shape[1]
    n_lora = DECAY_LORA + ICLR_LORA + GATE_LORA
    n_shift = 3 * c + n_lora
    ca = (w_in.shape[2] - n_shift) // 3
    cos2, sin2 = _rope_tables(s)
    w_in16 = w_in[0].astype(BF16)
    pad_v = VRES_PAD - VRES_LORA
    x2 = x.reshape(bsz * s, d)
    v_first = None
    for i in range(depth):
        if i == 0:
            vres = None
        else:
            vres = (jnp.pad(w_in_vres[i - 1], ((0, 0), (0, pad_v))).astype(BF16),
                    jnp.pad(shift_mu_vres[i - 1], (0, pad_v)), v_first, vres_v0[i - 1],
                    jnp.pad(vres_v2[i - 1], ((0, pad_v), (0, 0))))
        mix, v_layer, attn = _in_proj(x2.reshape(bsz, s, d), norm_mix_pre[i], w_in16, c, shift_mu[i], decay_w0[i],
                                      decay_w2[i], iclr_a0[i], iclr_a2[i], gate_g2[i], k_k[i], k_a[i], cos2, sin2,
                                      vres)
        if i == 0:
            v_first = v_layer

        casts = [(w_out, i), (w_up, i), (w_down, i)] + ([(w_in, i + 1)] if i + 1 < depth else [])
        y_r, w16 = _wkv(*mix, r_k[i].reshape(-1), lnx_gain[i], lnx_bias[i], casts)
        w_out16, w_up16, w_down16 = w16[:3]

        y_a = _moba(*attn)

        x2 = _out_proj(y_r.reshape(bsz * s, c), y_a.reshape(bsz * s, ca), w_out16, x2, norm_mix_post[i])
        x2 = _mlp(x2, norm_mlp_pre[i], w_up16, w_down16, norm_mlp_post[i])
        if i + 1 < depth:
            w_in16 = w16[3]
    return x2.reshape(bsz, s, d)
```

```python
import functools

import jax
import jax.numpy as jnp
from jax import lax
from jax.experimental import pallas as pl
from jax.experimental.pallas import tpu as pltpu

F32 = jnp.float32
BF16 = jnp.bfloat16

RWKV_HEAD = 64
DECAY_LORA = 64
ICLR_LORA = 64
VRES_LORA = 32
GATE_LORA = 128
ATTN_HEAD = 128
MOBA_BLOCK = 256
MOBA_TOPK = 3
ROPE_THETA = 10000.0
NORM_EPS = 1e-6
LNX_EPS = 64e-5
LOG2E = 1.4426950408889634

LANES = 128
BF16_SUBLANES = 16
VRES_PAD = LANES
WKV_CHUNK = 64
MOBA_LOOKAHEAD = 6
MOBA_UNROLL = 4
ROW_CHUNK = 256
VMEM_LIMIT = 56 * 1024 * 1024


def _cparams(*sem):
    return pltpu.CompilerParams(dimension_semantics=sem, vmem_limit_bytes=VMEM_LIMIT)


_NN = (((1,), (0,)), ((), ()))
_NT = (((1,), (1,)), ((), ()))
_TN = (((0,), (0,)), ((), ()))


def _split2(x):
    hi = x.astype(BF16)
    lo = (x - hi.astype(F32)).astype(BF16)
    return hi, lo


def _split3(x):
    hi = x.astype(BF16)
    r1 = x - hi.astype(F32)
    mid = r1.astype(BF16)
    lo = (r1 - mid.astype(F32)).astype(BF16)
    return hi, mid, lo


def _mm(a, b, dims=_NN, passes=1):
    d = lambda p, q: lax.dot_general(p, q, dims, preferred_element_type=F32)
    if passes == 1:
        return d(a.astype(BF16), b.astype(BF16))
    ah, al = _split2(a)
    bh, bl = _split2(b)
    return d(ah, bh) + (d(ah, bl) + d(al, bh))


def _sigmoid(x):
    return 1.0 / (1.0 + jnp.exp(-x))


def _softplus(x):
    return jnp.maximum(x, 0.0) + jnp.log(1.0 + jnp.exp(-jnp.abs(x)))


def _in_proj_kernel(has_vres, c, spb, *refs):
    it = iter(refs)
    take = lambda cnt: [next(it) for _ in range(cnt)]
    x_ref, g_ref, w_ref, cos_ref, sin_ref, mum_ref, mul_ref, w0_ref, w2_ref, a0_ref, a2_ref, g2_ref, kk_ref, ka_ref = take(14)
    if has_vres:
        we_ref, muv_ref, vf_ref, v0_ref, v2_ref = take(5)
    r_ref, lw_ref, k_ref, v_ref, an_ref, b_ref, gate_ref = take(7)
    if not has_vres:
        vf32_ref, = take(1)
    qs_ref, ko_ref, vt_ref, bias_ref = take(4)
    km_ref, cm_ref, cl_ref = take(3)
    if has_vres:
        cv_ref, = take(1)

    i = pl.program_id(0)
    nb = km_ref.shape[0]
    blk = MOBA_BLOCK
    width = km_ref.shape[1]
    nh = width // ATTN_HEAD
    n_lora = mul_ref.shape[1]
    n_mix = 3 * c + n_lora
    pw = 2 * RWKV_HEAD

    @pl.when(i == 0)
    def _():
        for ref in [km_ref, cm_ref, cl_ref] + ([cv_ref] if has_vres else []):
            ref[...] = jnp.zeros_like(ref)

    blk_id = lax.broadcasted_iota(jnp.int32, (nb, blk), 0)
    km_row = lax.broadcasted_iota(jnp.int32, (nb, ATTN_HEAD), 0)
    first_row = lax.broadcasted_iota(jnp.int32, (blk, 1), 0) == 0
    even = lax.broadcasted_iota(jnp.int32, (blk, pw), 1) < RWKV_HEAD
    qk_scale = (ATTN_HEAD ** -0.5) * LOG2E

    def shifted(cur, carry_ref, lo, hi, mu_ref, start):
        last = jnp.where(start, 0.0, carry_ref[:, lo:hi])
        prev = jnp.where(first_row, last, pltpu.roll(cur, 1, axis=0))
        carry_ref[:, lo:hi] = cur[blk - 1:blk, :]
        return cur + (prev - cur) * mu_ref[:, lo:hi]

    for j in range(x_ref.shape[0] // blk):
        rows = pl.ds(j * blk, blk)
        x = x_ref[rows, :]
        ms = jnp.mean(x * x, axis=-1, keepdims=True)
        h = (x * lax.rsqrt(ms + NORM_EPS) * g_ref[...]).astype(BF16)
        proj = lambda c0, c1: jnp.dot(h, w_ref[:, c0:c1], preferred_element_type=F32)
        bid = (i % spb) * (x_ref.shape[0] // blk) + j
        start = bid == 0

        z_l = proj(3 * c, n_mix)
        if has_vres:
            z_e = jnp.dot(h, we_ref[...], preferred_element_type=F32)
        z_r = proj(0, c)

        zl = shifted(z_l, cl_ref, 0, n_lora, mul_ref, start)
        o = 0
        wd = zl[:, o:o + DECAY_LORA]
        o += DECAY_LORA
        ad = zl[:, o:o + ICLR_LORA]
        o += ICLR_LORA
        gd = zl[:, o:o + GATE_LORA]
        w_log = -_softplus(-(w0_ref[...] + _mm(jnp.tanh(wd), w2_ref[...], passes=3))) - 0.5
        lw_ref[rows, :] = -jnp.exp(w_log)
        a = _sigmoid(a0_ref[...] + _mm(ad, a2_ref[...]))
        gate_ref[rows, :] = _mm(_sigmoid(gd), g2_ref[...]).astype(gate_ref.dtype)
        if has_vres:
            vd = shifted(z_e, cv_ref, 0, z_e.shape[1], muv_ref, start)
            v_mix = _sigmoid(v0_ref[...] + _mm(vd, v2_ref[...]))
        z_k, z_v = proj(c, 2 * c), proj(2 * c, 3 * c)

        r_ref[rows, :] = shifted(z_r, cm_ref, 0, c, mum_ref, start).astype(r_ref.dtype)
        k = shifted(z_k, cm_ref, c, 2 * c, mum_ref, start)
        v = shifted(z_v, cm_ref, 2 * c, 3 * c, mum_ref, start)
        if has_vres:
            v = v + (vf_ref[rows, :] - v) * v_mix
        else:
            vf32_ref[rows, :] = v
        v_ref[rows, :] = v.astype(v_ref.dtype)
        kk = k * kk_ref[...]
        unit = []
        for p in range(c // pw):
            xk = kk[:, p * pw:(p + 1) * pw]
            sq = xk * xk
            ss = jnp.where(even, jnp.sum(jnp.where(even, sq, 0.0), axis=-1, keepdims=True),
                           jnp.sum(jnp.where(even, 0.0, sq), axis=-1, keepdims=True))
            unit.append(xk / jnp.maximum(jnp.sqrt(ss), 1e-12))
        kk = jnp.concatenate(unit, axis=1)
        k_ref[rows, :] = (k * (1.0 + (a - 1.0) * ka_ref[...])).astype(k_ref.dtype)
        an_ref[rows, :] = (-kk).astype(an_ref.dtype)
        b_ref[rows, :] = (kk * a).astype(b_ref.dtype)
        q_all, k_all = (proj(n_mix + t * width, n_mix + (t + 1) * width) for t in range(2))

        cos = cos_ref[rows, :]
        sin = sin_ref[rows, :]
        past = blk_id < bid
        gates = []
        for hd in range(nh):
            sl = slice(hd * ATTN_HEAD, (hd + 1) * ATTN_HEAD)
            q, ka = q_all[:, sl], k_all[:, sl]
            qr = (q * cos + pltpu.roll(q, ATTN_HEAD // 2, axis=1) * sin).T
            kr = ka * cos + pltpu.roll(ka, ATTN_HEAD // 2, axis=1) * sin
            qs_ref[0, j, hd] = (qr * qk_scale).astype(BF16)
            ko_ref[0, j, hd] = kr.astype(BF16)
            gates.append(jnp.where(past, _mm(km_ref[:, sl], qr, passes=3), -jnp.inf))
            km_new = jnp.sum(kr, axis=0, keepdims=True) * (1.0 / blk)
            km_ref[:, sl] = jnp.where(km_row == bid, km_new, km_ref[:, sl])
        v_all = proj(n_mix + 2 * width, n_mix + 3 * width)

        for hd, gate in enumerate(gates):
            rank = jnp.zeros((nb, blk), jnp.int32)
            for m in range(nb):
                gm = gate[m:m + 1, :]
                rank += ((gm > gate) | ((gm == gate) & (m < blk_id))).astype(jnp.int32)
            bias_ref[0, j, hd] = jnp.where(past & (rank < MOBA_TOPK), 0.0, -jnp.inf)
            vt_ref[0, j, hd] = v_all[:, hd * ATTN_HEAD:(hd + 1) * ATTN_HEAD].T.astype(BF16)


def _in_proj(x3, gain, w, c, mu, w0, w2, a0, a2, g2, k_k, k_a, cos2, sin2, vres=None, tm=256):
    bsz, s, d = x3.shape
    n = w.shape[1]
    n_lora = w2.shape[0] + a2.shape[0] + g2.shape[0]
    n_mix = 3 * c + n_lora
    width = (n - n_mix) // 3
    nh = width // ATTN_HEAD
    nb = s // MOBA_BLOCK
    bpt = tm // MOBA_BLOCK
    spb = s // tm
    assert s % tm == 0 and tm % MOBA_BLOCK == 0 and c % (2 * RWKV_HEAD) == 0 and n_lora % LANES == 0
    assert (n - n_mix) % (3 * ATTN_HEAD) == 0
    has_vres = vres is not None
    once = pl.Buffered(1)
    row = lambda a: a.reshape(1, -1)
    full = lambda a: pl.BlockSpec(a.shape, lambda i: (0,) * a.ndim)
    tile = lambda wd_: pl.BlockSpec((tm, wd_), lambda i: (i, 0))
    small = [row(mu[:3 * c]), row(mu[3 * c:]), row(w0), w2, row(a0), a2, g2, row(k_k), row(k_a)]
    ins = [x3.reshape(bsz * s, d), gain.reshape(1, d), w, cos2, sin2] + small
    in_specs = [tile(d), pl.BlockSpec((1, d), lambda i: (0, 0)),
                pl.BlockSpec((d, n), lambda i: (0, 0), pipeline_mode=once),
                pl.BlockSpec((tm, ATTN_HEAD), lambda i: (i % spb, 0)),
                pl.BlockSpec((tm, ATTN_HEAD), lambda i: (i % spb, 0))] + [full(a) for a in small]
    scratch = [pltpu.VMEM((nb, width), F32), pltpu.VMEM((1, 3 * c), F32), pltpu.VMEM((1, n_lora), F32)]
    if has_vres:
        w_v, mu_v, v_first, v0, v2 = vres
        extra = [w_v, row(mu_v), v_first, row(v0), v2]
        ins += extra
        in_specs += [pl.BlockSpec(w_v.shape, lambda i: (0, 0), pipeline_mode=once), full(extra[1]), tile(c),
                     full(extra[3]), full(extra[4])]
        scratch.append(pltpu.VMEM((1, w_v.shape[1]), F32))
    mix_dtypes = [BF16, F32, BF16, BF16, BF16, BF16, BF16] + ([] if has_vres else [F32])
    t_spec = pl.BlockSpec((1, bpt, nh, ATTN_HEAD, MOBA_BLOCK), lambda i: (i // spb, i % spb, 0, 0, 0))
    n_spec = pl.BlockSpec((1, bpt, nh, MOBA_BLOCK, ATTN_HEAD), lambda i: (i // spb, i % spb, 0, 0, 0))
    out_specs = [tile(c)] * len(mix_dtypes) + [
        t_spec, n_spec, t_spec, pl.BlockSpec((1, bpt, nh, nb, MOBA_BLOCK), lambda i: (i // spb, i % spb, 0, 0, 0))]
    out_shape = [jax.ShapeDtypeStruct((bsz * s, c), dt) for dt in mix_dtypes] + [
        jax.ShapeDtypeStruct((bsz, nb, nh, ATTN_HEAD, MOBA_BLOCK), BF16),
        jax.ShapeDtypeStruct((bsz, nb, nh, MOBA_BLOCK, ATTN_HEAD), BF16),
        jax.ShapeDtypeStruct((bsz, nb, nh, ATTN_HEAD, MOBA_BLOCK), BF16),
        jax.ShapeDtypeStruct((bsz, nb, nh, nb, MOBA_BLOCK), F32)]
    out = pl.pallas_call(
        functools.partial(_in_proj_kernel, has_vres, c, spb),
        grid=(bsz * spb,),
        in_specs=in_specs,
        out_specs=out_specs,
        out_shape=out_shape,
        scratch_shapes=scratch,
        compiler_params=_cparams("arbitrary"),
        name="in_proj",
    )(*ins)
    mix = [o.reshape(bsz, s, c) for o in out[:7]]
    return mix, (None if has_vres else out[7]), out[-4:]


def _dots(a_list, b_list, dims=_NN):
    return [lax.dot_general(a.astype(BF16), b.astype(BF16), dims, preferred_element_type=F32)
            for a, b in zip(a_list, b_list)]


def _pair_diag(x, even):
    return jnp.concatenate([jnp.where(even, x, 0.0), jnp.where(even, 0.0, x)], axis=0)


def _unit_lower_inverse(a_list, row_w, col_w, even):
    n = row_w.shape[0]
    lower = row_w > col_w
    base = lower & ((row_w >> 1) == (col_w >> 1))
    t = [jnp.where(row_w == col_w, 1.0, jnp.where(base, a, 0.0)) for a in a_list]
    sh = 1
    while (2 << sh) <= n:
        sub = lower & ((row_w >> (sh + 1)) == (col_w >> (sh + 1))) & ((row_w >> sh) != (col_w >> sh))
        off = [_pair_diag(jnp.where(sub, a, 0.0), even) for a in a_list]
        upd = _dots(_dots(t, off), [_pair_diag(x, even) for x in t])
        t = [x + u for x, u in zip(t, upd)]
        sh += 1
    return t


def _mm_exact_rhs_left(l_bf16, a):
    d = lambda p: lax.dot_general(l_bf16, p, _NN, preferred_element_type=F32)
    hi, mid, lo = _split3(a)
    return d(hi) + (d(mid) + d(lo))


def _wkv_kernel(n_cast, *refs):
    r_ref, lw_ref, k_ref, v_ref, an_ref, b_ref, g_ref, rk_ref, lg_ref, lb_ref = refs[:10]
    cast_in, y_ref = refs[10:10 + n_cast], refs[10 + n_cast]
    cast_out, state_ref = refs[11 + n_cast:11 + 2 * n_cast], refs[11 + 2 * n_cast]
    for src, dst in zip(cast_in, cast_out):
        dst[...] = src[...].astype(dst.dtype)

    nb, ln = r_ref.shape[0], r_ref.shape[1]
    n = RWKV_HEAD
    pw = 2 * n
    npair = r_ref.shape[2] // pw
    ent = [(bi, slice(p * pw, (p + 1) * pw)) for bi in range(nb) for p in range(npair)]

    @pl.when(pl.program_id(1) == 0)
    def _():
        state_ref[...] = jnp.zeros_like(state_ref)

    rows = lax.broadcasted_iota(jnp.int32, (ln, ln), 0)
    cols = lax.broadcasted_iota(jnp.int32, (ln, ln), 1)
    tril = jnp.where(rows >= cols, 1.0, 0.0).astype(BF16)
    r, k, v, w_end, r_t, a_t, b_t, k_t, b_h, k_h = ([] for _ in range(10))
    for bi in range(nb):
        lw = lw_ref[bi]
        cw = _mm_exact_rhs_left(tril, lw)
        cw_end = cw[ln - 1:ln, :]
        e_neg = jnp.exp(-cw)
        e_end = jnp.exp(cw_end - cw)
        bb = b_ref[bi].astype(F32)
        r.append(r_ref[bi].astype(F32))
        k.append(k_ref[bi].astype(F32))
        v.append(v_ref[bi].astype(F32))
        w_end.append(jnp.exp(cw_end))
        r_t.append(r[bi] * jnp.exp(cw))
        a_t.append(an_ref[bi].astype(F32) * jnp.exp(cw - lw))
        b_t.append(bb * e_neg)
        k_t.append(k[bi] * e_neg)
        b_h.append(bb * e_end)
        k_h.append(k[bi] * e_end)

    lane = lax.broadcasted_iota(jnp.int32, (ln, pw), 1)
    row_w = lax.broadcasted_iota(jnp.int32, (ln, pw), 0)
    even = lane < n
    even2 = lax.broadcasted_iota(jnp.int32, (2 * ln, pw), 1) < n
    col_w = lane & (n - 1)
    strict_w = row_w > col_w
    incl_w = row_w >= col_w
    zeros_w = jnp.zeros((ln, pw), F32)
    diag = lambda x: _pair_diag(x, even)
    swap = lambda x: jnp.concatenate([x[x.shape[0] // 2:], x[:x.shape[0] // 2]], axis=0)

    ar_p = [jnp.concatenate([a_t[bi][:, ps], r_t[bi][:, ps]], axis=0) for bi, ps in ent]
    bk_p = [jnp.concatenate([b_t[bi][:, ps], k_t[bi][:, ps]], axis=0).astype(BF16) for bi, ps in ent]
    kb_p = [jnp.concatenate([k_t[bi][:, ps], b_t[bi][:, ps]], axis=0).astype(BF16) for bi, ps in ent]
    am_e = _dots([jnp.where(even2, x, 0.0) for x in ar_p], bk_p, _NT)
    am_o = _dots([jnp.where(even2, 0.0, x) for x in ar_p], kb_p, _NT)
    a_ab = [jnp.where(strict_w, jnp.where(even, e[:ln], o[:ln]), 0.0) for e, o in zip(am_e, am_o)]
    a_ak = [jnp.where(strict_w, jnp.where(even, o[:ln], e[:ln]), 0.0) for e, o in zip(am_e, am_o)]
    a_rb = [jnp.where(incl_w, jnp.where(even, e[ln:], o[ln:]), 0.0) for e, o in zip(am_e, am_o)]
    a_rk = [jnp.where(incl_w, jnp.where(even, o[ln:], e[ln:]), 0.0) for e, o in zip(am_e, am_o)]
    v_p = [v[bi][:, ps] for bi, ps in ent]
    v_d = [diag(x) for x in v_p]
    akv = _dots(a_ak, [swap(x) for x in v_d])
    t = _unit_lower_inverse(a_ab, row_w, col_w, even)
    rhs = [jnp.concatenate([diag(a_t[bi][:, ps]), diag(x)], axis=1) for (bi, ps), x in zip(ent, akv)]
    pq = _dots(t, rhs)
    ry = _dots([jnp.concatenate([x, y], axis=1) for x, y in zip(a_rb, a_rk)],
               [jnp.concatenate([jnp.concatenate([diag(x[:, :pw]), diag(x[:, pw:])], axis=1),
                                 jnp.concatenate([jnp.zeros((2 * ln, pw), F32), swap(u)], axis=1)], axis=0)
                for x, u in zip(pq, v_d)])
    pqv_p = [jnp.concatenate([x, jnp.concatenate([zeros_w, u], axis=1)], axis=0)
             for x, u in zip(pq, v_p)]
    bkh_p = [jnp.concatenate([b_h[bi][:, ps], k_h[bi][:, ps]], axis=0) for bi, ps in ent]
    mn_p = _dots(bkh_p, pqv_p, _TN)
    sq_r = lax.broadcasted_iota(jnp.int32, (pw, pw), 0)
    sq_c = lax.broadcasted_iota(jnp.int32, (pw, pw), 1)
    same_head = (sq_r < n) == (sq_c < n)
    lhs_p = [jnp.concatenate([r_t[bi][:, ps] + x[:, :pw],
                              jnp.where(sq_r == sq_c, jnp.broadcast_to(w_end[bi][:, ps], (pw, pw)),
                                        jnp.where(same_head, m[:, :pw], 0.0))], axis=0)
             for (bi, ps), x, m in zip(ent, ry, mn_p)]
    st = [state_ref[e] for e in range(len(ent))]
    upd = _dots(lhs_p, st)
    for e in range(len(ent)):
        state_ref[e] = upd[e][ln:, :] + jnp.where(same_head, mn_p[e][:, pw:], 0.0)
    y0_p = [x[:, pw:] for x in ry]

    hsum = lambda x: jnp.where(even, jnp.sum(jnp.where(even, x, 0.0), axis=-1, keepdims=True),
                               jnp.sum(jnp.where(even, 0.0, x), axis=-1, keepdims=True))
    inv_n = 1.0 / n
    yn = []
    for u, y0 in zip(upd, y0_p):
        y = u[:ln, :] + y0
        d = y - hsum(y) * inv_n
        yn.append(d * lax.rsqrt(hsum(d * d) * inv_n + LNX_EPS))
    for bi in range(nb):
        rk = r[bi] * k[bi] * rk_ref[...]
        mine = range(bi * npair, (bi + 1) * npair)
        bonus = jnp.concatenate([hsum(rk[:, ent[e][1]]) * v_p[e] for e in mine], axis=1)
        out = jnp.concatenate([yn[e] for e in mine], axis=1) * lg_ref[...] + lb_ref[...] + bonus
        y_ref[bi] = (out * g_ref[bi].astype(F32)).astype(y_ref.dtype)


def _wkv(r, lw, k, v, an, b, g, r_k, lnx_g, lnx_b, casts=(), heads_per_step=16):
    bsz, s, c = r.shape
    wb = heads_per_step * RWKV_HEAD
    nc = s // WKV_CHUNK
    assert s % WKV_CHUNK == 0 and c % wb == 0 and heads_per_step % 2 == 0 and WKV_CHUNK == RWKV_HEAD
    assert not casts or c == wb
    spec = pl.BlockSpec((bsz, WKV_CHUNK, wb), lambda hi, ci: (0, ci, hi))
    pspec = pl.BlockSpec((1, wb), lambda hi, ci: (0, hi))
    row = lambda a: a.reshape(1, -1)
    cast_ins, cast_in_specs, cast_out_specs, cast_out_shape = [], [], [], []
    for w, layer in casts:
        nl, rows, cols = w.shape
        assert rows % (BF16_SUBLANES * nc) == 0
        slab = rows // nc
        cast_ins.append(w.reshape(nl, nc, slab, cols))
        cast_in_specs.append(pl.BlockSpec((None, None, slab, cols), lambda hi, ci, layer=layer: (layer, ci, 0, 0)))
        cast_out_specs.append(pl.BlockSpec((None, slab, cols), lambda hi, ci: (ci, 0, 0)))
        cast_out_shape.append(jax.ShapeDtypeStruct((nc, slab, cols), BF16))
    out = pl.pallas_call(
        functools.partial(_wkv_kernel, len(casts)),
        grid=(c // wb, nc),
        in_specs=[spec] * 7 + [pspec] * 3 + cast_in_specs,
        out_specs=[spec] + cast_out_specs,
        out_shape=[jax.ShapeDtypeStruct((bsz, s, c), BF16)] + cast_out_shape,
        scratch_shapes=[pltpu.VMEM((bsz * heads_per_step // 2, 2 * RWKV_HEAD, 2 * RWKV_HEAD), F32)],
        compiler_params=_cparams("parallel", "arbitrary"),
        name="wkv",
    )(r, lw, k, v, an, b, g, row(r_k), row(lnx_g), row(lnx_b), *cast_ins)
    return out[0], [o.reshape(w.shape[1], w.shape[2]) for o, (w, _) in zip(out[1:], casts)]


def _moba_kernel(qs_ref, k_ref, vt_ref, bias_ref, o_ref):
    blk = MOBA_BLOCK
    dh = ATTN_HEAD
    hs = range(qs_ref.shape[2])
    qb = pl.program_id(2)
    neg = -jnp.inf
    qs = [qs_ref[0, 0, h] for h in hs]
    ki = lax.broadcasted_iota(jnp.int32, (blk, blk), 0)
    qi = lax.broadcasted_iota(jnp.int32, (blk, blk), 1)
    causal = ki <= qi

    def pipelined(work, stage):
        scores = lambda kb, h: jnp.dot(k_ref[0, kb, h], qs[h], preferred_element_type=F32)
        ahead = [scores(*w) for w in work[:MOBA_LOOKAHEAD]]
        for i, (kb, h) in enumerate(work):
            if i + MOBA_LOOKAHEAD < len(work):
                ahead.append(scores(*work[i + MOBA_LOOKAHEAD]))
            stage(kb, h, ahead[i])

    m_run, l_run, acc = [None] * len(hs), [None] * len(hs), [None] * len(hs)

    def own_block(kb, h, s):
        s = jnp.where(causal, s, neg)
        m_run[h] = jnp.max(s, axis=0, keepdims=True)
        p = jnp.exp2(s - m_run[h])
        l_run[h] = jnp.sum(p, axis=0, keepdims=True)
        acc[h] = jnp.dot(vt_ref[0, kb, h], p.astype(BF16), preferred_element_type=F32)

    pipelined([(qb, h) for h in hs], own_block)

    def past_blocks(kbs, carry):
        m_c, l_c, acc_c = (list(c) for c in carry)

        def stage(kb, h, s):
            b = bias_ref[0, 0, h, pl.ds(kb, 1), :]
            m_new = jnp.where(b == 0.0, jnp.maximum(m_c[h], jnp.max(s, axis=0, keepdims=True)), m_c[h])
            alpha = jnp.exp2(m_c[h] - m_new)
            p = jnp.exp2(s - (m_new - b))
            pv = jnp.dot(vt_ref[0, kb, h], p.astype(BF16), preferred_element_type=F32)
            m_c[h] = m_new
            l_c[h] = alpha * l_c[h] + jnp.sum(p, axis=0, keepdims=True)
            acc_c[h] = alpha * acc_c[h] + pv

        pipelined([(kb, h) for kb in kbs for h in hs], stage)
        return tuple(m_c), tuple(l_c), tuple(acc_c)

    u = MOBA_UNROLL
    _, l_fin, acc = lax.fori_loop(0, (qb + u - 1) // u, lambda j, c: past_blocks([j * u + i for i in range(u)], c),
                                  (tuple(m_run), tuple(l_run), tuple(acc)))
    for h in hs:
        o_ref[0, :, h * dh:(h + 1) * dh] = (acc[h] / l_fin[h]).T.astype(o_ref.dtype)


def _moba(qs, k, vt, bias, heads_per_step=8):
    bsz, nb, nh, dh, blk = qs.shape
    hp = heads_per_step
    assert nh % hp == 0 and nb % MOBA_UNROLL == 0
    return pl.pallas_call(
        _moba_kernel,
        grid=(bsz, nh // hp, nb),
        in_specs=[pl.BlockSpec((1, 1, hp, dh, blk), lambda b, h, i: (b, i, h, 0, 0)),
                  pl.BlockSpec((1, nb, hp, blk, dh), lambda b, h, i: (b, 0, h, 0, 0)),
                  pl.BlockSpec((1, nb, hp, dh, blk), lambda b, h, i: (b, 0, h, 0, 0)),
                  pl.BlockSpec((1, 1, hp, nb, blk), lambda b, h, i: (b, i, h, 0, 0))],
        out_specs=pl.BlockSpec((1, blk, hp * dh), lambda b, h, i: (b, i, h)),
        out_shape=jax.ShapeDtypeStruct((bsz, nb * blk, nh * dh), BF16),
        compiler_params=_cparams("parallel", "parallel", "arbitrary"),
        name="moba",
    )(qs, k, vt, bias)


def _out_proj_kernel(yr_ref, ya_ref, wr_ref, wa_ref, x_ref, g_ref, o_ref):
    for r in range(0, x_ref.shape[0], ROW_CHUNK):
        rows = pl.ds(r, ROW_CHUNK)
        y = jnp.dot(yr_ref[rows, :], wr_ref[...], preferred_element_type=F32)
        y += jnp.dot(ya_ref[rows, :], wa_ref[...], preferred_element_type=F32)
        ms = jnp.mean(y * y, axis=-1, keepdims=True)
        o_ref[rows, :] = x_ref[rows, :] + y * lax.rsqrt(ms + NORM_EPS) * g_ref[...]


def _out_proj(y_r, y_a, w, x2, gain, tm=512):
    m, d = x2.shape
    cw = y_r.shape[1]
    assert y_a.shape[1] == cw and w.shape[0] == 2 * cw and m % tm == 0
    return pl.pallas_call(
        _out_proj_kernel,
        grid=(m // tm,),
        in_specs=[pl.BlockSpec((tm, cw), lambda i: (i, 0)), pl.BlockSpec((tm, cw), lambda i: (i, 0)),
                  pl.BlockSpec((cw, d), lambda i: (0, 0)), pl.BlockSpec((cw, d), lambda i: (1, 0)),
                  pl.BlockSpec((tm, d), lambda i: (i, 0)), pl.BlockSpec((1, d), lambda i: (0, 0))],
        out_specs=pl.BlockSpec((tm, d), lambda i: (i, 0)),
        out_shape=jax.ShapeDtypeStruct((m, d), F32),
        compiler_params=_cparams("parallel"),
        name="out_proj",
    )(y_r, y_a, w, w, x2, gain.reshape(1, d))


def _mlp_kernel(x_ref, gpre_ref, wu_ref, wd_ref, gpost_ref, o_ref, h_ref):
    f = pl.program_id(1)
    last = pl.num_programs(1) - 1
    chunks = [pl.ds(r, ROW_CHUNK) for r in range(0, x_ref.shape[0], ROW_CHUNK)]

    def part(h):
        u = jnp.maximum(jnp.dot(h, wu_ref[...], preferred_element_type=F32), 0.0)
        return jnp.dot((u * u).astype(BF16), wd_ref[...], preferred_element_type=F32)

    @pl.when(f == 0)
    def _():
        for rows in chunks:
            x = x_ref[rows, :]
            ms = jnp.mean(x * x, axis=-1, keepdims=True)
            h = (x * lax.rsqrt(ms + NORM_EPS) * gpre_ref[...]).astype(BF16)
            h_ref[rows, :] = h
            o_ref[rows, :] = part(h)

    @pl.when((f > 0) & (f < last))
    def _():
        for rows in chunks:
            o_ref[rows, :] += part(h_ref[rows, :])

    @pl.when(f == last)
    def _():
        for rows in chunks:
            mlp = o_ref[rows, :] + part(h_ref[rows, :])
            ms = jnp.mean(mlp * mlp, axis=-1, keepdims=True)
            o_ref[rows, :] = x_ref[rows, :] + mlp * lax.rsqrt(ms + NORM_EPS) * gpost_ref[...]


def _mlp(x2, g_pre, w_up, w_down, g_post, tm=1024, tf=1024):
    m, d = x2.shape
    dff = w_up.shape[1]
    assert m % tm == 0 and dff % tf == 0 and dff // tf >= 2 and tm % ROW_CHUNK == 0
    return pl.pallas_call(
        _mlp_kernel,
        grid=(m // tm, dff // tf),
        in_specs=[pl.BlockSpec((tm, d), lambda i, f: (i, 0)),
                  pl.BlockSpec((1, d), lambda i, f: (0, 0)),
                  pl.BlockSpec((d, tf), lambda i, f: (0, f)),
                  pl.BlockSpec((tf, d), lambda i, f: (f, 0)),
                  pl.BlockSpec((1, d), lambda i, f: (0, 0))],
        out_specs=pl.BlockSpec((tm, d), lambda i, f: (i, 0)),
        out_shape=jax.ShapeDtypeStruct((m, d), F32),
        scratch_shapes=[pltpu.VMEM((tm, d), BF16)],
        compiler_params=_cparams("parallel", "arbitrary"),
        name="mlp",
    )(x2, g_pre.reshape(1, d), w_up, w_down, g_post.reshape(1, d))


def _rope_tables(s):
    half = ATTN_HEAD // 2
    inv_freq = ROPE_THETA ** (-jnp.arange(half, dtype=F32) / half)
    ang = jnp.arange(s).astype(F32)[:, None] * inv_freq[None, :]
    cos, sin = jnp.cos(ang), jnp.sin(ang)
    return jnp.concatenate([cos, cos], axis=-1), jnp.concatenate([-sin, sin], axis=-1)


def kernel(x, norm_mix_pre, norm_mix_post, norm_mlp_pre, norm_mlp_post, w_in, w_in_vres, shift_mu, shift_mu_vres, decay_w0, decay_w2, iclr_a0, iclr_a2, vres_v0, vres_v2, gate_g2, k_k, k_a, r_k, lnx_gain, lnx_bias, w_out, w_up, w_down):
    bsz, s, d = x.shape
    depth = w_in.shape[0]
    c = decay_w0.shape[1]
    n_lora = DECAY_LORA + ICLR_LORA + GATE_LORA
    n_shift = 3 * c + n_lora
    ca = (w_in.shape[2] - n_shift) // 3
    cos2, sin2 = _rope_tables(s)
    w_in16 = w_in[0].astype(BF16)
    pad_v = VRES_PAD - VRES_LORA
    x2 = x.reshape(bsz * s, d)
    v_first = None
    for i in range(depth):
        if i == 0:
            vres = None
        else:
            vres = (jnp.pad(w_in_vres[i - 1], ((0, 0), (0, pad_v))).astype(BF16),
                    jnp.pad(shift_mu_vres[i - 1], (0, pad_v)), v_first, vres_v0[i - 1],
                    jnp.pad(vres_v2[i - 1], ((0, pad_v), (0, 0))))
        mix, v_layer, attn = _in_proj(x2.reshape(bsz, s, d), norm_mix_pre[i], w_in16, c, shift_mu[i], decay_w0[i],
                                      decay_w2[i], iclr_a0[i], iclr_a2[i], gate_g2[i], k_k[i], k_a[i], cos2, sin2,
                                      vres)
        if i == 0:
            v_first = v_layer

        casts = [(w_out, i), (w_up, i), (w_down, i)] + ([(w_in, i + 1)] if i + 1 < depth else [])
        y_r, w16 = _wkv(*mix, r_k[i].reshape(-1), lnx_gain[i], lnx_bias[i], casts)
        w_out16, w_up16, w_down16 = w16[:3]

        y_a = _moba(*attn)

        x2 = _out_proj(y_r.reshape(bsz * s, c), y_a.reshape(bsz * s, ca), w_out16, x2, norm_mix_post[i])
        x2 = _mlp(x2, norm_mlp_pre[i], w_up16, w_down16, norm_mlp_post[i])
        if i + 1 < depth:
            w_in16 = w16[3]
    return x2.reshape(bsz, s, d)
```

```python
import functools

import jax
import jax.numpy as jnp
from jax import lax
from jax.experimental import pallas as pl
from jax.experimental.pallas import tpu as pltpu

F32 = jnp.float32
BF16 = jnp.bfloat16

RWKV_HEAD = 64
DECAY_LORA = 64
ICLR_LORA = 64
VRES_LORA = 32
GATE_LORA = 128
ATTN_HEAD = 128
MOBA_BLOCK = 256
MOBA_TOPK = 3
ROPE_THETA = 10000.0
NORM_EPS = 1e-6
LNX_EPS = 64e-5
LOG2E = 1.4426950408889634

LANES = 128
BF16_SUBLANES = 16
VRES_PAD = LANES
WKV_CHUNK = 64
MOBA_LOOKAHEAD = 6
MOBA_UNROLL = 4
ROW_CHUNK = 256
VMEM_LIMIT = 56 * 1024 * 1024


def _cparams(*sem):
    return pltpu.CompilerParams(dimension_semantics=sem, vmem_limit_bytes=VMEM_LIMIT)


_NN = (((1,), (0,)), ((), ()))
_NT = (((1,), (1,)), ((), ()))
_TN = (((0,), (0,)), ((), ()))


def _split2(x):
    hi = x.astype(BF16)
    lo = (x - hi.astype(F32)).astype(BF16)
    return hi, lo


def _split3(x):
    hi = x.astype(BF16)
    r1 = x - hi.astype(F32)
    mid = r1.astype(BF16)
    lo = (r1 - mid.astype(F32)).astype(BF16)
    return hi, mid, lo


def _mm(a, b, dims=_NN, passes=1):
    d = lambda p, q: lax.dot_general(p, q, dims, preferred_element_type=F32)
    if passes == 1:
        return d(a.astype(BF16), b.astype(BF16))
    ah, al = _split2(a)
    bh, bl = _split2(b)
    return d(ah, bh) + (d(ah, bl) + d(al, bh))


def _sigmoid(x):
    return 1.0 / (1.0 + jnp.exp(-x))


def _softplus(x):
    return jnp.maximum(x, 0.0) + jnp.log(1.0 + jnp.exp(-jnp.abs(x)))


def _in_proj_kernel(has_vres, c, spb, *refs):
    it = iter(refs)
    take = lambda cnt: [next(it) for _ in range(cnt)]
    x_ref, g_ref, w_ref, cos_ref, sin_ref, mum_ref, mul_ref, w0_ref, w2_ref, a0_ref, a2_ref, g2_ref, kk_ref, ka_ref = take(14)
    if has_vres:
        we_ref, muv_ref, vf_ref, v0_ref, v2_ref = take(5)
    r_ref, lw_ref, k_ref, v_ref, an_ref, b_ref, gate_ref = take(7)
    if not has_vres:
        vf32_ref, = take(1)
    qs_ref, ko_ref, vt_ref, bias_ref = take(4)
    km_ref, cm_ref, cl_ref = take(3)
    if has_vres:
        cv_ref, = take(1)

    i = pl.program_id(0)
    nb = km_ref.shape[0]
    blk = MOBA_BLOCK
    width = km_ref.shape[1]
    nh = width // ATTN_HEAD
    n_lora = mul_ref.shape[1]
    n_mix = 3 * c + n_lora
    pw = 2 * RWKV_HEAD

    @pl.when(i == 0)
    def _():
        for ref in [km_ref, cm_ref, cl_ref] + ([cv_ref] if has_vres else []):
            ref[...] = jnp.zeros_like(ref)

    blk_id = lax.broadcasted_iota(jnp.int32, (nb, blk), 0)
    km_row = lax.broadcasted_iota(jnp.int32, (nb, ATTN_HEAD), 0)
    first_row = lax.broadcasted_iota(jnp.int32, (blk, 1), 0) == 0
    even = lax.broadcasted_iota(jnp.int32, (blk, pw), 1) < RWKV_HEAD
    qk_scale = (ATTN_HEAD ** -0.5) * LOG2E

    def shifted(cur, carry_ref, lo, hi, mu_ref, start):
        last = jnp.where(start, 0.0, carry_ref[:, lo:hi])
        prev = jnp.where(first_row, last, pltpu.roll(cur, 1, axis=0))
        carry_ref[:, lo:hi] = cur[blk - 1:blk, :]
        return cur + (prev - cur) * mu_ref[:, lo:hi]

    for j in range(x_ref.shape[0] // blk):
        rows = pl.ds(j * blk, blk)
        x = x_ref[rows, :]
        ms = jnp.mean(x * x, axis=-1, keepdims=True)
        h = (x * lax.rsqrt(ms + NORM_EPS) * g_ref[...]).astype(BF16)
        proj = lambda c0, c1: jnp.dot(h, w_ref[:, c0:c1], preferred_element_type=F32)
        bid = (i % spb) * (x_ref.shape[0] // blk) + j
        start = bid == 0

        z_l = proj(3 * c, n_mix)
        if has_vres:
            z_e = jnp.dot(h, we_ref[...], preferred_element_type=F32)
        z_r = proj(0, c)

        zl = shifted(z_l, cl_ref, 0, n_lora, mul_ref, start)
        o = 0
        wd = zl[:, o:o + DECAY_LORA]
        o += DECAY_LORA
        ad = zl[:, o:o + ICLR_LORA]
        o += ICLR_LORA
        gd = zl[:, o:o + GATE_LORA]
        w_log = -_softplus(-(w0_ref[...] + _mm(jnp.tanh(wd), w2_ref[...], passes=3))) - 0.5
        lw_ref[rows, :] = -jnp.exp(w_log)
        a = _sigmoid(a0_ref[...] + _mm(ad, a2_ref[...]))
        gate_ref[rows, :] = _mm(_sigmoid(gd), g2_ref[...]).astype(gate_ref.dtype)
        if has_vres:
            vd = shifted(z_e, cv_ref, 0, z_e.shape[1], muv_ref, start)
            v_mix = _sigmoid(v0_ref[...] + _mm(vd, v2_ref[...]))
        z_k, z_v = proj(c, 2 * c), proj(2 * c, 3 * c)

        r_ref[rows, :] = shifted(z_r, cm_ref, 0, c, mum_ref, start).astype(r_ref.dtype)
        k = shifted(z_k, cm_ref, c, 2 * c, mum_ref, start)
        v = shifted(z_v, cm_ref, 2 * c, 3 * c, mum_ref, start)
        if has_vres:
            v = v + (vf_ref[rows, :] - v) * v_mix
        else:
            vf32_ref[rows, :] = v
        v_ref[rows, :] = v.astype(v_ref.dtype)
        kk = k * kk_ref[...]
        unit = []
        for p in range(c // pw):
            xk = kk[:, p * pw:(p + 1) * pw]
            sq = xk * xk
            ss = jnp.where(even, jnp.sum(jnp.where(even, sq, 0.0), axis=-1, keepdims=True),
                           jnp.sum(jnp.where(even, 0.0, sq), axis=-1, keepdims=True))
            unit.append(xk / jnp.maximum(jnp.sqrt(ss), 1e-12))
        kk = jnp.concatenate(unit, axis=1)
        k_ref[rows, :] = (k * (1.0 + (a - 1.0) * ka_ref[...])).astype(k_ref.dtype)
        an_ref[rows, :] = (-kk).astype(an_ref.dtype)
        b_ref[rows, :] = (kk * a).astype(b_ref.dtype)
        q_all, k_all = (proj(n_mix + t * width, n_mix + (t + 1) * width) for t in range(2))

        cos = cos_ref[rows, :]
        sin = sin_ref[rows, :]
        past = blk_id < bid
        gates = []
        for hd in range(nh):
            sl = slice(hd * ATTN_HEAD, (hd + 1) * ATTN_HEAD)
            q, ka = q_all[:, sl], k_all[:, sl]
            qr = (q * cos + pltpu.roll(q, ATTN_HEAD // 2, axis=1) * sin).T
            kr = ka * cos + pltpu.roll(ka, ATTN_HEAD // 2, axis=1) * sin
            qs_ref[0, j, hd] = (qr * qk_scale).astype(BF16)
            ko_ref[0, j, hd] = kr.astype(BF16)
            gates.append(jnp.where(past, _mm(km_ref[:, sl], qr, passes=3), -jnp.inf))
            km_new = jnp.sum(kr, axis=0, keepdims=True) * (1.0 / blk)
            km_ref[:, sl] = jnp.where(km_row == bid, km_new, km_ref[:, sl])
        v_all = proj(n_mix + 2 * width, n_mix + 3 * width)

        for hd, gate in enumerate(gates):
            rank = jnp.zeros((nb, blk), jnp.int32)
            for m in range(nb):
                gm = gate[m:m + 1, :]
                rank += ((gm > gate) | ((gm == gate) & (m < blk_id))).astype(jnp.int32)
            bias_ref[0, j, hd] = jnp.where(past & (rank < MOBA_TOPK), 0.0, -jnp.inf)
            vt_ref[0, j, hd] = v_all[:, hd * ATTN_HEAD:(hd + 1) * ATTN_HEAD].T.astype(BF16)


def _in_proj(x3, gain, w, c, mu, w0, w2, a0, a2, g2, k_k, k_a, cos2, sin2, vres=None, tm=256):
    bsz, s, d = x3.shape
    n = w.shape[1]
    n_lora = w2.shape[0] + a2.shape[0] + g2.shape[0]
    n_mix = 3 * c + n_lora
    width = (n - n_mix) // 3
    nh = width // ATTN_HEAD
    nb = s // MOBA_BLOCK
    bpt = tm // MOBA_BLOCK
    spb = s // tm
    assert s % tm == 0 and tm % MOBA_BLOCK == 0 and c % (2 * RWKV_HEAD) == 0 and n_lora % LANES == 0
    assert (n - n_mix) % (3 * ATTN_HEAD) == 0
    has_vres = vres is not None
    once = pl.Buffered(1)
    row = lambda a: a.reshape(1, -1)
    full = lambda a: pl.BlockSpec(a.shape, lambda i: (0,) * a.ndim)
    tile = lambda wd_: pl.BlockSpec((tm, wd_), lambda i: (i, 0))
    small = [row(mu[:3 * c]), row(mu[3 * c:]), row(w0), w2, row(a0), a2, g2, row(k_k), row(k_a)]
    ins = [x3.reshape(bsz * s, d), gain.reshape(1, d), w, cos2, sin2] + small
    in_specs = [tile(d), pl.BlockSpec((1, d), lambda i: (0, 0)),
                pl.BlockSpec((d, n), lambda i: (0, 0), pipeline_mode=once),
                pl.BlockSpec((tm, ATTN_HEAD), lambda i: (i % spb, 0)),
                pl.BlockSpec((tm, ATTN_HEAD), lambda i: (i % spb, 0))] + [full(a) for a in small]
    scratch = [pltpu.VMEM((nb, width), F32), pltpu.VMEM((1, 3 * c), F32), pltpu.VMEM((1, n_lora), F32)]
    if has_vres:
        w_v, mu_v, v_first, v0, v2 = vres
        extra = [w_v, row(mu_v), v_first, row(v0), v2]
        ins += extra
        in_specs += [pl.BlockSpec(w_v.shape, lambda i: (0, 0), pipeline_mode=once), full(extra[1]), tile(c),
                     full(extra[3]), full(extra[4])]
        scratch.append(pltpu.VMEM((1, w_v.shape[1]), F32))
    mix_dtypes = [BF16, F32, BF16, BF16, BF16, BF16, BF16] + ([] if has_vres else [F32])
    t_spec = pl.BlockSpec((1, bpt, nh, ATTN_HEAD, MOBA_BLOCK), lambda i: (i // spb, i % spb, 0, 0, 0))
    n_spec = pl.BlockSpec((1, bpt, nh, MOBA_BLOCK, ATTN_HEAD), lambda i: (i // spb, i % spb, 0, 0, 0))
    out_specs = [tile(c)] * len(mix_dtypes) + [
        t_spec, n_spec, t_spec, pl.BlockSpec((1, bpt, nh, nb, MOBA_BLOCK), lambda i: (i // spb, i % spb, 0, 0, 0))]
    out_shape = [jax.ShapeDtypeStruct((bsz * s, c), dt) for dt in mix_dtypes] + [
        jax.ShapeDtypeStruct((bsz, nb, nh, ATTN_HEAD, MOBA_BLOCK), BF16),
        jax.ShapeDtypeStruct((bsz, nb, nh, MOBA_BLOCK, ATTN_HEAD), BF16),
        jax.ShapeDtypeStruct((bsz, nb, nh, ATTN_HEAD, MOBA_BLOCK), BF16),
        jax.ShapeDtypeStruct((bsz, nb, nh, nb, MOBA_BLOCK), F32)]
    out = pl.pallas_call(
        functools.partial(_in_proj_kernel, has_vres, c, spb),
        grid=(bsz * spb,),
        in_specs=in_specs,
        out_specs=out_specs,
        out_shape=out_shape,
        scratch_shapes=scratch,
        compiler_params=_cparams("arbitrary"),
        name="in_proj",
    )(*ins)
    mix = [o.reshape(bsz, s, c) for o in out[:7]]
    return mix, (None if has_vres else out[7]), out[-4:]


def _dots(a_list, b_list, dims=_NN):
    return [lax.dot_general(a.astype(BF16), b.astype(BF16), dims, preferred_element_type=F32)
            for a, b in zip(a_list, b_list)]


def _pair_diag(x, even):
    return jnp.concatenate([jnp.where(even, x, 0.0), jnp.where(even, 0.0, x)], axis=0)


def _unit_lower_inverse(a_list, row_w, col_w, even):
    n = row_w.shape[0]
    lower = row_w > col_w
    base = lower & ((row_w >> 1) == (col_w >> 1))
    t = [jnp.where(row_w == col_w, 1.0, jnp.where(base, a, 0.0)) for a in a_list]
    sh = 1
    while (2 << sh) <= n:
        sub = lower & ((row_w >> (sh + 1)) == (col_w >> (sh + 1))) & ((row_w >> sh) != (col_w >> sh))
        off = [_pair_diag(jnp.where(sub, a, 0.0), even) for a in a_list]
        upd = _dots(_dots(t, off), [_pair_diag(x, even) for x in t])
        t = [x + u for x, u in zip(t, upd)]
        sh += 1
    return t


def _mm_exact_rhs_left(l_bf16, a):
    d = lambda p: lax.dot_general(l_bf16, p, _NN, preferred_element_type=F32)
    hi, mid, lo = _split3(a)
    return d(hi) + (d(mid) + d(lo))


def _wkv_kernel(n_cast, *refs):
    r_ref, lw_ref, k_ref, v_ref, an_ref, b_ref, g_ref, rk_ref, lg_ref, lb_ref = refs[:10]
    cast_in, y_ref = refs[10:10 + n_cast], refs[10 + n_cast]
    cast_out, state_ref = refs[11 + n_cast:11 + 2 * n_cast], refs[11 + 2 * n_cast]
    for src, dst in zip(cast_in, cast_out):
        dst[...] = src[...].astype(dst.dtype)

    @pl.when(pl.program_id(1) == 0)
    def _():
        state_ref[...] = jnp.zeros_like(state_ref)

    for c0 in range(0, r_ref.shape[1], WKV_CHUNK):
        _wkv_chunk(slice(c0, c0 + WKV_CHUNK), *refs[:10], y_ref, state_ref)


def _wkv_chunk(rs, r_ref, lw_ref, k_ref, v_ref, an_ref, b_ref, g_ref, rk_ref, lg_ref, lb_ref, y_ref, state_ref):
    nb, ln = r_ref.shape[0], WKV_CHUNK
    n = RWKV_HEAD
    pw = 2 * n
    npair = r_ref.shape[2] // pw
    ent = [(bi, slice(p * pw, (p + 1) * pw)) for bi in range(nb) for p in range(npair)]

    rows = lax.broadcasted_iota(jnp.int32, (ln, ln), 0)
    cols = lax.broadcasted_iota(jnp.int32, (ln, ln), 1)
    tril = jnp.where(rows >= cols, 1.0, 0.0).astype(BF16)
    r, k, v, w_end, r_t, a_t, b_t, k_t, b_h, k_h = ([] for _ in range(10))
    for bi in range(nb):
        lw = lw_ref[bi, rs, :]
        cw = _mm_exact_rhs_left(tril, lw)
        cw_end = cw[ln - 1:ln, :]
        e_neg = jnp.exp(-cw)
        e_end = jnp.exp(cw_end - cw)
        bb = b_ref[bi, rs, :].astype(F32)
        r.append(r_ref[bi, rs, :].astype(F32))
        k.append(k_ref[bi, rs, :].astype(F32))
        v.append(v_ref[bi, rs, :].astype(F32))
        w_end.append(jnp.exp(cw_end))
        r_t.append(r[bi] * jnp.exp(cw))
        a_t.append(an_ref[bi, rs, :].astype(F32) * jnp.exp(cw - lw))
        b_t.append(bb * e_neg)
        k_t.append(k[bi] * e_neg)
        b_h.append(bb * e_end)
        k_h.append(k[bi] * e_end)

    lane = lax.broadcasted_iota(jnp.int32, (ln, pw), 1)
    row_w = lax.broadcasted_iota(jnp.int32, (ln, pw), 0)
    even = lane < n
    even2 = lax.broadcasted_iota(jnp.int32, (2 * ln, pw), 1) < n
    col_w = lane & (n - 1)
    strict_w = row_w > col_w
    incl_w = row_w >= col_w
    zeros_w = jnp.zeros((ln, pw), F32)
    diag = lambda x: _pair_diag(x, even)
    swap = lambda x: jnp.concatenate([x[x.shape[0] // 2:], x[:x.shape[0] // 2]], axis=0)

    ar_p = [jnp.concatenate([a_t[bi][:, ps], r_t[bi][:, ps]], axis=0) for bi, ps in ent]
    bk_p = [jnp.concatenate([b_t[bi][:, ps], k_t[bi][:, ps]], axis=0).astype(BF16) for bi, ps in ent]
    kb_p = [jnp.concatenate([k_t[bi][:, ps], b_t[bi][:, ps]], axis=0).astype(BF16) for bi, ps in ent]
    am_e = _dots([jnp.where(even2, x, 0.0) for x in ar_p], bk_p, _NT)
    am_o = _dots([jnp.where(even2, 0.0, x) for x in ar_p], kb_p, _NT)
    a_ab = [jnp.where(strict_w, jnp.where(even, e[:ln], o[:ln]), 0.0) for e, o in zip(am_e, am_o)]
    a_ak = [jnp.where(strict_w, jnp.where(even, o[:ln], e[:ln]), 0.0) for e, o in zip(am_e, am_o)]
    a_rb = [jnp.where(incl_w, jnp.where(even, e[ln:], o[ln:]), 0.0) for e, o in zip(am_e, am_o)]
    a_rk = [jnp.where(incl_w, jnp.where(even, o[ln:], e[ln:]), 0.0) for e, o in zip(am_e, am_o)]
    v_p = [v[bi][:, ps] for bi, ps in ent]
    v_d = [diag(x) for x in v_p]
    akv = _dots(a_ak, [swap(x) for x in v_d])
    t = _unit_lower_inverse(a_ab, row_w, col_w, even)
    rhs = [jnp.concatenate([diag(a_t[bi][:, ps]), diag(x)], axis=1) for (bi, ps), x in zip(ent, akv)]
    pq = _dots(t, rhs)
    ry = _dots([jnp.concatenate([x, y], axis=1) for x, y in zip(a_rb, a_rk)],
               [jnp.concatenate([jnp.concatenate([diag(x[:, :pw]), diag(x[:, pw:])], axis=1),
                                 jnp.concatenate([jnp.zeros((2 * ln, pw), F32), swap(u)], axis=1)], axis=0)
                for x, u in zip(pq, v_d)])
    pqv_p = [jnp.concatenate([x, jnp.concatenate([zeros_w, u], axis=1)], axis=0)
             for x, u in zip(pq, v_p)]
    bkh_p = [jnp.concatenate([b_h[bi][:, ps], k_h[bi][:, ps]], axis=0) for bi, ps in ent]
    mn_p = _dots(bkh_p, pqv_p, _TN)
    sq_r = lax.broadcasted_iota(jnp.int32, (pw, pw), 0)
    sq_c = lax.broadcasted_iota(jnp.int32, (pw, pw), 1)
    same_head = (sq_r < n) == (sq_c < n)
    lhs_p = [jnp.concatenate([r_t[bi][:, ps] + x[:, :pw],
                              jnp.where(sq_r == sq_c, jnp.broadcast_to(w_end[bi][:, ps], (pw, pw)),
                                        jnp.where(same_head, m[:, :pw], 0.0))], axis=0)
             for (bi, ps), x, m in zip(ent, ry, mn_p)]
    st = [state_ref[e] for e in range(len(ent))]
    upd = _dots(lhs_p, st)
    for e in range(len(ent)):
        state_ref[e] = upd[e][ln:, :] + jnp.where(same_head, mn_p[e][:, pw:], 0.0)
    y0_p = [x[:, pw:] for x in ry]

    hsum = lambda x: jnp.where(even, jnp.sum(jnp.where(even, x, 0.0), axis=-1, keepdims=True),
                               jnp.sum(jnp.where(even, 0.0, x), axis=-1, keepdims=True))
    inv_n = 1.0 / n
    yn = []
    for u, y0 in zip(upd, y0_p):
        y = u[:ln, :] + y0
        d = y - hsum(y) * inv_n
        yn.append(d * lax.rsqrt(hsum(d * d) * inv_n + LNX_EPS))
    for bi in range(nb):
        rk = r[bi] * k[bi] * rk_ref[...]
        mine = range(bi * npair, (bi + 1) * npair)
        bonus = jnp.concatenate([hsum(rk[:, ent[e][1]]) * v_p[e] for e in mine], axis=1)
        out = jnp.concatenate([yn[e] for e in mine], axis=1) * lg_ref[...] + lb_ref[...] + bonus
        y_ref[bi, rs, :] = (out * g_ref[bi, rs, :].astype(F32)).astype(y_ref.dtype)


def _wkv(r, lw, k, v, an, b, g, r_k, lnx_g, lnx_b, casts=(), heads_per_step=16, chunks_per_step=2):
    bsz, s, c = r.shape
    wb = heads_per_step * RWKV_HEAD
    ts = chunks_per_step * WKV_CHUNK
    nc = s // ts
    assert s % ts == 0 and c % wb == 0 and heads_per_step % 2 == 0 and WKV_CHUNK == RWKV_HEAD
    assert not casts or c == wb
    spec = pl.BlockSpec((bsz, ts, wb), lambda hi, ci: (0, ci, hi))
    pspec = pl.BlockSpec((1, wb), lambda hi, ci: (0, hi))
    row = lambda a: a.reshape(1, -1)
    cast_ins, cast_in_specs, cast_out_specs, cast_out_shape = [], [], [], []
    for w, layer in casts:
        nl, rows, cols = w.shape
        assert rows % (BF16_SUBLANES * nc) == 0
        slab = rows // nc
        cast_ins.append(w.reshape(nl, nc, slab, cols))
        cast_in_specs.append(pl.BlockSpec((None, None, slab, cols), lambda hi, ci, layer=layer: (layer, ci, 0, 0)))
        cast_out_specs.append(pl.BlockSpec((None, slab, cols), lambda hi, ci: (ci, 0, 0)))
        cast_out_shape.append(jax.ShapeDtypeStruct((nc, slab, cols), BF16))
    out = pl.pallas_call(
        functools.partial(_wkv_kernel, len(casts)),
        grid=(c // wb, nc),
        in_specs=[spec] * 7 + [pspec] * 3 + cast_in_specs,
        out_specs=[spec] + cast_out_specs,
        out_shape=[jax.ShapeDtypeStruct((bsz, s, c), BF16)] + cast_out_shape,
        scratch_shapes=[pltpu.VMEM((bsz * heads_per_step // 2, 2 * RWKV_HEAD, 2 * RWKV_HEAD), F32)],
        compiler_params=_cparams("parallel", "arbitrary"),
        name="wkv",
    )(r, lw, k, v, an, b, g, row(r_k), row(lnx_g), row(lnx_b), *cast_ins)
    return out[0], [o.reshape(w.shape[1], w.shape[2]) for o, (w, _) in zip(out[1:], casts)]


def _moba_kernel(qs_ref, k_ref, vt_ref, bias_ref, o_ref):
    blk = MOBA_BLOCK
    dh = ATTN_HEAD
    hs = range(qs_ref.shape[2])
    qb = pl.program_id(2)
    neg = -jnp.inf
    qs = [qs_ref[0, 0, h] for h in hs]
    ki = lax.broadcasted_iota(jnp.int32, (blk, blk), 0)
    qi = lax.broadcasted_iota(jnp.int32, (blk, blk), 1)
    causal = ki <= qi

    def pipelined(work, stage):
        scores = lambda kb, h: jnp.dot(k_ref[0, kb, h], qs[h], preferred_element_type=F32)
        ahead = [scores(*w) for w in work[:MOBA_LOOKAHEAD]]
        for i, (kb, h) in enumerate(work):
            if i + MOBA_LOOKAHEAD < len(work):
                ahead.append(scores(*work[i + MOBA_LOOKAHEAD]))
            stage(kb, h, ahead[i])

    m_run, l_run, acc = [None] * len(hs), [None] * len(hs), [None] * len(hs)

    def own_block(kb, h, s):
        s = jnp.where(causal, s, neg)
        m_run[h] = jnp.max(s, axis=0, keepdims=True)
        p = jnp.exp2(s - m_run[h])
        l_run[h] = jnp.sum(p, axis=0, keepdims=True)
        acc[h] = jnp.dot(vt_ref[0, kb, h], p.astype(BF16), preferred_element_type=F32)

    pipelined([(qb, h) for h in hs], own_block)

    def past_blocks(kbs, carry):
        m_c, l_c, acc_c = (list(c) for c in carry)

        def stage(kb, h, s):
            b = bias_ref[0, 0, h, pl.ds(kb, 1), :]
            m_new = jnp.where(b == 0.0, jnp.maximum(m_c[h], jnp.max(s, axis=0, keepdims=True)), m_c[h])
            alpha = jnp.exp2(m_c[h] - m_new)
            p = jnp.exp2(s - (m_new - b))
            pv = jnp.dot(vt_ref[0, kb, h], p.astype(BF16), preferred_element_type=F32)
            m_c[h] = m_new
            l_c[h] = alpha * l_c[h] + jnp.sum(p, axis=0, keepdims=True)
            acc_c[h] = alpha * acc_c[h] + pv

        pipelined([(kb, h) for kb in kbs for h in hs], stage)
        return tuple(m_c), tuple(l_c), tuple(acc_c)

    u = MOBA_UNROLL
    carry = lax.fori_loop(0, qb // u, lambda j, c: past_blocks([j * u + i for i in range(u)], c),
                          (tuple(m_run), tuple(l_run), tuple(acc)))
    _, l_fin, acc = lax.fori_loop((qb // u) * u, qb, lambda kb, c: past_blocks([kb], c), carry)
    for h in hs:
        o_ref[0, :, h * dh:(h + 1) * dh] = (acc[h] / l_fin[h]).T.astype(o_ref.dtype)


def _moba(qs, k, vt, bias, heads_per_step=8):
    bsz, nb, nh, dh, blk = qs.shape
    hp = heads_per_step
    assert nh % hp == 0
    return pl.pallas_call(
        _moba_kernel,
        grid=(bsz, nh // hp, nb),
        in_specs=[pl.BlockSpec((1, 1, hp, dh, blk), lambda b, h, i: (b, i, h, 0, 0)),
                  pl.BlockSpec((1, nb, hp, blk, dh), lambda b, h, i: (b, 0, h, 0, 0)),
                  pl.BlockSpec((1, nb, hp, dh, blk), lambda b, h, i: (b, 0, h, 0, 0)),
                  pl.BlockSpec((1, 1, hp, nb, blk), lambda b, h, i: (b, i, h, 0, 0))],
        out_specs=pl.BlockSpec((1, blk, hp * dh), lambda b, h, i: (b, i, h)),
        out_shape=jax.ShapeDtypeStruct((bsz, nb * blk, nh * dh), BF16),
        compiler_params=_cparams("parallel", "parallel", "arbitrary"),
        name="moba",
    )(qs, k, vt, bias)


def _out_proj_kernel(yr_ref, ya_ref, wr_ref, wa_ref, x_ref, g_ref, o_ref):
    for r in range(0, x_ref.shape[0], ROW_CHUNK):
        rows = pl.ds(r, ROW_CHUNK)
        y = jnp.dot(yr_ref[rows, :], wr_ref[...], preferred_element_type=F32)
        y += jnp.dot(ya_ref[rows, :], wa_ref[...], preferred_element_type=F32)
        ms = jnp.mean(y * y, axis=-1, keepdims=True)
        o_ref[rows, :] = x_ref[rows, :] + y * lax.rsqrt(ms + NORM_EPS) * g_ref[...]


def _out_proj(y_r, y_a, w, x2, gain, tm=512):
    m, d = x2.shape
    cw = y_r.shape[1]
    assert y_a.shape[1] == cw and w.shape[0] == 2 * cw and m % tm == 0
    return pl.pallas_call(
        _out_proj_kernel,
        grid=(m // tm,),
        in_specs=[pl.BlockSpec((tm, cw), lambda i: (i, 0)), pl.BlockSpec((tm, cw), lambda i: (i, 0)),
                  pl.BlockSpec((cw, d), lambda i: (0, 0)), pl.BlockSpec((cw, d), lambda i: (1, 0)),
                  pl.BlockSpec((tm, d), lambda i: (i, 0)), pl.BlockSpec((1, d), lambda i: (0, 0))],
        out_specs=pl.BlockSpec((tm, d), lambda i: (i, 0)),
        out_shape=jax.ShapeDtypeStruct((m, d), F32),
        compiler_params=_cparams("parallel"),
        name="out_proj",
    )(y_r, y_a, w, w, x2, gain.reshape(1, d))


def _mlp_kernel(x_ref, gpre_ref, wu_ref, wd_ref, gpost_ref, o_ref, h_ref):
    f = pl.program_id(1)
    last = pl.num_programs(1) - 1
    chunks = [pl.ds(r, ROW_CHUNK) for r in range(0, x_ref.shape[0], ROW_CHUNK)]

    def part(h):
        u = jnp.maximum(jnp.dot(h, wu_ref[...], preferred_element_type=F32), 0.0)
        return jnp.dot((u * u).astype(BF16), wd_ref[...], preferred_element_type=F32)

    @pl.when(f == 0)
    def _():
        for rows in chunks:
            x = x_ref[rows, :]
            ms = jnp.mean(x * x, axis=-1, keepdims=True)
            h = (x * lax.rsqrt(ms + NORM_EPS) * gpre_ref[...]).astype(BF16)
            h_ref[rows, :] = h
            o_ref[rows, :] = part(h)

    @pl.when((f > 0) & (f < last))
    def _():
        for rows in chunks:
            o_ref[rows, :] += part(h_ref[rows, :])

    @pl.when(f == last)
    def _():
        for rows in chunks:
            mlp = o_ref[rows, :] + part(h_ref[rows, :])
            ms = jnp.mean(mlp * mlp, axis=-1, keepdims=True)
            o_ref[rows, :] = x_ref[rows, :] + mlp * lax.rsqrt(ms + NORM_EPS) * gpost_ref[...]


def _mlp(x2, g_pre, w_up, w_down, g_post, tm=1024, tf=1024):
    m, d = x2.shape
    dff = w_up.shape[1]
    assert m % tm == 0 and dff % tf == 0 and dff // tf >= 2 and tm % ROW_CHUNK == 0
    return pl.pallas_call(
        _mlp_kernel,
        grid=(m // tm, dff // tf),
        in_specs=[pl.BlockSpec((tm, d), lambda i, f: (i, 0)),
                  pl.BlockSpec((1, d), lambda i, f: (0, 0)),
                  pl.BlockSpec((d, tf), lambda i, f: (0, f)),
                  pl.BlockSpec((tf, d), lambda i, f: (f, 0)),
                  pl.BlockSpec((1, d), lambda i, f: (0, 0))],
        out_specs=pl.BlockSpec((tm, d), lambda i, f: (i, 0)),
        out_shape=jax.ShapeDtypeStruct((m, d), F32),
        scratch_shapes=[pltpu.VMEM((tm, d), BF16)],
        compiler_params=_cparams("parallel", "arbitrary"),
        name="mlp",
    )(x2, g_pre.reshape(1, d), w_up, w_down, g_post.reshape(1, d))


def _rope_tables(s):
    half = ATTN_HEAD // 2
    inv_freq = ROPE_THETA ** (-jnp.arange(half, dtype=F32) / half)
    ang = jnp.arange(s).astype(F32)[:, None] * inv_freq[None, :]
    cos, sin = jnp.cos(ang), jnp.sin(ang)
    return jnp.concatenate([cos, cos], axis=-1), jnp.concatenate([-sin, sin], axis=-1)


def kernel(x, norm_mix_pre, norm_mix_post, norm_mlp_pre, norm_mlp_post, w_in, w_in_vres, shift_mu, shift_mu_vres, decay_w0, decay_w2, iclr_a0, iclr_a2, vres_v0, vres_v2, gate_g2, k_k, k_a, r_k, lnx_gain, lnx_bias, w_out, w_up, w_down):
    bsz, s, d = x.shape
    depth = w_in.shape[0]
    c = decay_w0.shape[1]
    n_lora = DECAY_LORA + ICLR_LORA + GATE_LORA
    n_shift = 3 * c + n_lora
    ca = (w_in.shape[2] - n_shift) // 3
    cos2, sin2 = _rope_tables(s)
    w_in16 = w_in[0].astype(BF16)
    pad_v = VRES_PAD - VRES_LORA
    x2 = x.reshape(bsz * s, d)
    v_first = None
    for i in range(depth):
        if i == 0:
            vres = None
        else:
            vres = (jnp.pad(w_in_vres[i - 1], ((0, 0), (0, pad_v))).astype(BF16),
                    jnp.pad(shift_mu_vres[i - 1], (0, pad_v)), v_first, vres_v0[i - 1],
                    jnp.pad(vres_v2[i - 1], ((0, pad_v), (0, 0))))
        mix, v_layer, attn = _in_proj(x2.reshape(bsz, s, d), norm_mix_pre[i], w_in16, c, shift_mu[i], decay_w0[i],
                                      decay_w2[i], iclr_a0[i], iclr_a2[i], gate_g2[i], k_k[i], k_a[i], cos2, sin2,
                                      vres)
        if i == 0:
            v_first = v_layer

        casts = [(w_out, i), (w_up, i), (w_down, i)] + ([(w_in, i + 1)] if i + 1 < depth else [])
        y_r, w16 = _wkv(*mix, r_k[i].reshape(-1), lnx_gain[i], lnx_bias[i], casts)
        w_out16, w_up16, w_down16 = w16[:3]

        y_a = _moba(*attn)

        x2 = _out_proj(y_r.reshape(bsz * s, c), y_a.reshape(bsz * s, ca), w_out16, x2, norm_mix_post[i])
        x2 = _mlp(x2, norm_mlp_pre[i], w_up16, w_down16, norm_mlp_post[i])
        if i + 1 < depth:
            w_in16 = w16[3]
    return x2.reshape(bsz, s, d)
```

```python
import functools

import jax
import jax.numpy as jnp
from jax import lax
from jax.experimental import pallas as pl
from jax.experimental.pallas import tpu as pltpu

F32 = jnp.float32
BF16 = jnp.bfloat16

RWKV_HEAD = 64
DECAY_LORA = 64
ICLR_LORA = 64
VRES_LORA = 32
GATE_LORA = 128
ATTN_HEAD = 128
MOBA_BLOCK = 256
MOBA_TOPK = 3
ROPE_THETA = 10000.0
NORM_EPS = 1e-6
LNX_EPS = 64e-5
LOG2E = 1.4426950408889634

LANES = 128
BF16_SUBLANES = 16
VRES_PAD = LANES
WKV_CHUNK = 64
MOBA_LOOKAHEAD = 8
MOBA_UNROLL = 4
ROW_CHUNK = 256
VMEM_LIMIT = 56 * 1024 * 1024


def _cparams(*sem):
    return pltpu.CompilerParams(dimension_semantics=sem, vmem_limit_bytes=VMEM_LIMIT)


_NN = (((1,), (0,)), ((), ()))
_NT = (((1,), (1,)), ((), ()))
_TN = (((0,), (0,)), ((), ()))


def _split2(x):
    hi = x.astype(BF16)
    lo = (x - hi.astype(F32)).astype(BF16)
    return hi, lo


def _split3(x):
    hi = x.astype(BF16)
    r1 = x - hi.astype(F32)
    mid = r1.astype(BF16)
    lo = (r1 - mid.astype(F32)).astype(BF16)
    return hi, mid, lo


def _mm(a, b, dims=_NN, passes=1):
    d = lambda p, q: lax.dot_general(p, q, dims, preferred_element_type=F32)
    if passes == 1:
        return d(a.astype(BF16), b.astype(BF16))
    ah, al = _split2(a)
    bh, bl = _split2(b)
    return d(ah, bh) + (d(ah, bl) + d(al, bh))


def _sigmoid(x):
    return 1.0 / (1.0 + jnp.exp(-x))


def _softplus(x):
    return jnp.maximum(x, 0.0) + jnp.log(1.0 + jnp.exp(-jnp.abs(x)))


def _in_proj_kernel(has_vres, c, spb, *refs):
    it = iter(refs)
    take = lambda cnt: [next(it) for _ in range(cnt)]
    x_ref, g_ref, w_ref, cos_ref, sin_ref, mum_ref, mul_ref, w0_ref, w2_ref, a0_ref, a2_ref, g2_ref, kk_ref, ka_ref = take(14)
    if has_vres:
        we_ref, muv_ref, vf_ref, v0_ref, v2_ref = take(5)
    r_ref, lw_ref, k_ref, v_ref, an_ref, b_ref, gate_ref = take(7)
    if not has_vres:
        vf32_ref, = take(1)
    qs_ref, ko_ref, vt_ref, bias_ref = take(4)
    km_ref, cm_ref, cl_ref = take(3)
    if has_vres:
        cv_ref, = take(1)

    i = pl.program_id(0)
    nb = km_ref.shape[0]
    blk = MOBA_BLOCK
    width = km_ref.shape[1]
    nh = width // ATTN_HEAD
    n_lora = mul_ref.shape[1]
    n_mix = 3 * c + n_lora
    pw = 2 * RWKV_HEAD

    @pl.when(i == 0)
    def _():
        for ref in [km_ref, cm_ref, cl_ref] + ([cv_ref] if has_vres else []):
            ref[...] = jnp.zeros_like(ref)

    blk_id = lax.broadcasted_iota(jnp.int32, (nb, blk), 0)
    km_row = lax.broadcasted_iota(jnp.int32, (nb, ATTN_HEAD), 0)
    first_row = lax.broadcasted_iota(jnp.int32, (blk, 1), 0) == 0
    even = lax.broadcasted_iota(jnp.int32, (blk, pw), 1) < RWKV_HEAD
    qk_scale = (ATTN_HEAD ** -0.5) * LOG2E

    def shifted(cur, carry_ref, lo, hi, mu_ref, start):
        last = jnp.where(start, 0.0, carry_ref[:, lo:hi])
        prev = jnp.where(first_row, last, pltpu.roll(cur, 1, axis=0))
        carry_ref[:, lo:hi] = cur[blk - 1:blk, :]
        return cur + (prev - cur) * mu_ref[:, lo:hi]

    for j in range(x_ref.shape[0] // blk):
        rows = pl.ds(j * blk, blk)
        x = x_ref[rows, :]
        ms = jnp.mean(x * x, axis=-1, keepdims=True)
        h = (x * lax.rsqrt(ms + NORM_EPS) * g_ref[...]).astype(BF16)
        proj = lambda c0, c1: jnp.dot(h, w_ref[:, c0:c1], preferred_element_type=F32)
        bid = (i % spb) * (x_ref.shape[0] // blk) + j
        start = bid == 0

        z_l = proj(3 * c, n_mix)
        if has_vres:
            z_e = jnp.dot(h, we_ref[...], preferred_element_type=F32)
        z_r = proj(0, c)

        zl = shifted(z_l, cl_ref, 0, n_lora, mul_ref, start)
        o = 0
        wd = zl[:, o:o + DECAY_LORA]
        o += DECAY_LORA
        ad = zl[:, o:o + ICLR_LORA]
        o += ICLR_LORA
        gd = zl[:, o:o + GATE_LORA]
        w_log = -_softplus(-(w0_ref[...] + _mm(jnp.tanh(wd), w2_ref[...], passes=3))) - 0.5
        lw_ref[rows, :] = -jnp.exp(w_log)
        a = _sigmoid(a0_ref[...] + _mm(ad, a2_ref[...]))
        gate_ref[rows, :] = _mm(_sigmoid(gd), g2_ref[...]).astype(gate_ref.dtype)
        if has_vres:
            vd = shifted(z_e, cv_ref, 0, z_e.shape[1], muv_ref, start)
            v_mix = _sigmoid(v0_ref[...] + _mm(vd, v2_ref[...]))
        z_k, z_v = proj(c, 2 * c), proj(2 * c, 3 * c)

        r_ref[rows, :] = shifted(z_r, cm_ref, 0, c, mum_ref, start).astype(r_ref.dtype)
        k = shifted(z_k, cm_ref, c, 2 * c, mum_ref, start)
        v = shifted(z_v, cm_ref, 2 * c, 3 * c, mum_ref, start)
        if has_vres:
            v = v + (vf_ref[rows, :] - v) * v_mix
        else:
            vf32_ref[rows, :] = v
        v_ref[rows, :] = v.astype(v_ref.dtype)
        kk = k * kk_ref[...]
        unit = []
        for p in range(c // pw):
            xk = kk[:, p * pw:(p + 1) * pw]
            sq = xk * xk
            ss = jnp.where(even, jnp.sum(jnp.where(even, sq, 0.0), axis=-1, keepdims=True),
                           jnp.sum(jnp.where(even, 0.0, sq), axis=-1, keepdims=True))
            unit.append(xk / jnp.maximum(jnp.sqrt(ss), 1e-12))
        kk = jnp.concatenate(unit, axis=1)
        k_ref[rows, :] = (k * (1.0 + (a - 1.0) * ka_ref[...])).astype(k_ref.dtype)
        an_ref[rows, :] = (-kk).astype(an_ref.dtype)
        b_ref[rows, :] = (kk * a).astype(b_ref.dtype)
        q_all, k_all = (proj(n_mix + t * width, n_mix + (t + 1) * width) for t in range(2))

        cos = cos_ref[rows, :]
        sin = sin_ref[rows, :]
        past = blk_id < bid
        gates = []
        for hd in range(nh):
            sl = slice(hd * ATTN_HEAD, (hd + 1) * ATTN_HEAD)
            q, ka = q_all[:, sl], k_all[:, sl]
            qr = (q * cos + pltpu.roll(q, ATTN_HEAD // 2, axis=1) * sin).T
            kr = ka * cos + pltpu.roll(ka, ATTN_HEAD // 2, axis=1) * sin
            qs_ref[0, j, hd] = (qr * qk_scale).astype(BF16)
            ko_ref[0, j, hd] = kr.astype(BF16)
            gates.append(jnp.where(past, _mm(km_ref[:, sl], qr, passes=3), -jnp.inf))
            km_new = jnp.sum(kr, axis=0, keepdims=True) * (1.0 / blk)
            km_ref[:, sl] = jnp.where(km_row == bid, km_new, km_ref[:, sl])
        v_all = proj(n_mix + 2 * width, n_mix + 3 * width)

        for hd, gate in enumerate(gates):
            rank = jnp.zeros((nb, blk), jnp.int32)
            for m in range(nb):
                gm = gate[m:m + 1, :]
                rank += ((gm > gate) | ((gm == gate) & (m < blk_id))).astype(jnp.int32)
            bias_ref[0, j, hd] = jnp.where(past & (rank < MOBA_TOPK), 0.0, -jnp.inf)
            vt_ref[0, j, hd] = v_all[:, hd * ATTN_HEAD:(hd + 1) * ATTN_HEAD].T.astype(BF16)


def _in_proj(x3, gain, w, c, mu, w0, w2, a0, a2, g2, k_k, k_a, cos2, sin2, vres=None, tm=256):
    bsz, s, d = x3.shape
    n = w.shape[1]
    n_lora = w2.shape[0] + a2.shape[0] + g2.shape[0]
    n_mix = 3 * c + n_lora
    width = (n - n_mix) // 3
    nh = width // ATTN_HEAD
    nb = s // MOBA_BLOCK
    bpt = tm // MOBA_BLOCK
    spb = s // tm
    assert s % tm == 0 and tm % MOBA_BLOCK == 0 and c % (2 * RWKV_HEAD) == 0 and n_lora % LANES == 0
    assert (n - n_mix) % (3 * ATTN_HEAD) == 0
    has_vres = vres is not None
    once = pl.Buffered(1)
    row = lambda a: a.reshape(1, -1)
    full = lambda a: pl.BlockSpec(a.shape, lambda i: (0,) * a.ndim)
    tile = lambda wd_: pl.BlockSpec((tm, wd_), lambda i: (i, 0))
    small = [row(mu[:3 * c]), row(mu[3 * c:]), row(w0), w2, row(a0), a2, g2, row(k_k), row(k_a)]
    ins = [x3.reshape(bsz * s, d), gain.reshape(1, d), w, cos2, sin2] + small
    in_specs = [tile(d), pl.BlockSpec((1, d), lambda i: (0, 0)),
                pl.BlockSpec((d, n), lambda i: (0, 0), pipeline_mode=once),
                pl.BlockSpec((tm, ATTN_HEAD), lambda i: (i % spb, 0)),
                pl.BlockSpec((tm, ATTN_HEAD), lambda i: (i % spb, 0))] + [full(a) for a in small]
    scratch = [pltpu.VMEM((nb, width), F32), pltpu.VMEM((1, 3 * c), F32), pltpu.VMEM((1, n_lora), F32)]
    if has_vres:
        w_v, mu_v, v_first, v0, v2 = vres
        extra = [w_v, row(mu_v), v_first, row(v0), v2]
        ins += extra
        in_specs += [pl.BlockSpec(w_v.shape, lambda i: (0, 0), pipeline_mode=once), full(extra[1]), tile(c),
                     full(extra[3]), full(extra[4])]
        scratch.append(pltpu.VMEM((1, w_v.shape[1]), F32))
    mix_dtypes = [BF16, F32, BF16, BF16, BF16, BF16, BF16] + ([] if has_vres else [F32])
    t_spec = pl.BlockSpec((1, bpt, nh, ATTN_HEAD, MOBA_BLOCK), lambda i: (i // spb, i % spb, 0, 0, 0))
    n_spec = pl.BlockSpec((1, bpt, nh, MOBA_BLOCK, ATTN_HEAD), lambda i: (i // spb, i % spb, 0, 0, 0))
    out_specs = [tile(c)] * len(mix_dtypes) + [
        t_spec, n_spec, t_spec, pl.BlockSpec((1, bpt, nh, nb, MOBA_BLOCK), lambda i: (i // spb, i % spb, 0, 0, 0))]
    out_shape = [jax.ShapeDtypeStruct((bsz * s, c), dt) for dt in mix_dtypes] + [
        jax.ShapeDtypeStruct((bsz, nb, nh, ATTN_HEAD, MOBA_BLOCK), BF16),
        jax.ShapeDtypeStruct((bsz, nb, nh, MOBA_BLOCK, ATTN_HEAD), BF16),
        jax.ShapeDtypeStruct((bsz, nb, nh, ATTN_HEAD, MOBA_BLOCK), BF16),
        jax.ShapeDtypeStruct((bsz, nb, nh, nb, MOBA_BLOCK), F32)]
    out = pl.pallas_call(
        functools.partial(_in_proj_kernel, has_vres, c, spb),
        grid=(bsz * spb,),
        in_specs=in_specs,
        out_specs=out_specs,
        out_shape=out_shape,
        scratch_shapes=scratch,
        compiler_params=_cparams("arbitrary"),
        name="in_proj",
    )(*ins)
    mix = [o.reshape(bsz, s, c) for o in out[:7]]
    return mix, (None if has_vres else out[7]), out[-4:]


def _dots(a_list, b_list, dims=_NN):
    return [lax.dot_general(a.astype(BF16), b.astype(BF16), dims, preferred_element_type=F32)
            for a, b in zip(a_list, b_list)]


def _pair_diag(x, even):
    return jnp.concatenate([jnp.where(even, x, 0.0), jnp.where(even, 0.0, x)], axis=0)


def _unit_lower_inverse(a_list, row_w, col_w, even):
    n = row_w.shape[0]
    lower = row_w > col_w
    base = lower & ((row_w >> 1) == (col_w >> 1))
    t = [jnp.where(row_w == col_w, 1.0, jnp.where(base, a, 0.0)) for a in a_list]
    sh = 1
    while (2 << sh) <= n:
        sub = lower & ((row_w >> (sh + 1)) == (col_w >> (sh + 1))) & ((row_w >> sh) != (col_w >> sh))
        off = [_pair_diag(jnp.where(sub, a, 0.0), even) for a in a_list]
        upd = _dots(_dots(t, off), [_pair_diag(x, even) for x in t])
        t = [x + u for x, u in zip(t, upd)]
        sh += 1
    return t


def _mm_exact_rhs_left(l_bf16, a):
    d = lambda p: lax.dot_general(l_bf16, p, _NN, preferred_element_type=F32)
    hi, mid, lo = _split3(a)
    return d(hi) + (d(mid) + d(lo))


def _wkv_kernel(n_cast, *refs):
    r_ref, lw_ref, k_ref, v_ref, an_ref, b_ref, g_ref, rk_ref, lg_ref, lb_ref = refs[:10]
    cast_in, y_ref = refs[10:10 + n_cast], refs[10 + n_cast]
    cast_out, state_ref = refs[11 + n_cast:11 + 2 * n_cast], refs[11 + 2 * n_cast]
    for src, dst in zip(cast_in, cast_out):
        dst[...] = src[...].astype(dst.dtype)

    @pl.when(pl.program_id(1) == 0)
    def _():
        state_ref[...] = jnp.zeros_like(state_ref)

    for c0 in range(0, r_ref.shape[1], WKV_CHUNK):
        _wkv_chunk(slice(c0, c0 + WKV_CHUNK), *refs[:10], y_ref, state_ref)


def _wkv_chunk(rs, r_ref, lw_ref, k_ref, v_ref, an_ref, b_ref, g_ref, rk_ref, lg_ref, lb_ref, y_ref, state_ref):
    nb, ln = r_ref.shape[0], WKV_CHUNK
    n = RWKV_HEAD
    pw = 2 * n
    npair = r_ref.shape[2] // pw
    ent = [(bi, slice(p * pw, (p + 1) * pw)) for bi in range(nb) for p in range(npair)]

    rows = lax.broadcasted_iota(jnp.int32, (ln, ln), 0)
    cols = lax.broadcasted_iota(jnp.int32, (ln, ln), 1)
    tril = jnp.where(rows >= cols, 1.0, 0.0).astype(BF16)
    r, k, v, w_end, r_t, a_t, b_t, k_t, b_h, k_h = ([] for _ in range(10))
    for bi in range(nb):
        lw = lw_ref[bi, rs, :]
        cw = _mm_exact_rhs_left(tril, lw)
        cw_end = cw[ln - 1:ln, :]
        e_neg = jnp.exp(-cw)
        e_end = jnp.exp(cw_end - cw)
        bb = b_ref[bi, rs, :].astype(F32)
        r.append(r_ref[bi, rs, :].astype(F32))
        k.append(k_ref[bi, rs, :].astype(F32))
        v.append(v_ref[bi, rs, :].astype(F32))
        w_end.append(jnp.exp(cw_end))
        r_t.append(r[bi] * jnp.exp(cw))
        a_t.append(an_ref[bi, rs, :].astype(F32) * jnp.exp(cw - lw))
        b_t.append(bb * e_neg)
        k_t.append(k[bi] * e_neg)
        b_h.append(bb * e_end)
        k_h.append(k[bi] * e_end)

    lane = lax.broadcasted_iota(jnp.int32, (ln, pw), 1)
    row_w = lax.broadcasted_iota(jnp.int32, (ln, pw), 0)
    even = lane < n
    even2 = lax.broadcasted_iota(jnp.int32, (2 * ln, pw), 1) < n
    col_w = lane & (n - 1)
    strict_w = row_w > col_w
    incl_w = row_w >= col_w
    zeros_w = jnp.zeros((ln, pw), F32)
    diag = lambda x: _pair_diag(x, even)
    swap = lambda x: jnp.concatenate([x[x.shape[0] // 2:], x[:x.shape[0] // 2]], axis=0)

    ar_p = [jnp.concatenate([a_t[bi][:, ps], r_t[bi][:, ps]], axis=0) for bi, ps in ent]
    bk_p = [jnp.concatenate([b_t[bi][:, ps], k_t[bi][:, ps]], axis=0).astype(BF16) for bi, ps in ent]
    kb_p = [jnp.concatenate([k_t[bi][:, ps], b_t[bi][:, ps]], axis=0).astype(BF16) for bi, ps in ent]
    am_e = _dots([jnp.where(even2, x, 0.0) for x in ar_p], bk_p, _NT)
    am_o = _dots([jnp.where(even2, 0.0, x) for x in ar_p], kb_p, _NT)
    a_ab = [jnp.where(strict_w, jnp.where(even, e[:ln], o[:ln]), 0.0) for e, o in zip(am_e, am_o)]
    a_ak = [jnp.where(strict_w, jnp.where(even, o[:ln], e[:ln]), 0.0) for e, o in zip(am_e, am_o)]
    a_rb = [jnp.where(incl_w, jnp.where(even, e[ln:], o[ln:]), 0.0) for e, o in zip(am_e, am_o)]
    a_rk = [jnp.where(incl_w, jnp.where(even, o[ln:], e[ln:]), 0.0) for e, o in zip(am_e, am_o)]
    v_p = [v[bi][:, ps] for bi, ps in ent]
    v_d = [diag(x) for x in v_p]
    akv = _dots(a_ak, [swap(x) for x in v_d])
    t = _unit_lower_inverse(a_ab, row_w, col_w, even)
    rhs = [jnp.concatenate([diag(a_t[bi][:, ps]), diag(x)], axis=1) for (bi, ps), x in zip(ent, akv)]
    pq = _dots(t, rhs)
    ry = _dots([jnp.concatenate([x, y], axis=1) for x, y in zip(a_rb, a_rk)],
               [jnp.concatenate([jnp.concatenate([diag(x[:, :pw]), diag(x[:, pw:])], axis=1),
                                 jnp.concatenate([jnp.zeros((2 * ln, pw), F32), swap(u)], axis=1)], axis=0)
                for x, u in zip(pq, v_d)])
    pqv_p = [jnp.concatenate([x, jnp.concatenate([zeros_w, u], axis=1)], axis=0)
             for x, u in zip(pq, v_p)]
    bkh_p = [jnp.concatenate([b_h[bi][:, ps], k_h[bi][:, ps]], axis=0) for bi, ps in ent]
    mn_p = _dots(bkh_p, pqv_p, _TN)
    sq_r = lax.broadcasted_iota(jnp.int32, (pw, pw), 0)
    sq_c = lax.broadcasted_iota(jnp.int32, (pw, pw), 1)
    same_head = (sq_r < n) == (sq_c < n)
    lhs_p = [jnp.concatenate([r_t[bi][:, ps] + x[:, :pw],
                              jnp.where(sq_r == sq_c, jnp.broadcast_to(w_end[bi][:, ps], (pw, pw)),
                                        jnp.where(same_head, m[:, :pw], 0.0))], axis=0)
             for (bi, ps), x, m in zip(ent, ry, mn_p)]
    st = [state_ref[e] for e in range(len(ent))]
    upd = _dots(lhs_p, st)
    for e in range(len(ent)):
        state_ref[e] = upd[e][ln:, :] + jnp.where(same_head, mn_p[e][:, pw:], 0.0)
    y0_p = [x[:, pw:] for x in ry]

    hsum = lambda x: jnp.where(even, jnp.sum(jnp.where(even, x, 0.0), axis=-1, keepdims=True),
                               jnp.sum(jnp.where(even, 0.0, x), axis=-1, keepdims=True))
    inv_n = 1.0 / n
    yn = []
    for u, y0 in zip(upd, y0_p):
        y = u[:ln, :] + y0
        d = y - hsum(y) * inv_n
        yn.append(d * lax.rsqrt(hsum(d * d) * inv_n + LNX_EPS))
    for bi in range(nb):
        rk = r[bi] * k[bi] * rk_ref[...]
        mine = range(bi * npair, (bi + 1) * npair)
        bonus = jnp.concatenate([hsum(rk[:, ent[e][1]]) * v_p[e] for e in mine], axis=1)
        out = jnp.concatenate([yn[e] for e in mine], axis=1) * lg_ref[...] + lb_ref[...] + bonus
        y_ref[bi, rs, :] = (out * g_ref[bi, rs, :].astype(F32)).astype(y_ref.dtype)


def _wkv(r, lw, k, v, an, b, g, r_k, lnx_g, lnx_b, casts=(), heads_per_step=16, chunks_per_step=2):
    bsz, s, c = r.shape
    wb = heads_per_step * RWKV_HEAD
    ts = chunks_per_step * WKV_CHUNK
    nc = s // ts
    assert s % ts == 0 and c % wb == 0 and heads_per_step % 2 == 0 and WKV_CHUNK == RWKV_HEAD
    assert not casts or c == wb
    spec = pl.BlockSpec((bsz, ts, wb), lambda hi, ci: (0, ci, hi))
    pspec = pl.BlockSpec((1, wb), lambda hi, ci: (0, hi))
    row = lambda a: a.reshape(1, -1)
    cast_ins, cast_in_specs, cast_out_specs, cast_out_shape = [], [], [], []
    for w, layer in casts:
        nl, rows, cols = w.shape
        assert rows % (BF16_SUBLANES * nc) == 0
        slab = rows // nc
        cast_ins.append(w.reshape(nl, nc, slab, cols))
        cast_in_specs.append(pl.BlockSpec((None, None, slab, cols), lambda hi, ci, layer=layer: (layer, ci, 0, 0)))
        cast_out_specs.append(pl.BlockSpec((None, slab, cols), lambda hi, ci: (ci, 0, 0)))
        cast_out_shape.append(jax.ShapeDtypeStruct((nc, slab, cols), BF16))
    out = pl.pallas_call(
        functools.partial(_wkv_kernel, len(casts)),
        grid=(c // wb, nc),
        in_specs=[spec] * 7 + [pspec] * 3 + cast_in_specs,
        out_specs=[spec] + cast_out_specs,
        out_shape=[jax.ShapeDtypeStruct((bsz, s, c), BF16)] + cast_out_shape,
        scratch_shapes=[pltpu.VMEM((bsz * heads_per_step // 2, 2 * RWKV_HEAD, 2 * RWKV_HEAD), F32)],
        compiler_params=_cparams("parallel", "arbitrary"),
        name="wkv",
    )(r, lw, k, v, an, b, g, row(r_k), row(lnx_g), row(lnx_b), *cast_ins)
    return out[0], [o.reshape(w.shape[1], w.shape[2]) for o, (w, _) in zip(out[1:], casts)]


def _moba_kernel(qs_ref, k_ref, vt_ref, bias_ref, o_ref):
    blk = MOBA_BLOCK
    dh = ATTN_HEAD
    hs = range(qs_ref.shape[2])
    qb = pl.program_id(2)
    neg = -jnp.inf
    qs = [qs_ref[0, 0, h] for h in hs]
    ki = lax.broadcasted_iota(jnp.int32, (blk, blk), 0)
    qi = lax.broadcasted_iota(jnp.int32, (blk, blk), 1)
    causal = ki <= qi

    def pipelined(work, stage):
        scores = lambda kb, h: jnp.dot(k_ref[0, kb, h], qs[h], preferred_element_type=F32)
        ahead = [scores(*w) for w in work[:MOBA_LOOKAHEAD]]
        for i, (kb, h) in enumerate(work):
            if i + MOBA_LOOKAHEAD < len(work):
                ahead.append(scores(*work[i + MOBA_LOOKAHEAD]))
            stage(kb, h, ahead[i])

    m_run, l_run, acc = [None] * len(hs), [None] * len(hs), [None] * len(hs)

    def own_block(kb, h, s):
        s = jnp.where(causal, s, neg)
        m_run[h] = jnp.max(s, axis=0, keepdims=True)
        p = jnp.exp2(s - m_run[h])
        l_run[h] = jnp.sum(p, axis=0, keepdims=True)
        acc[h] = jnp.dot(vt_ref[0, kb, h], p.astype(BF16), preferred_element_type=F32)

    pipelined([(qb, h) for h in hs], own_block)

    def past_blocks(kbs, carry):
        m_c, l_c, acc_c = (list(c) for c in carry)

        def stage(kb, h, s):
            b = bias_ref[0, 0, h, pl.ds(kb, 1), :]
            m_new = jnp.where(b == 0.0, jnp.maximum(m_c[h], jnp.max(s, axis=0, keepdims=True)), m_c[h])
            alpha = jnp.exp2(m_c[h] - m_new)
            p = jnp.exp2(s - (m_new - b))
            pv = jnp.dot(vt_ref[0, kb, h], p.astype(BF16), preferred_element_type=F32)
            m_c[h] = m_new
            l_c[h] = alpha * l_c[h] + jnp.sum(p, axis=0, keepdims=True)
            acc_c[h] = alpha * acc_c[h] + pv

        pipelined([(kb, h) for kb in kbs for h in hs], stage)
        return tuple(m_c), tuple(l_c), tuple(acc_c)

    u = MOBA_UNROLL
    carry = lax.fori_loop(0, qb // u, lambda j, c: past_blocks([j * u + i for i in range(u)], c),
                          (tuple(m_run), tuple(l_run), tuple(acc)))
    _, l_fin, acc = lax.fori_loop((qb // u) * u, qb, lambda kb, c: past_blocks([kb], c), carry)
    for h in hs:
        o_ref[0, :, h * dh:(h + 1) * dh] = (acc[h] / l_fin[h]).T.astype(o_ref.dtype)


def _moba(qs, k, vt, bias, heads_per_step=8):
    bsz, nb, nh, dh, blk = qs.shape
    hp = heads_per_step
    assert nh % hp == 0
    return pl.pallas_call(
        _moba_kernel,
        grid=(bsz, nh // hp, nb),
        in_specs=[pl.BlockSpec((1, 1, hp, dh, blk), lambda b, h, i: (b, i, h, 0, 0)),
                  pl.BlockSpec((1, nb, hp, blk, dh), lambda b, h, i: (b, 0, h, 0, 0)),
                  pl.BlockSpec((1, nb, hp, dh, blk), lambda b, h, i: (b, 0, h, 0, 0)),
                  pl.BlockSpec((1, 1, hp, nb, blk), lambda b, h, i: (b, i, h, 0, 0))],
        out_specs=pl.BlockSpec((1, blk, hp * dh), lambda b, h, i: (b, i, h)),
        out_shape=jax.ShapeDtypeStruct((bsz, nb * blk, nh * dh), BF16),
        compiler_params=_cparams("parallel", "parallel", "arbitrary"),
        name="moba",
    )(qs, k, vt, bias)


def _out_proj_kernel(yr_ref, ya_ref, wr_ref, wa_ref, x_ref, g_ref, o_ref):
    for r in range(0, x_ref.shape[0], ROW_CHUNK):
        rows = pl.ds(r, ROW_CHUNK)
        y = jnp.dot(yr_ref[rows, :], wr_ref[...], preferred_element_type=F32)
        y += jnp.dot(ya_ref[rows, :], wa_ref[...], preferred_element_type=F32)
        ms = jnp.mean(y * y, axis=-1, keepdims=True)
        o_ref[rows, :] = x_ref[rows, :] + y * lax.rsqrt(ms + NORM_EPS) * g_ref[...]


def _out_proj(y_r, y_a, w, x2, gain, tm=512):
    m, d = x2.shape
    cw = y_r.shape[1]
    assert y_a.shape[1] == cw and w.shape[0] == 2 * cw and m % tm == 0
    return pl.pallas_call(
        _out_proj_kernel,
        grid=(m // tm,),
        in_specs=[pl.BlockSpec((tm, cw), lambda i: (i, 0)), pl.BlockSpec((tm, cw), lambda i: (i, 0)),
                  pl.BlockSpec((cw, d), lambda i: (0, 0)), pl.BlockSpec((cw, d), lambda i: (1, 0)),
                  pl.BlockSpec((tm, d), lambda i: (i, 0)), pl.BlockSpec((1, d), lambda i: (0, 0))],
        out_specs=pl.BlockSpec((tm, d), lambda i: (i, 0)),
        out_shape=jax.ShapeDtypeStruct((m, d), F32),
        compiler_params=_cparams("parallel"),
        name="out_proj",
    )(y_r, y_a, w, w, x2, gain.reshape(1, d))


def _mlp_kernel(x_ref, gpre_ref, wu_ref, wd_ref, gpost_ref, o_ref, h_ref):
    f = pl.program_id(1)
    last = pl.num_programs(1) - 1
    chunks = [pl.ds(r, ROW_CHUNK) for r in range(0, x_ref.shape[0], ROW_CHUNK)]

    def part(h):
        u = jnp.maximum(jnp.dot(h, wu_ref[...], preferred_element_type=F32), 0.0)
        return jnp.dot((u * u).astype(BF16), wd_ref[...], preferred_element_type=F32)

    @pl.when(f == 0)
    def _():
        for rows in chunks:
            x = x_ref[rows, :]
            ms = jnp.mean(x * x, axis=-1, keepdims=True)
            h = (x * lax.rsqrt(ms + NORM_EPS) * gpre_ref[...]).astype(BF16)
            h_ref[rows, :] = h
            o_ref[rows, :] = part(h)

    @pl.when((f > 0) & (f < last))
    def _():
        for rows in chunks:
            o_ref[rows, :] += part(h_ref[rows, :])

    @pl.when(f == last)
    def _():
        for rows in chunks:
            mlp = o_ref[rows, :] + part(h_ref[rows, :])
            ms = jnp.mean(mlp * mlp, axis=-1, keepdims=True)
            o_ref[rows, :] = x_ref[rows, :] + mlp * lax.rsqrt(ms + NORM_EPS) * gpost_ref[...]


def _mlp(x2, g_pre, w_up, w_down, g_post, tm=1024, tf=1024):
    m, d = x2.shape
    dff = w_up.shape[1]
    assert m % tm == 0 and dff % tf == 0 and dff // tf >= 2 and tm % ROW_CHUNK == 0
    return pl.pallas_call(
        _mlp_kernel,
        grid=(m // tm, dff // tf),
        in_specs=[pl.BlockSpec((tm, d), lambda i, f: (i, 0)),
                  pl.BlockSpec((1, d), lambda i, f: (0, 0)),
                  pl.BlockSpec((d, tf), lambda i, f: (0, f)),
                  pl.BlockSpec((tf, d), lambda i, f: (f, 0)),
                  pl.BlockSpec((1, d), lambda i, f: (0, 0))],
        out_specs=pl.BlockSpec((tm, d), lambda i, f: (i, 0)),
        out_shape=jax.ShapeDtypeStruct((m, d), F32),
        scratch_shapes=[pltpu.VMEM((tm, d), BF16)],
        compiler_params=_cparams("parallel", "arbitrary"),
        name="mlp",
    )(x2, g_pre.reshape(1, d), w_up, w_down, g_post.reshape(1, d))


def _rope_tables(s):
    half = ATTN_HEAD // 2
    inv_freq = ROPE_THETA ** (-jnp.arange(half, dtype=F32) / half)
    ang = jnp.arange(s).astype(F32)[:, None] * inv_freq[None, :]
    cos, sin = jnp.cos(ang), jnp.sin(ang)
    return jnp.concatenate([cos, cos], axis=-1), jnp.concatenate([-sin, sin], axis=-1)


def kernel(x, norm_mix_pre, norm_mix_post, norm_mlp_pre, norm_mlp_post, w_in, w_in_vres, shift_mu, shift_mu_vres, decay_w0, decay_w2, iclr_a0, iclr_a2, vres_v0, vres_v2, gate_g2, k_k, k_a, r_k, lnx_gain, lnx_bias, w_out, w_up, w_down):
    bsz, s, d = x.shape
    depth = w_in.shape[0]
    c = decay_w0.shape[1]
    n_lora = DECAY_LORA + ICLR_LORA + GATE_LORA
    n_shift = 3 * c + n_lora
    ca = (w_in.shape[2] - n_shift) // 3
    cos2, sin2 = _rope_tables(s)
    w_in16 = w_in[0].astype(BF16)
    pad_v = VRES_PAD - VRES_LORA
    x2 = x.reshape(bsz * s, d)
    v_first = None
    for i in range(depth):
        if i == 0:
            vres = None
        else:
            vres = (jnp.pad(w_in_vres[i - 1], ((0, 0), (0, pad_v))).astype(BF16),
                    jnp.pad(shift_mu_vres[i - 1], (0, pad_v)), v_first, vres_v0[i - 1],
                    jnp.pad(vres_v2[i - 1], ((0, pad_v), (0, 0))))
        mix, v_layer, attn = _in_proj(x2.reshape(bsz, s, d), norm_mix_pre[i], w_in16, c, shift_mu[i], decay_w0[i],
                                      decay_w2[i], iclr_a0[i], iclr_a2[i], gate_g2[i], k_k[i], k_a[i], cos2, sin2,
                                      vres)
        if i == 0:
            v_first = v_layer

        casts = [(w_out, i), (w_up, i), (w_down, i)] + ([(w_in, i + 1)] if i + 1 < depth else [])
        y_r, w16 = _wkv(*mix, r_k[i].reshape(-1), lnx_gain[i], lnx_bias[i], casts)
        w_out16, w_up16, w_down16 = w16[:3]

        y_a = _moba(*attn)

        x2 = _out_proj(y_r.reshape(bsz * s, c), y_a.reshape(bsz * s, ca), w_out16, x2, norm_mix_post[i])
        x2 = _mlp(x2, norm_mlp_pre[i], w_up16, w_down16, norm_mlp_post[i])
        if i + 1 < depth:
            w_in16 = w16[3]
    return x2.reshape(bsz, s, d)
```

```python
import functools

import jax
import jax.numpy as jnp
from jax import lax
from jax.experimental import pallas as pl
from jax.experimental.pallas import tpu as pltpu

F32 = jnp.float32
BF16 = jnp.bfloat16

RWKV_HEAD = 64
DECAY_LORA = 64
ICLR_LORA = 64
VRES_LORA = 32
GATE_LORA = 128
ATTN_HEAD = 128
MOBA_BLOCK = 256
MOBA_TOPK = 3
ROPE_THETA = 10000.0
NORM_EPS = 1e-6
LNX_EPS = 64e-5
LOG2E = 1.4426950408889634

LANES = 128
BF16_SUBLANES = 16
VRES_PAD = LANES
WKV_CHUNK = 64
MOBA_LOOKAHEAD = 12
MOBA_UNROLL = 4
ROW_CHUNK = 256
VMEM_LIMIT = 56 * 1024 * 1024


def _cparams(*sem):
    return pltpu.CompilerParams(dimension_semantics=sem, vmem_limit_bytes=VMEM_LIMIT)


_NN = (((1,), (0,)), ((), ()))
_NT = (((1,), (1,)), ((), ()))
_TN = (((0,), (0,)), ((), ()))


def _split2(x):
    hi = x.astype(BF16)
    lo = (x - hi.astype(F32)).astype(BF16)
    return hi, lo


def _split3(x):
    hi = x.astype(BF16)
    r1 = x - hi.astype(F32)
    mid = r1.astype(BF16)
    lo = (r1 - mid.astype(F32)).astype(BF16)
    return hi, mid, lo


def _mm(a, b, dims=_NN, passes=1):
    d = lambda p, q: lax.dot_general(p, q, dims, preferred_element_type=F32)
    if passes == 1:
        return d(a.astype(BF16), b.astype(BF16))
    ah, al = _split2(a)
    bh, bl = _split2(b)
    return d(ah, bh) + (d(ah, bl) + d(al, bh))


def _sigmoid(x):
    return 1.0 / (1.0 + jnp.exp(-x))


def _softplus(x):
    return jnp.maximum(x, 0.0) + jnp.log(1.0 + jnp.exp(-jnp.abs(x)))


def _in_proj_kernel(has_vres, c, spb, *refs):
    it = iter(refs)
    take = lambda cnt: [next(it) for _ in range(cnt)]
    x_ref, g_ref, w_ref, cos_ref, sin_ref, mum_ref, mul_ref, w0_ref, w2_ref, a0_ref, a2_ref, g2_ref, kk_ref, ka_ref = take(14)
    if has_vres:
        we_ref, muv_ref, vf_ref, v0_ref, v2_ref = take(5)
    r_ref, lw_ref, k_ref, v_ref, an_ref, b_ref, gate_ref = take(7)
    if not has_vres:
        vf32_ref, = take(1)
    qs_ref, ko_ref, vt_ref, bias_ref = take(4)
    km_ref, cm_ref, cl_ref = take(3)
    if has_vres:
        cv_ref, = take(1)

    i = pl.program_id(0)
    nb = km_ref.shape[0]
    blk = MOBA_BLOCK
    width = km_ref.shape[1]
    nh = width // ATTN_HEAD
    n_lora = mul_ref.shape[1]
    n_mix = 3 * c + n_lora
    pw = 2 * RWKV_HEAD

    @pl.when(i == 0)
    def _():
        for ref in [km_ref, cm_ref, cl_ref] + ([cv_ref] if has_vres else []):
            ref[...] = jnp.zeros_like(ref)

    blk_id = lax.broadcasted_iota(jnp.int32, (nb, blk), 0)
    km_row = lax.broadcasted_iota(jnp.int32, (nb, ATTN_HEAD), 0)
    first_row = lax.broadcasted_iota(jnp.int32, (blk, 1), 0) == 0
    even = lax.broadcasted_iota(jnp.int32, (blk, pw), 1) < RWKV_HEAD
    qk_scale = (ATTN_HEAD ** -0.5) * LOG2E

    def shifted(cur, carry_ref, lo, hi, mu_ref, start):
        last = jnp.where(start, 0.0, carry_ref[:, lo:hi])
        prev = jnp.where(first_row, last, pltpu.roll(cur, 1, axis=0))
        carry_ref[:, lo:hi] = cur[blk - 1:blk, :]
        return cur + (prev - cur) * mu_ref[:, lo:hi]

    for j in range(x_ref.shape[0] // blk):
        rows = pl.ds(j * blk, blk)
        x = x_ref[rows, :]
        ms = jnp.mean(x * x, axis=-1, keepdims=True)
        h = (x * lax.rsqrt(ms + NORM_EPS) * g_ref[...]).astype(BF16)
        proj = lambda c0, c1: jnp.dot(h, w_ref[:, c0:c1], preferred_element_type=F32)
        bid = (i % spb) * (x_ref.shape[0] // blk) + j
        start = bid == 0

        z_l = proj(3 * c, n_mix)
        if has_vres:
            z_e = jnp.dot(h, we_ref[...], preferred_element_type=F32)
        z_r = proj(0, c)

        zl = shifted(z_l, cl_ref, 0, n_lora, mul_ref, start)
        o = 0
        wd = zl[:, o:o + DECAY_LORA]
        o += DECAY_LORA
        ad = zl[:, o:o + ICLR_LORA]
        o += ICLR_LORA
        gd = zl[:, o:o + GATE_LORA]
        w_log = -_softplus(-(w0_ref[...] + _mm(jnp.tanh(wd), w2_ref[...], passes=3))) - 0.5
        lw_ref[rows, :] = -jnp.exp(w_log)
        a = _sigmoid(a0_ref[...] + _mm(ad, a2_ref[...]))
        gate_ref[rows, :] = _mm(_sigmoid(gd), g2_ref[...]).astype(gate_ref.dtype)
        if has_vres:
            vd = shifted(z_e, cv_ref, 0, z_e.shape[1], muv_ref, start)
            v_mix = _sigmoid(v0_ref[...] + _mm(vd, v2_ref[...]))
        z_k, z_v = proj(c, 2 * c), proj(2 * c, 3 * c)

        r_ref[rows, :] = shifted(z_r, cm_ref, 0, c, mum_ref, start).astype(r_ref.dtype)
        k = shifted(z_k, cm_ref, c, 2 * c, mum_ref, start)
        v = shifted(z_v, cm_ref, 2 * c, 3 * c, mum_ref, start)
        if has_vres:
            v = v + (vf_ref[rows, :] - v) * v_mix
        else:
            vf32_ref[rows, :] = v
        v_ref[rows, :] = v.astype(v_ref.dtype)
        kk = k * kk_ref[...]
        unit = []
        for p in range(c // pw):
            xk = kk[:, p * pw:(p + 1) * pw]
            sq = xk * xk
            ss = jnp.where(even, jnp.sum(jnp.where(even, sq, 0.0), axis=-1, keepdims=True),
                           jnp.sum(jnp.where(even, 0.0, sq), axis=-1, keepdims=True))
            unit.append(xk / jnp.maximum(jnp.sqrt(ss), 1e-12))
        kk = jnp.concatenate(unit, axis=1)
        k_ref[rows, :] = (k * (1.0 + (a - 1.0) * ka_ref[...])).astype(k_ref.dtype)
        an_ref[rows, :] = (-kk).astype(an_ref.dtype)
        b_ref[rows, :] = (kk * a).astype(b_ref.dtype)
        q_all, k_all = (proj(n_mix + t * width, n_mix + (t + 1) * width) for t in range(2))

        cos = cos_ref[rows, :]
        sin = sin_ref[rows, :]
        past = blk_id < bid
        gates = []
        for hd in range(nh):
            sl = slice(hd * ATTN_HEAD, (hd + 1) * ATTN_HEAD)
            q, ka = q_all[:, sl], k_all[:, sl]
            qr = (q * cos + pltpu.roll(q, ATTN_HEAD // 2, axis=1) * sin).T
            kr = ka * cos + pltpu.roll(ka, ATTN_HEAD // 2, axis=1) * sin
            qs_ref[0, j, hd] = (qr * qk_scale).astype(BF16)
            ko_ref[0, j, hd] = kr.astype(BF16)
            gates.append(jnp.where(past, _mm(km_ref[:, sl], qr, passes=3), -jnp.inf))
            km_new = jnp.sum(kr, axis=0, keepdims=True) * (1.0 / blk)
            km_ref[:, sl] = jnp.where(km_row == bid, km_new, km_ref[:, sl])
        v_all = proj(n_mix + 2 * width, n_mix + 3 * width)

        for hd, gate in enumerate(gates):
            rank = jnp.zeros((nb, blk), jnp.int32)
            for m in range(nb):
                gm = gate[m:m + 1, :]
                rank += ((gm > gate) | ((gm == gate) & (m < blk_id))).astype(jnp.int32)
            bias_ref[0, j, hd] = jnp.where(past & (rank < MOBA_TOPK), 0.0, -jnp.inf)
            vt_ref[0, j, hd] = v_all[:, hd * ATTN_HEAD:(hd + 1) * ATTN_HEAD].T.astype(BF16)


def _in_proj(x3, gain, w, c, mu, w0, w2, a0, a2, g2, k_k, k_a, cos2, sin2, vres=None, tm=256):
    bsz, s, d = x3.shape
    n = w.shape[1]
    n_lora = w2.shape[0] + a2.shape[0] + g2.shape[0]
    n_mix = 3 * c + n_lora
    width = (n - n_mix) // 3
    nh = width // ATTN_HEAD
    nb = s // MOBA_BLOCK
    bpt = tm // MOBA_BLOCK
    spb = s // tm
    assert s % tm == 0 and tm % MOBA_BLOCK == 0 and c % (2 * RWKV_HEAD) == 0 and n_lora % LANES == 0
    assert (n - n_mix) % (3 * ATTN_HEAD) == 0
    has_vres = vres is not None
    once = pl.Buffered(1)
    row = lambda a: a.reshape(1, -1)
    full = lambda a: pl.BlockSpec(a.shape, lambda i: (0,) * a.ndim)
    tile = lambda wd_: pl.BlockSpec((tm, wd_), lambda i: (i, 0))
    small = [row(mu[:3 * c]), row(mu[3 * c:]), row(w0), w2, row(a0), a2, g2, row(k_k), row(k_a)]
    ins = [x3.reshape(bsz * s, d), gain.reshape(1, d), w, cos2, sin2] + small
    in_specs = [tile(d), pl.BlockSpec((1, d), lambda i: (0, 0)),
                pl.BlockSpec((d, n), lambda i: (0, 0), pipeline_mode=once),
                pl.BlockSpec((tm, ATTN_HEAD), lambda i: (i % spb, 0)),
                pl.BlockSpec((tm, ATTN_HEAD), lambda i: (i % spb, 0))] + [full(a) for a in small]
    scratch = [pltpu.VMEM((nb, width), F32), pltpu.VMEM((1, 3 * c), F32), pltpu.VMEM((1, n_lora), F32)]
    if has_vres:
        w_v, mu_v, v_first, v0, v2 = vres
        extra = [w_v, row(mu_v), v_first, row(v0), v2]
        ins += extra
        in_specs += [pl.BlockSpec(w_v.shape, lambda i: (0, 0), pipeline_mode=once), full(extra[1]), tile(c),
                     full(extra[3]), full(extra[4])]
        scratch.append(pltpu.VMEM((1, w_v.shape[1]), F32))
    mix_dtypes = [BF16, F32, BF16, BF16, BF16, BF16, BF16] + ([] if has_vres else [F32])
    t_spec = pl.BlockSpec((1, bpt, nh, ATTN_HEAD, MOBA_BLOCK), lambda i: (i // spb, i % spb, 0, 0, 0))
    n_spec = pl.BlockSpec((1, bpt, nh, MOBA_BLOCK, ATTN_HEAD), lambda i: (i // spb, i % spb, 0, 0, 0))
    out_specs = [tile(c)] * len(mix_dtypes) + [
        t_spec, n_spec, t_spec, pl.BlockSpec((1, bpt, nh, nb, MOBA_BLOCK), lambda i: (i // spb, i % spb, 0, 0, 0))]
    out_shape = [jax.ShapeDtypeStruct((bsz * s, c), dt) for dt in mix_dtypes] + [
        jax.ShapeDtypeStruct((bsz, nb, nh, ATTN_HEAD, MOBA_BLOCK), BF16),
        jax.ShapeDtypeStruct((bsz, nb, nh, MOBA_BLOCK, ATTN_HEAD), BF16),
        jax.ShapeDtypeStruct((bsz, nb, nh, ATTN_HEAD, MOBA_BLOCK), BF16),
        jax.ShapeDtypeStruct((bsz, nb, nh, nb, MOBA_BLOCK), F32)]
    out = pl.pallas_call(
        functools.partial(_in_proj_kernel, has_vres, c, spb),
        grid=(bsz * spb,),
        in_specs=in_specs,
        out_specs=out_specs,
        out_shape=out_shape,
        scratch_shapes=scratch,
        compiler_params=_cparams("arbitrary"),
        name="in_proj",
    )(*ins)
    mix = [o.reshape(bsz, s, c) for o in out[:7]]
    return mix, (None if has_vres else out[7]), out[-4:]


def _dots(a_list, b_list, dims=_NN):
    return [lax.dot_general(a.astype(BF16), b.astype(BF16), dims, preferred_element_type=F32)
            for a, b in zip(a_list, b_list)]


def _pair_diag(x, even):
    return jnp.concatenate([jnp.where(even, x, 0.0), jnp.where(even, 0.0, x)], axis=0)


def _unit_lower_inverse(a_list, row_w, col_w, even):
    n = row_w.shape[0]
    lower = row_w > col_w
    base = lower & ((row_w >> 1) == (col_w >> 1))
    t = [jnp.where(row_w == col_w, 1.0, jnp.where(base, a, 0.0)) for a in a_list]
    sh = 1
    while (2 << sh) <= n:
        sub = lower & ((row_w >> (sh + 1)) == (col_w >> (sh + 1))) & ((row_w >> sh) != (col_w >> sh))
        off = [_pair_diag(jnp.where(sub, a, 0.0), even) for a in a_list]
        upd = _dots(_dots(t, off), [_pair_diag(x, even) for x in t])
        t = [x + u for x, u in zip(t, upd)]
        sh += 1
    return t


def _mm_exact_rhs_left(l_bf16, a):
    d = lambda p: lax.dot_general(l_bf16, p, _NN, preferred_element_type=F32)
    hi, mid, lo = _split3(a)
    return d(hi) + (d(mid) + d(lo))


def _wkv_kernel(n_cast, *refs):
    r_ref, lw_ref, k_ref, v_ref, an_ref, b_ref, g_ref, rk_ref, lg_ref, lb_ref = refs[:10]
    cast_in, y_ref = refs[10:10 + n_cast], refs[10 + n_cast]
    cast_out, state_ref = refs[11 + n_cast:11 + 2 * n_cast], refs[11 + 2 * n_cast]
    for src, dst in zip(cast_in, cast_out):
        dst[...] = src[...].astype(dst.dtype)

    @pl.when(pl.program_id(1) == 0)
    def _():
        state_ref[...] = jnp.zeros_like(state_ref)

    for c0 in range(0, r_ref.shape[1], WKV_CHUNK):
        _wkv_chunk(slice(c0, c0 + WKV_CHUNK), *refs[:10], y_ref, state_ref)


def _wkv_chunk(rs, r_ref, lw_ref, k_ref, v_ref, an_ref, b_ref, g_ref, rk_ref, lg_ref, lb_ref, y_ref, state_ref):
    nb, ln = r_ref.shape[0], WKV_CHUNK
    n = RWKV_HEAD
    pw = 2 * n
    npair = r_ref.shape[2] // pw
    ent = [(bi, slice(p * pw, (p + 1) * pw)) for bi in range(nb) for p in range(npair)]

    rows = lax.broadcasted_iota(jnp.int32, (ln, ln), 0)
    cols = lax.broadcasted_iota(jnp.int32, (ln, ln), 1)
    tril = jnp.where(rows >= cols, 1.0, 0.0).astype(BF16)
    r, k, v, w_end, r_t, a_t, b_t, k_t, b_h, k_h = ([] for _ in range(10))
    for bi in range(nb):
        lw = lw_ref[bi, rs, :]
        cw = _mm_exact_rhs_left(tril, lw)
        cw_end = cw[ln - 1:ln, :]
        e_neg = jnp.exp(-cw)
        e_end = jnp.exp(cw_end - cw)
        bb = b_ref[bi, rs, :].astype(F32)
        r.append(r_ref[bi, rs, :].astype(F32))
        k.append(k_ref[bi, rs, :].astype(F32))
        v.append(v_ref[bi, rs, :].astype(F32))
        w_end.append(jnp.exp(cw_end))
        r_t.append(r[bi] * jnp.exp(cw))
        a_t.append(an_ref[bi, rs, :].astype(F32) * jnp.exp(cw - lw))
        b_t.append(bb * e_neg)
        k_t.append(k[bi] * e_neg)
        b_h.append(bb * e_end)
        k_h.append(k[bi] * e_end)

    lane = lax.broadcasted_iota(jnp.int32, (ln, pw), 1)
    row_w = lax.broadcasted_iota(jnp.int32, (ln, pw), 0)
    even = lane < n
    even2 = lax.broadcasted_iota(jnp.int32, (2 * ln, pw), 1) < n
    col_w = lane & (n - 1)
    strict_w = row_w > col_w
    incl_w = row_w >= col_w
    zeros_w = jnp.zeros((ln, pw), F32)
    diag = lambda x: _pair_diag(x, even)
    swap = lambda x: jnp.concatenate([x[x.shape[0] // 2:], x[:x.shape[0] // 2]], axis=0)

    ar_p = [jnp.concatenate([a_t[bi][:, ps], r_t[bi][:, ps]], axis=0) for bi, ps in ent]
    bk_p = [jnp.concatenate([b_t[bi][:, ps], k_t[bi][:, ps]], axis=0).astype(BF16) for bi, ps in ent]
    kb_p = [jnp.concatenate([k_t[bi][:, ps], b_t[bi][:, ps]], axis=0).astype(BF16) for bi, ps in ent]
    am_e = _dots([jnp.where(even2, x, 0.0) for x in ar_p], bk_p, _NT)
    am_o = _dots([jnp.where(even2, 0.0, x) for x in ar_p], kb_p, _NT)
    a_ab = [jnp.where(strict_w, jnp.where(even, e[:ln], o[:ln]), 0.0) for e, o in zip(am_e, am_o)]
    a_ak = [jnp.where(strict_w, jnp.where(even, o[:ln], e[:ln]), 0.0) for e, o in zip(am_e, am_o)]
    a_rb = [jnp.where(incl_w, jnp.where(even, e[ln:], o[ln:]), 0.0) for e, o in zip(am_e, am_o)]
    a_rk = [jnp.where(incl_w, jnp.where(even, o[ln:], e[ln:]), 0.0) for e, o in zip(am_e, am_o)]
    v_p = [v[bi][:, ps] for bi, ps in ent]
    v_d = [diag(x) for x in v_p]
    akv = _dots(a_ak, [swap(x) for x in v_d])
    t = _unit_lower_inverse(a_ab, row_w, col_w, even)
    rhs = [jnp.concatenate([diag(a_t[bi][:, ps]), diag(x)], axis=1) for (bi, ps), x in zip(ent, akv)]
    pq = _dots(t, rhs)
    ry = _dots([jnp.concatenate([x, y], axis=1) for x, y in zip(a_rb, a_rk)],
               [jnp.concatenate([jnp.concatenate([diag(x[:, :pw]), diag(x[:, pw:])], axis=1),
                                 jnp.concatenate([jnp.zeros((2 * ln, pw), F32), swap(u)], axis=1)], axis=0)
                for x, u in zip(pq, v_d)])
    pqv_p = [jnp.concatenate([x, jnp.concatenate([zeros_w, u], axis=1)], axis=0)
             for x, u in zip(pq, v_p)]
    bkh_p = [jnp.concatenate([b_h[bi][:, ps], k_h[bi][:, ps]], axis=0) for bi, ps in ent]
    mn_p = _dots(bkh_p, pqv_p, _TN)
    sq_r = lax.broadcasted_iota(jnp.int32, (pw, pw), 0)
    sq_c = lax.broadcasted_iota(jnp.int32, (pw, pw), 1)
    same_head = (sq_r < n) == (sq_c < n)
    lhs_p = [jnp.concatenate([r_t[bi][:, ps] + x[:, :pw],
                              jnp.where(sq_r == sq_c, jnp.broadcast_to(w_end[bi][:, ps], (pw, pw)),
                                        jnp.where(same_head, m[:, :pw], 0.0))], axis=0)
             for (bi, ps), x, m in zip(ent, ry, mn_p)]
    st = [state_ref[e] for e in range(len(ent))]
    upd = _dots(lhs_p, st)
    for e in range(len(ent)):
        state_ref[e] = upd[e][ln:, :] + jnp.where(same_head, mn_p[e][:, pw:], 0.0)
    y0_p = [x[:, pw:] for x in ry]

    hsum = lambda x: jnp.where(even, jnp.sum(jnp.where(even, x, 0.0), axis=-1, keepdims=True),
                               jnp.sum(jnp.where(even, 0.0, x), axis=-1, keepdims=True))
    inv_n = 1.0 / n
    yn = []
    for u, y0 in zip(upd, y0_p):
        y = u[:ln, :] + y0
        d = y - hsum(y) * inv_n
        yn.append(d * lax.rsqrt(hsum(d * d) * inv_n + LNX_EPS))
    for bi in range(nb):
        rk = r[bi] * k[bi] * rk_ref[...]
        mine = range(bi * npair, (bi + 1) * npair)
        bonus = jnp.concatenate([hsum(rk[:, ent[e][1]]) * v_p[e] for e in mine], axis=1)
        out = jnp.concatenate([yn[e] for e in mine], axis=1) * lg_ref[...] + lb_ref[...] + bonus
        y_ref[bi, rs, :] = (out * g_ref[bi, rs, :].astype(F32)).astype(y_ref.dtype)


def _wkv(r, lw, k, v, an, b, g, r_k, lnx_g, lnx_b, casts=(), heads_per_step=16, chunks_per_step=2):
    bsz, s, c = r.shape
    wb = heads_per_step * RWKV_HEAD
    ts = chunks_per_step * WKV_CHUNK
    nc = s // ts
    assert s % ts == 0 and c % wb == 0 and heads_per_step % 2 == 0 and WKV_CHUNK == RWKV_HEAD
    assert not casts or c == wb
    spec = pl.BlockSpec((bsz, ts, wb), lambda hi, ci: (0, ci, hi))
    pspec = pl.BlockSpec((1, wb), lambda hi, ci: (0, hi))
    row = lambda a: a.reshape(1, -1)
    cast_ins, cast_in_specs, cast_out_specs, cast_out_shape = [], [], [], []
    for w, layer in casts:
        nl, rows, cols = w.shape
        assert rows % (BF16_SUBLANES * nc) == 0
        slab = rows // nc
        cast_ins.append(w.reshape(nl, nc, slab, cols))
        cast_in_specs.append(pl.BlockSpec((None, None, slab, cols), lambda hi, ci, layer=layer: (layer, ci, 0, 0)))
        cast_out_specs.append(pl.BlockSpec((None, slab, cols), lambda hi, ci: (ci, 0, 0)))
        cast_out_shape.append(jax.ShapeDtypeStruct((nc, slab, cols), BF16))
    out = pl.pallas_call(
        functools.partial(_wkv_kernel, len(casts)),
        grid=(c // wb, nc),
        in_specs=[spec] * 7 + [pspec] * 3 + cast_in_specs,
        out_specs=[spec] + cast_out_specs,
        out_shape=[jax.ShapeDtypeStruct((bsz, s, c), BF16)] + cast_out_shape,
        scratch_shapes=[pltpu.VMEM((bsz * heads_per_step // 2, 2 * RWKV_HEAD, 2 * RWKV_HEAD), F32)],
        compiler_params=_cparams("parallel", "arbitrary"),
        name="wkv",
    )(r, lw, k, v, an, b, g, row(r_k), row(lnx_g), row(lnx_b), *cast_ins)
    return out[0], [o.reshape(w.shape[1], w.shape[2]) for o, (w, _) in zip(out[1:], casts)]


def _moba_kernel(qs_ref, k_ref, vt_ref, bias_ref, o_ref):
    blk = MOBA_BLOCK
    dh = ATTN_HEAD
    hs = range(qs_ref.shape[2])
    qb = pl.program_id(2)
    neg = -jnp.inf
    qs = [qs_ref[0, 0, h] for h in hs]
    ki = lax.broadcasted_iota(jnp.int32, (blk, blk), 0)
    qi = lax.broadcasted_iota(jnp.int32, (blk, blk), 1)
    causal = ki <= qi

    def pipelined(work, stage):
        scores = lambda kb, h: jnp.dot(k_ref[0, kb, h], qs[h], preferred_element_type=F32)
        ahead = [scores(*w) for w in work[:MOBA_LOOKAHEAD]]
        for i, (kb, h) in enumerate(work):
            if i + MOBA_LOOKAHEAD < len(work):
                ahead.append(scores(*work[i + MOBA_LOOKAHEAD]))
            stage(kb, h, ahead[i])

    m_run, l_run, acc = [None] * len(hs), [None] * len(hs), [None] * len(hs)

    def own_block(kb, h, s):
        s = jnp.where(causal, s, neg)
        m_run[h] = jnp.max(s, axis=0, keepdims=True)
        p = jnp.exp2(s - m_run[h])
        l_run[h] = jnp.sum(p, axis=0, keepdims=True)
        acc[h] = jnp.dot(vt_ref[0, kb, h], p.astype(BF16), preferred_element_type=F32)

    pipelined([(qb, h) for h in hs], own_block)

    def past_blocks(kbs, carry):
        m_c, l_c, acc_c = (list(c) for c in carry)

        def stage(kb, h, s):
            b = bias_ref[0, 0, h, pl.ds(kb, 1), :]
            m_new = jnp.where(b == 0.0, jnp.maximum(m_c[h], jnp.max(s, axis=0, keepdims=True)), m_c[h])
            alpha = jnp.exp2(m_c[h] - m_new)
            p = jnp.exp2(s - (m_new - b))
            pv = jnp.dot(vt_ref[0, kb, h], p.astype(BF16), preferred_element_type=F32)
            m_c[h] = m_new
            l_c[h] = alpha * l_c[h] + jnp.sum(p, axis=0, keepdims=True)
            acc_c[h] = alpha * acc_c[h] + pv

        pipelined([(kb, h) for kb in kbs for h in hs], stage)
        return tuple(m_c), tuple(l_c), tuple(acc_c)

    u = MOBA_UNROLL
    carry = lax.fori_loop(0, qb // u, lambda j, c: past_blocks([j * u + i for i in range(u)], c),
                          (tuple(m_run), tuple(l_run), tuple(acc)))
    _, l_fin, acc = lax.fori_loop((qb // u) * u, qb, lambda kb, c: past_blocks([kb], c), carry)
    for h in hs:
        o_ref[0, :, h * dh:(h + 1) * dh] = (acc[h] / l_fin[h]).T.astype(o_ref.dtype)


def _moba(qs, k, vt, bias, heads_per_step=8):
    bsz, nb, nh, dh, blk = qs.shape
    hp = heads_per_step
    assert nh % hp == 0
    return pl.pallas_call(
        _moba_kernel,
        grid=(bsz, nh // hp, nb),
        in_specs=[pl.BlockSpec((1, 1, hp, dh, blk), lambda b, h, i: (b, i, h, 0, 0)),
                  pl.BlockSpec((1, nb, hp, blk, dh), lambda b, h, i: (b, 0, h, 0, 0)),
                  pl.BlockSpec((1, nb, hp, dh, blk), lambda b, h, i: (b, 0, h, 0, 0)),
                  pl.BlockSpec((1, 1, hp, nb, blk), lambda b, h, i: (b, i, h, 0, 0))],
        out_specs=pl.BlockSpec((1, blk, hp * dh), lambda b, h, i: (b, i, h)),
        out_shape=jax.ShapeDtypeStruct((bsz, nb * blk, nh * dh), BF16),
        compiler_params=_cparams("parallel", "parallel", "arbitrary"),
        name="moba",
    )(qs, k, vt, bias)


def _out_proj_kernel(yr_ref, ya_ref, wr_ref, wa_ref, x_ref, g_ref, o_ref):
    for r in range(0, x_ref.shape[0], ROW_CHUNK):
        rows = pl.ds(r, ROW_CHUNK)
        y = jnp.dot(yr_ref[rows, :], wr_ref[...], preferred_element_type=F32)
        y += jnp.dot(ya_ref[rows, :], wa_ref[...], preferred_element_type=F32)
        ms = jnp.mean(y * y, axis=-1, keepdims=True)
        o_ref[rows, :] = x_ref[rows, :] + y * lax.rsqrt(ms + NORM_EPS) * g_ref[...]


def _out_proj(y_r, y_a, w, x2, gain, tm=512):
    m, d = x2.shape
    cw = y_r.shape[1]
    assert y_a.shape[1] == cw and w.shape[0] == 2 * cw and m % tm == 0
    return pl.pallas_call(
        _out_proj_kernel,
        grid=(m // tm,),
        in_specs=[pl.BlockSpec((tm, cw), lambda i: (i, 0)), pl.BlockSpec((tm, cw), lambda i: (i, 0)),
                  pl.BlockSpec((cw, d), lambda i: (0, 0)), pl.BlockSpec((cw, d), lambda i: (1, 0)),
                  pl.BlockSpec((tm, d), lambda i: (i, 0)), pl.BlockSpec((1, d), lambda i: (0, 0))],
        out_specs=pl.BlockSpec((tm, d), lambda i: (i, 0)),
        out_shape=jax.ShapeDtypeStruct((m, d), F32),
        compiler_params=_cparams("parallel"),
        name="out_proj",
    )(y_r, y_a, w, w, x2, gain.reshape(1, d))


def _mlp_kernel(x_ref, gpre_ref, wu_ref, wd_ref, gpost_ref, o_ref, h_ref):
    f = pl.program_id(1)
    last = pl.num_programs(1) - 1
    chunks = [pl.ds(r, ROW_CHUNK) for r in range(0, x_ref.shape[0], ROW_CHUNK)]

    def part(h):
        u = jnp.maximum(jnp.dot(h, wu_ref[...], preferred_element_type=F32), 0.0)
        return jnp.dot((u * u).astype(BF16), wd_ref[...], preferred_element_type=F32)

    @pl.when(f == 0)
    def _():
        for rows in chunks:
            x = x_ref[rows, :]
            ms = jnp.mean(x * x, axis=-1, keepdims=True)
            h = (x * lax.rsqrt(ms + NORM_EPS) * gpre_ref[...]).astype(BF16)
            h_ref[rows, :] = h
            o_ref[rows, :] = part(h)

    @pl.when((f > 0) & (f < last))
    def _():
        for rows in chunks:
            o_ref[rows, :] += part(h_ref[rows, :])

    @pl.when(f == last)
    def _():
        for rows in chunks:
            mlp = o_ref[rows, :] + part(h_ref[rows, :])
            ms = jnp.mean(mlp * mlp, axis=-1, keepdims=True)
            o_ref[rows, :] = x_ref[rows, :] + mlp * lax.rsqrt(ms + NORM_EPS) * gpost_ref[...]


def _mlp(x2, g_pre, w_up, w_down, g_post, tm=1024, tf=1024):
    m, d = x2.shape
    dff = w_up.shape[1]
    assert m % tm == 0 and dff % tf == 0 and dff // tf >= 2 and tm % ROW_CHUNK == 0
    return pl.pallas_call(
        _mlp_kernel,
        grid=(m // tm, dff // tf),
        in_specs=[pl.BlockSpec((tm, d), lambda i, f: (i, 0)),
                  pl.BlockSpec((1, d), lambda i, f: (0, 0)),
                  pl.BlockSpec((d, tf), lambda i, f: (0, f)),
                  pl.BlockSpec((tf, d), lambda i, f: (f, 0)),
                  pl.BlockSpec((1, d), lambda i, f: (0, 0))],
        out_specs=pl.BlockSpec((tm, d), lambda i, f: (i, 0)),
        out_shape=jax.ShapeDtypeStruct((m, d), F32),
        scratch_shapes=[pltpu.VMEM((tm, d), BF16)],
        compiler_params=_cparams("parallel", "arbitrary"),
        name="mlp",
    )(x2, g_pre.reshape(1, d), w_up, w_down, g_post.reshape(1, d))


def _rope_tables(s):
    half = ATTN_HEAD // 2
    inv_freq = ROPE_THETA ** (-jnp.arange(half, dtype=F32) / half)
    ang = jnp.arange(s).astype(F32)[:, None] * inv_freq[None, :]
    cos, sin = jnp.cos(ang), jnp.sin(ang)
    return jnp.concatenate([cos, cos], axis=-1), jnp.concatenate([-sin, sin], axis=-1)


def kernel(x, norm_mix_pre, norm_mix_post, norm_mlp_pre, norm_mlp_post, w_in, w_in_vres, shift_mu, shift_mu_vres, decay_w0, decay_w2, iclr_a0, iclr_a2, vres_v0, vres_v2, gate_g2, k_k, k_a, r_k, lnx_gain, lnx_bias, w_out, w_up, w_down):
    bsz, s, d = x.shape
    depth = w_in.shape[0]
    c = decay_w0.shape[1]
    n_lora = DECAY_LORA + ICLR_LORA + GATE_LORA
    n_shift = 3 * c + n_lora
    ca = (w_in.shape[2] - n_shift) // 3
    cos2, sin2 = _rope_tables(s)
    w_in16 = w_in[0].astype(BF16)
    pad_v = VRES_PAD - VRES_LORA
    x2 = x.reshape(bsz * s, d)
    v_first = None
    for i in range(depth):
        if i == 0:
            vres = None
        else:
            vres = (jnp.pad(w_in_vres[i - 1], ((0, 0), (0, pad_v))).astype(BF16),
                    jnp.pad(shift_mu_vres[i - 1], (0, pad_v)), v_first, vres_v0[i - 1],
                    jnp.pad(vres_v2[i - 1], ((0, pad_v), (0, 0))))
        mix, v_layer, attn = _in_proj(x2.reshape(bsz, s, d), norm_mix_pre[i], w_in16, c, shift_mu[i], decay_w0[i],
                                      decay_w2[i], iclr_a0[i], iclr_a2[i], gate_g2[i], k_k[i], k_a[i], cos2, sin2,
                                      vres)
        if i == 0:
            v_first = v_layer

        casts = [(w_out, i), (w_up, i), (w_down, i)] + ([(w_in, i + 1)] if i + 1 < depth else [])
        y_r, w16 = _wkv(*mix, r_k[i].reshape(-1), lnx_gain[i], lnx_bias[i], casts)
        w_out16, w_up16, w_down16 = w16[:3]

        y_a = _moba(*attn)

        x2 = _out_proj(y_r.reshape(bsz * s, c), y_a.reshape(bsz * s, ca), w_out16, x2, norm_mix_post[i])
        x2 = _mlp(x2, norm_mlp_pre[i], w_up16, w_down16, norm_mlp_post[i])
        if i + 1 < depth:
            w_in16 = w16[3]
    return x2.reshape(bsz, s, d)
```
